```python
import jax
import jax.numpy as jnp
from jax import lax
import numpy as np

D_MODEL = 1024
BATCH = 2
SEQ = 8192
DEPTH = 1
DEC_BATCH = 128
DEC_SEQ = 1
PAST_LEN = 2048
PAGE_SIZE = 128

HEAD_DIM = 64
MIX_WIDTH = D_MODEL
N_HEADS = MIX_WIDTH // HEAD_DIM
H_MOBA = N_HEADS // 2
H_NSA = N_HEADS - H_MOBA
G_NSA = 2
R_NSA = H_NSA // G_NSA
MOBA_BLOCK = 256
MOBA_TOPK = 3
CMP_LEN = 32
CMP_STRIDE = 16
SEL_BLOCK = 64
SEL_TOPN = 16
WINDOW = 512
N_EXPERTS = 32
TOP_K = 4
D_EXPERT = D_MODEL
SWIGLU_LIMIT = 7.0
SWIGLU_ALPHA = 1.702
ROPE_THETA = 10000.0
NORM_EPS = 1e-6
Q_BLOCK = 128
MOE_MIN_ROWS = 8
MOE_MAX_ROWS = 128
NEG_BIG = -1e30
TINY = 1e-30
SCALE = HEAD_DIM ** -0.5
IN_COLS = 3 * H_MOBA * HEAD_DIM + H_NSA * HEAD_DIM + 6 * G_NSA * HEAD_DIM + 3 * H_NSA
SPLITS = (H_MOBA * HEAD_DIM,
          3 * H_MOBA * HEAD_DIM,
          3 * H_MOBA * HEAD_DIM + H_NSA * HEAD_DIM,
          3 * H_MOBA * HEAD_DIM + H_NSA * HEAD_DIM + 6 * G_NSA * HEAD_DIM)

kernel_name = 'hymba_moba_nsa_moe_adaln_step'


def _rms_norm(x, g):
    xf = x.astype(jnp.float32)
    y = xf * lax.rsqrt(jnp.mean(xf * xf, axis=-1, keepdims=True) + NORM_EPS)
    return (y * g.astype(jnp.float32)).astype(x.dtype)


def _modulate(x, g, shift, scale):
    return _rms_norm(x, g) * (1 + scale) + shift


def _rope(x, pos):
    half = HEAD_DIM // 2
    inv = ROPE_THETA ** (-jnp.arange(half, dtype=jnp.float32) / half)
    ang = pos.astype(jnp.float32)[:, None] * inv[None, :]
    bshape = (pos.shape[0],) + (1,) * (x.ndim - 3) + (half,)
    cos = jnp.cos(ang).reshape(bshape)
    sin = jnp.sin(ang).reshape(bshape)
    xf = x.astype(jnp.float32)
    x1, x2 = xf[..., :half], xf[..., half:]
    return jnp.concatenate([x1 * cos - x2 * sin, x2 * cos + x1 * sin], axis=-1).astype(x.dtype)


def _masked_softmax(s, mask, axes):
    s = jnp.where(mask, s.astype(jnp.float32), NEG_BIG)
    m = jnp.max(s, axis=axes, keepdims=True)
    e = jnp.where(mask, jnp.exp(s - m), 0.0)
    return e / jnp.maximum(jnp.sum(e, axis=axes, keepdims=True), TINY)


def _rows(a, b, start, size):
    starts = (b, start) + (0,) * (a.ndim - 2)
    return lax.dynamic_slice(a, starts, (1, size) + a.shape[2:])[0]


def _at(a, b):
    return lax.dynamic_index_in_dim(a, b, axis=0, keepdims=False)


def _ada(c, w_ada, b_ada):
    m = jnp.einsum('bd,de->be', jax.nn.silu(c), w_ada) + b_ada
    return jnp.split(m[:, None, :], 6, axis=-1)


def _project(h, pos, w_in, qk_gain):
    n, t, _ = h.shape
    z = jnp.einsum('ntd,dc->ntc', h, w_in)
    zq_m, zkv_m, zq_n, zkv_n, zg = jnp.split(z, list(SPLITS), axis=-1)
    q_m = _rope(_rms_norm(zq_m.reshape(n, t, H_MOBA, HEAD_DIM), qk_gain[0]), pos)
    kv_m = zkv_m.reshape(n, t, 2, H_MOBA, HEAD_DIM)
    k_m = _rope(_rms_norm(kv_m[:, :, 0], qk_gain[1]), pos)
    moba_rows = jnp.stack([k_m, kv_m[:, :, 1]], axis=2)
    qn = _rms_norm(zq_n.reshape(n, t, G_NSA, R_NSA, HEAD_DIM), qk_gain[2])
    qr = _rope(qn, pos)
    kv_n = zkv_n.reshape(n, t, 6, G_NSA, HEAD_DIM)
    k_sel = _rope(_rms_norm(kv_n[:, :, 2], qk_gain[4]), pos)
    k_win = _rope(_rms_norm(kv_n[:, :, 4], qk_gain[5]), pos)
    nsa_rows = jnp.stack([kv_n[:, :, 0], kv_n[:, :, 1], k_sel, kv_n[:, :, 3]], axis=2)
    win_rows = jnp.stack([k_win, kv_n[:, :, 5]], axis=2)
    gates = jax.nn.sigmoid(zg.reshape(n, t, G_NSA, R_NSA, 3).astype(jnp.float32)).astype(h.dtype)
    return q_m, moba_rows, qn, qr, gates, nsa_rows, win_rows


def _moba_blocks(rows):
    t = rows.shape[0]
    nb = -(-t // MOBA_BLOCK)
    blocks = jnp.pad(rows, ((0, nb * MOBA_BLOCK - t), (0, 0), (0, 0), (0, 0)))
    kvb = blocks.reshape(nb, MOBA_BLOCK, 2, H_MOBA, HEAD_DIM).transpose(3, 0, 1, 2, 4)
    kmean = jnp.mean(kvb[:, :, :, 0].astype(jnp.float32), axis=2).astype(rows.dtype)
    return kvb, kmean


def _moba_attend(q, pos, kvb, kmean):
    n_q = q.shape[0]
    nb = kvb.shape[1]
    own = pos // MOBA_BLOCK
    blk = jnp.arange(nb, dtype=jnp.int32)[None, None, :]
    score = jnp.einsum('qhd,hnd->qhn', q, kmean).astype(jnp.float32)
    score = jnp.where(blk < own[:, None, None], score, -jnp.inf)
    _, top = lax.top_k(score, min(MOBA_TOPK, nb))
    own_b = jnp.broadcast_to(own[:, None, None], (n_q, H_MOBA, 1))
    idx = jnp.concatenate([top, own_b], axis=-1)
    ok = jnp.concatenate([top < own[:, None, None], jnp.ones((n_q, H_MOBA, 1), bool)], axis=-1)
    g = kvb[jnp.arange(H_MOBA)[None, :, None], idx]
    kpos = idx[..., None] * MOBA_BLOCK + jnp.arange(MOBA_BLOCK, dtype=jnp.int32)
    mask = ok[..., None] & (kpos <= pos[:, None, None, None])
    s = jnp.einsum('qhd,qhnkd->qhnk', q, g[..., 0, :]).astype(jnp.float32) * SCALE
    p = _masked_softmax(s, mask, (-2, -1))
    o = jnp.einsum('qhnk,qhnkd->qhd', p.astype(g.dtype), g[..., 1, :])
    return o.reshape(n_q, H_MOBA * HEAD_DIM)


def _compress(raw, pe, w1, w2):
    r = CMP_LEN // CMP_STRIDE
    n_chunk = raw.shape[0] // CMP_STRIDE
    ch = raw[: n_chunk * CMP_STRIDE].reshape(n_chunk, CMP_STRIDE, G_NSA, HEAD_DIM)
    n_cmp = n_chunk - r + 1
    blocks = jnp.concatenate([ch[i:i + n_cmp] for i in range(r)], axis=1) + pe[None, :, None, :]
    flat = blocks.transpose(0, 2, 1, 3).reshape(n_cmp, G_NSA, CMP_LEN * HEAD_DIM)
    hid = jax.nn.gelu(jnp.einsum('cgf,fe->cge', flat, w1))
    return jnp.einsum('cge,ed->cgd', hid, w2)


def _nsa_blocks(rows, cmp_p, gain_k_cmp):
    cmp_pos, cmp_w1, cmp_w2 = cmp_p
    kc = _rms_norm(_compress(rows[:, 0], cmp_pos[0], cmp_w1[0], cmp_w2[0]), gain_k_cmp)
    vc = _compress(rows[:, 1], cmp_pos[1], cmp_w1[1], cmp_w2[1])
    t = rows.shape[0]
    nsb = -(-t // SEL_BLOCK)
    sel = jnp.pad(rows[:, 2:], ((0, nsb * SEL_BLOCK - t), (0, 0), (0, 0), (0, 0)))
    kvs = sel.reshape(nsb, SEL_BLOCK, 2, G_NSA, HEAD_DIM).transpose(3, 0, 1, 2, 4)
    return kc, vc, kvs


def _cmp_sel_cover(n_cmp, n_sel):
    c0 = np.arange(n_cmp)[:, None] * CMP_STRIDE
    b0 = np.arange(n_sel)[None, :] * SEL_BLOCK
    return jnp.asarray(((c0 < b0 + SEL_BLOCK) & (c0 + CMP_LEN > b0)).astype(np.float32))


def _nsa_attend(qn, qr, gates, pos, kc, vc, kvs, wkv, wpos):
    n_q = qn.shape[0]
    n_cmp = kc.shape[0]
    nsb = kvs.shape[1]
    cend = jnp.arange(n_cmp, dtype=jnp.int32) * CMP_STRIDE + (CMP_LEN - 1)
    cmask = (cend[None, :] <= pos[:, None])[:, None, None, :]
    s = jnp.einsum('qgrd,cgd->qgrc', qn, kc).astype(jnp.float32) * SCALE
    p_cmp = _masked_softmax(s, cmask, (-1,))
    o_cmp = jnp.einsum('qgrc,cgd->qgrd', p_cmp.astype(vc.dtype), vc)
    imp = jnp.einsum('qgrc,cj->qgj', p_cmp, _cmp_sel_cover(n_cmp, nsb))
    blk = jnp.arange(nsb, dtype=jnp.int32)[None, None, :]
    own = (pos // SEL_BLOCK)[:, None, None]
    forced = (blk == 0) | (blk == own) | (blk == own - 1)
    score = jnp.where(blk <= own, jnp.where(forced, jnp.inf, imp), -jnp.inf)
    _, top = lax.top_k(score, min(SEL_TOPN, nsb))
    g = kvs[jnp.arange(G_NSA)[None, :, None], top]
    kpos = top[..., None] * SEL_BLOCK + jnp.arange(SEL_BLOCK, dtype=jnp.int32)
    smask = ((top <= own)[..., None] & (kpos <= pos[:, None, None, None]))[:, :, None]
    s = jnp.einsum('qgrd,qgnkd->qgrnk', qr, g[..., 0, :]).astype(jnp.float32) * SCALE
    p_sel = _masked_softmax(s, smask, (-2, -1))
    o_sel = jnp.einsum('qgrnk,qgnkd->qgrd', p_sel.astype(g.dtype), g[..., 1, :])
    wmask = (wpos[None, :] <= pos[:, None]) & (wpos[None, :] > pos[:, None] - WINDOW) & (wpos[None, :] >= 0)
    s = jnp.einsum('qgrd,kgd->qgrk', qr, wkv[:, 0]).astype(jnp.float32) * SCALE
    p_win = _masked_softmax(s, wmask[:, None, None, :], (-1,))
    o_win = jnp.einsum('qgrk,kgd->qgrd', p_win.astype(wkv.dtype), wkv[:, 1])
    o = gates[..., 0:1] * o_cmp + gates[..., 1:2] * o_sel + gates[..., 2:3] * o_win
    return o.reshape(n_q, H_NSA * HEAD_DIM)


def _prompt_mixer(h, w_in, qk_gain, cmp_p, w_out):
    bsz, seq, _ = h.shape
    pos = jnp.arange(seq, dtype=jnp.int32)
    q_m, moba_rows, qn, qr, gates, nsa_rows, win_rows = _project(h, pos, w_in, qk_gain)
    kvb, kmean = jax.vmap(_moba_blocks)(moba_rows)
    kc, vc, kvs = jax.vmap(lambda r: _nsa_blocks(r, cmp_p, qk_gain[3]))(nsa_rows)
    win_pad = jnp.pad(win_rows, ((0, 0), (WINDOW, 0), (0, 0), (0, 0), (0, 0)))
    n_qb = seq // Q_BLOCK

    def block(i):
        b = i // n_qb
        start = (i % n_qb) * Q_BLOCK
        qpos = start + jnp.arange(Q_BLOCK, dtype=jnp.int32)
        o_m = _moba_attend(_rows(q_m, b, start, Q_BLOCK), qpos, _at(kvb, b), _at(kmean, b))
        wkv = _rows(win_pad, b, start, WINDOW + Q_BLOCK)
        wpos = start - WINDOW + jnp.arange(WINDOW + Q_BLOCK, dtype=jnp.int32)
        o_n = _nsa_attend(_rows(qn, b, start, Q_BLOCK), _rows(qr, b, start, Q_BLOCK),
                          _rows(gates, b, start, Q_BLOCK), qpos, _at(kc, b), _at(vc, b),
                          _at(kvs, b), wkv, wpos)
        return jnp.concatenate([o_m, o_n], axis=-1)

    o = lax.map(block, jnp.arange(bsz * n_qb, dtype=jnp.int32)).reshape(bsz, seq, MIX_WIDTH)
    y = jnp.einsum('bsm,md->bsd', o, w_out)
    keep = min(WINDOW, seq)
    return y, moba_rows, nsa_rows, win_rows[:, seq - keep:]


def _sample_mixer(h, layer, cache_moba_kv, cache_nsa_kv, win_buf, page_table, w_in, qk_gain, cmp_p, w_out):
    dec_b, dec_s, _ = h.shape
    past = page_table.shape[1] * cache_moba_kv.shape[2]
    pos = past + jnp.arange(dec_s, dtype=jnp.int32)
    q_m, moba_new, qn, qr, gates, nsa_new, win_new = _project(h, pos, w_in, qk_gain)
    win_all = jnp.concatenate([win_buf, win_new], axis=1)
    wpos = past - win_buf.shape[1] + jnp.arange(win_all.shape[1], dtype=jnp.int32)

    def one(args):
        qm_i, mnew_i, qn_i, qr_i, g_i, nnew_i, w_i, pt_i = args
        moba_past = cache_moba_kv[layer, pt_i].reshape((past,) + cache_moba_kv.shape[3:])
        kvb, kmean = _moba_blocks(jnp.concatenate([moba_past, mnew_i], axis=0))
        o_m = _moba_attend(qm_i, pos, kvb, kmean)
        nsa_past = cache_nsa_kv[layer, pt_i].reshape((past,) + cache_nsa_kv.shape[3:])
        kc, vc, kvs = _nsa_blocks(jnp.concatenate([nsa_past, nnew_i], axis=0), cmp_p, qk_gain[3])
        o_n = _nsa_attend(qn_i, qr_i, g_i, pos, kc, vc, kvs, w_i, wpos)
        return jnp.concatenate([o_m, o_n], axis=-1)

    o = lax.map(one, (q_m, moba_new, qn, qr, gates, nsa_new, win_all, page_table))
    y = jnp.einsum('bsm,md->bsd', o, w_out)
    keep = min(WINDOW, past + dec_s)
    return y, moba_new, nsa_new, win_all[:, win_all.shape[1] - keep:]


def _moe_rows_per_block(n_assign):
    rows = MOE_MIN_ROWS
    while rows * 2 <= min(MOE_MAX_ROWS, n_assign // N_EXPERTS):
        rows *= 2
    return rows


def _moe(h, layer, w_router, b_router, w_gu, b_gu, w_down, b_down):
    shape = h.shape
    x = h.reshape(-1, shape[-1])
    n_tok = x.shape[0]
    logits = jnp.einsum('nd,de->ne', x, w_router[layer]).astype(jnp.float32) + b_router[layer].astype(jnp.float32)
    top_v, top_e = lax.top_k(logits, TOP_K)
    weight = jax.nn.softmax(top_v, axis=-1)
    n_assign = n_tok * TOP_K
    rows = _moe_rows_per_block(n_assign)
    e_flat = top_e.reshape(-1)
    order = jnp.argsort(e_flat)
    e_sorted = e_flat[order]
    tok_sorted = (order // TOP_K).astype(jnp.int32)
    w_sorted = weight.reshape(-1)[order]
    counts = jnp.bincount(e_flat, length=N_EXPERTS)
    padded = (counts + rows - 1) // rows * rows
    pad_end = jnp.cumsum(padded)
    pad_start = pad_end - padded
    sort_start = jnp.cumsum(counts) - counts
    dest = pad_start[e_sorted] + jnp.arange(n_assign, dtype=jnp.int32) - sort_start[e_sorted]
    n_blocks = -(-n_assign // rows) + N_EXPERTS
    tok = jnp.full((n_blocks * rows,), n_tok, jnp.int32).at[dest].set(tok_sorted)
    wgt = jnp.zeros((n_blocks * rows,), jnp.float32).at[dest].set(w_sorted)
    blk_e = jnp.minimum(jnp.searchsorted(pad_end, jnp.arange(n_blocks, dtype=jnp.int32) * rows, side='right'),
                        N_EXPERTS - 1)
    xin = jnp.concatenate([x, jnp.zeros((1, x.shape[1]), x.dtype)], axis=0)[tok]
    xin = xin.reshape(n_blocks, rows, x.shape[1])

    def expert_block(args):
        xb, e = args
        gu = jnp.einsum('rd,df->rf', xb, w_gu[layer, e]) + b_gu[layer, e]
        gt, up = jnp.split(gu, 2, axis=-1)
        gt = jnp.minimum(gt, SWIGLU_LIMIT)
        up = jnp.clip(up, -SWIGLU_LIMIT, SWIGLU_LIMIT)
        act = (up + 1) * (gt * jax.nn.sigmoid(SWIGLU_ALPHA * gt))
        return jnp.einsum('rf,fd->rd', act, w_down[layer, e]) + b_down[layer, e]

    y = lax.map(expert_block, (xin, blk_e)).reshape(n_blocks * rows, x.shape[1])
    out = jax.ops.segment_sum(y.astype(jnp.float32) * wgt[:, None], tok, num_segments=n_tok + 1)[:n_tok]
    return out.astype(h.dtype).reshape(shape)


def setup_inputs(seed: int = 0) -> dict:
    key = jax.random.key(seed)
    ks = jax.random.split(key, 24)
    f32 = jnp.float32

    def nrm(k, shape, scale):
        return scale * jax.random.normal(k, shape, f32)

    n_pages = PAST_LEN // PAGE_SIZE
    n_used = DEC_BATCH * n_pages
    n_pool = n_used + max(1, n_used // 4)
    wbuf = min(WINDOW, PAST_LEN)
    page_table = jax.random.permutation(ks[7], n_pool)[:n_used].reshape(DEC_BATCH, n_pages).astype(jnp.int32)
    return {
        'x_prompt': nrm(ks[0], (BATCH, SEQ, D_MODEL), 1.0),
        'x_sample': nrm(ks[1], (DEC_BATCH, DEC_SEQ, D_MODEL), 1.0),
        'c_prompt': nrm(ks[2], (BATCH, D_MODEL), 1.0),
        'c_sample': nrm(ks[3], (DEC_BATCH, D_MODEL), 1.0),
        'cache_moba_kv': nrm(ks[4], (DEPTH, n_pool, PAGE_SIZE, 2, H_MOBA, HEAD_DIM), 1.0),
        'cache_nsa_kv': nrm(ks[5], (DEPTH, n_pool, PAGE_SIZE, 4, G_NSA, HEAD_DIM), 1.0),
        'state_nsa_win_kv': nrm(ks[6], (DEPTH, DEC_BATCH, wbuf, 2, G_NSA, HEAD_DIM), 1.0),
        'page_table': page_table,
        'norm_g': 1.0 + nrm(ks[8], (DEPTH, 2, D_MODEL), 0.1),
        'w_ada': nrm(ks[9], (DEPTH, D_MODEL, 6 * D_MODEL), D_MODEL ** -0.5),
        'b_ada': nrm(ks[10], (DEPTH, 6 * D_MODEL), 0.02),
        'w_in': nrm(ks[11], (DEPTH, D_MODEL, IN_COLS), D_MODEL ** -0.5),
        'qk_gain': 1.0 + nrm(ks[12], (DEPTH, 6, HEAD_DIM), 0.1),
        'cmp_pos': nrm(ks[13], (DEPTH, 2, CMP_LEN, HEAD_DIM), 0.1),
        'cmp_w1': nrm(ks[14], (DEPTH, 2, CMP_LEN * HEAD_DIM, HEAD_DIM), (CMP_LEN * HEAD_DIM) ** -0.5),
        'cmp_w2': nrm(ks[15], (DEPTH, 2, HEAD_DIM, HEAD_DIM), HEAD_DIM ** -0.5),
        'w_out': nrm(ks[16], (DEPTH, MIX_WIDTH, D_MODEL), MIX_WIDTH ** -0.5),
        'w_router': nrm(ks[17], (DEPTH, D_MODEL, N_EXPERTS), D_MODEL ** -0.5),
        'b_router': nrm(ks[18], (DEPTH, N_EXPERTS), 0.01),
        'w_gu': nrm(ks[19], (DEPTH, N_EXPERTS, D_MODEL, 2 * D_EXPERT), D_MODEL ** -0.5),
        'b_gu': nrm(ks[20], (DEPTH, N_EXPERTS, 2 * D_EXPERT), 0.02),
        'w_down': nrm(ks[21], (DEPTH, N_EXPERTS, D_EXPERT, D_MODEL), D_EXPERT ** -0.5),
        'b_down': nrm(ks[22], (DEPTH, N_EXPERTS, D_MODEL), 0.02),
    }


def reference(x_prompt, x_sample, c_prompt, c_sample, cache_moba_kv, cache_nsa_kv, state_nsa_win_kv, page_table,
              norm_g, w_ada, b_ada, w_in, qk_gain, cmp_pos, cmp_w1, cmp_w2, w_out,
              w_router, b_router, w_gu, b_gu, w_down, b_down):
    yp = x_prompt
    ys = x_sample
    moba_p, nsa_p, win_p, moba_s, nsa_s, win_s = [], [], [], [], [], []
    for layer in range(DEPTH):
        cmp_p = (cmp_pos[layer], cmp_w1[layer], cmp_w2[layer])
        sh1p, sc1p, gt1p, sh2p, sc2p, gt2p = _ada(c_prompt, w_ada[layer], b_ada[layer])
        sh1s, sc1s, gt1s, sh2s, sc2s, gt2s = _ada(c_sample, w_ada[layer], b_ada[layer])
        h = _modulate(yp, norm_g[layer, 0], sh1p, sc1p)
        mix, m_rows, n_rows, w_rows = _prompt_mixer(h, w_in[layer], qk_gain[layer], cmp_p, w_out[layer])
        yp = yp + gt1p * mix
        h = _modulate(yp, norm_g[layer, 1], sh2p, sc2p)
        yp = yp + gt2p * _moe(h, layer, w_router, b_router, w_gu, b_gu, w_down, b_down)
        moba_p.append(m_rows)
        nsa_p.append(n_rows)
        win_p.append(w_rows)
        h = _modulate(ys, norm_g[layer, 0], sh1s, sc1s)
        mix, m_rows, n_rows, w_rows = _sample_mixer(h, layer, cache_moba_kv, cache_nsa_kv, state_nsa_win_kv[layer],
                                                    page_table, w_in[layer], qk_gain[layer], cmp_p, w_out[layer])
        ys = ys + gt1s * mix
        h = _modulate(ys, norm_g[layer, 1], sh2s, sc2s)
        ys = ys + gt2s * _moe(h, layer, w_router, b_router, w_gu, b_gu, w_down, b_down)
        moba_s.append(m_rows)
        nsa_s.append(n_rows)
        win_s.append(w_rows)
    return (yp, ys, jnp.stack(moba_p), jnp.stack(nsa_p), jnp.stack(win_p),
            jnp.stack(moba_s), jnp.stack(nsa_s), jnp.stack(win_s))
```

```python
import functools

import numpy as np
import jax
import jax.numpy as jnp
from jax import lax
from jax.experimental import pallas as pl
from jax.experimental.pallas import tpu as pltpu

F32 = jnp.float32
BF16 = jnp.bfloat16
HIGHEST = lax.Precision.HIGHEST

LANES = 128
SUBLANES = 8
HEAD_DIM = 64
H_MOBA = 8
H_NSA = 8
G_NSA = 2
R_NSA = H_NSA // G_NSA
MOBA_BLOCK = 256
MOBA_TOPK = 3
CMP_LEN = 32
CMP_STRIDE = 16
SEL_BLOCK = 64
SEL_TOPN = 16
WINDOW = 512
N_EXPERTS = 32
TOP_K = 4
SWIGLU_LIMIT = 7.0
SWIGLU_ALPHA = 1.702
ROPE_THETA = 10000.0
NORM_EPS = 1e-6
NEG_BIG = -1e30
TINY = 1e-30
SCALE = HEAD_DIM ** -0.5
Q_TILE = 128
MOE_ROWS = 128
VMEM_LIMIT = 56 * 1024 * 1024

W_QM = H_MOBA * HEAD_DIM
W_KVM = 2 * H_MOBA * HEAD_DIM
W_QN = H_NSA * HEAD_DIM
W_KVN = 6 * G_NSA * HEAD_DIM
N_GATE = 3 * H_NSA
IN_COLS = W_QM + W_KVM + W_QN + W_KVN + N_GATE
IN_COLS_PAD = W_QM + W_KVM + W_QN + W_KVN + LANES


def _iota(shape, dim):
    return lax.broadcasted_iota(jnp.int32, shape, dim)


def _dot(a, b, precision=None):
    return jnp.dot(a, b, preferred_element_type=F32, precision=precision)


def _dot_nt(a, b, precision=None):
    return lax.dot_general(a, b, (((1,), (1,)), ((), ())), preferred_element_type=F32, precision=precision)


def _params(*sem):
    return pltpu.CompilerParams(dimension_semantics=sem, vmem_limit_bytes=VMEM_LIMIT)


def _seg_meansq(z, bd):
    zz = z * z
    hi = zz.astype(BF16)
    lo = (zz - hi.astype(F32)).astype(BF16)
    outs = []
    for c in range(z.shape[1] // LANES):
        sl = slice(c * LANES, (c + 1) * LANES)
        outs.append(_dot(hi[:, sl], bd) + _dot(lo[:, sl], bd))
    return outs[0] if len(outs) == 1 else jnp.concatenate(outs, axis=1)


def _head_norm(z, gain, bd):
    return z * lax.rsqrt(_seg_meansq(z, bd) + NORM_EPS) * gain


def _rope(z, cos, sin):
    outs = []
    first = (_iota((z.shape[0], LANES), 1) % HEAD_DIM) < (HEAD_DIM // 2)
    for c in range(z.shape[1] // LANES):
        x = z[:, c * LANES:(c + 1) * LANES]
        swapped = jnp.where(first, pltpu.roll(x, LANES - HEAD_DIM // 2, 1), pltpu.roll(x, HEAD_DIM // 2, 1))
        outs.append(x * cos + swapped * sin)
    return outs[0] if len(outs) == 1 else jnp.concatenate(outs, axis=1)


def _top_k_lanes(cur, k):
    lane = _iota(cur.shape, 1).astype(F32)
    picked = jnp.zeros(cur.shape, F32)
    vals, ids = [], []
    for _ in range(k):
        mx = jnp.max(cur, axis=1, keepdims=True)
        first = jnp.min(jnp.where(cur == mx, lane, 1e9), axis=1, keepdims=True)
        hit = lane == first
        picked = jnp.where(hit, 1.0, picked)
        cur = jnp.where(hit, -jnp.inf, cur)
        vals.append(mx)
        ids.append(first)
    return picked, vals, ids


def _flash_step(q, k, v, mask, m, l, acc):
    s = _dot_nt(q, k)
    if mask is not None:
        s = jnp.where(mask, s, NEG_BIG)
    m_new = jnp.maximum(m, jnp.max(s, axis=1, keepdims=True))
    alpha = jnp.exp(m - m_new)
    p = jnp.exp(s - m_new)
    l_new = alpha * l + jnp.sum(p, axis=1, keepdims=True)
    acc_new = alpha * acc + _dot(p.astype(BF16), v)
    return m_new, l_new, acc_new


def _ada_kernel(c_ref, w_ref, b_ref, o_ref):
    c = c_ref[...]
    o_ref[...] = _dot(c * jax.nn.sigmoid(c), w_ref[...], HIGHEST) + b_ref[...]


def _ada(c_all, w_ada, b_ada):
    n, d = c_all.shape
    cols = w_ada.shape[1]
    tn = 1024
    return pl.pallas_call(
        _ada_kernel,
        grid=(cols // tn,),
        in_specs=[pl.BlockSpec((n, d), lambda j: (0, 0)),
                  pl.BlockSpec((d, tn), lambda j: (0, j)),
                  pl.BlockSpec((1, tn), lambda j: (0, j))],
        out_specs=pl.BlockSpec((n, tn), lambda j: (0, j)),
        out_shape=jax.ShapeDtypeStruct((n, cols), F32),
        compiler_params=_params("arbitrary"),
        name="ada",
    )(c_all, w_ada, b_ada.reshape(1, cols))


def _proj_kernel(x_ref, sh_ref, sc_ref, g_ref, w_ref, gains_ref, bd_ref, cos_ref, sin_ref,
                 moba_ref, nsa_ref, win_ref, qm_ref, qn_ref, qr_ref, gate_ref, *maybe_kmean, with_kmean):
    x = x_ref[...]
    y = x * lax.rsqrt(jnp.mean(x * x, axis=1, keepdims=True) + NORM_EPS) * g_ref[...]
    h = (y * (1.0 + sc_ref[0]) + sh_ref[0]).astype(BF16)
    bd = bd_ref[...]
    cos = cos_ref[...]
    sin = sin_ref[...]
    o = 0

    def seg(width):
        nonlocal o
        z = _dot(h, w_ref[:, o:o + width])
        o += width
        return z

    def gain(i, width):
        return gains_ref[i:i + 1, 0:width]

    qm_ref[...] = _rope(_head_norm(seg(W_QM), gain(0, W_QM), bd), cos, sin)
    k_m = _rope(_head_norm(seg(W_QM), gain(1, W_QM), bd), cos, sin)
    moba_ref[:, 0:W_QM] = k_m
    moba_ref[:, W_QM:2 * W_QM] = seg(W_QM)
    qn = _head_norm(seg(W_QN), gain(2, W_QN), bd)
    qn_ref[...] = qn
    qr_ref[...] = _rope(qn, cos, sin)
    gw = G_NSA * HEAD_DIM
    nsa_ref[:, 0:2 * gw] = seg(2 * gw)
    nsa_ref[:, 2 * gw:3 * gw] = _rope(_head_norm(seg(gw), gain(4, gw), bd), cos, sin)
    nsa_ref[:, 3 * gw:4 * gw] = seg(gw)
    win_ref[:, 0:gw] = _rope(_head_norm(seg(gw), gain(5, gw), bd), cos, sin)
    win_ref[:, gw:2 * gw] = seg(gw)
    gate_ref[...] = jax.nn.sigmoid(seg(LANES))
    if with_kmean:
        (kmean_ref,) = maybe_kmean
        tm = k_m.shape[0]
        kmean_ref[0] = jnp.mean(k_m.reshape(tm // MOBA_BLOCK, MOBA_BLOCK, W_QM), axis=1)


def _proj(x, shift, scale, g, w_in_bf, gains, bd, cos, sin, *, tm, rows_per_mod, pos_blocks, with_kmean):
    n, d = x.shape
    nt = n // tm
    mod_r = shift.shape[1]
    mod_map = lambda i: (i // (rows_per_mod // tm), 0, 0)
    pos_map = lambda i: (i % pos_blocks, 0)
    row = lambda w: pl.BlockSpec((tm, w), lambda i: (i, 0))
    const = lambda a: pl.BlockSpec(a.shape, lambda i: (0,) * a.ndim)
    out_shapes = [jax.ShapeDtypeStruct((n, W_KVM), F32), jax.ShapeDtypeStruct((n, 4 * G_NSA * HEAD_DIM), F32),
                  jax.ShapeDtypeStruct((n, 2 * G_NSA * HEAD_DIM), F32), jax.ShapeDtypeStruct((n, W_QM), F32),
                  jax.ShapeDtypeStruct((n, W_QN), F32), jax.ShapeDtypeStruct((n, W_QN), F32),
                  jax.ShapeDtypeStruct((n, LANES), F32)]
    out_specs = [row(W_KVM), row(4 * G_NSA * HEAD_DIM), row(2 * G_NSA * HEAD_DIM), row(W_QM), row(W_QN), row(W_QN),
                 row(LANES)]
    if with_kmean:
        nbt = tm // MOBA_BLOCK
        out_shapes.append(jax.ShapeDtypeStruct((nt, nbt, W_QM), F32))
        out_specs.append(pl.BlockSpec((1, nbt, W_QM), lambda i: (i, 0, 0)))
    return pl.pallas_call(
        functools.partial(_proj_kernel, with_kmean=with_kmean),
        grid=(nt,),
        in_specs=[row(d), pl.BlockSpec((1, mod_r, d), mod_map), pl.BlockSpec((1, mod_r, d), mod_map),
                  const(g), const(w_in_bf), const(gains), const(bd),
                  pl.BlockSpec((tm, LANES), pos_map), pl.BlockSpec((tm, LANES), pos_map)],
        out_specs=out_specs,
        out_shape=out_shapes,
        compiler_params=_params("parallel"),
        name="proj",
    )(x, shift, scale, g, w_in_bf, gains, bd, cos, sin)


def _compress_compute(src_refs, pe_ref, w1_ref, w2_ref, gk_ref, bd_ref, kc_ref, vc_ref, n_rows):
    n_chunk = n_rows // CMP_STRIDE
    gw = G_NSA * HEAD_DIM
    for kv in range(2):
        acc_a = jnp.zeros((n_chunk, gw), F32)
        acc_b = jnp.zeros((n_chunk, gw), F32)
        for j in range(CMP_STRIDE):
            xj = src_refs[kv][pl.ds(j, n_chunk, stride=CMP_STRIDE), :]
            acc_a = acc_a + _dot(xj + pe_ref[kv, 0, j:j + 1, :], w1_ref[kv, 0, j], HIGHEST)
            acc_b = acc_b + _dot(xj + pe_ref[kv, 1, j:j + 1, :], w1_ref[kv, 1, j], HIGHEST)
        hid = jax.nn.gelu(acc_a + pltpu.roll(acc_b, n_chunk - 1, 0))
        out = _dot(hid, w2_ref[kv], HIGHEST)
        if kv == 0:
            kc_ref[0] = _head_norm(out, gk_ref[...], bd_ref[...])
        else:
            vc_ref[0] = out


def _compress_prompt_kernel(k_ref, v_ref, pe_ref, w1_ref, w2_ref, gk_ref, bd_ref, kc_ref, vc_ref, *, n_rows):
    _compress_compute((k_ref.at[0], v_ref.at[0]), pe_ref, w1_ref, w2_ref, gk_ref, bd_ref, kc_ref, vc_ref, n_rows)


def _compress_paged_kernel(pt_ref, page_ref, pe_ref, w1_ref, w2_ref, gk_ref, bd_ref, kc_ref, vc_ref, kbuf, vbuf,
                           *, n_rows, page):
    p = pl.program_id(1)
    gw = G_NSA * HEAD_DIM
    rows = pl.ds(pl.multiple_of(p * page, page), page)
    kbuf[rows, :] = page_ref[0, :, 0:gw]
    vbuf[rows, :] = page_ref[0, :, gw:2 * gw]

    @pl.when(p == pl.num_programs(1) - 1)
    def _():
        _compress_compute((kbuf, vbuf), pe_ref, w1_ref, w2_ref, gk_ref, bd_ref, kc_ref, vc_ref, n_rows)


def _compress_consts(cmp_pos, cmp_w1, cmp_w2, gain_k_cmp):
    pe = jnp.tile(cmp_pos.reshape(2, 2, CMP_STRIDE, HEAD_DIM), (1, 1, 1, G_NSA))
    eye = jnp.eye(G_NSA, dtype=F32)
    w1 = cmp_w1.reshape(2, 2, CMP_STRIDE, HEAD_DIM, HEAD_DIM)
    w1bd = jnp.einsum("gh,kajde->kajgdhe", eye, w1).reshape(2, 2, CMP_STRIDE, G_NSA * HEAD_DIM, G_NSA * HEAD_DIM)
    w2bd = jnp.einsum("gh,kde->kgdhe", eye, cmp_w2).reshape(2, G_NSA * HEAD_DIM, G_NSA * HEAD_DIM)
    gk = jnp.tile(gain_k_cmp.reshape(1, HEAD_DIM), (1, G_NSA))
    return pe, w1bd, w2bd, gk


def _compress_prompt(nsa_rows, consts, bd):
    b, s, _ = nsa_rows.shape
    pe, w1bd, w2bd, gk = consts
    n_chunk = s // CMP_STRIDE
    gw = G_NSA * HEAD_DIM
    const = lambda a: pl.BlockSpec(a.shape, lambda i: (0,) * a.ndim)
    out = jax.ShapeDtypeStruct((b, n_chunk, gw), F32)
    return pl.pallas_call(
        functools.partial(_compress_prompt_kernel, n_rows=s),
        grid=(b,),
        in_specs=[pl.BlockSpec((1, s, gw), lambda i: (i, 0, 0)), pl.BlockSpec((1, s, gw), lambda i: (i, 0, 1)),
                  const(pe), const(w1bd), const(w2bd), const(gk), const(bd)],
        out_specs=[pl.BlockSpec((1, n_chunk, gw), lambda i: (i, 0, 0))] * 2,
        out_shape=[out, out],
        compiler_params=_params("parallel"),
        name="compress_prompt",
    )(nsa_rows, nsa_rows, pe, w1bd, w2bd, gk, bd)


def _compress_paged(cache_nsa, pt_flat, consts, bd, *, n_req, n_pages):
    _, page, _ = cache_nsa.shape
    pe, w1bd, w2bd, gk = consts
    n_rows = n_pages * page
    n_chunk = n_rows // CMP_STRIDE
    gw = G_NSA * HEAD_DIM
    const = lambda a: pl.BlockSpec(a.shape, lambda b, p, pt: (0,) * a.ndim)
    out = jax.ShapeDtypeStruct((n_req, n_chunk, gw), F32)
    return pl.pallas_call(
        functools.partial(_compress_paged_kernel, n_rows=n_rows, page=page),
        grid_spec=pltpu.PrefetchScalarGridSpec(
            num_scalar_prefetch=1,
            grid=(n_req, n_pages),
            in_specs=[pl.BlockSpec((1, page, 2 * gw), lambda b, p, pt: (pt[b * n_pages + p], 0, 0)),
                      const(pe), const(w1bd), const(w2bd), const(gk), const(bd)],
            out_specs=[pl.BlockSpec((1, n_chunk, gw), lambda b, p, pt: (b, 0, 0))] * 2,
            scratch_shapes=[pltpu.VMEM((n_rows, gw), F32)] * 2),
        out_shape=[out, out],
        compiler_params=_params("parallel", "arbitrary"),
        name="compress_paged",
    )(pt_flat, cache_nsa, pe, w1bd, w2bd, gk, bd)


def _moba_kernel(q_ref, kmp_ref, k_ref, v_ref, o_ref):
    qi = pl.program_id(2)
    own = (qi * Q_TILE) // MOBA_BLOCK
    q2 = q_ref[0]
    lane = _iota((Q_TILE, LANES), 1)
    pos = qi * Q_TILE + _iota((Q_TILE, MOBA_BLOCK), 0)
    blk = lane - HEAD_DIM
    valid = (blk >= 0) & (blk < own)
    outs = []
    for h in range(2):
        qh = q2 if h == 0 else pltpu.roll(q2, HEAD_DIM, 1)
        q0 = jnp.where(lane < HEAD_DIM, qh, 0.0)
        score = _dot_nt(q0, kmp_ref[0, h], HIGHEST)
        picked, _, _ = _top_k_lanes(jnp.where(valid, score, -jnp.inf), MOBA_TOPK)
        sel = ((picked > 0.5) & valid) | (blk == own)
        bias = jnp.where(sel, 0.0, NEG_BIG)
        q_aug = jnp.where(lane < HEAD_DIM, q0 * SCALE,
                          jnp.where(lane < HEAD_DIM + MOBA_BLOCK // SUBLANES, bias, 0.0)).astype(BF16)
        kcols = slice(h * LANES, (h + 1) * LANES)

        def tile(j, carry, masked, q_aug=q_aug, kcols=kcols):
            start = pl.multiple_of(j * MOBA_BLOCK, MOBA_BLOCK)
            k = k_ref[0, pl.ds(start, MOBA_BLOCK), kcols]
            v = v_ref[0, pl.ds(start, MOBA_BLOCK), :]
            mask = None
            if masked:
                mask = (j * MOBA_BLOCK + _iota((Q_TILE, MOBA_BLOCK), 1)) <= pos
            return _flash_step(q_aug, k, v, mask, *carry)

        init = (jnp.full((Q_TILE, 1), NEG_BIG, F32), jnp.zeros((Q_TILE, 1), F32), jnp.zeros((Q_TILE, LANES), F32))
        carry = tile(own, init, True)
        m, l, acc = lax.fori_loop(0, own, lambda j, c: tile(j, c, False), carry)
        outs.append(acc / jnp.maximum(l, TINY))
    o_ref[0] = jnp.where(lane < HEAD_DIM, outs[0], outs[1]).astype(o_ref.dtype)


def _moba_attention(q_m, kmp, k_aug, v_m):
    b, s, _ = q_m.shape
    return pl.pallas_call(
        _moba_kernel,
        grid=(b, H_MOBA // 2, s // Q_TILE),
        in_specs=[pl.BlockSpec((1, Q_TILE, LANES), lambda b, h, i: (b, i, h)),
                  pl.BlockSpec((1, 2, LANES, LANES), lambda b, h, i: (b, h, 0, 0)),
                  pl.BlockSpec((1, s, 2 * LANES), lambda b, h, i: (b, 0, h)),
                  pl.BlockSpec((1, s, LANES), lambda b, h, i: (b, 0, h))],
        out_specs=pl.BlockSpec((1, Q_TILE, LANES), lambda b, h, i: (b, i, h)),
        out_shape=jax.ShapeDtypeStruct((b, s, W_QM), BF16),
        compiler_params=_params("parallel", "parallel", "arbitrary"),
        name="moba_attention",
    )(q_m, kmp, k_aug, v_m)


def _stack_heads(q4):
    lane = _iota((Q_TILE, LANES), 1)
    parts = []
    for r in range(R_NSA):
        c = q4[:, (r // 2) * LANES:(r // 2 + 1) * LANES]
        if r % 2:
            c = pltpu.roll(c, HEAD_DIM, 1)
        parts.append(jnp.where(lane < HEAD_DIM, c, 0.0))
    return jnp.concatenate(parts, axis=0)


def _nsa_kernel(qn_ref, qr_ref, gate_ref, kc_ref, vc_ref, cover_ref, ks_ref, vs_ref, kw_ref, vw_ref, o_ref,
                *, n_chunk):
    qi = pl.program_id(2)
    rows = R_NSA * Q_TILE
    qloc = _iota((rows, 1), 0) % Q_TILE
    pos = qi * Q_TILE + qloc

    qn = _stack_heads(qn_ref[0])
    s = _dot_nt(qn, kc_ref[0, 0], HIGHEST) * SCALE
    cmask = (_iota((rows, n_chunk), 1) * CMP_STRIDE + (CMP_LEN - 1)) <= pos
    s = jnp.where(cmask, s, NEG_BIG)
    e = jnp.where(cmask, jnp.exp(s - jnp.max(s, axis=1, keepdims=True)), 0.0)
    p_cmp = e / jnp.maximum(jnp.sum(e, axis=1, keepdims=True), TINY)
    o_cmp = _dot(p_cmp.astype(BF16), vc_ref[0, 0])

    p_grp = p_cmp[0:Q_TILE]
    for r in range(1, R_NSA):
        p_grp = p_grp + p_cmp[r * Q_TILE:(r + 1) * Q_TILE]
    imp = _dot(p_grp, cover_ref[...], HIGHEST)
    blk = _iota((Q_TILE, LANES), 1)
    own = (qi * Q_TILE + _iota((Q_TILE, 1), 0)) // SEL_BLOCK
    forced = (blk == 0) | (blk == own) | (blk == own - 1)
    score = jnp.where(blk <= own, jnp.where(forced, jnp.inf, imp), -jnp.inf)
    picked, _, _ = _top_k_lanes(score, SEL_TOPN)
    bias = jnp.where((picked > 0.5) & (blk <= own), 0.0, NEG_BIG)

    qr = (_stack_heads(qr_ref[0]) * SCALE).astype(BF16)
    q_aug = jnp.concatenate([jnp.concatenate([bias] * R_NSA, axis=0).astype(BF16), qr], axis=1)
    init = (jnp.full((rows, 1), NEG_BIG, F32), jnp.zeros((rows, 1), F32), jnp.zeros((rows, LANES), F32))

    tk = MOBA_BLOCK
    jd = (qi * Q_TILE) // tk

    def sel_tile(j, carry, masked):
        start = pl.multiple_of(j * tk, tk)
        mask = None
        if masked:
            mask = (j * tk + _iota((rows, tk), 1)) <= pos
        return _flash_step(q_aug, ks_ref[0, 0, pl.ds(start, tk), :], vs_ref[0, 0, pl.ds(start, tk), :], mask, *carry)

    m, l, acc = lax.fori_loop(0, jd, lambda j, c: sel_tile(j, c, False), sel_tile(jd, init, True))
    o_sel = acc / jnp.maximum(l, TINY)

    kloc = _iota((rows, Q_TILE), 1)

    def win_tile(kt, mask, carry):
        start = pl.multiple_of(jnp.maximum(kt, 0) * Q_TILE, Q_TILE)
        return _flash_step(qr, kw_ref[0, 0, pl.ds(start, Q_TILE), :], vw_ref[0, 0, pl.ds(start, Q_TILE), :], mask,
                           *carry)

    carry = win_tile(qi, kloc <= qloc, init)
    n_back = WINDOW // Q_TILE
    for t in range(1, n_back + 1):
        mask = ((qi - t) * Q_TILE + kloc) >= 0
        if t == n_back:
            mask = mask & (kloc > qloc)
        carry = win_tile(qi - t, mask, carry)
    m, l, acc = carry
    o_win = acc / jnp.maximum(l, TINY)

    gates = gate_ref[0, 0]
    heads = []
    for r in range(R_NSA):
        rs = slice(r * Q_TILE, (r + 1) * Q_TILE)
        heads.append(gates[:, 3 * r:3 * r + 1] * o_cmp[rs] + gates[:, 3 * r + 1:3 * r + 2] * o_sel[rs]
                     + gates[:, 3 * r + 2:3 * r + 3] * o_win[rs])
    lo = blk < HEAD_DIM
    o_ref[0] = jnp.concatenate([jnp.where(lo, heads[0], heads[1]), jnp.where(lo, heads[2], heads[3])],
                               axis=1).astype(o_ref.dtype)


def _nsa_attention(qn, qr, gates_g, kcp, vcd, cover, ks_aug, vs_dup, kw_pad, vw_dup):
    b, s, _ = qn.shape
    n_chunk = kcp.shape[2]
    gq = R_NSA * HEAD_DIM
    per_g = lambda w: pl.BlockSpec((1, 1, s, w), lambda b, g, i: (b, g, 0, 0))
    return pl.pallas_call(
        functools.partial(_nsa_kernel, n_chunk=n_chunk),
        grid=(b, G_NSA, s // Q_TILE),
        in_specs=[pl.BlockSpec((1, Q_TILE, gq), lambda b, g, i: (b, i, g)),
                  pl.BlockSpec((1, Q_TILE, gq), lambda b, g, i: (b, i, g)),
                  pl.BlockSpec((1, 1, Q_TILE, LANES), lambda b, g, i: (b, g, i, 0)),
                  pl.BlockSpec((1, 1, n_chunk, LANES), lambda b, g, i: (b, g, 0, 0)),
                  pl.BlockSpec((1, 1, n_chunk, LANES), lambda b, g, i: (b, g, 0, 0)),
                  pl.BlockSpec(cover.shape, lambda b, g, i: (0, 0)),
                  per_g(2 * LANES), per_g(LANES), per_g(LANES), per_g(LANES)],
        out_specs=pl.BlockSpec((1, Q_TILE, gq), lambda b, g, i: (b, i, g)),
        out_shape=jax.ShapeDtypeStruct((b, s, W_QN), BF16),
        compiler_params=_params("parallel", "parallel", "arbitrary"),
        name="nsa_attention",
    )(qn, qr, gates_g, kcp, vcd, cover, ks_aug, vs_dup, kw_pad, vw_dup)


def _cover(n_cmp, n_sel, rows):
    c0 = np.arange(rows)[:, None] * CMP_STRIDE
    b0 = np.arange(LANES)[None, :] * SEL_BLOCK
    ok = (c0 < b0 + SEL_BLOCK) & (c0 + CMP_LEN > b0) & (np.arange(rows)[:, None] < n_cmp) & (np.arange(LANES)[None, :] < n_sel)
    return jnp.asarray(ok.astype(np.float32))


def _dec_kernel(pt_ref, ma_ref, mb_ref, na_ref, nb_ref, wb_ref, qbd_ref, q8n_ref, q8r_ref, g8_ref, newm_ref,
                news_ref, neww_ref, kc_ref, vc_ref, cover_ref, om_ref, o8_ref,
                m_sc, l_sc, s_sc, acc_sc, msel_sc, lsel_sc, asel_sc, bias_sc, ocmp_sc, *, n_steps, page):
    j = pl.program_id(1)
    sub8 = _iota((SUBLANES, LANES), 0)
    lane8 = _iota((SUBLANES, LANES), 1)
    tk = 2 * page
    q8r = q8r_ref[0] * SCALE
    q8r_bf = q8r.astype(BF16)

    @pl.when(j == 0)
    def _():
        n_chunk = kc_ref.shape[1]
        s = _dot_nt(q8n_ref[0], kc_ref[0], HIGHEST) * SCALE
        cmask = _iota((SUBLANES, n_chunk), 1) < (n_chunk - 1)
        s = jnp.where(cmask, s, NEG_BIG)
        e = jnp.where(cmask, jnp.exp(s - jnp.max(s, axis=1, keepdims=True)), 0.0)
        p_cmp = e / jnp.maximum(jnp.sum(e, axis=1, keepdims=True), TINY)
        ocmp_sc[...] = _dot(p_cmp.astype(BF16), vc_ref[0].astype(BF16))
        subc = _iota((SUBLANES, n_chunk), 0)
        g0 = jnp.sum(jnp.where(subc < R_NSA, p_cmp, 0.0), axis=0, keepdims=True)
        g1 = jnp.sum(jnp.where(subc >= R_NSA, p_cmp, 0.0), axis=0, keepdims=True)
        imp = _dot(jnp.where(subc < R_NSA, g0, g1), cover_ref[...], HIGHEST)
        own = (n_steps * tk) // SEL_BLOCK
        forced = (lane8 == 0) | (lane8 == own) | (lane8 == own - 1)
        score = jnp.where(lane8 <= own, jnp.where(forced, jnp.inf, imp), -jnp.inf)
        picked, _, _ = _top_k_lanes(score, SEL_TOPN)
        bias_sc[...] = jnp.where((picked > 0.5) & (lane8 <= own), 0.0, NEG_BIG)
        msel_sc[...] = jnp.full((SUBLANES, LANES), NEG_BIG, F32)
        lsel_sc[...] = jnp.zeros((SUBLANES, LANES), F32)
        asel_sc[...] = jnp.zeros((SUBLANES, LANES), F32)

    hw = H_MOBA * HEAD_DIM
    k_m = jnp.concatenate([ma_ref[0, :, 0:hw], mb_ref[0, :, 0:hw]], axis=0)
    v_m = jnp.concatenate([ma_ref[0, :, hw:2 * hw], mb_ref[0, :, hw:2 * hw]], axis=0)
    qbd = qbd_ref[0]
    s = _dot_nt((qbd * SCALE).astype(BF16), k_m.astype(BF16))
    mj = jnp.max(s, axis=1, keepdims=True)
    p = jnp.exp(s - mj)
    m_sc[j] = jnp.broadcast_to(mj, (SUBLANES, LANES))
    l_sc[j] = jnp.broadcast_to(jnp.sum(p, axis=1, keepdims=True), (SUBLANES, LANES))
    acc_sc[j] = _dot(p.astype(BF16), v_m.astype(BF16))
    kmean = jnp.sum(k_m, axis=0, keepdims=True) * (1.0 / MOBA_BLOCK)
    s_sc[j] = jnp.broadcast_to(jnp.sum(qbd * kmean, axis=1, keepdims=True), (SUBLANES, LANES))

    gw = G_NSA * HEAD_DIM
    k_s = jnp.concatenate([na_ref[0, :, 0:gw], nb_ref[0, :, 0:gw]], axis=0).astype(BF16)
    v_s = jnp.concatenate([na_ref[0, :, gw:2 * gw], nb_ref[0, :, gw:2 * gw]], axis=0).astype(BF16)
    s = _dot_nt(q8r_bf, k_s)
    bias = bias_sc[...]
    kb = _iota((SUBLANES, tk), 1) // SEL_BLOCK
    bias_k = jnp.zeros((SUBLANES, tk), F32)
    for q in range(tk // SEL_BLOCK):
        bq = jnp.sum(jnp.where(lane8 == j * (tk // SEL_BLOCK) + q, bias, 0.0), axis=1, keepdims=True)
        bias_k = jnp.where(kb == q, bq, bias_k)
    s = s + bias_k
    m_old = msel_sc[:, 0:1]
    m_new = jnp.maximum(m_old, jnp.max(s, axis=1, keepdims=True))
    alpha = jnp.exp(m_old - m_new)
    p = jnp.exp(s - m_new)
    msel_sc[...] = jnp.broadcast_to(m_new, (SUBLANES, LANES))
    lsel_sc[...] = alpha * lsel_sc[...] + jnp.sum(p, axis=1, keepdims=True)
    asel_sc[...] = alpha * asel_sc[...] + _dot(p.astype(BF16), v_s)

    @pl.when(j == n_steps - 1)
    def _():
        news = news_ref[0]
        s_new = jnp.sum(q8r * news[0:1], axis=1, keepdims=True)
        m_old = msel_sc[:, 0:1]
        m_new = jnp.maximum(m_old, s_new)
        alpha = jnp.exp(m_old - m_new)
        pn = jnp.exp(s_new - m_new)
        l = alpha * lsel_sc[...] + pn
        o_sel = (alpha * asel_sc[...] + pn * news[1:2]) / jnp.maximum(l, TINY)

        neww = neww_ref[0]
        kw = wb_ref[0, :, 0:gw].astype(BF16)
        vw = wb_ref[0, :, gw:2 * gw].astype(BF16)
        nw = kw.shape[0]
        s = _dot_nt(q8r_bf, kw)
        wmask = _iota((SUBLANES, nw), 1) >= 1
        s = jnp.where(wmask, s, NEG_BIG)
        s_new = jnp.sum(q8r * neww[0:1], axis=1, keepdims=True)
        mw = jnp.maximum(jnp.max(s, axis=1, keepdims=True), s_new)
        e = jnp.where(wmask, jnp.exp(s - mw), 0.0)
        en = jnp.exp(s_new - mw)
        lw = jnp.sum(e, axis=1, keepdims=True) + en
        o_win = (_dot(e.astype(BF16), vw) + en * neww[1:2]) / jnp.maximum(lw, TINY)

        g8 = g8_ref[0]
        o8_ref[0] = g8[:, 0:1] * ocmp_sc[...] + g8[:, 1:2] * o_sel + g8[:, 2:3] * o_win

        newm = newm_ref[0]
        scores = [s_sc[b] for b in range(n_steps)]
        m_tot = jnp.broadcast_to(jnp.sum(qbd * newm[0:1], axis=1, keepdims=True) * SCALE, (SUBLANES, LANES))
        s_own = m_tot
        sels = []
        for b in range(n_steps):
            rank = jnp.zeros((SUBLANES, LANES), F32)
            for c in range(n_steps):
                if c == b:
                    continue
                ahead = (scores[c] > scores[b]) | ((scores[c] == scores[b]) & (c < b))
                rank = rank + jnp.where(ahead, 1.0, 0.0)
            sels.append(rank < MOBA_TOPK)
            m_tot = jnp.where(sels[b], jnp.maximum(m_tot, m_sc[b]), m_tot)
        w_own = jnp.exp(s_own - m_tot)
        l_tot = w_own
        acc = w_own[:, 0:1] * newm[1:2]
        for b in range(n_steps):
            wgt = jnp.where(sels[b], jnp.exp(m_sc[b] - m_tot), 0.0)
            l_tot = l_tot + wgt * l_sc[b]
            acc = acc + wgt[:, 0:1] * acc_sc[b]
        o_full = acc / jnp.maximum(l_tot[:, 0:1], TINY)
        own_head = (_iota((SUBLANES, hw), 1) // HEAD_DIM) == _iota((SUBLANES, hw), 0)
        om_ref[0] = jnp.sum(jnp.where(own_head, o_full, 0.0), axis=0, keepdims=True)


def _dec_attention(cache_moba, cache_nsa, win_buf, pt_flat, qbd, q8n, q8r, g8, newm, news, neww, kc, vc, cover,
                   *, n_req, n_pages):
    _, page, mw = cache_moba.shape
    n_steps = n_pages // 2
    gw = G_NSA * HEAD_DIM
    hw = H_MOBA * HEAD_DIM
    n_chunk = kc.shape[1]
    nwin = win_buf.shape[1]
    req = lambda a: pl.BlockSpec((1,) + a.shape[1:], lambda b, j, pt: (b,) + (0,) * (a.ndim - 1))
    return pl.pallas_call(
        functools.partial(_dec_kernel, n_steps=n_steps, page=page),
        grid_spec=pltpu.PrefetchScalarGridSpec(
            num_scalar_prefetch=1,
            grid=(n_req, n_steps),
            in_specs=[pl.BlockSpec((1, page, mw), lambda b, j, pt: (pt[b * n_pages + 2 * j], 0, 0)),
                      pl.BlockSpec((1, page, mw), lambda b, j, pt: (pt[b * n_pages + 2 * j + 1], 0, 0)),
                      pl.BlockSpec((1, page, 2 * gw), lambda b, j, pt: (pt[b * n_pages + 2 * j], 0, 1)),
                      pl.BlockSpec((1, page, 2 * gw), lambda b, j, pt: (pt[b * n_pages + 2 * j + 1], 0, 1)),
                      req(win_buf), req(qbd), req(q8n), req(q8r), req(g8), req(newm), req(news), req(neww),
                      req(kc), req(vc), pl.BlockSpec(cover.shape, lambda b, j, pt: (0, 0))],
            out_specs=[pl.BlockSpec((1, 1, hw), lambda b, j, pt: (b, 0, 0)),
                       pl.BlockSpec((1, SUBLANES, LANES), lambda b, j, pt: (b, 0, 0))],
            scratch_shapes=[pltpu.VMEM((n_steps, SUBLANES, LANES), F32)] * 3
            + [pltpu.VMEM((n_steps, SUBLANES, hw), F32)]
            + [pltpu.VMEM((SUBLANES, LANES), F32)] * 5),
        out_shape=[jax.ShapeDtypeStruct((n_req, 1, hw), F32), jax.ShapeDtypeStruct((n_req, SUBLANES, LANES), F32)],
        compiler_params=_params("parallel", "arbitrary"),
        name="dec_attention",
    )(pt_flat, cache_moba, cache_moba, cache_nsa, cache_nsa, win_buf, qbd, q8n, q8r, g8, newm, news, neww, kc, vc,
      cover)


def _post_kernel(o_ref, x_ref, gt_ref, sh_ref, sc_ref, g_ref, wo_ref, wr_ref, br_ref,
                 y_ref, h3_ref, te_ref, tw_ref):
    y = x_ref[...] + gt_ref[0] * _dot(o_ref[...], wo_ref[...])
    y_ref[...] = y
    h = y * lax.rsqrt(jnp.mean(y * y, axis=1, keepdims=True) + NORM_EPS) * g_ref[...]
    h = h * (1.0 + sc_ref[0]) + sh_ref[0]
    tm = h.shape[0]
    for s in range(h.shape[1] // LANES):
        h3_ref[pl.ds(s, tm, stride=SUBLANES), :] = h[:, s * LANES:(s + 1) * LANES]
    logits = _dot(h, wr_ref[...], HIGHEST) + br_ref[...]
    _, vals, ids = _top_k_lanes(logits, TOP_K)
    lane = _iota((tm, LANES), 1)
    es = [jnp.exp(v - vals[0]) for v in vals]
    den = es[0]
    for e in es[1:]:
        den = den + e
    te = jnp.zeros((tm, LANES), F32)
    tw = jnp.zeros((tm, LANES), F32)
    for k in range(TOP_K):
        te = jnp.where(lane == k, ids[k], te)
        tw = jnp.where(lane == k, es[k] / den, tw)
    te_ref[...] = te.astype(jnp.int32)
    tw_ref[...] = tw


def _post(o, x, gate, shift, scale, g2, wo_bf, wr_pad, br_pad, *, tm, rows_per_mod):
    n, d = x.shape
    mod_r = gate.shape[1]
    mod_map = lambda i: (i // (rows_per_mod // tm), 0, 0)
    row = lambda w: pl.BlockSpec((tm, w), lambda i: (i, 0))
    const = lambda a: pl.BlockSpec(a.shape, lambda i: (0,) * a.ndim)
    mod = pl.BlockSpec((1, mod_r, d), mod_map)
    return pl.pallas_call(
        _post_kernel,
        grid=(n // tm,),
        in_specs=[row(d), row(d), mod, mod, mod, const(g2), const(wo_bf), const(wr_pad), const(br_pad)],
        out_specs=[row(d), pl.BlockSpec((tm * SUBLANES, LANES), lambda i: (i, 0)), row(LANES), row(LANES)],
        out_shape=[jax.ShapeDtypeStruct((n, d), F32), jax.ShapeDtypeStruct((n * SUBLANES, LANES), F32),
                   jax.ShapeDtypeStruct((n, LANES), jnp.int32), jax.ShapeDtypeStruct((n, LANES), F32)],
        compiler_params=_params("parallel"),
        name="post",
    )(o, x, gate, shift, scale, g2, wo_bf, wr_pad, br_pad)


def _expert_kernel(be_ref, na_ref, tok_ref, h3_ref, wgu_ref, bgu_ref, wd_ref, bd_ref, y_ref, buf, xb, sem):
    i = pl.program_id(0)
    n_active = na_ref[0]
    rows8 = MOE_ROWS * SUBLANES

    def gather(blk, slot):
        def body(r, carry):
            t = tok_ref[blk * MOE_ROWS + r]
            pltpu.make_async_copy(h3_ref.at[pl.ds(pl.multiple_of(t * SUBLANES, SUBLANES), SUBLANES), :],
                                  buf.at[slot, pl.ds(pl.multiple_of(r * SUBLANES, SUBLANES), SUBLANES), :],
                                  sem.at[slot]).start()
            return carry
        lax.fori_loop(0, MOE_ROWS, body, 0)

    @pl.when(i == 0)
    def _():
        gather(0, 0)

    @pl.when(i < n_active)
    def _():
        slot = i % 2
        pltpu.make_async_copy(h3_ref.at[pl.ds(0, rows8), :], buf.at[slot], sem.at[slot]).wait()

        @pl.when(i + 1 < n_active)
        def _():
            gather(i + 1, 1 - slot)

        d = xb.shape[1]
        for s in range(d // LANES):
            xb[:, s * LANES:(s + 1) * LANES] = buf[slot, pl.ds(s, MOE_ROWS, stride=SUBLANES), :].astype(BF16)
        gu = _dot(xb[...], wgu_ref[0]) + bgu_ref[0]
        f = gu.shape[1] // 2
        gt = jnp.minimum(gu[:, 0:f], SWIGLU_LIMIT)
        up = jnp.clip(gu[:, f:2 * f], -SWIGLU_LIMIT, SWIGLU_LIMIT)
        act = (up + 1.0) * (gt * jax.nn.sigmoid(SWIGLU_ALPHA * gt))
        y = _dot(act.astype(BF16), wd_ref[0]) + bd_ref[0]
        for s in range(d // LANES):
            y_ref[pl.ds(s, MOE_ROWS, stride=SUBLANES), :] = y[:, s * LANES:(s + 1) * LANES]

    @pl.when(i >= n_active)
    def _():
        y_ref[...] = jnp.zeros(y_ref.shape, F32)


def _experts(blk_e, n_active, tok, h3, wgu_bf, b_gu, wd_bf, b_down):
    n_blocks = blk_e.shape[0]
    e, d, f2 = wgu_bf.shape
    rows8 = MOE_ROWS * SUBLANES
    return pl.pallas_call(
        _expert_kernel,
        grid_spec=pltpu.PrefetchScalarGridSpec(
            num_scalar_prefetch=3,
            grid=(n_blocks,),
            in_specs=[pl.BlockSpec(memory_space=pl.ANY),
                      pl.BlockSpec((1, d, f2), lambda i, be, na, tok: (be[i], 0, 0)),
                      pl.BlockSpec((1, 1, f2), lambda i, be, na, tok: (be[i], 0, 0)),
                      pl.BlockSpec((1, f2 // 2, d), lambda i, be, na, tok: (be[i], 0, 0)),
                      pl.BlockSpec((1, 1, d), lambda i, be, na, tok: (be[i], 0, 0))],
            out_specs=pl.BlockSpec((rows8, LANES), lambda i, be, na, tok: (i, 0)),
            scratch_shapes=[pltpu.VMEM((2, rows8, LANES), F32), pltpu.VMEM((MOE_ROWS, d), BF16),
                            pltpu.SemaphoreType.DMA((2,))]),
        out_shape=jax.ShapeDtypeStruct((n_blocks * rows8, LANES), F32),
        compiler_params=_params("arbitrary"),
        name="experts",
    )(blk_e, n_active, tok, h3, wgu_bf, b_gu.reshape(e, 1, f2), wd_bf, b_down.reshape(e, 1, d))


def _combine_kernel(pos_ref, ys_ref, y1_ref, gt_ref, tw_ref, o_ref, buf, sem, *, tile_off):
    i = pl.program_id(0)
    n = pl.num_programs(0)
    tm = y1_ref.shape[0]
    rows8 = tm * SUBLANES

    def gather(tile, slot):
        def body(r, carry):
            for k in range(TOP_K):
                p = pos_ref[((tile + tile_off) * tm + r) * TOP_K + k]
                pltpu.make_async_copy(ys_ref.at[pl.ds(pl.multiple_of(p * SUBLANES, SUBLANES), SUBLANES), :],
                                      buf.at[slot, k, pl.ds(pl.multiple_of(r * SUBLANES, SUBLANES), SUBLANES), :],
                                      sem.at[slot]).start()
            return carry
        lax.fori_loop(0, tm, body, 0)

    @pl.when(i == 0)
    def _():
        gather(0, 0)

    slot = i % 2
    for k in range(TOP_K):
        pltpu.make_async_copy(ys_ref.at[pl.ds(0, rows8), :], buf.at[slot, k], sem.at[slot]).wait()

    @pl.when(i + 1 < n)
    def _():
        gather(i + 1, 1 - slot)

    tw = tw_ref[...]
    wk = [jnp.broadcast_to(tw[:, k:k + 1], (tm, LANES)) for k in range(TOP_K)]
    gt = gt_ref[0]
    for s in range(o_ref.shape[1] // LANES):
        moe = wk[0] * buf[slot, 0, pl.ds(s, tm, stride=SUBLANES), :]
        for k in range(1, TOP_K):
            moe = moe + wk[k] * buf[slot, k, pl.ds(s, tm, stride=SUBLANES), :]
        cols = slice(s * LANES, (s + 1) * LANES)
        o_ref[:, cols] = y1_ref[:, cols] + gt[:, cols] * moe


def _combine(pos_flat, ys, y1, gate, tw, *, tm, rows_per_mod, tile_off):
    n, d = y1.shape
    mod_r = gate.shape[1]
    return pl.pallas_call(
        functools.partial(_combine_kernel, tile_off=tile_off),
        grid_spec=pltpu.PrefetchScalarGridSpec(
            num_scalar_prefetch=1,
            grid=(n // tm,),
            in_specs=[pl.BlockSpec(memory_space=pl.ANY),
                      pl.BlockSpec((tm, d), lambda i, pos: (i, 0)),
                      pl.BlockSpec((1, mod_r, d), lambda i, pos: (i // (rows_per_mod // tm), 0, 0)),
                      pl.BlockSpec((tm, LANES), lambda i, pos: (i, 0))],
            out_specs=pl.BlockSpec((tm, d), lambda i, pos: (i, 0)),
            scratch_shapes=[pltpu.VMEM((2, TOP_K, tm * SUBLANES, LANES), F32), pltpu.SemaphoreType.DMA((2,))]),
        out_shape=jax.ShapeDtypeStruct((n, d), F32),
        compiler_params=_params("arbitrary"),
        name="combine",
    )(pos_flat, ys, y1, gate, tw)


def _rope_tables(pos):
    half = HEAD_DIM // 2
    inv = ROPE_THETA ** (-jnp.arange(half, dtype=F32) / half)
    ang = pos.astype(F32)[:, None] * inv[None, :]
    cos = jnp.cos(ang)
    sin = jnp.sin(ang)
    reps = LANES // HEAD_DIM
    return (jnp.tile(jnp.concatenate([cos, cos], axis=1), (1, reps)),
            jnp.tile(jnp.concatenate([-sin, sin], axis=1), (1, reps)))


def _routing(top_e, n_tok):
    n_assign = n_tok * TOP_K
    e_flat = top_e.reshape(-1)
    order = jnp.argsort(e_flat)
    e_sorted = e_flat[order]
    tok_sorted = (order // TOP_K).astype(jnp.int32)
    counts = jnp.bincount(e_flat, length=N_EXPERTS)
    padded = (counts + MOE_ROWS - 1) // MOE_ROWS * MOE_ROWS
    pad_end = jnp.cumsum(padded)
    pad_start = pad_end - padded
    sort_start = jnp.cumsum(counts) - counts
    dest = (pad_start[e_sorted] + jnp.arange(n_assign, dtype=jnp.int32) - sort_start[e_sorted]).astype(jnp.int32)
    n_blocks = -(-n_assign // MOE_ROWS) + N_EXPERTS
    tok = jnp.full((n_blocks * MOE_ROWS,), n_tok, jnp.int32).at[dest].set(tok_sorted)
    blk_e = jnp.minimum(jnp.searchsorted(pad_end, jnp.arange(n_blocks, dtype=jnp.int32) * MOE_ROWS, side='right'),
                        N_EXPERTS - 1).astype(jnp.int32)
    pos = jnp.zeros((n_assign,), jnp.int32).at[order].set(dest)
    n_active = (pad_end[-1] // MOE_ROWS).astype(jnp.int32).reshape(1)
    return blk_e, n_active, tok, pos


def _head_dup(a):
    return jnp.concatenate([a, a], axis=-1)


def _head_pad(a):
    return jnp.concatenate([a, jnp.zeros_like(a)], axis=-1)


def kernel(x_prompt, x_sample, c_prompt, c_sample, cache_moba_kv, cache_nsa_kv, state_nsa_win_kv, page_table,
           norm_g, w_ada, b_ada, w_in, qk_gain, cmp_pos, cmp_w1, cmp_w2, w_out, w_router, b_router, w_gu, b_gu,
           w_down, b_down):
    bsz, seq, d = x_prompt.shape
    n_req = x_sample.shape[0]
    depth = norm_g.shape[0]
    assert depth == 1 and x_sample.shape[1] == 1
    assert seq % MOBA_BLOCK == 0 and seq // SEL_BLOCK <= LANES and seq // MOBA_BLOCK <= MOBA_BLOCK // SUBLANES
    n_pool, page = cache_moba_kv.shape[1], cache_moba_kv.shape[2]
    n_pages = page_table.shape[1]
    past = n_pages * page
    assert past % MOBA_BLOCK == 0 and 2 * page == MOBA_BLOCK and past // SEL_BLOCK < LANES
    assert state_nsa_win_kv.shape[2] == WINDOW
    layer = 0
    gw = G_NSA * HEAD_DIM
    n_prompt = bsz * seq

    bd = jnp.asarray(np.kron(np.eye(LANES // HEAD_DIM), np.full((HEAD_DIM, HEAD_DIM), 1.0 / HEAD_DIM)), BF16)
    w_in_bf = jnp.pad(w_in[layer], ((0, 0), (0, IN_COLS_PAD - IN_COLS))).astype(BF16)
    gains = jnp.tile(qk_gain[layer], (1, W_QM // HEAD_DIM))
    g1 = norm_g[layer, 0].reshape(1, d)
    g2 = norm_g[layer, 1].reshape(1, d)
    wo_bf = w_out[layer].astype(BF16)
    wr_pad = jnp.pad(w_router[layer], ((0, 0), (0, LANES - N_EXPERTS)))
    br_pad = jnp.pad(b_router[layer].reshape(1, N_EXPERTS), ((0, 0), (0, LANES - N_EXPERTS)),
                     constant_values=-jnp.inf)
    wgu_bf = w_gu[layer].astype(BF16)
    wd_bf = w_down[layer].astype(BF16)
    cmp_consts = _compress_consts(cmp_pos[layer], cmp_w1[layer], cmp_w2[layer], qk_gain[layer, 3])

    n_c = bsz + n_req
    n_c_pad = -(-n_c // SUBLANES) * SUBLANES
    c_all = jnp.pad(jnp.concatenate([c_prompt, c_sample], axis=0), ((0, n_c_pad - n_c), (0, 0)))
    mods = _ada(c_all, w_ada[layer], b_ada[layer])
    mods_p = [m.reshape(bsz, 1, d) for m in jnp.split(mods[:bsz], 6, axis=1)]
    mods_s = [m.reshape(1, n_req, d) for m in jnp.split(mods[bsz:n_c], 6, axis=1)]

    cos_p, sin_p = _rope_tables(jnp.arange(seq, dtype=jnp.int32))
    tm_p = 256
    moba_rows, nsa_rows, win_rows, q_m, qn, qr, gates, kmean = _proj(
        x_prompt.reshape(n_prompt, d), mods_p[0], mods_p[1], g1, w_in_bf, gains, bd, cos_p, sin_p,
        tm=tm_p, rows_per_mod=seq, pos_blocks=seq // tm_p, with_kmean=True)

    nbk = seq // MOBA_BLOCK
    pos = jnp.arange(seq, dtype=jnp.int32)
    k_m = moba_rows[:, :W_QM].reshape(bsz, seq, H_MOBA, HEAD_DIM).astype(BF16)
    oh_m = (pos[:, None] // MOBA_BLOCK == jnp.arange(MOBA_BLOCK // SUBLANES)[None, :]).astype(BF16)
    oh_m = jnp.broadcast_to(oh_m[None, :, None, :], (bsz, seq, H_MOBA, MOBA_BLOCK // SUBLANES))
    k_aug = jnp.concatenate([k_m, oh_m, jnp.zeros_like(oh_m)], axis=-1).reshape(bsz, seq, H_MOBA * LANES)
    v_m = moba_rows[:, W_QM:].reshape(bsz, seq, W_QM).astype(BF16)
    kmean_h = kmean.reshape(bsz, nbk, H_MOBA, HEAD_DIM).transpose(0, 2, 1, 3)
    kmp = jnp.zeros((bsz, H_MOBA, LANES, LANES), F32).at[:, :, HEAD_DIM:HEAD_DIM + nbk, :HEAD_DIM].set(kmean_h)
    o_m = _moba_attention(q_m.reshape(bsz, seq, W_QM), kmp, k_aug, v_m)

    nsa4 = nsa_rows.reshape(bsz, seq, 4, G_NSA, HEAD_DIM)
    win4 = win_rows.reshape(bsz, seq, 2, G_NSA, HEAD_DIM)
    kc, vc = _compress_prompt(nsa_rows.reshape(bsz, seq, 4 * gw), cmp_consts, bd)
    n_chunk = seq // CMP_STRIDE
    per_g = lambda a: a.reshape(bsz, -1, G_NSA, HEAD_DIM).transpose(0, 2, 1, 3)
    kcp = _head_pad(per_g(kc))
    vcd = _head_dup(per_g(vc)).astype(BF16)
    oh_s = (pos[:, None] // SEL_BLOCK == jnp.arange(LANES)[None, :]).astype(BF16)
    k_sel = per_g(nsa4[:, :, 2]).astype(BF16)
    ks_aug = jnp.concatenate([jnp.broadcast_to(oh_s[None, None], (bsz, G_NSA, seq, LANES)), k_sel,
                              jnp.zeros_like(k_sel)], axis=-1)
    vs_dup = _head_dup(per_g(nsa4[:, :, 3])).astype(BF16)
    kw_pad = _head_pad(per_g(win4[:, :, 0])).astype(BF16)
    vw_dup = _head_dup(per_g(win4[:, :, 1])).astype(BF16)
    gates_g = jnp.pad(gates[:, :N_GATE].reshape(bsz, seq, G_NSA, 3 * R_NSA).transpose(0, 2, 1, 3),
                      ((0, 0), (0, 0), (0, 0), (0, LANES - 3 * R_NSA)))
    cover_p = _cover(n_chunk - 1, seq // SEL_BLOCK, n_chunk)
    o_n = _nsa_attention(qn.reshape(bsz, seq, W_QN), qr.reshape(bsz, seq, W_QN), gates_g, kcp, vcd, cover_p,
                         ks_aug, vs_dup, kw_pad, vw_dup)
    o_p = jnp.concatenate([o_m, o_n], axis=-1).reshape(n_prompt, d)

    y1_p, h3_p, te_p, tw_p = _post(o_p, x_prompt.reshape(n_prompt, d), mods_p[2], mods_p[3], mods_p[4], g2, wo_bf,
                                   wr_pad, br_pad, tm=256, rows_per_mod=seq)

    cos_s, sin_s = _rope_tables(jnp.full((n_req,), past, jnp.int32))
    moba_new, nsa_new, win_new, q_m_s, qn_s, qr_s, gates_s = _proj(
        x_sample.reshape(n_req, d), mods_s[0], mods_s[1], g1, w_in_bf, gains, bd, cos_s, sin_s,
        tm=n_req, rows_per_mod=n_req, pos_blocks=1, with_kmean=False)
    pt_flat = page_table.reshape(-1).astype(jnp.int32)
    cache_m = cache_moba_kv[layer].reshape(n_pool, page, W_KVM)
    cache_n = cache_nsa_kv[layer].reshape(n_pool, page, 4 * gw)
    kc_s, vc_s = _compress_paged(cache_n, pt_flat, cmp_consts, bd, n_req=n_req, n_pages=n_pages)

    head_of_lane = jnp.arange(W_QM) // HEAD_DIM
    qbd = jnp.where(head_of_lane[None, None, :] == jnp.arange(H_MOBA)[None, :, None], q_m_s[:, None, :], 0.0)

    def rows8(q):
        qh = q.reshape(n_req, G_NSA, R_NSA, 1, HEAD_DIM)
        place = jnp.arange(G_NSA)[None, :, None, None, None] == jnp.arange(G_NSA)[None, None, None, :, None]
        return jnp.where(place, qh, 0.0).reshape(n_req, H_NSA, gw)

    g8 = jnp.pad(gates_s[:, :N_GATE].reshape(n_req, H_NSA, 3), ((0, 0), (0, 0), (0, LANES - 3)))
    newm = moba_new.reshape(n_req, 2, W_QM)
    news = nsa_new.reshape(n_req, 4, gw)[:, 2:4]
    neww = win_new.reshape(n_req, 2, gw)
    n_cmp_s = past // CMP_STRIDE
    cover_s = _cover(n_cmp_s - 1, past // SEL_BLOCK + 1, n_cmp_s)
    win_buf = state_nsa_win_kv[layer].reshape(n_req, WINDOW, 2 * gw)
    om_s, o8_s = _dec_attention(cache_m, cache_n, win_buf, pt_flat, qbd, rows8(qn_s), rows8(qr_s), g8, newm, news,
                                neww, kc_s, vc_s, cover_s, n_req=n_req, n_pages=n_pages)
    o8g = o8_s.reshape(n_req, G_NSA, R_NSA, G_NSA, HEAD_DIM)
    o_n_s = jnp.stack([o8g[:, g, :, g, :] for g in range(G_NSA)], axis=1).reshape(n_req, W_QN)
    o_s = jnp.concatenate([om_s.reshape(n_req, W_QM), o_n_s], axis=-1).astype(BF16)
    y1_s, h3_s, te_s, tw_s = _post(o_s, x_sample.reshape(n_req, d), mods_s[2], mods_s[3], mods_s[4], g2, wo_bf,
                                   wr_pad, br_pad, tm=n_req, rows_per_mod=n_req)

    n_tok = n_prompt + n_req
    h3 = jnp.concatenate([h3_p, h3_s, jnp.zeros((SUBLANES, LANES), F32)], axis=0)
    top_e = jnp.concatenate([te_p[:, :TOP_K], te_s[:, :TOP_K]], axis=0)
    blk_e, n_active, tok, pos_flat = _routing(top_e, n_tok)
    ys = _experts(blk_e, n_active, tok, h3, wgu_bf, b_gu[layer], wd_bf, b_down[layer])
    tm_c = 128
    y_p = _combine(pos_flat, ys, y1_p, mods_p[5], tw_p, tm=tm_c, rows_per_mod=seq, tile_off=0)
    y_s = _combine(pos_flat, ys, y1_s, mods_s[5], tw_s, tm=n_req, rows_per_mod=n_req, tile_off=n_prompt // n_req)

    keep = min(WINDOW, seq)
    win_p = win_rows.reshape(bsz, seq, 2, G_NSA, HEAD_DIM)[:, seq - keep:]
    win_s = jnp.concatenate([state_nsa_win_kv[layer][:, 1:], win_new.reshape(n_req, 1, 2, G_NSA, HEAD_DIM)], axis=1)
    return (y_p.reshape(bsz, seq, d), y_s.reshape(n_req, 1, d),
            moba_rows.reshape(1, bsz, seq, 2, H_MOBA, HEAD_DIM), nsa_rows.reshape(1, bsz, seq, 4, G_NSA, HEAD_DIM),
            win_p[None], moba_new.reshape(1, n_req, 1, 2, H_MOBA, HEAD_DIM),
            nsa_new.reshape(1, n_req, 1, 4, G_NSA, HEAD_DIM), win_s[None])
```

```python
import functools

import numpy as np
import jax
import jax.numpy as jnp
from jax import lax
from jax.experimental import pallas as pl
from jax.experimental.pallas import tpu as pltpu

F32 = jnp.float32
BF16 = jnp.bfloat16
HIGHEST = lax.Precision.HIGHEST

LANES = 128
SUBLANES = 8
HEAD_DIM = 64
H_MOBA = 8
H_NSA = 8
G_NSA = 2
R_NSA = H_NSA // G_NSA
MOBA_BLOCK = 256
MOBA_TOPK = 3
CMP_LEN = 32
CMP_STRIDE = 16
SEL_BLOCK = 64
SEL_TOPN = 16
WINDOW = 512
N_EXPERTS = 32
TOP_K = 4
SWIGLU_LIMIT = 7.0
SWIGLU_ALPHA = 1.702
ROPE_THETA = 10000.0
NORM_EPS = 1e-6
NEG_BIG = -1e30
TINY = 1e-30
SCALE = HEAD_DIM ** -0.5
Q_TILE = 128
MOBA_Q = MOBA_BLOCK
ATT_TK = 512
MOE_ROWS = 128
VMEM_LIMIT = 56 * 1024 * 1024

W_QM = H_MOBA * HEAD_DIM
W_KVM = 2 * H_MOBA * HEAD_DIM
W_QN = H_NSA * HEAD_DIM
W_KVN = 6 * G_NSA * HEAD_DIM
N_GATE = 3 * H_NSA
IN_COLS = W_QM + W_KVM + W_QN + W_KVN + N_GATE
IN_COLS_PAD = W_QM + W_KVM + W_QN + W_KVN + LANES


def _iota(shape, dim):
    return lax.broadcasted_iota(jnp.int32, shape, dim)


def _dot(a, b, precision=None):
    return jnp.dot(a, b, preferred_element_type=F32, precision=precision)


def _dot_nt(a, b, precision=None):
    return lax.dot_general(a, b, (((1,), (1,)), ((), ())), preferred_element_type=F32, precision=precision)


def _params(*sem):
    return pltpu.CompilerParams(dimension_semantics=sem, vmem_limit_bytes=VMEM_LIMIT)


def _seg_meansq(z, bd):
    zz = z * z
    hi = zz.astype(BF16)
    lo = (zz - hi.astype(F32)).astype(BF16)
    outs = []
    for c in range(z.shape[1] // LANES):
        sl = slice(c * LANES, (c + 1) * LANES)
        outs.append(_dot(hi[:, sl], bd) + _dot(lo[:, sl], bd))
    return outs[0] if len(outs) == 1 else jnp.concatenate(outs, axis=1)


def _head_norm(z, gain, bd):
    return z * lax.rsqrt(_seg_meansq(z, bd) + NORM_EPS) * gain


def _rope(z, cos, sin):
    outs = []
    first = (_iota((z.shape[0], LANES), 1) % HEAD_DIM) < (HEAD_DIM // 2)
    for c in range(z.shape[1] // LANES):
        x = z[:, c * LANES:(c + 1) * LANES]
        swapped = jnp.where(first, pltpu.roll(x, LANES - HEAD_DIM // 2, 1), pltpu.roll(x, HEAD_DIM // 2, 1))
        outs.append(x * cos + swapped * sin)
    return outs[0] if len(outs) == 1 else jnp.concatenate(outs, axis=1)


def _top_k_lanes(cur, k):
    lane = _iota(cur.shape, 1).astype(F32)
    picked = jnp.zeros(cur.shape, F32)
    vals, ids = [], []
    for _ in range(k):
        mx = jnp.max(cur, axis=1, keepdims=True)
        first = jnp.min(jnp.where(cur == mx, lane, 1e9), axis=1, keepdims=True)
        hit = lane == first
        picked = jnp.where(hit, 1.0, picked)
        cur = jnp.where(hit, -jnp.inf, cur)
        vals.append(mx)
        ids.append(first)
    return picked, vals, ids


def _flash_step(q, k, v, mask, m, l, acc):
    s = _dot_nt(q, k)
    if mask is not None:
        s = jnp.where(mask, s, NEG_BIG)
    m_new = jnp.maximum(m, jnp.max(s, axis=1, keepdims=True))
    alpha = jnp.exp(m - m_new)
    p = jnp.exp(s - m_new)
    l_new = alpha * l + jnp.sum(p, axis=1, keepdims=True)
    acc_new = alpha * acc + _dot(p.astype(BF16), v)
    return m_new, l_new, acc_new


def _ada_kernel(c_ref, w_ref, b_ref, o_ref):
    c = c_ref[...]
    o_ref[...] = _dot(c * jax.nn.sigmoid(c), w_ref[...], HIGHEST) + b_ref[...]


def _ada(c_all, w_ada, b_ada):
    n, d = c_all.shape
    cols = w_ada.shape[1]
    tn = 1024
    return pl.pallas_call(
        _ada_kernel,
        grid=(cols // tn,),
        in_specs=[pl.BlockSpec((n, d), lambda j: (0, 0)),
                  pl.BlockSpec((d, tn), lambda j: (0, j)),
                  pl.BlockSpec((1, tn), lambda j: (0, j))],
        out_specs=pl.BlockSpec((n, tn), lambda j: (0, j)),
        out_shape=jax.ShapeDtypeStruct((n, cols), F32),
        compiler_params=_params("arbitrary"),
        name="ada",
    )(c_all, w_ada, b_ada.reshape(1, cols))


def _proj_kernel(x_ref, sh_ref, sc_ref, g_ref, w_ref, gains_ref, bd_ref, cos_ref, sin_ref,
                 moba_ref, nsa_ref, win_ref, qm_ref, qn_ref, qr_ref, gate_ref, *maybe_kmean, with_kmean):
    x = x_ref[...]
    y = x * lax.rsqrt(jnp.mean(x * x, axis=1, keepdims=True) + NORM_EPS) * g_ref[...]
    h = (y * (1.0 + sc_ref[0]) + sh_ref[0]).astype(BF16)
    bd = bd_ref[...]
    cos = cos_ref[...]
    sin = sin_ref[...]
    o = 0

    def seg(width):
        nonlocal o
        z = _dot(h, w_ref[:, o:o + width])
        o += width
        return z

    def gain(i, width):
        return gains_ref[i:i + 1, 0:width]

    qm_ref[...] = _rope(_head_norm(seg(W_QM), gain(0, W_QM), bd), cos, sin)
    k_m = _rope(_head_norm(seg(W_QM), gain(1, W_QM), bd), cos, sin)
    moba_ref[:, 0:W_QM] = k_m
    moba_ref[:, W_QM:2 * W_QM] = seg(W_QM)
    qn = _head_norm(seg(W_QN), gain(2, W_QN), bd)
    qn_ref[...] = qn
    qr_ref[...] = _rope(qn, cos, sin)
    gw = G_NSA * HEAD_DIM
    nsa_ref[:, 0:2 * gw] = seg(2 * gw)
    nsa_ref[:, 2 * gw:3 * gw] = _rope(_head_norm(seg(gw), gain(4, gw), bd), cos, sin)
    nsa_ref[:, 3 * gw:4 * gw] = seg(gw)
    win_ref[:, 0:gw] = _rope(_head_norm(seg(gw), gain(5, gw), bd), cos, sin)
    win_ref[:, gw:2 * gw] = seg(gw)
    gate_ref[...] = jax.nn.sigmoid(seg(LANES))
    if with_kmean:
        (kmean_ref,) = maybe_kmean
        tm = k_m.shape[0]
        kmean_ref[0] = jnp.mean(k_m.reshape(tm // MOBA_BLOCK, MOBA_BLOCK, W_QM), axis=1)


def _proj(x, shift, scale, g, w_in_bf, gains, bd, cos, sin, *, tm, rows_per_mod, pos_blocks, with_kmean):
    n, d = x.shape
    nt = n // tm
    mod_r = shift.shape[1]
    mod_map = lambda i: (i // (rows_per_mod // tm), 0, 0)
    pos_map = lambda i: (i % pos_blocks, 0)
    row = lambda w: pl.BlockSpec((tm, w), lambda i: (i, 0))
    const = lambda a: pl.BlockSpec(a.shape, lambda i: (0,) * a.ndim)
    out_shapes = [jax.ShapeDtypeStruct((n, W_KVM), F32), jax.ShapeDtypeStruct((n, 4 * G_NSA * HEAD_DIM), F32),
                  jax.ShapeDtypeStruct((n, 2 * G_NSA * HEAD_DIM), F32), jax.ShapeDtypeStruct((n, W_QM), F32),
                  jax.ShapeDtypeStruct((n, W_QN), F32), jax.ShapeDtypeStruct((n, W_QN), F32),
                  jax.ShapeDtypeStruct((n, LANES), F32)]
    out_specs = [row(W_KVM), row(4 * G_NSA * HEAD_DIM), row(2 * G_NSA * HEAD_DIM), row(W_QM), row(W_QN), row(W_QN),
                 row(LANES)]
    if with_kmean:
        nbt = tm // MOBA_BLOCK
        out_shapes.append(jax.ShapeDtypeStruct((nt, nbt, W_QM), F32))
        out_specs.append(pl.BlockSpec((1, nbt, W_QM), lambda i: (i, 0, 0)))
    return pl.pallas_call(
        functools.partial(_proj_kernel, with_kmean=with_kmean),
        grid=(nt,),
        in_specs=[row(d), pl.BlockSpec((1, mod_r, d), mod_map), pl.BlockSpec((1, mod_r, d), mod_map),
                  const(g), const(w_in_bf), const(gains), const(bd),
                  pl.BlockSpec((tm, LANES), pos_map), pl.BlockSpec((tm, LANES), pos_map)],
        out_specs=out_specs,
        out_shape=out_shapes,
        compiler_params=_params("parallel"),
        name="proj",
    )(x, shift, scale, g, w_in_bf, gains, bd, cos, sin)


def _compress_compute(src_refs, pe_ref, w1_ref, w2_ref, gk_ref, bd_ref, kc_ref, vc_ref, n_rows):
    n_chunk = n_rows // CMP_STRIDE
    gw = G_NSA * HEAD_DIM
    for kv in range(2):
        acc_a = jnp.zeros((n_chunk, gw), F32)
        acc_b = jnp.zeros((n_chunk, gw), F32)
        for j in range(CMP_STRIDE):
            xj = src_refs[kv][pl.ds(j, n_chunk, stride=CMP_STRIDE), :]
            acc_a = acc_a + _dot(xj + pe_ref[kv, 0, j:j + 1, :], w1_ref[kv, 0, j], HIGHEST)
            acc_b = acc_b + _dot(xj + pe_ref[kv, 1, j:j + 1, :], w1_ref[kv, 1, j], HIGHEST)
        hid = jax.nn.gelu(acc_a + pltpu.roll(acc_b, n_chunk - 1, 0))
        out = _dot(hid, w2_ref[kv], HIGHEST)
        if kv == 0:
            kc_ref[0] = _head_norm(out, gk_ref[...], bd_ref[...])
        else:
            vc_ref[0] = out


def _compress_prompt_kernel(k_ref, v_ref, pe_ref, w1_ref, w2_ref, gk_ref, bd_ref, kc_ref, vc_ref, *, n_rows):
    _compress_compute((k_ref.at[0], v_ref.at[0]), pe_ref, w1_ref, w2_ref, gk_ref, bd_ref, kc_ref, vc_ref, n_rows)


def _compress_paged_kernel(pt_ref, page_ref, pe_ref, w1_ref, w2_ref, gk_ref, bd_ref, kc_ref, vc_ref, kbuf, vbuf,
                           *, n_rows, page):
    p = pl.program_id(1)
    gw = G_NSA * HEAD_DIM
    rows = pl.ds(pl.multiple_of(p * page, page), page)
    kbuf[rows, :] = page_ref[0, :, 0:gw]
    vbuf[rows, :] = page_ref[0, :, gw:2 * gw]

    @pl.when(p == pl.num_programs(1) - 1)
    def _():
        _compress_compute((kbuf, vbuf), pe_ref, w1_ref, w2_ref, gk_ref, bd_ref, kc_ref, vc_ref, n_rows)


def _compress_consts(cmp_pos, cmp_w1, cmp_w2, gain_k_cmp):
    pe = jnp.tile(cmp_pos.reshape(2, 2, CMP_STRIDE, HEAD_DIM), (1, 1, 1, G_NSA))
    eye = jnp.eye(G_NSA, dtype=F32)
    w1 = cmp_w1.reshape(2, 2, CMP_STRIDE, HEAD_DIM, HEAD_DIM)
    w1bd = jnp.einsum("gh,kajde->kajgdhe", eye, w1).reshape(2, 2, CMP_STRIDE, G_NSA * HEAD_DIM, G_NSA * HEAD_DIM)
    w2bd = jnp.einsum("gh,kde->kgdhe", eye, cmp_w2).reshape(2, G_NSA * HEAD_DIM, G_NSA * HEAD_DIM)
    gk = jnp.tile(gain_k_cmp.reshape(1, HEAD_DIM), (1, G_NSA))
    return pe, w1bd, w2bd, gk


def _compress_prompt(nsa_rows, consts, bd):
    b, s, _ = nsa_rows.shape
    pe, w1bd, w2bd, gk = consts
    n_chunk = s // CMP_STRIDE
    gw = G_NSA * HEAD_DIM
    const = lambda a: pl.BlockSpec(a.shape, lambda i: (0,) * a.ndim)
    out = jax.ShapeDtypeStruct((b, n_chunk, gw), F32)
    return pl.pallas_call(
        functools.partial(_compress_prompt_kernel, n_rows=s),
        grid=(b,),
        in_specs=[pl.BlockSpec((1, s, gw), lambda i: (i, 0, 0)), pl.BlockSpec((1, s, gw), lambda i: (i, 0, 1)),
                  const(pe), const(w1bd), const(w2bd), const(gk), const(bd)],
        out_specs=[pl.BlockSpec((1, n_chunk, gw), lambda i: (i, 0, 0))] * 2,
        out_shape=[out, out],
        compiler_params=_params("parallel"),
        name="compress_prompt",
    )(nsa_rows, nsa_rows, pe, w1bd, w2bd, gk, bd)


def _compress_paged(cache_nsa, pt_flat, consts, bd, *, n_req, n_pages):
    _, page, _ = cache_nsa.shape
    pe, w1bd, w2bd, gk = consts
    n_rows = n_pages * page
    n_chunk = n_rows // CMP_STRIDE
    gw = G_NSA * HEAD_DIM
    const = lambda a: pl.BlockSpec(a.shape, lambda b, p, pt: (0,) * a.ndim)
    out = jax.ShapeDtypeStruct((n_req, n_chunk, gw), F32)
    return pl.pallas_call(
        functools.partial(_compress_paged_kernel, n_rows=n_rows, page=page),
        grid_spec=pltpu.PrefetchScalarGridSpec(
            num_scalar_prefetch=1,
            grid=(n_req, n_pages),
            in_specs=[pl.BlockSpec((1, page, 2 * gw), lambda b, p, pt: (pt[b * n_pages + p], 0, 0)),
                      const(pe), const(w1bd), const(w2bd), const(gk), const(bd)],
            out_specs=[pl.BlockSpec((1, n_chunk, gw), lambda b, p, pt: (b, 0, 0))] * 2,
            scratch_shapes=[pltpu.VMEM((n_rows, gw), F32)] * 2),
        out_shape=[out, out],
        compiler_params=_params("parallel", "arbitrary"),
        name="compress_paged",
    )(pt_flat, cache_nsa, pe, w1bd, w2bd, gk, bd)


def _moba_kernel(q_ref, kmp_ref, k_ref, v_ref, o_ref):
    qi = pl.program_id(2)
    own = qi
    q2 = q_ref[0]
    lane = _iota((MOBA_Q, LANES), 1)
    blk = lane - HEAD_DIM
    valid = (blk >= 0) & (blk < own)
    q_augs = []
    for h in range(2):
        qh = q2 if h == 0 else pltpu.roll(q2, HEAD_DIM, 1)
        q0 = jnp.where(lane < HEAD_DIM, qh, 0.0)
        score = _dot_nt(q0, kmp_ref[0, h], HIGHEST)
        picked, _, _ = _top_k_lanes(jnp.where(valid, score, -jnp.inf), MOBA_TOPK)
        sel = ((picked > 0.5) & valid) | (blk == own)
        bias = jnp.where(sel, 0.0, NEG_BIG)
        q_augs.append(jnp.where(lane < HEAD_DIM, q0 * SCALE,
                                jnp.where(lane < HEAD_DIM + MOBA_BLOCK // SUBLANES, bias, 0.0)).astype(BF16))
    pos = qi * MOBA_Q + _iota((MOBA_Q, ATT_TK), 0)

    def tile(j, carry, masked):
        start = pl.multiple_of(j * ATT_TK, ATT_TK)
        v = v_ref[0, pl.ds(start, ATT_TK), :]
        mask = None
        if masked:
            mask = (j * ATT_TK + _iota((MOBA_Q, ATT_TK), 1)) <= pos
        return tuple(_flash_step(q_augs[h], k_ref[0, pl.ds(start, ATT_TK), h * LANES:(h + 1) * LANES], v, mask,
                                 *carry[h]) for h in range(2))

    init = (jnp.full((MOBA_Q, 1), NEG_BIG, F32), jnp.zeros((MOBA_Q, 1), F32), jnp.zeros((MOBA_Q, LANES), F32))
    jd = (own * MOBA_BLOCK) // ATT_TK
    carry = lax.fori_loop(0, jd, lambda j, c: tile(j, c, False), tile(jd, (init, init), True))
    outs = [acc / jnp.maximum(l, TINY) for (_, l, acc) in carry]
    o_ref[0] = jnp.where(lane < HEAD_DIM, outs[0], outs[1]).astype(o_ref.dtype)


def _moba_attention(q_m, kmp, k_aug, v_m):
    b, s, _ = q_m.shape
    return pl.pallas_call(
        _moba_kernel,
        grid=(b, H_MOBA // 2, s // MOBA_Q),
        in_specs=[pl.BlockSpec((1, MOBA_Q, LANES), lambda b, h, i: (b, i, h)),
                  pl.BlockSpec((1, 2, LANES, LANES), lambda b, h, i: (b, h, 0, 0)),
                  pl.BlockSpec((1, s, 2 * LANES), lambda b, h, i: (b, 0, h)),
                  pl.BlockSpec((1, s, LANES), lambda b, h, i: (b, 0, h))],
        out_specs=pl.BlockSpec((1, MOBA_Q, LANES), lambda b, h, i: (b, i, h)),
        out_shape=jax.ShapeDtypeStruct((b, s, W_QM), BF16),
        compiler_params=_params("parallel", "parallel", "arbitrary"),
        name="moba_attention",
    )(q_m, kmp, k_aug, v_m)


def _stack_heads(q4):
    lane = _iota((Q_TILE, LANES), 1)
    parts = []
    for r in range(R_NSA):
        c = q4[:, (r // 2) * LANES:(r // 2 + 1) * LANES]
        if r % 2:
            c = pltpu.roll(c, HEAD_DIM, 1)
        parts.append(jnp.where(lane < HEAD_DIM, c, 0.0))
    return jnp.concatenate(parts, axis=0)


def _nsa_kernel(qn_ref, qr_ref, gate_ref, kc_ref, vc_ref, cover_ref, ks_ref, vs_ref, kw_ref, vw_ref, o_ref,
                *, n_chunk):
    qi = pl.program_id(2)
    rows = R_NSA * Q_TILE
    qloc = _iota((rows, 1), 0) % Q_TILE
    pos = qi * Q_TILE + qloc

    qn = _stack_heads(qn_ref[0])
    s = _dot_nt(qn, kc_ref[0, 0], HIGHEST) * SCALE
    cmask = (_iota((rows, n_chunk), 1) * CMP_STRIDE + (CMP_LEN - 1)) <= pos
    s = jnp.where(cmask, s, NEG_BIG)
    e = jnp.where(cmask, jnp.exp(s - jnp.max(s, axis=1, keepdims=True)), 0.0)
    p_cmp = e / jnp.maximum(jnp.sum(e, axis=1, keepdims=True), TINY)
    o_cmp = _dot(p_cmp.astype(BF16), vc_ref[0, 0])

    p_grp = p_cmp[0:Q_TILE]
    for r in range(1, R_NSA):
        p_grp = p_grp + p_cmp[r * Q_TILE:(r + 1) * Q_TILE]
    imp = _dot(p_grp, cover_ref[...], HIGHEST)
    blk = _iota((Q_TILE, LANES), 1)
    own = (qi * Q_TILE + _iota((Q_TILE, 1), 0)) // SEL_BLOCK
    forced = (blk == 0) | (blk == own) | (blk == own - 1)
    score = jnp.where(blk <= own, jnp.where(forced, jnp.inf, imp), -jnp.inf)
    picked, _, _ = _top_k_lanes(score, SEL_TOPN)
    bias = jnp.where((picked > 0.5) & (blk <= own), 0.0, NEG_BIG)

    qr = (_stack_heads(qr_ref[0]) * SCALE).astype(BF16)
    q_aug = jnp.concatenate([jnp.concatenate([bias] * R_NSA, axis=0).astype(BF16), qr], axis=1)
    init = (jnp.full((rows, 1), NEG_BIG, F32), jnp.zeros((rows, 1), F32), jnp.zeros((rows, LANES), F32))

    tk = ATT_TK
    jd = (qi * Q_TILE) // tk

    def sel_tile(j, carry, masked):
        start = pl.multiple_of(j * tk, tk)
        mask = None
        if masked:
            mask = (j * tk + _iota((rows, tk), 1)) <= pos
        return _flash_step(q_aug, ks_ref[0, 0, pl.ds(start, tk), :], vs_ref[0, 0, pl.ds(start, tk), :], mask, *carry)

    m, l, acc = lax.fori_loop(0, jd, lambda j, c: sel_tile(j, c, False), sel_tile(jd, init, True))
    o_sel = acc / jnp.maximum(l, TINY)

    kloc = _iota((rows, Q_TILE), 1)

    def win_tile(kt, mask, carry):
        start = pl.multiple_of(jnp.maximum(kt, 0) * Q_TILE, Q_TILE)
        return _flash_step(qr, kw_ref[0, 0, pl.ds(start, Q_TILE), :], vw_ref[0, 0, pl.ds(start, Q_TILE), :], mask,
                           *carry)

    carry = win_tile(qi, kloc <= qloc, init)
    n_back = WINDOW // Q_TILE
    for t in range(1, n_back + 1):
        mask = ((qi - t) * Q_TILE + kloc) >= 0
        if t == n_back:
            mask = mask & (kloc > qloc)
        carry = win_tile(qi - t, mask, carry)
    m, l, acc = carry
    o_win = acc / jnp.maximum(l, TINY)

    gates = gate_ref[0, 0]
    heads = []
    for r in range(R_NSA):
        rs = slice(r * Q_TILE, (r + 1) * Q_TILE)
        heads.append(gates[:, 3 * r:3 * r + 1] * o_cmp[rs] + gates[:, 3 * r + 1:3 * r + 2] * o_sel[rs]
                     + gates[:, 3 * r + 2:3 * r + 3] * o_win[rs])
    lo = blk < HEAD_DIM
    o_ref[0] = jnp.concatenate([jnp.where(lo, heads[0], heads[1]), jnp.where(lo, heads[2], heads[3])],
                               axis=1).astype(o_ref.dtype)


def _nsa_attention(qn, qr, gates_g, kcp, vcd, cover, ks_aug, vs_dup, kw_pad, vw_dup):
    b, s, _ = qn.shape
    n_chunk = kcp.shape[2]
    gq = R_NSA * HEAD_DIM
    per_g = lambda w: pl.BlockSpec((1, 1, s, w), lambda b, g, i: (b, g, 0, 0))
    return pl.pallas_call(
        functools.partial(_nsa_kernel, n_chunk=n_chunk),
        grid=(b, G_NSA, s // Q_TILE),
        in_specs=[pl.BlockSpec((1, Q_TILE, gq), lambda b, g, i: (b, i, g)),
                  pl.BlockSpec((1, Q_TILE, gq), lambda b, g, i: (b, i, g)),
                  pl.BlockSpec((1, 1, Q_TILE, LANES), lambda b, g, i: (b, g, i, 0)),
                  pl.BlockSpec((1, 1, n_chunk, LANES), lambda b, g, i: (b, g, 0, 0)),
                  pl.BlockSpec((1, 1, n_chunk, LANES), lambda b, g, i: (b, g, 0, 0)),
                  pl.BlockSpec(cover.shape, lambda b, g, i: (0, 0)),
                  per_g(2 * LANES), per_g(LANES), per_g(LANES), per_g(LANES)],
        out_specs=pl.BlockSpec((1, Q_TILE, gq), lambda b, g, i: (b, i, g)),
        out_shape=jax.ShapeDtypeStruct((b, s, W_QN), BF16),
        compiler_params=_params("parallel", "parallel", "arbitrary"),
        name="nsa_attention",
    )(qn, qr, gates_g, kcp, vcd, cover, ks_aug, vs_dup, kw_pad, vw_dup)


def _cover(n_cmp, n_sel, rows):
    c0 = np.arange(rows)[:, None] * CMP_STRIDE
    b0 = np.arange(LANES)[None, :] * SEL_BLOCK
    ok = (c0 < b0 + SEL_BLOCK) & (c0 + CMP_LEN > b0) & (np.arange(rows)[:, None] < n_cmp) & (np.arange(LANES)[None, :] < n_sel)
    return jnp.asarray(ok.astype(np.float32))


def _dec_kernel(pt_ref, ma_ref, mb_ref, na_ref, nb_ref, wb_ref, qbd_ref, q8n_ref, q8r_ref, g8_ref, newm_ref,
                news_ref, neww_ref, kc_ref, vc_ref, cover_ref, om_ref, o8_ref,
                m_sc, l_sc, s_sc, acc_sc, msel_sc, lsel_sc, asel_sc, bias_sc, ocmp_sc, *, n_steps, page):
    j = pl.program_id(1)
    sub8 = _iota((SUBLANES, LANES), 0)
    lane8 = _iota((SUBLANES, LANES), 1)
    tk = 2 * page
    q8r = q8r_ref[0] * SCALE
    q8r_bf = q8r.astype(BF16)

    @pl.when(j == 0)
    def _():
        n_chunk = kc_ref.shape[1]
        s = _dot_nt(q8n_ref[0], kc_ref[0], HIGHEST) * SCALE
        cmask = _iota((SUBLANES, n_chunk), 1) < (n_chunk - 1)
        s = jnp.where(cmask, s, NEG_BIG)
        e = jnp.where(cmask, jnp.exp(s - jnp.max(s, axis=1, keepdims=True)), 0.0)
        p_cmp = e / jnp.maximum(jnp.sum(e, axis=1, keepdims=True), TINY)
        ocmp_sc[...] = _dot(p_cmp.astype(BF16), vc_ref[0].astype(BF16))
        subc = _iota((SUBLANES, n_chunk), 0)
        g0 = jnp.sum(jnp.where(subc < R_NSA, p_cmp, 0.0), axis=0, keepdims=True)
        g1 = jnp.sum(jnp.where(subc >= R_NSA, p_cmp, 0.0), axis=0, keepdims=True)
        imp = _dot(jnp.where(subc < R_NSA, g0, g1), cover_ref[...], HIGHEST)
        own = (n_steps * tk) // SEL_BLOCK
        forced = (lane8 == 0) | (lane8 == own) | (lane8 == own - 1)
        score = jnp.where(lane8 <= own, jnp.where(forced, jnp.inf, imp), -jnp.inf)
        picked, _, _ = _top_k_lanes(score, SEL_TOPN)
        bias_sc[...] = jnp.where((picked > 0.5) & (lane8 <= own), 0.0, NEG_BIG)
        msel_sc[...] = jnp.full((SUBLANES, LANES), NEG_BIG, F32)
        lsel_sc[...] = jnp.zeros((SUBLANES, LANES), F32)
        asel_sc[...] = jnp.zeros((SUBLANES, LANES), F32)

    hw = H_MOBA * HEAD_DIM
    k_m = jnp.concatenate([ma_ref[0, :, 0:hw], mb_ref[0, :, 0:hw]], axis=0)
    v_m = jnp.concatenate([ma_ref[0, :, hw:2 * hw], mb_ref[0, :, hw:2 * hw]], axis=0)
    qbd = qbd_ref[0]
    s = _dot_nt((qbd * SCALE).astype(BF16), k_m.astype(BF16))
    mj = jnp.max(s, axis=1, keepdims=True)
    p = jnp.exp(s - mj)
    m_sc[j] = jnp.broadcast_to(mj, (SUBLANES, LANES))
    l_sc[j] = jnp.broadcast_to(jnp.sum(p, axis=1, keepdims=True), (SUBLANES, LANES))
    acc_sc[j] = _dot(p.astype(BF16), v_m.astype(BF16))
    kmean = jnp.sum(k_m, axis=0, keepdims=True) * (1.0 / MOBA_BLOCK)
    s_sc[j] = jnp.broadcast_to(jnp.sum(qbd * kmean, axis=1, keepdims=True), (SUBLANES, LANES))

    gw = G_NSA * HEAD_DIM
    k_s = jnp.concatenate([na_ref[0, :, 0:gw], nb_ref[0, :, 0:gw]], axis=0).astype(BF16)
    v_s = jnp.concatenate([na_ref[0, :, gw:2 * gw], nb_ref[0, :, gw:2 * gw]], axis=0).astype(BF16)
    s = _dot_nt(q8r_bf, k_s)
    bias = bias_sc[...]
    kb = _iota((SUBLANES, tk), 1) // SEL_BLOCK
    bias_k = jnp.zeros((SUBLANES, tk), F32)
    for q in range(tk // SEL_BLOCK):
        bq = jnp.sum(jnp.where(lane8 == j * (tk // SEL_BLOCK) + q, bias, 0.0), axis=1, keepdims=True)
        bias_k = jnp.where(kb == q, bq, bias_k)
    s = s + bias_k
    m_old = msel_sc[:, 0:1]
    m_new = jnp.maximum(m_old, jnp.max(s, axis=1, keepdims=True))
    alpha = jnp.exp(m_old - m_new)
    p = jnp.exp(s - m_new)
    msel_sc[...] = jnp.broadcast_to(m_new, (SUBLANES, LANES))
    lsel_sc[...] = alpha * lsel_sc[...] + jnp.sum(p, axis=1, keepdims=True)
    asel_sc[...] = alpha * asel_sc[...] + _dot(p.astype(BF16), v_s)

    @pl.when(j == n_steps - 1)
    def _():
        news = news_ref[0]
        s_new = jnp.sum(q8r * news[0:1], axis=1, keepdims=True)
        m_old = msel_sc[:, 0:1]
        m_new = jnp.maximum(m_old, s_new)
        alpha = jnp.exp(m_old - m_new)
        pn = jnp.exp(s_new - m_new)
        l = alpha * lsel_sc[...] + pn
        o_sel = (alpha * asel_sc[...] + pn * news[1:2]) / jnp.maximum(l, TINY)

        neww = neww_ref[0]
        kw = wb_ref[0, :, 0:gw].astype(BF16)
        vw = wb_ref[0, :, gw:2 * gw].astype(BF16)
        nw = kw.shape[0]
        s = _dot_nt(q8r_bf, kw)
        wmask = _iota((SUBLANES, nw), 1) >= 1
        s = jnp.where(wmask, s, NEG_BIG)
        s_new = jnp.sum(q8r * neww[0:1], axis=1, keepdims=True)
        mw = jnp.maximum(jnp.max(s, axis=1, keepdims=True), s_new)
        e = jnp.where(wmask, jnp.exp(s - mw), 0.0)
        en = jnp.exp(s_new - mw)
        lw = jnp.sum(e, axis=1, keepdims=True) + en
        o_win = (_dot(e.astype(BF16), vw) + en * neww[1:2]) / jnp.maximum(lw, TINY)

        g8 = g8_ref[0]
        o8_ref[0] = g8[:, 0:1] * ocmp_sc[...] + g8[:, 1:2] * o_sel + g8[:, 2:3] * o_win

        newm = newm_ref[0]
        scores = [s_sc[b] for b in range(n_steps)]
        m_tot = jnp.broadcast_to(jnp.sum(qbd * newm[0:1], axis=1, keepdims=True) * SCALE, (SUBLANES, LANES))
        s_own = m_tot
        sels = []
        for b in range(n_steps):
            rank = jnp.zeros((SUBLANES, LANES), F32)
            for c in range(n_steps):
                if c == b:
                    continue
                ahead = (scores[c] > scores[b]) | ((scores[c] == scores[b]) & (c < b))
                rank = rank + jnp.where(ahead, 1.0, 0.0)
            sels.append(rank < MOBA_TOPK)
            m_tot = jnp.where(sels[b], jnp.maximum(m_tot, m_sc[b]), m_tot)
        w_own = jnp.exp(s_own - m_tot)
        l_tot = w_own
        acc = w_own[:, 0:1] * newm[1:2]
        for b in range(n_steps):
            wgt = jnp.where(sels[b], jnp.exp(m_sc[b] - m_tot), 0.0)
            l_tot = l_tot + wgt * l_sc[b]
            acc = acc + wgt[:, 0:1] * acc_sc[b]
        o_full = acc / jnp.maximum(l_tot[:, 0:1], TINY)
        own_head = (_iota((SUBLANES, hw), 1) // HEAD_DIM) == _iota((SUBLANES, hw), 0)
        om_ref[0] = jnp.sum(jnp.where(own_head, o_full, 0.0), axis=0, keepdims=True)


def _dec_attention(cache_moba, cache_nsa, win_buf, pt_flat, qbd, q8n, q8r, g8, newm, news, neww, kc, vc, cover,
                   *, n_req, n_pages):
    _, page, mw = cache_moba.shape
    n_steps = n_pages // 2
    gw = G_NSA * HEAD_DIM
    hw = H_MOBA * HEAD_DIM
    n_chunk = kc.shape[1]
    nwin = win_buf.shape[1]
    req = lambda a: pl.BlockSpec((1,) + a.shape[1:], lambda b, j, pt: (b,) + (0,) * (a.ndim - 1))
    return pl.pallas_call(
        functools.partial(_dec_kernel, n_steps=n_steps, page=page),
        grid_spec=pltpu.PrefetchScalarGridSpec(
            num_scalar_prefetch=1,
            grid=(n_req, n_steps),
            in_specs=[pl.BlockSpec((1, page, mw), lambda b, j, pt: (pt[b * n_pages + 2 * j], 0, 0)),
                      pl.BlockSpec((1, page, mw), lambda b, j, pt: (pt[b * n_pages + 2 * j + 1], 0, 0)),
                      pl.BlockSpec((1, page, 2 * gw), lambda b, j, pt: (pt[b * n_pages + 2 * j], 0, 1)),
                      pl.BlockSpec((1, page, 2 * gw), lambda b, j, pt: (pt[b * n_pages + 2 * j + 1], 0, 1)),
                      req(win_buf), req(qbd), req(q8n), req(q8r), req(g8), req(newm), req(news), req(neww),
                      req(kc), req(vc), pl.BlockSpec(cover.shape, lambda b, j, pt: (0, 0))],
            out_specs=[pl.BlockSpec((1, 1, hw), lambda b, j, pt: (b, 0, 0)),
                       pl.BlockSpec((1, SUBLANES, LANES), lambda b, j, pt: (b, 0, 0))],
            scratch_shapes=[pltpu.VMEM((n_steps, SUBLANES, LANES), F32)] * 3
            + [pltpu.VMEM((n_steps, SUBLANES, hw), F32)]
            + [pltpu.VMEM((SUBLANES, LANES), F32)] * 5),
        out_shape=[jax.ShapeDtypeStruct((n_req, 1, hw), F32), jax.ShapeDtypeStruct((n_req, SUBLANES, LANES), F32)],
        compiler_params=_params("parallel", "arbitrary"),
        name="dec_attention",
    )(pt_flat, cache_moba, cache_moba, cache_nsa, cache_nsa, win_buf, qbd, q8n, q8r, g8, newm, news, neww, kc, vc,
      cover)


def _post_kernel(o_ref, x_ref, gt_ref, sh_ref, sc_ref, g_ref, wo_ref, wr_ref, br_ref,
                 y_ref, h3_ref, te_ref, tw_ref):
    y = x_ref[...] + gt_ref[0] * _dot(o_ref[...], wo_ref[...])
    y_ref[...] = y
    h = y * lax.rsqrt(jnp.mean(y * y, axis=1, keepdims=True) + NORM_EPS) * g_ref[...]
    h = h * (1.0 + sc_ref[0]) + sh_ref[0]
    tm = h.shape[0]
    for s in range(h.shape[1] // LANES):
        h3_ref[pl.ds(s, tm, stride=SUBLANES), :] = h[:, s * LANES:(s + 1) * LANES]
    logits = _dot(h, wr_ref[...], HIGHEST) + br_ref[...]
    _, vals, ids = _top_k_lanes(logits, TOP_K)
    lane = _iota((tm, LANES), 1)
    es = [jnp.exp(v - vals[0]) for v in vals]
    den = es[0]
    for e in es[1:]:
        den = den + e
    te = jnp.zeros((tm, LANES), F32)
    tw = jnp.zeros((tm, LANES), F32)
    for k in range(TOP_K):
        te = jnp.where(lane == k, ids[k], te)
        tw = jnp.where(lane == k, es[k] / den, tw)
    te_ref[...] = te.astype(jnp.int32)
    tw_ref[...] = tw


def _post(o, x, gate, shift, scale, g2, wo_bf, wr_pad, br_pad, *, tm, rows_per_mod):
    n, d = x.shape
    mod_r = gate.shape[1]
    mod_map = lambda i: (i // (rows_per_mod // tm), 0, 0)
    row = lambda w: pl.BlockSpec((tm, w), lambda i: (i, 0))
    const = lambda a: pl.BlockSpec(a.shape, lambda i: (0,) * a.ndim)
    mod = pl.BlockSpec((1, mod_r, d), mod_map)
    return pl.pallas_call(
        _post_kernel,
        grid=(n // tm,),
        in_specs=[row(d), row(d), mod, mod, mod, const(g2), const(wo_bf), const(wr_pad), const(br_pad)],
        out_specs=[row(d), pl.BlockSpec((tm * SUBLANES, LANES), lambda i: (i, 0)), row(LANES), row(LANES)],
        out_shape=[jax.ShapeDtypeStruct((n, d), F32), jax.ShapeDtypeStruct((n * SUBLANES, LANES), F32),
                   jax.ShapeDtypeStruct((n, LANES), jnp.int32), jax.ShapeDtypeStruct((n, LANES), F32)],
        compiler_params=_params("parallel"),
        name="post",
    )(o, x, gate, shift, scale, g2, wo_bf, wr_pad, br_pad)


def _expert_kernel(be_ref, na_ref, tok_ref, h3_ref, wgu_ref, bgu_ref, wd_ref, bd_ref, y_ref, buf, xb, sem):
    i = pl.program_id(0)
    n_active = na_ref[0]
    rows8 = MOE_ROWS * SUBLANES

    def gather(blk, slot):
        def body(r, carry):
            t = tok_ref[blk * MOE_ROWS + r]
            pltpu.make_async_copy(h3_ref.at[pl.ds(pl.multiple_of(t * SUBLANES, SUBLANES), SUBLANES), :],
                                  buf.at[slot, pl.ds(pl.multiple_of(r * SUBLANES, SUBLANES), SUBLANES), :],
                                  sem.at[slot]).start()
            return carry
        lax.fori_loop(0, MOE_ROWS, body, 0)

    @pl.when(i == 0)
    def _():
        gather(0, 0)

    @pl.when(i < n_active)
    def _():
        slot = i % 2
        pltpu.make_async_copy(h3_ref.at[pl.ds(0, rows8), :], buf.at[slot], sem.at[slot]).wait()

        @pl.when(i + 1 < n_active)
        def _():
            gather(i + 1, 1 - slot)

        d = xb.shape[1]
        for s in range(d // LANES):
            xb[:, s * LANES:(s + 1) * LANES] = buf[slot, pl.ds(s, MOE_ROWS, stride=SUBLANES), :].astype(BF16)
        gu = _dot(xb[...], wgu_ref[0]) + bgu_ref[0]
        f = gu.shape[1] // 2
        gt = jnp.minimum(gu[:, 0:f], SWIGLU_LIMIT)
        up = jnp.clip(gu[:, f:2 * f], -SWIGLU_LIMIT, SWIGLU_LIMIT)
        act = (up + 1.0) * (gt * jax.nn.sigmoid(SWIGLU_ALPHA * gt))
        y = _dot(act.astype(BF16), wd_ref[0]) + bd_ref[0]
        for s in range(d // LANES):
            y_ref[pl.ds(s, MOE_ROWS, stride=SUBLANES), :] = y[:, s * LANES:(s + 1) * LANES]

    @pl.when(i >= n_active)
    def _():
        y_ref[...] = jnp.zeros(y_ref.shape, F32)


def _experts(blk_e, n_active, tok, h3, wgu_bf, b_gu, wd_bf, b_down):
    n_blocks = blk_e.shape[0]
    e, d, f2 = wgu_bf.shape
    rows8 = MOE_ROWS * SUBLANES
    return pl.pallas_call(
        _expert_kernel,
        grid_spec=pltpu.PrefetchScalarGridSpec(
            num_scalar_prefetch=3,
            grid=(n_blocks,),
            in_specs=[pl.BlockSpec(memory_space=pl.ANY),
                      pl.BlockSpec((1, d, f2), lambda i, be, na, tok: (be[i], 0, 0)),
                      pl.BlockSpec((1, 1, f2), lambda i, be, na, tok: (be[i], 0, 0)),
                      pl.BlockSpec((1, f2 // 2, d), lambda i, be, na, tok: (be[i], 0, 0)),
                      pl.BlockSpec((1, 1, d), lambda i, be, na, tok: (be[i], 0, 0))],
            out_specs=pl.BlockSpec((rows8, LANES), lambda i, be, na, tok: (i, 0)),
            scratch_shapes=[pltpu.VMEM((2, rows8, LANES), F32), pltpu.VMEM((MOE_ROWS, d), BF16),
                            pltpu.SemaphoreType.DMA((2,))]),
        out_shape=jax.ShapeDtypeStruct((n_blocks * rows8, LANES), F32),
        compiler_params=_params("arbitrary"),
        name="experts",
    )(blk_e, n_active, tok, h3, wgu_bf, b_gu.reshape(e, 1, f2), wd_bf, b_down.reshape(e, 1, d))


def _combine_kernel(pos_ref, ys_ref, y1_ref, gt_ref, tw_ref, o_ref, buf, sem, *, tile_off):
    i = pl.program_id(0)
    n = pl.num_programs(0)
    tm = y1_ref.shape[0]
    rows8 = tm * SUBLANES

    def gather(tile, slot):
        def body(r, carry):
            for k in range(TOP_K):
                p = pos_ref[((tile + tile_off) * tm + r) * TOP_K + k]
                pltpu.make_async_copy(ys_ref.at[pl.ds(pl.multiple_of(p * SUBLANES, SUBLANES), SUBLANES), :],
                                      buf.at[slot, k, pl.ds(pl.multiple_of(r * SUBLANES, SUBLANES), SUBLANES), :],
                                      sem.at[slot]).start()
            return carry
        lax.fori_loop(0, tm, body, 0)

    @pl.when(i == 0)
    def _():
        gather(0, 0)

    slot = i % 2
    for k in range(TOP_K):
        pltpu.make_async_copy(ys_ref.at[pl.ds(0, rows8), :], buf.at[slot, k], sem.at[slot]).wait()

    @pl.when(i + 1 < n)
    def _():
        gather(i + 1, 1 - slot)

    tw = tw_ref[...]
    wk = [jnp.broadcast_to(tw[:, k:k + 1], (tm, LANES)) for k in range(TOP_K)]
    gt = gt_ref[0]
    for s in range(o_ref.shape[1] // LANES):
        moe = wk[0] * buf[slot, 0, pl.ds(s, tm, stride=SUBLANES), :]
        for k in range(1, TOP_K):
            moe = moe + wk[k] * buf[slot, k, pl.ds(s, tm, stride=SUBLANES), :]
        cols = slice(s * LANES, (s + 1) * LANES)
        o_ref[:, cols] = y1_ref[:, cols] + gt[:, cols] * moe


def _combine(pos_flat, ys, y1, gate, tw, *, tm, rows_per_mod, tile_off):
    n, d = y1.shape
    mod_r = gate.shape[1]
    return pl.pallas_call(
        functools.partial(_combine_kernel, tile_off=tile_off),
        grid_spec=pltpu.PrefetchScalarGridSpec(
            num_scalar_prefetch=1,
            grid=(n // tm,),
            in_specs=[pl.BlockSpec(memory_space=pl.ANY),
                      pl.BlockSpec((tm, d), lambda i, pos: (i, 0)),
                      pl.BlockSpec((1, mod_r, d), lambda i, pos: (i // (rows_per_mod // tm), 0, 0)),
                      pl.BlockSpec((tm, LANES), lambda i, pos: (i, 0))],
            out_specs=pl.BlockSpec((tm, d), lambda i, pos: (i, 0)),
            scratch_shapes=[pltpu.VMEM((2, TOP_K, tm * SUBLANES, LANES), F32), pltpu.SemaphoreType.DMA((2,))]),
        out_shape=jax.ShapeDtypeStruct((n, d), F32),
        compiler_params=_params("arbitrary"),
        name="combine",
    )(pos_flat, ys, y1, gate, tw)


def _rope_tables(pos):
    half = HEAD_DIM // 2
    inv = ROPE_THETA ** (-jnp.arange(half, dtype=F32) / half)
    ang = pos.astype(F32)[:, None] * inv[None, :]
    cos = jnp.cos(ang)
    sin = jnp.sin(ang)
    reps = LANES // HEAD_DIM
    return (jnp.tile(jnp.concatenate([cos, cos], axis=1), (1, reps)),
            jnp.tile(jnp.concatenate([-sin, sin], axis=1), (1, reps)))


def _routing(top_e, n_tok):
    n_assign = n_tok * TOP_K
    e_flat = top_e.reshape(-1)
    order = jnp.argsort(e_flat)
    e_sorted = e_flat[order]
    tok_sorted = (order // TOP_K).astype(jnp.int32)
    counts = jnp.bincount(e_flat, length=N_EXPERTS)
    padded = (counts + MOE_ROWS - 1) // MOE_ROWS * MOE_ROWS
    pad_end = jnp.cumsum(padded)
    pad_start = pad_end - padded
    sort_start = jnp.cumsum(counts) - counts
    dest = (pad_start[e_sorted] + jnp.arange(n_assign, dtype=jnp.int32) - sort_start[e_sorted]).astype(jnp.int32)
    n_blocks = -(-n_assign // MOE_ROWS) + N_EXPERTS
    tok = jnp.full((n_blocks * MOE_ROWS,), n_tok, jnp.int32).at[dest].set(tok_sorted)
    blk_e = jnp.minimum(jnp.searchsorted(pad_end, jnp.arange(n_blocks, dtype=jnp.int32) * MOE_ROWS, side='right'),
                        N_EXPERTS - 1).astype(jnp.int32)
    pos = jnp.zeros((n_assign,), jnp.int32).at[order].set(dest)
    n_active = (pad_end[-1] // MOE_ROWS).astype(jnp.int32).reshape(1)
    return blk_e, n_active, tok, pos


def _head_dup(a):
    return jnp.concatenate([a, a], axis=-1)


def _head_pad(a):
    return jnp.concatenate([a, jnp.zeros_like(a)], axis=-1)


def kernel(x_prompt, x_sample, c_prompt, c_sample, cache_moba_kv, cache_nsa_kv, state_nsa_win_kv, page_table,
           norm_g, w_ada, b_ada, w_in, qk_gain, cmp_pos, cmp_w1, cmp_w2, w_out, w_router, b_router, w_gu, b_gu,
           w_down, b_down):
    bsz, seq, d = x_prompt.shape
    n_req = x_sample.shape[0]
    depth = norm_g.shape[0]
    assert depth == 1 and x_sample.shape[1] == 1
    assert seq % ATT_TK == 0 and seq // SEL_BLOCK <= LANES and seq // MOBA_BLOCK <= MOBA_BLOCK // SUBLANES
    n_pool, page = cache_moba_kv.shape[1], cache_moba_kv.shape[2]
    n_pages = page_table.shape[1]
    past = n_pages * page
    assert past % MOBA_BLOCK == 0 and 2 * page == MOBA_BLOCK and past // SEL_BLOCK < LANES
    assert state_nsa_win_kv.shape[2] == WINDOW
    layer = 0
    gw = G_NSA * HEAD_DIM
    n_prompt = bsz * seq

    bd = jnp.asarray(np.kron(np.eye(LANES // HEAD_DIM), np.full((HEAD_DIM, HEAD_DIM), 1.0 / HEAD_DIM)), BF16)
    w_in_bf = jnp.pad(w_in[layer], ((0, 0), (0, IN_COLS_PAD - IN_COLS))).astype(BF16)
    gains = jnp.tile(qk_gain[layer], (1, W_QM // HEAD_DIM))
    g1 = norm_g[layer, 0].reshape(1, d)
    g2 = norm_g[layer, 1].reshape(1, d)
    wo_bf = w_out[layer].astype(BF16)
    wr_pad = jnp.pad(w_router[layer], ((0, 0), (0, LANES - N_EXPERTS)))
    br_pad = jnp.pad(b_router[layer].reshape(1, N_EXPERTS), ((0, 0), (0, LANES - N_EXPERTS)),
                     constant_values=-jnp.inf)
    wgu_bf = w_gu[layer].astype(BF16)
    wd_bf = w_down[layer].astype(BF16)
    cmp_consts = _compress_consts(cmp_pos[layer], cmp_w1[layer], cmp_w2[layer], qk_gain[layer, 3])

    n_c = bsz + n_req
    n_c_pad = -(-n_c // SUBLANES) * SUBLANES
    c_all = jnp.pad(jnp.concatenate([c_prompt, c_sample], axis=0), ((0, n_c_pad - n_c), (0, 0)))
    mods = _ada(c_all, w_ada[layer], b_ada[layer])
    mods_p = [m.reshape(bsz, 1, d) for m in jnp.split(mods[:bsz], 6, axis=1)]
    mods_s = [m.reshape(1, n_req, d) for m in jnp.split(mods[bsz:n_c], 6, axis=1)]

    cos_p, sin_p = _rope_tables(jnp.arange(seq, dtype=jnp.int32))
    tm_p = 256
    moba_rows, nsa_rows, win_rows, q_m, qn, qr, gates, kmean = _proj(
        x_prompt.reshape(n_prompt, d), mods_p[0], mods_p[1], g1, w_in_bf, gains, bd, cos_p, sin_p,
        tm=tm_p, rows_per_mod=seq, pos_blocks=seq // tm_p, with_kmean=True)

    nbk = seq // MOBA_BLOCK
    pos = jnp.arange(seq, dtype=jnp.int32)
    k_m = moba_rows[:, :W_QM].reshape(bsz, seq, H_MOBA, HEAD_DIM).astype(BF16)
    oh_m = (pos[:, None] // MOBA_BLOCK == jnp.arange(MOBA_BLOCK // SUBLANES)[None, :]).astype(BF16)
    oh_m = jnp.broadcast_to(oh_m[None, :, None, :], (bsz, seq, H_MOBA, MOBA_BLOCK // SUBLANES))
    k_aug = jnp.concatenate([k_m, oh_m, jnp.zeros_like(oh_m)], axis=-1).reshape(bsz, seq, H_MOBA * LANES)
    v_m = moba_rows[:, W_QM:].reshape(bsz, seq, W_QM).astype(BF16)
    kmean_h = kmean.reshape(bsz, nbk, H_MOBA, HEAD_DIM).transpose(0, 2, 1, 3)
    kmp = jnp.zeros((bsz, H_MOBA, LANES, LANES), F32).at[:, :, HEAD_DIM:HEAD_DIM + nbk, :HEAD_DIM].set(kmean_h)
    o_m = _moba_attention(q_m.reshape(bsz, seq, W_QM), kmp, k_aug, v_m)

    nsa4 = nsa_rows.reshape(bsz, seq, 4, G_NSA, HEAD_DIM)
    win4 = win_rows.reshape(bsz, seq, 2, G_NSA, HEAD_DIM)
    kc, vc = _compress_prompt(nsa_rows.reshape(bsz, seq, 4 * gw), cmp_consts, bd)
    n_chunk = seq // CMP_STRIDE
    per_g = lambda a: a.reshape(bsz, -1, G_NSA, HEAD_DIM).transpose(0, 2, 1, 3)
    kcp = _head_pad(per_g(kc))
    vcd = _head_dup(per_g(vc)).astype(BF16)
    oh_s = (pos[:, None] // SEL_BLOCK == jnp.arange(LANES)[None, :]).astype(BF16)
    k_sel = per_g(nsa4[:, :, 2]).astype(BF16)
    ks_aug = jnp.concatenate([jnp.broadcast_to(oh_s[None, None], (bsz, G_NSA, seq, LANES)), k_sel,
                              jnp.zeros_like(k_sel)], axis=-1)
    vs_dup = _head_dup(per_g(nsa4[:, :, 3])).astype(BF16)
    kw_pad = _head_pad(per_g(win4[:, :, 0])).astype(BF16)
    vw_dup = _head_dup(per_g(win4[:, :, 1])).astype(BF16)
    gates_g = jnp.pad(gates[:, :N_GATE].reshape(bsz, seq, G_NSA, 3 * R_NSA).transpose(0, 2, 1, 3),
                      ((0, 0), (0, 0), (0, 0), (0, LANES - 3 * R_NSA)))
    cover_p = _cover(n_chunk - 1, seq // SEL_BLOCK, n_chunk)
    o_n = _nsa_attention(qn.reshape(bsz, seq, W_QN), qr.reshape(bsz, seq, W_QN), gates_g, kcp, vcd, cover_p,
                         ks_aug, vs_dup, kw_pad, vw_dup)
    o_p = jnp.concatenate([o_m, o_n], axis=-1).reshape(n_prompt, d)

    y1_p, h3_p, te_p, tw_p = _post(o_p, x_prompt.reshape(n_prompt, d), mods_p[2], mods_p[3], mods_p[4], g2, wo_bf,
                                   wr_pad, br_pad, tm=256, rows_per_mod=seq)

    cos_s, sin_s = _rope_tables(jnp.full((n_req,), past, jnp.int32))
    moba_new, nsa_new, win_new, q_m_s, qn_s, qr_s, gates_s = _proj(
        x_sample.reshape(n_req, d), mods_s[0], mods_s[1], g1, w_in_bf, gains, bd, cos_s, sin_s,
        tm=n_req, rows_per_mod=n_req, pos_blocks=1, with_kmean=False)
    pt_flat = page_table.reshape(-1).astype(jnp.int32)
    cache_m = cache_moba_kv[layer].reshape(n_pool, page, W_KVM)
    cache_n = cache_nsa_kv[layer].reshape(n_pool, page, 4 * gw)
    kc_s, vc_s = _compress_paged(cache_n, pt_flat, cmp_consts, bd, n_req=n_req, n_pages=n_pages)

    head_of_lane = jnp.arange(W_QM) // HEAD_DIM
    qbd = jnp.where(head_of_lane[None, None, :] == jnp.arange(H_MOBA)[None, :, None], q_m_s[:, None, :], 0.0)

    def rows8(q):
        qh = q.reshape(n_req, G_NSA, R_NSA, 1, HEAD_DIM)
        place = jnp.arange(G_NSA)[None, :, None, None, None] == jnp.arange(G_NSA)[None, None, None, :, None]
        return jnp.where(place, qh, 0.0).reshape(n_req, H_NSA, gw)

    g8 = jnp.pad(gates_s[:, :N_GATE].reshape(n_req, H_NSA, 3), ((0, 0), (0, 0), (0, LANES - 3)))
    newm = moba_new.reshape(n_req, 2, W_QM)
    news = nsa_new.reshape(n_req, 4, gw)[:, 2:4]
    neww = win_new.reshape(n_req, 2, gw)
    n_cmp_s = past // CMP_STRIDE
    cover_s = _cover(n_cmp_s - 1, past // SEL_BLOCK + 1, n_cmp_s)
    win_buf = state_nsa_win_kv[layer].reshape(n_req, WINDOW, 2 * gw)
    om_s, o8_s = _dec_attention(cache_m, cache_n, win_buf, pt_flat, qbd, rows8(qn_s), rows8(qr_s), g8, newm, news,
                                neww, kc_s, vc_s, cover_s, n_req=n_req, n_pages=n_pages)
    o8g = o8_s.reshape(n_req, G_NSA, R_NSA, G_NSA, HEAD_DIM)
    o_n_s = jnp.stack([o8g[:, g, :, g, :] for g in range(G_NSA)], axis=1).reshape(n_req, W_QN)
    o_s = jnp.concatenate([om_s.reshape(n_req, W_QM), o_n_s], axis=-1).astype(BF16)
    y1_s, h3_s, te_s, tw_s = _post(o_s, x_sample.reshape(n_req, d), mods_s[2], mods_s[3], mods_s[4], g2, wo_bf,
                                   wr_pad, br_pad, tm=n_req, rows_per_mod=n_req)

    n_tok = n_prompt + n_req
    h3 = jnp.concatenate([h3_p, h3_s, jnp.zeros((SUBLANES, LANES), F32)], axis=0)
    top_e = jnp.concatenate([te_p[:, :TOP_K], te_s[:, :TOP_K]], axis=0)
    blk_e, n_active, tok, pos_flat = _routing(top_e, n_tok)
    ys = _experts(blk_e, n_active, tok, h3, wgu_bf, b_gu[layer], wd_bf, b_down[layer])
    tm_c = 128
    y_p = _combine(pos_flat, ys, y1_p, mods_p[5], tw_p, tm=tm_c, rows_per_mod=seq, tile_off=0)
    y_s = _combine(pos_flat, ys, y1_s, mods_s[5], tw_s, tm=n_req, rows_per_mod=n_req, tile_off=n_prompt // n_req)

    keep = min(WINDOW, seq)
    win_p = win_rows.reshape(bsz, seq, 2, G_NSA, HEAD_DIM)[:, seq - keep:]
    win_s = jnp.concatenate([state_nsa_win_kv[layer][:, 1:], win_new.reshape(n_req, 1, 2, G_NSA, HEAD_DIM)], axis=1)
    return (y_p.reshape(bsz, seq, d), y_s.reshape(n_req, 1, d),
            moba_rows.reshape(1, bsz, seq, 2, H_MOBA, HEAD_DIM), nsa_rows.reshape(1, bsz, seq, 4, G_NSA, HEAD_DIM),
            win_p[None], moba_new.reshape(1, n_req, 1, 2, H_MOBA, HEAD_DIM),
            nsa_new.reshape(1, n_req, 1, 4, G_NSA, HEAD_DIM), win_s[None])
```

```python
import functools

import numpy as np
import jax
import jax.numpy as jnp
from jax import lax
from jax.experimental import pallas as pl
from jax.experimental.pallas import tpu as pltpu

F32 = jnp.float32
BF16 = jnp.bfloat16
HIGHEST = lax.Precision.HIGHEST

LANES = 128
SUBLANES = 8
HEAD_DIM = 64
H_MOBA = 8
H_NSA = 8
G_NSA = 2
R_NSA = H_NSA // G_NSA
MOBA_BLOCK = 256
MOBA_TOPK = 3
CMP_LEN = 32
CMP_STRIDE = 16
SEL_BLOCK = 64
SEL_TOPN = 16
WINDOW = 512
N_EXPERTS = 32
TOP_K = 4
SWIGLU_LIMIT = 7.0
SWIGLU_ALPHA = 1.702
ROPE_THETA = 10000.0
NORM_EPS = 1e-6
NEG_BIG = -1e30
TINY = 1e-30
SCALE = HEAD_DIM ** -0.5
Q_TILE = 128
MOBA_Q = MOBA_BLOCK
ATT_TK = 512
MOE_ROWS = 128
VMEM_LIMIT = 56 * 1024 * 1024

W_QM = H_MOBA * HEAD_DIM
W_KVM = 2 * H_MOBA * HEAD_DIM
W_QN = H_NSA * HEAD_DIM
W_KVN = 6 * G_NSA * HEAD_DIM
N_GATE = 3 * H_NSA
IN_COLS = W_QM + W_KVM + W_QN + W_KVN + N_GATE
IN_COLS_PAD = W_QM + W_KVM + W_QN + W_KVN + LANES


def _iota(shape, dim):
    return lax.broadcasted_iota(jnp.int32, shape, dim)


def _dot(a, b, precision=None):
    return jnp.dot(a, b, preferred_element_type=F32, precision=precision)


def _dot_nt(a, b, precision=None):
    return lax.dot_general(a, b, (((1,), (1,)), ((), ())), preferred_element_type=F32, precision=precision)


def _params(*sem):
    return pltpu.CompilerParams(dimension_semantics=sem, vmem_limit_bytes=VMEM_LIMIT)


def _seg_meansq(z, bd):
    zz = z * z
    hi = zz.astype(BF16)
    lo = (zz - hi.astype(F32)).astype(BF16)
    outs = []
    for c in range(z.shape[1] // LANES):
        sl = slice(c * LANES, (c + 1) * LANES)
        outs.append(_dot(hi[:, sl], bd) + _dot(lo[:, sl], bd))
    return outs[0] if len(outs) == 1 else jnp.concatenate(outs, axis=1)


def _head_norm(z, gain, bd):
    return z * lax.rsqrt(_seg_meansq(z, bd) + NORM_EPS) * gain


def _rope(z, cos, sin):
    outs = []
    first = (_iota((z.shape[0], LANES), 1) % HEAD_DIM) < (HEAD_DIM // 2)
    for c in range(z.shape[1] // LANES):
        x = z[:, c * LANES:(c + 1) * LANES]
        swapped = jnp.where(first, pltpu.roll(x, LANES - HEAD_DIM // 2, 1), pltpu.roll(x, HEAD_DIM // 2, 1))
        outs.append(x * cos + swapped * sin)
    return outs[0] if len(outs) == 1 else jnp.concatenate(outs, axis=1)


def _top_k_lanes(cur, k):
    lane = _iota(cur.shape, 1).astype(F32)
    picked = jnp.zeros(cur.shape, F32)
    vals, ids = [], []
    for _ in range(k):
        mx = jnp.max(cur, axis=1, keepdims=True)
        first = jnp.min(jnp.where(cur == mx, lane, 1e9), axis=1, keepdims=True)
        hit = lane == first
        picked = jnp.where(hit, 1.0, picked)
        cur = jnp.where(hit, -jnp.inf, cur)
        vals.append(mx)
        ids.append(first)
    return picked, vals, ids


def _top_k_sublanes(cur, k):
    idx = _iota(cur.shape, 0).astype(F32)
    picked = jnp.zeros(cur.shape, F32)
    for _ in range(k):
        mx = jnp.max(cur, axis=0, keepdims=True)
        first = jnp.min(jnp.where(cur == mx, idx, 1e9), axis=0, keepdims=True)
        hit = idx == first
        picked = jnp.where(hit, 1.0, picked)
        cur = jnp.where(hit, -jnp.inf, cur)
    return picked


def _flash_step(q, k, v, mask, m, l, acc):
    s = _dot_nt(q, k)
    if mask is not None:
        s = jnp.where(mask, s, NEG_BIG)
    m_new = jnp.maximum(m, jnp.max(s, axis=1, keepdims=True))
    alpha = jnp.exp(m - m_new)
    p = jnp.exp(s - m_new)
    l_new = alpha * l + jnp.sum(p, axis=1, keepdims=True)
    acc_new = alpha * acc + _dot(p.astype(BF16), v)
    return m_new, l_new, acc_new


def _ada_kernel(c_ref, w_ref, b_ref, o_ref):
    c = c_ref[...]
    o_ref[...] = _dot(c * jax.nn.sigmoid(c), w_ref[...], HIGHEST) + b_ref[...]


def _ada(c_all, w_ada, b_ada):
    n, d = c_all.shape
    cols = w_ada.shape[1]
    tn = 1024
    return pl.pallas_call(
        _ada_kernel,
        grid=(cols // tn,),
        in_specs=[pl.BlockSpec((n, d), lambda j: (0, 0)),
                  pl.BlockSpec((d, tn), lambda j: (0, j)),
                  pl.BlockSpec((1, tn), lambda j: (0, j))],
        out_specs=pl.BlockSpec((n, tn), lambda j: (0, j)),
        out_shape=jax.ShapeDtypeStruct((n, cols), F32),
        compiler_params=_params("arbitrary"),
        name="ada",
    )(c_all, w_ada, b_ada.reshape(1, cols))


def _proj_kernel(x_ref, sh_ref, sc_ref, g_ref, w_ref, gains_ref, bd_ref, cos_ref, sin_ref,
                 moba_ref, nsa_ref, win_ref, qm_ref, qn_ref, qr_ref, gate_ref, *maybe_kmean, with_kmean):
    x = x_ref[...]
    y = x * lax.rsqrt(jnp.mean(x * x, axis=1, keepdims=True) + NORM_EPS) * g_ref[...]
    h = (y * (1.0 + sc_ref[0]) + sh_ref[0]).astype(BF16)
    bd = bd_ref[...]
    cos = cos_ref[...]
    sin = sin_ref[...]
    o = 0

    def seg(width):
        nonlocal o
        z = _dot(h, w_ref[:, o:o + width])
        o += width
        return z

    def gain(i, width):
        return gains_ref[i:i + 1, 0:width]

    qm_ref[...] = _rope(_head_norm(seg(W_QM), gain(0, W_QM), bd), cos, sin)
    k_m = _rope(_head_norm(seg(W_QM), gain(1, W_QM), bd), cos, sin)
    moba_ref[:, 0:W_QM] = k_m
    moba_ref[:, W_QM:2 * W_QM] = seg(W_QM)
    qn = _head_norm(seg(W_QN), gain(2, W_QN), bd)
    qn_ref[...] = qn
    qr_ref[...] = _rope(qn, cos, sin)
    gw = G_NSA * HEAD_DIM
    nsa_ref[:, 0:2 * gw] = seg(2 * gw)
    nsa_ref[:, 2 * gw:3 * gw] = _rope(_head_norm(seg(gw), gain(4, gw), bd), cos, sin)
    nsa_ref[:, 3 * gw:4 * gw] = seg(gw)
    win_ref[:, 0:gw] = _rope(_head_norm(seg(gw), gain(5, gw), bd), cos, sin)
    win_ref[:, gw:2 * gw] = seg(gw)
    gate_ref[...] = jax.nn.sigmoid(seg(LANES))
    if with_kmean:
        (kmean_ref,) = maybe_kmean
        tm = k_m.shape[0]
        kmean_ref[0] = jnp.mean(k_m.reshape(tm // MOBA_BLOCK, MOBA_BLOCK, W_QM), axis=1)


def _proj(x, shift, scale, g, w_in_bf, gains, bd, cos, sin, *, tm, rows_per_mod, pos_blocks, with_kmean):
    n, d = x.shape
    nt = n // tm
    mod_r = shift.shape[1]
    mod_map = lambda i: (i // (rows_per_mod // tm), 0, 0)
    pos_map = lambda i: (i % pos_blocks, 0)
    row = lambda w: pl.BlockSpec((tm, w), lambda i: (i, 0))
    const = lambda a: pl.BlockSpec(a.shape, lambda i: (0,) * a.ndim)
    out_shapes = [jax.ShapeDtypeStruct((n, W_KVM), F32), jax.ShapeDtypeStruct((n, 4 * G_NSA * HEAD_DIM), F32),
                  jax.ShapeDtypeStruct((n, 2 * G_NSA * HEAD_DIM), F32), jax.ShapeDtypeStruct((n, W_QM), F32),
                  jax.ShapeDtypeStruct((n, W_QN), F32), jax.ShapeDtypeStruct((n, W_QN), F32),
                  jax.ShapeDtypeStruct((n, LANES), F32)]
    out_specs = [row(W_KVM), row(4 * G_NSA * HEAD_DIM), row(2 * G_NSA * HEAD_DIM), row(W_QM), row(W_QN), row(W_QN),
                 row(LANES)]
    if with_kmean:
        nbt = tm // MOBA_BLOCK
        out_shapes.append(jax.ShapeDtypeStruct((nt, nbt, W_QM), F32))
        out_specs.append(pl.BlockSpec((1, nbt, W_QM), lambda i: (i, 0, 0)))
    return pl.pallas_call(
        functools.partial(_proj_kernel, with_kmean=with_kmean),
        grid=(nt,),
        in_specs=[row(d), pl.BlockSpec((1, mod_r, d), mod_map), pl.BlockSpec((1, mod_r, d), mod_map),
                  const(g), const(w_in_bf), const(gains), const(bd),
                  pl.BlockSpec((tm, LANES), pos_map), pl.BlockSpec((tm, LANES), pos_map)],
        out_specs=out_specs,
        out_shape=out_shapes,
        compiler_params=_params("parallel"),
        name="proj",
    )(x, shift, scale, g, w_in_bf, gains, bd, cos, sin)


def _compress_compute(src_refs, pe_ref, w1_ref, w2_ref, gk_ref, bd_ref, kc_ref, vc_ref, n_rows):
    n_chunk = n_rows // CMP_STRIDE
    gw = G_NSA * HEAD_DIM
    for kv in range(2):
        acc_a = jnp.zeros((n_chunk, gw), F32)
        acc_b = jnp.zeros((n_chunk, gw), F32)
        for j in range(CMP_STRIDE):
            xj = src_refs[kv][pl.ds(j, n_chunk, stride=CMP_STRIDE), :]
            acc_a = acc_a + _dot(xj + pe_ref[kv, 0, j:j + 1, :], w1_ref[kv, 0, j], HIGHEST)
            acc_b = acc_b + _dot(xj + pe_ref[kv, 1, j:j + 1, :], w1_ref[kv, 1, j], HIGHEST)
        hid = jax.nn.gelu(acc_a + pltpu.roll(acc_b, n_chunk - 1, 0))
        out = _dot(hid, w2_ref[kv], HIGHEST)
        if kv == 0:
            kc_ref[0] = _head_norm(out, gk_ref[...], bd_ref[...])
        else:
            vc_ref[0] = out


def _compress_prompt_kernel(k_ref, v_ref, pe_ref, w1_ref, w2_ref, gk_ref, bd_ref, kc_ref, vc_ref, *, n_rows):
    _compress_compute((k_ref.at[0], v_ref.at[0]), pe_ref, w1_ref, w2_ref, gk_ref, bd_ref, kc_ref, vc_ref, n_rows)


def _compress_paged_kernel(pt_ref, page_ref, pe_ref, w1_ref, w2_ref, gk_ref, bd_ref, kc_ref, vc_ref, kbuf, vbuf,
                           *, n_rows, page):
    p = pl.program_id(1)
    gw = G_NSA * HEAD_DIM
    rows = pl.ds(pl.multiple_of(p * page, page), page)
    kbuf[rows, :] = page_ref[0, :, 0:gw]
    vbuf[rows, :] = page_ref[0, :, gw:2 * gw]

    @pl.when(p == pl.num_programs(1) - 1)
    def _():
        _compress_compute((kbuf, vbuf), pe_ref, w1_ref, w2_ref, gk_ref, bd_ref, kc_ref, vc_ref, n_rows)


def _compress_consts(cmp_pos, cmp_w1, cmp_w2, gain_k_cmp):
    pe = jnp.tile(cmp_pos.reshape(2, 2, CMP_STRIDE, HEAD_DIM), (1, 1, 1, G_NSA))
    eye = jnp.eye(G_NSA, dtype=F32)
    w1 = cmp_w1.reshape(2, 2, CMP_STRIDE, HEAD_DIM, HEAD_DIM)
    w1bd = jnp.einsum("gh,kajde->kajgdhe", eye, w1).reshape(2, 2, CMP_STRIDE, G_NSA * HEAD_DIM, G_NSA * HEAD_DIM)
    w2bd = jnp.einsum("gh,kde->kgdhe", eye, cmp_w2).reshape(2, G_NSA * HEAD_DIM, G_NSA * HEAD_DIM)
    gk = jnp.tile(gain_k_cmp.reshape(1, HEAD_DIM), (1, G_NSA))
    return pe, w1bd, w2bd, gk


def _compress_prompt(nsa_rows, consts, bd):
    b, s, _ = nsa_rows.shape
    pe, w1bd, w2bd, gk = consts
    n_chunk = s // CMP_STRIDE
    gw = G_NSA * HEAD_DIM
    const = lambda a: pl.BlockSpec(a.shape, lambda i: (0,) * a.ndim)
    out = jax.ShapeDtypeStruct((b, n_chunk, gw), F32)
    return pl.pallas_call(
        functools.partial(_compress_prompt_kernel, n_rows=s),
        grid=(b,),
        in_specs=[pl.BlockSpec((1, s, gw), lambda i: (i, 0, 0)), pl.BlockSpec((1, s, gw), lambda i: (i, 0, 1)),
                  const(pe), const(w1bd), const(w2bd), const(gk), const(bd)],
        out_specs=[pl.BlockSpec((1, n_chunk, gw), lambda i: (i, 0, 0))] * 2,
        out_shape=[out, out],
        compiler_params=_params("parallel"),
        name="compress_prompt",
    )(nsa_rows, nsa_rows, pe, w1bd, w2bd, gk, bd)


def _compress_paged(cache_nsa, pt_flat, consts, bd, *, n_req, n_pages):
    _, page, _ = cache_nsa.shape
    pe, w1bd, w2bd, gk = consts
    n_rows = n_pages * page
    n_chunk = n_rows // CMP_STRIDE
    gw = G_NSA * HEAD_DIM
    const = lambda a: pl.BlockSpec(a.shape, lambda b, p, pt: (0,) * a.ndim)
    out = jax.ShapeDtypeStruct((n_req, n_chunk, gw), F32)
    return pl.pallas_call(
        functools.partial(_compress_paged_kernel, n_rows=n_rows, page=page),
        grid_spec=pltpu.PrefetchScalarGridSpec(
            num_scalar_prefetch=1,
            grid=(n_req, n_pages),
            in_specs=[pl.BlockSpec((1, page, 2 * gw), lambda b, p, pt: (pt[b * n_pages + p], 0, 0)),
                      const(pe), const(w1bd), const(w2bd), const(gk), const(bd)],
            out_specs=[pl.BlockSpec((1, n_chunk, gw), lambda b, p, pt: (b, 0, 0))] * 2,
            scratch_shapes=[pltpu.VMEM((n_rows, gw), F32)] * 2),
        out_shape=[out, out],
        compiler_params=_params("parallel", "arbitrary"),
        name="compress_paged",
    )(pt_flat, cache_nsa, pe, w1bd, w2bd, gk, bd)


def _moba_kernel(q_ref, kmp_ref, k_ref, v_ref, o_ref):
    qi = pl.program_id(2)
    own = qi
    q2 = q_ref[0]
    lane = _iota((MOBA_Q, LANES), 1)
    n_blk = MOBA_BLOCK // SUBLANES
    blk = _iota((n_blk, MOBA_Q), 0)
    valid = blk < own
    q_augs = []
    for h in range(2):
        qh = q2 if h == 0 else pltpu.roll(q2, HEAD_DIM, 1)
        q0 = jnp.where(lane < HEAD_DIM, qh, 0.0)
        score = _dot_nt(kmp_ref[0, h], q0, HIGHEST)[HEAD_DIM:HEAD_DIM + n_blk, :]
        picked = _top_k_sublanes(jnp.where(valid, score, -jnp.inf), MOBA_TOPK)
        sel = ((picked > 0.5) & valid) | (blk == own)
        bias_t = jnp.concatenate([jnp.zeros((HEAD_DIM, MOBA_Q), F32), jnp.where(sel, 0.0, NEG_BIG),
                                  jnp.zeros((LANES - HEAD_DIM - n_blk, MOBA_Q), F32)], axis=0)
        q_augs.append(jnp.where(lane < HEAD_DIM, q0 * SCALE, bias_t.T).astype(BF16))
    pos = qi * MOBA_Q + _iota((MOBA_Q, ATT_TK), 0)

    def tile(j, carry, masked):
        start = pl.multiple_of(j * ATT_TK, ATT_TK)
        v = v_ref[0, pl.ds(start, ATT_TK), :]
        mask = None
        if masked:
            mask = (j * ATT_TK + _iota((MOBA_Q, ATT_TK), 1)) <= pos
        return tuple(_flash_step(q_augs[h], k_ref[0, pl.ds(start, ATT_TK), h * LANES:(h + 1) * LANES], v, mask,
                                 *carry[h]) for h in range(2))

    init = (jnp.full((MOBA_Q, 1), NEG_BIG, F32), jnp.zeros((MOBA_Q, 1), F32), jnp.zeros((MOBA_Q, LANES), F32))
    jd = (own * MOBA_BLOCK) // ATT_TK
    carry = lax.fori_loop(0, jd, lambda j, c: tile(j, c, False), tile(jd, (init, init), True))
    outs = [acc / jnp.maximum(l, TINY) for (_, l, acc) in carry]
    o_ref[0] = jnp.where(lane < HEAD_DIM, outs[0], outs[1]).astype(o_ref.dtype)


def _moba_attention(q_m, kmp, k_aug, v_m):
    b, s, _ = q_m.shape
    return pl.pallas_call(
        _moba_kernel,
        grid=(b, H_MOBA // 2, s // MOBA_Q),
        in_specs=[pl.BlockSpec((1, MOBA_Q, LANES), lambda b, h, i: (b, i, h)),
                  pl.BlockSpec((1, 2, LANES, LANES), lambda b, h, i: (b, h, 0, 0)),
                  pl.BlockSpec((1, s, 2 * LANES), lambda b, h, i: (b, 0, h)),
                  pl.BlockSpec((1, s, LANES), lambda b, h, i: (b, 0, h))],
        out_specs=pl.BlockSpec((1, MOBA_Q, LANES), lambda b, h, i: (b, i, h)),
        out_shape=jax.ShapeDtypeStruct((b, s, W_QM), BF16),
        compiler_params=_params("parallel", "parallel", "arbitrary"),
        name="moba_attention",
    )(q_m, kmp, k_aug, v_m)


def _stack_heads(q4):
    lane = _iota((Q_TILE, LANES), 1)
    parts = []
    for r in range(R_NSA):
        c = q4[:, (r // 2) * LANES:(r // 2 + 1) * LANES]
        if r % 2:
            c = pltpu.roll(c, HEAD_DIM, 1)
        parts.append(jnp.where(lane < HEAD_DIM, c, 0.0))
    return jnp.concatenate(parts, axis=0)


def _nsa_kernel(qn_ref, qr_ref, gate_ref, kc_ref, vc_ref, cover_ref, ks_ref, vs_ref, kw_ref, vw_ref, o_ref,
                *, n_chunk):
    qi = pl.program_id(2)
    rows = R_NSA * Q_TILE
    qloc = _iota((rows, 1), 0) % Q_TILE
    pos = qi * Q_TILE + qloc

    qn = _stack_heads(qn_ref[0])
    s = _dot_nt(qn, kc_ref[0, 0], HIGHEST) * SCALE
    cmask = (_iota((rows, n_chunk), 1) * CMP_STRIDE + (CMP_LEN - 1)) <= pos
    s = jnp.where(cmask, s, NEG_BIG)
    e = jnp.where(cmask, jnp.exp(s - jnp.max(s, axis=1, keepdims=True)), 0.0)
    p_cmp = e / jnp.maximum(jnp.sum(e, axis=1, keepdims=True), TINY)
    o_cmp = _dot(p_cmp.astype(BF16), vc_ref[0, 0])

    p_grp = p_cmp[0:Q_TILE]
    for r in range(1, R_NSA):
        p_grp = p_grp + p_cmp[r * Q_TILE:(r + 1) * Q_TILE]
    imp_t = _dot_nt(cover_ref[...], p_grp, HIGHEST)
    blk_t = _iota((LANES, Q_TILE), 0)
    own_t = (qi * Q_TILE + _iota((LANES, Q_TILE), 1)) // SEL_BLOCK
    forced = (blk_t == 0) | (blk_t == own_t) | (blk_t == own_t - 1)
    score = jnp.where(blk_t <= own_t, jnp.where(forced, jnp.inf, imp_t), -jnp.inf)
    picked = _top_k_sublanes(score, SEL_TOPN)
    bias = jnp.where((picked > 0.5) & (blk_t <= own_t), 0.0, NEG_BIG).T
    blk = _iota((Q_TILE, LANES), 1)

    qr = (_stack_heads(qr_ref[0]) * SCALE).astype(BF16)
    q_aug = jnp.concatenate([jnp.concatenate([bias] * R_NSA, axis=0).astype(BF16), qr], axis=1)
    init = (jnp.full((rows, 1), NEG_BIG, F32), jnp.zeros((rows, 1), F32), jnp.zeros((rows, LANES), F32))

    tk = ATT_TK
    jd = (qi * Q_TILE) // tk

    def sel_tile(j, carry, masked):
        start = pl.multiple_of(j * tk, tk)
        mask = None
        if masked:
            mask = (j * tk + _iota((rows, tk), 1)) <= pos
        return _flash_step(q_aug, ks_ref[0, 0, pl.ds(start, tk), :], vs_ref[0, 0, pl.ds(start, tk), :], mask, *carry)

    m, l, acc = lax.fori_loop(0, jd, lambda j, c: sel_tile(j, c, False), sel_tile(jd, init, True))
    o_sel = acc / jnp.maximum(l, TINY)

    kloc = _iota((rows, Q_TILE), 1)

    def win_tile(kt, mask, carry):
        start = pl.multiple_of(jnp.maximum(kt, 0) * Q_TILE, Q_TILE)
        return _flash_step(qr, kw_ref[0, 0, pl.ds(start, Q_TILE), :], vw_ref[0, 0, pl.ds(start, Q_TILE), :], mask,
                           *carry)

    carry = win_tile(qi, kloc <= qloc, init)
    n_back = WINDOW // Q_TILE
    for t in range(1, n_back + 1):
        mask = ((qi - t) * Q_TILE + kloc) >= 0
        if t == n_back:
            mask = mask & (kloc > qloc)
        carry = win_tile(qi - t, mask, carry)
    m, l, acc = carry
    o_win = acc / jnp.maximum(l, TINY)

    gates = gate_ref[0, 0]
    heads = []
    for r in range(R_NSA):
        rs = slice(r * Q_TILE, (r + 1) * Q_TILE)
        heads.append(gates[:, 3 * r:3 * r + 1] * o_cmp[rs] + gates[:, 3 * r + 1:3 * r + 2] * o_sel[rs]
                     + gates[:, 3 * r + 2:3 * r + 3] * o_win[rs])
    lo = blk < HEAD_DIM
    o_ref[0] = jnp.concatenate([jnp.where(lo, heads[0], heads[1]), jnp.where(lo, heads[2], heads[3])],
                               axis=1).astype(o_ref.dtype)


def _nsa_attention(qn, qr, gates_g, kcp, vcd, cover, ks_aug, vs_dup, kw_pad, vw_dup):
    b, s, _ = qn.shape
    n_chunk = kcp.shape[2]
    gq = R_NSA * HEAD_DIM
    per_g = lambda w: pl.BlockSpec((1, 1, s, w), lambda b, g, i: (b, g, 0, 0))
    return pl.pallas_call(
        functools.partial(_nsa_kernel, n_chunk=n_chunk),
        grid=(b, G_NSA, s // Q_TILE),
        in_specs=[pl.BlockSpec((1, Q_TILE, gq), lambda b, g, i: (b, i, g)),
                  pl.BlockSpec((1, Q_TILE, gq), lambda b, g, i: (b, i, g)),
                  pl.BlockSpec((1, 1, Q_TILE, LANES), lambda b, g, i: (b, g, i, 0)),
                  pl.BlockSpec((1, 1, n_chunk, LANES), lambda b, g, i: (b, g, 0, 0)),
                  pl.BlockSpec((1, 1, n_chunk, LANES), lambda b, g, i: (b, g, 0, 0)),
                  pl.BlockSpec(cover.shape, lambda b, g, i: (0, 0)),
                  per_g(2 * LANES), per_g(LANES), per_g(LANES), per_g(LANES)],
        out_specs=pl.BlockSpec((1, Q_TILE, gq), lambda b, g, i: (b, i, g)),
        out_shape=jax.ShapeDtypeStruct((b, s, W_QN), BF16),
        compiler_params=_params("parallel", "parallel", "arbitrary"),
        name="nsa_attention",
    )(qn, qr, gates_g, kcp, vcd, cover, ks_aug, vs_dup, kw_pad, vw_dup)


def _cover(n_cmp, n_sel, rows):
    c0 = np.arange(rows)[:, None] * CMP_STRIDE
    b0 = np.arange(LANES)[None, :] * SEL_BLOCK
    ok = (c0 < b0 + SEL_BLOCK) & (c0 + CMP_LEN > b0) & (np.arange(rows)[:, None] < n_cmp) & (np.arange(LANES)[None, :] < n_sel)
    return jnp.asarray(ok.astype(np.float32))


def _dec_kernel(pt_ref, ma_ref, mb_ref, na_ref, nb_ref, wb_ref, qbd_ref, q8n_ref, q8r_ref, g8_ref, newm_ref,
                news_ref, neww_ref, kc_ref, vc_ref, cover_ref, om_ref, o8_ref,
                m_sc, l_sc, s_sc, acc_sc, msel_sc, lsel_sc, asel_sc, bias_sc, ocmp_sc, *, n_steps, page):
    j = pl.program_id(1)
    sub8 = _iota((SUBLANES, LANES), 0)
    lane8 = _iota((SUBLANES, LANES), 1)
    tk = 2 * page
    q8r = q8r_ref[0] * SCALE
    q8r_bf = q8r.astype(BF16)

    @pl.when(j == 0)
    def _():
        n_chunk = kc_ref.shape[1]
        s = _dot_nt(q8n_ref[0], kc_ref[0], HIGHEST) * SCALE
        cmask = _iota((SUBLANES, n_chunk), 1) < (n_chunk - 1)
        s = jnp.where(cmask, s, NEG_BIG)
        e = jnp.where(cmask, jnp.exp(s - jnp.max(s, axis=1, keepdims=True)), 0.0)
        p_cmp = e / jnp.maximum(jnp.sum(e, axis=1, keepdims=True), TINY)
        ocmp_sc[...] = _dot(p_cmp.astype(BF16), vc_ref[0].astype(BF16))
        subc = _iota((SUBLANES, n_chunk), 0)
        g0 = jnp.sum(jnp.where(subc < R_NSA, p_cmp, 0.0), axis=0, keepdims=True)
        g1 = jnp.sum(jnp.where(subc >= R_NSA, p_cmp, 0.0), axis=0, keepdims=True)
        imp = _dot(jnp.where(subc < R_NSA, g0, g1), cover_ref[...], HIGHEST)
        own = (n_steps * tk) // SEL_BLOCK
        forced = (lane8 == 0) | (lane8 == own) | (lane8 == own - 1)
        score = jnp.where(lane8 <= own, jnp.where(forced, jnp.inf, imp), -jnp.inf)
        picked, _, _ = _top_k_lanes(score, SEL_TOPN)
        bias_sc[...] = jnp.where((picked > 0.5) & (lane8 <= own), 0.0, NEG_BIG)
        msel_sc[...] = jnp.full((SUBLANES, LANES), NEG_BIG, F32)
        lsel_sc[...] = jnp.zeros((SUBLANES, LANES), F32)
        asel_sc[...] = jnp.zeros((SUBLANES, LANES), F32)

    hw = H_MOBA * HEAD_DIM
    k_m = jnp.concatenate([ma_ref[0, :, 0:hw], mb_ref[0, :, 0:hw]], axis=0)
    v_m = jnp.concatenate([ma_ref[0, :, hw:2 * hw], mb_ref[0, :, hw:2 * hw]], axis=0)
    qbd = qbd_ref[0]
    s = _dot_nt((qbd * SCALE).astype(BF16), k_m.astype(BF16))
    mj = jnp.max(s, axis=1, keepdims=True)
    p = jnp.exp(s - mj)
    m_sc[j] = jnp.broadcast_to(mj, (SUBLANES, LANES))
    l_sc[j] = jnp.broadcast_to(jnp.sum(p, axis=1, keepdims=True), (SUBLANES, LANES))
    acc_sc[j] = _dot(p.astype(BF16), v_m.astype(BF16))
    kmean = jnp.sum(k_m, axis=0, keepdims=True) * (1.0 / MOBA_BLOCK)
    s_sc[j] = jnp.broadcast_to(jnp.sum(qbd * kmean, axis=1, keepdims=True), (SUBLANES, LANES))

    gw = G_NSA * HEAD_DIM
    k_s = jnp.concatenate([na_ref[0, :, 0:gw], nb_ref[0, :, 0:gw]], axis=0).astype(BF16)
    v_s = jnp.concatenate([na_ref[0, :, gw:2 * gw], nb_ref[0, :, gw:2 * gw]], axis=0).astype(BF16)
    s = _dot_nt(q8r_bf, k_s)
    bias = bias_sc[...]
    kb = _iota((SUBLANES, tk), 1) // SEL_BLOCK
    bias_k = jnp.zeros((SUBLANES, tk), F32)
    for q in range(tk // SEL_BLOCK):
        bq = jnp.sum(jnp.where(lane8 == j * (tk // SEL_BLOCK) + q, bias, 0.0), axis=1, keepdims=True)
        bias_k = jnp.where(kb == q, bq, bias_k)
    s = s + bias_k
    m_old = msel_sc[:, 0:1]
    m_new = jnp.maximum(m_old, jnp.max(s, axis=1, keepdims=True))
    alpha = jnp.exp(m_old - m_new)
    p = jnp.exp(s - m_new)
    msel_sc[...] = jnp.broadcast_to(m_new, (SUBLANES, LANES))
    lsel_sc[...] = alpha * lsel_sc[...] + jnp.sum(p, axis=1, keepdims=True)
    asel_sc[...] = alpha * asel_sc[...] + _dot(p.astype(BF16), v_s)

    @pl.when(j == n_steps - 1)
    def _():
        news = news_ref[0]
        s_new = jnp.sum(q8r * news[0:1], axis=1, keepdims=True)
        m_old = msel_sc[:, 0:1]
        m_new = jnp.maximum(m_old, s_new)
        alpha = jnp.exp(m_old - m_new)
        pn = jnp.exp(s_new - m_new)
        l = alpha * lsel_sc[...] + pn
        o_sel = (alpha * asel_sc[...] + pn * news[1:2]) / jnp.maximum(l, TINY)

        neww = neww_ref[0]
        kw = wb_ref[0, :, 0:gw].astype(BF16)
        vw = wb_ref[0, :, gw:2 * gw].astype(BF16)
        nw = kw.shape[0]
        s = _dot_nt(q8r_bf, kw)
        wmask = _iota((SUBLANES, nw), 1) >= 1
        s = jnp.where(wmask, s, NEG_BIG)
        s_new = jnp.sum(q8r * neww[0:1], axis=1, keepdims=True)
        mw = jnp.maximum(jnp.max(s, axis=1, keepdims=True), s_new)
        e = jnp.where(wmask, jnp.exp(s - mw), 0.0)
        en = jnp.exp(s_new - mw)
        lw = jnp.sum(e, axis=1, keepdims=True) + en
        o_win = (_dot(e.astype(BF16), vw) + en * neww[1:2]) / jnp.maximum(lw, TINY)

        g8 = g8_ref[0]
        o8_ref[0] = g8[:, 0:1] * ocmp_sc[...] + g8[:, 1:2] * o_sel + g8[:, 2:3] * o_win

        newm = newm_ref[0]
        scores = [s_sc[b] for b in range(n_steps)]
        m_tot = jnp.broadcast_to(jnp.sum(qbd * newm[0:1], axis=1, keepdims=True) * SCALE, (SUBLANES, LANES))
        s_own = m_tot
        sels = []
        for b in range(n_steps):
            rank = jnp.zeros((SUBLANES, LANES), F32)
            for c in range(n_steps):
                if c == b:
                    continue
                ahead = (scores[c] > scores[b]) | ((scores[c] == scores[b]) & (c < b))
                rank = rank + jnp.where(ahead, 1.0, 0.0)
            sels.append(rank < MOBA_TOPK)
            m_tot = jnp.where(sels[b], jnp.maximum(m_tot, m_sc[b]), m_tot)
        w_own = jnp.exp(s_own - m_tot)
        l_tot = w_own
        acc = w_own[:, 0:1] * newm[1:2]
        for b in range(n_steps):
            wgt = jnp.where(sels[b], jnp.exp(m_sc[b] - m_tot), 0.0)
            l_tot = l_tot + wgt * l_sc[b]
            acc = acc + wgt[:, 0:1] * acc_sc[b]
        o_full = acc / jnp.maximum(l_tot[:, 0:1], TINY)
        own_head = (_iota((SUBLANES, hw), 1) // HEAD_DIM) == _iota((SUBLANES, hw), 0)
        om_ref[0] = jnp.sum(jnp.where(own_head, o_full, 0.0), axis=0, keepdims=True)


def _dec_attention(cache_moba, cache_nsa, win_buf, pt_flat, qbd, q8n, q8r, g8, newm, news, neww, kc, vc, cover,
                   *, n_req, n_pages):
    _, page, mw = cache_moba.shape
    n_steps = n_pages // 2
    gw = G_NSA * HEAD_DIM
    hw = H_MOBA * HEAD_DIM
    n_chunk = kc.shape[1]
    nwin = win_buf.shape[1]
    req = lambda a: pl.BlockSpec((1,) + a.shape[1:], lambda b, j, pt: (b,) + (0,) * (a.ndim - 1))
    return pl.pallas_call(
        functools.partial(_dec_kernel, n_steps=n_steps, page=page),
        grid_spec=pltpu.PrefetchScalarGridSpec(
            num_scalar_prefetch=1,
            grid=(n_req, n_steps),
            in_specs=[pl.BlockSpec((1, page, mw), lambda b, j, pt: (pt[b * n_pages + 2 * j], 0, 0)),
                      pl.BlockSpec((1, page, mw), lambda b, j, pt: (pt[b * n_pages + 2 * j + 1], 0, 0)),
                      pl.BlockSpec((1, page, 2 * gw), lambda b, j, pt: (pt[b * n_pages + 2 * j], 0, 1)),
                      pl.BlockSpec((1, page, 2 * gw), lambda b, j, pt: (pt[b * n_pages + 2 * j + 1], 0, 1)),
                      req(win_buf), req(qbd), req(q8n), req(q8r), req(g8), req(newm), req(news), req(neww),
                      req(kc), req(vc), pl.BlockSpec(cover.shape, lambda b, j, pt: (0, 0))],
            out_specs=[pl.BlockSpec((1, 1, hw), lambda b, j, pt: (b, 0, 0)),
                       pl.BlockSpec((1, SUBLANES, LANES), lambda b, j, pt: (b, 0, 0))],
            scratch_shapes=[pltpu.VMEM((n_steps, SUBLANES, LANES), F32)] * 3
            + [pltpu.VMEM((n_steps, SUBLANES, hw), F32)]
            + [pltpu.VMEM((SUBLANES, LANES), F32)] * 5),
        out_shape=[jax.ShapeDtypeStruct((n_req, 1, hw), F32), jax.ShapeDtypeStruct((n_req, SUBLANES, LANES), F32)],
        compiler_params=_params("parallel", "arbitrary"),
        name="dec_attention",
    )(pt_flat, cache_moba, cache_moba, cache_nsa, cache_nsa, win_buf, qbd, q8n, q8r, g8, newm, news, neww, kc, vc,
      cover)


def _lane_rep(col):
    return jnp.broadcast_to(col, (col.shape[0], LANES))


def _head_sums(prod_row):
    w = prod_row.shape[1]
    own = (_iota((SUBLANES, w), 1) // HEAD_DIM) == _iota((SUBLANES, w), 0)
    return _lane_rep(jnp.sum(jnp.where(own, jnp.broadcast_to(prod_row, (SUBLANES, w)), 0.0), axis=1, keepdims=True))


def _pair_row(x8, h):
    return jnp.where(_iota((1, LANES), 1) < HEAD_DIM, x8[h:h + 1, :], x8[h + 1:h + 2, :])


def _cols_to_row(acc_a, acc_b):
    return jnp.sum(jnp.concatenate([acc_a, acc_b], axis=0).T, axis=0, keepdims=True)


def _dec2_kernel(pt_ref, cm_ref, cn_ref, wb_ref, q_ref, q8n_ref, new_ref, gate_ref, wcat_ref, pecat_ref, w2_ref,
                 gk_ref, bd_ref, cover_ref, o_ref,
                 mbuf, nbuf, sem, xk, xv, qmb, qrb, s_sc, p_sc, *, n_pages, page):
    b = pl.program_id(0)
    n_req = pl.num_programs(0)
    slot = b % 2
    hw = H_MOBA * HEAD_DIM
    gw = G_NSA * HEAD_DIM
    past = n_pages * page

    def copies(req, sl):
        out = []
        for p in range(n_pages):
            pg = pt_ref[req * n_pages + p]
            out.append(pltpu.make_async_copy(cm_ref.at[pg], mbuf.at[sl, p], sem.at[0, sl]))
            out.append(pltpu.make_async_copy(cn_ref.at[pg], nbuf.at[sl, p], sem.at[1, sl]))
        return out

    @pl.when(b == 0)
    def _():
        for c in copies(0, 0):
            c.start()

    for c in copies(b, slot):
        c.wait()

    @pl.when(b + 1 < n_req)
    def _():
        for c in copies(b + 1, 1 - slot):
            c.start()

    lane1 = _iota((1, LANES), 1)
    lane8 = _iota((SUBLANES, LANES), 1)
    qrow = q_ref[0]
    new = new_ref[0]
    for c in range(hw // LANES):
        cols = slice(c * LANES, (c + 1) * LANES)
        qmb[cols, :] = jnp.broadcast_to(qrow[0:1, cols] * SCALE, (LANES, LANES)).T
        qrb[cols, :] = jnp.broadcast_to(qrow[1:2, cols] * SCALE, (LANES, LANES)).T

    def softmax_pv(scores, extra8, s_new8, v_rows, vbuf_ref, v_row0, per_g):
        del extra8
        m8 = s_new8
        for s in scores:
            m8 = jnp.maximum(m8, _lane_rep(jnp.max(s, axis=1, keepdims=True)))
        w_new = jnp.exp(s_new8 - m8)
        l8 = w_new
        for p, s in enumerate(scores):
            pr = jnp.exp(s - m8)
            p_sc[p] = pr
            l8 = l8 + _lane_rep(jnp.sum(pr, axis=1, keepdims=True))
        inv8 = 1.0 / jnp.maximum(l8, TINY)
        rows = []
        for hp in range(SUBLANES // 2):
            accs = []
            for h in (2 * hp, 2 * hp + 1):
                r0 = v_row0 + (h // R_NSA if per_g else h) * HEAD_DIM

                def body(p, acc, h=h, r0=r0):
                    return acc + vbuf_ref[slot, p, r0:r0 + HEAD_DIM, :] * p_sc[p, h:h + 1, :]
                accs.append(lax.fori_loop(0, n_pages, body, jnp.zeros((HEAD_DIM, LANES), F32)))
            row = _cols_to_row(accs[0], accs[1])
            cols = slice(hp * LANES, (hp + 1) * LANES)
            rows.append((row + _pair_row(w_new, 2 * hp) * v_rows[:, cols]) * _pair_row(inv8, 2 * hp))
        return rows

    def moba_scores(p, carry):
        rows = [jnp.sum(mbuf[slot, p, h * HEAD_DIM:(h + 1) * HEAD_DIM, :] * qmb[h * HEAD_DIM:(h + 1) * HEAD_DIM, :],
                        axis=0, keepdims=True) for h in range(H_MOBA)]
        s_sc[p] = jnp.concatenate(rows, axis=0)
        return carry
    lax.fori_loop(0, n_pages, moba_scores, 0)
    s_all = [s_sc[p] for p in range(n_pages)]
    ppb = MOBA_BLOCK // page
    n_blk = n_pages // ppb
    bsc = []
    for j in range(n_blk):
        tot = s_all[j * ppb]
        for t in range(1, ppb):
            tot = tot + s_all[j * ppb + t]
        bsc.append(_lane_rep(jnp.sum(tot, axis=1, keepdims=True)))
    masked = []
    for j in range(n_blk):
        rank = jnp.zeros((SUBLANES, LANES), F32)
        for c in range(n_blk):
            if c != j:
                ahead = (bsc[c] > bsc[j]) | ((bsc[c] == bsc[j]) & (c < j))
                rank = rank + jnp.where(ahead, 1.0, 0.0)
        for t in range(ppb):
            masked.append(jnp.where(rank < MOBA_TOPK, s_all[j * ppb + t], NEG_BIG))
    s_own = _head_sums(qrow[0:1, :] * new[0:1, :]) * SCALE
    o_rows = softmax_pv(masked, None, s_own, new[1:2, :], mbuf, hw, False)

    for p in range(n_pages):
        xk[p * page:(p + 1) * page, :] = nbuf[slot, p, 0:gw, :].T
        xv[p * page:(p + 1) * page, :] = nbuf[slot, p, gw:2 * gw, :].T
    n_chunk = past // CMP_STRIDE
    cmp_out = []
    for kv, xref in enumerate((xk, xv)):
        xcat = jnp.concatenate([xref[pl.ds(j, n_chunk, stride=CMP_STRIDE), :] for j in range(CMP_STRIDE)], axis=1)
        pe2 = _dot(pecat_ref[kv].astype(BF16), wcat_ref[kv])
        ab = _dot(xcat.astype(BF16), wcat_ref[kv])
        hid = jax.nn.gelu(ab[:, 0:gw] + pe2[0:1, 0:gw] + pltpu.roll(ab[:, gw:2 * gw] + pe2[1:2, gw:2 * gw],
                                                                    n_chunk - 1, 0))
        cmp_out.append(_dot(hid, w2_ref[kv], HIGHEST))
    kc = _head_norm(cmp_out[0], gk_ref[...], bd_ref[...])
    vc = cmp_out[1]

    s = _dot_nt(q8n_ref[0], kc, HIGHEST) * SCALE
    cmask = _iota((SUBLANES, n_chunk), 1) < (n_chunk - 1)
    s = jnp.where(cmask, s, NEG_BIG)
    e = jnp.where(cmask, jnp.exp(s - jnp.max(s, axis=1, keepdims=True)), 0.0)
    p_cmp = e / jnp.maximum(jnp.sum(e, axis=1, keepdims=True), TINY)
    o_cmp8 = _dot(p_cmp.astype(BF16), vc.astype(BF16))
    subc = _iota((SUBLANES, n_chunk), 0)
    g0 = jnp.sum(jnp.where(subc < R_NSA, p_cmp, 0.0), axis=0, keepdims=True)
    g1 = jnp.sum(jnp.where(subc >= R_NSA, p_cmp, 0.0), axis=0, keepdims=True)
    p_grp = jnp.concatenate([jnp.where(subc < R_NSA, g0, g1), jnp.zeros((LANES - SUBLANES, n_chunk), F32)], axis=0)
    imp_t = _dot_nt(cover_ref[...], p_grp, HIGHEST)
    blk_t = _iota((LANES, LANES), 0)
    own = past // SEL_BLOCK
    forced = (blk_t == 0) | (blk_t == own) | (blk_t == own - 1)
    score = jnp.where(blk_t <= own, jnp.where(forced, jnp.inf, imp_t), -jnp.inf)
    picked = _top_k_sublanes(score, SEL_TOPN)
    bias8 = jnp.where((picked > 0.5) & (blk_t <= own), 0.0, NEG_BIG).T[0:SUBLANES, :]
    cmp_rows = []
    for hp in range(H_NSA // 2):
        g = (2 * hp) // R_NSA
        ra = o_cmp8[2 * hp:2 * hp + 1, :]
        rb = o_cmp8[2 * hp + 1:2 * hp + 2, :]
        cmp_rows.append(jnp.where(lane1 < HEAD_DIM, ra if g == 0 else pltpu.roll(ra, HEAD_DIM, 1),
                                  rb if g == 1 else pltpu.roll(rb, HEAD_DIM, 1)))

    def sel_scores(p, carry):
        rows = [jnp.sum(nbuf[slot, p, 2 * gw + (i // R_NSA) * HEAD_DIM:2 * gw + (i // R_NSA + 1) * HEAD_DIM, :]
                        * qrb[i * HEAD_DIM:(i + 1) * HEAD_DIM, :], axis=0, keepdims=True) for i in range(H_NSA)]
        s_sc[p] = jnp.concatenate(rows, axis=0)
        return carry
    lax.fori_loop(0, n_pages, sel_scores, 0)
    bpp = page // SEL_BLOCK
    sel_s = []
    for p in range(n_pages):
        bias_p = bias8[:, p * bpp:p * bpp + 1]
        for t in range(1, bpp):
            bias_p = jnp.where(lane8 < t * SEL_BLOCK, bias_p, bias8[:, p * bpp + t:p * bpp + t + 1])
        sel_s.append(s_sc[p] + bias_p)
    s_new = _head_sums(qrow[1:2, :] * new[2:3, :]) * SCALE
    sel_rows = softmax_pv(sel_s, None, s_new, new[3:4, :], nbuf, 3 * gw, True)

    nw = wb_ref.shape[2]
    wk = nw // LANES
    w_s = []
    for i in range(H_NSA):
        g = i // R_NSA
        qcol = jnp.concatenate([qrb[i * HEAD_DIM:(i + 1) * HEAD_DIM, :]] * wk, axis=1)
        w_s.append(jnp.sum(wb_ref[0, g * HEAD_DIM:(g + 1) * HEAD_DIM, :] * qcol, axis=0, keepdims=True))
    s = jnp.concatenate(w_s, axis=0)
    wmask = _iota((SUBLANES, nw), 1) >= 1
    s = jnp.where(wmask, s, NEG_BIG)
    s_new = _head_sums(qrow[1:2, :] * new[4:5, :]) * SCALE
    m8 = jnp.maximum(_lane_rep(jnp.max(s, axis=1, keepdims=True)), s_new)
    pw = jnp.where(wmask, jnp.exp(s - m8[:, 0:1]), 0.0)
    w_new = jnp.exp(s_new - m8)
    inv8 = 1.0 / jnp.maximum(_lane_rep(jnp.sum(pw, axis=1, keepdims=True)) + w_new, TINY)
    win_rows = []
    for hp in range(H_NSA // 2):
        accs = []
        for i in (2 * hp, 2 * hp + 1):
            g = i // R_NSA
            prod = wb_ref[0, gw + g * HEAD_DIM:gw + (g + 1) * HEAD_DIM, :] * pw[i:i + 1, :]
            acc = prod[:, 0:LANES]
            for c in range(1, wk):
                acc = acc + prod[:, c * LANES:(c + 1) * LANES]
            accs.append(acc)
        cols = slice(hp * LANES, (hp + 1) * LANES)
        win_rows.append((_cols_to_row(accs[0], accs[1]) + _pair_row(w_new, 2 * hp) * new[5:6, cols])
                        * _pair_row(inv8, 2 * hp))

    gates = gate_ref[0]
    for c in range(hw // LANES):
        o_ref[0, :, c * LANES:(c + 1) * LANES] = o_rows[c]
    for c in range(W_QN // LANES):
        cols = slice(c * LANES, (c + 1) * LANES)
        o_ref[0, :, hw + c * LANES:hw + (c + 1) * LANES] = (
            gates[0:1, cols] * cmp_rows[c] + gates[1:2, cols] * sel_rows[c] + gates[2:3, cols] * win_rows[c])


def _dec2_attention(cm, cn, wb, pt_flat, qrows, q8n, new, gate_rows, wcat, pecat, w2bd, gk, bd, cover,
                    *, n_req, n_pages):
    page = cm.shape[2]
    n_chunk = n_pages * page // CMP_STRIDE
    req = lambda a: pl.BlockSpec((1,) + a.shape[1:], lambda b, pt: (b,) + (0,) * (a.ndim - 1))
    const = lambda a: pl.BlockSpec(a.shape, lambda b, pt: (0,) * a.ndim)
    return pl.pallas_call(
        functools.partial(_dec2_kernel, n_pages=n_pages, page=page),
        grid_spec=pltpu.PrefetchScalarGridSpec(
            num_scalar_prefetch=1,
            grid=(n_req,),
            in_specs=[pl.BlockSpec(memory_space=pl.ANY), pl.BlockSpec(memory_space=pl.ANY), req(wb), req(qrows),
                      req(q8n), req(new), req(gate_rows), const(wcat), const(pecat), const(w2bd), const(gk),
                      const(bd), const(cover)],
            out_specs=pl.BlockSpec((1, 1, W_QM + W_QN), lambda b, pt: (b, 0, 0)),
            scratch_shapes=[pltpu.VMEM((2, n_pages) + cm.shape[1:], F32), pltpu.VMEM((2, n_pages) + cn.shape[1:], F32),
                            pltpu.SemaphoreType.DMA((2, 2)),
                            pltpu.VMEM((n_pages * page, LANES), F32), pltpu.VMEM((n_pages * page, LANES), F32),
                            pltpu.VMEM((W_QM, LANES), F32), pltpu.VMEM((W_QN, LANES), F32),
                            pltpu.VMEM((n_pages, SUBLANES, LANES), F32), pltpu.VMEM((n_pages, SUBLANES, LANES), F32)]),
        out_shape=jax.ShapeDtypeStruct((n_req, 1, W_QM + W_QN), F32),
        compiler_params=_params("arbitrary"),
        name="dec_attention",
    )(pt_flat, cm, cn, wb, qrows, q8n, new, gate_rows, wcat, pecat, w2bd, gk, bd, cover)


def _compress_cat_consts(cmp_pos, cmp_w1):
    eye = jnp.eye(G_NSA, dtype=F32)
    w1 = cmp_w1.reshape(2, 2, CMP_STRIDE, HEAD_DIM, HEAD_DIM)
    wcat = jnp.einsum("gh,kajde->kjgdahe", eye, w1).reshape(2, CMP_STRIDE * G_NSA * HEAD_DIM, 2 * G_NSA * HEAD_DIM)
    pe = jnp.tile(cmp_pos.reshape(2, 2, CMP_STRIDE, 1, HEAD_DIM), (1, 1, 1, G_NSA, 1))
    pecat = jnp.pad(pe.reshape(2, 2, CMP_STRIDE * G_NSA * HEAD_DIM), ((0, 0), (0, SUBLANES - 2), (0, 0)))
    return wcat.astype(BF16), pecat


def _post_kernel(o_ref, x_ref, gt_ref, sh_ref, sc_ref, g_ref, wo_ref, wr_ref, br_ref,
                 y_ref, h3_ref, te_ref, tw_ref):
    y = x_ref[...] + gt_ref[0] * _dot(o_ref[...], wo_ref[...])
    y_ref[...] = y
    h = y * lax.rsqrt(jnp.mean(y * y, axis=1, keepdims=True) + NORM_EPS) * g_ref[...]
    h = h * (1.0 + sc_ref[0]) + sh_ref[0]
    tm = h.shape[0]
    for s in range(h.shape[1] // LANES):
        h3_ref[pl.ds(s, tm, stride=SUBLANES), :] = h[:, s * LANES:(s + 1) * LANES]
    logits = _dot(h, wr_ref[...], HIGHEST) + br_ref[...]
    _, vals, ids = _top_k_lanes(logits, TOP_K)
    lane = _iota((tm, LANES), 1)
    es = [jnp.exp(v - vals[0]) for v in vals]
    den = es[0]
    for e in es[1:]:
        den = den + e
    te = jnp.zeros((tm, LANES), F32)
    tw = jnp.zeros((tm, LANES), F32)
    for k in range(TOP_K):
        te = jnp.where(lane == k, ids[k], te)
        tw = jnp.where(lane == k, es[k] / den, tw)
    te_ref[...] = te.astype(jnp.int32)
    tw_ref[...] = tw


def _post(o, x, gate, shift, scale, g2, wo_bf, wr_pad, br_pad, *, tm, rows_per_mod):
    n, d = x.shape
    mod_r = gate.shape[1]
    mod_map = lambda i: (i // (rows_per_mod // tm), 0, 0)
    row = lambda w: pl.BlockSpec((tm, w), lambda i: (i, 0))
    const = lambda a: pl.BlockSpec(a.shape, lambda i: (0,) * a.ndim)
    mod = pl.BlockSpec((1, mod_r, d), mod_map)
    return pl.pallas_call(
        _post_kernel,
        grid=(n // tm,),
        in_specs=[row(d), row(d), mod, mod, mod, const(g2), const(wo_bf), const(wr_pad), const(br_pad)],
        out_specs=[row(d), pl.BlockSpec((tm * SUBLANES, LANES), lambda i: (i, 0)), row(LANES), row(LANES)],
        out_shape=[jax.ShapeDtypeStruct((n, d), F32), jax.ShapeDtypeStruct((n * SUBLANES, LANES), F32),
                   jax.ShapeDtypeStruct((n, LANES), jnp.int32), jax.ShapeDtypeStruct((n, LANES), F32)],
        compiler_params=_params("parallel"),
        name="post",
    )(o, x, gate, shift, scale, g2, wo_bf, wr_pad, br_pad)


def _expert_kernel(be_ref, na_ref, tok_ref, h3_ref, wgu_ref, bgu_ref, wd_ref, bd_ref, y_ref, buf, xb, sem):
    i = pl.program_id(0)
    n_active = na_ref[0]
    rows8 = MOE_ROWS * SUBLANES

    def gather(blk, slot):
        def body(r, carry):
            t = tok_ref[blk * MOE_ROWS + r]
            pltpu.make_async_copy(h3_ref.at[pl.ds(pl.multiple_of(t * SUBLANES, SUBLANES), SUBLANES), :],
                                  buf.at[slot, pl.ds(pl.multiple_of(r * SUBLANES, SUBLANES), SUBLANES), :],
                                  sem.at[slot]).start()
            return carry
        lax.fori_loop(0, MOE_ROWS, body, 0)

    @pl.when(i == 0)
    def _():
        gather(0, 0)

    @pl.when(i < n_active)
    def _():
        slot = i % 2
        pltpu.make_async_copy(h3_ref.at[pl.ds(0, rows8), :], buf.at[slot], sem.at[slot]).wait()

        @pl.when(i + 1 < n_active)
        def _():
            gather(i + 1, 1 - slot)

        d = xb.shape[1]
        for s in range(d // LANES):
            xb[:, s * LANES:(s + 1) * LANES] = buf[slot, pl.ds(s, MOE_ROWS, stride=SUBLANES), :].astype(BF16)
        gu = _dot(xb[...], wgu_ref[0]) + bgu_ref[0]
        f = gu.shape[1] // 2
        gt = jnp.minimum(gu[:, 0:f], SWIGLU_LIMIT)
        up = jnp.clip(gu[:, f:2 * f], -SWIGLU_LIMIT, SWIGLU_LIMIT)
        act = (up + 1.0) * (gt * jax.nn.sigmoid(SWIGLU_ALPHA * gt))
        y = _dot(act.astype(BF16), wd_ref[0]) + bd_ref[0]
        for s in range(d // LANES):
            y_ref[pl.ds(s, MOE_ROWS, stride=SUBLANES), :] = y[:, s * LANES:(s + 1) * LANES]

    @pl.when(i >= n_active)
    def _():
        y_ref[...] = jnp.zeros(y_ref.shape, F32)


def _experts(blk_e, n_active, tok, h3, wgu_bf, b_gu, wd_bf, b_down):
    n_blocks = blk_e.shape[0]
    e, d, f2 = wgu_bf.shape
    rows8 = MOE_ROWS * SUBLANES
    return pl.pallas_call(
        _expert_kernel,
        grid_spec=pltpu.PrefetchScalarGridSpec(
            num_scalar_prefetch=3,
            grid=(n_blocks,),
            in_specs=[pl.BlockSpec(memory_space=pl.ANY),
                      pl.BlockSpec((1, d, f2), lambda i, be, na, tok: (be[i], 0, 0)),
                      pl.BlockSpec((1, 1, f2), lambda i, be, na, tok: (be[i], 0, 0)),
                      pl.BlockSpec((1, f2 // 2, d), lambda i, be, na, tok: (be[i], 0, 0)),
                      pl.BlockSpec((1, 1, d), lambda i, be, na, tok: (be[i], 0, 0))],
            out_specs=pl.BlockSpec((rows8, LANES), lambda i, be, na, tok: (i, 0)),
            scratch_shapes=[pltpu.VMEM((2, rows8, LANES), F32), pltpu.VMEM((MOE_ROWS, d), BF16),
                            pltpu.SemaphoreType.DMA((2,))]),
        out_shape=jax.ShapeDtypeStruct((n_blocks * rows8, LANES), F32),
        compiler_params=_params("arbitrary"),
        name="experts",
    )(blk_e, n_active, tok, h3, wgu_bf, b_gu.reshape(e, 1, f2), wd_bf, b_down.reshape(e, 1, d))


def _combine_kernel(pos_ref, ys_ref, y1_ref, gt_ref, tw_ref, o_ref, buf, sem, *, tile_off):
    i = pl.program_id(0)
    n = pl.num_programs(0)
    tm = y1_ref.shape[0]
    rows8 = tm * SUBLANES

    def gather(tile, slot):
        def body(r, carry):
            for k in range(TOP_K):
                p = pos_ref[((tile + tile_off) * tm + r) * TOP_K + k]
                pltpu.make_async_copy(ys_ref.at[pl.ds(pl.multiple_of(p * SUBLANES, SUBLANES), SUBLANES), :],
                                      buf.at[slot, k, pl.ds(pl.multiple_of(r * SUBLANES, SUBLANES), SUBLANES), :],
                                      sem.at[slot]).start()
            return carry
        lax.fori_loop(0, tm, body, 0)

    @pl.when(i == 0)
    def _():
        gather(0, 0)

    slot = i % 2
    for k in range(TOP_K):
        pltpu.make_async_copy(ys_ref.at[pl.ds(0, rows8), :], buf.at[slot, k], sem.at[slot]).wait()

    @pl.when(i + 1 < n)
    def _():
        gather(i + 1, 1 - slot)

    tw = tw_ref[...]
    wk = [jnp.broadcast_to(tw[:, k:k + 1], (tm, LANES)) for k in range(TOP_K)]
    gt = gt_ref[0]
    for s in range(o_ref.shape[1] // LANES):
        moe = wk[0] * buf[slot, 0, pl.ds(s, tm, stride=SUBLANES), :]
        for k in range(1, TOP_K):
            moe = moe + wk[k] * buf[slot, k, pl.ds(s, tm, stride=SUBLANES), :]
        cols = slice(s * LANES, (s + 1) * LANES)
        o_ref[:, cols] = y1_ref[:, cols] + gt[:, cols] * moe


def _combine(pos_flat, ys, y1, gate, tw, *, tm, rows_per_mod, tile_off):
    n, d = y1.shape
    mod_r = gate.shape[1]
    return pl.pallas_call(
        functools.partial(_combine_kernel, tile_off=tile_off),
        grid_spec=pltpu.PrefetchScalarGridSpec(
            num_scalar_prefetch=1,
            grid=(n // tm,),
            in_specs=[pl.BlockSpec(memory_space=pl.ANY),
                      pl.BlockSpec((tm, d), lambda i, pos: (i, 0)),
                      pl.BlockSpec((1, mod_r, d), lambda i, pos: (i // (rows_per_mod // tm), 0, 0)),
                      pl.BlockSpec((tm, LANES), lambda i, pos: (i, 0))],
            out_specs=pl.BlockSpec((tm, d), lambda i, pos: (i, 0)),
            scratch_shapes=[pltpu.VMEM((2, TOP_K, tm * SUBLANES, LANES), F32), pltpu.SemaphoreType.DMA((2,))]),
        out_shape=jax.ShapeDtypeStruct((n, d), F32),
        compiler_params=_params("arbitrary"),
        name="combine",
    )(pos_flat, ys, y1, gate, tw)


def _rope_tables(pos):
    half = HEAD_DIM // 2
    inv = ROPE_THETA ** (-jnp.arange(half, dtype=F32) / half)
    ang = pos.astype(F32)[:, None] * inv[None, :]
    cos = jnp.cos(ang)
    sin = jnp.sin(ang)
    reps = LANES // HEAD_DIM
    return (jnp.tile(jnp.concatenate([cos, cos], axis=1), (1, reps)),
            jnp.tile(jnp.concatenate([-sin, sin], axis=1), (1, reps)))


def _routing(top_e, n_tok):
    n_assign = n_tok * TOP_K
    e_flat = top_e.reshape(-1)
    order = jnp.argsort(e_flat)
    e_sorted = e_flat[order]
    tok_sorted = (order // TOP_K).astype(jnp.int32)
    counts = jnp.bincount(e_flat, length=N_EXPERTS)
    padded = (counts + MOE_ROWS - 1) // MOE_ROWS * MOE_ROWS
    pad_end = jnp.cumsum(padded)
    pad_start = pad_end - padded
    sort_start = jnp.cumsum(counts) - counts
    dest = (pad_start[e_sorted] + jnp.arange(n_assign, dtype=jnp.int32) - sort_start[e_sorted]).astype(jnp.int32)
    n_blocks = -(-n_assign // MOE_ROWS) + N_EXPERTS
    tok = jnp.full((n_blocks * MOE_ROWS,), n_tok, jnp.int32).at[dest].set(tok_sorted)
    blk_e = jnp.minimum(jnp.searchsorted(pad_end, jnp.arange(n_blocks, dtype=jnp.int32) * MOE_ROWS, side='right'),
                        N_EXPERTS - 1).astype(jnp.int32)
    pos = jnp.zeros((n_assign,), jnp.int32).at[order].set(dest)
    n_active = (pad_end[-1] // MOE_ROWS).astype(jnp.int32).reshape(1)
    return blk_e, n_active, tok, pos


def _head_dup(a):
    return jnp.concatenate([a, a], axis=-1)


def _head_pad(a):
    return jnp.concatenate([a, jnp.zeros_like(a)], axis=-1)


def kernel(x_prompt, x_sample, c_prompt, c_sample, cache_moba_kv, cache_nsa_kv, state_nsa_win_kv, page_table,
           norm_g, w_ada, b_ada, w_in, qk_gain, cmp_pos, cmp_w1, cmp_w2, w_out, w_router, b_router, w_gu, b_gu,
           w_down, b_down):
    bsz, seq, d = x_prompt.shape
    n_req = x_sample.shape[0]
    depth = norm_g.shape[0]
    assert depth == 1 and x_sample.shape[1] == 1
    assert seq % ATT_TK == 0 and seq // SEL_BLOCK <= LANES and seq // MOBA_BLOCK <= MOBA_BLOCK // SUBLANES
    n_pool, page = cache_moba_kv.shape[1], cache_moba_kv.shape[2]
    n_pages = page_table.shape[1]
    past = n_pages * page
    assert past % MOBA_BLOCK == 0 and 2 * page == MOBA_BLOCK and past // SEL_BLOCK < LANES
    assert state_nsa_win_kv.shape[2] == WINDOW
    layer = 0
    gw = G_NSA * HEAD_DIM
    n_prompt = bsz * seq

    bd = jnp.asarray(np.kron(np.eye(LANES // HEAD_DIM), np.full((HEAD_DIM, HEAD_DIM), 1.0 / HEAD_DIM)), BF16)
    w_in_bf = jnp.pad(w_in[layer], ((0, 0), (0, IN_COLS_PAD - IN_COLS))).astype(BF16)
    gains = jnp.tile(qk_gain[layer], (1, W_QM // HEAD_DIM))
    g1 = norm_g[layer, 0].reshape(1, d)
    g2 = norm_g[layer, 1].reshape(1, d)
    wo_bf = w_out[layer].astype(BF16)
    wr_pad = jnp.pad(w_router[layer], ((0, 0), (0, LANES - N_EXPERTS)))
    br_pad = jnp.pad(b_router[layer].reshape(1, N_EXPERTS), ((0, 0), (0, LANES - N_EXPERTS)),
                     constant_values=-jnp.inf)
    wgu_bf = w_gu[layer].astype(BF16)
    wd_bf = w_down[layer].astype(BF16)
    cmp_consts = _compress_consts(cmp_pos[layer], cmp_w1[layer], cmp_w2[layer], qk_gain[layer, 3])

    n_c = bsz + n_req
    n_c_pad = -(-n_c // SUBLANES) * SUBLANES
    c_all = jnp.pad(jnp.concatenate([c_prompt, c_sample], axis=0), ((0, n_c_pad - n_c), (0, 0)))
    mods = _ada(c_all, w_ada[layer], b_ada[layer])
    mods_p = [m.reshape(bsz, 1, d) for m in jnp.split(mods[:bsz], 6, axis=1)]
    mods_s = [m.reshape(1, n_req, d) for m in jnp.split(mods[bsz:n_c], 6, axis=1)]

    cos_p, sin_p = _rope_tables(jnp.arange(seq, dtype=jnp.int32))
    tm_p = 256
    moba_rows, nsa_rows, win_rows, q_m, qn, qr, gates, kmean = _proj(
        x_prompt.reshape(n_prompt, d), mods_p[0], mods_p[1], g1, w_in_bf, gains, bd, cos_p, sin_p,
        tm=tm_p, rows_per_mod=seq, pos_blocks=seq // tm_p, with_kmean=True)

    nbk = seq // MOBA_BLOCK
    pos = jnp.arange(seq, dtype=jnp.int32)
    k_m = moba_rows[:, :W_QM].reshape(bsz, seq, H_MOBA, HEAD_DIM).astype(BF16)
    oh_m = (pos[:, None] // MOBA_BLOCK == jnp.arange(MOBA_BLOCK // SUBLANES)[None, :]).astype(BF16)
    oh_m = jnp.broadcast_to(oh_m[None, :, None, :], (bsz, seq, H_MOBA, MOBA_BLOCK // SUBLANES))
    k_aug = jnp.concatenate([k_m, oh_m, jnp.zeros_like(oh_m)], axis=-1).reshape(bsz, seq, H_MOBA * LANES)
    v_m = moba_rows[:, W_QM:].reshape(bsz, seq, W_QM).astype(BF16)
    kmean_h = kmean.reshape(bsz, nbk, H_MOBA, HEAD_DIM).transpose(0, 2, 1, 3)
    kmp = jnp.zeros((bsz, H_MOBA, LANES, LANES), F32).at[:, :, HEAD_DIM:HEAD_DIM + nbk, :HEAD_DIM].set(kmean_h)
    o_m = _moba_attention(q_m.reshape(bsz, seq, W_QM), kmp, k_aug, v_m)

    nsa4 = nsa_rows.reshape(bsz, seq, 4, G_NSA, HEAD_DIM)
    win4 = win_rows.reshape(bsz, seq, 2, G_NSA, HEAD_DIM)
    kc, vc = _compress_prompt(nsa_rows.reshape(bsz, seq, 4 * gw), cmp_consts, bd)
    n_chunk = seq // CMP_STRIDE
    per_g = lambda a: a.reshape(bsz, -1, G_NSA, HEAD_DIM).transpose(0, 2, 1, 3)
    kcp = _head_pad(per_g(kc))
    vcd = _head_dup(per_g(vc)).astype(BF16)
    oh_s = (pos[:, None] // SEL_BLOCK == jnp.arange(LANES)[None, :]).astype(BF16)
    k_sel = per_g(nsa4[:, :, 2]).astype(BF16)
    ks_aug = jnp.concatenate([jnp.broadcast_to(oh_s[None, None], (bsz, G_NSA, seq, LANES)), k_sel,
                              jnp.zeros_like(k_sel)], axis=-1)
    vs_dup = _head_dup(per_g(nsa4[:, :, 3])).astype(BF16)
    kw_pad = _head_pad(per_g(win4[:, :, 0])).astype(BF16)
    vw_dup = _head_dup(per_g(win4[:, :, 1])).astype(BF16)
    gates_g = jnp.pad(gates[:, :N_GATE].reshape(bsz, seq, G_NSA, 3 * R_NSA).transpose(0, 2, 1, 3),
                      ((0, 0), (0, 0), (0, 0), (0, LANES - 3 * R_NSA)))
    cover_p = _cover(n_chunk - 1, seq // SEL_BLOCK, n_chunk).T
    o_n = _nsa_attention(qn.reshape(bsz, seq, W_QN), qr.reshape(bsz, seq, W_QN), gates_g, kcp, vcd, cover_p,
                         ks_aug, vs_dup, kw_pad, vw_dup)
    o_p = jnp.concatenate([o_m, o_n], axis=-1).reshape(n_prompt, d)

    y1_p, h3_p, te_p, tw_p = _post(o_p, x_prompt.reshape(n_prompt, d), mods_p[2], mods_p[3], mods_p[4], g2, wo_bf,
                                   wr_pad, br_pad, tm=256, rows_per_mod=seq)

    cos_s, sin_s = _rope_tables(jnp.full((n_req,), past, jnp.int32))
    moba_new, nsa_new, win_new, q_m_s, qn_s, qr_s, gates_s = _proj(
        x_sample.reshape(n_req, d), mods_s[0], mods_s[1], g1, w_in_bf, gains, bd, cos_s, sin_s,
        tm=n_req, rows_per_mod=n_req, pos_blocks=1, with_kmean=False)
    pt_flat = page_table.reshape(-1).astype(jnp.int32)
    cache_m = cache_moba_kv[layer].transpose(0, 2, 3, 4, 1).reshape(n_pool, W_KVM, page)
    cache_n = cache_nsa_kv[layer].transpose(0, 2, 3, 4, 1).reshape(n_pool, 4 * gw, page)
    win_buf = state_nsa_win_kv[layer].transpose(0, 2, 3, 4, 1).reshape(n_req, 2 * gw, WINDOW)

    def rows8(q):
        qh = q.reshape(n_req, G_NSA, R_NSA, 1, HEAD_DIM)
        place = jnp.arange(G_NSA)[None, :, None, None, None] == jnp.arange(G_NSA)[None, None, None, :, None]
        return jnp.where(place, qh, 0.0).reshape(n_req, H_NSA, gw)

    def per_head(a):
        return jnp.repeat(a.reshape(n_req, G_NSA, HEAD_DIM), R_NSA, axis=1).reshape(n_req, W_QN)

    new_rows = jnp.stack([moba_new[:, :W_QM], moba_new[:, W_QM:], per_head(nsa_new[:, 2 * gw:3 * gw]),
                          per_head(nsa_new[:, 3 * gw:]), per_head(win_new[:, :gw]), per_head(win_new[:, gw:])], axis=1)
    gate_rows = jnp.repeat(gates_s[:, :N_GATE].reshape(n_req, H_NSA, 3).transpose(0, 2, 1), HEAD_DIM, axis=2)
    n_cmp_s = past // CMP_STRIDE
    cover_s = _cover(n_cmp_s - 1, past // SEL_BLOCK + 1, n_cmp_s).T
    wcat, pecat = _compress_cat_consts(cmp_pos[layer], cmp_w1[layer])
    o_s = _dec2_attention(cache_m, cache_n, win_buf, pt_flat, jnp.stack([q_m_s, qr_s], axis=1), rows8(qn_s), new_rows,
                          gate_rows, wcat, pecat, cmp_consts[2], cmp_consts[3], bd, cover_s,
                          n_req=n_req, n_pages=n_pages).reshape(n_req, d).astype(BF16)
    y1_s, h3_s, te_s, tw_s = _post(o_s, x_sample.reshape(n_req, d), mods_s[2], mods_s[3], mods_s[4], g2, wo_bf,
                                   wr_pad, br_pad, tm=n_req, rows_per_mod=n_req)

    n_tok = n_prompt + n_req
    h3 = jnp.concatenate([h3_p, h3_s, jnp.zeros((SUBLANES, LANES), F32)], axis=0)
    top_e = jnp.concatenate([te_p[:, :TOP_K], te_s[:, :TOP_K]], axis=0)
    blk_e, n_active, tok, pos_flat = _routing(top_e, n_tok)
    ys = _experts(blk_e, n_active, tok, h3, wgu_bf, b_gu[layer], wd_bf, b_down[layer])
    tm_c = 128
    y_p = _combine(pos_flat, ys, y1_p, mods_p[5], tw_p, tm=tm_c, rows_per_mod=seq, tile_off=0)
    y_s = _combine(pos_flat, ys, y1_s, mods_s[5], tw_s, tm=n_req, rows_per_mod=n_req, tile_off=n_prompt // n_req)

    keep = min(WINDOW, seq)
    win_p = win_rows.reshape(bsz, seq, 2, G_NSA, HEAD_DIM)[:, seq - keep:]
    win_s = jnp.concatenate([state_nsa_win_kv[layer][:, 1:], win_new.reshape(n_req, 1, 2, G_NSA, HEAD_DIM)], axis=1)
    return (y_p.reshape(bsz, seq, d), y_s.reshape(n_req, 1, d),
            moba_rows.reshape(1, bsz, seq, 2, H_MOBA, HEAD_DIM), nsa_rows.reshape(1, bsz, seq, 4, G_NSA, HEAD_DIM),
            win_p[None], moba_new.reshape(1, n_req, 1, 2, H_MOBA, HEAD_DIM),
            nsa_new.reshape(1, n_req, 1, 4, G_NSA, HEAD_DIM), win_s[None])
```

```python
import functools

import numpy as np
import jax
import jax.numpy as jnp
from jax import lax
from jax.experimental import pallas as pl
from jax.experimental.pallas import tpu as pltpu

F32 = jnp.float32
BF16 = jnp.bfloat16
HIGHEST = lax.Precision.HIGHEST

LANES = 128
SUBLANES = 8
HEAD_DIM = 64
H_MOBA = 8
H_NSA = 8
G_NSA = 2
R_NSA = H_NSA // G_NSA
MOBA_BLOCK = 256
MOBA_TOPK = 3
CMP_LEN = 32
CMP_STRIDE = 16
SEL_BLOCK = 64
SEL_TOPN = 16
WINDOW = 512
N_EXPERTS = 32
TOP_K = 4
SWIGLU_LIMIT = 7.0
SWIGLU_ALPHA = 1.702
ROPE_THETA = 10000.0
NORM_EPS = 1e-6
NEG_BIG = -1e30
TINY = 1e-30
SCALE = HEAD_DIM ** -0.5
SCALE_LOG2E = SCALE * 1.4426950408889634
Q_TILE = 128
MOBA_Q = MOBA_BLOCK
ATT_TK = 1024
MOE_ROWS = 128
VMEM_LIMIT = 56 * 1024 * 1024

W_QM = H_MOBA * HEAD_DIM
W_KVM = 2 * H_MOBA * HEAD_DIM
W_QN = H_NSA * HEAD_DIM
W_KVN = 6 * G_NSA * HEAD_DIM
N_GATE = 3 * H_NSA
IN_COLS = W_QM + W_KVM + W_QN + W_KVN + N_GATE
IN_COLS_PAD = W_QM + W_KVM + W_QN + W_KVN + LANES


def _iota(shape, dim):
    return lax.broadcasted_iota(jnp.int32, shape, dim)


def _dot(a, b, precision=None):
    return jnp.dot(a, b, preferred_element_type=F32, precision=precision)


def _dot_nt(a, b, precision=None):
    return lax.dot_general(a, b, (((1,), (1,)), ((), ())), preferred_element_type=F32, precision=precision)


def _split_bf16(a):
    hi = a.astype(BF16)
    return hi, (a - hi.astype(F32)).astype(BF16)


def _dot_nt_x3(a, b):
    ah, al = _split_bf16(a)
    bh, bl = _split_bf16(b)
    return _dot_nt(ah, bh) + _dot_nt(al, bh) + _dot_nt(ah, bl)


def _params(*sem):
    return pltpu.CompilerParams(dimension_semantics=sem, vmem_limit_bytes=VMEM_LIMIT)


def _seg_meansq(z, bd):
    zz = z * z
    hi = zz.astype(BF16)
    lo = (zz - hi.astype(F32)).astype(BF16)
    outs = []
    for c in range(z.shape[1] // LANES):
        sl = slice(c * LANES, (c + 1) * LANES)
        outs.append(_dot(hi[:, sl], bd) + _dot(lo[:, sl], bd))
    return outs[0] if len(outs) == 1 else jnp.concatenate(outs, axis=1)


def _head_norm(z, gain, bd):
    return z * lax.rsqrt(_seg_meansq(z, bd) + NORM_EPS) * gain


def _rope(z, cos, sin):
    outs = []
    first = (_iota((z.shape[0], LANES), 1) % HEAD_DIM) < (HEAD_DIM // 2)
    for c in range(z.shape[1] // LANES):
        x = z[:, c * LANES:(c + 1) * LANES]
        swapped = jnp.where(first, pltpu.roll(x, LANES - HEAD_DIM // 2, 1), pltpu.roll(x, HEAD_DIM // 2, 1))
        outs.append(x * cos + swapped * sin)
    return outs[0] if len(outs) == 1 else jnp.concatenate(outs, axis=1)


def _top_k_lanes(cur, k):
    lane = _iota(cur.shape, 1).astype(F32)
    picked = jnp.zeros(cur.shape, F32)
    vals, ids = [], []
    for _ in range(k):
        mx = jnp.max(cur, axis=1, keepdims=True)
        first = jnp.min(jnp.where(cur == mx, lane, 1e9), axis=1, keepdims=True)
        hit = lane == first
        picked = jnp.where(hit, 1.0, picked)
        cur = jnp.where(hit, -jnp.inf, cur)
        vals.append(mx)
        ids.append(first)
    return picked, vals, ids


def _top_k_sublanes(cur, k):
    idx = _iota(cur.shape, 0).astype(F32)
    picked = jnp.zeros(cur.shape, F32)
    for _ in range(k):
        mx = jnp.max(cur, axis=0, keepdims=True)
        first = jnp.min(jnp.where(cur == mx, idx, 1e9), axis=0, keepdims=True)
        hit = idx == first
        picked = jnp.where(hit, 1.0, picked)
        cur = jnp.where(hit, -jnp.inf, cur)
    return picked


def _flash_step(q, k, v, mask, m, acc):
    s = _dot_nt(q, k)
    if mask is not None:
        s = jnp.where(mask, s, NEG_BIG)
    m_new = jnp.maximum(m, jnp.max(s, axis=1, keepdims=True))
    acc_new = jnp.exp2(m - m_new) * acc + _dot(jnp.exp2(s - m_new).astype(BF16), v)
    return m_new, acc_new


def _flash_finish(acc):
    return acc / jnp.maximum(acc[:, HEAD_DIM:HEAD_DIM + 1], TINY)


def _ada_kernel(c_ref, w_ref, b_ref, o_ref):
    c = c_ref[...]
    o_ref[...] = _dot(c * jax.nn.sigmoid(c), w_ref[...], HIGHEST) + b_ref[...]


def _ada(c_all, w_ada, b_ada):
    n, d = c_all.shape
    cols = w_ada.shape[1]
    tn = 1024
    return pl.pallas_call(
        _ada_kernel,
        grid=(cols // tn,),
        in_specs=[pl.BlockSpec((n, d), lambda j: (0, 0)),
                  pl.BlockSpec((d, tn), lambda j: (0, j)),
                  pl.BlockSpec((1, tn), lambda j: (0, j))],
        out_specs=pl.BlockSpec((n, tn), lambda j: (0, j)),
        out_shape=jax.ShapeDtypeStruct((n, cols), F32),
        compiler_params=_params("arbitrary"),
        name="ada",
    )(c_all, w_ada, b_ada.reshape(1, cols))


def _proj_kernel(x_ref, sh_ref, sc_ref, g_ref, w_ref, gains_ref, bd_ref, cos_ref, sin_ref,
                 moba_ref, nsa_ref, win_ref, qm_ref, qn_ref, qr_ref, gate_ref, *maybe_kmean, with_kmean):
    x = x_ref[...]
    y = x * lax.rsqrt(jnp.mean(x * x, axis=1, keepdims=True) + NORM_EPS) * g_ref[...]
    h = (y * (1.0 + sc_ref[0]) + sh_ref[0]).astype(BF16)
    bd = bd_ref[...]
    cos = cos_ref[...]
    sin = sin_ref[...]
    o = 0

    def seg(width):
        nonlocal o
        z = _dot(h, w_ref[:, o:o + width])
        o += width
        return z

    def gain(i, width):
        return gains_ref[i:i + 1, 0:width]

    qm_ref[...] = _rope(_head_norm(seg(W_QM), gain(0, W_QM), bd), cos, sin)
    k_m = _rope(_head_norm(seg(W_QM), gain(1, W_QM), bd), cos, sin)
    moba_ref[:, 0:W_QM] = k_m
    moba_ref[:, W_QM:2 * W_QM] = seg(W_QM)
    qn = _head_norm(seg(W_QN), gain(2, W_QN), bd)
    qn_ref[...] = qn
    qr_ref[...] = _rope(qn, cos, sin)
    gw = G_NSA * HEAD_DIM
    nsa_ref[:, 0:2 * gw] = seg(2 * gw)
    nsa_ref[:, 2 * gw:3 * gw] = _rope(_head_norm(seg(gw), gain(4, gw), bd), cos, sin)
    nsa_ref[:, 3 * gw:4 * gw] = seg(gw)
    win_ref[:, 0:gw] = _rope(_head_norm(seg(gw), gain(5, gw), bd), cos, sin)
    win_ref[:, gw:2 * gw] = seg(gw)
    gate_ref[...] = jax.nn.sigmoid(seg(LANES))
    if with_kmean:
        (kmean_ref,) = maybe_kmean
        tm = k_m.shape[0]
        kmean_ref[0] = jnp.mean(k_m.reshape(tm // MOBA_BLOCK, MOBA_BLOCK, W_QM), axis=1)


def _proj(x, shift, scale, g, w_in_bf, gains, bd, cos, sin, *, tm, rows_per_mod, pos_blocks, with_kmean):
    n, d = x.shape
    nt = n // tm
    mod_r = shift.shape[1]
    mod_map = lambda i: (i // (rows_per_mod // tm), 0, 0)
    pos_map = lambda i: (i % pos_blocks, 0)
    row = lambda w: pl.BlockSpec((tm, w), lambda i: (i, 0))
    const = lambda a: pl.BlockSpec(a.shape, lambda i: (0,) * a.ndim)
    out_shapes = [jax.ShapeDtypeStruct((n, W_KVM), F32), jax.ShapeDtypeStruct((n, 4 * G_NSA * HEAD_DIM), F32),
                  jax.ShapeDtypeStruct((n, 2 * G_NSA * HEAD_DIM), F32), jax.ShapeDtypeStruct((n, W_QM), F32),
                  jax.ShapeDtypeStruct((n, W_QN), F32), jax.ShapeDtypeStruct((n, W_QN), F32),
                  jax.ShapeDtypeStruct((n, LANES), F32)]
    out_specs = [row(W_KVM), row(4 * G_NSA * HEAD_DIM), row(2 * G_NSA * HEAD_DIM), row(W_QM), row(W_QN), row(W_QN),
                 row(LANES)]
    if with_kmean:
        nbt = tm // MOBA_BLOCK
        out_shapes.append(jax.ShapeDtypeStruct((nt, nbt, W_QM), F32))
        out_specs.append(pl.BlockSpec((1, nbt, W_QM), lambda i: (i, 0, 0)))
    return pl.pallas_call(
        functools.partial(_proj_kernel, with_kmean=with_kmean),
        grid=(nt,),
        in_specs=[row(d), pl.BlockSpec((1, mod_r, d), mod_map), pl.BlockSpec((1, mod_r, d), mod_map),
                  const(g), const(w_in_bf), const(gains), const(bd),
                  pl.BlockSpec((tm, LANES), pos_map), pl.BlockSpec((tm, LANES), pos_map)],
        out_specs=out_specs,
        out_shape=out_shapes,
        compiler_params=_params("parallel"),
        name="proj",
    )(x, shift, scale, g, w_in_bf, gains, bd, cos, sin)


def _compress_compute(src_refs, pe_ref, w1_ref, w2_ref, gk_ref, bd_ref, kc_ref, vc_ref, n_rows):
    n_chunk = n_rows // CMP_STRIDE
    gw = G_NSA * HEAD_DIM
    for kv in range(2):
        acc_a = jnp.zeros((n_chunk, gw), F32)
        acc_b = jnp.zeros((n_chunk, gw), F32)
        for j in range(CMP_STRIDE):
            xj = src_refs[kv][pl.ds(j, n_chunk, stride=CMP_STRIDE), :]
            acc_a = acc_a + _dot(xj + pe_ref[kv, 0, j:j + 1, :], w1_ref[kv, 0, j], HIGHEST)
            acc_b = acc_b + _dot(xj + pe_ref[kv, 1, j:j + 1, :], w1_ref[kv, 1, j], HIGHEST)
        hid = jax.nn.gelu(acc_a + pltpu.roll(acc_b, n_chunk - 1, 0))
        out = _dot(hid, w2_ref[kv], HIGHEST)
        if kv == 0:
            kc_ref[0] = _head_norm(out, gk_ref[...], bd_ref[...])
        else:
            vc_ref[0] = out


def _compress_prompt_kernel(k_ref, v_ref, pe_ref, w1_ref, w2_ref, gk_ref, bd_ref, kc_ref, vc_ref, *, n_rows):
    _compress_compute((k_ref.at[0], v_ref.at[0]), pe_ref, w1_ref, w2_ref, gk_ref, bd_ref, kc_ref, vc_ref, n_rows)


def _compress_paged_kernel(pt_ref, page_ref, pe_ref, w1_ref, w2_ref, gk_ref, bd_ref, kc_ref, vc_ref, kbuf, vbuf,
                           *, n_rows, page):
    p = pl.program_id(1)
    gw = G_NSA * HEAD_DIM
    rows = pl.ds(pl.multiple_of(p * page, page), page)
    kbuf[rows, :] = page_ref[0, :, 0:gw]
    vbuf[rows, :] = page_ref[0, :, gw:2 * gw]

    @pl.when(p == pl.num_programs(1) - 1)
    def _():
        _compress_compute((kbuf, vbuf), pe_ref, w1_ref, w2_ref, gk_ref, bd_ref, kc_ref, vc_ref, n_rows)


def _compress_consts(cmp_pos, cmp_w1, cmp_w2, gain_k_cmp):
    pe = jnp.tile(cmp_pos.reshape(2, 2, CMP_STRIDE, HEAD_DIM), (1, 1, 1, G_NSA))
    eye = jnp.eye(G_NSA, dtype=F32)
    w1 = cmp_w1.reshape(2, 2, CMP_STRIDE, HEAD_DIM, HEAD_DIM)
    w1bd = jnp.einsum("gh,kajde->kajgdhe", eye, w1).reshape(2, 2, CMP_STRIDE, G_NSA * HEAD_DIM, G_NSA * HEAD_DIM)
    w2bd = jnp.einsum("gh,kde->kgdhe", eye, cmp_w2).reshape(2, G_NSA * HEAD_DIM, G_NSA * HEAD_DIM)
    gk = jnp.tile(gain_k_cmp.reshape(1, HEAD_DIM), (1, G_NSA))
    return pe, w1bd, w2bd, gk


def _compress_prompt(nsa_rows, consts, bd):
    b, s, _ = nsa_rows.shape
    pe, w1bd, w2bd, gk = consts
    n_chunk = s // CMP_STRIDE
    gw = G_NSA * HEAD_DIM
    const = lambda a: pl.BlockSpec(a.shape, lambda i: (0,) * a.ndim)
    out = jax.ShapeDtypeStruct((b, n_chunk, gw), F32)
    return pl.pallas_call(
        functools.partial(_compress_prompt_kernel, n_rows=s),
        grid=(b,),
        in_specs=[pl.BlockSpec((1, s, gw), lambda i: (i, 0, 0)), pl.BlockSpec((1, s, gw), lambda i: (i, 0, 1)),
                  const(pe), const(w1bd), const(w2bd), const(gk), const(bd)],
        out_specs=[pl.BlockSpec((1, n_chunk, gw), lambda i: (i, 0, 0))] * 2,
        out_shape=[out, out],
        compiler_params=_params("parallel"),
        name="compress_prompt",
    )(nsa_rows, nsa_rows, pe, w1bd, w2bd, gk, bd)


def _compress_paged(cache_nsa, pt_flat, consts, bd, *, n_req, n_pages):
    _, page, _ = cache_nsa.shape
    pe, w1bd, w2bd, gk = consts
    n_rows = n_pages * page
    n_chunk = n_rows // CMP_STRIDE
    gw = G_NSA * HEAD_DIM
    const = lambda a: pl.BlockSpec(a.shape, lambda b, p, pt: (0,) * a.ndim)
    out = jax.ShapeDtypeStruct((n_req, n_chunk, gw), F32)
    return pl.pallas_call(
        functools.partial(_compress_paged_kernel, n_rows=n_rows, page=page),
        grid_spec=pltpu.PrefetchScalarGridSpec(
            num_scalar_prefetch=1,
            grid=(n_req, n_pages),
            in_specs=[pl.BlockSpec((1, page, 2 * gw), lambda b, p, pt: (pt[b * n_pages + p], 0, 0)),
                      const(pe), const(w1bd), const(w2bd), const(gk), const(bd)],
            out_specs=[pl.BlockSpec((1, n_chunk, gw), lambda b, p, pt: (b, 0, 0))] * 2,
            scratch_shapes=[pltpu.VMEM((n_rows, gw), F32)] * 2),
        out_shape=[out, out],
        compiler_params=_params("parallel", "arbitrary"),
        name="compress_paged",
    )(pt_flat, cache_nsa, pe, w1bd, w2bd, gk, bd)


def _moba_kernel(q_ref, kmp_ref, k_ref, v_ref, o_ref):
    qi = pl.program_id(2)
    own = qi
    q2 = q_ref[0]
    lane = _iota((MOBA_Q, LANES), 1)
    n_blk = MOBA_BLOCK // SUBLANES
    blk = _iota((n_blk, MOBA_Q), 0)
    valid = blk < own
    q_augs = []
    for h in range(2):
        qh = q2 if h == 0 else pltpu.roll(q2, HEAD_DIM, 1)
        q0 = jnp.where(lane < HEAD_DIM, qh, 0.0)
        score = _dot_nt_x3(kmp_ref[0, h], q0)[HEAD_DIM:HEAD_DIM + n_blk, :]
        picked = _top_k_sublanes(jnp.where(valid, score, -jnp.inf), MOBA_TOPK)
        sel = ((picked > 0.5) & valid) | (blk == own)
        bias_t = jnp.concatenate([jnp.zeros((HEAD_DIM, MOBA_Q), F32), jnp.where(sel, 0.0, NEG_BIG),
                                  jnp.zeros((LANES - HEAD_DIM - n_blk, MOBA_Q), F32)], axis=0)
        q_augs.append(jnp.where(lane < HEAD_DIM, q0 * SCALE_LOG2E, bias_t.T).astype(BF16))
    pos = qi * MOBA_Q + _iota((MOBA_Q, ATT_TK), 0)

    def tile(j, carry, masked):
        start = pl.multiple_of(j * ATT_TK, ATT_TK)
        mask = None
        if masked:
            mask = (j * ATT_TK + _iota((MOBA_Q, ATT_TK), 1)) <= pos
        return tuple(_flash_step(q_augs[h], k_ref[0, pl.ds(start, ATT_TK), h * LANES:(h + 1) * LANES],
                                 v_ref[0, pl.ds(start, ATT_TK), h * LANES:(h + 1) * LANES], mask, *carry[h])
                     for h in range(2))

    init = (jnp.full((MOBA_Q, 1), NEG_BIG, F32), jnp.zeros((MOBA_Q, LANES), F32))
    jd = (own * MOBA_BLOCK) // ATT_TK
    carry = lax.fori_loop(0, jd, lambda j, c: tile(j, c, False), tile(jd, (init, init), True))
    o_ref[0] = jnp.where(lane < HEAD_DIM, _flash_finish(carry[0][1]),
                         pltpu.roll(_flash_finish(carry[1][1]), HEAD_DIM, 1)).astype(o_ref.dtype)


def _moba_attention(q_m, kmp, k_aug, v_aug):
    b, s, _ = q_m.shape
    return pl.pallas_call(
        _moba_kernel,
        grid=(b, H_MOBA // 2, s // MOBA_Q),
        in_specs=[pl.BlockSpec((1, MOBA_Q, LANES), lambda b, h, i: (b, i, h)),
                  pl.BlockSpec((1, 2, LANES, LANES), lambda b, h, i: (b, h, 0, 0)),
                  pl.BlockSpec((1, s, 2 * LANES), lambda b, h, i: (b, 0, h)),
                  pl.BlockSpec((1, s, 2 * LANES), lambda b, h, i: (b, 0, h))],
        out_specs=pl.BlockSpec((1, MOBA_Q, LANES), lambda b, h, i: (b, i, h)),
        out_shape=jax.ShapeDtypeStruct((b, s, W_QM), BF16),
        compiler_params=_params("parallel", "parallel", "arbitrary"),
        name="moba_attention",
    )(q_m, kmp, k_aug, v_aug)


def _stack_heads(q4):
    lane = _iota((Q_TILE, LANES), 1)
    parts = []
    for r in range(R_NSA):
        c = q4[:, (r // 2) * LANES:(r // 2 + 1) * LANES]
        if r % 2:
            c = pltpu.roll(c, HEAD_DIM, 1)
        parts.append(jnp.where(lane < HEAD_DIM, c, 0.0))
    return jnp.concatenate(parts, axis=0)


def _nsa_kernel(qn_ref, qr_ref, gate_ref, kc_ref, vc_ref, cover_ref, ks_ref, vs_ref, kw_ref, vw_ref, o_ref,
                *, n_chunk):
    qi = pl.program_id(2)
    rows = R_NSA * Q_TILE
    qloc = _iota((rows, 1), 0) % Q_TILE
    pos = qi * Q_TILE + qloc

    qn = _stack_heads(qn_ref[0])
    s = _dot_nt_x3(qn, kc_ref[0, 0]) * SCALE
    cmask = (_iota((rows, n_chunk), 1) * CMP_STRIDE + (CMP_LEN - 1)) <= pos
    s = jnp.where(cmask, s, NEG_BIG)
    e = jnp.where(cmask, jnp.exp(s - jnp.max(s, axis=1, keepdims=True)), 0.0)
    p_cmp = e / jnp.maximum(jnp.sum(e, axis=1, keepdims=True), TINY)
    o_cmp = _dot(p_cmp.astype(BF16), vc_ref[0, 0])

    p_grp = p_cmp[0:Q_TILE]
    for r in range(1, R_NSA):
        p_grp = p_grp + p_cmp[r * Q_TILE:(r + 1) * Q_TILE]
    p_hi, p_lo = _split_bf16(p_grp)
    imp_t = _dot_nt(cover_ref[...], p_hi) + _dot_nt(cover_ref[...], p_lo)
    blk_t = _iota((LANES, Q_TILE), 0)
    own_t = (qi * Q_TILE + _iota((LANES, Q_TILE), 1)) // SEL_BLOCK
    forced = (blk_t == 0) | (blk_t == own_t) | (blk_t == own_t - 1)
    score = jnp.where(blk_t <= own_t, jnp.where(forced, jnp.inf, imp_t), -jnp.inf)
    picked = _top_k_sublanes(score, SEL_TOPN)
    bias = jnp.where((picked > 0.5) & (blk_t <= own_t), 0.0, NEG_BIG).T
    blk = _iota((Q_TILE, LANES), 1)

    qr = (_stack_heads(qr_ref[0]) * SCALE_LOG2E).astype(BF16)
    q_aug = jnp.concatenate([jnp.concatenate([bias] * R_NSA, axis=0).astype(BF16), qr], axis=1)
    init = (jnp.full((rows, 1), NEG_BIG, F32), jnp.zeros((rows, LANES), F32))

    tk = ATT_TK
    jd = (qi * Q_TILE) // tk

    def sel_tile(j, carry, masked):
        start = pl.multiple_of(j * tk, tk)
        mask = None
        if masked:
            mask = (j * tk + _iota((rows, tk), 1)) <= pos
        return _flash_step(q_aug, ks_ref[0, 0, pl.ds(start, tk), :], vs_ref[0, 0, pl.ds(start, tk), :], mask, *carry)

    o_sel = _flash_finish(lax.fori_loop(0, jd, lambda j, c: sel_tile(j, c, False), sel_tile(jd, init, True))[1])

    span = WINDOW + Q_TILE
    w0 = pl.multiple_of(jnp.maximum(qi * Q_TILE - WINDOW, 0), Q_TILE)
    kpos = w0 + _iota((rows, span), 1)
    wmask = (kpos <= pos) & (kpos > pos - WINDOW)
    o_win = _flash_finish(_flash_step(qr, kw_ref[0, 0, pl.ds(w0, span), :], vw_ref[0, 0, pl.ds(w0, span), :], wmask,
                                      *init)[1])

    gates = gate_ref[0, 0]
    heads = []
    for r in range(R_NSA):
        rs = slice(r * Q_TILE, (r + 1) * Q_TILE)
        heads.append(gates[:, 3 * r:3 * r + 1] * o_cmp[rs] + gates[:, 3 * r + 1:3 * r + 2] * o_sel[rs]
                     + gates[:, 3 * r + 2:3 * r + 3] * o_win[rs])
    lo = blk < HEAD_DIM
    o_ref[0] = jnp.concatenate([jnp.where(lo, heads[0], pltpu.roll(heads[1], HEAD_DIM, 1)),
                                jnp.where(lo, heads[2], pltpu.roll(heads[3], HEAD_DIM, 1))], axis=1).astype(o_ref.dtype)


def _nsa_attention(qn, qr, gates_g, kcp, vcd, cover, ks_aug, vs_dup, kw_pad, vw_dup):
    b, s, _ = qn.shape
    n_chunk = kcp.shape[2]
    gq = R_NSA * HEAD_DIM
    per_g = lambda w: pl.BlockSpec((1, 1, s, w), lambda b, g, i: (b, g, 0, 0))
    return pl.pallas_call(
        functools.partial(_nsa_kernel, n_chunk=n_chunk),
        grid=(b, G_NSA, s // Q_TILE),
        in_specs=[pl.BlockSpec((1, Q_TILE, gq), lambda b, g, i: (b, i, g)),
                  pl.BlockSpec((1, Q_TILE, gq), lambda b, g, i: (b, i, g)),
                  pl.BlockSpec((1, 1, Q_TILE, LANES), lambda b, g, i: (b, g, i, 0)),
                  pl.BlockSpec((1, 1, n_chunk, LANES), lambda b, g, i: (b, g, 0, 0)),
                  pl.BlockSpec((1, 1, n_chunk, LANES), lambda b, g, i: (b, g, 0, 0)),
                  pl.BlockSpec(cover.shape, lambda b, g, i: (0, 0)),
                  per_g(2 * LANES), per_g(LANES), per_g(LANES), per_g(LANES)],
        out_specs=pl.BlockSpec((1, Q_TILE, gq), lambda b, g, i: (b, i, g)),
        out_shape=jax.ShapeDtypeStruct((b, s, W_QN), BF16),
        compiler_params=_params("parallel", "parallel", "arbitrary"),
        name="nsa_attention",
    )(qn, qr, gates_g, kcp, vcd, cover, ks_aug, vs_dup, kw_pad, vw_dup)


def _cover(n_cmp, n_sel, rows):
    c0 = np.arange(rows)[:, None] * CMP_STRIDE
    b0 = np.arange(LANES)[None, :] * SEL_BLOCK
    ok = (c0 < b0 + SEL_BLOCK) & (c0 + CMP_LEN > b0) & (np.arange(rows)[:, None] < n_cmp) & (np.arange(LANES)[None, :] < n_sel)
    return jnp.asarray(ok.astype(np.float32))


def _dec_kernel(pt_ref, ma_ref, mb_ref, na_ref, nb_ref, wb_ref, qbd_ref, q8n_ref, q8r_ref, g8_ref, newm_ref,
                news_ref, neww_ref, kc_ref, vc_ref, cover_ref, om_ref, o8_ref,
                m_sc, l_sc, s_sc, acc_sc, msel_sc, lsel_sc, asel_sc, bias_sc, ocmp_sc, *, n_steps, page):
    j = pl.program_id(1)
    sub8 = _iota((SUBLANES, LANES), 0)
    lane8 = _iota((SUBLANES, LANES), 1)
    tk = 2 * page
    q8r = q8r_ref[0] * SCALE
    q8r_bf = q8r.astype(BF16)

    @pl.when(j == 0)
    def _():
        n_chunk = kc_ref.shape[1]
        s = _dot_nt(q8n_ref[0], kc_ref[0], HIGHEST) * SCALE
        cmask = _iota((SUBLANES, n_chunk), 1) < (n_chunk - 1)
        s = jnp.where(cmask, s, NEG_BIG)
        e = jnp.where(cmask, jnp.exp(s - jnp.max(s, axis=1, keepdims=True)), 0.0)
        p_cmp = e / jnp.maximum(jnp.sum(e, axis=1, keepdims=True), TINY)
        ocmp_sc[...] = _dot(p_cmp.astype(BF16), vc_ref[0].astype(BF16))
        subc = _iota((SUBLANES, n_chunk), 0)
        g0 = jnp.sum(jnp.where(subc < R_NSA, p_cmp, 0.0), axis=0, keepdims=True)
        g1 = jnp.sum(jnp.where(subc >= R_NSA, p_cmp, 0.0), axis=0, keepdims=True)
        imp = _dot(jnp.where(subc < R_NSA, g0, g1), cover_ref[...], HIGHEST)
        own = (n_steps * tk) // SEL_BLOCK
        forced = (lane8 == 0) | (lane8 == own) | (lane8 == own - 1)
        score = jnp.where(lane8 <= own, jnp.where(forced, jnp.inf, imp), -jnp.inf)
        picked, _, _ = _top_k_lanes(score, SEL_TOPN)
        bias_sc[...] = jnp.where((picked > 0.5) & (lane8 <= own), 0.0, NEG_BIG)
        msel_sc[...] = jnp.full((SUBLANES, LANES), NEG_BIG, F32)
        lsel_sc[...] = jnp.zeros((SUBLANES, LANES), F32)
        asel_sc[...] = jnp.zeros((SUBLANES, LANES), F32)

    hw = H_MOBA * HEAD_DIM
    k_m = jnp.concatenate([ma_ref[0, :, 0:hw], mb_ref[0, :, 0:hw]], axis=0)
    v_m = jnp.concatenate([ma_ref[0, :, hw:2 * hw], mb_ref[0, :, hw:2 * hw]], axis=0)
    qbd = qbd_ref[0]
    s = _dot_nt((qbd * SCALE).astype(BF16), k_m.astype(BF16))
    mj = jnp.max(s, axis=1, keepdims=True)
    p = jnp.exp(s - mj)
    m_sc[j] = jnp.broadcast_to(mj, (SUBLANES, LANES))
    l_sc[j] = jnp.broadcast_to(jnp.sum(p, axis=1, keepdims=True), (SUBLANES, LANES))
    acc_sc[j] = _dot(p.astype(BF16), v_m.astype(BF16))
    kmean = jnp.sum(k_m, axis=0, keepdims=True) * (1.0 / MOBA_BLOCK)
    s_sc[j] = jnp.broadcast_to(jnp.sum(qbd * kmean, axis=1, keepdims=True), (SUBLANES, LANES))

    gw = G_NSA * HEAD_DIM
    k_s = jnp.concatenate([na_ref[0, :, 0:gw], nb_ref[0, :, 0:gw]], axis=0).astype(BF16)
    v_s = jnp.concatenate([na_ref[0, :, gw:2 * gw], nb_ref[0, :, gw:2 * gw]], axis=0).astype(BF16)
    s = _dot_nt(q8r_bf, k_s)
    bias = bias_sc[...]
    kb = _iota((SUBLANES, tk), 1) // SEL_BLOCK
    bias_k = jnp.zeros((SUBLANES, tk), F32)
    for q in range(tk // SEL_BLOCK):
        bq = jnp.sum(jnp.where(lane8 == j * (tk // SEL_BLOCK) + q, bias, 0.0), axis=1, keepdims=True)
        bias_k = jnp.where(kb == q, bq, bias_k)
    s = s + bias_k
    m_old = msel_sc[:, 0:1]
    m_new = jnp.maximum(m_old, jnp.max(s, axis=1, keepdims=True))
    alpha = jnp.exp(m_old - m_new)
    p = jnp.exp(s - m_new)
    msel_sc[...] = jnp.broadcast_to(m_new, (SUBLANES, LANES))
    lsel_sc[...] = alpha * lsel_sc[...] + jnp.sum(p, axis=1, keepdims=True)
    asel_sc[...] = alpha * asel_sc[...] + _dot(p.astype(BF16), v_s)

    @pl.when(j == n_steps - 1)
    def _():
        news = news_ref[0]
        s_new = jnp.sum(q8r * news[0:1], axis=1, keepdims=True)
        m_old = msel_sc[:, 0:1]
        m_new = jnp.maximum(m_old, s_new)
        alpha = jnp.exp(m_old - m_new)
        pn = jnp.exp(s_new - m_new)
        l = alpha * lsel_sc[...] + pn
        o_sel = (alpha * asel_sc[...] + pn * news[1:2]) / jnp.maximum(l, TINY)

        neww = neww_ref[0]
        kw = wb_ref[0, :, 0:gw].astype(BF16)
        vw = wb_ref[0, :, gw:2 * gw].astype(BF16)
        nw = kw.shape[0]
        s = _dot_nt(q8r_bf, kw)
        wmask = _iota((SUBLANES, nw), 1) >= 1
        s = jnp.where(wmask, s, NEG_BIG)
        s_new = jnp.sum(q8r * neww[0:1], axis=1, keepdims=True)
        mw = jnp.maximum(jnp.max(s, axis=1, keepdims=True), s_new)
        e = jnp.where(wmask, jnp.exp(s - mw), 0.0)
        en = jnp.exp(s_new - mw)
        lw = jnp.sum(e, axis=1, keepdims=True) + en
        o_win = (_dot(e.astype(BF16), vw) + en * neww[1:2]) / jnp.maximum(lw, TINY)

        g8 = g8_ref[0]
        o8_ref[0] = g8[:, 0:1] * ocmp_sc[...] + g8[:, 1:2] * o_sel + g8[:, 2:3] * o_win

        newm = newm_ref[0]
        scores = [s_sc[b] for b in range(n_steps)]
        m_tot = jnp.broadcast_to(jnp.sum(qbd * newm[0:1], axis=1, keepdims=True) * SCALE, (SUBLANES, LANES))
        s_own = m_tot
        sels = []
        for b in range(n_steps):
            rank = jnp.zeros((SUBLANES, LANES), F32)
            for c in range(n_steps):
                if c == b:
                    continue
                ahead = (scores[c] > scores[b]) | ((scores[c] == scores[b]) & (c < b))
                rank = rank + jnp.where(ahead, 1.0, 0.0)
            sels.append(rank < MOBA_TOPK)
            m_tot = jnp.where(sels[b], jnp.maximum(m_tot, m_sc[b]), m_tot)
        w_own = jnp.exp(s_own - m_tot)
        l_tot = w_own
        acc = w_own[:, 0:1] * newm[1:2]
        for b in range(n_steps):
            wgt = jnp.where(sels[b], jnp.exp(m_sc[b] - m_tot), 0.0)
            l_tot = l_tot + wgt * l_sc[b]
            acc = acc + wgt[:, 0:1] * acc_sc[b]
        o_full = acc / jnp.maximum(l_tot[:, 0:1], TINY)
        own_head = (_iota((SUBLANES, hw), 1) // HEAD_DIM) == _iota((SUBLANES, hw), 0)
        om_ref[0] = jnp.sum(jnp.where(own_head, o_full, 0.0), axis=0, keepdims=True)


def _dec_attention(cache_moba, cache_nsa, win_buf, pt_flat, qbd, q8n, q8r, g8, newm, news, neww, kc, vc, cover,
                   *, n_req, n_pages):
    _, page, mw = cache_moba.shape
    n_steps = n_pages // 2
    gw = G_NSA * HEAD_DIM
    hw = H_MOBA * HEAD_DIM
    n_chunk = kc.shape[1]
    nwin = win_buf.shape[1]
    req = lambda a: pl.BlockSpec((1,) + a.shape[1:], lambda b, j, pt: (b,) + (0,) * (a.ndim - 1))
    return pl.pallas_call(
        functools.partial(_dec_kernel, n_steps=n_steps, page=page),
        grid_spec=pltpu.PrefetchScalarGridSpec(
            num_scalar_prefetch=1,
            grid=(n_req, n_steps),
            in_specs=[pl.BlockSpec((1, page, mw), lambda b, j, pt: (pt[b * n_pages + 2 * j], 0, 0)),
                      pl.BlockSpec((1, page, mw), lambda b, j, pt: (pt[b * n_pages + 2 * j + 1], 0, 0)),
                      pl.BlockSpec((1, page, 2 * gw), lambda b, j, pt: (pt[b * n_pages + 2 * j], 0, 1)),
                      pl.BlockSpec((1, page, 2 * gw), lambda b, j, pt: (pt[b * n_pages + 2 * j + 1], 0, 1)),
                      req(win_buf), req(qbd), req(q8n), req(q8r), req(g8), req(newm), req(news), req(neww),
                      req(kc), req(vc), pl.BlockSpec(cover.shape, lambda b, j, pt: (0, 0))],
            out_specs=[pl.BlockSpec((1, 1, hw), lambda b, j, pt: (b, 0, 0)),
                       pl.BlockSpec((1, SUBLANES, LANES), lambda b, j, pt: (b, 0, 0))],
            scratch_shapes=[pltpu.VMEM((n_steps, SUBLANES, LANES), F32)] * 3
            + [pltpu.VMEM((n_steps, SUBLANES, hw), F32)]
            + [pltpu.VMEM((SUBLANES, LANES), F32)] * 5),
        out_shape=[jax.ShapeDtypeStruct((n_req, 1, hw), F32), jax.ShapeDtypeStruct((n_req, SUBLANES, LANES), F32)],
        compiler_params=_params("parallel", "arbitrary"),
        name="dec_attention",
    )(pt_flat, cache_moba, cache_moba, cache_nsa, cache_nsa, win_buf, qbd, q8n, q8r, g8, newm, news, neww, kc, vc,
      cover)


def _lane_rep(col):
    return jnp.broadcast_to(col, (col.shape[0], LANES))


def _head_sums(prod_row):
    w = prod_row.shape[1]
    own = (_iota((SUBLANES, w), 1) // HEAD_DIM) == _iota((SUBLANES, w), 0)
    return _lane_rep(jnp.sum(jnp.where(own, jnp.broadcast_to(prod_row, (SUBLANES, w)), 0.0), axis=1, keepdims=True))


def _pair_row(x8, h):
    return jnp.where(_iota((1, LANES), 1) < HEAD_DIM, x8[h:h + 1, :], x8[h + 1:h + 2, :])


def _cols_to_row(acc_a, acc_b):
    return jnp.sum(jnp.concatenate([acc_a, acc_b], axis=0).T, axis=0, keepdims=True)


def _dec2_kernel(pt_ref, cm_ref, cn_ref, wb_ref, q_ref, q8n_ref, new_ref, gate_ref, wcat_ref, pecat_ref, w2_ref,
                 gk_ref, bd_ref, cover_ref, o_ref,
                 mbuf, nbuf, sem, xk, xv, qmb, qrb, s_sc, p_sc, *, n_pages, page):
    b = pl.program_id(0)
    n_req = pl.num_programs(0)
    slot = b % 2
    hw = H_MOBA * HEAD_DIM
    gw = G_NSA * HEAD_DIM
    past = n_pages * page

    def copies(req, sl):
        out = []
        for p in range(n_pages):
            pg = pt_ref[req * n_pages + p]
            out.append(pltpu.make_async_copy(cm_ref.at[pg], mbuf.at[sl, p], sem.at[0, sl]))
            out.append(pltpu.make_async_copy(cn_ref.at[pg], nbuf.at[sl, p], sem.at[1, sl]))
        return out

    @pl.when(b == 0)
    def _():
        for c in copies(0, 0):
            c.start()

    for c in copies(b, slot):
        c.wait()

    @pl.when(b + 1 < n_req)
    def _():
        for c in copies(b + 1, 1 - slot):
            c.start()

    lane1 = _iota((1, LANES), 1)
    lane8 = _iota((SUBLANES, LANES), 1)
    qrow = q_ref[0]
    new = new_ref[0]
    for c in range(hw // LANES):
        cols = slice(c * LANES, (c + 1) * LANES)
        qmb[cols, :] = jnp.broadcast_to(qrow[0:1, cols] * SCALE, (LANES, LANES)).T
        qrb[cols, :] = jnp.broadcast_to(qrow[1:2, cols] * SCALE, (LANES, LANES)).T

    def softmax_pv(scores, extra8, s_new8, v_rows, vbuf_ref, v_row0, per_g):
        del extra8
        m8 = s_new8
        for s in scores:
            m8 = jnp.maximum(m8, _lane_rep(jnp.max(s, axis=1, keepdims=True)))
        w_new = jnp.exp(s_new8 - m8)
        l8 = w_new
        for p, s in enumerate(scores):
            pr = jnp.exp(s - m8)
            p_sc[p] = pr
            l8 = l8 + _lane_rep(jnp.sum(pr, axis=1, keepdims=True))
        inv8 = 1.0 / jnp.maximum(l8, TINY)
        rows = []
        for hp in range(SUBLANES // 2):
            accs = []
            for h in (2 * hp, 2 * hp + 1):
                r0 = v_row0 + (h // R_NSA if per_g else h) * HEAD_DIM

                def body(p, acc, h=h, r0=r0):
                    return acc + vbuf_ref[slot, p, r0:r0 + HEAD_DIM, :] * p_sc[p, h:h + 1, :]
                accs.append(lax.fori_loop(0, n_pages, body, jnp.zeros((HEAD_DIM, LANES), F32)))
            row = _cols_to_row(accs[0], accs[1])
            cols = slice(hp * LANES, (hp + 1) * LANES)
            rows.append((row + _pair_row(w_new, 2 * hp) * v_rows[:, cols]) * _pair_row(inv8, 2 * hp))
        return rows

    def moba_scores(p, carry):
        rows = [jnp.sum(mbuf[slot, p, h * HEAD_DIM:(h + 1) * HEAD_DIM, :] * qmb[h * HEAD_DIM:(h + 1) * HEAD_DIM, :],
                        axis=0, keepdims=True) for h in range(H_MOBA)]
        s_sc[p] = jnp.concatenate(rows, axis=0)
        return carry
    lax.fori_loop(0, n_pages, moba_scores, 0)
    s_all = [s_sc[p] for p in range(n_pages)]
    ppb = MOBA_BLOCK // page
    n_blk = n_pages // ppb
    bsc = []
    for j in range(n_blk):
        tot = s_all[j * ppb]
        for t in range(1, ppb):
            tot = tot + s_all[j * ppb + t]
        bsc.append(_lane_rep(jnp.sum(tot, axis=1, keepdims=True)))
    masked = []
    for j in range(n_blk):
        rank = jnp.zeros((SUBLANES, LANES), F32)
        for c in range(n_blk):
            if c != j:
                ahead = (bsc[c] > bsc[j]) | ((bsc[c] == bsc[j]) & (c < j))
                rank = rank + jnp.where(ahead, 1.0, 0.0)
        for t in range(ppb):
            masked.append(jnp.where(rank < MOBA_TOPK, s_all[j * ppb + t], NEG_BIG))
    s_own = _head_sums(qrow[0:1, :] * new[0:1, :]) * SCALE
    o_rows = softmax_pv(masked, None, s_own, new[1:2, :], mbuf, hw, False)

    for p in range(n_pages):
        xk[p * page:(p + 1) * page, :] = nbuf[slot, p, 0:gw, :].T
        xv[p * page:(p + 1) * page, :] = nbuf[slot, p, gw:2 * gw, :].T
    n_chunk = past // CMP_STRIDE
    cmp_out = []
    for kv, xref in enumerate((xk, xv)):
        xcat = jnp.concatenate([xref[pl.ds(j, n_chunk, stride=CMP_STRIDE), :] for j in range(CMP_STRIDE)], axis=1)
        pe2 = _dot(pecat_ref[kv].astype(BF16), wcat_ref[kv])
        ab = _dot(xcat.astype(BF16), wcat_ref[kv])
        hid = jax.nn.gelu(ab[:, 0:gw] + pe2[0:1, 0:gw] + pltpu.roll(ab[:, gw:2 * gw] + pe2[1:2, gw:2 * gw],
                                                                    n_chunk - 1, 0))
        cmp_out.append(_dot(hid, w2_ref[kv], HIGHEST))
    kc = _head_norm(cmp_out[0], gk_ref[...], bd_ref[...])
    vc = cmp_out[1]

    s = _dot_nt(q8n_ref[0], kc, HIGHEST) * SCALE
    cmask = _iota((SUBLANES, n_chunk), 1) < (n_chunk - 1)
    s = jnp.where(cmask, s, NEG_BIG)
    e = jnp.where(cmask, jnp.exp(s - jnp.max(s, axis=1, keepdims=True)), 0.0)
    p_cmp = e / jnp.maximum(jnp.sum(e, axis=1, keepdims=True), TINY)
    o_cmp8 = _dot(p_cmp.astype(BF16), vc.astype(BF16))
    subc = _iota((SUBLANES, n_chunk), 0)
    g0 = jnp.sum(jnp.where(subc < R_NSA, p_cmp, 0.0), axis=0, keepdims=True)
    g1 = jnp.sum(jnp.where(subc >= R_NSA, p_cmp, 0.0), axis=0, keepdims=True)
    p_grp = jnp.concatenate([jnp.where(subc < R_NSA, g0, g1), jnp.zeros((LANES - SUBLANES, n_chunk), F32)], axis=0)
    imp_t = _dot_nt(cover_ref[...], p_grp, HIGHEST)
    blk_t = _iota((LANES, LANES), 0)
    own = past // SEL_BLOCK
    forced = (blk_t == 0) | (blk_t == own) | (blk_t == own - 1)
    score = jnp.where(blk_t <= own, jnp.where(forced, jnp.inf, imp_t), -jnp.inf)
    picked = _top_k_sublanes(score, SEL_TOPN)
    bias8 = jnp.where((picked > 0.5) & (blk_t <= own), 0.0, NEG_BIG).T[0:SUBLANES, :]
    cmp_rows = []
    for hp in range(H_NSA // 2):
        g = (2 * hp) // R_NSA
        ra = o_cmp8[2 * hp:2 * hp + 1, :]
        rb = o_cmp8[2 * hp + 1:2 * hp + 2, :]
        cmp_rows.append(jnp.where(lane1 < HEAD_DIM, ra if g == 0 else pltpu.roll(ra, HEAD_DIM, 1),
                                  rb if g == 1 else pltpu.roll(rb, HEAD_DIM, 1)))

    def sel_scores(p, carry):
        rows = [jnp.sum(nbuf[slot, p, 2 * gw + (i // R_NSA) * HEAD_DIM:2 * gw + (i // R_NSA + 1) * HEAD_DIM, :]
                        * qrb[i * HEAD_DIM:(i + 1) * HEAD_DIM, :], axis=0, keepdims=True) for i in range(H_NSA)]
        s_sc[p] = jnp.concatenate(rows, axis=0)
        return carry
    lax.fori_loop(0, n_pages, sel_scores, 0)
    bpp = page // SEL_BLOCK
    sel_s = []
    for p in range(n_pages):
        bias_p = bias8[:, p * bpp:p * bpp + 1]
        for t in range(1, bpp):
            bias_p = jnp.where(lane8 < t * SEL_BLOCK, bias_p, bias8[:, p * bpp + t:p * bpp + t + 1])
        sel_s.append(s_sc[p] + bias_p)
    s_new = _head_sums(qrow[1:2, :] * new[2:3, :]) * SCALE
    sel_rows = softmax_pv(sel_s, None, s_new, new[3:4, :], nbuf, 3 * gw, True)

    nw = wb_ref.shape[2]
    wk = nw // LANES
    w_s = []
    for i in range(H_NSA):
        g = i // R_NSA
        qcol = jnp.concatenate([qrb[i * HEAD_DIM:(i + 1) * HEAD_DIM, :]] * wk, axis=1)
        w_s.append(jnp.sum(wb_ref[0, g * HEAD_DIM:(g + 1) * HEAD_DIM, :] * qcol, axis=0, keepdims=True))
    s = jnp.concatenate(w_s, axis=0)
    wmask = _iota((SUBLANES, nw), 1) >= 1
    s = jnp.where(wmask, s, NEG_BIG)
    s_new = _head_sums(qrow[1:2, :] * new[4:5, :]) * SCALE
    m8 = jnp.maximum(_lane_rep(jnp.max(s, axis=1, keepdims=True)), s_new)
    pw = jnp.where(wmask, jnp.exp(s - m8[:, 0:1]), 0.0)
    w_new = jnp.exp(s_new - m8)
    inv8 = 1.0 / jnp.maximum(_lane_rep(jnp.sum(pw, axis=1, keepdims=True)) + w_new, TINY)
    win_rows = []
    for hp in range(H_NSA // 2):
        accs = []
        for i in (2 * hp, 2 * hp + 1):
            g = i // R_NSA
            prod = wb_ref[0, gw + g * HEAD_DIM:gw + (g + 1) * HEAD_DIM, :] * pw[i:i + 1, :]
            acc = prod[:, 0:LANES]
            for c in range(1, wk):
                acc = acc + prod[:, c * LANES:(c + 1) * LANES]
            accs.append(acc)
        cols = slice(hp * LANES, (hp + 1) * LANES)
        win_rows.append((_cols_to_row(accs[0], accs[1]) + _pair_row(w_new, 2 * hp) * new[5:6, cols])
                        * _pair_row(inv8, 2 * hp))

    gates = gate_ref[0]
    for c in range(hw // LANES):
        o_ref[0, :, c * LANES:(c + 1) * LANES] = o_rows[c]
    for c in range(W_QN // LANES):
        cols = slice(c * LANES, (c + 1) * LANES)
        o_ref[0, :, hw + c * LANES:hw + (c + 1) * LANES] = (
            gates[0:1, cols] * cmp_rows[c] + gates[1:2, cols] * sel_rows[c] + gates[2:3, cols] * win_rows[c])


def _dec2_attention(cm, cn, wb, pt_flat, qrows, q8n, new, gate_rows, wcat, pecat, w2bd, gk, bd, cover,
                    *, n_req, n_pages):
    page = cm.shape[2]
    n_chunk = n_pages * page // CMP_STRIDE
    req = lambda a: pl.BlockSpec((1,) + a.shape[1:], lambda b, pt: (b,) + (0,) * (a.ndim - 1))
    const = lambda a: pl.BlockSpec(a.shape, lambda b, pt: (0,) * a.ndim)
    return pl.pallas_call(
        functools.partial(_dec2_kernel, n_pages=n_pages, page=page),
        grid_spec=pltpu.PrefetchScalarGridSpec(
            num_scalar_prefetch=1,
            grid=(n_req,),
            in_specs=[pl.BlockSpec(memory_space=pl.ANY), pl.BlockSpec(memory_space=pl.ANY), req(wb), req(qrows),
                      req(q8n), req(new), req(gate_rows), const(wcat), const(pecat), const(w2bd), const(gk),
                      const(bd), const(cover)],
            out_specs=pl.BlockSpec((1, 1, W_QM + W_QN), lambda b, pt: (b, 0, 0)),
            scratch_shapes=[pltpu.VMEM((2, n_pages) + cm.shape[1:], F32), pltpu.VMEM((2, n_pages) + cn.shape[1:], F32),
                            pltpu.SemaphoreType.DMA((2, 2)),
                            pltpu.VMEM((n_pages * page, LANES), F32), pltpu.VMEM((n_pages * page, LANES), F32),
                            pltpu.VMEM((W_QM, LANES), F32), pltpu.VMEM((W_QN, LANES), F32),
                            pltpu.VMEM((n_pages, SUBLANES, LANES), F32), pltpu.VMEM((n_pages, SUBLANES, LANES), F32)]),
        out_shape=jax.ShapeDtypeStruct((n_req, 1, W_QM + W_QN), F32),
        compiler_params=_params("arbitrary"),
        name="dec_attention",
    )(pt_flat, cm, cn, wb, qrows, q8n, new, gate_rows, wcat, pecat, w2bd, gk, bd, cover)


def _compress_cat_consts(cmp_pos, cmp_w1):
    eye = jnp.eye(G_NSA, dtype=F32)
    w1 = cmp_w1.reshape(2, 2, CMP_STRIDE, HEAD_DIM, HEAD_DIM)
    wcat = jnp.einsum("gh,kajde->kjgdahe", eye, w1).reshape(2, CMP_STRIDE * G_NSA * HEAD_DIM, 2 * G_NSA * HEAD_DIM)
    pe = jnp.tile(cmp_pos.reshape(2, 2, CMP_STRIDE, 1, HEAD_DIM), (1, 1, 1, G_NSA, 1))
    pecat = jnp.pad(pe.reshape(2, 2, CMP_STRIDE * G_NSA * HEAD_DIM), ((0, 0), (0, SUBLANES - 2), (0, 0)))
    return wcat.astype(BF16), pecat


def _post_kernel(o_ref, x_ref, gt_ref, sh_ref, sc_ref, g_ref, wo_ref, wr_ref, br_ref,
                 y_ref, h3_ref, te_ref, tw_ref):
    y = x_ref[...] + gt_ref[0] * _dot(o_ref[...], wo_ref[...])
    y_ref[...] = y
    h = y * lax.rsqrt(jnp.mean(y * y, axis=1, keepdims=True) + NORM_EPS) * g_ref[...]
    h = h * (1.0 + sc_ref[0]) + sh_ref[0]
    tm = h.shape[0]
    for s in range(h.shape[1] // LANES):
        h3_ref[pl.ds(s, tm, stride=SUBLANES), :] = h[:, s * LANES:(s + 1) * LANES]
    logits = _dot(h, wr_ref[...], HIGHEST) + br_ref[...]
    _, vals, ids = _top_k_lanes(logits, TOP_K)
    lane = _iota((tm, LANES), 1)
    es = [jnp.exp(v - vals[0]) for v in vals]
    den = es[0]
    for e in es[1:]:
        den = den + e
    te = jnp.zeros((tm, LANES), F32)
    tw = jnp.zeros((tm, LANES), F32)
    for k in range(TOP_K):
        te = jnp.where(lane == k, ids[k], te)
        tw = jnp.where(lane == k, es[k] / den, tw)
    te_ref[...] = te.astype(jnp.int32)
    tw_ref[...] = tw


def _post(o, x, gate, shift, scale, g2, wo_bf, wr_pad, br_pad, *, tm, rows_per_mod):
    n, d = x.shape
    mod_r = gate.shape[1]
    mod_map = lambda i: (i // (rows_per_mod // tm), 0, 0)
    row = lambda w: pl.BlockSpec((tm, w), lambda i: (i, 0))
    const = lambda a: pl.BlockSpec(a.shape, lambda i: (0,) * a.ndim)
    mod = pl.BlockSpec((1, mod_r, d), mod_map)
    return pl.pallas_call(
        _post_kernel,
        grid=(n // tm,),
        in_specs=[row(d), row(d), mod, mod, mod, const(g2), const(wo_bf), const(wr_pad), const(br_pad)],
        out_specs=[row(d), pl.BlockSpec((tm * SUBLANES, LANES), lambda i: (i, 0)), row(LANES), row(LANES)],
        out_shape=[jax.ShapeDtypeStruct((n, d), F32), jax.ShapeDtypeStruct((n * SUBLANES, LANES), F32),
                   jax.ShapeDtypeStruct((n, LANES), jnp.int32), jax.ShapeDtypeStruct((n, LANES), F32)],
        compiler_params=_params("parallel"),
        name="post",
    )(o, x, gate, shift, scale, g2, wo_bf, wr_pad, br_pad)


def _expert_kernel(be_ref, na_ref, tok_ref, h3_ref, wgu_ref, bgu_ref, wd_ref, bd_ref, y_ref, buf, xb, sem):
    i = pl.program_id(0)
    n_active = na_ref[0]
    rows8 = MOE_ROWS * SUBLANES

    def gather(blk, slot):
        def body(r, carry):
            t = tok_ref[blk * MOE_ROWS + r]
            pltpu.make_async_copy(h3_ref.at[pl.ds(pl.multiple_of(t * SUBLANES, SUBLANES), SUBLANES), :],
                                  buf.at[slot, pl.ds(pl.multiple_of(r * SUBLANES, SUBLANES), SUBLANES), :],
                                  sem.at[slot]).start()
            return carry
        lax.fori_loop(0, MOE_ROWS, body, 0)

    @pl.when(i == 0)
    def _():
        gather(0, 0)

    @pl.when(i < n_active)
    def _():
        slot = i % 2
        pltpu.make_async_copy(h3_ref.at[pl.ds(0, rows8), :], buf.at[slot], sem.at[slot]).wait()

        @pl.when(i + 1 < n_active)
        def _():
            gather(i + 1, 1 - slot)

        d = xb.shape[1]
        for s in range(d // LANES):
            xb[:, s * LANES:(s + 1) * LANES] = buf[slot, pl.ds(s, MOE_ROWS, stride=SUBLANES), :].astype(BF16)
        gu = _dot(xb[...], wgu_ref[0]) + bgu_ref[0]
        f = gu.shape[1] // 2
        gt = jnp.minimum(gu[:, 0:f], SWIGLU_LIMIT)
        up = jnp.clip(gu[:, f:2 * f], -SWIGLU_LIMIT, SWIGLU_LIMIT)
        act = (up + 1.0) * (gt * jax.nn.sigmoid(SWIGLU_ALPHA * gt))
        y = _dot(act.astype(BF16), wd_ref[0]) + bd_ref[0]
        for s in range(d // LANES):
            y_ref[pl.ds(s, MOE_ROWS, stride=SUBLANES), :] = y[:, s * LANES:(s + 1) * LANES]

    @pl.when(i >= n_active)
    def _():
        y_ref[...] = jnp.zeros(y_ref.shape, F32)


def _experts(blk_e, n_active, tok, h3, wgu_bf, b_gu, wd_bf, b_down):
    n_blocks = blk_e.shape[0]
    e, d, f2 = wgu_bf.shape
    rows8 = MOE_ROWS * SUBLANES
    return pl.pallas_call(
        _expert_kernel,
        grid_spec=pltpu.PrefetchScalarGridSpec(
            num_scalar_prefetch=3,
            grid=(n_blocks,),
            in_specs=[pl.BlockSpec(memory_space=pl.ANY),
                      pl.BlockSpec((1, d, f2), lambda i, be, na, tok: (be[i], 0, 0)),
                      pl.BlockSpec((1, 1, f2), lambda i, be, na, tok: (be[i], 0, 0)),
                      pl.BlockSpec((1, f2 // 2, d), lambda i, be, na, tok: (be[i], 0, 0)),
                      pl.BlockSpec((1, 1, d), lambda i, be, na, tok: (be[i], 0, 0))],
            out_specs=pl.BlockSpec((rows8, LANES), lambda i, be, na, tok: (i, 0)),
            scratch_shapes=[pltpu.VMEM((2, rows8, LANES), F32), pltpu.VMEM((MOE_ROWS, d), BF16),
                            pltpu.SemaphoreType.DMA((2,))]),
        out_shape=jax.ShapeDtypeStruct((n_blocks * rows8, LANES), F32),
        compiler_params=_params("arbitrary"),
        name="experts",
    )(blk_e, n_active, tok, h3, wgu_bf, b_gu.reshape(e, 1, f2), wd_bf, b_down.reshape(e, 1, d))


def _dispatch_kernel(pos_ref, h3_ref, xs_in_ref, xs_ref, sem, *, tm, tile_off):
    del xs_in_ref
    i = pl.program_id(0)
    n = pl.num_programs(0)

    def tile_copy(rows8):
        return pltpu.make_async_copy(h3_ref.at[pl.ds(0, rows8), :], xs_ref.at[pl.ds(0, rows8), :], sem.at[0])

    def body(r, carry):
        src = h3_ref.at[pl.ds(pl.multiple_of((i * tm + r) * SUBLANES, SUBLANES), SUBLANES), :]
        for k in range(TOP_K):
            p = pos_ref[((i + tile_off) * tm + r) * TOP_K + k]
            pltpu.make_async_copy(src, xs_ref.at[pl.ds(pl.multiple_of(p * SUBLANES, SUBLANES), SUBLANES), :],
                                  sem.at[0]).start()
        return carry
    lax.fori_loop(0, tm, body, 0)

    @pl.when(i > 0)
    def _():
        tile_copy(tm * TOP_K * SUBLANES).wait()

    @pl.when(i == n - 1)
    def _():
        tile_copy(tm * TOP_K * SUBLANES).wait()


def _dispatch(pos_flat, h3, xs, *, tm, tile_off):
    n8 = h3.shape[0]
    return pl.pallas_call(
        functools.partial(_dispatch_kernel, tm=tm, tile_off=tile_off),
        grid_spec=pltpu.PrefetchScalarGridSpec(
            num_scalar_prefetch=1,
            grid=(n8 // (tm * SUBLANES),),
            in_specs=[pl.BlockSpec(memory_space=pl.ANY), pl.BlockSpec(memory_space=pl.ANY)],
            out_specs=pl.BlockSpec(memory_space=pl.ANY),
            scratch_shapes=[pltpu.SemaphoreType.DMA((1,))]),
        out_shape=jax.ShapeDtypeStruct(xs.shape, xs.dtype),
        input_output_aliases={2: 0},
        compiler_params=_params("arbitrary"),
        name="dispatch",
    )(pos_flat, h3, xs)


def _expert_block_kernel(be_ref, na_ref, x_ref, wgu_ref, bgu_ref, wd_ref, bd_ref, y_ref, xb):
    i = pl.program_id(0)

    @pl.when(i < na_ref[0])
    def _():
        d = xb.shape[1]
        for s in range(d // LANES):
            xb[:, s * LANES:(s + 1) * LANES] = x_ref[pl.ds(s, MOE_ROWS, stride=SUBLANES), :].astype(BF16)
        gu = _dot(xb[...], wgu_ref[0]) + bgu_ref[0]
        f = gu.shape[1] // 2
        gt = jnp.minimum(gu[:, 0:f], SWIGLU_LIMIT)
        up = jnp.clip(gu[:, f:2 * f], -SWIGLU_LIMIT, SWIGLU_LIMIT)
        act = (up + 1.0) * (gt * jax.nn.sigmoid(SWIGLU_ALPHA * gt))
        y = _dot(act.astype(BF16), wd_ref[0]) + bd_ref[0]
        for s in range(d // LANES):
            y_ref[pl.ds(s, MOE_ROWS, stride=SUBLANES), :] = y[:, s * LANES:(s + 1) * LANES]

    @pl.when(i >= na_ref[0])
    def _():
        y_ref[...] = jnp.zeros(y_ref.shape, F32)


def _expert_blocks(blk_e, n_active, xs, wgu_bf, b_gu, wd_bf, b_down):
    n_blocks = blk_e.shape[0]
    e, d, f2 = wgu_bf.shape
    rows8 = MOE_ROWS * SUBLANES
    row_map = lambda i, be, na: (jnp.minimum(i, na[0] - 1), 0)
    return pl.pallas_call(
        _expert_block_kernel,
        grid_spec=pltpu.PrefetchScalarGridSpec(
            num_scalar_prefetch=2,
            grid=(n_blocks,),
            in_specs=[pl.BlockSpec((rows8, LANES), row_map),
                      pl.BlockSpec((1, d, f2), lambda i, be, na: (be[i], 0, 0)),
                      pl.BlockSpec((1, 1, f2), lambda i, be, na: (be[i], 0, 0)),
                      pl.BlockSpec((1, f2 // 2, d), lambda i, be, na: (be[i], 0, 0)),
                      pl.BlockSpec((1, 1, d), lambda i, be, na: (be[i], 0, 0))],
            out_specs=pl.BlockSpec((rows8, LANES), lambda i, be, na: (i, 0)),
            scratch_shapes=[pltpu.VMEM((MOE_ROWS, d), BF16)]),
        out_shape=jax.ShapeDtypeStruct((n_blocks * rows8, LANES), F32),
        compiler_params=_params("arbitrary"),
        name="experts",
    )(blk_e, n_active, xs, wgu_bf, b_gu.reshape(e, 1, f2), wd_bf, b_down.reshape(e, 1, d))


def _routing_pos(top_e, n_tok):
    n_assign = n_tok * TOP_K
    n_pad = -(-n_assign // LANES) * LANES
    e_flat = jnp.pad(top_e.reshape(-1), (0, n_pad - n_assign), constant_values=N_EXPERTS)
    onehot = (e_flat[:, None] == jnp.arange(N_EXPERTS)[None, :]).astype(F32).reshape(n_pad // LANES, LANES, N_EXPERTS)
    tril = jnp.tril(jnp.ones((LANES, LANES), F32))
    within = jnp.einsum("ij,bjk->bik", tril, onehot)
    block_tot = within[:, -1, :]
    offs = jnp.cumsum(block_tot, axis=0) - block_tot
    counts = jnp.sum(block_tot, axis=0).astype(jnp.int32)
    padded = (counts + MOE_ROWS - 1) // MOE_ROWS * MOE_ROWS
    pad_end = jnp.cumsum(padded)
    pad_start = (pad_end - padded).astype(F32)
    slot = jnp.sum(onehot * (within + offs[:, None, :] - 1.0 + pad_start[None, None, :]), axis=-1)
    pos = slot.reshape(-1)[:n_assign].astype(jnp.int32)
    n_blocks = -(-n_assign // MOE_ROWS) + N_EXPERTS
    starts = jnp.arange(n_blocks, dtype=jnp.int32) * MOE_ROWS
    blk_e = jnp.minimum(jnp.sum((pad_end[None, :] <= starts[:, None]).astype(jnp.int32), axis=1), N_EXPERTS - 1)
    n_active = (pad_end[-1] // MOE_ROWS).astype(jnp.int32).reshape(1)
    return blk_e, n_active, pos


def _combine_kernel(pos_ref, ys_ref, y1_ref, gt_ref, tw_ref, o_ref, buf, sem, *, tile_off):
    i = pl.program_id(0)
    n = pl.num_programs(0)
    tm = y1_ref.shape[0]
    rows8 = tm * SUBLANES

    def gather(tile, slot):
        def body(r, carry):
            for k in range(TOP_K):
                p = pos_ref[((tile + tile_off) * tm + r) * TOP_K + k]
                pltpu.make_async_copy(ys_ref.at[pl.ds(pl.multiple_of(p * SUBLANES, SUBLANES), SUBLANES), :],
                                      buf.at[slot, k, pl.ds(pl.multiple_of(r * SUBLANES, SUBLANES), SUBLANES), :],
                                      sem.at[slot]).start()
            return carry
        lax.fori_loop(0, tm, body, 0)

    @pl.when(i == 0)
    def _():
        gather(0, 0)

    slot = i % 2
    for k in range(TOP_K):
        pltpu.make_async_copy(ys_ref.at[pl.ds(0, rows8), :], buf.at[slot, k], sem.at[slot]).wait()

    @pl.when(i + 1 < n)
    def _():
        gather(i + 1, 1 - slot)

    tw = tw_ref[...]
    wk = [jnp.broadcast_to(tw[:, k:k + 1], (tm, LANES)) for k in range(TOP_K)]
    gt = gt_ref[0]
    for s in range(o_ref.shape[1] // LANES):
        moe = wk[0] * buf[slot, 0, pl.ds(s, tm, stride=SUBLANES), :]
        for k in range(1, TOP_K):
            moe = moe + wk[k] * buf[slot, k, pl.ds(s, tm, stride=SUBLANES), :]
        cols = slice(s * LANES, (s + 1) * LANES)
        o_ref[:, cols] = y1_ref[:, cols] + gt[:, cols] * moe


def _combine(pos_flat, ys, y1, gate, tw, *, tm, rows_per_mod, tile_off):
    n, d = y1.shape
    mod_r = gate.shape[1]
    return pl.pallas_call(
        functools.partial(_combine_kernel, tile_off=tile_off),
        grid_spec=pltpu.PrefetchScalarGridSpec(
            num_scalar_prefetch=1,
            grid=(n // tm,),
            in_specs=[pl.BlockSpec(memory_space=pl.ANY),
                      pl.BlockSpec((tm, d), lambda i, pos: (i, 0)),
                      pl.BlockSpec((1, mod_r, d), lambda i, pos: (i // (rows_per_mod // tm), 0, 0)),
                      pl.BlockSpec((tm, LANES), lambda i, pos: (i, 0))],
            out_specs=pl.BlockSpec((tm, d), lambda i, pos: (i, 0)),
            scratch_shapes=[pltpu.VMEM((2, TOP_K, tm * SUBLANES, LANES), F32), pltpu.SemaphoreType.DMA((2,))]),
        out_shape=jax.ShapeDtypeStruct((n, d), F32),
        compiler_params=_params("arbitrary"),
        name="combine",
    )(pos_flat, ys, y1, gate, tw)


def _rope_tables(pos):
    half = HEAD_DIM // 2
    inv = ROPE_THETA ** (-jnp.arange(half, dtype=F32) / half)
    ang = pos.astype(F32)[:, None] * inv[None, :]
    cos = jnp.cos(ang)
    sin = jnp.sin(ang)
    reps = LANES // HEAD_DIM
    return (jnp.tile(jnp.concatenate([cos, cos], axis=1), (1, reps)),
            jnp.tile(jnp.concatenate([-sin, sin], axis=1), (1, reps)))


def _routing(top_e, n_tok):
    n_assign = n_tok * TOP_K
    e_flat = top_e.reshape(-1)
    order = jnp.argsort(e_flat)
    e_sorted = e_flat[order]
    tok_sorted = (order // TOP_K).astype(jnp.int32)
    counts = jnp.bincount(e_flat, length=N_EXPERTS)
    padded = (counts + MOE_ROWS - 1) // MOE_ROWS * MOE_ROWS
    pad_end = jnp.cumsum(padded)
    pad_start = pad_end - padded
    sort_start = jnp.cumsum(counts) - counts
    dest = (pad_start[e_sorted] + jnp.arange(n_assign, dtype=jnp.int32) - sort_start[e_sorted]).astype(jnp.int32)
    n_blocks = -(-n_assign // MOE_ROWS) + N_EXPERTS
    tok = jnp.full((n_blocks * MOE_ROWS,), n_tok, jnp.int32).at[dest].set(tok_sorted)
    blk_e = jnp.minimum(jnp.searchsorted(pad_end, jnp.arange(n_blocks, dtype=jnp.int32) * MOE_ROWS, side='right'),
                        N_EXPERTS - 1).astype(jnp.int32)
    pos = jnp.zeros((n_assign,), jnp.int32).at[order].set(dest)
    n_active = (pad_end[-1] // MOE_ROWS).astype(jnp.int32).reshape(1)
    return blk_e, n_active, tok, pos


def _ones_pad(a):
    return jnp.broadcast_to((jnp.arange(a.shape[-1]) == 0).astype(a.dtype), a.shape)


def _head_ones(a):
    return jnp.concatenate([a, _ones_pad(a)], axis=-1)


def _head_pad(a):
    return jnp.concatenate([a, jnp.zeros_like(a)], axis=-1)


def kernel(x_prompt, x_sample, c_prompt, c_sample, cache_moba_kv, cache_nsa_kv, state_nsa_win_kv, page_table,
           norm_g, w_ada, b_ada, w_in, qk_gain, cmp_pos, cmp_w1, cmp_w2, w_out, w_router, b_router, w_gu, b_gu,
           w_down, b_down):
    bsz, seq, d = x_prompt.shape
    n_req = x_sample.shape[0]
    depth = norm_g.shape[0]
    assert depth == 1 and x_sample.shape[1] == 1
    assert seq % ATT_TK == 0 and seq >= WINDOW + Q_TILE and seq // SEL_BLOCK <= LANES and seq // MOBA_BLOCK <= MOBA_BLOCK // SUBLANES
    n_pool, page = cache_moba_kv.shape[1], cache_moba_kv.shape[2]
    n_pages = page_table.shape[1]
    past = n_pages * page
    assert past % MOBA_BLOCK == 0 and 2 * page == MOBA_BLOCK and past // SEL_BLOCK < LANES
    assert state_nsa_win_kv.shape[2] == WINDOW
    layer = 0
    gw = G_NSA * HEAD_DIM
    n_prompt = bsz * seq

    bd = jnp.asarray(np.kron(np.eye(LANES // HEAD_DIM), np.full((HEAD_DIM, HEAD_DIM), 1.0 / HEAD_DIM)), BF16)
    w_in_bf = jnp.pad(w_in[layer], ((0, 0), (0, IN_COLS_PAD - IN_COLS))).astype(BF16)
    gains = jnp.tile(qk_gain[layer], (1, W_QM // HEAD_DIM))
    g1 = norm_g[layer, 0].reshape(1, d)
    g2 = norm_g[layer, 1].reshape(1, d)
    wo_bf = w_out[layer].astype(BF16)
    wr_pad = jnp.pad(w_router[layer], ((0, 0), (0, LANES - N_EXPERTS)))
    br_pad = jnp.pad(b_router[layer].reshape(1, N_EXPERTS), ((0, 0), (0, LANES - N_EXPERTS)),
                     constant_values=-jnp.inf)
    wgu_bf = w_gu[layer].astype(BF16)
    wd_bf = w_down[layer].astype(BF16)
    cmp_consts = _compress_consts(cmp_pos[layer], cmp_w1[layer], cmp_w2[layer], qk_gain[layer, 3])

    n_c = bsz + n_req
    n_c_pad = -(-n_c // SUBLANES) * SUBLANES
    c_all = jnp.pad(jnp.concatenate([c_prompt, c_sample], axis=0), ((0, n_c_pad - n_c), (0, 0)))
    mods = _ada(c_all, w_ada[layer], b_ada[layer])
    mods_p = [m.reshape(bsz, 1, d) for m in jnp.split(mods[:bsz], 6, axis=1)]
    mods_s = [m.reshape(1, n_req, d) for m in jnp.split(mods[bsz:n_c], 6, axis=1)]

    cos_p, sin_p = _rope_tables(jnp.arange(seq, dtype=jnp.int32))
    tm_p = 256
    moba_rows, nsa_rows, win_rows, q_m, qn, qr, gates, kmean = _proj(
        x_prompt.reshape(n_prompt, d), mods_p[0], mods_p[1], g1, w_in_bf, gains, bd, cos_p, sin_p,
        tm=tm_p, rows_per_mod=seq, pos_blocks=seq // tm_p, with_kmean=True)

    nbk = seq // MOBA_BLOCK
    pos = jnp.arange(seq, dtype=jnp.int32)
    k_m = moba_rows[:, :W_QM].reshape(bsz, seq, H_MOBA, HEAD_DIM).astype(BF16)
    oh_m = (pos[:, None] // MOBA_BLOCK == jnp.arange(MOBA_BLOCK // SUBLANES)[None, :]).astype(BF16)
    oh_m = jnp.broadcast_to(oh_m[None, :, None, :], (bsz, seq, H_MOBA, MOBA_BLOCK // SUBLANES))
    k_aug = jnp.concatenate([k_m, oh_m, jnp.zeros_like(oh_m)], axis=-1).reshape(bsz, seq, H_MOBA * LANES)
    v_m = moba_rows[:, W_QM:].reshape(bsz, seq, H_MOBA, HEAD_DIM).astype(BF16)
    v_aug = jnp.concatenate([v_m, _ones_pad(v_m)], axis=-1).reshape(bsz, seq, H_MOBA * LANES)
    kmean_h = kmean.reshape(bsz, nbk, H_MOBA, HEAD_DIM).transpose(0, 2, 1, 3)
    kmp = jnp.zeros((bsz, H_MOBA, LANES, LANES), F32).at[:, :, HEAD_DIM:HEAD_DIM + nbk, :HEAD_DIM].set(kmean_h)
    o_m = _moba_attention(q_m.reshape(bsz, seq, W_QM), kmp, k_aug, v_aug)

    nsa4 = nsa_rows.reshape(bsz, seq, 4, G_NSA, HEAD_DIM)
    win4 = win_rows.reshape(bsz, seq, 2, G_NSA, HEAD_DIM)
    kc, vc = _compress_prompt(nsa_rows.reshape(bsz, seq, 4 * gw), cmp_consts, bd)
    n_chunk = seq // CMP_STRIDE
    per_g = lambda a: a.reshape(bsz, -1, G_NSA, HEAD_DIM).transpose(0, 2, 1, 3)
    kcp = _head_pad(per_g(kc))
    vcd = _head_pad(per_g(vc)).astype(BF16)
    oh_s = (pos[:, None] // SEL_BLOCK == jnp.arange(LANES)[None, :]).astype(BF16)
    k_sel = per_g(nsa4[:, :, 2]).astype(BF16)
    ks_aug = jnp.concatenate([jnp.broadcast_to(oh_s[None, None], (bsz, G_NSA, seq, LANES)), k_sel,
                              jnp.zeros_like(k_sel)], axis=-1)
    vs_dup = _head_ones(per_g(nsa4[:, :, 3]).astype(BF16))
    kw_pad = _head_pad(per_g(win4[:, :, 0])).astype(BF16)
    vw_dup = _head_ones(per_g(win4[:, :, 1]).astype(BF16))
    gates_g = jnp.pad(gates[:, :N_GATE].reshape(bsz, seq, G_NSA, 3 * R_NSA).transpose(0, 2, 1, 3),
                      ((0, 0), (0, 0), (0, 0), (0, LANES - 3 * R_NSA)))
    cover_p = _cover(n_chunk - 1, seq // SEL_BLOCK, n_chunk).T.astype(BF16)
    o_n = _nsa_attention(qn.reshape(bsz, seq, W_QN), qr.reshape(bsz, seq, W_QN), gates_g, kcp, vcd, cover_p,
                         ks_aug, vs_dup, kw_pad, vw_dup)
    o_p = jnp.concatenate([o_m, o_n], axis=-1).reshape(n_prompt, d)

    y1_p, h3_p, te_p, tw_p = _post(o_p, x_prompt.reshape(n_prompt, d), mods_p[2], mods_p[3], mods_p[4], g2, wo_bf,
                                   wr_pad, br_pad, tm=256, rows_per_mod=seq)

    cos_s, sin_s = _rope_tables(jnp.full((n_req,), past, jnp.int32))
    moba_new, nsa_new, win_new, q_m_s, qn_s, qr_s, gates_s = _proj(
        x_sample.reshape(n_req, d), mods_s[0], mods_s[1], g1, w_in_bf, gains, bd, cos_s, sin_s,
        tm=n_req, rows_per_mod=n_req, pos_blocks=1, with_kmean=False)
    pt_flat = page_table.reshape(-1).astype(jnp.int32)
    cache_m = cache_moba_kv[layer].transpose(0, 2, 3, 4, 1).reshape(n_pool, W_KVM, page)
    cache_n = cache_nsa_kv[layer].transpose(0, 2, 3, 4, 1).reshape(n_pool, 4 * gw, page)
    win_buf = state_nsa_win_kv[layer].transpose(0, 2, 3, 4, 1).reshape(n_req, 2 * gw, WINDOW)

    def rows8(q):
        qh = q.reshape(n_req, G_NSA, R_NSA, 1, HEAD_DIM)
        place = jnp.arange(G_NSA)[None, :, None, None, None] == jnp.arange(G_NSA)[None, None, None, :, None]
        return jnp.where(place, qh, 0.0).reshape(n_req, H_NSA, gw)

    def per_head(a):
        return jnp.repeat(a.reshape(n_req, G_NSA, HEAD_DIM), R_NSA, axis=1).reshape(n_req, W_QN)

    new_rows = jnp.stack([moba_new[:, :W_QM], moba_new[:, W_QM:], per_head(nsa_new[:, 2 * gw:3 * gw]),
                          per_head(nsa_new[:, 3 * gw:]), per_head(win_new[:, :gw]), per_head(win_new[:, gw:])], axis=1)
    gate_rows = jnp.repeat(gates_s[:, :N_GATE].reshape(n_req, H_NSA, 3).transpose(0, 2, 1), HEAD_DIM, axis=2)
    n_cmp_s = past // CMP_STRIDE
    cover_s = _cover(n_cmp_s - 1, past // SEL_BLOCK + 1, n_cmp_s).T
    wcat, pecat = _compress_cat_consts(cmp_pos[layer], cmp_w1[layer])
    o_s = _dec2_attention(cache_m, cache_n, win_buf, pt_flat, jnp.stack([q_m_s, qr_s], axis=1), rows8(qn_s), new_rows,
                          gate_rows, wcat, pecat, cmp_consts[2], cmp_consts[3], bd, cover_s,
                          n_req=n_req, n_pages=n_pages).reshape(n_req, d).astype(BF16)
    y1_s, h3_s, te_s, tw_s = _post(o_s, x_sample.reshape(n_req, d), mods_s[2], mods_s[3], mods_s[4], g2, wo_bf,
                                   wr_pad, br_pad, tm=n_req, rows_per_mod=n_req)

    n_tok = n_prompt + n_req
    top_e = jnp.concatenate([te_p[:, :TOP_K], te_s[:, :TOP_K]], axis=0)
    blk_e, n_active, pos_flat = _routing_pos(top_e, n_tok)
    tm_c = 128
    xs = jnp.zeros((blk_e.shape[0] * MOE_ROWS * SUBLANES, LANES), F32)
    xs = _dispatch(pos_flat, h3_p, xs, tm=tm_c, tile_off=0)
    xs = _dispatch(pos_flat, h3_s, xs, tm=n_req, tile_off=n_prompt // n_req)
    ys = _expert_blocks(blk_e, n_active, xs, wgu_bf, b_gu[layer], wd_bf, b_down[layer])
    y_p = _combine(pos_flat, ys, y1_p, mods_p[5], tw_p, tm=tm_c, rows_per_mod=seq, tile_off=0)
    y_s = _combine(pos_flat, ys, y1_s, mods_s[5], tw_s, tm=n_req, rows_per_mod=n_req, tile_off=n_prompt // n_req)

    keep = min(WINDOW, seq)
    win_p = win_rows.reshape(bsz, seq, 2, G_NSA, HEAD_DIM)[:, seq - keep:]
    win_s = jnp.concatenate([state_nsa_win_kv[layer][:, 1:], win_new.reshape(n_req, 1, 2, G_NSA, HEAD_DIM)], axis=1)
    return (y_p.reshape(bsz, seq, d), y_s.reshape(n_req, 1, d),
            moba_rows.reshape(1, bsz, seq, 2, H_MOBA, HEAD_DIM), nsa_rows.reshape(1, bsz, seq, 4, G_NSA, HEAD_DIM),
            win_p[None], moba_new.reshape(1, n_req, 1, 2, H_MOBA, HEAD_DIM),
            nsa_new.reshape(1, n_req, 1, 4, G_NSA, HEAD_DIM), win_s[None])
```

```python
import functools

import numpy as np
import jax
import jax.numpy as jnp
from jax import lax
from jax.experimental import pallas as pl
from jax.experimental.pallas import tpu as pltpu

F32 = jnp.float32
BF16 = jnp.bfloat16
HIGHEST = lax.Precision.HIGHEST

LANES = 128
SUBLANES = 8
HEAD_DIM = 64
H_MOBA = 8
H_NSA = 8
G_NSA = 2
R_NSA = H_NSA // G_NSA
MOBA_BLOCK = 256
MOBA_TOPK = 3
CMP_LEN = 32
CMP_STRIDE = 16
SEL_BLOCK = 64
SEL_TOPN = 16
WINDOW = 512
N_EXPERTS = 32
TOP_K = 4
SWIGLU_LIMIT = 7.0
SWIGLU_ALPHA = 1.702
ROPE_THETA = 10000.0
NORM_EPS = 1e-6
NEG_BIG = -1e30
TINY = 1e-30
SCALE = HEAD_DIM ** -0.5
SCALE_LOG2E = SCALE * 1.4426950408889634
Q_TILE = 128
MOBA_Q = MOBA_BLOCK
ATT_TK = 1024
MOE_ROWS = 128
VMEM_LIMIT = 56 * 1024 * 1024

W_QM = H_MOBA * HEAD_DIM
W_KVM = 2 * H_MOBA * HEAD_DIM
W_QN = H_NSA * HEAD_DIM
W_KVN = 6 * G_NSA * HEAD_DIM
N_GATE = 3 * H_NSA
IN_COLS = W_QM + W_KVM + W_QN + W_KVN + N_GATE
IN_COLS_PAD = W_QM + W_KVM + W_QN + W_KVN + LANES


def _iota(shape, dim):
    return lax.broadcasted_iota(jnp.int32, shape, dim)


def _dot(a, b, precision=None):
    return jnp.dot(a, b, preferred_element_type=F32, precision=precision)


def _dot_nt(a, b, precision=None):
    return lax.dot_general(a, b, (((1,), (1,)), ((), ())), preferred_element_type=F32, precision=precision)


def _split_bf16(a):
    hi = a.astype(BF16)
    return hi, (a - hi.astype(F32)).astype(BF16)


def _dot_nt_x3(a, b):
    ah, al = _split_bf16(a)
    bh, bl = _split_bf16(b)
    return _dot_nt(ah, bh) + _dot_nt(al, bh) + _dot_nt(ah, bl)


def _params(*sem):
    return pltpu.CompilerParams(dimension_semantics=sem, vmem_limit_bytes=VMEM_LIMIT)


def _seg_meansq(z, bd):
    zz = z * z
    hi = zz.astype(BF16)
    lo = (zz - hi.astype(F32)).astype(BF16)
    outs = []
    for c in range(z.shape[1] // LANES):
        sl = slice(c * LANES, (c + 1) * LANES)
        outs.append(_dot(hi[:, sl], bd) + _dot(lo[:, sl], bd))
    return outs[0] if len(outs) == 1 else jnp.concatenate(outs, axis=1)


def _head_norm(z, gain, bd):
    return z * lax.rsqrt(_seg_meansq(z, bd) + NORM_EPS) * gain


def _rope(z, cos, sin):
    outs = []
    first = (_iota((z.shape[0], LANES), 1) % HEAD_DIM) < (HEAD_DIM // 2)
    for c in range(z.shape[1] // LANES):
        x = z[:, c * LANES:(c + 1) * LANES]
        swapped = jnp.where(first, pltpu.roll(x, LANES - HEAD_DIM // 2, 1), pltpu.roll(x, HEAD_DIM // 2, 1))
        outs.append(x * cos + swapped * sin)
    return outs[0] if len(outs) == 1 else jnp.concatenate(outs, axis=1)


def _top_k_lanes(cur, k):
    lane = _iota(cur.shape, 1).astype(F32)
    picked = jnp.zeros(cur.shape, F32)
    vals, ids = [], []
    for _ in range(k):
        mx = jnp.max(cur, axis=1, keepdims=True)
        first = jnp.min(jnp.where(cur == mx, lane, 1e9), axis=1, keepdims=True)
        hit = lane == first
        picked = jnp.where(hit, 1.0, picked)
        cur = jnp.where(hit, -jnp.inf, cur)
        vals.append(mx)
        ids.append(first)
    return picked, vals, ids


def _top_k_sublanes(cur, k):
    idx = _iota(cur.shape, 0).astype(F32)
    picked = jnp.zeros(cur.shape, F32)
    for _ in range(k):
        mx = jnp.max(cur, axis=0, keepdims=True)
        first = jnp.min(jnp.where(cur == mx, idx, 1e9), axis=0, keepdims=True)
        hit = idx == first
        picked = jnp.where(hit, 1.0, picked)
        cur = jnp.where(hit, -jnp.inf, cur)
    return picked


def _flash_step(q, k, v, mask, m, acc):
    s = _dot_nt(q, k)
    if mask is not None:
        s = jnp.where(mask, s, NEG_BIG)
    m_new = jnp.maximum(m, jnp.max(s, axis=1, keepdims=True))
    acc_new = jnp.exp2(m - m_new) * acc + _dot(jnp.exp2(s - m_new).astype(BF16), v)
    return m_new, acc_new


def _flash_finish(acc):
    return acc / jnp.maximum(acc[:, HEAD_DIM:HEAD_DIM + 1], TINY)


def _ada_kernel(c_ref, w_ref, b_ref, o_ref):
    c = c_ref[...]
    o_ref[...] = _dot(c * jax.nn.sigmoid(c), w_ref[...], HIGHEST) + b_ref[...]


def _ada(c_all, w_ada, b_ada):
    n, d = c_all.shape
    cols = w_ada.shape[1]
    tn = 1024
    return pl.pallas_call(
        _ada_kernel,
        grid=(cols // tn,),
        in_specs=[pl.BlockSpec((n, d), lambda j: (0, 0)),
                  pl.BlockSpec((d, tn), lambda j: (0, j)),
                  pl.BlockSpec((1, tn), lambda j: (0, j))],
        out_specs=pl.BlockSpec((n, tn), lambda j: (0, j)),
        out_shape=jax.ShapeDtypeStruct((n, cols), F32),
        compiler_params=_params("arbitrary"),
        name="ada",
    )(c_all, w_ada, b_ada.reshape(1, cols))


def _proj_kernel(x_ref, sh_ref, sc_ref, g_ref, w_ref, gains_ref, bd_ref, cos_ref, sin_ref,
                 moba_ref, nsa_ref, win_ref, qm_ref, qn_ref, qr_ref, gate_ref, *attn_refs, with_kmean, pos_blocks):
    x = x_ref[...]
    y = x * lax.rsqrt(jnp.mean(x * x, axis=1, keepdims=True) + NORM_EPS) * g_ref[...]
    h = (y * (1.0 + sc_ref[0]) + sh_ref[0]).astype(BF16)
    bd = bd_ref[...]
    cos = cos_ref[...]
    sin = sin_ref[...]
    o = 0

    def seg(width):
        nonlocal o
        z = _dot(h, w_ref[:, o:o + width])
        o += width
        return z

    def gain(i, width):
        return gains_ref[i:i + 1, 0:width]

    qm_ref[...] = _rope(_head_norm(seg(W_QM), gain(0, W_QM), bd), cos, sin)
    k_m = _rope(_head_norm(seg(W_QM), gain(1, W_QM), bd), cos, sin)
    v_m = seg(W_QM)
    moba_ref[:, 0:W_QM] = k_m
    moba_ref[:, W_QM:2 * W_QM] = v_m
    qn = _head_norm(seg(W_QN), gain(2, W_QN), bd)
    qn_ref[...] = qn
    qr_ref[...] = _rope(qn, cos, sin)
    gw = G_NSA * HEAD_DIM
    nsa_ref[:, 0:2 * gw] = seg(2 * gw)
    k_sel = _rope(_head_norm(seg(gw), gain(4, gw), bd), cos, sin)
    v_sel = seg(gw)
    k_win = _rope(_head_norm(seg(gw), gain(5, gw), bd), cos, sin)
    v_win = seg(gw)
    nsa_ref[:, 2 * gw:3 * gw] = k_sel
    nsa_ref[:, 3 * gw:4 * gw] = v_sel
    win_ref[:, 0:gw] = k_win
    win_ref[:, gw:2 * gw] = v_win
    gates = jax.nn.sigmoid(seg(LANES))
    gate_ref[...] = gates
    if with_kmean:
        kmean_ref, kaug_ref, vaug_ref, ksaug_ref, vsaug_ref, kwp_ref, vwaug_ref, gg_ref = attn_refs
        tm = k_m.shape[0]
        kmean_ref[0] = jnp.mean(k_m.reshape(tm // MOBA_BLOCK, MOBA_BLOCK, W_QM), axis=1)
        lane = _iota((tm, LANES), 1)
        posv = (pl.program_id(0) % pos_blocks) * tm + _iota((tm, LANES), 0)
        lo = lane < HEAD_DIM
        pad_blk = jnp.where((lane >= HEAD_DIM) & (lane - HEAD_DIM == posv // MOBA_BLOCK), 1.0, 0.0)
        pad_one = jnp.where(lane == HEAD_DIM, 1.0, 0.0)
        oh_sel = jnp.where(lane == posv // SEL_BLOCK, 1.0, 0.0).astype(BF16)

        def halves(x):
            return x, pltpu.roll(x, HEAD_DIM, 1)

        for c in range(W_QM // LANES):
            cols = slice(c * LANES, (c + 1) * LANES)
            for hh, (kh, vh) in enumerate(zip(halves(k_m[:, cols]), halves(v_m[:, cols]))):
                hcols = slice((2 * c + hh) * LANES, (2 * c + hh + 1) * LANES)
                kaug_ref[:, hcols] = jnp.where(lo, kh, pad_blk).astype(BF16)
                vaug_ref[:, hcols] = jnp.where(lo, vh, pad_one).astype(BF16)
        for g, (ks, vs, kw, vw) in enumerate(zip(halves(k_sel), halves(v_sel), halves(k_win), halves(v_win))):
            ksaug_ref[:, 2 * g * LANES:(2 * g + 1) * LANES] = oh_sel
            ksaug_ref[:, (2 * g + 1) * LANES:(2 * g + 2) * LANES] = jnp.where(lo, ks, 0.0).astype(BF16)
            gcols = slice(g * LANES, (g + 1) * LANES)
            vsaug_ref[:, gcols] = jnp.where(lo, vs, pad_one).astype(BF16)
            kwp_ref[:, gcols] = jnp.where(lo, kw, 0.0).astype(BF16)
            vwaug_ref[:, gcols] = jnp.where(lo, vw, pad_one).astype(BF16)
            gg_ref[:, gcols] = gates if g == 0 else pltpu.roll(gates, LANES - g * 3 * R_NSA, 1)


def _proj(x, shift, scale, g, w_in_bf, gains, bd, cos, sin, *, tm, rows_per_mod, pos_blocks, with_kmean):
    n, d = x.shape
    nt = n // tm
    mod_r = shift.shape[1]
    mod_map = lambda i: (i // (rows_per_mod // tm), 0, 0)
    pos_map = lambda i: (i % pos_blocks, 0)
    row = lambda w: pl.BlockSpec((tm, w), lambda i: (i, 0))
    const = lambda a: pl.BlockSpec(a.shape, lambda i: (0,) * a.ndim)
    out_shapes = [jax.ShapeDtypeStruct((n, W_KVM), F32), jax.ShapeDtypeStruct((n, 4 * G_NSA * HEAD_DIM), F32),
                  jax.ShapeDtypeStruct((n, 2 * G_NSA * HEAD_DIM), F32), jax.ShapeDtypeStruct((n, W_QM), F32),
                  jax.ShapeDtypeStruct((n, W_QN), F32), jax.ShapeDtypeStruct((n, W_QN), F32),
                  jax.ShapeDtypeStruct((n, LANES), F32)]
    out_specs = [row(W_KVM), row(4 * G_NSA * HEAD_DIM), row(2 * G_NSA * HEAD_DIM), row(W_QM), row(W_QN), row(W_QN),
                 row(LANES)]
    if with_kmean:
        nbt = tm // MOBA_BLOCK
        out_shapes.append(jax.ShapeDtypeStruct((nt, nbt, W_QM), F32))
        out_specs.append(pl.BlockSpec((1, nbt, W_QM), lambda i: (i, 0, 0)))
        gl = G_NSA * LANES
        for width, dtype in ((H_MOBA * LANES, BF16), (H_MOBA * LANES, BF16), (2 * gl, BF16), (gl, BF16), (gl, BF16),
                             (gl, BF16), (gl, F32)):
            out_shapes.append(jax.ShapeDtypeStruct((n, width), dtype))
            out_specs.append(row(width))
    return pl.pallas_call(
        functools.partial(_proj_kernel, with_kmean=with_kmean, pos_blocks=pos_blocks),
        grid=(nt,),
        in_specs=[row(d), pl.BlockSpec((1, mod_r, d), mod_map), pl.BlockSpec((1, mod_r, d), mod_map),
                  const(g), const(w_in_bf), const(gains), const(bd),
                  pl.BlockSpec((tm, LANES), pos_map), pl.BlockSpec((tm, LANES), pos_map)],
        out_specs=out_specs,
        out_shape=out_shapes,
        compiler_params=_params("parallel"),
        name="proj",
    )(x, shift, scale, g, w_in_bf, gains, bd, cos, sin)


def _compress_compute(src_refs, pe_ref, w1_ref, w2_ref, gk_ref, bd_ref, kc_ref, vc_ref, n_rows):
    n_chunk = n_rows // CMP_STRIDE
    gw = G_NSA * HEAD_DIM
    for kv in range(2):
        acc_a = jnp.zeros((n_chunk, gw), F32)
        acc_b = jnp.zeros((n_chunk, gw), F32)
        for j in range(CMP_STRIDE):
            xj = src_refs[kv][pl.ds(j, n_chunk, stride=CMP_STRIDE), :]
            acc_a = acc_a + _dot(xj + pe_ref[kv, 0, j:j + 1, :], w1_ref[kv, 0, j], HIGHEST)
            acc_b = acc_b + _dot(xj + pe_ref[kv, 1, j:j + 1, :], w1_ref[kv, 1, j], HIGHEST)
        hid = jax.nn.gelu(acc_a + pltpu.roll(acc_b, n_chunk - 1, 0))
        out = _dot(hid, w2_ref[kv], HIGHEST)
        if kv == 0:
            kc_ref[0] = _head_norm(out, gk_ref[...], bd_ref[...])
        else:
            vc_ref[0] = out


def _compress_prompt_kernel(k_ref, v_ref, pe_ref, w1_ref, w2_ref, gk_ref, bd_ref, kc_ref, vc_ref, *, n_rows):
    _compress_compute((k_ref.at[0], v_ref.at[0]), pe_ref, w1_ref, w2_ref, gk_ref, bd_ref, kc_ref, vc_ref, n_rows)


def _compress_paged_kernel(pt_ref, page_ref, pe_ref, w1_ref, w2_ref, gk_ref, bd_ref, kc_ref, vc_ref, kbuf, vbuf,
                           *, n_rows, page):
    p = pl.program_id(1)
    gw = G_NSA * HEAD_DIM
    rows = pl.ds(pl.multiple_of(p * page, page), page)
    kbuf[rows, :] = page_ref[0, :, 0:gw]
    vbuf[rows, :] = page_ref[0, :, gw:2 * gw]

    @pl.when(p == pl.num_programs(1) - 1)
    def _():
        _compress_compute((kbuf, vbuf), pe_ref, w1_ref, w2_ref, gk_ref, bd_ref, kc_ref, vc_ref, n_rows)


def _compress_consts(cmp_pos, cmp_w1, cmp_w2, gain_k_cmp):
    pe = jnp.tile(cmp_pos.reshape(2, 2, CMP_STRIDE, HEAD_DIM), (1, 1, 1, G_NSA))
    eye = jnp.eye(G_NSA, dtype=F32)
    w1 = cmp_w1.reshape(2, 2, CMP_STRIDE, HEAD_DIM, HEAD_DIM)
    w1bd = jnp.einsum("gh,kajde->kajgdhe", eye, w1).reshape(2, 2, CMP_STRIDE, G_NSA * HEAD_DIM, G_NSA * HEAD_DIM)
    w2bd = jnp.einsum("gh,kde->kgdhe", eye, cmp_w2).reshape(2, G_NSA * HEAD_DIM, G_NSA * HEAD_DIM)
    gk = jnp.tile(gain_k_cmp.reshape(1, HEAD_DIM), (1, G_NSA))
    return pe, w1bd, w2bd, gk


def _compress_prompt(nsa_rows, consts, bd):
    b, s, _ = nsa_rows.shape
    pe, w1bd, w2bd, gk = consts
    n_chunk = s // CMP_STRIDE
    gw = G_NSA * HEAD_DIM
    const = lambda a: pl.BlockSpec(a.shape, lambda i: (0,) * a.ndim)
    out = jax.ShapeDtypeStruct((b, n_chunk, gw), F32)
    return pl.pallas_call(
        functools.partial(_compress_prompt_kernel, n_rows=s),
        grid=(b,),
        in_specs=[pl.BlockSpec((1, s, gw), lambda i: (i, 0, 0)), pl.BlockSpec((1, s, gw), lambda i: (i, 0, 1)),
                  const(pe), const(w1bd), const(w2bd), const(gk), const(bd)],
        out_specs=[pl.BlockSpec((1, n_chunk, gw), lambda i: (i, 0, 0))] * 2,
        out_shape=[out, out],
        compiler_params=_params("parallel"),
        name="compress_prompt",
    )(nsa_rows, nsa_rows, pe, w1bd, w2bd, gk, bd)


def _compress_paged(cache_nsa, pt_flat, consts, bd, *, n_req, n_pages):
    _, page, _ = cache_nsa.shape
    pe, w1bd, w2bd, gk = consts
    n_rows = n_pages * page
    n_chunk = n_rows // CMP_STRIDE
    gw = G_NSA * HEAD_DIM
    const = lambda a: pl.BlockSpec(a.shape, lambda b, p, pt: (0,) * a.ndim)
    out = jax.ShapeDtypeStruct((n_req, n_chunk, gw), F32)
    return pl.pallas_call(
        functools.partial(_compress_paged_kernel, n_rows=n_rows, page=page),
        grid_spec=pltpu.PrefetchScalarGridSpec(
            num_scalar_prefetch=1,
            grid=(n_req, n_pages),
            in_specs=[pl.BlockSpec((1, page, 2 * gw), lambda b, p, pt: (pt[b * n_pages + p], 0, 0)),
                      const(pe), const(w1bd), const(w2bd), const(gk), const(bd)],
            out_specs=[pl.BlockSpec((1, n_chunk, gw), lambda b, p, pt: (b, 0, 0))] * 2,
            scratch_shapes=[pltpu.VMEM((n_rows, gw), F32)] * 2),
        out_shape=[out, out],
        compiler_params=_params("parallel", "arbitrary"),
        name="compress_paged",
    )(pt_flat, cache_nsa, pe, w1bd, w2bd, gk, bd)


def _moba_kernel(q_ref, kmp_ref, k_ref, v_ref, o_ref):
    qi = pl.program_id(2)
    own = qi
    q2 = q_ref[0]
    lane = _iota((MOBA_Q, LANES), 1)
    n_blk = MOBA_BLOCK // SUBLANES
    blk = _iota((n_blk, MOBA_Q), 0)
    valid = blk < own
    q_augs = []
    for h in range(2):
        qh = q2 if h == 0 else pltpu.roll(q2, HEAD_DIM, 1)
        q0 = jnp.where(lane < HEAD_DIM, qh, 0.0)
        score = _dot_nt_x3(kmp_ref[0, h], q0)[HEAD_DIM:HEAD_DIM + n_blk, :]
        picked = _top_k_sublanes(jnp.where(valid, score, -jnp.inf), MOBA_TOPK)
        sel = ((picked > 0.5) & valid) | (blk == own)
        bias_t = jnp.concatenate([jnp.zeros((HEAD_DIM, MOBA_Q), F32), jnp.where(sel, 0.0, NEG_BIG),
                                  jnp.zeros((LANES - HEAD_DIM - n_blk, MOBA_Q), F32)], axis=0)
        q_augs.append(jnp.where(lane < HEAD_DIM, q0 * SCALE_LOG2E, bias_t.T).astype(BF16))
    pos = qi * MOBA_Q + _iota((MOBA_Q, ATT_TK), 0)

    def tile(j, carry, masked):
        start = pl.multiple_of(j * ATT_TK, ATT_TK)
        mask = None
        if masked:
            mask = (j * ATT_TK + _iota((MOBA_Q, ATT_TK), 1)) <= pos
        return tuple(_flash_step(q_augs[h], k_ref[0, pl.ds(start, ATT_TK), h * LANES:(h + 1) * LANES],
                                 v_ref[0, pl.ds(start, ATT_TK), h * LANES:(h + 1) * LANES], mask, *carry[h])
                     for h in range(2))

    init = (jnp.full((MOBA_Q, 1), NEG_BIG, F32), jnp.zeros((MOBA_Q, LANES), F32))
    jd = (own * MOBA_BLOCK) // ATT_TK
    carry = lax.fori_loop(0, jd, lambda j, c: tile(j, c, False), tile(jd, (init, init), True))
    o_ref[0] = jnp.where(lane < HEAD_DIM, _flash_finish(carry[0][1]),
                         pltpu.roll(_flash_finish(carry[1][1]), HEAD_DIM, 1)).astype(o_ref.dtype)


def _moba_attention(q_m, kmp, k_aug, v_aug):
    b, s, _ = q_m.shape
    return pl.pallas_call(
        _moba_kernel,
        grid=(b, H_MOBA // 2, s // MOBA_Q),
        in_specs=[pl.BlockSpec((1, MOBA_Q, LANES), lambda b, h, i: (b, i, h)),
                  pl.BlockSpec((1, 2, LANES, LANES), lambda b, h, i: (b, h, 0, 0)),
                  pl.BlockSpec((1, s, 2 * LANES), lambda b, h, i: (b, 0, h)),
                  pl.BlockSpec((1, s, 2 * LANES), lambda b, h, i: (b, 0, h))],
        out_specs=pl.BlockSpec((1, MOBA_Q, LANES), lambda b, h, i: (b, i, h)),
        out_shape=jax.ShapeDtypeStruct((b, s, W_QM), BF16),
        compiler_params=_params("parallel", "parallel", "arbitrary"),
        name="moba_attention",
    )(q_m, kmp, k_aug, v_aug)


def _stack_heads(q4):
    lane = _iota((Q_TILE, LANES), 1)
    parts = []
    for r in range(R_NSA):
        c = q4[:, (r // 2) * LANES:(r // 2 + 1) * LANES]
        if r % 2:
            c = pltpu.roll(c, HEAD_DIM, 1)
        parts.append(jnp.where(lane < HEAD_DIM, c, 0.0))
    return jnp.concatenate(parts, axis=0)


def _nsa_kernel(qn_ref, qr_ref, gate_ref, kc_ref, vc_ref, cover_ref, ks_ref, vs_ref, kw_ref, vw_ref, o_ref,
                *, n_chunk):
    qi = pl.program_id(2)
    rows = R_NSA * Q_TILE
    qloc = _iota((rows, 1), 0) % Q_TILE
    pos = qi * Q_TILE + qloc

    qn = _stack_heads(qn_ref[0])
    s = _dot_nt_x3(qn, kc_ref[0, 0]) * SCALE
    cmask = (_iota((rows, n_chunk), 1) * CMP_STRIDE + (CMP_LEN - 1)) <= pos
    s = jnp.where(cmask, s, NEG_BIG)
    e = jnp.where(cmask, jnp.exp(s - jnp.max(s, axis=1, keepdims=True)), 0.0)
    p_cmp = e / jnp.maximum(jnp.sum(e, axis=1, keepdims=True), TINY)
    o_cmp = _dot(p_cmp.astype(BF16), vc_ref[0, 0])

    p_grp = p_cmp[0:Q_TILE]
    for r in range(1, R_NSA):
        p_grp = p_grp + p_cmp[r * Q_TILE:(r + 1) * Q_TILE]
    p_hi, p_lo = _split_bf16(p_grp)
    imp_t = _dot_nt(cover_ref[...], p_hi) + _dot_nt(cover_ref[...], p_lo)
    blk_t = _iota((LANES, Q_TILE), 0)
    own_t = (qi * Q_TILE + _iota((LANES, Q_TILE), 1)) // SEL_BLOCK
    forced = (blk_t == 0) | (blk_t == own_t) | (blk_t == own_t - 1)
    score = jnp.where(blk_t <= own_t, jnp.where(forced, jnp.inf, imp_t), -jnp.inf)
    picked = _top_k_sublanes(score, SEL_TOPN)
    bias = jnp.where((picked > 0.5) & (blk_t <= own_t), 0.0, NEG_BIG).T
    blk = _iota((Q_TILE, LANES), 1)

    qr = (_stack_heads(qr_ref[0]) * SCALE_LOG2E).astype(BF16)
    q_aug = jnp.concatenate([jnp.concatenate([bias] * R_NSA, axis=0).astype(BF16), qr], axis=1)
    init = (jnp.full((rows, 1), NEG_BIG, F32), jnp.zeros((rows, LANES), F32))

    tk = ATT_TK
    jd = (qi * Q_TILE) // tk

    def sel_tile(j, carry, masked):
        start = pl.multiple_of(j * tk, tk)
        mask = None
        if masked:
            mask = (j * tk + _iota((rows, tk), 1)) <= pos
        return _flash_step(q_aug, ks_ref[0, pl.ds(start, tk), :], vs_ref[0, pl.ds(start, tk), :], mask, *carry)

    o_sel = _flash_finish(lax.fori_loop(0, jd, lambda j, c: sel_tile(j, c, False), sel_tile(jd, init, True))[1])

    span = WINDOW + Q_TILE
    w0 = pl.multiple_of(jnp.maximum(qi * Q_TILE - WINDOW, 0), Q_TILE)
    kpos = w0 + _iota((rows, span), 1)
    wmask = (kpos <= pos) & (kpos > pos - WINDOW)
    o_win = _flash_finish(_flash_step(qr, kw_ref[0, pl.ds(w0, span), :], vw_ref[0, pl.ds(w0, span), :], wmask,
                                      *init)[1])

    gates = gate_ref[0]
    heads = []
    for r in range(R_NSA):
        rs = slice(r * Q_TILE, (r + 1) * Q_TILE)
        heads.append(gates[:, 3 * r:3 * r + 1] * o_cmp[rs] + gates[:, 3 * r + 1:3 * r + 2] * o_sel[rs]
                     + gates[:, 3 * r + 2:3 * r + 3] * o_win[rs])
    lo = blk < HEAD_DIM
    o_ref[0] = jnp.concatenate([jnp.where(lo, heads[0], pltpu.roll(heads[1], HEAD_DIM, 1)),
                                jnp.where(lo, heads[2], pltpu.roll(heads[3], HEAD_DIM, 1))], axis=1).astype(o_ref.dtype)


def _nsa_attention(qn, qr, gates_g, kcp, vcd, cover, ks_aug, vs_dup, kw_pad, vw_dup):
    b, s, _ = qn.shape
    n_chunk = kcp.shape[2]
    gq = R_NSA * HEAD_DIM
    per_g = lambda w: pl.BlockSpec((1, s, w), lambda b, g, i: (b, 0, g))
    return pl.pallas_call(
        functools.partial(_nsa_kernel, n_chunk=n_chunk),
        grid=(b, G_NSA, s // Q_TILE),
        in_specs=[pl.BlockSpec((1, Q_TILE, gq), lambda b, g, i: (b, i, g)),
                  pl.BlockSpec((1, Q_TILE, gq), lambda b, g, i: (b, i, g)),
                  pl.BlockSpec((1, Q_TILE, LANES), lambda b, g, i: (b, i, g)),
                  pl.BlockSpec((1, 1, n_chunk, LANES), lambda b, g, i: (b, g, 0, 0)),
                  pl.BlockSpec((1, 1, n_chunk, LANES), lambda b, g, i: (b, g, 0, 0)),
                  pl.BlockSpec(cover.shape, lambda b, g, i: (0, 0)),
                  per_g(2 * LANES), per_g(LANES), per_g(LANES), per_g(LANES)],
        out_specs=pl.BlockSpec((1, Q_TILE, gq), lambda b, g, i: (b, i, g)),
        out_shape=jax.ShapeDtypeStruct((b, s, W_QN), BF16),
        compiler_params=_params("parallel", "parallel", "arbitrary"),
        name="nsa_attention",
    )(qn, qr, gates_g, kcp, vcd, cover, ks_aug, vs_dup, kw_pad, vw_dup)


def _cover(n_cmp, n_sel, rows):
    c0 = np.arange(rows)[:, None] * CMP_STRIDE
    b0 = np.arange(LANES)[None, :] * SEL_BLOCK
    ok = (c0 < b0 + SEL_BLOCK) & (c0 + CMP_LEN > b0) & (np.arange(rows)[:, None] < n_cmp) & (np.arange(LANES)[None, :] < n_sel)
    return jnp.asarray(ok.astype(np.float32))


def _dec_kernel(pt_ref, ma_ref, mb_ref, na_ref, nb_ref, wb_ref, qbd_ref, q8n_ref, q8r_ref, g8_ref, newm_ref,
                news_ref, neww_ref, kc_ref, vc_ref, cover_ref, om_ref, o8_ref,
                m_sc, l_sc, s_sc, acc_sc, msel_sc, lsel_sc, asel_sc, bias_sc, ocmp_sc, *, n_steps, page):
    j = pl.program_id(1)
    sub8 = _iota((SUBLANES, LANES), 0)
    lane8 = _iota((SUBLANES, LANES), 1)
    tk = 2 * page
    q8r = q8r_ref[0] * SCALE
    q8r_bf = q8r.astype(BF16)

    @pl.when(j == 0)
    def _():
        n_chunk = kc_ref.shape[1]
        s = _dot_nt(q8n_ref[0], kc_ref[0], HIGHEST) * SCALE
        cmask = _iota((SUBLANES, n_chunk), 1) < (n_chunk - 1)
        s = jnp.where(cmask, s, NEG_BIG)
        e = jnp.where(cmask, jnp.exp(s - jnp.max(s, axis=1, keepdims=True)), 0.0)
        p_cmp = e / jnp.maximum(jnp.sum(e, axis=1, keepdims=True), TINY)
        ocmp_sc[...] = _dot(p_cmp.astype(BF16), vc_ref[0].astype(BF16))
        subc = _iota((SUBLANES, n_chunk), 0)
        g0 = jnp.sum(jnp.where(subc < R_NSA, p_cmp, 0.0), axis=0, keepdims=True)
        g1 = jnp.sum(jnp.where(subc >= R_NSA, p_cmp, 0.0), axis=0, keepdims=True)
        imp = _dot(jnp.where(subc < R_NSA, g0, g1), cover_ref[...], HIGHEST)
        own = (n_steps * tk) // SEL_BLOCK
        forced = (lane8 == 0) | (lane8 == own) | (lane8 == own - 1)
        score = jnp.where(lane8 <= own, jnp.where(forced, jnp.inf, imp), -jnp.inf)
        picked, _, _ = _top_k_lanes(score, SEL_TOPN)
        bias_sc[...] = jnp.where((picked > 0.5) & (lane8 <= own), 0.0, NEG_BIG)
        msel_sc[...] = jnp.full((SUBLANES, LANES), NEG_BIG, F32)
        lsel_sc[...] = jnp.zeros((SUBLANES, LANES), F32)
        asel_sc[...] = jnp.zeros((SUBLANES, LANES), F32)

    hw = H_MOBA * HEAD_DIM
    k_m = jnp.concatenate([ma_ref[0, :, 0:hw], mb_ref[0, :, 0:hw]], axis=0)
    v_m = jnp.concatenate([ma_ref[0, :, hw:2 * hw], mb_ref[0, :, hw:2 * hw]], axis=0)
    qbd = qbd_ref[0]
    s = _dot_nt((qbd * SCALE).astype(BF16), k_m.astype(BF16))
    mj = jnp.max(s, axis=1, keepdims=True)
    p = jnp.exp(s - mj)
    m_sc[j] = jnp.broadcast_to(mj, (SUBLANES, LANES))
    l_sc[j] = jnp.broadcast_to(jnp.sum(p, axis=1, keepdims=True), (SUBLANES, LANES))
    acc_sc[j] = _dot(p.astype(BF16), v_m.astype(BF16))
    kmean = jnp.sum(k_m, axis=0, keepdims=True) * (1.0 / MOBA_BLOCK)
    s_sc[j] = jnp.broadcast_to(jnp.sum(qbd * kmean, axis=1, keepdims=True), (SUBLANES, LANES))

    gw = G_NSA * HEAD_DIM
    k_s = jnp.concatenate([na_ref[0, :, 0:gw], nb_ref[0, :, 0:gw]], axis=0).astype(BF16)
    v_s = jnp.concatenate([na_ref[0, :, gw:2 * gw], nb_ref[0, :, gw:2 * gw]], axis=0).astype(BF16)
    s = _dot_nt(q8r_bf, k_s)
    bias = bias_sc[...]
    kb = _iota((SUBLANES, tk), 1) // SEL_BLOCK
    bias_k = jnp.zeros((SUBLANES, tk), F32)
    for q in range(tk // SEL_BLOCK):
        bq = jnp.sum(jnp.where(lane8 == j * (tk // SEL_BLOCK) + q, bias, 0.0), axis=1, keepdims=True)
        bias_k = jnp.where(kb == q, bq, bias_k)
    s = s + bias_k
    m_old = msel_sc[:, 0:1]
    m_new = jnp.maximum(m_old, jnp.max(s, axis=1, keepdims=True))
    alpha = jnp.exp(m_old - m_new)
    p = jnp.exp(s - m_new)
    msel_sc[...] = jnp.broadcast_to(m_new, (SUBLANES, LANES))
    lsel_sc[...] = alpha * lsel_sc[...] + jnp.sum(p, axis=1, keepdims=True)
    asel_sc[...] = alpha * asel_sc[...] + _dot(p.astype(BF16), v_s)

    @pl.when(j == n_steps - 1)
    def _():
        news = news_ref[0]
        s_new = jnp.sum(q8r * news[0:1], axis=1, keepdims=True)
        m_old = msel_sc[:, 0:1]
        m_new = jnp.maximum(m_old, s_new)
        alpha = jnp.exp(m_old - m_new)
        pn = jnp.exp(s_new - m_new)
        l = alpha * lsel_sc[...] + pn
        o_sel = (alpha * asel_sc[...] + pn * news[1:2]) / jnp.maximum(l, TINY)

        neww = neww_ref[0]
        kw = wb_ref[0, :, 0:gw].astype(BF16)
        vw = wb_ref[0, :, gw:2 * gw].astype(BF16)
        nw = kw.shape[0]
        s = _dot_nt(q8r_bf, kw)
        wmask = _iota((SUBLANES, nw), 1) >= 1
        s = jnp.where(wmask, s, NEG_BIG)
        s_new = jnp.sum(q8r * neww[0:1], axis=1, keepdims=True)
        mw = jnp.maximum(jnp.max(s, axis=1, keepdims=True), s_new)
        e = jnp.where(wmask, jnp.exp(s - mw), 0.0)
        en = jnp.exp(s_new - mw)
        lw = jnp.sum(e, axis=1, keepdims=True) + en
        o_win = (_dot(e.astype(BF16), vw) + en * neww[1:2]) / jnp.maximum(lw, TINY)

        g8 = g8_ref[0]
        o8_ref[0] = g8[:, 0:1] * ocmp_sc[...] + g8[:, 1:2] * o_sel + g8[:, 2:3] * o_win

        newm = newm_ref[0]
        scores = [s_sc[b] for b in range(n_steps)]
        m_tot = jnp.broadcast_to(jnp.sum(qbd * newm[0:1], axis=1, keepdims=True) * SCALE, (SUBLANES, LANES))
        s_own = m_tot
        sels = []
        for b in range(n_steps):
            rank = jnp.zeros((SUBLANES, LANES), F32)
            for c in range(n_steps):
                if c == b:
                    continue
                ahead = (scores[c] > scores[b]) | ((scores[c] == scores[b]) & (c < b))
                rank = rank + jnp.where(ahead, 1.0, 0.0)
            sels.append(rank < MOBA_TOPK)
            m_tot = jnp.where(sels[b], jnp.maximum(m_tot, m_sc[b]), m_tot)
        w_own = jnp.exp(s_own - m_tot)
        l_tot = w_own
        acc = w_own[:, 0:1] * newm[1:2]
        for b in range(n_steps):
            wgt = jnp.where(sels[b], jnp.exp(m_sc[b] - m_tot), 0.0)
            l_tot = l_tot + wgt * l_sc[b]
            acc = acc + wgt[:, 0:1] * acc_sc[b]
        o_full = acc / jnp.maximum(l_tot[:, 0:1], TINY)
        own_head = (_iota((SUBLANES, hw), 1) // HEAD_DIM) == _iota((SUBLANES, hw), 0)
        om_ref[0] = jnp.sum(jnp.where(own_head, o_full, 0.0), axis=0, keepdims=True)


def _dec_attention(cache_moba, cache_nsa, win_buf, pt_flat, qbd, q8n, q8r, g8, newm, news, neww, kc, vc, cover,
                   *, n_req, n_pages):
    _, page, mw = cache_moba.shape
    n_steps = n_pages // 2
    gw = G_NSA * HEAD_DIM
    hw = H_MOBA * HEAD_DIM
    n_chunk = kc.shape[1]
    nwin = win_buf.shape[1]
    req = lambda a: pl.BlockSpec((1,) + a.shape[1:], lambda b, j, pt: (b,) + (0,) * (a.ndim - 1))
    return pl.pallas_call(
        functools.partial(_dec_kernel, n_steps=n_steps, page=page),
        grid_spec=pltpu.PrefetchScalarGridSpec(
            num_scalar_prefetch=1,
            grid=(n_req, n_steps),
            in_specs=[pl.BlockSpec((1, page, mw), lambda b, j, pt: (pt[b * n_pages + 2 * j], 0, 0)),
                      pl.BlockSpec((1, page, mw), lambda b, j, pt: (pt[b * n_pages + 2 * j + 1], 0, 0)),
                      pl.BlockSpec((1, page, 2 * gw), lambda b, j, pt: (pt[b * n_pages + 2 * j], 0, 1)),
                      pl.BlockSpec((1, page, 2 * gw), lambda b, j, pt: (pt[b * n_pages + 2 * j + 1], 0, 1)),
                      req(win_buf), req(qbd), req(q8n), req(q8r), req(g8), req(newm), req(news), req(neww),
                      req(kc), req(vc), pl.BlockSpec(cover.shape, lambda b, j, pt: (0, 0))],
            out_specs=[pl.BlockSpec((1, 1, hw), lambda b, j, pt: (b, 0, 0)),
                       pl.BlockSpec((1, SUBLANES, LANES), lambda b, j, pt: (b, 0, 0))],
            scratch_shapes=[pltpu.VMEM((n_steps, SUBLANES, LANES), F32)] * 3
            + [pltpu.VMEM((n_steps, SUBLANES, hw), F32)]
            + [pltpu.VMEM((SUBLANES, LANES), F32)] * 5),
        out_shape=[jax.ShapeDtypeStruct((n_req, 1, hw), F32), jax.ShapeDtypeStruct((n_req, SUBLANES, LANES), F32)],
        compiler_params=_params("parallel", "arbitrary"),
        name="dec_attention",
    )(pt_flat, cache_moba, cache_moba, cache_nsa, cache_nsa, win_buf, qbd, q8n, q8r, g8, newm, news, neww, kc, vc,
      cover)


def _lane_rep(col):
    return jnp.broadcast_to(col, (col.shape[0], LANES))


def _head_sums(prod_row):
    w = prod_row.shape[1]
    own = (_iota((SUBLANES, w), 1) // HEAD_DIM) == _iota((SUBLANES, w), 0)
    return _lane_rep(jnp.sum(jnp.where(own, jnp.broadcast_to(prod_row, (SUBLANES, w)), 0.0), axis=1, keepdims=True))


def _pair_row(x8, h):
    return jnp.where(_iota((1, LANES), 1) < HEAD_DIM, x8[h:h + 1, :], x8[h + 1:h + 2, :])


def _cols_to_row(acc_a, acc_b):
    return jnp.sum(jnp.concatenate([acc_a, acc_b], axis=0).T, axis=0, keepdims=True)


def _dec2_kernel(pt_ref, cm_ref, cn_ref, wb_ref, q_ref, q8n_ref, new_ref, gate_ref, wcat_ref, pecat_ref, w2_ref,
                 gk_ref, bd_ref, cover_ref, o_ref,
                 mbuf, nbuf, sem, xk, xv, qmb, qrb, s_sc, p_sc, *, n_pages, page):
    b = pl.program_id(0)
    n_req = pl.num_programs(0)
    slot = b % 2
    hw = H_MOBA * HEAD_DIM
    gw = G_NSA * HEAD_DIM
    past = n_pages * page

    def copies(req, sl):
        out = []
        for p in range(n_pages):
            pg = pt_ref[req * n_pages + p]
            out.append(pltpu.make_async_copy(cm_ref.at[pg], mbuf.at[sl, p], sem.at[0, sl]))
            out.append(pltpu.make_async_copy(cn_ref.at[pg], nbuf.at[sl, p], sem.at[1, sl]))
        return out

    @pl.when(b == 0)
    def _():
        for c in copies(0, 0):
            c.start()

    for c in copies(b, slot):
        c.wait()

    @pl.when(b + 1 < n_req)
    def _():
        for c in copies(b + 1, 1 - slot):
            c.start()

    lane1 = _iota((1, LANES), 1)
    lane8 = _iota((SUBLANES, LANES), 1)
    qrow = q_ref[0]
    new = new_ref[0]
    for c in range(hw // LANES):
        cols = slice(c * LANES, (c + 1) * LANES)
        qmb[cols, :] = jnp.broadcast_to(qrow[0:1, cols] * SCALE, (LANES, LANES)).T
        qrb[cols, :] = jnp.broadcast_to(qrow[1:2, cols] * SCALE, (LANES, LANES)).T

    def softmax_pv(scores, extra8, s_new8, v_rows, vbuf_ref, v_row0, per_g):
        del extra8
        m8 = s_new8
        for s in scores:
            m8 = jnp.maximum(m8, _lane_rep(jnp.max(s, axis=1, keepdims=True)))
        w_new = jnp.exp(s_new8 - m8)
        l8 = w_new
        for p, s in enumerate(scores):
            pr = jnp.exp(s - m8)
            p_sc[p] = pr
            l8 = l8 + _lane_rep(jnp.sum(pr, axis=1, keepdims=True))
        inv8 = 1.0 / jnp.maximum(l8, TINY)
        rows = []
        for hp in range(SUBLANES // 2):
            accs = []
            for h in (2 * hp, 2 * hp + 1):
                r0 = v_row0 + (h // R_NSA if per_g else h) * HEAD_DIM

                def body(p, acc, h=h, r0=r0):
                    return acc + vbuf_ref[slot, p, r0:r0 + HEAD_DIM, :] * p_sc[p, h:h + 1, :]
                accs.append(lax.fori_loop(0, n_pages, body, jnp.zeros((HEAD_DIM, LANES), F32)))
            row = _cols_to_row(accs[0], accs[1])
            cols = slice(hp * LANES, (hp + 1) * LANES)
            rows.append((row + _pair_row(w_new, 2 * hp) * v_rows[:, cols]) * _pair_row(inv8, 2 * hp))
        return rows

    def moba_scores(p, carry):
        rows = [jnp.sum(mbuf[slot, p, h * HEAD_DIM:(h + 1) * HEAD_DIM, :] * qmb[h * HEAD_DIM:(h + 1) * HEAD_DIM, :],
                        axis=0, keepdims=True) for h in range(H_MOBA)]
        s_sc[p] = jnp.concatenate(rows, axis=0)
        return carry
    lax.fori_loop(0, n_pages, moba_scores, 0)
    s_all = [s_sc[p] for p in range(n_pages)]
    ppb = MOBA_BLOCK // page
    n_blk = n_pages // ppb
    bsc = []
    for j in range(n_blk):
        tot = s_all[j * ppb]
        for t in range(1, ppb):
            tot = tot + s_all[j * ppb + t]
        bsc.append(_lane_rep(jnp.sum(tot, axis=1, keepdims=True)))
    masked = []
    for j in range(n_blk):
        rank = jnp.zeros((SUBLANES, LANES), F32)
        for c in range(n_blk):
            if c != j:
                ahead = (bsc[c] > bsc[j]) | ((bsc[c] == bsc[j]) & (c < j))
                rank = rank + jnp.where(ahead, 1.0, 0.0)
        for t in range(ppb):
            masked.append(jnp.where(rank < MOBA_TOPK, s_all[j * ppb + t], NEG_BIG))
    s_own = _head_sums(qrow[0:1, :] * new[0:1, :]) * SCALE
    o_rows = softmax_pv(masked, None, s_own, new[1:2, :], mbuf, hw, False)

    for p in range(n_pages):
        xk[p * page:(p + 1) * page, :] = nbuf[slot, p, 0:gw, :].T
        xv[p * page:(p + 1) * page, :] = nbuf[slot, p, gw:2 * gw, :].T
    n_chunk = past // CMP_STRIDE
    cmp_out = []
    for kv, xref in enumerate((xk, xv)):
        xcat = jnp.concatenate([xref[pl.ds(j, n_chunk, stride=CMP_STRIDE), :] for j in range(CMP_STRIDE)], axis=1)
        pe2 = _dot(pecat_ref[kv].astype(BF16), wcat_ref[kv])
        ab = _dot(xcat.astype(BF16), wcat_ref[kv])
        hid = jax.nn.gelu(ab[:, 0:gw] + pe2[0:1, 0:gw] + pltpu.roll(ab[:, gw:2 * gw] + pe2[1:2, gw:2 * gw],
                                                                    n_chunk - 1, 0))
        cmp_out.append(_dot(hid, w2_ref[kv], HIGHEST))
    kc = _head_norm(cmp_out[0], gk_ref[...], bd_ref[...])
    vc = cmp_out[1]

    s = _dot_nt(q8n_ref[0], kc, HIGHEST) * SCALE
    cmask = _iota((SUBLANES, n_chunk), 1) < (n_chunk - 1)
    s = jnp.where(cmask, s, NEG_BIG)
    e = jnp.where(cmask, jnp.exp(s - jnp.max(s, axis=1, keepdims=True)), 0.0)
    p_cmp = e / jnp.maximum(jnp.sum(e, axis=1, keepdims=True), TINY)
    o_cmp8 = _dot(p_cmp.astype(BF16), vc.astype(BF16))
    subc = _iota((SUBLANES, n_chunk), 0)
    g0 = jnp.sum(jnp.where(subc < R_NSA, p_cmp, 0.0), axis=0, keepdims=True)
    g1 = jnp.sum(jnp.where(subc >= R_NSA, p_cmp, 0.0), axis=0, keepdims=True)
    p_grp = jnp.concatenate([jnp.where(subc < R_NSA, g0, g1), jnp.zeros((LANES - SUBLANES, n_chunk), F32)], axis=0)
    imp_t = _dot_nt(cover_ref[...], p_grp, HIGHEST)
    blk_t = _iota((LANES, LANES), 0)
    own = past // SEL_BLOCK
    forced = (blk_t == 0) | (blk_t == own) | (blk_t == own - 1)
    score = jnp.where(blk_t <= own, jnp.where(forced, jnp.inf, imp_t), -jnp.inf)
    picked = _top_k_sublanes(score, SEL_TOPN)
    bias8 = jnp.where((picked > 0.5) & (blk_t <= own), 0.0, NEG_BIG).T[0:SUBLANES, :]
    cmp_rows = []
    for hp in range(H_NSA // 2):
        g = (2 * hp) // R_NSA
        ra = o_cmp8[2 * hp:2 * hp + 1, :]
        rb = o_cmp8[2 * hp + 1:2 * hp + 2, :]
        cmp_rows.append(jnp.where(lane1 < HEAD_DIM, ra if g == 0 else pltpu.roll(ra, HEAD_DIM, 1),
                                  rb if g == 1 else pltpu.roll(rb, HEAD_DIM, 1)))

    def sel_scores(p, carry):
        rows = [jnp.sum(nbuf[slot, p, 2 * gw + (i // R_NSA) * HEAD_DIM:2 * gw + (i // R_NSA + 1) * HEAD_DIM, :]
                        * qrb[i * HEAD_DIM:(i + 1) * HEAD_DIM, :], axis=0, keepdims=True) for i in range(H_NSA)]
        s_sc[p] = jnp.concatenate(rows, axis=0)
        return carry
    lax.fori_loop(0, n_pages, sel_scores, 0)
    bpp = page // SEL_BLOCK
    sel_s = []
    for p in range(n_pages):
        bias_p = bias8[:, p * bpp:p * bpp + 1]
        for t in range(1, bpp):
            bias_p = jnp.where(lane8 < t * SEL_BLOCK, bias_p, bias8[:, p * bpp + t:p * bpp + t + 1])
        sel_s.append(s_sc[p] + bias_p)
    s_new = _head_sums(qrow[1:2, :] * new[2:3, :]) * SCALE
    sel_rows = softmax_pv(sel_s, None, s_new, new[3:4, :], nbuf, 3 * gw, True)

    nw = wb_ref.shape[2]
    wk = nw // LANES
    w_s = []
    for i in range(H_NSA):
        g = i // R_NSA
        qcol = jnp.concatenate([qrb[i * HEAD_DIM:(i + 1) * HEAD_DIM, :]] * wk, axis=1)
        w_s.append(jnp.sum(wb_ref[0, g * HEAD_DIM:(g + 1) * HEAD_DIM, :] * qcol, axis=0, keepdims=True))
    s = jnp.concatenate(w_s, axis=0)
    wmask = _iota((SUBLANES, nw), 1) >= 1
    s = jnp.where(wmask, s, NEG_BIG)
    s_new = _head_sums(qrow[1:2, :] * new[4:5, :]) * SCALE
    m8 = jnp.maximum(_lane_rep(jnp.max(s, axis=1, keepdims=True)), s_new)
    pw = jnp.where(wmask, jnp.exp(s - m8[:, 0:1]), 0.0)
    w_new = jnp.exp(s_new - m8)
    inv8 = 1.0 / jnp.maximum(_lane_rep(jnp.sum(pw, axis=1, keepdims=True)) + w_new, TINY)
    win_rows = []
    for hp in range(H_NSA // 2):
        accs = []
        for i in (2 * hp, 2 * hp + 1):
            g = i // R_NSA
            prod = wb_ref[0, gw + g * HEAD_DIM:gw + (g + 1) * HEAD_DIM, :] * pw[i:i + 1, :]
            acc = prod[:, 0:LANES]
            for c in range(1, wk):
                acc = acc + prod[:, c * LANES:(c + 1) * LANES]
            accs.append(acc)
        cols = slice(hp * LANES, (hp + 1) * LANES)
        win_rows.append((_cols_to_row(accs[0], accs[1]) + _pair_row(w_new, 2 * hp) * new[5:6, cols])
                        * _pair_row(inv8, 2 * hp))

    gates = gate_ref[0]
    for c in range(hw // LANES):
        o_ref[0, :, c * LANES:(c + 1) * LANES] = o_rows[c]
    for c in range(W_QN // LANES):
        cols = slice(c * LANES, (c + 1) * LANES)
        o_ref[0, :, hw + c * LANES:hw + (c + 1) * LANES] = (
            gates[0:1, cols] * cmp_rows[c] + gates[1:2, cols] * sel_rows[c] + gates[2:3, cols] * win_rows[c])


def _dec2_attention(cm, cn, wb, pt_flat, qrows, q8n, new, gate_rows, wcat, pecat, w2bd, gk, bd, cover,
                    *, n_req, n_pages):
    page = cm.shape[2]
    n_chunk = n_pages * page // CMP_STRIDE
    req = lambda a: pl.BlockSpec((1,) + a.shape[1:], lambda b, pt: (b,) + (0,) * (a.ndim - 1))
    const = lambda a: pl.BlockSpec(a.shape, lambda b, pt: (0,) * a.ndim)
    return pl.pallas_call(
        functools.partial(_dec2_kernel, n_pages=n_pages, page=page),
        grid_spec=pltpu.PrefetchScalarGridSpec(
            num_scalar_prefetch=1,
            grid=(n_req,),
            in_specs=[pl.BlockSpec(memory_space=pl.ANY), pl.BlockSpec(memory_space=pl.ANY), req(wb), req(qrows),
                      req(q8n), req(new), req(gate_rows), const(wcat), const(pecat), const(w2bd), const(gk),
                      const(bd), const(cover)],
            out_specs=pl.BlockSpec((1, 1, W_QM + W_QN), lambda b, pt: (b, 0, 0)),
            scratch_shapes=[pltpu.VMEM((2, n_pages) + cm.shape[1:], F32), pltpu.VMEM((2, n_pages) + cn.shape[1:], F32),
                            pltpu.SemaphoreType.DMA((2, 2)),
                            pltpu.VMEM((n_pages * page, LANES), F32), pltpu.VMEM((n_pages * page, LANES), F32),
                            pltpu.VMEM((W_QM, LANES), F32), pltpu.VMEM((W_QN, LANES), F32),
                            pltpu.VMEM((n_pages, SUBLANES, LANES), F32), pltpu.VMEM((n_pages, SUBLANES, LANES), F32)]),
        out_shape=jax.ShapeDtypeStruct((n_req, 1, W_QM + W_QN), F32),
        compiler_params=_params("arbitrary"),
        name="dec_attention",
    )(pt_flat, cm, cn, wb, qrows, q8n, new, gate_rows, wcat, pecat, w2bd, gk, bd, cover)


def _compress_cat_consts(cmp_pos, cmp_w1):
    eye = jnp.eye(G_NSA, dtype=F32)
    w1 = cmp_w1.reshape(2, 2, CMP_STRIDE, HEAD_DIM, HEAD_DIM)
    wcat = jnp.einsum("gh,kajde->kjgdahe", eye, w1).reshape(2, CMP_STRIDE * G_NSA * HEAD_DIM, 2 * G_NSA * HEAD_DIM)
    pe = jnp.tile(cmp_pos.reshape(2, 2, CMP_STRIDE, 1, HEAD_DIM), (1, 1, 1, G_NSA, 1))
    pecat = jnp.pad(pe.reshape(2, 2, CMP_STRIDE * G_NSA * HEAD_DIM), ((0, 0), (0, SUBLANES - 2), (0, 0)))
    return wcat.astype(BF16), pecat


def _post_kernel(o_ref, x_ref, gt_ref, sh_ref, sc_ref, g_ref, wo_ref, wr_ref, br_ref,
                 y_ref, h3_ref, te_ref, tw_ref):
    y = x_ref[...] + gt_ref[0] * _dot(o_ref[...], wo_ref[...])
    y_ref[...] = y
    h = y * lax.rsqrt(jnp.mean(y * y, axis=1, keepdims=True) + NORM_EPS) * g_ref[...]
    h = h * (1.0 + sc_ref[0]) + sh_ref[0]
    tm = h.shape[0]
    for s in range(h.shape[1] // LANES):
        h3_ref[pl.ds(s, tm, stride=SUBLANES), :] = h[:, s * LANES:(s + 1) * LANES]
    logits = _dot(h, wr_ref[...], HIGHEST) + br_ref[...]
    _, vals, ids = _top_k_lanes(logits, TOP_K)
    lane = _iota((tm, LANES), 1)
    es = [jnp.exp(v - vals[0]) for v in vals]
    den = es[0]
    for e in es[1:]:
        den = den + e
    te = jnp.zeros((tm, LANES), F32)
    tw = jnp.zeros((tm, LANES), F32)
    for k in range(TOP_K):
        te = jnp.where(lane == k, ids[k], te)
        tw = jnp.where(lane == k, es[k] / den, tw)
    te_ref[...] = te.astype(jnp.int32)
    tw_ref[...] = tw


def _post(o, x, gate, shift, scale, g2, wo_bf, wr_pad, br_pad, *, tm, rows_per_mod):
    n, d = x.shape
    mod_r = gate.shape[1]
    mod_map = lambda i: (i // (rows_per_mod // tm), 0, 0)
    row = lambda w: pl.BlockSpec((tm, w), lambda i: (i, 0))
    const = lambda a: pl.BlockSpec(a.shape, lambda i: (0,) * a.ndim)
    mod = pl.BlockSpec((1, mod_r, d), mod_map)
    return pl.pallas_call(
        _post_kernel,
        grid=(n // tm,),
        in_specs=[row(d), row(d), mod, mod, mod, const(g2), const(wo_bf), const(wr_pad), const(br_pad)],
        out_specs=[row(d), pl.BlockSpec((tm * SUBLANES, LANES), lambda i: (i, 0)), row(LANES), row(LANES)],
        out_shape=[jax.ShapeDtypeStruct((n, d), F32), jax.ShapeDtypeStruct((n * SUBLANES, LANES), F32),
                   jax.ShapeDtypeStruct((n, LANES), jnp.int32), jax.ShapeDtypeStruct((n, LANES), F32)],
        compiler_params=_params("parallel"),
        name="post",
    )(o, x, gate, shift, scale, g2, wo_bf, wr_pad, br_pad)


def _expert_kernel(be_ref, na_ref, tok_ref, h3_ref, wgu_ref, bgu_ref, wd_ref, bd_ref, y_ref, buf, xb, sem):
    i = pl.program_id(0)
    n_active = na_ref[0]
    rows8 = MOE_ROWS * SUBLANES

    def gather(blk, slot):
        def body(r, carry):
            t = tok_ref[blk * MOE_ROWS + r]
            pltpu.make_async_copy(h3_ref.at[pl.ds(pl.multiple_of(t * SUBLANES, SUBLANES), SUBLANES), :],
                                  buf.at[slot, pl.ds(pl.multiple_of(r * SUBLANES, SUBLANES), SUBLANES), :],
                                  sem.at[slot]).start()
            return carry
        lax.fori_loop(0, MOE_ROWS, body, 0)

    @pl.when(i == 0)
    def _():
        gather(0, 0)

    @pl.when(i < n_active)
    def _():
        slot = i % 2
        pltpu.make_async_copy(h3_ref.at[pl.ds(0, rows8), :], buf.at[slot], sem.at[slot]).wait()

        @pl.when(i + 1 < n_active)
        def _():
            gather(i + 1, 1 - slot)

        d = xb.shape[1]
        for s in range(d // LANES):
            xb[:, s * LANES:(s + 1) * LANES] = buf[slot, pl.ds(s, MOE_ROWS, stride=SUBLANES), :].astype(BF16)
        gu = _dot(xb[...], wgu_ref[0]) + bgu_ref[0]
        f = gu.shape[1] // 2
        gt = jnp.minimum(gu[:, 0:f], SWIGLU_LIMIT)
        up = jnp.clip(gu[:, f:2 * f], -SWIGLU_LIMIT, SWIGLU_LIMIT)
        act = (up + 1.0) * (gt * jax.nn.sigmoid(SWIGLU_ALPHA * gt))
        y = _dot(act.astype(BF16), wd_ref[0]) + bd_ref[0]
        for s in range(d // LANES):
            y_ref[pl.ds(s, MOE_ROWS, stride=SUBLANES), :] = y[:, s * LANES:(s + 1) * LANES]

    @pl.when(i >= n_active)
    def _():
        y_ref[...] = jnp.zeros(y_ref.shape, F32)


def _experts(blk_e, n_active, tok, h3, wgu_bf, b_gu, wd_bf, b_down):
    n_blocks = blk_e.shape[0]
    e, d, f2 = wgu_bf.shape
    rows8 = MOE_ROWS * SUBLANES
    return pl.pallas_call(
        _expert_kernel,
        grid_spec=pltpu.PrefetchScalarGridSpec(
            num_scalar_prefetch=3,
            grid=(n_blocks,),
            in_specs=[pl.BlockSpec(memory_space=pl.ANY),
                      pl.BlockSpec((1, d, f2), lambda i, be, na, tok: (be[i], 0, 0)),
                      pl.BlockSpec((1, 1, f2), lambda i, be, na, tok: (be[i], 0, 0)),
                      pl.BlockSpec((1, f2 // 2, d), lambda i, be, na, tok: (be[i], 0, 0)),
                      pl.BlockSpec((1, 1, d), lambda i, be, na, tok: (be[i], 0, 0))],
            out_specs=pl.BlockSpec((rows8, LANES), lambda i, be, na, tok: (i, 0)),
            scratch_shapes=[pltpu.VMEM((2, rows8, LANES), F32), pltpu.VMEM((MOE_ROWS, d), BF16),
                            pltpu.SemaphoreType.DMA((2,))]),
        out_shape=jax.ShapeDtypeStruct((n_blocks * rows8, LANES), F32),
        compiler_params=_params("arbitrary"),
        name="experts",
    )(blk_e, n_active, tok, h3, wgu_bf, b_gu.reshape(e, 1, f2), wd_bf, b_down.reshape(e, 1, d))


def _dispatch_kernel(pos_ref, h3_ref, xs_in_ref, xs_ref, sem, *, tm, tile_off):
    del xs_in_ref
    i = pl.program_id(0)

    def body(r, carry):
        src = h3_ref.at[pl.ds(pl.multiple_of(r * SUBLANES, SUBLANES), SUBLANES), :]
        for k in range(TOP_K):
            p = pos_ref[((i + tile_off) * tm + r) * TOP_K + k]
            pltpu.make_async_copy(src, xs_ref.at[pl.ds(pl.multiple_of(p * SUBLANES, SUBLANES), SUBLANES), :],
                                  sem.at[0]).start()
        return carry
    lax.fori_loop(0, tm, body, 0)
    for k in range(TOP_K):
        pltpu.make_async_copy(h3_ref, xs_ref.at[pl.ds(0, tm * SUBLANES), :], sem.at[0]).wait()


def _dispatch(pos_flat, h3, xs, *, tm, tile_off):
    n8 = h3.shape[0]
    return pl.pallas_call(
        functools.partial(_dispatch_kernel, tm=tm, tile_off=tile_off),
        grid_spec=pltpu.PrefetchScalarGridSpec(
            num_scalar_prefetch=1,
            grid=(n8 // (tm * SUBLANES),),
            in_specs=[pl.BlockSpec((tm * SUBLANES, LANES), lambda i, pos: (i, 0)), pl.BlockSpec(memory_space=pl.ANY)],
            out_specs=pl.BlockSpec(memory_space=pl.ANY),
            scratch_shapes=[pltpu.SemaphoreType.DMA((1,))]),
        out_shape=jax.ShapeDtypeStruct(xs.shape, xs.dtype),
        input_output_aliases={2: 0},
        compiler_params=_params("arbitrary"),
        name="dispatch",
    )(pos_flat, h3, xs)


def _expert_block_kernel(be_ref, na_ref, x_ref, wgu_ref, bgu_ref, wd_ref, bd_ref, y_ref, xb):
    i = pl.program_id(0)

    @pl.when(i < na_ref[0])
    def _():
        d = xb.shape[1]
        for s in range(d // LANES):
            xb[:, s * LANES:(s + 1) * LANES] = x_ref[pl.ds(s, MOE_ROWS, stride=SUBLANES), :].astype(BF16)
        gu = _dot(xb[...], wgu_ref[0]) + bgu_ref[0]
        f = gu.shape[1] // 2
        gt = jnp.minimum(gu[:, 0:f], SWIGLU_LIMIT)
        up = jnp.clip(gu[:, f:2 * f], -SWIGLU_LIMIT, SWIGLU_LIMIT)
        act = (up + 1.0) * (gt * jax.nn.sigmoid(SWIGLU_ALPHA * gt))
        y = _dot(act.astype(BF16), wd_ref[0]) + bd_ref[0]
        for s in range(d // LANES):
            y_ref[pl.ds(s, MOE_ROWS, stride=SUBLANES), :] = y[:, s * LANES:(s + 1) * LANES]

    @pl.when(i >= na_ref[0])
    def _():
        y_ref[...] = jnp.zeros(y_ref.shape, F32)


def _expert_blocks(blk_e, n_active, xs, wgu_bf, b_gu, wd_bf, b_down):
    n_blocks = blk_e.shape[0]
    e, d, f2 = wgu_bf.shape
    rows8 = MOE_ROWS * SUBLANES
    row_map = lambda i, be, na: (jnp.minimum(i, na[0] - 1), 0)
    return pl.pallas_call(
        _expert_block_kernel,
        grid_spec=pltpu.PrefetchScalarGridSpec(
            num_scalar_prefetch=2,
            grid=(n_blocks,),
            in_specs=[pl.BlockSpec((rows8, LANES), row_map),
                      pl.BlockSpec((1, d, f2), lambda i, be, na: (be[i], 0, 0)),
                      pl.BlockSpec((1, 1, f2), lambda i, be, na: (be[i], 0, 0)),
                      pl.BlockSpec((1, f2 // 2, d), lambda i, be, na: (be[i], 0, 0)),
                      pl.BlockSpec((1, 1, d), lambda i, be, na: (be[i], 0, 0))],
            out_specs=pl.BlockSpec((rows8, LANES), lambda i, be, na: (i, 0)),
            scratch_shapes=[pltpu.VMEM((MOE_ROWS, d), BF16)]),
        out_shape=jax.ShapeDtypeStruct((n_blocks * rows8, LANES), F32),
        compiler_params=_params("arbitrary"),
        name="experts",
    )(blk_e, n_active, xs, wgu_bf, b_gu.reshape(e, 1, f2), wd_bf, b_down.reshape(e, 1, d))


def _routing_pos(top_e, n_tok):
    n_assign = n_tok * TOP_K
    n_pad = -(-n_assign // LANES) * LANES
    e_flat = jnp.pad(top_e.reshape(-1), (0, n_pad - n_assign), constant_values=N_EXPERTS)
    onehot = (e_flat[:, None] == jnp.arange(N_EXPERTS)[None, :]).astype(F32).reshape(n_pad // LANES, LANES, N_EXPERTS)
    tril = jnp.tril(jnp.ones((LANES, LANES), F32))
    within = jnp.einsum("ij,bjk->bik", tril, onehot)
    block_tot = within[:, -1, :]
    offs = jnp.cumsum(block_tot, axis=0) - block_tot
    counts = jnp.sum(block_tot, axis=0).astype(jnp.int32)
    padded = (counts + MOE_ROWS - 1) // MOE_ROWS * MOE_ROWS
    pad_end = jnp.cumsum(padded)
    pad_start = (pad_end - padded).astype(F32)
    slot = jnp.sum(onehot * (within + offs[:, None, :] - 1.0 + pad_start[None, None, :]), axis=-1)
    pos = slot.reshape(-1)[:n_assign].astype(jnp.int32)
    n_blocks = -(-n_assign // MOE_ROWS) + N_EXPERTS
    starts = jnp.arange(n_blocks, dtype=jnp.int32) * MOE_ROWS
    blk_e = jnp.minimum(jnp.sum((pad_end[None, :] <= starts[:, None]).astype(jnp.int32), axis=1), N_EXPERTS - 1)
    n_active = (pad_end[-1] // MOE_ROWS).astype(jnp.int32).reshape(1)
    return blk_e, n_active, pos


def _combine_kernel(pos_ref, ys_ref, y1_ref, gt_ref, tw_ref, o_ref, buf, sem, *, tile_off):
    i = pl.program_id(0)
    n = pl.num_programs(0)
    tm = y1_ref.shape[0]
    rows8 = tm * SUBLANES

    def gather(tile, slot):
        def body(r, carry):
            for k in range(TOP_K):
                p = pos_ref[((tile + tile_off) * tm + r) * TOP_K + k]
                pltpu.make_async_copy(ys_ref.at[pl.ds(pl.multiple_of(p * SUBLANES, SUBLANES), SUBLANES), :],
                                      buf.at[slot, k, pl.ds(pl.multiple_of(r * SUBLANES, SUBLANES), SUBLANES), :],
                                      sem.at[slot]).start()
            return carry
        lax.fori_loop(0, tm, body, 0)

    @pl.when(i == 0)
    def _():
        gather(0, 0)

    slot = i % 2
    for k in range(TOP_K):
        pltpu.make_async_copy(ys_ref.at[pl.ds(0, rows8), :], buf.at[slot, k], sem.at[slot]).wait()

    @pl.when(i + 1 < n)
    def _():
        gather(i + 1, 1 - slot)

    tw = tw_ref[...]
    wk = [jnp.broadcast_to(tw[:, k:k + 1], (tm, LANES)) for k in range(TOP_K)]
    gt = gt_ref[0]
    for s in range(o_ref.shape[1] // LANES):
        moe = wk[0] * buf[slot, 0, pl.ds(s, tm, stride=SUBLANES), :]
        for k in range(1, TOP_K):
            moe = moe + wk[k] * buf[slot, k, pl.ds(s, tm, stride=SUBLANES), :]
        cols = slice(s * LANES, (s + 1) * LANES)
        o_ref[:, cols] = y1_ref[:, cols] + gt[:, cols] * moe


def _combine(pos_flat, ys, y1, gate, tw, *, tm, rows_per_mod, tile_off):
    n, d = y1.shape
    mod_r = gate.shape[1]
    return pl.pallas_call(
        functools.partial(_combine_kernel, tile_off=tile_off),
        grid_spec=pltpu.PrefetchScalarGridSpec(
            num_scalar_prefetch=1,
            grid=(n // tm,),
            in_specs=[pl.BlockSpec(memory_space=pl.ANY),
                      pl.BlockSpec((tm, d), lambda i, pos: (i, 0)),
                      pl.BlockSpec((1, mod_r, d), lambda i, pos: (i // (rows_per_mod // tm), 0, 0)),
                      pl.BlockSpec((tm, LANES), lambda i, pos: (i, 0))],
            out_specs=pl.BlockSpec((tm, d), lambda i, pos: (i, 0)),
            scratch_shapes=[pltpu.VMEM((2, TOP_K, tm * SUBLANES, LANES), F32), pltpu.SemaphoreType.DMA((2,))]),
        out_shape=jax.ShapeDtypeStruct((n, d), F32),
        compiler_params=_params("arbitrary"),
        name="combine",
    )(pos_flat, ys, y1, gate, tw)


def _rope_tables(pos):
    half = HEAD_DIM // 2
    inv = ROPE_THETA ** (-jnp.arange(half, dtype=F32) / half)
    ang = pos.astype(F32)[:, None] * inv[None, :]
    cos = jnp.cos(ang)
    sin = jnp.sin(ang)
    reps = LANES // HEAD_DIM
    return (jnp.tile(jnp.concatenate([cos, cos], axis=1), (1, reps)),
            jnp.tile(jnp.concatenate([-sin, sin], axis=1), (1, reps)))


def _routing(top_e, n_tok):
    n_assign = n_tok * TOP_K
    e_flat = top_e.reshape(-1)
    order = jnp.argsort(e_flat)
    e_sorted = e_flat[order]
    tok_sorted = (order // TOP_K).astype(jnp.int32)
    counts = jnp.bincount(e_flat, length=N_EXPERTS)
    padded = (counts + MOE_ROWS - 1) // MOE_ROWS * MOE_ROWS
    pad_end = jnp.cumsum(padded)
    pad_start = pad_end - padded
    sort_start = jnp.cumsum(counts) - counts
    dest = (pad_start[e_sorted] + jnp.arange(n_assign, dtype=jnp.int32) - sort_start[e_sorted]).astype(jnp.int32)
    n_blocks = -(-n_assign // MOE_ROWS) + N_EXPERTS
    tok = jnp.full((n_blocks * MOE_ROWS,), n_tok, jnp.int32).at[dest].set(tok_sorted)
    blk_e = jnp.minimum(jnp.searchsorted(pad_end, jnp.arange(n_blocks, dtype=jnp.int32) * MOE_ROWS, side='right'),
                        N_EXPERTS - 1).astype(jnp.int32)
    pos = jnp.zeros((n_assign,), jnp.int32).at[order].set(dest)
    n_active = (pad_end[-1] // MOE_ROWS).astype(jnp.int32).reshape(1)
    return blk_e, n_active, tok, pos


def _ones_pad(a):
    return jnp.broadcast_to((jnp.arange(a.shape[-1]) == 0).astype(a.dtype), a.shape)


def _head_ones(a):
    return jnp.concatenate([a, _ones_pad(a)], axis=-1)


def _head_pad(a):
    return jnp.concatenate([a, jnp.zeros_like(a)], axis=-1)


def kernel(x_prompt, x_sample, c_prompt, c_sample, cache_moba_kv, cache_nsa_kv, state_nsa_win_kv, page_table,
           norm_g, w_ada, b_ada, w_in, qk_gain, cmp_pos, cmp_w1, cmp_w2, w_out, w_router, b_router, w_gu, b_gu,
           w_down, b_down):
    bsz, seq, d = x_prompt.shape
    n_req = x_sample.shape[0]
    depth = norm_g.shape[0]
    assert depth == 1 and x_sample.shape[1] == 1
    assert seq % ATT_TK == 0 and seq >= WINDOW + Q_TILE and seq // SEL_BLOCK <= LANES and seq // MOBA_BLOCK <= MOBA_BLOCK // SUBLANES
    n_pool, page = cache_moba_kv.shape[1], cache_moba_kv.shape[2]
    n_pages = page_table.shape[1]
    past = n_pages * page
    assert past % MOBA_BLOCK == 0 and 2 * page == MOBA_BLOCK and past // SEL_BLOCK < LANES
    assert state_nsa_win_kv.shape[2] == WINDOW
    layer = 0
    gw = G_NSA * HEAD_DIM
    n_prompt = bsz * seq

    bd = jnp.asarray(np.kron(np.eye(LANES // HEAD_DIM), np.full((HEAD_DIM, HEAD_DIM), 1.0 / HEAD_DIM)), BF16)
    w_in_bf = jnp.pad(w_in[layer], ((0, 0), (0, IN_COLS_PAD - IN_COLS))).astype(BF16)
    gains = jnp.tile(qk_gain[layer], (1, W_QM // HEAD_DIM))
    g1 = norm_g[layer, 0].reshape(1, d)
    g2 = norm_g[layer, 1].reshape(1, d)
    wo_bf = w_out[layer].astype(BF16)
    wr_pad = jnp.pad(w_router[layer], ((0, 0), (0, LANES - N_EXPERTS)))
    br_pad = jnp.pad(b_router[layer].reshape(1, N_EXPERTS), ((0, 0), (0, LANES - N_EXPERTS)),
                     constant_values=-jnp.inf)
    wgu_bf = w_gu[layer].astype(BF16)
    wd_bf = w_down[layer].astype(BF16)
    cmp_consts = _compress_consts(cmp_pos[layer], cmp_w1[layer], cmp_w2[layer], qk_gain[layer, 3])

    n_c = bsz + n_req
    n_c_pad = -(-n_c // SUBLANES) * SUBLANES
    c_all = jnp.pad(jnp.concatenate([c_prompt, c_sample], axis=0), ((0, n_c_pad - n_c), (0, 0)))
    mods = _ada(c_all, w_ada[layer], b_ada[layer])
    mods_p = [m.reshape(bsz, 1, d) for m in jnp.split(mods[:bsz], 6, axis=1)]
    mods_s = [m.reshape(1, n_req, d) for m in jnp.split(mods[bsz:n_c], 6, axis=1)]

    cos_p, sin_p = _rope_tables(jnp.arange(seq, dtype=jnp.int32))
    tm_p = 256
    (moba_rows, nsa_rows, win_rows, q_m, qn, qr, _, kmean, k_aug, v_aug, ks_aug, vs_aug, kw_pad, vw_aug,
     gates_g) = _proj(
        x_prompt.reshape(n_prompt, d), mods_p[0], mods_p[1], g1, w_in_bf, gains, bd, cos_p, sin_p,
        tm=tm_p, rows_per_mod=seq, pos_blocks=seq // tm_p, with_kmean=True)

    nbk = seq // MOBA_BLOCK
    kmean_h = kmean.reshape(bsz, nbk, H_MOBA, HEAD_DIM).transpose(0, 2, 1, 3)
    kmp = jnp.zeros((bsz, H_MOBA, LANES, LANES), F32).at[:, :, HEAD_DIM:HEAD_DIM + nbk, :HEAD_DIM].set(kmean_h)
    per_b = lambda a: a.reshape(bsz, seq, a.shape[-1])
    o_m = _moba_attention(per_b(q_m), kmp, per_b(k_aug), per_b(v_aug))

    kc, vc = _compress_prompt(per_b(nsa_rows), cmp_consts, bd)
    n_chunk = seq // CMP_STRIDE
    per_g = lambda a: a.reshape(bsz, -1, G_NSA, HEAD_DIM).transpose(0, 2, 1, 3)
    kcp = _head_pad(per_g(kc))
    vcd = _head_pad(per_g(vc)).astype(BF16)
    cover_p = _cover(n_chunk - 1, seq // SEL_BLOCK, n_chunk).T.astype(BF16)
    o_n = _nsa_attention(per_b(qn), per_b(qr), per_b(gates_g), kcp, vcd, cover_p, per_b(ks_aug), per_b(vs_aug),
                         per_b(kw_pad), per_b(vw_aug))
    o_p = jnp.concatenate([o_m, o_n], axis=-1).reshape(n_prompt, d)

    y1_p, h3_p, te_p, tw_p = _post(o_p, x_prompt.reshape(n_prompt, d), mods_p[2], mods_p[3], mods_p[4], g2, wo_bf,
                                   wr_pad, br_pad, tm=256, rows_per_mod=seq)

    cos_s, sin_s = _rope_tables(jnp.full((n_req,), past, jnp.int32))
    moba_new, nsa_new, win_new, q_m_s, qn_s, qr_s, gates_s = _proj(
        x_sample.reshape(n_req, d), mods_s[0], mods_s[1], g1, w_in_bf, gains, bd, cos_s, sin_s,
        tm=n_req, rows_per_mod=n_req, pos_blocks=1, with_kmean=False)
    pt_flat = page_table.reshape(-1).astype(jnp.int32)
    cache_m = cache_moba_kv[layer].transpose(0, 2, 3, 4, 1).reshape(n_pool, W_KVM, page)
    cache_n = cache_nsa_kv[layer].transpose(0, 2, 3, 4, 1).reshape(n_pool, 4 * gw, page)
    win_buf = state_nsa_win_kv[layer].transpose(0, 2, 3, 4, 1).reshape(n_req, 2 * gw, WINDOW)

    def rows8(q):
        qh = q.reshape(n_req, G_NSA, R_NSA, 1, HEAD_DIM)
        place = jnp.arange(G_NSA)[None, :, None, None, None] == jnp.arange(G_NSA)[None, None, None, :, None]
        return jnp.where(place, qh, 0.0).reshape(n_req, H_NSA, gw)

    def per_head(a):
        return jnp.repeat(a.reshape(n_req, G_NSA, HEAD_DIM), R_NSA, axis=1).reshape(n_req, W_QN)

    new_rows = jnp.stack([moba_new[:, :W_QM], moba_new[:, W_QM:], per_head(nsa_new[:, 2 * gw:3 * gw]),
                          per_head(nsa_new[:, 3 * gw:]), per_head(win_new[:, :gw]), per_head(win_new[:, gw:])], axis=1)
    gate_rows = jnp.repeat(gates_s[:, :N_GATE].reshape(n_req, H_NSA, 3).transpose(0, 2, 1), HEAD_DIM, axis=2)
    n_cmp_s = past // CMP_STRIDE
    cover_s = _cover(n_cmp_s - 1, past // SEL_BLOCK + 1, n_cmp_s).T
    wcat, pecat = _compress_cat_consts(cmp_pos[layer], cmp_w1[layer])
    o_s = _dec2_attention(cache_m, cache_n, win_buf, pt_flat, jnp.stack([q_m_s, qr_s], axis=1), rows8(qn_s), new_rows,
                          gate_rows, wcat, pecat, cmp_consts[2], cmp_consts[3], bd, cover_s,
                          n_req=n_req, n_pages=n_pages).reshape(n_req, d).astype(BF16)
    y1_s, h3_s, te_s, tw_s = _post(o_s, x_sample.reshape(n_req, d), mods_s[2], mods_s[3], mods_s[4], g2, wo_bf,
                                   wr_pad, br_pad, tm=n_req, rows_per_mod=n_req)

    n_tok = n_prompt + n_req
    top_e = jnp.concatenate([te_p[:, :TOP_K], te_s[:, :TOP_K]], axis=0)
    blk_e, n_active, pos_flat = _routing_pos(top_e, n_tok)
    tm_c = 128
    xs = jnp.zeros((blk_e.shape[0] * MOE_ROWS * SUBLANES, LANES), F32)
    xs = _dispatch(pos_flat, h3_p, xs, tm=2 * tm_c, tile_off=0)
    xs = _dispatch(pos_flat, h3_s, xs, tm=n_req, tile_off=n_prompt // n_req)
    ys = _expert_blocks(blk_e, n_active, xs, wgu_bf, b_gu[layer], wd_bf, b_down[layer])
    y_p = _combine(pos_flat, ys, y1_p, mods_p[5], tw_p, tm=tm_c, rows_per_mod=seq, tile_off=0)
    y_s = _combine(pos_flat, ys, y1_s, mods_s[5], tw_s, tm=n_req, rows_per_mod=n_req, tile_off=n_prompt // n_req)

    keep = min(WINDOW, seq)
    win_p = win_rows.reshape(bsz, seq, 2, G_NSA, HEAD_DIM)[:, seq - keep:]
    win_s = jnp.concatenate([state_nsa_win_kv[layer][:, 1:], win_new.reshape(n_req, 1, 2, G_NSA, HEAD_DIM)], axis=1)
    return (y_p.reshape(bsz, seq, d), y_s.reshape(n_req, 1, d),
            moba_rows.reshape(1, bsz, seq, 2, H_MOBA, HEAD_DIM), nsa_rows.reshape(1, bsz, seq, 4, G_NSA, HEAD_DIM),
            win_p[None], moba_new.reshape(1, n_req, 1, 2, H_MOBA, HEAD_DIM),
            nsa_new.reshape(1, n_req, 1, 4, G_NSA, HEAD_DIM), win_s[None])
```

```python
import functools

import numpy as np
import jax
import jax.numpy as jnp
from jax import lax
from jax.experimental import pallas as pl
from jax.experimental.pallas import tpu as pltpu

F32 = jnp.float32
BF16 = jnp.bfloat16
HIGHEST = lax.Precision.HIGHEST

LANES = 128
SUBLANES = 8
HEAD_DIM = 64
H_MOBA = 8
H_NSA = 8
G_NSA = 2
R_NSA = H_NSA // G_NSA
MOBA_BLOCK = 256
MOBA_TOPK = 3
CMP_LEN = 32
CMP_STRIDE = 16
SEL_BLOCK = 64
SEL_TOPN = 16
WINDOW = 512
N_EXPERTS = 32
TOP_K = 4
SWIGLU_LIMIT = 7.0
SWIGLU_ALPHA = 1.702
ROPE_THETA = 10000.0
NORM_EPS = 1e-6
NEG_BIG = -1e30
TINY = 1e-30
SCALE = HEAD_DIM ** -0.5
SCALE_LOG2E = SCALE * 1.4426950408889634
Q_TILE = 128
MOBA_Q = MOBA_BLOCK
ATT_TK = 1024
MOE_ROWS = 256
VMEM_LIMIT = 56 * 1024 * 1024

W_QM = H_MOBA * HEAD_DIM
W_KVM = 2 * H_MOBA * HEAD_DIM
W_QN = H_NSA * HEAD_DIM
W_KVN = 6 * G_NSA * HEAD_DIM
N_GATE = 3 * H_NSA
IN_COLS = W_QM + W_KVM + W_QN + W_KVN + N_GATE
IN_COLS_PAD = W_QM + W_KVM + W_QN + W_KVN + LANES


def _iota(shape, dim):
    return lax.broadcasted_iota(jnp.int32, shape, dim)


def _dot(a, b, precision=None):
    return jnp.dot(a, b, preferred_element_type=F32, precision=precision)


def _dot_nt(a, b, precision=None):
    return lax.dot_general(a, b, (((1,), (1,)), ((), ())), preferred_element_type=F32, precision=precision)


def _split_bf16(a):
    hi = a.astype(BF16)
    return hi, (a - hi.astype(F32)).astype(BF16)


def _dot_nt_x3(a, b):
    ah, al = _split_bf16(a)
    bh, bl = _split_bf16(b)
    return _dot_nt(ah, bh) + _dot_nt(al, bh) + _dot_nt(ah, bl)


def _params(*sem):
    return pltpu.CompilerParams(dimension_semantics=sem, vmem_limit_bytes=VMEM_LIMIT)


def _seg_meansq(z, bd):
    zz = z * z
    hi = zz.astype(BF16)
    lo = (zz - hi.astype(F32)).astype(BF16)
    outs = []
    for c in range(z.shape[1] // LANES):
        sl = slice(c * LANES, (c + 1) * LANES)
        outs.append(_dot(hi[:, sl], bd) + _dot(lo[:, sl], bd))
    return outs[0] if len(outs) == 1 else jnp.concatenate(outs, axis=1)


def _head_norm(z, gain, bd):
    return z * lax.rsqrt(_seg_meansq(z, bd) + NORM_EPS) * gain


def _rope(z, cos, sin):
    outs = []
    first = (_iota((z.shape[0], LANES), 1) % HEAD_DIM) < (HEAD_DIM // 2)
    for c in range(z.shape[1] // LANES):
        x = z[:, c * LANES:(c + 1) * LANES]
        swapped = jnp.where(first, pltpu.roll(x, LANES - HEAD_DIM // 2, 1), pltpu.roll(x, HEAD_DIM // 2, 1))
        outs.append(x * cos + swapped * sin)
    return outs[0] if len(outs) == 1 else jnp.concatenate(outs, axis=1)


def _top_k_lanes(cur, k):
    lane = _iota(cur.shape, 1).astype(F32)
    picked = jnp.zeros(cur.shape, F32)
    vals, ids = [], []
    for _ in range(k):
        mx = jnp.max(cur, axis=1, keepdims=True)
        first = jnp.min(jnp.where(cur == mx, lane, 1e9), axis=1, keepdims=True)
        hit = lane == first
        picked = jnp.where(hit, 1.0, picked)
        cur = jnp.where(hit, -jnp.inf, cur)
        vals.append(mx)
        ids.append(first)
    return picked, vals, ids


def _top_k_sublanes(cur, k):
    idx = _iota(cur.shape, 0).astype(F32)
    picked = jnp.zeros(cur.shape, F32)
    for _ in range(k):
        mx = jnp.max(cur, axis=0, keepdims=True)
        first = jnp.min(jnp.where(cur == mx, idx, 1e9), axis=0, keepdims=True)
        hit = idx == first
        picked = jnp.where(hit, 1.0, picked)
        cur = jnp.where(hit, -jnp.inf, cur)
    return picked


def _flash_step(q, k, v, mask, m, acc):
    s = _dot_nt(q, k)
    if mask is not None:
        s = jnp.where(mask, s, NEG_BIG)
    m_new = jnp.maximum(m, jnp.max(s, axis=1, keepdims=True))
    acc_new = jnp.exp2(m - m_new) * acc + _dot(jnp.exp2(s - m_new).astype(BF16), v)
    return m_new, acc_new


def _flash_finish(acc):
    return acc / jnp.maximum(acc[:, HEAD_DIM:HEAD_DIM + 1], TINY)


def _ada_kernel(c_ref, w_ref, b_ref, o_ref):
    c = c_ref[...]
    o_ref[...] = _dot(c * jax.nn.sigmoid(c), w_ref[...], HIGHEST) + b_ref[...]


def _ada(c_all, w_ada, b_ada):
    n, d = c_all.shape
    cols = w_ada.shape[1]
    tn = 1024
    return pl.pallas_call(
        _ada_kernel,
        grid=(cols // tn,),
        in_specs=[pl.BlockSpec((n, d), lambda j: (0, 0)),
                  pl.BlockSpec((d, tn), lambda j: (0, j)),
                  pl.BlockSpec((1, tn), lambda j: (0, j))],
        out_specs=pl.BlockSpec((n, tn), lambda j: (0, j)),
        out_shape=jax.ShapeDtypeStruct((n, cols), F32),
        compiler_params=_params("arbitrary"),
        name="ada",
    )(c_all, w_ada, b_ada.reshape(1, cols))


def _proj_kernel(x_ref, sh_ref, sc_ref, g_ref, w_ref, gains_ref, bd_ref, cos_ref, sin_ref,
                 moba_ref, nsa_ref, win_ref, qm_ref, qn_ref, qr_ref, gate_ref, *attn_refs, with_kmean, pos_blocks):
    x = x_ref[...]
    y = x * lax.rsqrt(jnp.mean(x * x, axis=1, keepdims=True) + NORM_EPS) * g_ref[...]
    h = (y * (1.0 + sc_ref[0]) + sh_ref[0]).astype(BF16)
    bd = bd_ref[...]
    cos = cos_ref[...]
    sin = sin_ref[...]
    o = 0

    def seg(width):
        nonlocal o
        z = _dot(h, w_ref[:, o:o + width])
        o += width
        return z

    def gain(i, width):
        return gains_ref[i:i + 1, 0:width]

    qm_ref[...] = _rope(_head_norm(seg(W_QM), gain(0, W_QM), bd), cos, sin)
    k_m = _rope(_head_norm(seg(W_QM), gain(1, W_QM), bd), cos, sin)
    v_m = seg(W_QM)
    qn = _head_norm(seg(W_QN), gain(2, W_QN), bd)
    qn_ref[...] = qn
    qr_ref[...] = _rope(qn, cos, sin)
    gw = G_NSA * HEAD_DIM
    cmp_raw = seg(2 * gw)
    k_sel = _rope(_head_norm(seg(gw), gain(4, gw), bd), cos, sin)
    v_sel = seg(gw)
    k_win = _rope(_head_norm(seg(gw), gain(5, gw), bd), cos, sin)
    v_win = seg(gw)
    gates = jax.nn.sigmoid(seg(LANES))
    gate_ref[...] = gates
    moba_rows = jnp.concatenate([k_m, v_m], axis=1)
    nsa_rows = jnp.concatenate([cmp_raw, k_sel, v_sel], axis=1)
    win_rows = jnp.concatenate([k_win, v_win], axis=1)
    if not with_kmean:
        moba_ref[...] = moba_rows
        nsa_ref[...] = nsa_rows
        win_ref[...] = win_rows
    else:
        kmean_ref, cmpraw_ref, kaug_ref, vaug_ref, ksaug_ref, vsaug_ref, kwp_ref, vwaug_ref, gg_ref = attn_refs
        moba_ref[0] = moba_rows.T
        nsa_ref[0] = nsa_rows.T
        win_ref[0] = win_rows.T
        cmpraw_ref[...] = cmp_raw
        tm = k_m.shape[0]
        kmean_ref[0] = jnp.mean(k_m.reshape(tm // MOBA_BLOCK, MOBA_BLOCK, W_QM), axis=1)
        lane = _iota((tm, LANES), 1)
        posv = (pl.program_id(0) % pos_blocks) * tm + _iota((tm, LANES), 0)
        lo = lane < HEAD_DIM
        pad_blk = jnp.where((lane >= HEAD_DIM) & (lane - HEAD_DIM == posv // MOBA_BLOCK), 1.0, 0.0)
        pad_one = jnp.where(lane == HEAD_DIM, 1.0, 0.0)
        oh_sel = jnp.where(lane == posv // SEL_BLOCK, 1.0, 0.0).astype(BF16)

        def halves(x):
            return x, pltpu.roll(x, HEAD_DIM, 1)

        for c in range(W_QM // LANES):
            cols = slice(c * LANES, (c + 1) * LANES)
            for hh, (kh, vh) in enumerate(zip(halves(k_m[:, cols]), halves(v_m[:, cols]))):
                hcols = slice((2 * c + hh) * LANES, (2 * c + hh + 1) * LANES)
                kaug_ref[:, hcols] = jnp.where(lo, kh, pad_blk).astype(BF16)
                vaug_ref[:, hcols] = jnp.where(lo, vh, pad_one).astype(BF16)
        for g, (ks, vs, kw, vw) in enumerate(zip(halves(k_sel), halves(v_sel), halves(k_win), halves(v_win))):
            ksaug_ref[:, 2 * g * LANES:(2 * g + 1) * LANES] = oh_sel
            ksaug_ref[:, (2 * g + 1) * LANES:(2 * g + 2) * LANES] = jnp.where(lo, ks, 0.0).astype(BF16)
            gcols = slice(g * LANES, (g + 1) * LANES)
            vsaug_ref[:, gcols] = jnp.where(lo, vs, pad_one).astype(BF16)
            kwp_ref[:, gcols] = jnp.where(lo, kw, 0.0).astype(BF16)
            vwaug_ref[:, gcols] = jnp.where(lo, vw, pad_one).astype(BF16)
            gg_ref[:, gcols] = gates if g == 0 else pltpu.roll(gates, LANES - g * 3 * R_NSA, 1)


def _proj(x, shift, scale, g, w_in_bf, gains, bd, cos, sin, *, tm, rows_per_mod, pos_blocks, with_kmean):
    n, d = x.shape
    nt = n // tm
    mod_r = shift.shape[1]
    mod_map = lambda i: (i // (rows_per_mod // tm), 0, 0)
    pos_map = lambda i: (i % pos_blocks, 0)
    row = lambda w: pl.BlockSpec((tm, w), lambda i: (i, 0))
    const = lambda a: pl.BlockSpec(a.shape, lambda i: (0,) * a.ndim)
    gw = G_NSA * HEAD_DIM
    cache_widths = (W_KVM, 4 * gw, 2 * gw)
    if with_kmean:
        nb = n // rows_per_mod
        tpb = rows_per_mod // tm
        out_shapes = [jax.ShapeDtypeStruct((nb, w, rows_per_mod), F32) for w in cache_widths]
        out_specs = [pl.BlockSpec((1, w, tm), lambda i: (i // tpb, 0, i % tpb)) for w in cache_widths]
    else:
        out_shapes = [jax.ShapeDtypeStruct((n, w), F32) for w in cache_widths]
        out_specs = [row(w) for w in cache_widths]
    out_shapes += [jax.ShapeDtypeStruct((n, W_QM), F32), jax.ShapeDtypeStruct((n, W_QN), F32),
                   jax.ShapeDtypeStruct((n, W_QN), F32), jax.ShapeDtypeStruct((n, LANES), F32)]
    out_specs += [row(W_QM), row(W_QN), row(W_QN), row(LANES)]
    if with_kmean:
        nbt = tm // MOBA_BLOCK
        out_shapes.append(jax.ShapeDtypeStruct((nt, nbt, W_QM), F32))
        out_specs.append(pl.BlockSpec((1, nbt, W_QM), lambda i: (i, 0, 0)))
        gl = G_NSA * LANES
        for width, dtype in ((2 * gw, F32), (H_MOBA * LANES, BF16), (H_MOBA * LANES, BF16), (2 * gl, BF16), (gl, BF16),
                             (gl, BF16), (gl, BF16), (gl, F32)):
            out_shapes.append(jax.ShapeDtypeStruct((n, width), dtype))
            out_specs.append(row(width))
    return pl.pallas_call(
        functools.partial(_proj_kernel, with_kmean=with_kmean, pos_blocks=pos_blocks),
        grid=(nt,),
        in_specs=[row(d), pl.BlockSpec((1, mod_r, d), mod_map), pl.BlockSpec((1, mod_r, d), mod_map),
                  const(g), const(w_in_bf), const(gains), const(bd),
                  pl.BlockSpec((tm, LANES), pos_map), pl.BlockSpec((tm, LANES), pos_map)],
        out_specs=out_specs,
        out_shape=out_shapes,
        compiler_params=_params("parallel"),
        name="proj",
    )(x, shift, scale, g, w_in_bf, gains, bd, cos, sin)


def _compress_compute(src_refs, pe_ref, w1_ref, w2_ref, gk_ref, bd_ref, kc_ref, vc_ref, n_rows):
    n_chunk = n_rows // CMP_STRIDE
    gw = G_NSA * HEAD_DIM
    for kv in range(2):
        acc_a = jnp.zeros((n_chunk, gw), F32)
        acc_b = jnp.zeros((n_chunk, gw), F32)
        for j in range(CMP_STRIDE):
            xj = src_refs[kv][pl.ds(j, n_chunk, stride=CMP_STRIDE), :]
            acc_a = acc_a + _dot(xj + pe_ref[kv, 0, j:j + 1, :], w1_ref[kv, 0, j], HIGHEST)
            acc_b = acc_b + _dot(xj + pe_ref[kv, 1, j:j + 1, :], w1_ref[kv, 1, j], HIGHEST)
        hid = jax.nn.gelu(acc_a + pltpu.roll(acc_b, n_chunk - 1, 0))
        out = _dot(hid, w2_ref[kv], HIGHEST)
        if kv == 0:
            kc_ref[0] = _head_norm(out, gk_ref[...], bd_ref[...])
        else:
            vc_ref[0] = out


def _compress_prompt_kernel(k_ref, v_ref, pe_ref, w1_ref, w2_ref, gk_ref, bd_ref, kc_ref, vc_ref, *, n_rows):
    _compress_compute((k_ref.at[0], v_ref.at[0]), pe_ref, w1_ref, w2_ref, gk_ref, bd_ref, kc_ref, vc_ref, n_rows)


def _compress_paged_kernel(pt_ref, page_ref, pe_ref, w1_ref, w2_ref, gk_ref, bd_ref, kc_ref, vc_ref, kbuf, vbuf,
                           *, n_rows, page):
    p = pl.program_id(1)
    gw = G_NSA * HEAD_DIM
    rows = pl.ds(pl.multiple_of(p * page, page), page)
    kbuf[rows, :] = page_ref[0, :, 0:gw]
    vbuf[rows, :] = page_ref[0, :, gw:2 * gw]

    @pl.when(p == pl.num_programs(1) - 1)
    def _():
        _compress_compute((kbuf, vbuf), pe_ref, w1_ref, w2_ref, gk_ref, bd_ref, kc_ref, vc_ref, n_rows)


def _compress_consts(cmp_pos, cmp_w1, cmp_w2, gain_k_cmp):
    pe = jnp.tile(cmp_pos.reshape(2, 2, CMP_STRIDE, HEAD_DIM), (1, 1, 1, G_NSA))
    eye = jnp.eye(G_NSA, dtype=F32)
    w1 = cmp_w1.reshape(2, 2, CMP_STRIDE, HEAD_DIM, HEAD_DIM)
    w1bd = jnp.einsum("gh,kajde->kajgdhe", eye, w1).reshape(2, 2, CMP_STRIDE, G_NSA * HEAD_DIM, G_NSA * HEAD_DIM)
    w2bd = jnp.einsum("gh,kde->kgdhe", eye, cmp_w2).reshape(2, G_NSA * HEAD_DIM, G_NSA * HEAD_DIM)
    gk = jnp.tile(gain_k_cmp.reshape(1, HEAD_DIM), (1, G_NSA))
    return pe, w1bd, w2bd, gk


def _compress_prompt(cmp_raw, consts, bd):
    b, s, _ = cmp_raw.shape
    pe, w1bd, w2bd, gk = consts
    n_chunk = s // CMP_STRIDE
    gw = G_NSA * HEAD_DIM
    const = lambda a: pl.BlockSpec(a.shape, lambda i: (0,) * a.ndim)
    out = jax.ShapeDtypeStruct((b, n_chunk, gw), F32)
    return pl.pallas_call(
        functools.partial(_compress_prompt_kernel, n_rows=s),
        grid=(b,),
        in_specs=[pl.BlockSpec((1, s, gw), lambda i: (i, 0, 0)), pl.BlockSpec((1, s, gw), lambda i: (i, 0, 1)),
                  const(pe), const(w1bd), const(w2bd), const(gk), const(bd)],
        out_specs=[pl.BlockSpec((1, n_chunk, gw), lambda i: (i, 0, 0))] * 2,
        out_shape=[out, out],
        compiler_params=_params("parallel"),
        name="compress_prompt",
    )(cmp_raw, cmp_raw, pe, w1bd, w2bd, gk, bd)


def _compress_paged(cache_nsa, pt_flat, consts, bd, *, n_req, n_pages):
    _, page, _ = cache_nsa.shape
    pe, w1bd, w2bd, gk = consts
    n_rows = n_pages * page
    n_chunk = n_rows // CMP_STRIDE
    gw = G_NSA * HEAD_DIM
    const = lambda a: pl.BlockSpec(a.shape, lambda b, p, pt: (0,) * a.ndim)
    out = jax.ShapeDtypeStruct((n_req, n_chunk, gw), F32)
    return pl.pallas_call(
        functools.partial(_compress_paged_kernel, n_rows=n_rows, page=page),
        grid_spec=pltpu.PrefetchScalarGridSpec(
            num_scalar_prefetch=1,
            grid=(n_req, n_pages),
            in_specs=[pl.BlockSpec((1, page, 2 * gw), lambda b, p, pt: (pt[b * n_pages + p], 0, 0)),
                      const(pe), const(w1bd), const(w2bd), const(gk), const(bd)],
            out_specs=[pl.BlockSpec((1, n_chunk, gw), lambda b, p, pt: (b, 0, 0))] * 2,
            scratch_shapes=[pltpu.VMEM((n_rows, gw), F32)] * 2),
        out_shape=[out, out],
        compiler_params=_params("parallel", "arbitrary"),
        name="compress_paged",
    )(pt_flat, cache_nsa, pe, w1bd, w2bd, gk, bd)


def _moba_kernel(q_ref, kmp_ref, k_ref, v_ref, o_ref):
    qi = pl.program_id(2)
    own = qi
    q2 = q_ref[0]
    lane = _iota((MOBA_Q, LANES), 1)
    n_blk = MOBA_BLOCK // SUBLANES
    blk = _iota((n_blk, MOBA_Q), 0)
    valid = blk < own
    q_augs = []
    for h in range(2):
        qh = q2 if h == 0 else pltpu.roll(q2, HEAD_DIM, 1)
        q0 = jnp.where(lane < HEAD_DIM, qh, 0.0)
        score = _dot_nt_x3(kmp_ref[0, h], q0)[HEAD_DIM:HEAD_DIM + n_blk, :]
        picked = _top_k_sublanes(jnp.where(valid, score, -jnp.inf), MOBA_TOPK)
        sel = ((picked > 0.5) & valid) | (blk == own)
        bias_t = jnp.concatenate([jnp.zeros((HEAD_DIM, MOBA_Q), F32), jnp.where(sel, 0.0, NEG_BIG),
                                  jnp.zeros((LANES - HEAD_DIM - n_blk, MOBA_Q), F32)], axis=0)
        q_augs.append(jnp.where(lane < HEAD_DIM, q0 * SCALE_LOG2E, bias_t.T).astype(BF16))
    pos = qi * MOBA_Q + _iota((MOBA_Q, ATT_TK), 0)

    def tile(j, carry, masked):
        start = pl.multiple_of(j * ATT_TK, ATT_TK)
        mask = None
        if masked:
            mask = (j * ATT_TK + _iota((MOBA_Q, ATT_TK), 1)) <= pos
        return tuple(_flash_step(q_augs[h], k_ref[0, pl.ds(start, ATT_TK), h * LANES:(h + 1) * LANES],
                                 v_ref[0, pl.ds(start, ATT_TK), h * LANES:(h + 1) * LANES], mask, *carry[h])
                     for h in range(2))

    init = (jnp.full((MOBA_Q, 1), NEG_BIG, F32), jnp.zeros((MOBA_Q, LANES), F32))
    jd = (own * MOBA_BLOCK) // ATT_TK
    carry = lax.fori_loop(0, jd, lambda j, c: tile(j, c, False), tile(jd, (init, init), True))
    o_ref[0] = jnp.where(lane < HEAD_DIM, _flash_finish(carry[0][1]),
                         pltpu.roll(_flash_finish(carry[1][1]), HEAD_DIM, 1)).astype(o_ref.dtype)


def _moba_attention(q_m, kmp, k_aug, v_aug):
    b, s, _ = q_m.shape
    return pl.pallas_call(
        _moba_kernel,
        grid=(b, H_MOBA // 2, s // MOBA_Q),
        in_specs=[pl.BlockSpec((1, MOBA_Q, LANES), lambda b, h, i: (b, i, h)),
                  pl.BlockSpec((1, 2, LANES, LANES), lambda b, h, i: (b, h, 0, 0)),
                  pl.BlockSpec((1, s, 2 * LANES), lambda b, h, i: (b, 0, h)),
                  pl.BlockSpec((1, s, 2 * LANES), lambda b, h, i: (b, 0, h))],
        out_specs=pl.BlockSpec((1, MOBA_Q, LANES), lambda b, h, i: (b, i, h)),
        out_shape=jax.ShapeDtypeStruct((b, s, W_QM), BF16),
        compiler_params=_params("parallel", "parallel", "arbitrary"),
        name="moba_attention",
    )(q_m, kmp, k_aug, v_aug)


def _stack_heads(q4):
    lane = _iota((Q_TILE, LANES), 1)
    parts = []
    for r in range(R_NSA):
        c = q4[:, (r // 2) * LANES:(r // 2 + 1) * LANES]
        if r % 2:
            c = pltpu.roll(c, HEAD_DIM, 1)
        parts.append(jnp.where(lane < HEAD_DIM, c, 0.0))
    return jnp.concatenate(parts, axis=0)


def _nsa_kernel(qn_ref, qr_ref, gate_ref, kc_ref, vc_ref, cover_ref, ks_ref, vs_ref, kw_ref, vw_ref, o_ref,
                *, n_chunk):
    qi = pl.program_id(2)
    rows = R_NSA * Q_TILE
    qloc = _iota((rows, 1), 0) % Q_TILE
    pos = qi * Q_TILE + qloc

    qn = _stack_heads(qn_ref[0])
    s = _dot_nt_x3(qn, kc_ref[0, 0]) * SCALE
    cmask = (_iota((rows, n_chunk), 1) * CMP_STRIDE + (CMP_LEN - 1)) <= pos
    s = jnp.where(cmask, s, NEG_BIG)
    e = jnp.where(cmask, jnp.exp(s - jnp.max(s, axis=1, keepdims=True)), 0.0)
    p_cmp = e / jnp.maximum(jnp.sum(e, axis=1, keepdims=True), TINY)
    o_cmp = _dot(p_cmp.astype(BF16), vc_ref[0, 0])

    p_grp = p_cmp[0:Q_TILE]
    for r in range(1, R_NSA):
        p_grp = p_grp + p_cmp[r * Q_TILE:(r + 1) * Q_TILE]
    p_hi, p_lo = _split_bf16(p_grp)
    imp_t = _dot_nt(cover_ref[...], p_hi) + _dot_nt(cover_ref[...], p_lo)
    blk_t = _iota((LANES, Q_TILE), 0)
    own_t = (qi * Q_TILE + _iota((LANES, Q_TILE), 1)) // SEL_BLOCK
    forced = (blk_t == 0) | (blk_t == own_t) | (blk_t == own_t - 1)
    score = jnp.where(blk_t <= own_t, jnp.where(forced, jnp.inf, imp_t), -jnp.inf)
    picked = _top_k_sublanes(score, SEL_TOPN)
    bias = jnp.where((picked > 0.5) & (blk_t <= own_t), 0.0, NEG_BIG).T
    blk = _iota((Q_TILE, LANES), 1)

    qr = (_stack_heads(qr_ref[0]) * SCALE_LOG2E).astype(BF16)
    q_aug = jnp.concatenate([jnp.concatenate([bias] * R_NSA, axis=0).astype(BF16), qr], axis=1)
    init = (jnp.full((rows, 1), NEG_BIG, F32), jnp.zeros((rows, LANES), F32))

    tk = ATT_TK
    jd = (qi * Q_TILE) // tk

    def sel_tile(j, carry, masked):
        start = pl.multiple_of(j * tk, tk)
        mask = None
        if masked:
            mask = (j * tk + _iota((rows, tk), 1)) <= pos
        return _flash_step(q_aug, ks_ref[0, pl.ds(start, tk), :], vs_ref[0, pl.ds(start, tk), :], mask, *carry)

    o_sel = _flash_finish(lax.fori_loop(0, jd, lambda j, c: sel_tile(j, c, False), sel_tile(jd, init, True))[1])

    span = WINDOW + Q_TILE
    w0 = pl.multiple_of(jnp.maximum(qi * Q_TILE - WINDOW, 0), Q_TILE)
    kpos = w0 + _iota((rows, span), 1)
    wmask = (kpos <= pos) & (kpos > pos - WINDOW)
    o_win = _flash_finish(_flash_step(qr, kw_ref[0, pl.ds(w0, span), :], vw_ref[0, pl.ds(w0, span), :], wmask,
                                      *init)[1])

    gates = gate_ref[0]
    heads = []
    for r in range(R_NSA):
        rs = slice(r * Q_TILE, (r + 1) * Q_TILE)
        heads.append(gates[:, 3 * r:3 * r + 1] * o_cmp[rs] + gates[:, 3 * r + 1:3 * r + 2] * o_sel[rs]
                     + gates[:, 3 * r + 2:3 * r + 3] * o_win[rs])
    lo = blk < HEAD_DIM
    o_ref[0] = jnp.concatenate([jnp.where(lo, heads[0], pltpu.roll(heads[1], HEAD_DIM, 1)),
                                jnp.where(lo, heads[2], pltpu.roll(heads[3], HEAD_DIM, 1))], axis=1).astype(o_ref.dtype)


def _nsa_attention(qn, qr, gates_g, kcp, vcd, cover, ks_aug, vs_dup, kw_pad, vw_dup):
    b, s, _ = qn.shape
    n_chunk = kcp.shape[2]
    gq = R_NSA * HEAD_DIM
    per_g = lambda w: pl.BlockSpec((1, s, w), lambda b, g, i: (b, 0, g))
    return pl.pallas_call(
        functools.partial(_nsa_kernel, n_chunk=n_chunk),
        grid=(b, G_NSA, s // Q_TILE),
        in_specs=[pl.BlockSpec((1, Q_TILE, gq), lambda b, g, i: (b, i, g)),
                  pl.BlockSpec((1, Q_TILE, gq), lambda b, g, i: (b, i, g)),
                  pl.BlockSpec((1, Q_TILE, LANES), lambda b, g, i: (b, i, g)),
                  pl.BlockSpec((1, 1, n_chunk, LANES), lambda b, g, i: (b, g, 0, 0)),
                  pl.BlockSpec((1, 1, n_chunk, LANES), lambda b, g, i: (b, g, 0, 0)),
                  pl.BlockSpec(cover.shape, lambda b, g, i: (0, 0)),
                  per_g(2 * LANES), per_g(LANES), per_g(LANES), per_g(LANES)],
        out_specs=pl.BlockSpec((1, Q_TILE, gq), lambda b, g, i: (b, i, g)),
        out_shape=jax.ShapeDtypeStruct((b, s, W_QN), BF16),
        compiler_params=_params("parallel", "parallel", "arbitrary"),
        name="nsa_attention",
    )(qn, qr, gates_g, kcp, vcd, cover, ks_aug, vs_dup, kw_pad, vw_dup)


def _cover(n_cmp, n_sel, rows):
    c0 = np.arange(rows)[:, None] * CMP_STRIDE
    b0 = np.arange(LANES)[None, :] * SEL_BLOCK
    ok = (c0 < b0 + SEL_BLOCK) & (c0 + CMP_LEN > b0) & (np.arange(rows)[:, None] < n_cmp) & (np.arange(LANES)[None, :] < n_sel)
    return jnp.asarray(ok.astype(np.float32))


def _dec_kernel(pt_ref, ma_ref, mb_ref, na_ref, nb_ref, wb_ref, qbd_ref, q8n_ref, q8r_ref, g8_ref, newm_ref,
                news_ref, neww_ref, kc_ref, vc_ref, cover_ref, om_ref, o8_ref,
                m_sc, l_sc, s_sc, acc_sc, msel_sc, lsel_sc, asel_sc, bias_sc, ocmp_sc, *, n_steps, page):
    j = pl.program_id(1)
    sub8 = _iota((SUBLANES, LANES), 0)
    lane8 = _iota((SUBLANES, LANES), 1)
    tk = 2 * page
    q8r = q8r_ref[0] * SCALE
    q8r_bf = q8r.astype(BF16)

    @pl.when(j == 0)
    def _():
        n_chunk = kc_ref.shape[1]
        s = _dot_nt(q8n_ref[0], kc_ref[0], HIGHEST) * SCALE
        cmask = _iota((SUBLANES, n_chunk), 1) < (n_chunk - 1)
        s = jnp.where(cmask, s, NEG_BIG)
        e = jnp.where(cmask, jnp.exp(s - jnp.max(s, axis=1, keepdims=True)), 0.0)
        p_cmp = e / jnp.maximum(jnp.sum(e, axis=1, keepdims=True), TINY)
        ocmp_sc[...] = _dot(p_cmp.astype(BF16), vc_ref[0].astype(BF16))
        subc = _iota((SUBLANES, n_chunk), 0)
        g0 = jnp.sum(jnp.where(subc < R_NSA, p_cmp, 0.0), axis=0, keepdims=True)
        g1 = jnp.sum(jnp.where(subc >= R_NSA, p_cmp, 0.0), axis=0, keepdims=True)
        imp = _dot(jnp.where(subc < R_NSA, g0, g1), cover_ref[...], HIGHEST)
        own = (n_steps * tk) // SEL_BLOCK
        forced = (lane8 == 0) | (lane8 == own) | (lane8 == own - 1)
        score = jnp.where(lane8 <= own, jnp.where(forced, jnp.inf, imp), -jnp.inf)
        picked, _, _ = _top_k_lanes(score, SEL_TOPN)
        bias_sc[...] = jnp.where((picked > 0.5) & (lane8 <= own), 0.0, NEG_BIG)
        msel_sc[...] = jnp.full((SUBLANES, LANES), NEG_BIG, F32)
        lsel_sc[...] = jnp.zeros((SUBLANES, LANES), F32)
        asel_sc[...] = jnp.zeros((SUBLANES, LANES), F32)

    hw = H_MOBA * HEAD_DIM
    k_m = jnp.concatenate([ma_ref[0, :, 0:hw], mb_ref[0, :, 0:hw]], axis=0)
    v_m = jnp.concatenate([ma_ref[0, :, hw:2 * hw], mb_ref[0, :, hw:2 * hw]], axis=0)
    qbd = qbd_ref[0]
    s = _dot_nt((qbd * SCALE).astype(BF16), k_m.astype(BF16))
    mj = jnp.max(s, axis=1, keepdims=True)
    p = jnp.exp(s - mj)
    m_sc[j] = jnp.broadcast_to(mj, (SUBLANES, LANES))
    l_sc[j] = jnp.broadcast_to(jnp.sum(p, axis=1, keepdims=True), (SUBLANES, LANES))
    acc_sc[j] = _dot(p.astype(BF16), v_m.astype(BF16))
    kmean = jnp.sum(k_m, axis=0, keepdims=True) * (1.0 / MOBA_BLOCK)
    s_sc[j] = jnp.broadcast_to(jnp.sum(qbd * kmean, axis=1, keepdims=True), (SUBLANES, LANES))

    gw = G_NSA * HEAD_DIM
    k_s = jnp.concatenate([na_ref[0, :, 0:gw], nb_ref[0, :, 0:gw]], axis=0).astype(BF16)
    v_s = jnp.concatenate([na_ref[0, :, gw:2 * gw], nb_ref[0, :, gw:2 * gw]], axis=0).astype(BF16)
    s = _dot_nt(q8r_bf, k_s)
    bias = bias_sc[...]
    kb = _iota((SUBLANES, tk), 1) // SEL_BLOCK
    bias_k = jnp.zeros((SUBLANES, tk), F32)
    for q in range(tk // SEL_BLOCK):
        bq = jnp.sum(jnp.where(lane8 == j * (tk // SEL_BLOCK) + q, bias, 0.0), axis=1, keepdims=True)
        bias_k = jnp.where(kb == q, bq, bias_k)
    s = s + bias_k
    m_old = msel_sc[:, 0:1]
    m_new = jnp.maximum(m_old, jnp.max(s, axis=1, keepdims=True))
    alpha = jnp.exp(m_old - m_new)
    p = jnp.exp(s - m_new)
    msel_sc[...] = jnp.broadcast_to(m_new, (SUBLANES, LANES))
    lsel_sc[...] = alpha * lsel_sc[...] + jnp.sum(p, axis=1, keepdims=True)
    asel_sc[...] = alpha * asel_sc[...] + _dot(p.astype(BF16), v_s)

    @pl.when(j == n_steps - 1)
    def _():
        news = news_ref[0]
        s_new = jnp.sum(q8r * news[0:1], axis=1, keepdims=True)
        m_old = msel_sc[:, 0:1]
        m_new = jnp.maximum(m_old, s_new)
        alpha = jnp.exp(m_old - m_new)
        pn = jnp.exp(s_new - m_new)
        l = alpha * lsel_sc[...] + pn
        o_sel = (alpha * asel_sc[...] + pn * news[1:2]) / jnp.maximum(l, TINY)

        neww = neww_ref[0]
        kw = wb_ref[0, :, 0:gw].astype(BF16)
        vw = wb_ref[0, :, gw:2 * gw].astype(BF16)
        nw = kw.shape[0]
        s = _dot_nt(q8r_bf, kw)
        wmask = _iota((SUBLANES, nw), 1) >= 1
        s = jnp.where(wmask, s, NEG_BIG)
        s_new = jnp.sum(q8r * neww[0:1], axis=1, keepdims=True)
        mw = jnp.maximum(jnp.max(s, axis=1, keepdims=True), s_new)
        e = jnp.where(wmask, jnp.exp(s - mw), 0.0)
        en = jnp.exp(s_new - mw)
        lw = jnp.sum(e, axis=1, keepdims=True) + en
        o_win = (_dot(e.astype(BF16), vw) + en * neww[1:2]) / jnp.maximum(lw, TINY)

        g8 = g8_ref[0]
        o8_ref[0] = g8[:, 0:1] * ocmp_sc[...] + g8[:, 1:2] * o_sel + g8[:, 2:3] * o_win

        newm = newm_ref[0]
        scores = [s_sc[b] for b in range(n_steps)]
        m_tot = jnp.broadcast_to(jnp.sum(qbd * newm[0:1], axis=1, keepdims=True) * SCALE, (SUBLANES, LANES))
        s_own = m_tot
        sels = []
        for b in range(n_steps):
            rank = jnp.zeros((SUBLANES, LANES), F32)
            for c in range(n_steps):
                if c == b:
                    continue
                ahead = (scores[c] > scores[b]) | ((scores[c] == scores[b]) & (c < b))
                rank = rank + jnp.where(ahead, 1.0, 0.0)
            sels.append(rank < MOBA_TOPK)
            m_tot = jnp.where(sels[b], jnp.maximum(m_tot, m_sc[b]), m_tot)
        w_own = jnp.exp(s_own - m_tot)
        l_tot = w_own
        acc = w_own[:, 0:1] * newm[1:2]
        for b in range(n_steps):
            wgt = jnp.where(sels[b], jnp.exp(m_sc[b] - m_tot), 0.0)
            l_tot = l_tot + wgt * l_sc[b]
            acc = acc + wgt[:, 0:1] * acc_sc[b]
        o_full = acc / jnp.maximum(l_tot[:, 0:1], TINY)
        own_head = (_iota((SUBLANES, hw), 1) // HEAD_DIM) == _iota((SUBLANES, hw), 0)
        om_ref[0] = jnp.sum(jnp.where(own_head, o_full, 0.0), axis=0, keepdims=True)


def _dec_attention(cache_moba, cache_nsa, win_buf, pt_flat, qbd, q8n, q8r, g8, newm, news, neww, kc, vc, cover,
                   *, n_req, n_pages):
    _, page, mw = cache_moba.shape
    n_steps = n_pages // 2
    gw = G_NSA * HEAD_DIM
    hw = H_MOBA * HEAD_DIM
    n_chunk = kc.shape[1]
    nwin = win_buf.shape[1]
    req = lambda a: pl.BlockSpec((1,) + a.shape[1:], lambda b, j, pt: (b,) + (0,) * (a.ndim - 1))
    return pl.pallas_call(
        functools.partial(_dec_kernel, n_steps=n_steps, page=page),
        grid_spec=pltpu.PrefetchScalarGridSpec(
            num_scalar_prefetch=1,
            grid=(n_req, n_steps),
            in_specs=[pl.BlockSpec((1, page, mw), lambda b, j, pt: (pt[b * n_pages + 2 * j], 0, 0)),
                      pl.BlockSpec((1, page, mw), lambda b, j, pt: (pt[b * n_pages + 2 * j + 1], 0, 0)),
                      pl.BlockSpec((1, page, 2 * gw), lambda b, j, pt: (pt[b * n_pages + 2 * j], 0, 1)),
                      pl.BlockSpec((1, page, 2 * gw), lambda b, j, pt: (pt[b * n_pages + 2 * j + 1], 0, 1)),
                      req(win_buf), req(qbd), req(q8n), req(q8r), req(g8), req(newm), req(news), req(neww),
                      req(kc), req(vc), pl.BlockSpec(cover.shape, lambda b, j, pt: (0, 0))],
            out_specs=[pl.BlockSpec((1, 1, hw), lambda b, j, pt: (b, 0, 0)),
                       pl.BlockSpec((1, SUBLANES, LANES), lambda b, j, pt: (b, 0, 0))],
            scratch_shapes=[pltpu.VMEM((n_steps, SUBLANES, LANES), F32)] * 3
            + [pltpu.VMEM((n_steps, SUBLANES, hw), F32)]
            + [pltpu.VMEM((SUBLANES, LANES), F32)] * 5),
        out_shape=[jax.ShapeDtypeStruct((n_req, 1, hw), F32), jax.ShapeDtypeStruct((n_req, SUBLANES, LANES), F32)],
        compiler_params=_params("parallel", "arbitrary"),
        name="dec_attention",
    )(pt_flat, cache_moba, cache_moba, cache_nsa, cache_nsa, win_buf, qbd, q8n, q8r, g8, newm, news, neww, kc, vc,
      cover)


def _lane_rep(col):
    return jnp.broadcast_to(col, (col.shape[0], LANES))


def _head_sums(prod_row):
    w = prod_row.shape[1]
    own = (_iota((SUBLANES, w), 1) // HEAD_DIM) == _iota((SUBLANES, w), 0)
    return _lane_rep(jnp.sum(jnp.where(own, jnp.broadcast_to(prod_row, (SUBLANES, w)), 0.0), axis=1, keepdims=True))


def _pair_row(x8, h):
    return jnp.where(_iota((1, LANES), 1) < HEAD_DIM, x8[h:h + 1, :], x8[h + 1:h + 2, :])


def _cols_to_row(acc_a, acc_b):
    return jnp.sum(jnp.concatenate([acc_a, acc_b], axis=0).T, axis=0, keepdims=True)


def _dec2_kernel(pt_ref, cm_ref, cn_ref, wb_ref, q_ref, q8n_ref, new_ref, gate_ref, wcat_ref, pecat_ref, w2_ref,
                 gk_ref, bd_ref, cover_ref, o_ref,
                 mbuf, nbuf, sem, xk, xv, qmb, qrb, s_sc, p_sc, *, n_pages, page):
    b = pl.program_id(0)
    n_req = pl.num_programs(0)
    slot = b % 2
    hw = H_MOBA * HEAD_DIM
    gw = G_NSA * HEAD_DIM
    past = n_pages * page

    def copies(req, sl):
        out = []
        for p in range(n_pages):
            pg = pt_ref[req * n_pages + p]
            out.append(pltpu.make_async_copy(cm_ref.at[pg], mbuf.at[sl, p], sem.at[0, sl]))
            out.append(pltpu.make_async_copy(cn_ref.at[pg], nbuf.at[sl, p], sem.at[1, sl]))
        return out

    @pl.when(b == 0)
    def _():
        for c in copies(0, 0):
            c.start()

    for c in copies(b, slot):
        c.wait()

    @pl.when(b + 1 < n_req)
    def _():
        for c in copies(b + 1, 1 - slot):
            c.start()

    lane1 = _iota((1, LANES), 1)
    lane8 = _iota((SUBLANES, LANES), 1)
    qrow = q_ref[0]
    new = new_ref[0]
    for c in range(hw // LANES):
        cols = slice(c * LANES, (c + 1) * LANES)
        qmb[cols, :] = jnp.broadcast_to(qrow[0:1, cols] * SCALE, (LANES, LANES)).T
        qrb[cols, :] = jnp.broadcast_to(qrow[1:2, cols] * SCALE, (LANES, LANES)).T

    def softmax_pv(scores, extra8, s_new8, v_rows, vbuf_ref, v_row0, per_g):
        del extra8
        m8 = s_new8
        for s in scores:
            m8 = jnp.maximum(m8, _lane_rep(jnp.max(s, axis=1, keepdims=True)))
        w_new = jnp.exp(s_new8 - m8)
        l8 = w_new
        for p, s in enumerate(scores):
            pr = jnp.exp(s - m8)
            p_sc[p] = pr
            l8 = l8 + _lane_rep(jnp.sum(pr, axis=1, keepdims=True))
        inv8 = 1.0 / jnp.maximum(l8, TINY)
        rows = []
        for hp in range(SUBLANES // 2):
            accs = []
            for h in (2 * hp, 2 * hp + 1):
                r0 = v_row0 + (h // R_NSA if per_g else h) * HEAD_DIM

                def body(p, acc, h=h, r0=r0):
                    return acc + vbuf_ref[slot, p, r0:r0 + HEAD_DIM, :] * p_sc[p, h:h + 1, :]
                accs.append(lax.fori_loop(0, n_pages, body, jnp.zeros((HEAD_DIM, LANES), F32)))
            row = _cols_to_row(accs[0], accs[1])
            cols = slice(hp * LANES, (hp + 1) * LANES)
            rows.append((row + _pair_row(w_new, 2 * hp) * v_rows[:, cols]) * _pair_row(inv8, 2 * hp))
        return rows

    def moba_scores(p, carry):
        rows = [jnp.sum(mbuf[slot, p, h * HEAD_DIM:(h + 1) * HEAD_DIM, :] * qmb[h * HEAD_DIM:(h + 1) * HEAD_DIM, :],
                        axis=0, keepdims=True) for h in range(H_MOBA)]
        s_sc[p] = jnp.concatenate(rows, axis=0)
        return carry
    lax.fori_loop(0, n_pages, moba_scores, 0)
    s_all = [s_sc[p] for p in range(n_pages)]
    ppb = MOBA_BLOCK // page
    n_blk = n_pages // ppb
    bsc = []
    for j in range(n_blk):
        tot = s_all[j * ppb]
        for t in range(1, ppb):
            tot = tot + s_all[j * ppb + t]
        bsc.append(_lane_rep(jnp.sum(tot, axis=1, keepdims=True)))
    masked = []
    for j in range(n_blk):
        rank = jnp.zeros((SUBLANES, LANES), F32)
        for c in range(n_blk):
            if c != j:
                ahead = (bsc[c] > bsc[j]) | ((bsc[c] == bsc[j]) & (c < j))
                rank = rank + jnp.where(ahead, 1.0, 0.0)
        for t in range(ppb):
            masked.append(jnp.where(rank < MOBA_TOPK, s_all[j * ppb + t], NEG_BIG))
    s_own = _head_sums(qrow[0:1, :] * new[0:1, :]) * SCALE
    o_rows = softmax_pv(masked, None, s_own, new[1:2, :], mbuf, hw, False)

    for p in range(n_pages):
        xk[p * page:(p + 1) * page, :] = nbuf[slot, p, 0:gw, :].T
        xv[p * page:(p + 1) * page, :] = nbuf[slot, p, gw:2 * gw, :].T
    n_chunk = past // CMP_STRIDE
    cmp_out = []
    for kv, xref in enumerate((xk, xv)):
        xcat = jnp.concatenate([xref[pl.ds(j, n_chunk, stride=CMP_STRIDE), :] for j in range(CMP_STRIDE)], axis=1)
        pe2 = _dot(pecat_ref[kv].astype(BF16), wcat_ref[kv])
        ab = _dot(xcat.astype(BF16), wcat_ref[kv])
        hid = jax.nn.gelu(ab[:, 0:gw] + pe2[0:1, 0:gw] + pltpu.roll(ab[:, gw:2 * gw] + pe2[1:2, gw:2 * gw],
                                                                    n_chunk - 1, 0))
        cmp_out.append(_dot(hid, w2_ref[kv], HIGHEST))
    kc = _head_norm(cmp_out[0], gk_ref[...], bd_ref[...])
    vc = cmp_out[1]

    s = _dot_nt(q8n_ref[0], kc, HIGHEST) * SCALE
    cmask = _iota((SUBLANES, n_chunk), 1) < (n_chunk - 1)
    s = jnp.where(cmask, s, NEG_BIG)
    e = jnp.where(cmask, jnp.exp(s - jnp.max(s, axis=1, keepdims=True)), 0.0)
    p_cmp = e / jnp.maximum(jnp.sum(e, axis=1, keepdims=True), TINY)
    o_cmp8 = _dot(p_cmp.astype(BF16), vc.astype(BF16))
    subc = _iota((SUBLANES, n_chunk), 0)
    g0 = jnp.sum(jnp.where(subc < R_NSA, p_cmp, 0.0), axis=0, keepdims=True)
    g1 = jnp.sum(jnp.where(subc >= R_NSA, p_cmp, 0.0), axis=0, keepdims=True)
    p_grp = jnp.concatenate([jnp.where(subc < R_NSA, g0, g1), jnp.zeros((LANES - SUBLANES, n_chunk), F32)], axis=0)
    imp_t = _dot_nt(cover_ref[...], p_grp, HIGHEST)
    blk_t = _iota((LANES, LANES), 0)
    own = past // SEL_BLOCK
    forced = (blk_t == 0) | (blk_t == own) | (blk_t == own - 1)
    score = jnp.where(blk_t <= own, jnp.where(forced, jnp.inf, imp_t), -jnp.inf)
    picked = _top_k_sublanes(score, SEL_TOPN)
    bias8 = jnp.where((picked > 0.5) & (blk_t <= own), 0.0, NEG_BIG).T[0:SUBLANES, :]
    cmp_rows = []
    for hp in range(H_NSA // 2):
        g = (2 * hp) // R_NSA
        ra = o_cmp8[2 * hp:2 * hp + 1, :]
        rb = o_cmp8[2 * hp + 1:2 * hp + 2, :]
        cmp_rows.append(jnp.where(lane1 < HEAD_DIM, ra if g == 0 else pltpu.roll(ra, HEAD_DIM, 1),
                                  rb if g == 1 else pltpu.roll(rb, HEAD_DIM, 1)))

    def sel_scores(p, carry):
        rows = [jnp.sum(nbuf[slot, p, 2 * gw + (i // R_NSA) * HEAD_DIM:2 * gw + (i // R_NSA + 1) * HEAD_DIM, :]
                        * qrb[i * HEAD_DIM:(i + 1) * HEAD_DIM, :], axis=0, keepdims=True) for i in range(H_NSA)]
        s_sc[p] = jnp.concatenate(rows, axis=0)
        return carry
    lax.fori_loop(0, n_pages, sel_scores, 0)
    bpp = page // SEL_BLOCK
    sel_s = []
    for p in range(n_pages):
        bias_p = bias8[:, p * bpp:p * bpp + 1]
        for t in range(1, bpp):
            bias_p = jnp.where(lane8 < t * SEL_BLOCK, bias_p, bias8[:, p * bpp + t:p * bpp + t + 1])
        sel_s.append(s_sc[p] + bias_p)
    s_new = _head_sums(qrow[1:2, :] * new[2:3, :]) * SCALE
    sel_rows = softmax_pv(sel_s, None, s_new, new[3:4, :], nbuf, 3 * gw, True)

    nw = wb_ref.shape[2]
    wk = nw // LANES
    w_s = []
    for i in range(H_NSA):
        g = i // R_NSA
        qcol = jnp.concatenate([qrb[i * HEAD_DIM:(i + 1) * HEAD_DIM, :]] * wk, axis=1)
        w_s.append(jnp.sum(wb_ref[0, g * HEAD_DIM:(g + 1) * HEAD_DIM, :] * qcol, axis=0, keepdims=True))
    s = jnp.concatenate(w_s, axis=0)
    wmask = _iota((SUBLANES, nw), 1) >= 1
    s = jnp.where(wmask, s, NEG_BIG)
    s_new = _head_sums(qrow[1:2, :] * new[4:5, :]) * SCALE
    m8 = jnp.maximum(_lane_rep(jnp.max(s, axis=1, keepdims=True)), s_new)
    pw = jnp.where(wmask, jnp.exp(s - m8[:, 0:1]), 0.0)
    w_new = jnp.exp(s_new - m8)
    inv8 = 1.0 / jnp.maximum(_lane_rep(jnp.sum(pw, axis=1, keepdims=True)) + w_new, TINY)
    win_rows = []
    for hp in range(H_NSA // 2):
        accs = []
        for i in (2 * hp, 2 * hp + 1):
            g = i // R_NSA
            prod = wb_ref[0, gw + g * HEAD_DIM:gw + (g + 1) * HEAD_DIM, :] * pw[i:i + 1, :]
            acc = prod[:, 0:LANES]
            for c in range(1, wk):
                acc = acc + prod[:, c * LANES:(c + 1) * LANES]
            accs.append(acc)
        cols = slice(hp * LANES, (hp + 1) * LANES)
        win_rows.append((_cols_to_row(accs[0], accs[1]) + _pair_row(w_new, 2 * hp) * new[5:6, cols])
                        * _pair_row(inv8, 2 * hp))

    gates = gate_ref[0]
    for c in range(hw // LANES):
        o_ref[0, :, c * LANES:(c + 1) * LANES] = o_rows[c]
    for c in range(W_QN // LANES):
        cols = slice(c * LANES, (c + 1) * LANES)
        o_ref[0, :, hw + c * LANES:hw + (c + 1) * LANES] = (
            gates[0:1, cols] * cmp_rows[c] + gates[1:2, cols] * sel_rows[c] + gates[2:3, cols] * win_rows[c])


def _dec2_attention(cm, cn, wb, pt_flat, qrows, q8n, new, gate_rows, wcat, pecat, w2bd, gk, bd, cover,
                    *, n_req, n_pages):
    page = cm.shape[2]
    n_chunk = n_pages * page // CMP_STRIDE
    req = lambda a: pl.BlockSpec((1,) + a.shape[1:], lambda b, pt: (b,) + (0,) * (a.ndim - 1))
    const = lambda a: pl.BlockSpec(a.shape, lambda b, pt: (0,) * a.ndim)
    return pl.pallas_call(
        functools.partial(_dec2_kernel, n_pages=n_pages, page=page),
        grid_spec=pltpu.PrefetchScalarGridSpec(
            num_scalar_prefetch=1,
            grid=(n_req,),
            in_specs=[pl.BlockSpec(memory_space=pl.ANY), pl.BlockSpec(memory_space=pl.ANY), req(wb), req(qrows),
                      req(q8n), req(new), req(gate_rows), const(wcat), const(pecat), const(w2bd), const(gk),
                      const(bd), const(cover)],
            out_specs=pl.BlockSpec((1, 1, W_QM + W_QN), lambda b, pt: (b, 0, 0)),
            scratch_shapes=[pltpu.VMEM((2, n_pages) + cm.shape[1:], F32), pltpu.VMEM((2, n_pages) + cn.shape[1:], F32),
                            pltpu.SemaphoreType.DMA((2, 2)),
                            pltpu.VMEM((n_pages * page, LANES), F32), pltpu.VMEM((n_pages * page, LANES), F32),
                            pltpu.VMEM((W_QM, LANES), F32), pltpu.VMEM((W_QN, LANES), F32),
                            pltpu.VMEM((n_pages, SUBLANES, LANES), F32), pltpu.VMEM((n_pages, SUBLANES, LANES), F32)]),
        out_shape=jax.ShapeDtypeStruct((n_req, 1, W_QM + W_QN), F32),
        compiler_params=_params("arbitrary"),
        name="dec_attention",
    )(pt_flat, cm, cn, wb, qrows, q8n, new, gate_rows, wcat, pecat, w2bd, gk, bd, cover)


def _compress_cat_consts(cmp_pos, cmp_w1):
    eye = jnp.eye(G_NSA, dtype=F32)
    w1 = cmp_w1.reshape(2, 2, CMP_STRIDE, HEAD_DIM, HEAD_DIM)
    wcat = jnp.einsum("gh,kajde->kjgdahe", eye, w1).reshape(2, CMP_STRIDE * G_NSA * HEAD_DIM, 2 * G_NSA * HEAD_DIM)
    pe = jnp.tile(cmp_pos.reshape(2, 2, CMP_STRIDE, 1, HEAD_DIM), (1, 1, 1, G_NSA, 1))
    pecat = jnp.pad(pe.reshape(2, 2, CMP_STRIDE * G_NSA * HEAD_DIM), ((0, 0), (0, SUBLANES - 2), (0, 0)))
    return wcat.astype(BF16), pecat


def _post_kernel(o_ref, x_ref, gt_ref, sh_ref, sc_ref, g_ref, wo_ref, wr_ref, br_ref,
                 y_ref, h3_ref, te_ref, tw_ref):
    y = x_ref[...] + gt_ref[0] * _dot(o_ref[...], wo_ref[...])
    y_ref[...] = y
    h = y * lax.rsqrt(jnp.mean(y * y, axis=1, keepdims=True) + NORM_EPS) * g_ref[...]
    h = h * (1.0 + sc_ref[0]) + sh_ref[0]
    tm = h.shape[0]
    for s in range(h.shape[1] // LANES):
        h3_ref[pl.ds(s, tm, stride=SUBLANES), :] = h[:, s * LANES:(s + 1) * LANES]
    logits = _dot(h, wr_ref[...], HIGHEST) + br_ref[...]
    _, vals, ids = _top_k_lanes(logits, TOP_K)
    lane = _iota((tm, LANES), 1)
    es = [jnp.exp(v - vals[0]) for v in vals]
    den = es[0]
    for e in es[1:]:
        den = den + e
    te = jnp.zeros((tm, LANES), F32)
    tw = jnp.zeros((tm, LANES), F32)
    for k in range(TOP_K):
        te = jnp.where(lane == k, ids[k], te)
        tw = jnp.where(lane == k, es[k] / den, tw)
    te_ref[...] = te.astype(jnp.int32)
    tw_ref[...] = tw


def _post(o, x, gate, shift, scale, g2, wo_bf, wr_pad, br_pad, *, tm, rows_per_mod):
    n, d = x.shape
    mod_r = gate.shape[1]
    mod_map = lambda i: (i // (rows_per_mod // tm), 0, 0)
    row = lambda w: pl.BlockSpec((tm, w), lambda i: (i, 0))
    const = lambda a: pl.BlockSpec(a.shape, lambda i: (0,) * a.ndim)
    mod = pl.BlockSpec((1, mod_r, d), mod_map)
    return pl.pallas_call(
        _post_kernel,
        grid=(n // tm,),
        in_specs=[row(d), row(d), mod, mod, mod, const(g2), const(wo_bf), const(wr_pad), const(br_pad)],
        out_specs=[row(d), pl.BlockSpec((tm * SUBLANES, LANES), lambda i: (i, 0)), row(LANES), row(LANES)],
        out_shape=[jax.ShapeDtypeStruct((n, d), F32), jax.ShapeDtypeStruct((n * SUBLANES, LANES), F32),
                   jax.ShapeDtypeStruct((n, LANES), jnp.int32), jax.ShapeDtypeStruct((n, LANES), F32)],
        compiler_params=_params("parallel"),
        name="post",
    )(o, x, gate, shift, scale, g2, wo_bf, wr_pad, br_pad)


def _expert_kernel(be_ref, na_ref, tok_ref, h3_ref, wgu_ref, bgu_ref, wd_ref, bd_ref, y_ref, buf, xb, sem):
    i = pl.program_id(0)
    n_active = na_ref[0]
    rows8 = MOE_ROWS * SUBLANES

    def gather(blk, slot):
        def body(r, carry):
            t = tok_ref[blk * MOE_ROWS + r]
            pltpu.make_async_copy(h3_ref.at[pl.ds(pl.multiple_of(t * SUBLANES, SUBLANES), SUBLANES), :],
                                  buf.at[slot, pl.ds(pl.multiple_of(r * SUBLANES, SUBLANES), SUBLANES), :],
                                  sem.at[slot]).start()
            return carry
        lax.fori_loop(0, MOE_ROWS, body, 0)

    @pl.when(i == 0)
    def _():
        gather(0, 0)

    @pl.when(i < n_active)
    def _():
        slot = i % 2
        pltpu.make_async_copy(h3_ref.at[pl.ds(0, rows8), :], buf.at[slot], sem.at[slot]).wait()

        @pl.when(i + 1 < n_active)
        def _():
            gather(i + 1, 1 - slot)

        d = xb.shape[1]
        for s in range(d // LANES):
            xb[:, s * LANES:(s + 1) * LANES] = buf[slot, pl.ds(s, MOE_ROWS, stride=SUBLANES), :].astype(BF16)
        gu = _dot(xb[...], wgu_ref[0]) + bgu_ref[0]
        f = gu.shape[1] // 2
        gt = jnp.minimum(gu[:, 0:f], SWIGLU_LIMIT)
        up = jnp.clip(gu[:, f:2 * f], -SWIGLU_LIMIT, SWIGLU_LIMIT)
        act = (up + 1.0) * (gt * jax.nn.sigmoid(SWIGLU_ALPHA * gt))
        y = _dot(act.astype(BF16), wd_ref[0]) + bd_ref[0]
        for s in range(d // LANES):
            y_ref[pl.ds(s, MOE_ROWS, stride=SUBLANES), :] = y[:, s * LANES:(s + 1) * LANES]

    @pl.when(i >= n_active)
    def _():
        y_ref[...] = jnp.zeros(y_ref.shape, F32)


def _experts(blk_e, n_active, tok, h3, wgu_bf, b_gu, wd_bf, b_down):
    n_blocks = blk_e.shape[0]
    e, d, f2 = wgu_bf.shape
    rows8 = MOE_ROWS * SUBLANES
    return pl.pallas_call(
        _expert_kernel,
        grid_spec=pltpu.PrefetchScalarGridSpec(
            num_scalar_prefetch=3,
            grid=(n_blocks,),
            in_specs=[pl.BlockSpec(memory_space=pl.ANY),
                      pl.BlockSpec((1, d, f2), lambda i, be, na, tok: (be[i], 0, 0)),
                      pl.BlockSpec((1, 1, f2), lambda i, be, na, tok: (be[i], 0, 0)),
                      pl.BlockSpec((1, f2 // 2, d), lambda i, be, na, tok: (be[i], 0, 0)),
                      pl.BlockSpec((1, 1, d), lambda i, be, na, tok: (be[i], 0, 0))],
            out_specs=pl.BlockSpec((rows8, LANES), lambda i, be, na, tok: (i, 0)),
            scratch_shapes=[pltpu.VMEM((2, rows8, LANES), F32), pltpu.VMEM((MOE_ROWS, d), BF16),
                            pltpu.SemaphoreType.DMA((2,))]),
        out_shape=jax.ShapeDtypeStruct((n_blocks * rows8, LANES), F32),
        compiler_params=_params("arbitrary"),
        name="experts",
    )(blk_e, n_active, tok, h3, wgu_bf, b_gu.reshape(e, 1, f2), wd_bf, b_down.reshape(e, 1, d))


def _dispatch_kernel(pos_ref, h3_ref, xs_in_ref, xs_ref, sem, *, tm, tile_off):
    del xs_in_ref
    i = pl.program_id(0)

    def body(r, carry):
        src = h3_ref.at[pl.ds(pl.multiple_of(r * SUBLANES, SUBLANES), SUBLANES), :]
        for k in range(TOP_K):
            p = pos_ref[((i + tile_off) * tm + r) * TOP_K + k]
            pltpu.make_async_copy(src, xs_ref.at[pl.ds(pl.multiple_of(p * SUBLANES, SUBLANES), SUBLANES), :],
                                  sem.at[0]).start()
        return carry
    lax.fori_loop(0, tm, body, 0)
    for k in range(TOP_K):
        pltpu.make_async_copy(h3_ref, xs_ref.at[pl.ds(0, tm * SUBLANES), :], sem.at[0]).wait()


def _dispatch(pos_flat, h3, xs, *, tm, tile_off):
    n8 = h3.shape[0]
    return pl.pallas_call(
        functools.partial(_dispatch_kernel, tm=tm, tile_off=tile_off),
        grid_spec=pltpu.PrefetchScalarGridSpec(
            num_scalar_prefetch=1,
            grid=(n8 // (tm * SUBLANES),),
            in_specs=[pl.BlockSpec((tm * SUBLANES, LANES), lambda i, pos: (i, 0)), pl.BlockSpec(memory_space=pl.ANY)],
            out_specs=pl.BlockSpec(memory_space=pl.ANY),
            scratch_shapes=[pltpu.SemaphoreType.DMA((1,))]),
        out_shape=jax.ShapeDtypeStruct(xs.shape, xs.dtype),
        input_output_aliases={2: 0},
        compiler_params=_params("arbitrary"),
        name="dispatch",
    )(pos_flat, h3, xs)


def _expert_block_kernel(be_ref, na_ref, x_ref, wgu_ref, bgu_ref, wd_ref, bd_ref, y_ref, xb, wgu_bf, wd_bf):
    i = pl.program_id(0)
    active = i < na_ref[0]

    @pl.when(active & ((i == 0) | (be_ref[i] != be_ref[jnp.maximum(i - 1, 0)])))
    def _():
        for r in range(0, wgu_bf.shape[0], LANES):
            wgu_bf[r:r + LANES, :] = wgu_ref[0, r:r + LANES, :].astype(BF16)
        for r in range(0, wd_bf.shape[0], LANES):
            wd_bf[r:r + LANES, :] = wd_ref[0, r:r + LANES, :].astype(BF16)

    @pl.when(active)
    def _():
        d = xb.shape[1]
        for s in range(d // LANES):
            xb[:, s * LANES:(s + 1) * LANES] = x_ref[pl.ds(s, MOE_ROWS, stride=SUBLANES), :].astype(BF16)
        gu = _dot(xb[...], wgu_bf[...]) + bgu_ref[0]
        f = gu.shape[1] // 2
        gt = jnp.minimum(gu[:, 0:f], SWIGLU_LIMIT)
        up = jnp.clip(gu[:, f:2 * f], -SWIGLU_LIMIT, SWIGLU_LIMIT)
        act = (up + 1.0) * (gt * jax.nn.sigmoid(SWIGLU_ALPHA * gt))
        y = _dot(act.astype(BF16), wd_bf[...]) + bd_ref[0]
        for s in range(d // LANES):
            y_ref[pl.ds(s, MOE_ROWS, stride=SUBLANES), :] = y[:, s * LANES:(s + 1) * LANES]

    @pl.when(i >= na_ref[0])
    def _():
        y_ref[...] = jnp.zeros(y_ref.shape, F32)


def _expert_blocks(blk_e, n_active, xs, w_gu, b_gu, w_down, b_down):
    n_blocks = blk_e.shape[0]
    e, d, f2 = w_gu.shape
    rows8 = MOE_ROWS * SUBLANES
    row_map = lambda i, be, na: (jnp.minimum(i, na[0] - 1), 0)
    return pl.pallas_call(
        _expert_block_kernel,
        grid_spec=pltpu.PrefetchScalarGridSpec(
            num_scalar_prefetch=2,
            grid=(n_blocks,),
            in_specs=[pl.BlockSpec((rows8, LANES), row_map),
                      pl.BlockSpec((1, d, f2), lambda i, be, na: (be[i], 0, 0)),
                      pl.BlockSpec((1, 1, f2), lambda i, be, na: (be[i], 0, 0)),
                      pl.BlockSpec((1, f2 // 2, d), lambda i, be, na: (be[i], 0, 0)),
                      pl.BlockSpec((1, 1, d), lambda i, be, na: (be[i], 0, 0))],
            out_specs=pl.BlockSpec((rows8, LANES), lambda i, be, na: (i, 0)),
            scratch_shapes=[pltpu.VMEM((MOE_ROWS, d), BF16), pltpu.VMEM((d, f2), BF16),
                            pltpu.VMEM((f2 // 2, d), BF16)]),
        out_shape=jax.ShapeDtypeStruct((n_blocks * rows8, LANES), F32),
        compiler_params=_params("arbitrary"),
        name="experts",
    )(blk_e, n_active, xs, w_gu, b_gu.reshape(e, 1, f2), w_down, b_down.reshape(e, 1, d))


def _routing_pos(top_e, n_tok):
    n_assign = n_tok * TOP_K
    n_pad = -(-n_assign // LANES) * LANES
    e_flat = jnp.pad(top_e.reshape(-1), (0, n_pad - n_assign), constant_values=N_EXPERTS)
    onehot = (e_flat[:, None] == jnp.arange(N_EXPERTS)[None, :]).astype(F32).reshape(n_pad // LANES, LANES, N_EXPERTS)
    tril = jnp.tril(jnp.ones((LANES, LANES), F32))
    within = jnp.einsum("ij,bjk->bik", tril, onehot)
    block_tot = within[:, -1, :]
    offs = jnp.cumsum(block_tot, axis=0) - block_tot
    counts = jnp.sum(block_tot, axis=0).astype(jnp.int32)
    padded = (counts + MOE_ROWS - 1) // MOE_ROWS * MOE_ROWS
    pad_end = jnp.cumsum(padded)
    pad_start = (pad_end - padded).astype(F32)
    slot = jnp.sum(onehot * (within + offs[:, None, :] - 1.0 + pad_start[None, None, :]), axis=-1)
    pos = slot.reshape(-1)[:n_assign].astype(jnp.int32)
    n_blocks = -(-n_assign // MOE_ROWS) + N_EXPERTS
    starts = jnp.arange(n_blocks, dtype=jnp.int32) * MOE_ROWS
    blk_e = jnp.minimum(jnp.sum((pad_end[None, :] <= starts[:, None]).astype(jnp.int32), axis=1), N_EXPERTS - 1)
    n_active = (pad_end[-1] // MOE_ROWS).astype(jnp.int32).reshape(1)
    return blk_e, n_active, pos


def _combine_kernel(pos_ref, ys_ref, y1_ref, gt_ref, tw_ref, o_ref, buf, sem, *, tile_off):
    i = pl.program_id(0)
    n = pl.num_programs(0)
    tm = y1_ref.shape[0]
    rows8 = tm * SUBLANES

    def gather(tile, slot):
        def body(r, carry):
            for k in range(TOP_K):
                p = pos_ref[((tile + tile_off) * tm + r) * TOP_K + k]
                pltpu.make_async_copy(ys_ref.at[pl.ds(pl.multiple_of(p * SUBLANES, SUBLANES), SUBLANES), :],
                                      buf.at[slot, k, pl.ds(pl.multiple_of(r * SUBLANES, SUBLANES), SUBLANES), :],
                                      sem.at[slot]).start()
            return carry
        lax.fori_loop(0, tm, body, 0)

    @pl.when(i == 0)
    def _():
        gather(0, 0)

    slot = i % 2
    for k in range(TOP_K):
        pltpu.make_async_copy(ys_ref.at[pl.ds(0, rows8), :], buf.at[slot, k], sem.at[slot]).wait()

    @pl.when(i + 1 < n)
    def _():
        gather(i + 1, 1 - slot)

    tw = tw_ref[...]
    wk = [jnp.broadcast_to(tw[:, k:k + 1], (tm, LANES)) for k in range(TOP_K)]
    gt = gt_ref[0]
    for s in range(o_ref.shape[1] // LANES):
        moe = wk[0] * buf[slot, 0, pl.ds(s, tm, stride=SUBLANES), :]
        for k in range(1, TOP_K):
            moe = moe + wk[k] * buf[slot, k, pl.ds(s, tm, stride=SUBLANES), :]
        cols = slice(s * LANES, (s + 1) * LANES)
        o_ref[:, cols] = y1_ref[:, cols] + gt[:, cols] * moe


def _combine(pos_flat, ys, y1, gate, tw, *, tm, rows_per_mod, tile_off):
    n, d = y1.shape
    mod_r = gate.shape[1]
    return pl.pallas_call(
        functools.partial(_combine_kernel, tile_off=tile_off),
        grid_spec=pltpu.PrefetchScalarGridSpec(
            num_scalar_prefetch=1,
            grid=(n // tm,),
            in_specs=[pl.BlockSpec(memory_space=pl.ANY),
                      pl.BlockSpec((tm, d), lambda i, pos: (i, 0)),
                      pl.BlockSpec((1, mod_r, d), lambda i, pos: (i // (rows_per_mod // tm), 0, 0)),
                      pl.BlockSpec((tm, LANES), lambda i, pos: (i, 0))],
            out_specs=pl.BlockSpec((tm, d), lambda i, pos: (i, 0)),
            scratch_shapes=[pltpu.VMEM((2, TOP_K, tm * SUBLANES, LANES), F32), pltpu.SemaphoreType.DMA((2,))]),
        out_shape=jax.ShapeDtypeStruct((n, d), F32),
        compiler_params=_params("arbitrary"),
        name="combine",
    )(pos_flat, ys, y1, gate, tw)


def _rope_tables(pos):
    half = HEAD_DIM // 2
    inv = ROPE_THETA ** (-jnp.arange(half, dtype=F32) / half)
    ang = pos.astype(F32)[:, None] * inv[None, :]
    cos = jnp.cos(ang)
    sin = jnp.sin(ang)
    reps = LANES // HEAD_DIM
    return (jnp.tile(jnp.concatenate([cos, cos], axis=1), (1, reps)),
            jnp.tile(jnp.concatenate([-sin, sin], axis=1), (1, reps)))


def _routing(top_e, n_tok):
    n_assign = n_tok * TOP_K
    e_flat = top_e.reshape(-1)
    order = jnp.argsort(e_flat)
    e_sorted = e_flat[order]
    tok_sorted = (order // TOP_K).astype(jnp.int32)
    counts = jnp.bincount(e_flat, length=N_EXPERTS)
    padded = (counts + MOE_ROWS - 1) // MOE_ROWS * MOE_ROWS
    pad_end = jnp.cumsum(padded)
    pad_start = pad_end - padded
    sort_start = jnp.cumsum(counts) - counts
    dest = (pad_start[e_sorted] + jnp.arange(n_assign, dtype=jnp.int32) - sort_start[e_sorted]).astype(jnp.int32)
    n_blocks = -(-n_assign // MOE_ROWS) + N_EXPERTS
    tok = jnp.full((n_blocks * MOE_ROWS,), n_tok, jnp.int32).at[dest].set(tok_sorted)
    blk_e = jnp.minimum(jnp.searchsorted(pad_end, jnp.arange(n_blocks, dtype=jnp.int32) * MOE_ROWS, side='right'),
                        N_EXPERTS - 1).astype(jnp.int32)
    pos = jnp.zeros((n_assign,), jnp.int32).at[order].set(dest)
    n_active = (pad_end[-1] // MOE_ROWS).astype(jnp.int32).reshape(1)
    return blk_e, n_active, tok, pos


def _ones_pad(a):
    return jnp.broadcast_to((jnp.arange(a.shape[-1]) == 0).astype(a.dtype), a.shape)


def _head_ones(a):
    return jnp.concatenate([a, _ones_pad(a)], axis=-1)


def _head_pad(a):
    return jnp.concatenate([a, jnp.zeros_like(a)], axis=-1)


def kernel(x_prompt, x_sample, c_prompt, c_sample, cache_moba_kv, cache_nsa_kv, state_nsa_win_kv, page_table,
           norm_g, w_ada, b_ada, w_in, qk_gain, cmp_pos, cmp_w1, cmp_w2, w_out, w_router, b_router, w_gu, b_gu,
           w_down, b_down):
    bsz, seq, d = x_prompt.shape
    n_req = x_sample.shape[0]
    depth = norm_g.shape[0]
    assert depth == 1 and x_sample.shape[1] == 1
    assert seq % ATT_TK == 0 and seq >= WINDOW + Q_TILE and seq // SEL_BLOCK <= LANES and seq // MOBA_BLOCK <= MOBA_BLOCK // SUBLANES
    n_pool, page = cache_moba_kv.shape[1], cache_moba_kv.shape[2]
    n_pages = page_table.shape[1]
    past = n_pages * page
    assert past % MOBA_BLOCK == 0 and 2 * page == MOBA_BLOCK and past // SEL_BLOCK < LANES
    assert state_nsa_win_kv.shape[2] == WINDOW
    layer = 0
    gw = G_NSA * HEAD_DIM
    n_prompt = bsz * seq

    bd = jnp.asarray(np.kron(np.eye(LANES // HEAD_DIM), np.full((HEAD_DIM, HEAD_DIM), 1.0 / HEAD_DIM)), BF16)
    w_in_bf = jnp.pad(w_in[layer], ((0, 0), (0, IN_COLS_PAD - IN_COLS))).astype(BF16)
    gains = jnp.tile(qk_gain[layer], (1, W_QM // HEAD_DIM))
    g1 = norm_g[layer, 0].reshape(1, d)
    g2 = norm_g[layer, 1].reshape(1, d)
    wo_bf = w_out[layer].astype(BF16)
    wr_pad = jnp.pad(w_router[layer], ((0, 0), (0, LANES - N_EXPERTS)))
    br_pad = jnp.pad(b_router[layer].reshape(1, N_EXPERTS), ((0, 0), (0, LANES - N_EXPERTS)),
                     constant_values=-jnp.inf)
    cmp_consts = _compress_consts(cmp_pos[layer], cmp_w1[layer], cmp_w2[layer], qk_gain[layer, 3])

    n_c = bsz + n_req
    n_c_pad = -(-n_c // SUBLANES) * SUBLANES
    c_all = jnp.pad(jnp.concatenate([c_prompt, c_sample], axis=0), ((0, n_c_pad - n_c), (0, 0)))
    mods = _ada(c_all, w_ada[layer], b_ada[layer])
    mods_p = [m.reshape(bsz, 1, d) for m in jnp.split(mods[:bsz], 6, axis=1)]
    mods_s = [m.reshape(1, n_req, d) for m in jnp.split(mods[bsz:n_c], 6, axis=1)]

    cos_p, sin_p = _rope_tables(jnp.arange(seq, dtype=jnp.int32))
    tm_p = 256
    (moba_t, nsa_t, win_t, q_m, qn, qr, _, kmean, cmp_raw, k_aug, v_aug, ks_aug, vs_aug, kw_pad, vw_aug,
     gates_g) = _proj(
        x_prompt.reshape(n_prompt, d), mods_p[0], mods_p[1], g1, w_in_bf, gains, bd, cos_p, sin_p,
        tm=tm_p, rows_per_mod=seq, pos_blocks=seq // tm_p, with_kmean=True)

    nbk = seq // MOBA_BLOCK
    kmean_h = kmean.reshape(bsz, nbk, H_MOBA, HEAD_DIM).transpose(0, 2, 1, 3)
    kmp = jnp.zeros((bsz, H_MOBA, LANES, LANES), F32).at[:, :, HEAD_DIM:HEAD_DIM + nbk, :HEAD_DIM].set(kmean_h)
    per_b = lambda a: a.reshape(bsz, seq, a.shape[-1])
    o_m = _moba_attention(per_b(q_m), kmp, per_b(k_aug), per_b(v_aug))

    kc, vc = _compress_prompt(per_b(cmp_raw), cmp_consts, bd)
    n_chunk = seq // CMP_STRIDE
    per_g = lambda a: a.reshape(bsz, -1, G_NSA, HEAD_DIM).transpose(0, 2, 1, 3)
    kcp = _head_pad(per_g(kc))
    vcd = _head_pad(per_g(vc)).astype(BF16)
    cover_p = _cover(n_chunk - 1, seq // SEL_BLOCK, n_chunk).T.astype(BF16)
    o_n = _nsa_attention(per_b(qn), per_b(qr), per_b(gates_g), kcp, vcd, cover_p, per_b(ks_aug), per_b(vs_aug),
                         per_b(kw_pad), per_b(vw_aug))
    o_p = jnp.concatenate([o_m, o_n], axis=-1).reshape(n_prompt, d)

    y1_p, h3_p, te_p, tw_p = _post(o_p, x_prompt.reshape(n_prompt, d), mods_p[2], mods_p[3], mods_p[4], g2, wo_bf,
                                   wr_pad, br_pad, tm=256, rows_per_mod=seq)

    cos_s, sin_s = _rope_tables(jnp.full((n_req,), past, jnp.int32))
    moba_new, nsa_new, win_new, q_m_s, qn_s, qr_s, gates_s = _proj(
        x_sample.reshape(n_req, d), mods_s[0], mods_s[1], g1, w_in_bf, gains, bd, cos_s, sin_s,
        tm=n_req, rows_per_mod=n_req, pos_blocks=1, with_kmean=False)
    pt_flat = page_table.reshape(-1).astype(jnp.int32)
    cache_m = cache_moba_kv[layer].transpose(0, 2, 3, 4, 1).reshape(n_pool, W_KVM, page)
    cache_n = cache_nsa_kv[layer].transpose(0, 2, 3, 4, 1).reshape(n_pool, 4 * gw, page)
    win_buf = state_nsa_win_kv[layer].transpose(0, 2, 3, 4, 1).reshape(n_req, 2 * gw, WINDOW)

    def rows8(q):
        qh = q.reshape(n_req, G_NSA, R_NSA, 1, HEAD_DIM)
        place = jnp.arange(G_NSA)[None, :, None, None, None] == jnp.arange(G_NSA)[None, None, None, :, None]
        return jnp.where(place, qh, 0.0).reshape(n_req, H_NSA, gw)

    def per_head(a):
        return jnp.repeat(a.reshape(n_req, G_NSA, HEAD_DIM), R_NSA, axis=1).reshape(n_req, W_QN)

    new_rows = jnp.stack([moba_new[:, :W_QM], moba_new[:, W_QM:], per_head(nsa_new[:, 2 * gw:3 * gw]),
                          per_head(nsa_new[:, 3 * gw:]), per_head(win_new[:, :gw]), per_head(win_new[:, gw:])], axis=1)
    gate_rows = jnp.repeat(gates_s[:, :N_GATE].reshape(n_req, H_NSA, 3).transpose(0, 2, 1), HEAD_DIM, axis=2)
    n_cmp_s = past // CMP_STRIDE
    cover_s = _cover(n_cmp_s - 1, past // SEL_BLOCK + 1, n_cmp_s).T
    wcat, pecat = _compress_cat_consts(cmp_pos[layer], cmp_w1[layer])
    o_s = _dec2_attention(cache_m, cache_n, win_buf, pt_flat, jnp.stack([q_m_s, qr_s], axis=1), rows8(qn_s), new_rows,
                          gate_rows, wcat, pecat, cmp_consts[2], cmp_consts[3], bd, cover_s,
                          n_req=n_req, n_pages=n_pages).reshape(n_req, d).astype(BF16)
    y1_s, h3_s, te_s, tw_s = _post(o_s, x_sample.reshape(n_req, d), mods_s[2], mods_s[3], mods_s[4], g2, wo_bf,
                                   wr_pad, br_pad, tm=n_req, rows_per_mod=n_req)

    n_tok = n_prompt + n_req
    top_e = jnp.concatenate([te_p[:, :TOP_K], te_s[:, :TOP_K]], axis=0)
    blk_e, n_active, pos_flat = _routing_pos(top_e, n_tok)
    tm_c = 128
    xs = jnp.zeros((blk_e.shape[0] * MOE_ROWS * SUBLANES, LANES), F32)
    xs = _dispatch(pos_flat, h3_p, xs, tm=2 * tm_c, tile_off=0)
    xs = _dispatch(pos_flat, h3_s, xs, tm=n_req, tile_off=n_prompt // n_req)
    ys = _expert_blocks(blk_e, n_active, xs, w_gu[layer], b_gu[layer], w_down[layer], b_down[layer])
    y_p = _combine(pos_flat, ys, y1_p, mods_p[5], tw_p, tm=tm_c, rows_per_mod=seq, tile_off=0)
    y_s = _combine(pos_flat, ys, y1_s, mods_s[5], tw_s, tm=n_req, rows_per_mod=n_req, tile_off=n_prompt // n_req)

    def rows_view(t, n_slot, n_head):
        return t.reshape(1, bsz, n_slot, n_head, HEAD_DIM, t.shape[-1]).transpose(0, 1, 5, 2, 3, 4)

    keep = min(WINDOW, seq)
    win_s = jnp.concatenate([state_nsa_win_kv[layer][:, 1:], win_new.reshape(n_req, 1, 2, G_NSA, HEAD_DIM)], axis=1)
    return (y_p.reshape(bsz, seq, d), y_s.reshape(n_req, 1, d),
            rows_view(moba_t, 2, H_MOBA), rows_view(nsa_t, 4, G_NSA), rows_view(win_t[:, :, seq - keep:], 2, G_NSA),
            moba_new.reshape(1, n_req, 1, 2, H_MOBA, HEAD_DIM),
            nsa_new.reshape(1, n_req, 1, 4, G_NSA, HEAD_DIM), win_s[None])
```

```python
import functools

import numpy as np
import jax
import jax.numpy as jnp
from jax import lax
from jax.experimental import pallas as pl
from jax.experimental.pallas import tpu as pltpu

F32 = jnp.float32
BF16 = jnp.bfloat16
HIGHEST = lax.Precision.HIGHEST

LANES = 128
SUBLANES = 8
HEAD_DIM = 64
H_MOBA = 8
H_NSA = 8
G_NSA = 2
R_NSA = H_NSA // G_NSA
MOBA_BLOCK = 256
MOBA_TOPK = 3
CMP_LEN = 32
CMP_STRIDE = 16
SEL_BLOCK = 64
SEL_TOPN = 16
WINDOW = 512
N_EXPERTS = 32
TOP_K = 4
SWIGLU_LIMIT = 7.0
SWIGLU_ALPHA = 1.702
ROPE_THETA = 10000.0
NORM_EPS = 1e-6
NEG_BIG = -1e30
TINY = 1e-30
SCALE = HEAD_DIM ** -0.5
SCALE_LOG2E = SCALE * 1.4426950408889634
Q_TILE = 256
MOBA_Q = 1024
MOBA_HEADS = 2
ATT_TK = 1024
MOE_ROWS = 256
VMEM_LIMIT = 56 * 1024 * 1024

W_QM = H_MOBA * HEAD_DIM
W_KVM = 2 * H_MOBA * HEAD_DIM
W_QN = H_NSA * HEAD_DIM
W_KVN = 6 * G_NSA * HEAD_DIM
N_GATE = 3 * H_NSA
IN_COLS = W_QM + W_KVM + W_QN + W_KVN + N_GATE
IN_COLS_PAD = W_QM + W_KVM + W_QN + W_KVN + LANES


def _iota(shape, dim):
    return lax.broadcasted_iota(jnp.int32, shape, dim)


def _dot(a, b, precision=None):
    return jnp.dot(a, b, preferred_element_type=F32, precision=precision)


def _dot_nt(a, b, precision=None):
    return lax.dot_general(a, b, (((1,), (1,)), ((), ())), preferred_element_type=F32, precision=precision)


def _split_bf16(a):
    hi = a.astype(BF16)
    return hi, (a - hi.astype(F32)).astype(BF16)


def _dot_nt_x3(a, b):
    ah, al = _split_bf16(a)
    bh, bl = _split_bf16(b)
    return _dot_nt(ah, bh) + _dot_nt(al, bh) + _dot_nt(ah, bl)


def _params(*sem):
    return pltpu.CompilerParams(dimension_semantics=sem, vmem_limit_bytes=VMEM_LIMIT)


def _seg_meansq(z, bd):
    zz = z * z
    hi = zz.astype(BF16)
    lo = (zz - hi.astype(F32)).astype(BF16)
    outs = []
    for c in range(z.shape[1] // LANES):
        sl = slice(c * LANES, (c + 1) * LANES)
        outs.append(_dot(hi[:, sl], bd) + _dot(lo[:, sl], bd))
    return outs[0] if len(outs) == 1 else jnp.concatenate(outs, axis=1)


def _head_norm(z, gain, bd):
    return z * lax.rsqrt(_seg_meansq(z, bd) + NORM_EPS) * gain


def _rope(z, cos, sin):
    outs = []
    first = (_iota((z.shape[0], LANES), 1) % HEAD_DIM) < (HEAD_DIM // 2)
    for c in range(z.shape[1] // LANES):
        x = z[:, c * LANES:(c + 1) * LANES]
        swapped = jnp.where(first, pltpu.roll(x, LANES - HEAD_DIM // 2, 1), pltpu.roll(x, HEAD_DIM // 2, 1))
        outs.append(x * cos + swapped * sin)
    return outs[0] if len(outs) == 1 else jnp.concatenate(outs, axis=1)


def _top_k_lanes(cur, k):
    lane = _iota(cur.shape, 1).astype(F32)
    picked = jnp.zeros(cur.shape, F32)
    vals, ids = [], []
    for _ in range(k):
        mx = jnp.max(cur, axis=1, keepdims=True)
        first = jnp.min(jnp.where(cur == mx, lane, 1e9), axis=1, keepdims=True)
        hit = lane == first
        picked = jnp.where(hit, 1.0, picked)
        cur = jnp.where(hit, -jnp.inf, cur)
        vals.append(mx)
        ids.append(first)
    return picked, vals, ids


def _top_k_sublanes(cur, k):
    idx = _iota(cur.shape, 0).astype(F32)
    picked = jnp.zeros(cur.shape, F32)
    for _ in range(k):
        mx = jnp.max(cur, axis=0, keepdims=True)
        first = jnp.min(jnp.where(cur == mx, idx, 1e9), axis=0, keepdims=True)
        hit = idx == first
        picked = jnp.where(hit, 1.0, picked)
        cur = jnp.where(hit, -jnp.inf, cur)
    return picked


def _flash_step(q, k, v, mask, m, acc):
    s = _dot_nt(q, k)
    if mask is not None:
        s = jnp.where(mask, s, NEG_BIG)
    m_new = jnp.maximum(m, jnp.max(s, axis=1, keepdims=True))
    acc_new = jnp.exp2(m - m_new) * acc + _dot(jnp.exp2(s - m_new).astype(BF16), v)
    return m_new, acc_new


def _flash_finish(acc):
    return acc / jnp.maximum(acc[:, HEAD_DIM:HEAD_DIM + 1], TINY)


def _ada_kernel(c_ref, w_ref, b_ref, o_ref):
    c = c_ref[...]
    o_ref[...] = _dot(c * jax.nn.sigmoid(c), w_ref[...], HIGHEST) + b_ref[...]


def _ada(c_all, w_ada, b_ada):
    n, d = c_all.shape
    cols = w_ada.shape[1]
    tn = 1024
    return pl.pallas_call(
        _ada_kernel,
        grid=(cols // tn,),
        in_specs=[pl.BlockSpec((n, d), lambda j: (0, 0)),
                  pl.BlockSpec((d, tn), lambda j: (0, j)),
                  pl.BlockSpec((1, tn), lambda j: (0, j))],
        out_specs=pl.BlockSpec((n, tn), lambda j: (0, j)),
        out_shape=jax.ShapeDtypeStruct((n, cols), F32),
        compiler_params=_params("arbitrary"),
        name="ada",
    )(c_all, w_ada, b_ada.reshape(1, cols))


def _proj_kernel(x_ref, sh_ref, sc_ref, g_ref, w_ref, gains_ref, bd_ref, cos_ref, sin_ref,
                 moba_ref, nsa_ref, win_ref, qm_ref, qn_ref, qr_ref, gate_ref, *attn_refs, with_kmean, pos_blocks):
    x = x_ref[...]
    y = x * lax.rsqrt(jnp.mean(x * x, axis=1, keepdims=True) + NORM_EPS) * g_ref[...]
    h = (y * (1.0 + sc_ref[0]) + sh_ref[0]).astype(BF16)
    bd = bd_ref[...]
    cos = cos_ref[...]
    sin = sin_ref[...]
    o = 0

    def seg(width):
        nonlocal o
        z = _dot(h, w_ref[:, o:o + width])
        o += width
        return z

    def gain(i, width):
        return gains_ref[i:i + 1, 0:width]

    qm_ref[...] = _rope(_head_norm(seg(W_QM), gain(0, W_QM), bd), cos, sin)
    k_m = _rope(_head_norm(seg(W_QM), gain(1, W_QM), bd), cos, sin)
    v_m = seg(W_QM)
    qn = _head_norm(seg(W_QN), gain(2, W_QN), bd)
    qn_ref[...] = qn
    qr_ref[...] = _rope(qn, cos, sin)
    gw = G_NSA * HEAD_DIM
    cmp_raw = seg(2 * gw)
    k_sel = _rope(_head_norm(seg(gw), gain(4, gw), bd), cos, sin)
    v_sel = seg(gw)
    k_win = _rope(_head_norm(seg(gw), gain(5, gw), bd), cos, sin)
    v_win = seg(gw)
    gates = jax.nn.sigmoid(seg(LANES))
    gate_ref[...] = gates
    moba_rows = jnp.concatenate([k_m, v_m], axis=1)
    nsa_rows = jnp.concatenate([cmp_raw, k_sel, v_sel], axis=1)
    win_rows = jnp.concatenate([k_win, v_win], axis=1)
    if not with_kmean:
        moba_ref[...] = moba_rows
        nsa_ref[...] = nsa_rows
        win_ref[...] = win_rows
    else:
        kmean_ref, cmpraw_ref, kaug_ref, vaug_ref, ksaug_ref, vsaug_ref, kwp_ref, vwaug_ref, gg_ref = attn_refs
        moba_ref[0] = moba_rows.T
        nsa_ref[0] = nsa_rows.T
        win_ref[0] = win_rows.T
        cmpraw_ref[...] = cmp_raw
        tm = k_m.shape[0]
        kmean_ref[0] = jnp.mean(k_m.reshape(tm // MOBA_BLOCK, MOBA_BLOCK, W_QM), axis=1)
        lane = _iota((tm, LANES), 1)
        posv = (pl.program_id(0) % pos_blocks) * tm + _iota((tm, LANES), 0)
        lo = lane < HEAD_DIM
        pad_blk = jnp.where((lane >= HEAD_DIM) & (lane - HEAD_DIM == posv // MOBA_BLOCK), 1.0, 0.0)
        pad_one = jnp.where(lane == HEAD_DIM, 1.0, 0.0)
        oh_sel = jnp.where(lane == posv // SEL_BLOCK, 1.0, 0.0).astype(BF16)

        def halves(x):
            return x, pltpu.roll(x, HEAD_DIM, 1)

        for c in range(W_QM // LANES):
            cols = slice(c * LANES, (c + 1) * LANES)
            for hh, (kh, vh) in enumerate(zip(halves(k_m[:, cols]), halves(v_m[:, cols]))):
                hcols = slice((2 * c + hh) * LANES, (2 * c + hh + 1) * LANES)
                kaug_ref[:, hcols] = jnp.where(lo, kh, pad_blk).astype(BF16)
                vaug_ref[:, hcols] = jnp.where(lo, vh, pad_one).astype(BF16)
        for g, (ks, vs, kw, vw) in enumerate(zip(halves(k_sel), halves(v_sel), halves(k_win), halves(v_win))):
            ksaug_ref[:, 2 * g * LANES:(2 * g + 1) * LANES] = oh_sel
            ksaug_ref[:, (2 * g + 1) * LANES:(2 * g + 2) * LANES] = jnp.where(lo, ks, 0.0).astype(BF16)
            gcols = slice(g * LANES, (g + 1) * LANES)
            vsaug_ref[:, gcols] = jnp.where(lo, vs, pad_one).astype(BF16)
            kwp_ref[:, gcols] = jnp.where(lo, kw, 0.0).astype(BF16)
            vwaug_ref[:, gcols] = jnp.where(lo, vw, pad_one).astype(BF16)
            gg_ref[:, gcols] = gates if g == 0 else pltpu.roll(gates, LANES - g * 3 * R_NSA, 1)


def _proj(x, shift, scale, g, w_in_bf, gains, bd, cos, sin, *, tm, rows_per_mod, pos_blocks, with_kmean):
    n, d = x.shape
    nt = n // tm
    mod_r = shift.shape[1]
    mod_map = lambda i: (i // (rows_per_mod // tm), 0, 0)
    pos_map = lambda i: (i % pos_blocks, 0)
    row = lambda w: pl.BlockSpec((tm, w), lambda i: (i, 0))
    const = lambda a: pl.BlockSpec(a.shape, lambda i: (0,) * a.ndim)
    gw = G_NSA * HEAD_DIM
    cache_widths = (W_KVM, 4 * gw, 2 * gw)
    if with_kmean:
        nb = n // rows_per_mod
        tpb = rows_per_mod // tm
        out_shapes = [jax.ShapeDtypeStruct((nb, w, rows_per_mod), F32) for w in cache_widths]
        out_specs = [pl.BlockSpec((1, w, tm), lambda i: (i // tpb, 0, i % tpb)) for w in cache_widths]
    else:
        out_shapes = [jax.ShapeDtypeStruct((n, w), F32) for w in cache_widths]
        out_specs = [row(w) for w in cache_widths]
    out_shapes += [jax.ShapeDtypeStruct((n, W_QM), F32), jax.ShapeDtypeStruct((n, W_QN), F32),
                   jax.ShapeDtypeStruct((n, W_QN), F32), jax.ShapeDtypeStruct((n, LANES), F32)]
    out_specs += [row(W_QM), row(W_QN), row(W_QN), row(LANES)]
    if with_kmean:
        nbt = tm // MOBA_BLOCK
        out_shapes.append(jax.ShapeDtypeStruct((nt, nbt, W_QM), F32))
        out_specs.append(pl.BlockSpec((1, nbt, W_QM), lambda i: (i, 0, 0)))
        gl = G_NSA * LANES
        for width, dtype in ((2 * gw, F32), (H_MOBA * LANES, BF16), (H_MOBA * LANES, BF16), (2 * gl, BF16), (gl, BF16),
                             (gl, BF16), (gl, BF16), (gl, F32)):
            out_shapes.append(jax.ShapeDtypeStruct((n, width), dtype))
            out_specs.append(row(width))
    return pl.pallas_call(
        functools.partial(_proj_kernel, with_kmean=with_kmean, pos_blocks=pos_blocks),
        grid=(nt,),
        in_specs=[row(d), pl.BlockSpec((1, mod_r, d), mod_map), pl.BlockSpec((1, mod_r, d), mod_map),
                  const(g), const(w_in_bf), const(gains), const(bd),
                  pl.BlockSpec((tm, LANES), pos_map), pl.BlockSpec((tm, LANES), pos_map)],
        out_specs=out_specs,
        out_shape=out_shapes,
        compiler_params=_params("parallel"),
        name="proj",
    )(x, shift, scale, g, w_in_bf, gains, bd, cos, sin)


def _compress_compute(src_refs, pe_ref, w1_ref, w2_ref, gk_ref, bd_ref, kc_ref, vc_ref, n_rows):
    n_chunk = n_rows // CMP_STRIDE
    gw = G_NSA * HEAD_DIM
    for kv in range(2):
        acc_a = jnp.zeros((n_chunk, gw), F32)
        acc_b = jnp.zeros((n_chunk, gw), F32)
        for j in range(CMP_STRIDE):
            xj = src_refs[kv][pl.ds(j, n_chunk, stride=CMP_STRIDE), :]
            acc_a = acc_a + _dot(xj + pe_ref[kv, 0, j:j + 1, :], w1_ref[kv, 0, j], HIGHEST)
            acc_b = acc_b + _dot(xj + pe_ref[kv, 1, j:j + 1, :], w1_ref[kv, 1, j], HIGHEST)
        hid = jax.nn.gelu(acc_a + pltpu.roll(acc_b, n_chunk - 1, 0))
        out = _dot(hid, w2_ref[kv], HIGHEST)
        if kv == 0:
            kc_ref[0] = _head_norm(out, gk_ref[...], bd_ref[...])
        else:
            vc_ref[0] = out


def _compress_prompt_kernel(k_ref, v_ref, pe_ref, w1_ref, w2_ref, gk_ref, bd_ref, kc_ref, vc_ref, *, n_rows):
    _compress_compute((k_ref.at[0], v_ref.at[0]), pe_ref, w1_ref, w2_ref, gk_ref, bd_ref, kc_ref, vc_ref, n_rows)


def _compress_paged_kernel(pt_ref, page_ref, pe_ref, w1_ref, w2_ref, gk_ref, bd_ref, kc_ref, vc_ref, kbuf, vbuf,
                           *, n_rows, page):
    p = pl.program_id(1)
    gw = G_NSA * HEAD_DIM
    rows = pl.ds(pl.multiple_of(p * page, page), page)
    kbuf[rows, :] = page_ref[0, :, 0:gw]
    vbuf[rows, :] = page_ref[0, :, gw:2 * gw]

    @pl.when(p == pl.num_programs(1) - 1)
    def _():
        _compress_compute((kbuf, vbuf), pe_ref, w1_ref, w2_ref, gk_ref, bd_ref, kc_ref, vc_ref, n_rows)


def _compress_consts(cmp_pos, cmp_w1, cmp_w2, gain_k_cmp):
    pe = jnp.tile(cmp_pos.reshape(2, 2, CMP_STRIDE, HEAD_DIM), (1, 1, 1, G_NSA))
    eye = jnp.eye(G_NSA, dtype=F32)
    w1 = cmp_w1.reshape(2, 2, CMP_STRIDE, HEAD_DIM, HEAD_DIM)
    w1bd = jnp.einsum("gh,kajde->kajgdhe", eye, w1).reshape(2, 2, CMP_STRIDE, G_NSA * HEAD_DIM, G_NSA * HEAD_DIM)
    w2bd = jnp.einsum("gh,kde->kgdhe", eye, cmp_w2).reshape(2, G_NSA * HEAD_DIM, G_NSA * HEAD_DIM)
    gk = jnp.tile(gain_k_cmp.reshape(1, HEAD_DIM), (1, G_NSA))
    return pe, w1bd, w2bd, gk


def _compress_prompt(cmp_raw, consts, bd):
    b, s, _ = cmp_raw.shape
    pe, w1bd, w2bd, gk = consts
    n_chunk = s // CMP_STRIDE
    gw = G_NSA * HEAD_DIM
    const = lambda a: pl.BlockSpec(a.shape, lambda i: (0,) * a.ndim)
    out = jax.ShapeDtypeStruct((b, n_chunk, gw), F32)
    return pl.pallas_call(
        functools.partial(_compress_prompt_kernel, n_rows=s),
        grid=(b,),
        in_specs=[pl.BlockSpec((1, s, gw), lambda i: (i, 0, 0)), pl.BlockSpec((1, s, gw), lambda i: (i, 0, 1)),
                  const(pe), const(w1bd), const(w2bd), const(gk), const(bd)],
        out_specs=[pl.BlockSpec((1, n_chunk, gw), lambda i: (i, 0, 0))] * 2,
        out_shape=[out, out],
        compiler_params=_params("parallel"),
        name="compress_prompt",
    )(cmp_raw, cmp_raw, pe, w1bd, w2bd, gk, bd)


def _compress_paged(cache_nsa, pt_flat, consts, bd, *, n_req, n_pages):
    _, page, _ = cache_nsa.shape
    pe, w1bd, w2bd, gk = consts
    n_rows = n_pages * page
    n_chunk = n_rows // CMP_STRIDE
    gw = G_NSA * HEAD_DIM
    const = lambda a: pl.BlockSpec(a.shape, lambda b, p, pt: (0,) * a.ndim)
    out = jax.ShapeDtypeStruct((n_req, n_chunk, gw), F32)
    return pl.pallas_call(
        functools.partial(_compress_paged_kernel, n_rows=n_rows, page=page),
        grid_spec=pltpu.PrefetchScalarGridSpec(
            num_scalar_prefetch=1,
            grid=(n_req, n_pages),
            in_specs=[pl.BlockSpec((1, page, 2 * gw), lambda b, p, pt: (pt[b * n_pages + p], 0, 0)),
                      const(pe), const(w1bd), const(w2bd), const(gk), const(bd)],
            out_specs=[pl.BlockSpec((1, n_chunk, gw), lambda b, p, pt: (b, 0, 0))] * 2,
            scratch_shapes=[pltpu.VMEM((n_rows, gw), F32)] * 2),
        out_shape=[out, out],
        compiler_params=_params("parallel", "arbitrary"),
        name="compress_paged",
    )(pt_flat, cache_nsa, pe, w1bd, w2bd, gk, bd)


def _moba_kernel(q_ref, kmp_ref, k_ref, v_ref, o_ref):
    qi = pl.program_id(2)
    lane = _iota((MOBA_Q, LANES), 1)
    n_blk = MOBA_BLOCK // SUBLANES
    blk = _iota((n_blk, MOBA_Q), 0)
    own = (qi * MOBA_Q + _iota((n_blk, MOBA_Q), 1)) // MOBA_BLOCK
    valid = blk < own
    q_augs = []
    for h in range(MOBA_HEADS):
        q2 = q_ref[0, :, (h // 2) * LANES:(h // 2 + 1) * LANES]
        q0 = jnp.where(lane < HEAD_DIM, q2 if h % 2 == 0 else pltpu.roll(q2, HEAD_DIM, 1), 0.0)
        score = _dot_nt_x3(kmp_ref[0, h], q0)[HEAD_DIM:HEAD_DIM + n_blk, :]
        picked = _top_k_sublanes(jnp.where(valid, score, -jnp.inf), MOBA_TOPK)
        sel = ((picked > 0.5) & valid) | (blk == own)
        bias_t = jnp.concatenate([jnp.zeros((HEAD_DIM, MOBA_Q), F32), jnp.where(sel, 0.0, NEG_BIG),
                                  jnp.zeros((LANES - HEAD_DIM - n_blk, MOBA_Q), F32)], axis=0)
        q_augs.append(jnp.where(lane < HEAD_DIM, q0 * SCALE_LOG2E, bias_t.T).astype(BF16))
    pos = qi * MOBA_Q + _iota((MOBA_Q, ATT_TK), 0)

    def tile(j, carry, masked):
        start = pl.multiple_of(j * ATT_TK, ATT_TK)
        mask = None
        if masked:
            mask = (j * ATT_TK + _iota((MOBA_Q, ATT_TK), 1)) <= pos
        return tuple(_flash_step(q_augs[h], k_ref[0, pl.ds(start, ATT_TK), h * LANES:(h + 1) * LANES],
                                 v_ref[0, pl.ds(start, ATT_TK), h * LANES:(h + 1) * LANES], mask, *carry[h])
                     for h in range(MOBA_HEADS))

    init = (jnp.full((MOBA_Q, 1), NEG_BIG, F32), jnp.zeros((MOBA_Q, LANES), F32))
    jd = (qi * MOBA_Q) // ATT_TK
    carry = lax.fori_loop(0, jd, lambda j, c: tile(j, c, False), tile(jd, (init,) * MOBA_HEADS, True))
    for c in range(MOBA_HEADS // 2):
        o_ref[0, :, c * LANES:(c + 1) * LANES] = jnp.where(
            lane < HEAD_DIM, _flash_finish(carry[2 * c][1]),
            pltpu.roll(_flash_finish(carry[2 * c + 1][1]), HEAD_DIM, 1)).astype(o_ref.dtype)


def _moba_attention(q_m, kmp, k_aug, v_aug):
    b, s, _ = q_m.shape
    nh = MOBA_HEADS
    return pl.pallas_call(
        _moba_kernel,
        grid=(b, H_MOBA // nh, s // MOBA_Q),
        in_specs=[pl.BlockSpec((1, MOBA_Q, nh * HEAD_DIM), lambda b, h, i: (b, i, h)),
                  pl.BlockSpec((1, nh, LANES, LANES), lambda b, h, i: (b, h, 0, 0)),
                  pl.BlockSpec((1, s, nh * LANES), lambda b, h, i: (b, 0, h)),
                  pl.BlockSpec((1, s, nh * LANES), lambda b, h, i: (b, 0, h))],
        out_specs=pl.BlockSpec((1, MOBA_Q, nh * HEAD_DIM), lambda b, h, i: (b, i, h)),
        out_shape=jax.ShapeDtypeStruct((b, s, W_QM), BF16),
        compiler_params=_params("parallel", "parallel", "arbitrary"),
        name="moba_attention",
    )(q_m, kmp, k_aug, v_aug)


def _stack_heads(q4):
    lane = _iota((Q_TILE, LANES), 1)
    parts = []
    for r in range(R_NSA):
        c = q4[:, (r // 2) * LANES:(r // 2 + 1) * LANES]
        if r % 2:
            c = pltpu.roll(c, HEAD_DIM, 1)
        parts.append(jnp.where(lane < HEAD_DIM, c, 0.0))
    return jnp.concatenate(parts, axis=0)


def _nsa_kernel(qn_ref, qr_ref, gate_ref, kc_ref, vc_ref, cover_ref, ks_ref, vs_ref, kw_ref, vw_ref, o_ref,
                *, n_chunk):
    qi = pl.program_id(2)
    rows = R_NSA * Q_TILE
    qloc = _iota((rows, 1), 0) % Q_TILE
    pos = qi * Q_TILE + qloc

    qn = _stack_heads(qn_ref[0])
    s = _dot_nt_x3(qn, kc_ref[0, 0]) * SCALE
    cmask = (_iota((rows, n_chunk), 1) * CMP_STRIDE + (CMP_LEN - 1)) <= pos
    s = jnp.where(cmask, s, NEG_BIG)
    e = jnp.where(cmask, jnp.exp(s - jnp.max(s, axis=1, keepdims=True)), 0.0)
    p_cmp = e / jnp.maximum(jnp.sum(e, axis=1, keepdims=True), TINY)
    o_cmp = _dot(p_cmp.astype(BF16), vc_ref[0, 0])

    p_grp = p_cmp[0:Q_TILE]
    for r in range(1, R_NSA):
        p_grp = p_grp + p_cmp[r * Q_TILE:(r + 1) * Q_TILE]
    p_hi, p_lo = _split_bf16(p_grp)
    imp_t = _dot_nt(cover_ref[...], p_hi) + _dot_nt(cover_ref[...], p_lo)
    blk_t = _iota((LANES, Q_TILE), 0)
    own_t = (qi * Q_TILE + _iota((LANES, Q_TILE), 1)) // SEL_BLOCK
    forced = (blk_t == 0) | (blk_t == own_t) | (blk_t == own_t - 1)
    score = jnp.where(blk_t <= own_t, jnp.where(forced, jnp.inf, imp_t), -jnp.inf)
    picked = _top_k_sublanes(score, SEL_TOPN)
    bias = jnp.where((picked > 0.5) & (blk_t <= own_t), 0.0, NEG_BIG).T
    blk = _iota((Q_TILE, LANES), 1)

    qr = (_stack_heads(qr_ref[0]) * SCALE_LOG2E).astype(BF16)
    q_aug = jnp.concatenate([jnp.concatenate([bias] * R_NSA, axis=0).astype(BF16), qr], axis=1)
    init = (jnp.full((rows, 1), NEG_BIG, F32), jnp.zeros((rows, LANES), F32))

    tk = ATT_TK
    jd = (qi * Q_TILE) // tk

    def sel_tile(j, carry, masked):
        start = pl.multiple_of(j * tk, tk)
        mask = None
        if masked:
            mask = (j * tk + _iota((rows, tk), 1)) <= pos
        return _flash_step(q_aug, ks_ref[0, pl.ds(start, tk), :], vs_ref[0, pl.ds(start, tk), :], mask, *carry)

    o_sel = _flash_finish(lax.fori_loop(0, jd, lambda j, c: sel_tile(j, c, False), sel_tile(jd, init, True))[1])

    span = WINDOW + Q_TILE
    w0 = pl.multiple_of(jnp.maximum(qi * Q_TILE - WINDOW, 0), Q_TILE)
    kpos = w0 + _iota((rows, span), 1)
    wmask = (kpos <= pos) & (kpos > pos - WINDOW)
    o_win = _flash_finish(_flash_step(qr, kw_ref[0, pl.ds(w0, span), :], vw_ref[0, pl.ds(w0, span), :], wmask,
                                      *init)[1])

    gates = gate_ref[0]
    heads = []
    for r in range(R_NSA):
        rs = slice(r * Q_TILE, (r + 1) * Q_TILE)
        heads.append(gates[:, 3 * r:3 * r + 1] * o_cmp[rs] + gates[:, 3 * r + 1:3 * r + 2] * o_sel[rs]
                     + gates[:, 3 * r + 2:3 * r + 3] * o_win[rs])
    lo = blk < HEAD_DIM
    o_ref[0] = jnp.concatenate([jnp.where(lo, heads[0], pltpu.roll(heads[1], HEAD_DIM, 1)),
                                jnp.where(lo, heads[2], pltpu.roll(heads[3], HEAD_DIM, 1))], axis=1).astype(o_ref.dtype)


def _nsa_attention(qn, qr, gates_g, kcp, vcd, cover, ks_aug, vs_dup, kw_pad, vw_dup):
    b, s, _ = qn.shape
    n_chunk = kcp.shape[2]
    gq = R_NSA * HEAD_DIM
    per_g = lambda w: pl.BlockSpec((1, s, w), lambda b, g, i: (b, 0, g))
    return pl.pallas_call(
        functools.partial(_nsa_kernel, n_chunk=n_chunk),
        grid=(b, G_NSA, s // Q_TILE),
        in_specs=[pl.BlockSpec((1, Q_TILE, gq), lambda b, g, i: (b, i, g)),
                  pl.BlockSpec((1, Q_TILE, gq), lambda b, g, i: (b, i, g)),
                  pl.BlockSpec((1, Q_TILE, LANES), lambda b, g, i: (b, i, g)),
                  pl.BlockSpec((1, 1, n_chunk, LANES), lambda b, g, i: (b, g, 0, 0)),
                  pl.BlockSpec((1, 1, n_chunk, LANES), lambda b, g, i: (b, g, 0, 0)),
                  pl.BlockSpec(cover.shape, lambda b, g, i: (0, 0)),
                  per_g(2 * LANES), per_g(LANES), per_g(LANES), per_g(LANES)],
        out_specs=pl.BlockSpec((1, Q_TILE, gq), lambda b, g, i: (b, i, g)),
        out_shape=jax.ShapeDtypeStruct((b, s, W_QN), BF16),
        compiler_params=_params("parallel", "parallel", "arbitrary"),
        name="nsa_attention",
    )(qn, qr, gates_g, kcp, vcd, cover, ks_aug, vs_dup, kw_pad, vw_dup)


def _cover(n_cmp, n_sel, rows):
    c0 = np.arange(rows)[:, None] * CMP_STRIDE
    b0 = np.arange(LANES)[None, :] * SEL_BLOCK
    ok = (c0 < b0 + SEL_BLOCK) & (c0 + CMP_LEN > b0) & (np.arange(rows)[:, None] < n_cmp) & (np.arange(LANES)[None, :] < n_sel)
    return jnp.asarray(ok.astype(np.float32))


def _dec_kernel(pt_ref, ma_ref, mb_ref, na_ref, nb_ref, wb_ref, qbd_ref, q8n_ref, q8r_ref, g8_ref, newm_ref,
                news_ref, neww_ref, kc_ref, vc_ref, cover_ref, om_ref, o8_ref,
                m_sc, l_sc, s_sc, acc_sc, msel_sc, lsel_sc, asel_sc, bias_sc, ocmp_sc, *, n_steps, page):
    j = pl.program_id(1)
    sub8 = _iota((SUBLANES, LANES), 0)
    lane8 = _iota((SUBLANES, LANES), 1)
    tk = 2 * page
    q8r = q8r_ref[0] * SCALE
    q8r_bf = q8r.astype(BF16)

    @pl.when(j == 0)
    def _():
        n_chunk = kc_ref.shape[1]
        s = _dot_nt(q8n_ref[0], kc_ref[0], HIGHEST) * SCALE
        cmask = _iota((SUBLANES, n_chunk), 1) < (n_chunk - 1)
        s = jnp.where(cmask, s, NEG_BIG)
        e = jnp.where(cmask, jnp.exp(s - jnp.max(s, axis=1, keepdims=True)), 0.0)
        p_cmp = e / jnp.maximum(jnp.sum(e, axis=1, keepdims=True), TINY)
        ocmp_sc[...] = _dot(p_cmp.astype(BF16), vc_ref[0].astype(BF16))
        subc = _iota((SUBLANES, n_chunk), 0)
        g0 = jnp.sum(jnp.where(subc < R_NSA, p_cmp, 0.0), axis=0, keepdims=True)
        g1 = jnp.sum(jnp.where(subc >= R_NSA, p_cmp, 0.0), axis=0, keepdims=True)
        imp = _dot(jnp.where(subc < R_NSA, g0, g1), cover_ref[...], HIGHEST)
        own = (n_steps * tk) // SEL_BLOCK
        forced = (lane8 == 0) | (lane8 == own) | (lane8 == own - 1)
        score = jnp.where(lane8 <= own, jnp.where(forced, jnp.inf, imp), -jnp.inf)
        picked, _, _ = _top_k_lanes(score, SEL_TOPN)
        bias_sc[...] = jnp.where((picked > 0.5) & (lane8 <= own), 0.0, NEG_BIG)
        msel_sc[...] = jnp.full((SUBLANES, LANES), NEG_BIG, F32)
        lsel_sc[...] = jnp.zeros((SUBLANES, LANES), F32)
        asel_sc[...] = jnp.zeros((SUBLANES, LANES), F32)

    hw = H_MOBA * HEAD_DIM
    k_m = jnp.concatenate([ma_ref[0, :, 0:hw], mb_ref[0, :, 0:hw]], axis=0)
    v_m = jnp.concatenate([ma_ref[0, :, hw:2 * hw], mb_ref[0, :, hw:2 * hw]], axis=0)
    qbd = qbd_ref[0]
    s = _dot_nt((qbd * SCALE).astype(BF16), k_m.astype(BF16))
    mj = jnp.max(s, axis=1, keepdims=True)
    p = jnp.exp(s - mj)
    m_sc[j] = jnp.broadcast_to(mj, (SUBLANES, LANES))
    l_sc[j] = jnp.broadcast_to(jnp.sum(p, axis=1, keepdims=True), (SUBLANES, LANES))
    acc_sc[j] = _dot(p.astype(BF16), v_m.astype(BF16))
    kmean = jnp.sum(k_m, axis=0, keepdims=True) * (1.0 / MOBA_BLOCK)
    s_sc[j] = jnp.broadcast_to(jnp.sum(qbd * kmean, axis=1, keepdims=True), (SUBLANES, LANES))

    gw = G_NSA * HEAD_DIM
    k_s = jnp.concatenate([na_ref[0, :, 0:gw], nb_ref[0, :, 0:gw]], axis=0).astype(BF16)
    v_s = jnp.concatenate([na_ref[0, :, gw:2 * gw], nb_ref[0, :, gw:2 * gw]], axis=0).astype(BF16)
    s = _dot_nt(q8r_bf, k_s)
    bias = bias_sc[...]
    kb = _iota((SUBLANES, tk), 1) // SEL_BLOCK
    bias_k = jnp.zeros((SUBLANES, tk), F32)
    for q in range(tk // SEL_BLOCK):
        bq = jnp.sum(jnp.where(lane8 == j * (tk // SEL_BLOCK) + q, bias, 0.0), axis=1, keepdims=True)
        bias_k = jnp.where(kb == q, bq, bias_k)
    s = s + bias_k
    m_old = msel_sc[:, 0:1]
    m_new = jnp.maximum(m_old, jnp.max(s, axis=1, keepdims=True))
    alpha = jnp.exp(m_old - m_new)
    p = jnp.exp(s - m_new)
    msel_sc[...] = jnp.broadcast_to(m_new, (SUBLANES, LANES))
    lsel_sc[...] = alpha * lsel_sc[...] + jnp.sum(p, axis=1, keepdims=True)
    asel_sc[...] = alpha * asel_sc[...] + _dot(p.astype(BF16), v_s)

    @pl.when(j == n_steps - 1)
    def _():
        news = news_ref[0]
        s_new = jnp.sum(q8r * news[0:1], axis=1, keepdims=True)
        m_old = msel_sc[:, 0:1]
        m_new = jnp.maximum(m_old, s_new)
        alpha = jnp.exp(m_old - m_new)
        pn = jnp.exp(s_new - m_new)
        l = alpha * lsel_sc[...] + pn
        o_sel = (alpha * asel_sc[...] + pn * news[1:2]) / jnp.maximum(l, TINY)

        neww = neww_ref[0]
        kw = wb_ref[0, :, 0:gw].astype(BF16)
        vw = wb_ref[0, :, gw:2 * gw].astype(BF16)
        nw = kw.shape[0]
        s = _dot_nt(q8r_bf, kw)
        wmask = _iota((SUBLANES, nw), 1) >= 1
        s = jnp.where(wmask, s, NEG_BIG)
        s_new = jnp.sum(q8r * neww[0:1], axis=1, keepdims=True)
        mw = jnp.maximum(jnp.max(s, axis=1, keepdims=True), s_new)
        e = jnp.where(wmask, jnp.exp(s - mw), 0.0)
        en = jnp.exp(s_new - mw)
        lw = jnp.sum(e, axis=1, keepdims=True) + en
        o_win = (_dot(e.astype(BF16), vw) + en * neww[1:2]) / jnp.maximum(lw, TINY)

        g8 = g8_ref[0]
        o8_ref[0] = g8[:, 0:1] * ocmp_sc[...] + g8[:, 1:2] * o_sel + g8[:, 2:3] * o_win

        newm = newm_ref[0]
        scores = [s_sc[b] for b in range(n_steps)]
        m_tot = jnp.broadcast_to(jnp.sum(qbd * newm[0:1], axis=1, keepdims=True) * SCALE, (SUBLANES, LANES))
        s_own = m_tot
        sels = []
        for b in range(n_steps):
            rank = jnp.zeros((SUBLANES, LANES), F32)
            for c in range(n_steps):
                if c == b:
                    continue
                ahead = (scores[c] > scores[b]) | ((scores[c] == scores[b]) & (c < b))
                rank = rank + jnp.where(ahead, 1.0, 0.0)
            sels.append(rank < MOBA_TOPK)
            m_tot = jnp.where(sels[b], jnp.maximum(m_tot, m_sc[b]), m_tot)
        w_own = jnp.exp(s_own - m_tot)
        l_tot = w_own
        acc = w_own[:, 0:1] * newm[1:2]
        for b in range(n_steps):
            wgt = jnp.where(sels[b], jnp.exp(m_sc[b] - m_tot), 0.0)
            l_tot = l_tot + wgt * l_sc[b]
            acc = acc + wgt[:, 0:1] * acc_sc[b]
        o_full = acc / jnp.maximum(l_tot[:, 0:1], TINY)
        own_head = (_iota((SUBLANES, hw), 1) // HEAD_DIM) == _iota((SUBLANES, hw), 0)
        om_ref[0] = jnp.sum(jnp.where(own_head, o_full, 0.0), axis=0, keepdims=True)


def _dec_attention(cache_moba, cache_nsa, win_buf, pt_flat, qbd, q8n, q8r, g8, newm, news, neww, kc, vc, cover,
                   *, n_req, n_pages):
    _, page, mw = cache_moba.shape
    n_steps = n_pages // 2
    gw = G_NSA * HEAD_DIM
    hw = H_MOBA * HEAD_DIM
    n_chunk = kc.shape[1]
    nwin = win_buf.shape[1]
    req = lambda a: pl.BlockSpec((1,) + a.shape[1:], lambda b, j, pt: (b,) + (0,) * (a.ndim - 1))
    return pl.pallas_call(
        functools.partial(_dec_kernel, n_steps=n_steps, page=page),
        grid_spec=pltpu.PrefetchScalarGridSpec(
            num_scalar_prefetch=1,
            grid=(n_req, n_steps),
            in_specs=[pl.BlockSpec((1, page, mw), lambda b, j, pt: (pt[b * n_pages + 2 * j], 0, 0)),
                      pl.BlockSpec((1, page, mw), lambda b, j, pt: (pt[b * n_pages + 2 * j + 1], 0, 0)),
                      pl.BlockSpec((1, page, 2 * gw), lambda b, j, pt: (pt[b * n_pages + 2 * j], 0, 1)),
                      pl.BlockSpec((1, page, 2 * gw), lambda b, j, pt: (pt[b * n_pages + 2 * j + 1], 0, 1)),
                      req(win_buf), req(qbd), req(q8n), req(q8r), req(g8), req(newm), req(news), req(neww),
                      req(kc), req(vc), pl.BlockSpec(cover.shape, lambda b, j, pt: (0, 0))],
            out_specs=[pl.BlockSpec((1, 1, hw), lambda b, j, pt: (b, 0, 0)),
                       pl.BlockSpec((1, SUBLANES, LANES), lambda b, j, pt: (b, 0, 0))],
            scratch_shapes=[pltpu.VMEM((n_steps, SUBLANES, LANES), F32)] * 3
            + [pltpu.VMEM((n_steps, SUBLANES, hw), F32)]
            + [pltpu.VMEM((SUBLANES, LANES), F32)] * 5),
        out_shape=[jax.ShapeDtypeStruct((n_req, 1, hw), F32), jax.ShapeDtypeStruct((n_req, SUBLANES, LANES), F32)],
        compiler_params=_params("parallel", "arbitrary"),
        name="dec_attention",
    )(pt_flat, cache_moba, cache_moba, cache_nsa, cache_nsa, win_buf, qbd, q8n, q8r, g8, newm, news, neww, kc, vc,
      cover)


def _lane_rep(col):
    return jnp.broadcast_to(col, (col.shape[0], LANES))


def _head_sums(prod_row):
    w = prod_row.shape[1]
    own = (_iota((SUBLANES, w), 1) // HEAD_DIM) == _iota((SUBLANES, w), 0)
    return _lane_rep(jnp.sum(jnp.where(own, jnp.broadcast_to(prod_row, (SUBLANES, w)), 0.0), axis=1, keepdims=True))


def _pair_row(x8, h):
    return jnp.where(_iota((1, LANES), 1) < HEAD_DIM, x8[h:h + 1, :], x8[h + 1:h + 2, :])


def _cols_to_row(acc_a, acc_b):
    return jnp.sum(jnp.concatenate([acc_a, acc_b], axis=0).T, axis=0, keepdims=True)


def _dec2_kernel(pt_ref, cm_ref, cn_ref, wb_ref, q_ref, q8n_ref, new_ref, gate_ref, wcat_ref, pecat_ref, w2_ref,
                 gk_ref, bd_ref, cover_ref, o_ref,
                 mbuf, nbuf, sem, xk, xv, qmb, qrb, s_sc, p_sc, *, n_pages, page):
    b = pl.program_id(0)
    n_req = pl.num_programs(0)
    slot = b % 2
    hw = H_MOBA * HEAD_DIM
    gw = G_NSA * HEAD_DIM
    past = n_pages * page

    def copies(req, sl):
        out = []
        for p in range(n_pages):
            pg = pt_ref[req * n_pages + p]
            out.append(pltpu.make_async_copy(cm_ref.at[pg], mbuf.at[sl, p], sem.at[0, sl]))
            out.append(pltpu.make_async_copy(cn_ref.at[pg], nbuf.at[sl, p], sem.at[1, sl]))
        return out

    @pl.when(b == 0)
    def _():
        for c in copies(0, 0):
            c.start()

    for c in copies(b, slot):
        c.wait()

    @pl.when(b + 1 < n_req)
    def _():
        for c in copies(b + 1, 1 - slot):
            c.start()

    lane1 = _iota((1, LANES), 1)
    lane8 = _iota((SUBLANES, LANES), 1)
    qrow = q_ref[0]
    new = new_ref[0]
    for c in range(hw // LANES):
        cols = slice(c * LANES, (c + 1) * LANES)
        qmb[cols, :] = jnp.broadcast_to(qrow[0:1, cols] * SCALE, (LANES, LANES)).T
        qrb[cols, :] = jnp.broadcast_to(qrow[1:2, cols] * SCALE, (LANES, LANES)).T

    def softmax_pv(scores, extra8, s_new8, v_rows, vbuf_ref, v_row0, per_g):
        del extra8
        m8 = s_new8
        for s in scores:
            m8 = jnp.maximum(m8, _lane_rep(jnp.max(s, axis=1, keepdims=True)))
        w_new = jnp.exp(s_new8 - m8)
        l8 = w_new
        for p, s in enumerate(scores):
            pr = jnp.exp(s - m8)
            p_sc[p] = pr
            l8 = l8 + _lane_rep(jnp.sum(pr, axis=1, keepdims=True))
        inv8 = 1.0 / jnp.maximum(l8, TINY)
        rows = []
        for hp in range(SUBLANES // 2):
            accs = []
            for h in (2 * hp, 2 * hp + 1):
                r0 = v_row0 + (h // R_NSA if per_g else h) * HEAD_DIM

                def body(p, acc, h=h, r0=r0):
                    return acc + vbuf_ref[slot, p, r0:r0 + HEAD_DIM, :] * p_sc[p, h:h + 1, :]
                accs.append(lax.fori_loop(0, n_pages, body, jnp.zeros((HEAD_DIM, LANES), F32)))
            row = _cols_to_row(accs[0], accs[1])
            cols = slice(hp * LANES, (hp + 1) * LANES)
            rows.append((row + _pair_row(w_new, 2 * hp) * v_rows[:, cols]) * _pair_row(inv8, 2 * hp))
        return rows

    def moba_scores(p, carry):
        rows = [jnp.sum(mbuf[slot, p, h * HEAD_DIM:(h + 1) * HEAD_DIM, :] * qmb[h * HEAD_DIM:(h + 1) * HEAD_DIM, :],
                        axis=0, keepdims=True) for h in range(H_MOBA)]
        s_sc[p] = jnp.concatenate(rows, axis=0)
        return carry
    lax.fori_loop(0, n_pages, moba_scores, 0)
    s_all = [s_sc[p] for p in range(n_pages)]
    ppb = MOBA_BLOCK // page
    n_blk = n_pages // ppb
    bsc = []
    for j in range(n_blk):
        tot = s_all[j * ppb]
        for t in range(1, ppb):
            tot = tot + s_all[j * ppb + t]
        bsc.append(_lane_rep(jnp.sum(tot, axis=1, keepdims=True)))
    masked = []
    for j in range(n_blk):
        rank = jnp.zeros((SUBLANES, LANES), F32)
        for c in range(n_blk):
            if c != j:
                ahead = (bsc[c] > bsc[j]) | ((bsc[c] == bsc[j]) & (c < j))
                rank = rank + jnp.where(ahead, 1.0, 0.0)
        for t in range(ppb):
            masked.append(jnp.where(rank < MOBA_TOPK, s_all[j * ppb + t], NEG_BIG))
    s_own = _head_sums(qrow[0:1, :] * new[0:1, :]) * SCALE
    o_rows = softmax_pv(masked, None, s_own, new[1:2, :], mbuf, hw, False)

    for p in range(n_pages):
        xk[p * page:(p + 1) * page, :] = nbuf[slot, p, 0:gw, :].T
        xv[p * page:(p + 1) * page, :] = nbuf[slot, p, gw:2 * gw, :].T
    n_chunk = past // CMP_STRIDE
    cmp_out = []
    for kv, xref in enumerate((xk, xv)):
        xcat = jnp.concatenate([xref[pl.ds(j, n_chunk, stride=CMP_STRIDE), :] for j in range(CMP_STRIDE)], axis=1)
        pe2 = _dot(pecat_ref[kv].astype(BF16), wcat_ref[kv])
        ab = _dot(xcat.astype(BF16), wcat_ref[kv])
        hid = jax.nn.gelu(ab[:, 0:gw] + pe2[0:1, 0:gw] + pltpu.roll(ab[:, gw:2 * gw] + pe2[1:2, gw:2 * gw],
                                                                    n_chunk - 1, 0))
        cmp_out.append(_dot(hid, w2_ref[kv], HIGHEST))
    kc = _head_norm(cmp_out[0], gk_ref[...], bd_ref[...])
    vc = cmp_out[1]

    s = _dot_nt(q8n_ref[0], kc, HIGHEST) * SCALE
    cmask = _iota((SUBLANES, n_chunk), 1) < (n_chunk - 1)
    s = jnp.where(cmask, s, NEG_BIG)
    e = jnp.where(cmask, jnp.exp(s - jnp.max(s, axis=1, keepdims=True)), 0.0)
    p_cmp = e / jnp.maximum(jnp.sum(e, axis=1, keepdims=True), TINY)
    o_cmp8 = _dot(p_cmp.astype(BF16), vc.astype(BF16))
    subc = _iota((SUBLANES, n_chunk), 0)
    g0 = jnp.sum(jnp.where(subc < R_NSA, p_cmp, 0.0), axis=0, keepdims=True)
    g1 = jnp.sum(jnp.where(subc >= R_NSA, p_cmp, 0.0), axis=0, keepdims=True)
    p_grp = jnp.concatenate([jnp.where(subc < R_NSA, g0, g1), jnp.zeros((LANES - SUBLANES, n_chunk), F32)], axis=0)
    own = past // SEL_BLOCK
    n_sel = -(-(own + 1) // SUBLANES) * SUBLANES
    imp_t = _dot_nt(cover_ref[...], p_grp, HIGHEST)[0:n_sel]
    blk_t = _iota((n_sel, LANES), 0)
    forced = (blk_t == 0) | (blk_t == own) | (blk_t == own - 1)
    score = jnp.where(blk_t <= own, jnp.where(forced, jnp.inf, imp_t), -jnp.inf)
    picked = _top_k_sublanes(score, SEL_TOPN)
    bias_t = jnp.concatenate([jnp.where((picked > 0.5) & (blk_t <= own), 0.0, NEG_BIG),
                              jnp.full((LANES - n_sel, LANES), NEG_BIG, F32)], axis=0)
    bias8 = bias_t.T[0:SUBLANES, :]
    cmp_rows = []
    for hp in range(H_NSA // 2):
        g = (2 * hp) // R_NSA
        ra = o_cmp8[2 * hp:2 * hp + 1, :]
        rb = o_cmp8[2 * hp + 1:2 * hp + 2, :]
        cmp_rows.append(jnp.where(lane1 < HEAD_DIM, ra if g == 0 else pltpu.roll(ra, HEAD_DIM, 1),
                                  rb if g == 1 else pltpu.roll(rb, HEAD_DIM, 1)))

    def sel_scores(p, carry):
        rows = [jnp.sum(nbuf[slot, p, 2 * gw + (i // R_NSA) * HEAD_DIM:2 * gw + (i // R_NSA + 1) * HEAD_DIM, :]
                        * qrb[i * HEAD_DIM:(i + 1) * HEAD_DIM, :], axis=0, keepdims=True) for i in range(H_NSA)]
        s_sc[p] = jnp.concatenate(rows, axis=0)
        return carry
    lax.fori_loop(0, n_pages, sel_scores, 0)
    bpp = page // SEL_BLOCK
    sel_s = []
    for p in range(n_pages):
        bias_p = bias8[:, p * bpp:p * bpp + 1]
        for t in range(1, bpp):
            bias_p = jnp.where(lane8 < t * SEL_BLOCK, bias_p, bias8[:, p * bpp + t:p * bpp + t + 1])
        sel_s.append(s_sc[p] + bias_p)
    s_new = _head_sums(qrow[1:2, :] * new[2:3, :]) * SCALE
    sel_rows = softmax_pv(sel_s, None, s_new, new[3:4, :], nbuf, 3 * gw, True)

    nw = wb_ref.shape[2]
    wk = nw // LANES
    w_s = []
    for i in range(H_NSA):
        g = i // R_NSA
        qcol = jnp.concatenate([qrb[i * HEAD_DIM:(i + 1) * HEAD_DIM, :]] * wk, axis=1)
        w_s.append(jnp.sum(wb_ref[0, g * HEAD_DIM:(g + 1) * HEAD_DIM, :] * qcol, axis=0, keepdims=True))
    s = jnp.concatenate(w_s, axis=0)
    wmask = _iota((SUBLANES, nw), 1) >= 1
    s = jnp.where(wmask, s, NEG_BIG)
    s_new = _head_sums(qrow[1:2, :] * new[4:5, :]) * SCALE
    m8 = jnp.maximum(_lane_rep(jnp.max(s, axis=1, keepdims=True)), s_new)
    pw = jnp.where(wmask, jnp.exp(s - m8[:, 0:1]), 0.0)
    w_new = jnp.exp(s_new - m8)
    inv8 = 1.0 / jnp.maximum(_lane_rep(jnp.sum(pw, axis=1, keepdims=True)) + w_new, TINY)
    win_rows = []
    for hp in range(H_NSA // 2):
        accs = []
        for i in (2 * hp, 2 * hp + 1):
            g = i // R_NSA
            prod = wb_ref[0, gw + g * HEAD_DIM:gw + (g + 1) * HEAD_DIM, :] * pw[i:i + 1, :]
            acc = prod[:, 0:LANES]
            for c in range(1, wk):
                acc = acc + prod[:, c * LANES:(c + 1) * LANES]
            accs.append(acc)
        cols = slice(hp * LANES, (hp + 1) * LANES)
        win_rows.append((_cols_to_row(accs[0], accs[1]) + _pair_row(w_new, 2 * hp) * new[5:6, cols])
                        * _pair_row(inv8, 2 * hp))

    gates = gate_ref[0]
    for c in range(hw // LANES):
        o_ref[0, :, c * LANES:(c + 1) * LANES] = o_rows[c]
    for c in range(W_QN // LANES):
        cols = slice(c * LANES, (c + 1) * LANES)
        o_ref[0, :, hw + c * LANES:hw + (c + 1) * LANES] = (
            gates[0:1, cols] * cmp_rows[c] + gates[1:2, cols] * sel_rows[c] + gates[2:3, cols] * win_rows[c])


def _dec2_attention(cm, cn, wb, pt_flat, qrows, q8n, new, gate_rows, wcat, pecat, w2bd, gk, bd, cover,
                    *, n_req, n_pages):
    page = cm.shape[2]
    n_chunk = n_pages * page // CMP_STRIDE
    req = lambda a: pl.BlockSpec((1,) + a.shape[1:], lambda b, pt: (b,) + (0,) * (a.ndim - 1))
    const = lambda a: pl.BlockSpec(a.shape, lambda b, pt: (0,) * a.ndim)
    return pl.pallas_call(
        functools.partial(_dec2_kernel, n_pages=n_pages, page=page),
        grid_spec=pltpu.PrefetchScalarGridSpec(
            num_scalar_prefetch=1,
            grid=(n_req,),
            in_specs=[pl.BlockSpec(memory_space=pl.ANY), pl.BlockSpec(memory_space=pl.ANY), req(wb), req(qrows),
                      req(q8n), req(new), req(gate_rows), const(wcat), const(pecat), const(w2bd), const(gk),
                      const(bd), const(cover)],
            out_specs=pl.BlockSpec((1, 1, W_QM + W_QN), lambda b, pt: (b, 0, 0)),
            scratch_shapes=[pltpu.VMEM((2, n_pages) + cm.shape[1:], F32), pltpu.VMEM((2, n_pages) + cn.shape[1:], F32),
                            pltpu.SemaphoreType.DMA((2, 2)),
                            pltpu.VMEM((n_pages * page, LANES), F32), pltpu.VMEM((n_pages * page, LANES), F32),
                            pltpu.VMEM((W_QM, LANES), F32), pltpu.VMEM((W_QN, LANES), F32),
                            pltpu.VMEM((n_pages, SUBLANES, LANES), F32), pltpu.VMEM((n_pages, SUBLANES, LANES), F32)]),
        out_shape=jax.ShapeDtypeStruct((n_req, 1, W_QM + W_QN), F32),
        compiler_params=_params("arbitrary"),
        name="dec_attention",
    )(pt_flat, cm, cn, wb, qrows, q8n, new, gate_rows, wcat, pecat, w2bd, gk, bd, cover)


def _compress_cat_consts(cmp_pos, cmp_w1):
    eye = jnp.eye(G_NSA, dtype=F32)
    w1 = cmp_w1.reshape(2, 2, CMP_STRIDE, HEAD_DIM, HEAD_DIM)
    wcat = jnp.einsum("gh,kajde->kjgdahe", eye, w1).reshape(2, CMP_STRIDE * G_NSA * HEAD_DIM, 2 * G_NSA * HEAD_DIM)
    pe = jnp.tile(cmp_pos.reshape(2, 2, CMP_STRIDE, 1, HEAD_DIM), (1, 1, 1, G_NSA, 1))
    pecat = jnp.pad(pe.reshape(2, 2, CMP_STRIDE * G_NSA * HEAD_DIM), ((0, 0), (0, SUBLANES - 2), (0, 0)))
    return wcat.astype(BF16), pecat


def _post_kernel(o_ref, x_ref, gt_ref, sh_ref, sc_ref, g_ref, wo_ref, wr_ref, br_ref,
                 y_ref, h3_ref, te_ref, tw_ref):
    y = x_ref[...] + gt_ref[0] * _dot(o_ref[...], wo_ref[...])
    y_ref[...] = y
    h = y * lax.rsqrt(jnp.mean(y * y, axis=1, keepdims=True) + NORM_EPS) * g_ref[...]
    h = h * (1.0 + sc_ref[0]) + sh_ref[0]
    tm = h.shape[0]
    for s in range(h.shape[1] // LANES):
        h3_ref[pl.ds(s, tm, stride=SUBLANES), :] = h[:, s * LANES:(s + 1) * LANES]
    logits = _dot(h, wr_ref[...], HIGHEST) + br_ref[...]
    _, vals, ids = _top_k_lanes(logits, TOP_K)
    lane = _iota((tm, LANES), 1)
    es = [jnp.exp(v - vals[0]) for v in vals]
    den = es[0]
    for e in es[1:]:
        den = den + e
    te = jnp.zeros((tm, LANES), F32)
    tw = jnp.zeros((tm, LANES), F32)
    for k in range(TOP_K):
        te = jnp.where(lane == k, ids[k], te)
        tw = jnp.where(lane == k, es[k] / den, tw)
    te_ref[...] = te.astype(jnp.int32)
    tw_ref[...] = tw


def _post(o, x, gate, shift, scale, g2, wo_bf, wr_pad, br_pad, *, tm, rows_per_mod):
    n, d = x.shape
    mod_r = gate.shape[1]
    mod_map = lambda i: (i // (rows_per_mod // tm), 0, 0)
    row = lambda w: pl.BlockSpec((tm, w), lambda i: (i, 0))
    const = lambda a: pl.BlockSpec(a.shape, lambda i: (0,) * a.ndim)
    mod = pl.BlockSpec((1, mod_r, d), mod_map)
    return pl.pallas_call(
        _post_kernel,
        grid=(n // tm,),
        in_specs=[row(d), row(d), mod, mod, mod, const(g2), const(wo_bf), const(wr_pad), const(br_pad)],
        out_specs=[row(d), pl.BlockSpec((tm * SUBLANES, LANES), lambda i: (i, 0)), row(LANES), row(LANES)],
        out_shape=[jax.ShapeDtypeStruct((n, d), F32), jax.ShapeDtypeStruct((n * SUBLANES, LANES), F32),
                   jax.ShapeDtypeStruct((n, LANES), jnp.int32), jax.ShapeDtypeStruct((n, LANES), F32)],
        compiler_params=_params("parallel"),
        name="post",
    )(o, x, gate, shift, scale, g2, wo_bf, wr_pad, br_pad)


def _expert_kernel(be_ref, na_ref, tok_ref, h3_ref, wgu_ref, bgu_ref, wd_ref, bd_ref, y_ref, buf, xb, sem):
    i = pl.program_id(0)
    n_active = na_ref[0]
    rows8 = MOE_ROWS * SUBLANES

    def gather(blk, slot):
        def body(r, carry):
            t = tok_ref[blk * MOE_ROWS + r]
            pltpu.make_async_copy(h3_ref.at[pl.ds(pl.multiple_of(t * SUBLANES, SUBLANES), SUBLANES), :],
                                  buf.at[slot, pl.ds(pl.multiple_of(r * SUBLANES, SUBLANES), SUBLANES), :],
                                  sem.at[slot]).start()
            return carry
        lax.fori_loop(0, MOE_ROWS, body, 0)

    @pl.when(i == 0)
    def _():
        gather(0, 0)

    @pl.when(i < n_active)
    def _():
        slot = i % 2
        pltpu.make_async_copy(h3_ref.at[pl.ds(0, rows8), :], buf.at[slot], sem.at[slot]).wait()

        @pl.when(i + 1 < n_active)
        def _():
            gather(i + 1, 1 - slot)

        d = xb.shape[1]
        for s in range(d // LANES):
            xb[:, s * LANES:(s + 1) * LANES] = buf[slot, pl.ds(s, MOE_ROWS, stride=SUBLANES), :].astype(BF16)
        gu = _dot(xb[...], wgu_ref[0]) + bgu_ref[0]
        f = gu.shape[1] // 2
        gt = jnp.minimum(gu[:, 0:f], SWIGLU_LIMIT)
        up = jnp.clip(gu[:, f:2 * f], -SWIGLU_LIMIT, SWIGLU_LIMIT)
        act = (up + 1.0) * (gt * jax.nn.sigmoid(SWIGLU_ALPHA * gt))
        y = _dot(act.astype(BF16), wd_ref[0]) + bd_ref[0]
        for s in range(d // LANES):
            y_ref[pl.ds(s, MOE_ROWS, stride=SUBLANES), :] = y[:, s * LANES:(s + 1) * LANES]

    @pl.when(i >= n_active)
    def _():
        y_ref[...] = jnp.zeros(y_ref.shape, F32)


def _experts(blk_e, n_active, tok, h3, wgu_bf, b_gu, wd_bf, b_down):
    n_blocks = blk_e.shape[0]
    e, d, f2 = wgu_bf.shape
    rows8 = MOE_ROWS * SUBLANES
    return pl.pallas_call(
        _expert_kernel,
        grid_spec=pltpu.PrefetchScalarGridSpec(
            num_scalar_prefetch=3,
            grid=(n_blocks,),
            in_specs=[pl.BlockSpec(memory_space=pl.ANY),
                      pl.BlockSpec((1, d, f2), lambda i, be, na, tok: (be[i], 0, 0)),
                      pl.BlockSpec((1, 1, f2), lambda i, be, na, tok: (be[i], 0, 0)),
                      pl.BlockSpec((1, f2 // 2, d), lambda i, be, na, tok: (be[i], 0, 0)),
                      pl.BlockSpec((1, 1, d), lambda i, be, na, tok: (be[i], 0, 0))],
            out_specs=pl.BlockSpec((rows8, LANES), lambda i, be, na, tok: (i, 0)),
            scratch_shapes=[pltpu.VMEM((2, rows8, LANES), F32), pltpu.VMEM((MOE_ROWS, d), BF16),
                            pltpu.SemaphoreType.DMA((2,))]),
        out_shape=jax.ShapeDtypeStruct((n_blocks * rows8, LANES), F32),
        compiler_params=_params("arbitrary"),
        name="experts",
    )(blk_e, n_active, tok, h3, wgu_bf, b_gu.reshape(e, 1, f2), wd_bf, b_down.reshape(e, 1, d))


def _dispatch_kernel(pos_ref, h3_ref, xs_in_ref, xs_ref, sem, *, tm, tile_off):
    del xs_in_ref
    i = pl.program_id(0)

    def body(r, carry):
        src = h3_ref.at[pl.ds(pl.multiple_of(r * SUBLANES, SUBLANES), SUBLANES), :]
        for k in range(TOP_K):
            p = pos_ref[((i + tile_off) * tm + r) * TOP_K + k]
            pltpu.make_async_copy(src, xs_ref.at[pl.ds(pl.multiple_of(p * SUBLANES, SUBLANES), SUBLANES), :],
                                  sem.at[0]).start()
        return carry
    lax.fori_loop(0, tm, body, 0)
    for k in range(TOP_K):
        pltpu.make_async_copy(h3_ref, xs_ref.at[pl.ds(0, tm * SUBLANES), :], sem.at[0]).wait()


def _dispatch(pos_flat, h3, xs, *, tm, tile_off):
    n8 = h3.shape[0]
    return pl.pallas_call(
        functools.partial(_dispatch_kernel, tm=tm, tile_off=tile_off),
        grid_spec=pltpu.PrefetchScalarGridSpec(
            num_scalar_prefetch=1,
            grid=(n8 // (tm * SUBLANES),),
            in_specs=[pl.BlockSpec((tm * SUBLANES, LANES), lambda i, pos: (i, 0)), pl.BlockSpec(memory_space=pl.ANY)],
            out_specs=pl.BlockSpec(memory_space=pl.ANY),
            scratch_shapes=[pltpu.SemaphoreType.DMA((1,))]),
        out_shape=jax.ShapeDtypeStruct(xs.shape, xs.dtype),
        input_output_aliases={2: 0},
        compiler_params=_params("arbitrary"),
        name="dispatch",
    )(pos_flat, h3, xs)


def _expert_block_kernel(be_ref, na_ref, x_ref, wgu_ref, bgu_ref, wd_ref, bd_ref, y_ref, xb, wgu_bf, wd_bf):
    i = pl.program_id(0)
    active = i < na_ref[0]

    @pl.when(active & ((i == 0) | (be_ref[i] != be_ref[jnp.maximum(i - 1, 0)])))
    def _():
        for r in range(0, wgu_bf.shape[0], LANES):
            wgu_bf[r:r + LANES, :] = wgu_ref[0, r:r + LANES, :].astype(BF16)
        for r in range(0, wd_bf.shape[0], LANES):
            wd_bf[r:r + LANES, :] = wd_ref[0, r:r + LANES, :].astype(BF16)

    @pl.when(active)
    def _():
        d = xb.shape[1]
        for s in range(d // LANES):
            xb[:, s * LANES:(s + 1) * LANES] = x_ref[pl.ds(s, MOE_ROWS, stride=SUBLANES), :].astype(BF16)
        gu = _dot(xb[...], wgu_bf[...]) + bgu_ref[0]
        f = gu.shape[1] // 2
        gt = jnp.minimum(gu[:, 0:f], SWIGLU_LIMIT)
        up = jnp.clip(gu[:, f:2 * f], -SWIGLU_LIMIT, SWIGLU_LIMIT)
        act = (up + 1.0) * (gt * jax.nn.sigmoid(SWIGLU_ALPHA * gt))
        y = _dot(act.astype(BF16), wd_bf[...]) + bd_ref[0]
        for s in range(d // LANES):
            y_ref[pl.ds(s, MOE_ROWS, stride=SUBLANES), :] = y[:, s * LANES:(s + 1) * LANES]

    @pl.when(i >= na_ref[0])
    def _():
        y_ref[...] = jnp.zeros(y_ref.shape, F32)


def _expert_blocks(blk_e, n_active, xs, w_gu, b_gu, w_down, b_down):
    n_blocks = blk_e.shape[0]
    e, d, f2 = w_gu.shape
    rows8 = MOE_ROWS * SUBLANES
    row_map = lambda i, be, na: (jnp.minimum(i, na[0] - 1), 0)
    return pl.pallas_call(
        _expert_block_kernel,
        grid_spec=pltpu.PrefetchScalarGridSpec(
            num_scalar_prefetch=2,
            grid=(n_blocks,),
            in_specs=[pl.BlockSpec((rows8, LANES), row_map),
                      pl.BlockSpec((1, d, f2), lambda i, be, na: (be[i], 0, 0)),
                      pl.BlockSpec((1, 1, f2), lambda i, be, na: (be[i], 0, 0)),
                      pl.BlockSpec((1, f2 // 2, d), lambda i, be, na: (be[i], 0, 0)),
                      pl.BlockSpec((1, 1, d), lambda i, be, na: (be[i], 0, 0))],
            out_specs=pl.BlockSpec((rows8, LANES), lambda i, be, na: (i, 0)),
            scratch_shapes=[pltpu.VMEM((MOE_ROWS, d), BF16), pltpu.VMEM((d, f2), BF16),
                            pltpu.VMEM((f2 // 2, d), BF16)]),
        out_shape=jax.ShapeDtypeStruct((n_blocks * rows8, LANES), F32),
        compiler_params=_params("arbitrary"),
        name="experts",
    )(blk_e, n_active, xs, w_gu, b_gu.reshape(e, 1, f2), w_down, b_down.reshape(e, 1, d))


def _routing_pos(top_e, n_tok):
    n_assign = n_tok * TOP_K
    n_pad = -(-n_assign // LANES) * LANES
    e_flat = jnp.pad(top_e.reshape(-1), (0, n_pad - n_assign), constant_values=N_EXPERTS)
    onehot = (e_flat[:, None] == jnp.arange(N_EXPERTS)[None, :]).astype(F32).reshape(n_pad // LANES, LANES, N_EXPERTS)
    tril = jnp.tril(jnp.ones((LANES, LANES), F32))
    within = jnp.einsum("ij,bjk->bik", tril, onehot)
    block_tot = within[:, -1, :]
    offs = jnp.cumsum(block_tot, axis=0) - block_tot
    counts = jnp.sum(block_tot, axis=0).astype(jnp.int32)
    padded = (counts + MOE_ROWS - 1) // MOE_ROWS * MOE_ROWS
    pad_end = jnp.cumsum(padded)
    pad_start = (pad_end - padded).astype(F32)
    slot = jnp.sum(onehot * (within + offs[:, None, :] - 1.0 + pad_start[None, None, :]), axis=-1)
    pos = slot.reshape(-1)[:n_assign].astype(jnp.int32)
    n_blocks = -(-n_assign // MOE_ROWS) + N_EXPERTS
    starts = jnp.arange(n_blocks, dtype=jnp.int32) * MOE_ROWS
    blk_e = jnp.minimum(jnp.sum((pad_end[None, :] <= starts[:, None]).astype(jnp.int32), axis=1), N_EXPERTS - 1)
    n_active = (pad_end[-1] // MOE_ROWS).astype(jnp.int32).reshape(1)
    return blk_e, n_active, pos


def _combine_kernel(pos_ref, ys_ref, y1_ref, gt_ref, tw_ref, o_ref, buf, sem, *, tile_off):
    i = pl.program_id(0)
    n = pl.num_programs(0)
    tm = y1_ref.shape[0]
    rows8 = tm * SUBLANES

    def gather(tile, slot):
        def body(r, carry):
            for k in range(TOP_K):
                p = pos_ref[((tile + tile_off) * tm + r) * TOP_K + k]
                pltpu.make_async_copy(ys_ref.at[pl.ds(pl.multiple_of(p * SUBLANES, SUBLANES), SUBLANES), :],
                                      buf.at[slot, k, pl.ds(pl.multiple_of(r * SUBLANES, SUBLANES), SUBLANES), :],
                                      sem.at[slot]).start()
            return carry
        lax.fori_loop(0, tm, body, 0)

    @pl.when(i == 0)
    def _():
        gather(0, 0)

    slot = i % 2
    for k in range(TOP_K):
        pltpu.make_async_copy(ys_ref.at[pl.ds(0, rows8), :], buf.at[slot, k], sem.at[slot]).wait()

    @pl.when(i + 1 < n)
    def _():
        gather(i + 1, 1 - slot)

    tw = tw_ref[...]
    wk = [jnp.broadcast_to(tw[:, k:k + 1], (tm, LANES)) for k in range(TOP_K)]
    gt = gt_ref[0]
    for s in range(o_ref.shape[1] // LANES):
        moe = wk[0] * buf[slot, 0, pl.ds(s, tm, stride=SUBLANES), :]
        for k in range(1, TOP_K):
            moe = moe + wk[k] * buf[slot, k, pl.ds(s, tm, stride=SUBLANES), :]
        cols = slice(s * LANES, (s + 1) * LANES)
        o_ref[:, cols] = y1_ref[:, cols] + gt[:, cols] * moe


def _combine(pos_flat, ys, y1, gate, tw, *, tm, rows_per_mod, tile_off):
    n, d = y1.shape
    mod_r = gate.shape[1]
    return pl.pallas_call(
        functools.partial(_combine_kernel, tile_off=tile_off),
        grid_spec=pltpu.PrefetchScalarGridSpec(
            num_scalar_prefetch=1,
            grid=(n // tm,),
            in_specs=[pl.BlockSpec(memory_space=pl.ANY),
                      pl.BlockSpec((tm, d), lambda i, pos: (i, 0)),
                      pl.BlockSpec((1, mod_r, d), lambda i, pos: (i // (rows_per_mod // tm), 0, 0)),
                      pl.BlockSpec((tm, LANES), lambda i, pos: (i, 0))],
            out_specs=pl.BlockSpec((tm, d), lambda i, pos: (i, 0)),
            scratch_shapes=[pltpu.VMEM((2, TOP_K, tm * SUBLANES, LANES), F32), pltpu.SemaphoreType.DMA((2,))]),
        out_shape=jax.ShapeDtypeStruct((n, d), F32),
        compiler_params=_params("arbitrary"),
        name="combine",
    )(pos_flat, ys, y1, gate, tw)


def _rope_tables(pos):
    half = HEAD_DIM // 2
    inv = ROPE_THETA ** (-jnp.arange(half, dtype=F32) / half)
    ang = pos.astype(F32)[:, None] * inv[None, :]
    cos = jnp.cos(ang)
    sin = jnp.sin(ang)
    reps = LANES // HEAD_DIM
    return (jnp.tile(jnp.concatenate([cos, cos], axis=1), (1, reps)),
            jnp.tile(jnp.concatenate([-sin, sin], axis=1), (1, reps)))


def _routing(top_e, n_tok):
    n_assign = n_tok * TOP_K
    e_flat = top_e.reshape(-1)
    order = jnp.argsort(e_flat)
    e_sorted = e_flat[order]
    tok_sorted = (order // TOP_K).astype(jnp.int32)
    counts = jnp.bincount(e_flat, length=N_EXPERTS)
    padded = (counts + MOE_ROWS - 1) // MOE_ROWS * MOE_ROWS
    pad_end = jnp.cumsum(padded)
    pad_start = pad_end - padded
    sort_start = jnp.cumsum(counts) - counts
    dest = (pad_start[e_sorted] + jnp.arange(n_assign, dtype=jnp.int32) - sort_start[e_sorted]).astype(jnp.int32)
    n_blocks = -(-n_assign // MOE_ROWS) + N_EXPERTS
    tok = jnp.full((n_blocks * MOE_ROWS,), n_tok, jnp.int32).at[dest].set(tok_sorted)
    blk_e = jnp.minimum(jnp.searchsorted(pad_end, jnp.arange(n_blocks, dtype=jnp.int32) * MOE_ROWS, side='right'),
                        N_EXPERTS - 1).astype(jnp.int32)
    pos = jnp.zeros((n_assign,), jnp.int32).at[order].set(dest)
    n_active = (pad_end[-1] // MOE_ROWS).astype(jnp.int32).reshape(1)
    return blk_e, n_active, tok, pos


def _ones_pad(a):
    return jnp.broadcast_to((jnp.arange(a.shape[-1]) == 0).astype(a.dtype), a.shape)


def _head_ones(a):
    return jnp.concatenate([a, _ones_pad(a)], axis=-1)


def _head_pad(a):
    return jnp.concatenate([a, jnp.zeros_like(a)], axis=-1)


def kernel(x_prompt, x_sample, c_prompt, c_sample, cache_moba_kv, cache_nsa_kv, state_nsa_win_kv, page_table,
           norm_g, w_ada, b_ada, w_in, qk_gain, cmp_pos, cmp_w1, cmp_w2, w_out, w_router, b_router, w_gu, b_gu,
           w_down, b_down):
    bsz, seq, d = x_prompt.shape
    n_req = x_sample.shape[0]
    depth = norm_g.shape[0]
    assert depth == 1 and x_sample.shape[1] == 1
    assert seq % ATT_TK == 0 and seq >= WINDOW + Q_TILE and seq // SEL_BLOCK <= LANES and seq // MOBA_BLOCK <= MOBA_BLOCK // SUBLANES
    n_pool, page = cache_moba_kv.shape[1], cache_moba_kv.shape[2]
    n_pages = page_table.shape[1]
    past = n_pages * page
    assert past % MOBA_BLOCK == 0 and 2 * page == MOBA_BLOCK and past // SEL_BLOCK < LANES
    assert state_nsa_win_kv.shape[2] == WINDOW
    layer = 0
    gw = G_NSA * HEAD_DIM
    n_prompt = bsz * seq

    bd = jnp.asarray(np.kron(np.eye(LANES // HEAD_DIM), np.full((HEAD_DIM, HEAD_DIM), 1.0 / HEAD_DIM)), BF16)
    w_in_bf = jnp.pad(w_in[layer], ((0, 0), (0, IN_COLS_PAD - IN_COLS))).astype(BF16)
    gains = jnp.tile(qk_gain[layer], (1, W_QM // HEAD_DIM))
    g1 = norm_g[layer, 0].reshape(1, d)
    g2 = norm_g[layer, 1].reshape(1, d)
    wo_bf = w_out[layer].astype(BF16)
    wr_pad = jnp.pad(w_router[layer], ((0, 0), (0, LANES - N_EXPERTS)))
    br_pad = jnp.pad(b_router[layer].reshape(1, N_EXPERTS), ((0, 0), (0, LANES - N_EXPERTS)),
                     constant_values=-jnp.inf)
    cmp_consts = _compress_consts(cmp_pos[layer], cmp_w1[layer], cmp_w2[layer], qk_gain[layer, 3])

    n_c = bsz + n_req
    n_c_pad = -(-n_c // SUBLANES) * SUBLANES
    c_all = jnp.pad(jnp.concatenate([c_prompt, c_sample], axis=0), ((0, n_c_pad - n_c), (0, 0)))
    mods = _ada(c_all, w_ada[layer], b_ada[layer])
    mods_p = [m.reshape(bsz, 1, d) for m in jnp.split(mods[:bsz], 6, axis=1)]
    mods_s = [m.reshape(1, n_req, d) for m in jnp.split(mods[bsz:n_c], 6, axis=1)]

    cos_p, sin_p = _rope_tables(jnp.arange(seq, dtype=jnp.int32))
    tm_p = 256
    (moba_t, nsa_t, win_t, q_m, qn, qr, _, kmean, cmp_raw, k_aug, v_aug, ks_aug, vs_aug, kw_pad, vw_aug,
     gates_g) = _proj(
        x_prompt.reshape(n_prompt, d), mods_p[0], mods_p[1], g1, w_in_bf, gains, bd, cos_p, sin_p,
        tm=tm_p, rows_per_mod=seq, pos_blocks=seq // tm_p, with_kmean=True)

    nbk = seq // MOBA_BLOCK
    kmean_h = kmean.reshape(bsz, nbk, H_MOBA, HEAD_DIM).transpose(0, 2, 1, 3)
    kmp = jnp.zeros((bsz, H_MOBA, LANES, LANES), F32).at[:, :, HEAD_DIM:HEAD_DIM + nbk, :HEAD_DIM].set(kmean_h)
    per_b = lambda a: a.reshape(bsz, seq, a.shape[-1])
    o_m = _moba_attention(per_b(q_m), kmp, per_b(k_aug), per_b(v_aug))

    kc, vc = _compress_prompt(per_b(cmp_raw), cmp_consts, bd)
    n_chunk = seq // CMP_STRIDE
    per_g = lambda a: a.reshape(bsz, -1, G_NSA, HEAD_DIM).transpose(0, 2, 1, 3)
    kcp = _head_pad(per_g(kc))
    vcd = _head_pad(per_g(vc)).astype(BF16)
    cover_p = _cover(n_chunk - 1, seq // SEL_BLOCK, n_chunk).T.astype(BF16)
    o_n = _nsa_attention(per_b(qn), per_b(qr), per_b(gates_g), kcp, vcd, cover_p, per_b(ks_aug), per_b(vs_aug),
                         per_b(kw_pad), per_b(vw_aug))
    o_p = jnp.concatenate([o_m, o_n], axis=-1).reshape(n_prompt, d)

    y1_p, h3_p, te_p, tw_p = _post(o_p, x_prompt.reshape(n_prompt, d), mods_p[2], mods_p[3], mods_p[4], g2, wo_bf,
                                   wr_pad, br_pad, tm=256, rows_per_mod=seq)

    cos_s, sin_s = _rope_tables(jnp.full((n_req,), past, jnp.int32))
    moba_new, nsa_new, win_new, q_m_s, qn_s, qr_s, gates_s = _proj(
        x_sample.reshape(n_req, d), mods_s[0], mods_s[1], g1, w_in_bf, gains, bd, cos_s, sin_s,
        tm=n_req, rows_per_mod=n_req, pos_blocks=1, with_kmean=False)
    pt_flat = page_table.reshape(-1).astype(jnp.int32)
    cache_m = cache_moba_kv[layer].transpose(0, 2, 3, 4, 1).reshape(n_pool, W_KVM, page)
    cache_n = cache_nsa_kv[layer].transpose(0, 2, 3, 4, 1).reshape(n_pool, 4 * gw, page)
    win_buf = state_nsa_win_kv[layer].transpose(0, 2, 3, 4, 1).reshape(n_req, 2 * gw, WINDOW)

    def rows8(q):
        qh = q.reshape(n_req, G_NSA, R_NSA, 1, HEAD_DIM)
        place = jnp.arange(G_NSA)[None, :, None, None, None] == jnp.arange(G_NSA)[None, None, None, :, None]
        return jnp.where(place, qh, 0.0).reshape(n_req, H_NSA, gw)

    def per_head(a):
        return jnp.repeat(a.reshape(n_req, G_NSA, HEAD_DIM), R_NSA, axis=1).reshape(n_req, W_QN)

    new_rows = jnp.stack([moba_new[:, :W_QM], moba_new[:, W_QM:], per_head(nsa_new[:, 2 * gw:3 * gw]),
                          per_head(nsa_new[:, 3 * gw:]), per_head(win_new[:, :gw]), per_head(win_new[:, gw:])], axis=1)
    gate_rows = jnp.repeat(gates_s[:, :N_GATE].reshape(n_req, H_NSA, 3).transpose(0, 2, 1), HEAD_DIM, axis=2)
    n_cmp_s = past // CMP_STRIDE
    cover_s = _cover(n_cmp_s - 1, past // SEL_BLOCK + 1, n_cmp_s).T
    wcat, pecat = _compress_cat_consts(cmp_pos[layer], cmp_w1[layer])
    o_s = _dec2_attention(cache_m, cache_n, win_buf, pt_flat, jnp.stack([q_m_s, qr_s], axis=1), rows8(qn_s), new_rows,
                          gate_rows, wcat, pecat, cmp_consts[2], cmp_consts[3], bd, cover_s,
                          n_req=n_req, n_pages=n_pages).reshape(n_req, d).astype(BF16)
    y1_s, h3_s, te_s, tw_s = _post(o_s, x_sample.reshape(n_req, d), mods_s[2], mods_s[3], mods_s[4], g2, wo_bf,
                                   wr_pad, br_pad, tm=n_req, rows_per_mod=n_req)

    n_tok = n_prompt + n_req
    top_e = jnp.concatenate([te_p[:, :TOP_K], te_s[:, :TOP_K]], axis=0)
    blk_e, n_active, pos_flat = _routing_pos(top_e, n_tok)
    tm_c = 128
    xs = jnp.zeros((blk_e.shape[0] * MOE_ROWS * SUBLANES, LANES), F32)
    xs = _dispatch(pos_flat, h3_p, xs, tm=2 * tm_c, tile_off=0)
    xs = _dispatch(pos_flat, h3_s, xs, tm=n_req, tile_off=n_prompt // n_req)
    ys = _expert_blocks(blk_e, n_active, xs, w_gu[layer], b_gu[layer], w_down[layer], b_down[layer])
    y_p = _combine(pos_flat, ys, y1_p, mods_p[5], tw_p, tm=tm_c, rows_per_mod=seq, tile_off=0)
    y_s = _combine(pos_flat, ys, y1_s, mods_s[5], tw_s, tm=n_req, rows_per_mod=n_req, tile_off=n_prompt // n_req)

    def rows_view(t, n_slot, n_head):
        return t.reshape(1, bsz, n_slot, n_head, HEAD_DIM, t.shape[-1]).transpose(0, 1, 5, 2, 3, 4)

    keep = min(WINDOW, seq)
    win_s = jnp.concatenate([state_nsa_win_kv[layer][:, 1:], win_new.reshape(n_req, 1, 2, G_NSA, HEAD_DIM)], axis=1)
    return (y_p.reshape(bsz, seq, d), y_s.reshape(n_req, 1, d),
            rows_view(moba_t, 2, H_MOBA), rows_view(nsa_t, 4, G_NSA), rows_view(win_t[:, :, seq - keep:], 2, G_NSA),
            moba_new.reshape(1, n_req, 1, 2, H_MOBA, HEAD_DIM),
            nsa_new.reshape(1, n_req, 1, 4, G_NSA, HEAD_DIM), win_s[None])
```

```python
import functools

import numpy as np
import jax
import jax.numpy as jnp
from jax import lax
from jax.experimental import pallas as pl
from jax.experimental.pallas import tpu as pltpu

F32 = jnp.float32
BF16 = jnp.bfloat16
HIGHEST = lax.Precision.HIGHEST

LANES = 128
SUBLANES = 8
HEAD_DIM = 64
H_MOBA = 8
H_NSA = 8
G_NSA = 2
R_NSA = H_NSA // G_NSA
MOBA_BLOCK = 256
MOBA_TOPK = 3
CMP_LEN = 32
CMP_STRIDE = 16
SEL_BLOCK = 64
SEL_TOPN = 16
N_FORCED = 3
WINDOW = 512
N_EXPERTS = 32
TOP_K = 4
SWIGLU_LIMIT = 7.0
SWIGLU_ALPHA = 1.702
ROPE_THETA = 10000.0
NORM_EPS = 1e-6
NEG_BIG = -1e30
TINY = 1e-30
SCALE = HEAD_DIM ** -0.5
SCALE_LOG2E = SCALE * 1.4426950408889634
Q_TILE = 256
MOBA_Q = 1024
MOBA_HEADS = 2
ATT_TK = 1024
MOE_ROWS = 256
VMEM_LIMIT = 56 * 1024 * 1024

W_QM = H_MOBA * HEAD_DIM
W_KVM = 2 * H_MOBA * HEAD_DIM
W_QN = H_NSA * HEAD_DIM
W_KVN = 6 * G_NSA * HEAD_DIM
N_GATE = 3 * H_NSA
IN_COLS = W_QM + W_KVM + W_QN + W_KVN + N_GATE
IN_COLS_PAD = W_QM + W_KVM + W_QN + W_KVN + LANES


def _iota(shape, dim):
    return lax.broadcasted_iota(jnp.int32, shape, dim)


def _dot(a, b, precision=None):
    return jnp.dot(a, b, preferred_element_type=F32, precision=precision)


def _dot_nt(a, b, precision=None):
    return lax.dot_general(a, b, (((1,), (1,)), ((), ())), preferred_element_type=F32, precision=precision)


def _split_bf16(a):
    hi = a.astype(BF16)
    return hi, (a - hi.astype(F32)).astype(BF16)


def _dot_nt_x3(a, b):
    ah, al = _split_bf16(a)
    bh, bl = _split_bf16(b)
    return _dot_nt(ah, bh) + _dot_nt(al, bh) + _dot_nt(ah, bl)


def _params(*sem):
    return pltpu.CompilerParams(dimension_semantics=sem, vmem_limit_bytes=VMEM_LIMIT)


def _seg_meansq(z, bd):
    zz = z * z
    hi = zz.astype(BF16)
    lo = (zz - hi.astype(F32)).astype(BF16)
    outs = []
    for c in range(z.shape[1] // LANES):
        sl = slice(c * LANES, (c + 1) * LANES)
        outs.append(_dot(hi[:, sl], bd) + _dot(lo[:, sl], bd))
    return outs[0] if len(outs) == 1 else jnp.concatenate(outs, axis=1)


def _head_norm(z, gain, bd):
    return z * lax.rsqrt(_seg_meansq(z, bd) + NORM_EPS) * gain


def _rope(z, cos, sin):
    outs = []
    first = (_iota((z.shape[0], LANES), 1) % HEAD_DIM) < (HEAD_DIM // 2)
    for c in range(z.shape[1] // LANES):
        x = z[:, c * LANES:(c + 1) * LANES]
        swapped = jnp.where(first, pltpu.roll(x, LANES - HEAD_DIM // 2, 1), pltpu.roll(x, HEAD_DIM // 2, 1))
        outs.append(x * cos + swapped * sin)
    return outs[0] if len(outs) == 1 else jnp.concatenate(outs, axis=1)


def _top_k_lanes(cur, k):
    lane = _iota(cur.shape, 1).astype(F32)
    picked = jnp.zeros(cur.shape, F32)
    vals, ids = [], []
    for _ in range(k):
        mx = jnp.max(cur, axis=1, keepdims=True)
        first = jnp.min(jnp.where(cur == mx, lane, 1e9), axis=1, keepdims=True)
        hit = lane == first
        picked = jnp.where(hit, 1.0, picked)
        cur = jnp.where(hit, -jnp.inf, cur)
        vals.append(mx)
        ids.append(first)
    return picked, vals, ids


def _top_k_sublanes(cur, k):
    idx = _iota(cur.shape, 0).astype(F32)
    picked = jnp.zeros(cur.shape, F32)
    for _ in range(k):
        mx = jnp.max(cur, axis=0, keepdims=True)
        first = jnp.min(jnp.where(cur == mx, idx, 1e9), axis=0, keepdims=True)
        hit = idx == first
        picked = jnp.where(hit, 1.0, picked)
        cur = jnp.where(hit, -jnp.inf, cur)
    return picked


def _flash_step(q, k, v, mask, m, acc):
    s = _dot_nt(q, k)
    if mask is not None:
        s = jnp.where(mask, s, NEG_BIG)
    m_new = jnp.maximum(m, jnp.max(s, axis=1, keepdims=True))
    acc_new = jnp.exp2(m - m_new) * acc + _dot(jnp.exp2(s - m_new).astype(BF16), v)
    return m_new, acc_new


def _flash_finish(acc):
    return acc / jnp.maximum(acc[:, HEAD_DIM:HEAD_DIM + 1], TINY)


def _ada_kernel(c_ref, w_ref, b_ref, o_ref):
    c = c_ref[...]
    o_ref[...] = _dot(c * jax.nn.sigmoid(c), w_ref[...], HIGHEST) + b_ref[...]


def _ada(c_all, w_ada, b_ada):
    n, d = c_all.shape
    cols = w_ada.shape[1]
    tn = 1024
    return pl.pallas_call(
        _ada_kernel,
        grid=(cols // tn,),
        in_specs=[pl.BlockSpec((n, d), lambda j: (0, 0)),
                  pl.BlockSpec((d, tn), lambda j: (0, j)),
                  pl.BlockSpec((1, tn), lambda j: (0, j))],
        out_specs=pl.BlockSpec((n, tn), lambda j: (0, j)),
        out_shape=jax.ShapeDtypeStruct((n, cols), F32),
        compiler_params=_params("arbitrary"),
        name="ada",
    )(c_all, w_ada, b_ada.reshape(1, cols))


def _proj_kernel(x_ref, sh_ref, sc_ref, g_ref, w_ref, gains_ref, bd_ref, cos_ref, sin_ref,
                 moba_ref, nsa_ref, win_ref, qm_ref, qn_ref, qr_ref, gate_ref, *attn_refs, with_kmean, pos_blocks):
    x = x_ref[...]
    y = x * lax.rsqrt(jnp.mean(x * x, axis=1, keepdims=True) + NORM_EPS) * g_ref[...]
    h = (y * (1.0 + sc_ref[0]) + sh_ref[0]).astype(BF16)
    bd = bd_ref[...]
    cos = cos_ref[...]
    sin = sin_ref[...]
    o = 0

    def seg(width):
        nonlocal o
        z = _dot(h, w_ref[:, o:o + width])
        o += width
        return z

    def gain(i, width):
        return gains_ref[i:i + 1, 0:width]

    qm_ref[...] = _rope(_head_norm(seg(W_QM), gain(0, W_QM), bd), cos, sin)
    k_m = _rope(_head_norm(seg(W_QM), gain(1, W_QM), bd), cos, sin)
    v_m = seg(W_QM)
    qn = _head_norm(seg(W_QN), gain(2, W_QN), bd)
    qn_ref[...] = qn
    qr_ref[...] = _rope(qn, cos, sin)
    gw = G_NSA * HEAD_DIM
    cmp_raw = seg(2 * gw)
    k_sel = _rope(_head_norm(seg(gw), gain(4, gw), bd), cos, sin)
    v_sel = seg(gw)
    k_win = _rope(_head_norm(seg(gw), gain(5, gw), bd), cos, sin)
    v_win = seg(gw)
    gates = jax.nn.sigmoid(seg(LANES))
    gate_ref[...] = gates
    moba_rows = jnp.concatenate([k_m, v_m], axis=1)
    nsa_rows = jnp.concatenate([cmp_raw, k_sel, v_sel], axis=1)
    win_rows = jnp.concatenate([k_win, v_win], axis=1)
    if not with_kmean:
        moba_ref[...] = moba_rows
        nsa_ref[...] = nsa_rows
        win_ref[...] = win_rows
    else:
        kmean_ref, cmpraw_ref, kaug_ref, vaug_ref, ksaug_ref, vsaug_ref, kwp_ref, vwaug_ref, gg_ref = attn_refs
        moba_ref[0] = moba_rows.T
        nsa_ref[0] = nsa_rows.T
        win_ref[0] = win_rows.T
        cmpraw_ref[...] = cmp_raw
        tm = k_m.shape[0]
        kmean_ref[0] = jnp.mean(k_m.reshape(tm // MOBA_BLOCK, MOBA_BLOCK, W_QM), axis=1)
        lane = _iota((tm, LANES), 1)
        posv = (pl.program_id(0) % pos_blocks) * tm + _iota((tm, LANES), 0)
        lo = lane < HEAD_DIM
        pad_blk = jnp.where((lane >= HEAD_DIM) & (lane - HEAD_DIM == posv // MOBA_BLOCK), 1.0, 0.0)
        pad_one = jnp.where(lane == HEAD_DIM, 1.0, 0.0)
        oh_sel = jnp.where(lane == posv // SEL_BLOCK, 1.0, 0.0).astype(BF16)

        def halves(x):
            return x, pltpu.roll(x, HEAD_DIM, 1)

        for c in range(W_QM // LANES):
            cols = slice(c * LANES, (c + 1) * LANES)
            for hh, (kh, vh) in enumerate(zip(halves(k_m[:, cols]), halves(v_m[:, cols]))):
                hcols = slice((2 * c + hh) * LANES, (2 * c + hh + 1) * LANES)
                kaug_ref[:, hcols] = jnp.where(lo, kh, pad_blk).astype(BF16)
                vaug_ref[:, hcols] = jnp.where(lo, vh, pad_one).astype(BF16)
        for g, (ks, vs, kw, vw) in enumerate(zip(halves(k_sel), halves(v_sel), halves(k_win), halves(v_win))):
            ksaug_ref[:, 2 * g * LANES:(2 * g + 1) * LANES] = oh_sel
            ksaug_ref[:, (2 * g + 1) * LANES:(2 * g + 2) * LANES] = jnp.where(lo, ks, 0.0).astype(BF16)
            gcols = slice(g * LANES, (g + 1) * LANES)
            vsaug_ref[:, gcols] = jnp.where(lo, vs, pad_one).astype(BF16)
            kwp_ref[:, gcols] = jnp.where(lo, kw, 0.0).astype(BF16)
            vwaug_ref[:, gcols] = jnp.where(lo, vw, pad_one).astype(BF16)
            gg_ref[:, gcols] = gates if g == 0 else pltpu.roll(gates, LANES - g * 3 * R_NSA, 1)


def _proj(x, shift, scale, g, w_in_bf, gains, bd, cos, sin, *, tm, rows_per_mod, pos_blocks, with_kmean):
    n, d = x.shape
    nt = n // tm
    mod_r = shift.shape[1]
    mod_map = lambda i: (i // (rows_per_mod // tm), 0, 0)
    pos_map = lambda i: (i % pos_blocks, 0)
    row = lambda w: pl.BlockSpec((tm, w), lambda i: (i, 0))
    const = lambda a: pl.BlockSpec(a.shape, lambda i: (0,) * a.ndim)
    gw = G_NSA * HEAD_DIM
    cache_widths = (W_KVM, 4 * gw, 2 * gw)
    if with_kmean:
        nb = n // rows_per_mod
        tpb = rows_per_mod // tm
        out_shapes = [jax.ShapeDtypeStruct((nb, w, rows_per_mod), F32) for w in cache_widths]
        out_specs = [pl.BlockSpec((1, w, tm), lambda i: (i // tpb, 0, i % tpb)) for w in cache_widths]
    else:
        out_shapes = [jax.ShapeDtypeStruct((n, w), F32) for w in cache_widths]
        out_specs = [row(w) for w in cache_widths]
    out_shapes += [jax.ShapeDtypeStruct((n, W_QM), F32), jax.ShapeDtypeStruct((n, W_QN), F32),
                   jax.ShapeDtypeStruct((n, W_QN), F32), jax.ShapeDtypeStruct((n, LANES), F32)]
    out_specs += [row(W_QM), row(W_QN), row(W_QN), row(LANES)]
    if with_kmean:
        nbt = tm // MOBA_BLOCK
        out_shapes.append(jax.ShapeDtypeStruct((nt, nbt, W_QM), F32))
        out_specs.append(pl.BlockSpec((1, nbt, W_QM), lambda i: (i, 0, 0)))
        gl = G_NSA * LANES
        for width, dtype in ((2 * gw, F32), (H_MOBA * LANES, BF16), (H_MOBA * LANES, BF16), (2 * gl, BF16), (gl, BF16),
                             (gl, BF16), (gl, BF16), (gl, F32)):
            out_shapes.append(jax.ShapeDtypeStruct((n, width), dtype))
            out_specs.append(row(width))
    return pl.pallas_call(
        functools.partial(_proj_kernel, with_kmean=with_kmean, pos_blocks=pos_blocks),
        grid=(nt,),
        in_specs=[row(d), pl.BlockSpec((1, mod_r, d), mod_map), pl.BlockSpec((1, mod_r, d), mod_map),
                  const(g), const(w_in_bf), const(gains), const(bd),
                  pl.BlockSpec((tm, LANES), pos_map), pl.BlockSpec((tm, LANES), pos_map)],
        out_specs=out_specs,
        out_shape=out_shapes,
        compiler_params=_params("parallel"),
        name="proj",
    )(x, shift, scale, g, w_in_bf, gains, bd, cos, sin)


def _compress_compute(src_refs, pe_ref, w1_ref, w2_ref, gk_ref, bd_ref, kc_ref, vc_ref, n_rows):
    n_chunk = n_rows // CMP_STRIDE
    gw = G_NSA * HEAD_DIM
    for kv in range(2):
        acc_a = jnp.zeros((n_chunk, gw), F32)
        acc_b = jnp.zeros((n_chunk, gw), F32)
        for j in range(CMP_STRIDE):
            xj = src_refs[kv][pl.ds(j, n_chunk, stride=CMP_STRIDE), :]
            acc_a = acc_a + _dot(xj + pe_ref[kv, 0, j:j + 1, :], w1_ref[kv, 0, j], HIGHEST)
            acc_b = acc_b + _dot(xj + pe_ref[kv, 1, j:j + 1, :], w1_ref[kv, 1, j], HIGHEST)
        hid = jax.nn.gelu(acc_a + pltpu.roll(acc_b, n_chunk - 1, 0))
        out = _dot(hid, w2_ref[kv], HIGHEST)
        if kv == 0:
            kc_ref[0] = _head_norm(out, gk_ref[...], bd_ref[...])
        else:
            vc_ref[0] = out


def _compress_prompt_kernel(k_ref, v_ref, pe_ref, w1_ref, w2_ref, gk_ref, bd_ref, kc_ref, vc_ref, *, n_rows):
    _compress_compute((k_ref.at[0], v_ref.at[0]), pe_ref, w1_ref, w2_ref, gk_ref, bd_ref, kc_ref, vc_ref, n_rows)


def _compress_consts(cmp_pos, cmp_w1, cmp_w2, gain_k_cmp):
    pe = jnp.tile(cmp_pos.reshape(2, 2, CMP_STRIDE, HEAD_DIM), (1, 1, 1, G_NSA))
    eye = jnp.eye(G_NSA, dtype=F32)
    w1 = cmp_w1.reshape(2, 2, CMP_STRIDE, HEAD_DIM, HEAD_DIM)
    w1bd = jnp.einsum("gh,kajde->kajgdhe", eye, w1).reshape(2, 2, CMP_STRIDE, G_NSA * HEAD_DIM, G_NSA * HEAD_DIM)
    w2bd = jnp.einsum("gh,kde->kgdhe", eye, cmp_w2).reshape(2, G_NSA * HEAD_DIM, G_NSA * HEAD_DIM)
    gk = jnp.tile(gain_k_cmp.reshape(1, HEAD_DIM), (1, G_NSA))
    return pe, w1bd, w2bd, gk


def _compress_prompt(cmp_raw, consts, bd):
    b, s, _ = cmp_raw.shape
    pe, w1bd, w2bd, gk = consts
    n_chunk = s // CMP_STRIDE
    gw = G_NSA * HEAD_DIM
    const = lambda a: pl.BlockSpec(a.shape, lambda i: (0,) * a.ndim)
    out = jax.ShapeDtypeStruct((b, n_chunk, gw), F32)
    return pl.pallas_call(
        functools.partial(_compress_prompt_kernel, n_rows=s),
        grid=(b,),
        in_specs=[pl.BlockSpec((1, s, gw), lambda i: (i, 0, 0)), pl.BlockSpec((1, s, gw), lambda i: (i, 0, 1)),
                  const(pe), const(w1bd), const(w2bd), const(gk), const(bd)],
        out_specs=[pl.BlockSpec((1, n_chunk, gw), lambda i: (i, 0, 0))] * 2,
        out_shape=[out, out],
        compiler_params=_params("parallel"),
        name="compress_prompt",
    )(cmp_raw, cmp_raw, pe, w1bd, w2bd, gk, bd)


def _moba_kernel(q_ref, kmp_ref, k_ref, v_ref, o_ref):
    qi = pl.program_id(2)
    lane = _iota((MOBA_Q, LANES), 1)
    n_blk = MOBA_BLOCK // SUBLANES
    blk = _iota((n_blk, MOBA_Q), 0)
    own = (qi * MOBA_Q + _iota((n_blk, MOBA_Q), 1)) // MOBA_BLOCK
    valid = blk < own
    q_augs = []
    for h in range(MOBA_HEADS):
        q2 = q_ref[0, :, (h // 2) * LANES:(h // 2 + 1) * LANES]
        q0 = jnp.where(lane < HEAD_DIM, q2 if h % 2 == 0 else pltpu.roll(q2, HEAD_DIM, 1), 0.0)
        score = _dot_nt_x3(kmp_ref[0, h], q0)[HEAD_DIM:HEAD_DIM + n_blk, :]
        picked = _top_k_sublanes(jnp.where(valid, score, -jnp.inf), MOBA_TOPK)
        sel = ((picked > 0.5) & valid) | (blk == own)
        bias_t = jnp.concatenate([jnp.zeros((HEAD_DIM, MOBA_Q), F32), jnp.where(sel, 0.0, NEG_BIG),
                                  jnp.zeros((LANES - HEAD_DIM - n_blk, MOBA_Q), F32)], axis=0)
        q_augs.append(jnp.where(lane < HEAD_DIM, q0 * SCALE_LOG2E, bias_t.T).astype(BF16))
    pos = qi * MOBA_Q + _iota((MOBA_Q, ATT_TK), 0)

    def tile(j, carry, masked):
        start = pl.multiple_of(j * ATT_TK, ATT_TK)
        mask = None
        if masked:
            mask = (j * ATT_TK + _iota((MOBA_Q, ATT_TK), 1)) <= pos
        return tuple(_flash_step(q_augs[h], k_ref[0, pl.ds(start, ATT_TK), h * LANES:(h + 1) * LANES],
                                 v_ref[0, pl.ds(start, ATT_TK), h * LANES:(h + 1) * LANES], mask, *carry[h])
                     for h in range(MOBA_HEADS))

    init = (jnp.full((MOBA_Q, 1), NEG_BIG, F32), jnp.zeros((MOBA_Q, LANES), F32))
    jd = (qi * MOBA_Q) // ATT_TK
    carry = lax.fori_loop(0, jd, lambda j, c: tile(j, c, False), tile(jd, (init,) * MOBA_HEADS, True))
    for c in range(MOBA_HEADS // 2):
        o_ref[0, :, c * LANES:(c + 1) * LANES] = jnp.where(
            lane < HEAD_DIM, _flash_finish(carry[2 * c][1]),
            pltpu.roll(_flash_finish(carry[2 * c + 1][1]), HEAD_DIM, 1)).astype(o_ref.dtype)


def _moba_attention(q_m, kmp, k_aug, v_aug):
    b, s, _ = q_m.shape
    nh = MOBA_HEADS
    return pl.pallas_call(
        _moba_kernel,
        grid=(b, H_MOBA // nh, s // MOBA_Q),
        in_specs=[pl.BlockSpec((1, MOBA_Q, nh * HEAD_DIM), lambda b, h, i: (b, i, h)),
                  pl.BlockSpec((1, nh, LANES, LANES), lambda b, h, i: (b, h, 0, 0)),
                  pl.BlockSpec((1, s, nh * LANES), lambda b, h, i: (b, 0, h)),
                  pl.BlockSpec((1, s, nh * LANES), lambda b, h, i: (b, 0, h))],
        out_specs=pl.BlockSpec((1, MOBA_Q, nh * HEAD_DIM), lambda b, h, i: (b, i, h)),
        out_shape=jax.ShapeDtypeStruct((b, s, W_QM), BF16),
        compiler_params=_params("parallel", "parallel", "arbitrary"),
        name="moba_attention",
    )(q_m, kmp, k_aug, v_aug)


def _stack_heads(q4):
    lane = _iota((Q_TILE, LANES), 1)
    parts = []
    for r in range(R_NSA):
        c = q4[:, (r // 2) * LANES:(r // 2 + 1) * LANES]
        if r % 2:
            c = pltpu.roll(c, HEAD_DIM, 1)
        parts.append(jnp.where(lane < HEAD_DIM, c, 0.0))
    return jnp.concatenate(parts, axis=0)


def _nsa_kernel(qn_ref, qr_ref, gate_ref, kc_ref, vc_ref, cover_ref, ks_ref, vs_ref, kw_ref, vw_ref, o_ref,
                *, n_chunk):
    qi = pl.program_id(2)
    rows = R_NSA * Q_TILE
    qloc = _iota((rows, 1), 0) % Q_TILE
    pos = qi * Q_TILE + qloc

    qn = _stack_heads(qn_ref[0])
    s = _dot_nt_x3(qn, kc_ref[0, 0]) * SCALE
    cmask = (_iota((rows, n_chunk), 1) * CMP_STRIDE + (CMP_LEN - 1)) <= pos
    s = jnp.where(cmask, s, NEG_BIG)
    e = jnp.where(cmask, jnp.exp(s - jnp.max(s, axis=1, keepdims=True)), 0.0)
    p_cmp = e * (1.0 / jnp.maximum(jnp.sum(e, axis=1, keepdims=True), TINY))
    o_cmp = _dot(p_cmp.astype(BF16), vc_ref[0, 0])

    p_grp = p_cmp[0:Q_TILE]
    for r in range(1, R_NSA):
        p_grp = p_grp + p_cmp[r * Q_TILE:(r + 1) * Q_TILE]
    p_hi, p_lo = _split_bf16(p_grp)
    imp_t = _dot_nt(cover_ref[...], p_hi) + _dot_nt(cover_ref[...], p_lo)
    blk_t = _iota((LANES, Q_TILE), 0)
    own_t = (qi * Q_TILE + _iota((LANES, Q_TILE), 1)) // SEL_BLOCK
    forced = (blk_t == 0) | (blk_t == own_t) | (blk_t == own_t - 1)
    valid_t = blk_t <= own_t
    picked = _top_k_sublanes(jnp.where(valid_t & jnp.logical_not(forced), imp_t, -jnp.inf), SEL_TOPN - N_FORCED)
    bias = jnp.where(valid_t & (forced | (picked > 0.5)), 0.0, NEG_BIG).T
    blk = _iota((Q_TILE, LANES), 1)

    qr = (_stack_heads(qr_ref[0]) * SCALE_LOG2E).astype(BF16)
    q_aug = jnp.concatenate([jnp.concatenate([bias] * R_NSA, axis=0).astype(BF16), qr], axis=1)
    init = (jnp.full((rows, 1), NEG_BIG, F32), jnp.zeros((rows, LANES), F32))

    tk = ATT_TK
    jd = (qi * Q_TILE) // tk

    def sel_tile(j, carry, masked):
        start = pl.multiple_of(j * tk, tk)
        mask = None
        if masked:
            mask = (j * tk + _iota((rows, tk), 1)) <= pos
        return _flash_step(q_aug, ks_ref[0, pl.ds(start, tk), :], vs_ref[0, pl.ds(start, tk), :], mask, *carry)

    o_sel = _flash_finish(lax.fori_loop(0, jd, lambda j, c: sel_tile(j, c, False), sel_tile(jd, init, True))[1])

    span = WINDOW + Q_TILE
    w0 = pl.multiple_of(jnp.maximum(qi * Q_TILE - WINDOW, 0), Q_TILE)
    kpos = w0 + _iota((rows, span), 1)
    wmask = (kpos <= pos) & (kpos > pos - WINDOW)
    o_win = _flash_finish(_flash_step(qr, kw_ref[0, pl.ds(w0, span), :], vw_ref[0, pl.ds(w0, span), :], wmask,
                                      *init)[1])

    gates = gate_ref[0]
    heads = []
    for r in range(R_NSA):
        rs = slice(r * Q_TILE, (r + 1) * Q_TILE)
        heads.append(gates[:, 3 * r:3 * r + 1] * o_cmp[rs] + gates[:, 3 * r + 1:3 * r + 2] * o_sel[rs]
                     + gates[:, 3 * r + 2:3 * r + 3] * o_win[rs])
    lo = blk < HEAD_DIM
    o_ref[0] = jnp.concatenate([jnp.where(lo, heads[0], pltpu.roll(heads[1], HEAD_DIM, 1)),
                                jnp.where(lo, heads[2], pltpu.roll(heads[3], HEAD_DIM, 1))], axis=1).astype(o_ref.dtype)


def _nsa_attention(qn, qr, gates_g, kcp, vcd, cover, ks_aug, vs_dup, kw_pad, vw_dup):
    b, s, _ = qn.shape
    n_chunk = kcp.shape[2]
    gq = R_NSA * HEAD_DIM
    per_g = lambda w: pl.BlockSpec((1, s, w), lambda b, g, i: (b, 0, g))
    return pl.pallas_call(
        functools.partial(_nsa_kernel, n_chunk=n_chunk),
        grid=(b, G_NSA, s // Q_TILE),
        in_specs=[pl.BlockSpec((1, Q_TILE, gq), lambda b, g, i: (b, i, g)),
                  pl.BlockSpec((1, Q_TILE, gq), lambda b, g, i: (b, i, g)),
                  pl.BlockSpec((1, Q_TILE, LANES), lambda b, g, i: (b, i, g)),
                  pl.BlockSpec((1, 1, n_chunk, LANES), lambda b, g, i: (b, g, 0, 0)),
                  pl.BlockSpec((1, 1, n_chunk, LANES), lambda b, g, i: (b, g, 0, 0)),
                  pl.BlockSpec(cover.shape, lambda b, g, i: (0, 0)),
                  per_g(2 * LANES), per_g(LANES), per_g(LANES), per_g(LANES)],
        out_specs=pl.BlockSpec((1, Q_TILE, gq), lambda b, g, i: (b, i, g)),
        out_shape=jax.ShapeDtypeStruct((b, s, W_QN), BF16),
        compiler_params=_params("parallel", "parallel", "arbitrary"),
        name="nsa_attention",
    )(qn, qr, gates_g, kcp, vcd, cover, ks_aug, vs_dup, kw_pad, vw_dup)


def _cover(n_cmp, n_sel, rows):
    c0 = np.arange(rows)[:, None] * CMP_STRIDE
    b0 = np.arange(LANES)[None, :] * SEL_BLOCK
    ok = (c0 < b0 + SEL_BLOCK) & (c0 + CMP_LEN > b0) & (np.arange(rows)[:, None] < n_cmp) & (np.arange(LANES)[None, :] < n_sel)
    return jnp.asarray(ok.astype(np.float32))


def _lane_rep(col):
    return jnp.broadcast_to(col, (col.shape[0], LANES))


def _head_sums(prod_row):
    w = prod_row.shape[1]
    own = (_iota((SUBLANES, w), 1) // HEAD_DIM) == _iota((SUBLANES, w), 0)
    return _lane_rep(jnp.sum(jnp.where(own, jnp.broadcast_to(prod_row, (SUBLANES, w)), 0.0), axis=1, keepdims=True))


def _pair_row(x8, h):
    return jnp.where(_iota((1, LANES), 1) < HEAD_DIM, x8[h:h + 1, :], x8[h + 1:h + 2, :])


def _cols_to_row(acc_a, acc_b):
    return jnp.sum(jnp.concatenate([acc_a, acc_b], axis=0).T, axis=0, keepdims=True)


def _dec_kernel(pt_ref, cm_ref, cn_ref, wb_ref, q_ref, q8n_ref, new_ref, gate_ref, wcat_ref, pecat_ref, w2_ref,
                 gk_ref, bd_ref, cover_ref, o_ref,
                 mbuf, nbuf, sem, xk, xv, qmb, qrb, s_sc, p_sc, *, n_pages, page):
    b = pl.program_id(0)
    n_req = pl.num_programs(0)
    slot = b % 2
    hw = H_MOBA * HEAD_DIM
    gw = G_NSA * HEAD_DIM
    past = n_pages * page

    def copies(req, sl):
        out = []
        for p in range(n_pages):
            pg = pt_ref[req * n_pages + p]
            out.append(pltpu.make_async_copy(cm_ref.at[pg], mbuf.at[sl, p], sem.at[0, sl]))
            out.append(pltpu.make_async_copy(cn_ref.at[pg], nbuf.at[sl, p], sem.at[1, sl]))
        return out

    @pl.when(b == 0)
    def _():
        for c in copies(0, 0):
            c.start()

    for c in copies(b, slot):
        c.wait()

    @pl.when(b + 1 < n_req)
    def _():
        for c in copies(b + 1, 1 - slot):
            c.start()

    lane1 = _iota((1, LANES), 1)
    lane8 = _iota((SUBLANES, LANES), 1)
    qrow = q_ref[0]
    new = new_ref[0]
    for c in range(hw // LANES):
        cols = slice(c * LANES, (c + 1) * LANES)
        qmb[cols, :] = jnp.broadcast_to(qrow[0:1, cols] * SCALE, (LANES, LANES)).T
        qrb[cols, :] = jnp.broadcast_to(qrow[1:2, cols] * SCALE, (LANES, LANES)).T

    def softmax_pv(scores, s_new8, v_rows, vbuf_ref, v_row0, per_g):
        m8 = s_new8
        for s in scores:
            m8 = jnp.maximum(m8, _lane_rep(jnp.max(s, axis=1, keepdims=True)))
        w_new = jnp.exp(s_new8 - m8)
        l8 = w_new
        for p, s in enumerate(scores):
            pr = jnp.exp(s - m8)
            p_sc[p] = pr
            l8 = l8 + _lane_rep(jnp.sum(pr, axis=1, keepdims=True))
        inv8 = 1.0 / jnp.maximum(l8, TINY)
        rows = []
        for hp in range(SUBLANES // 2):
            accs = []
            for h in (2 * hp, 2 * hp + 1):
                r0 = v_row0 + (h // R_NSA if per_g else h) * HEAD_DIM

                def body(p, acc, h=h, r0=r0):
                    return acc + vbuf_ref[slot, p, r0:r0 + HEAD_DIM, :] * p_sc[p, h:h + 1, :]
                accs.append(lax.fori_loop(0, n_pages, body, jnp.zeros((HEAD_DIM, LANES), F32)))
            row = _cols_to_row(accs[0], accs[1])
            cols = slice(hp * LANES, (hp + 1) * LANES)
            rows.append((row + _pair_row(w_new, 2 * hp) * v_rows[:, cols]) * _pair_row(inv8, 2 * hp))
        return rows

    def moba_scores(p, carry):
        rows = [jnp.sum(mbuf[slot, p, h * HEAD_DIM:(h + 1) * HEAD_DIM, :] * qmb[h * HEAD_DIM:(h + 1) * HEAD_DIM, :],
                        axis=0, keepdims=True) for h in range(H_MOBA)]
        s_sc[p] = jnp.concatenate(rows, axis=0)
        return carry
    lax.fori_loop(0, n_pages, moba_scores, 0)
    s_all = [s_sc[p] for p in range(n_pages)]
    ppb = MOBA_BLOCK // page
    n_blk = n_pages // ppb
    bsc = []
    for j in range(n_blk):
        tot = s_all[j * ppb]
        for t in range(1, ppb):
            tot = tot + s_all[j * ppb + t]
        bsc.append(_lane_rep(jnp.sum(tot, axis=1, keepdims=True)))
    masked = []
    for j in range(n_blk):
        rank = jnp.zeros((SUBLANES, LANES), F32)
        for c in range(n_blk):
            if c != j:
                ahead = (bsc[c] > bsc[j]) | ((bsc[c] == bsc[j]) & (c < j))
                rank = rank + jnp.where(ahead, 1.0, 0.0)
        for t in range(ppb):
            masked.append(jnp.where(rank < MOBA_TOPK, s_all[j * ppb + t], NEG_BIG))
    s_own = _head_sums(qrow[0:1, :] * new[0:1, :]) * SCALE
    o_rows = softmax_pv(masked, s_own, new[1:2, :], mbuf, hw, False)

    for p in range(n_pages):
        xk[p * page:(p + 1) * page, :] = nbuf[slot, p, 0:gw, :].T
        xv[p * page:(p + 1) * page, :] = nbuf[slot, p, gw:2 * gw, :].T
    n_chunk = past // CMP_STRIDE
    cmp_out = []
    for kv, xref in enumerate((xk, xv)):
        xcat = jnp.concatenate([xref[pl.ds(j, n_chunk, stride=CMP_STRIDE), :] for j in range(CMP_STRIDE)], axis=1)
        pe2 = _dot(pecat_ref[kv].astype(BF16), wcat_ref[kv])
        ab = _dot(xcat.astype(BF16), wcat_ref[kv])
        hid = jax.nn.gelu(ab[:, 0:gw] + pe2[0:1, 0:gw] + pltpu.roll(ab[:, gw:2 * gw] + pe2[1:2, gw:2 * gw],
                                                                    n_chunk - 1, 0))
        cmp_out.append(_dot(hid, w2_ref[kv], HIGHEST))
    kc = _head_norm(cmp_out[0], gk_ref[...], bd_ref[...])
    vc = cmp_out[1]

    s = _dot_nt(q8n_ref[0], kc, HIGHEST) * SCALE
    cmask = _iota((SUBLANES, n_chunk), 1) < (n_chunk - 1)
    s = jnp.where(cmask, s, NEG_BIG)
    e = jnp.where(cmask, jnp.exp(s - jnp.max(s, axis=1, keepdims=True)), 0.0)
    p_cmp = e / jnp.maximum(jnp.sum(e, axis=1, keepdims=True), TINY)
    o_cmp8 = _dot(p_cmp.astype(BF16), vc.astype(BF16))
    subc = _iota((SUBLANES, n_chunk), 0)
    g0 = jnp.sum(jnp.where(subc < R_NSA, p_cmp, 0.0), axis=0, keepdims=True)
    g1 = jnp.sum(jnp.where(subc >= R_NSA, p_cmp, 0.0), axis=0, keepdims=True)
    p_grp = jnp.concatenate([jnp.where(subc < R_NSA, g0, g1), jnp.zeros((LANES - SUBLANES, n_chunk), F32)], axis=0)
    own = past // SEL_BLOCK
    n_sel = -(-(own + 1) // SUBLANES) * SUBLANES
    imp_t = _dot_nt(cover_ref[...], p_grp, HIGHEST)[0:n_sel]
    blk_t = _iota((n_sel, LANES), 0)
    forced = (blk_t == 0) | (blk_t == own) | (blk_t == own - 1)
    score = jnp.where(blk_t <= own, jnp.where(forced, jnp.inf, imp_t), -jnp.inf)
    picked = _top_k_sublanes(score, SEL_TOPN)
    bias_t = jnp.concatenate([jnp.where((picked > 0.5) & (blk_t <= own), 0.0, NEG_BIG),
                              jnp.full((LANES - n_sel, LANES), NEG_BIG, F32)], axis=0)
    bias8 = bias_t.T[0:SUBLANES, :]
    cmp_rows = []
    for hp in range(H_NSA // 2):
        g = (2 * hp) // R_NSA
        ra = o_cmp8[2 * hp:2 * hp + 1, :]
        rb = o_cmp8[2 * hp + 1:2 * hp + 2, :]
        cmp_rows.append(jnp.where(lane1 < HEAD_DIM, ra if g == 0 else pltpu.roll(ra, HEAD_DIM, 1),
                                  rb if g == 1 else pltpu.roll(rb, HEAD_DIM, 1)))

    def sel_scores(p, carry):
        rows = [jnp.sum(nbuf[slot, p, 2 * gw + (i // R_NSA) * HEAD_DIM:2 * gw + (i // R_NSA + 1) * HEAD_DIM, :]
                        * qrb[i * HEAD_DIM:(i + 1) * HEAD_DIM, :], axis=0, keepdims=True) for i in range(H_NSA)]
        s_sc[p] = jnp.concatenate(rows, axis=0)
        return carry
    lax.fori_loop(0, n_pages, sel_scores, 0)
    bpp = page // SEL_BLOCK
    sel_s = []
    for p in range(n_pages):
        bias_p = bias8[:, p * bpp:p * bpp + 1]
        for t in range(1, bpp):
            bias_p = jnp.where(lane8 < t * SEL_BLOCK, bias_p, bias8[:, p * bpp + t:p * bpp + t + 1])
        sel_s.append(s_sc[p] + bias_p)
    s_new = _head_sums(qrow[1:2, :] * new[2:3, :]) * SCALE
    sel_rows = softmax_pv(sel_s, s_new, new[3:4, :], nbuf, 3 * gw, True)

    nw = wb_ref.shape[2]
    wk = nw // LANES
    w_s = []
    for i in range(H_NSA):
        g = i // R_NSA
        qcol = jnp.concatenate([qrb[i * HEAD_DIM:(i + 1) * HEAD_DIM, :]] * wk, axis=1)
        w_s.append(jnp.sum(wb_ref[0, g * HEAD_DIM:(g + 1) * HEAD_DIM, :] * qcol, axis=0, keepdims=True))
    s = jnp.concatenate(w_s, axis=0)
    wmask = _iota((SUBLANES, nw), 1) >= 1
    s = jnp.where(wmask, s, NEG_BIG)
    s_new = _head_sums(qrow[1:2, :] * new[4:5, :]) * SCALE
    m8 = jnp.maximum(_lane_rep(jnp.max(s, axis=1, keepdims=True)), s_new)
    pw = jnp.where(wmask, jnp.exp(s - m8[:, 0:1]), 0.0)
    w_new = jnp.exp(s_new - m8)
    inv8 = 1.0 / jnp.maximum(_lane_rep(jnp.sum(pw, axis=1, keepdims=True)) + w_new, TINY)
    win_rows = []
    for hp in range(H_NSA // 2):
        accs = []
        for i in (2 * hp, 2 * hp + 1):
            g = i // R_NSA
            prod = wb_ref[0, gw + g * HEAD_DIM:gw + (g + 1) * HEAD_DIM, :] * pw[i:i + 1, :]
            acc = prod[:, 0:LANES]
            for c in range(1, wk):
                acc = acc + prod[:, c * LANES:(c + 1) * LANES]
            accs.append(acc)
        cols = slice(hp * LANES, (hp + 1) * LANES)
        win_rows.append((_cols_to_row(accs[0], accs[1]) + _pair_row(w_new, 2 * hp) * new[5:6, cols])
                        * _pair_row(inv8, 2 * hp))

    gates = gate_ref[0]
    for c in range(hw // LANES):
        o_ref[0, :, c * LANES:(c + 1) * LANES] = o_rows[c]
    for c in range(W_QN // LANES):
        cols = slice(c * LANES, (c + 1) * LANES)
        o_ref[0, :, hw + c * LANES:hw + (c + 1) * LANES] = (
            gates[0:1, cols] * cmp_rows[c] + gates[1:2, cols] * sel_rows[c] + gates[2:3, cols] * win_rows[c])


def _dec_attention(cm, cn, wb, pt_flat, qrows, q8n, new, gate_rows, wcat, pecat, w2bd, gk, bd, cover,
                    *, n_req, n_pages):
    page = cm.shape[2]
    n_chunk = n_pages * page // CMP_STRIDE
    req = lambda a: pl.BlockSpec((1,) + a.shape[1:], lambda b, pt: (b,) + (0,) * (a.ndim - 1))
    const = lambda a: pl.BlockSpec(a.shape, lambda b, pt: (0,) * a.ndim)
    return pl.pallas_call(
        functools.partial(_dec_kernel, n_pages=n_pages, page=page),
        grid_spec=pltpu.PrefetchScalarGridSpec(
            num_scalar_prefetch=1,
            grid=(n_req,),
            in_specs=[pl.BlockSpec(memory_space=pl.ANY), pl.BlockSpec(memory_space=pl.ANY), req(wb), req(qrows),
                      req(q8n), req(new), req(gate_rows), const(wcat), const(pecat), const(w2bd), const(gk),
                      const(bd), const(cover)],
            out_specs=pl.BlockSpec((1, 1, W_QM + W_QN), lambda b, pt: (b, 0, 0)),
            scratch_shapes=[pltpu.VMEM((2, n_pages) + cm.shape[1:], F32), pltpu.VMEM((2, n_pages) + cn.shape[1:], F32),
                            pltpu.SemaphoreType.DMA((2, 2)),
                            pltpu.VMEM((n_pages * page, LANES), F32), pltpu.VMEM((n_pages * page, LANES), F32),
                            pltpu.VMEM((W_QM, LANES), F32), pltpu.VMEM((W_QN, LANES), F32),
                            pltpu.VMEM((n_pages, SUBLANES, LANES), F32), pltpu.VMEM((n_pages, SUBLANES, LANES), F32)]),
        out_shape=jax.ShapeDtypeStruct((n_req, 1, W_QM + W_QN), F32),
        compiler_params=_params("arbitrary"),
        name="dec_attention",
    )(pt_flat, cm, cn, wb, qrows, q8n, new, gate_rows, wcat, pecat, w2bd, gk, bd, cover)


def _compress_cat_consts(cmp_pos, cmp_w1):
    eye = jnp.eye(G_NSA, dtype=F32)
    w1 = cmp_w1.reshape(2, 2, CMP_STRIDE, HEAD_DIM, HEAD_DIM)
    wcat = jnp.einsum("gh,kajde->kjgdahe", eye, w1).reshape(2, CMP_STRIDE * G_NSA * HEAD_DIM, 2 * G_NSA * HEAD_DIM)
    pe = jnp.tile(cmp_pos.reshape(2, 2, CMP_STRIDE, 1, HEAD_DIM), (1, 1, 1, G_NSA, 1))
    pecat = jnp.pad(pe.reshape(2, 2, CMP_STRIDE * G_NSA * HEAD_DIM), ((0, 0), (0, SUBLANES - 2), (0, 0)))
    return wcat.astype(BF16), pecat


def _post_kernel(o_ref, x_ref, gt_ref, sh_ref, sc_ref, g_ref, wo_ref, wr_ref, br_ref,
                 y_ref, h3_ref, te_ref, tw_ref):
    y = x_ref[...] + gt_ref[0] * _dot(o_ref[...], wo_ref[...])
    y_ref[...] = y
    h = y * lax.rsqrt(jnp.mean(y * y, axis=1, keepdims=True) + NORM_EPS) * g_ref[...]
    h = h * (1.0 + sc_ref[0]) + sh_ref[0]
    tm = h.shape[0]
    for s in range(h.shape[1] // LANES):
        h3_ref[pl.ds(s, tm, stride=SUBLANES), :] = h[:, s * LANES:(s + 1) * LANES]
    h_hi, h_lo = _split_bf16(h)
    w_hi, w_lo = _split_bf16(wr_ref[...])
    logits = _dot(h_hi, w_hi) + _dot(h_lo, w_hi) + _dot(h_hi, w_lo) + br_ref[...]
    _, vals, ids = _top_k_lanes(logits, TOP_K)
    lane = _iota((tm, LANES), 1)
    es = [jnp.exp(v - vals[0]) for v in vals]
    den = es[0]
    for e in es[1:]:
        den = den + e
    te = jnp.zeros((tm, LANES), F32)
    tw = jnp.zeros((tm, LANES), F32)
    for k in range(TOP_K):
        te = jnp.where(lane == k, ids[k], te)
        tw = jnp.where(lane == k, es[k] / den, tw)
    te_ref[...] = te.astype(jnp.int32)
    tw_ref[...] = tw


def _post(o, x, gate, shift, scale, g2, wo_bf, wr_pad, br_pad, *, tm, rows_per_mod):
    n, d = x.shape
    mod_r = gate.shape[1]
    mod_map = lambda i: (i // (rows_per_mod // tm), 0, 0)
    row = lambda w: pl.BlockSpec((tm, w), lambda i: (i, 0))
    const = lambda a: pl.BlockSpec(a.shape, lambda i: (0,) * a.ndim)
    mod = pl.BlockSpec((1, mod_r, d), mod_map)
    return pl.pallas_call(
        _post_kernel,
        grid=(n // tm,),
        in_specs=[row(d), row(d), mod, mod, mod, const(g2), const(wo_bf), const(wr_pad), const(br_pad)],
        out_specs=[row(d), pl.BlockSpec((tm * SUBLANES, LANES), lambda i: (i, 0)), row(LANES), row(LANES)],
        out_shape=[jax.ShapeDtypeStruct((n, d), F32), jax.ShapeDtypeStruct((n * SUBLANES, LANES), F32),
                   jax.ShapeDtypeStruct((n, LANES), jnp.int32), jax.ShapeDtypeStruct((n, LANES), F32)],
        compiler_params=_params("parallel"),
        name="post",
    )(o, x, gate, shift, scale, g2, wo_bf, wr_pad, br_pad)


def _dispatch_kernel(pos_ref, h3_ref, xs_in_ref, xs_ref, sem, *, tm, tile_off):
    del xs_in_ref
    i = pl.program_id(0)

    def body(r, carry):
        src = h3_ref.at[pl.ds(pl.multiple_of(r * SUBLANES, SUBLANES), SUBLANES), :]
        for k in range(TOP_K):
            p = pos_ref[((i + tile_off) * tm + r) * TOP_K + k]
            pltpu.make_async_copy(src, xs_ref.at[pl.ds(pl.multiple_of(p * SUBLANES, SUBLANES), SUBLANES), :],
                                  sem.at[0]).start()
        return carry
    lax.fori_loop(0, tm, body, 0)
    for k in range(TOP_K):
        pltpu.make_async_copy(h3_ref, xs_ref.at[pl.ds(0, tm * SUBLANES), :], sem.at[0]).wait()


def _dispatch(pos_flat, h3, xs, *, tm, tile_off):
    n8 = h3.shape[0]
    return pl.pallas_call(
        functools.partial(_dispatch_kernel, tm=tm, tile_off=tile_off),
        grid_spec=pltpu.PrefetchScalarGridSpec(
            num_scalar_prefetch=1,
            grid=(n8 // (tm * SUBLANES),),
            in_specs=[pl.BlockSpec((tm * SUBLANES, LANES), lambda i, pos: (i, 0)), pl.BlockSpec(memory_space=pl.ANY)],
            out_specs=pl.BlockSpec(memory_space=pl.ANY),
            scratch_shapes=[pltpu.SemaphoreType.DMA((1,))]),
        out_shape=jax.ShapeDtypeStruct(xs.shape, xs.dtype),
        input_output_aliases={2: 0},
        compiler_params=_params("arbitrary"),
        name="dispatch",
    )(pos_flat, h3, xs)


def _expert_block_kernel(be_ref, na_ref, x_ref, wgu_ref, bgu_ref, wd_ref, bd_ref, y_ref, xb, wgu_bf, wd_bf):
    i = pl.program_id(0)
    active = i < na_ref[0]

    @pl.when(active & ((i == 0) | (be_ref[i] != be_ref[jnp.maximum(i - 1, 0)])))
    def _():
        for r in range(0, wgu_bf.shape[0], LANES):
            wgu_bf[r:r + LANES, :] = wgu_ref[0, r:r + LANES, :].astype(BF16)
        for r in range(0, wd_bf.shape[0], LANES):
            wd_bf[r:r + LANES, :] = wd_ref[0, r:r + LANES, :].astype(BF16)

    @pl.when(active)
    def _():
        d = xb.shape[1]
        for s in range(d // LANES):
            xb[:, s * LANES:(s + 1) * LANES] = x_ref[pl.ds(s, MOE_ROWS, stride=SUBLANES), :].astype(BF16)
        gu = _dot(xb[...], wgu_bf[...]) + bgu_ref[0]
        f = gu.shape[1] // 2
        gt = jnp.minimum(gu[:, 0:f], SWIGLU_LIMIT)
        up = jnp.clip(gu[:, f:2 * f], -SWIGLU_LIMIT, SWIGLU_LIMIT)
        act = (up + 1.0) * (gt * jax.nn.sigmoid(SWIGLU_ALPHA * gt))
        y = _dot(act.astype(BF16), wd_bf[...]) + bd_ref[0]
        for s in range(d // LANES):
            y_ref[pl.ds(s, MOE_ROWS, stride=SUBLANES), :] = y[:, s * LANES:(s + 1) * LANES]

    @pl.when(i >= na_ref[0])
    def _():
        y_ref[...] = jnp.zeros(y_ref.shape, F32)


def _expert_blocks(blk_e, n_active, xs, w_gu, b_gu, w_down, b_down):
    n_blocks = blk_e.shape[0]
    e, d, f2 = w_gu.shape
    rows8 = MOE_ROWS * SUBLANES
    row_map = lambda i, be, na: (jnp.minimum(i, na[0] - 1), 0)
    return pl.pallas_call(
        _expert_block_kernel,
        grid_spec=pltpu.PrefetchScalarGridSpec(
            num_scalar_prefetch=2,
            grid=(n_blocks,),
            in_specs=[pl.BlockSpec((rows8, LANES), row_map),
                      pl.BlockSpec((1, d, f2), lambda i, be, na: (be[i], 0, 0)),
                      pl.BlockSpec((1, 1, f2), lambda i, be, na: (be[i], 0, 0)),
                      pl.BlockSpec((1, f2 // 2, d), lambda i, be, na: (be[i], 0, 0)),
                      pl.BlockSpec((1, 1, d), lambda i, be, na: (be[i], 0, 0))],
            out_specs=pl.BlockSpec((rows8, LANES), lambda i, be, na: (i, 0)),
            scratch_shapes=[pltpu.VMEM((MOE_ROWS, d), BF16), pltpu.VMEM((d, f2), BF16),
                            pltpu.VMEM((f2 // 2, d), BF16)]),
        out_shape=jax.ShapeDtypeStruct((n_blocks * rows8, LANES), F32),
        compiler_params=_params("arbitrary"),
        name="experts",
    )(blk_e, n_active, xs, w_gu, b_gu.reshape(e, 1, f2), w_down, b_down.reshape(e, 1, d))


def _routing_pos(top_e, n_tok):
    n_assign = n_tok * TOP_K
    n_pad = -(-n_assign // LANES) * LANES
    e_flat = jnp.pad(top_e.reshape(-1), (0, n_pad - n_assign), constant_values=N_EXPERTS)
    onehot = (e_flat[:, None] == jnp.arange(N_EXPERTS)[None, :]).astype(F32).reshape(n_pad // LANES, LANES, N_EXPERTS)
    tril = jnp.tril(jnp.ones((LANES, LANES), F32))
    within = jnp.einsum("ij,bjk->bik", tril, onehot)
    block_tot = within[:, -1, :]
    offs = jnp.cumsum(block_tot, axis=0) - block_tot
    counts = jnp.sum(block_tot, axis=0).astype(jnp.int32)
    padded = (counts + MOE_ROWS - 1) // MOE_ROWS * MOE_ROWS
    pad_end = jnp.cumsum(padded)
    pad_start = (pad_end - padded).astype(F32)
    slot = jnp.sum(onehot * (within + offs[:, None, :] - 1.0 + pad_start[None, None, :]), axis=-1)
    pos = slot.reshape(-1)[:n_assign].astype(jnp.int32)
    n_blocks = -(-n_assign // MOE_ROWS) + N_EXPERTS
    starts = jnp.arange(n_blocks, dtype=jnp.int32) * MOE_ROWS
    blk_e = jnp.minimum(jnp.sum((pad_end[None, :] <= starts[:, None]).astype(jnp.int32), axis=1), N_EXPERTS - 1)
    n_active = (pad_end[-1] // MOE_ROWS).astype(jnp.int32).reshape(1)
    return blk_e, n_active, pos


def _combine_kernel(pos_ref, ys_ref, y1_ref, gt_ref, tw_ref, o_ref, buf, sem, *, tile_off):
    i = pl.program_id(0)
    n = pl.num_programs(0)
    tm = y1_ref.shape[0]
    rows8 = tm * SUBLANES

    def gather(tile, slot):
        def body(r, carry):
            for k in range(TOP_K):
                p = pos_ref[((tile + tile_off) * tm + r) * TOP_K + k]
                pltpu.make_async_copy(ys_ref.at[pl.ds(pl.multiple_of(p * SUBLANES, SUBLANES), SUBLANES), :],
                                      buf.at[slot, k, pl.ds(pl.multiple_of(r * SUBLANES, SUBLANES), SUBLANES), :],
                                      sem.at[slot]).start()
            return carry
        lax.fori_loop(0, tm, body, 0)

    @pl.when(i == 0)
    def _():
        gather(0, 0)

    slot = i % 2
    for k in range(TOP_K):
        pltpu.make_async_copy(ys_ref.at[pl.ds(0, rows8), :], buf.at[slot, k], sem.at[slot]).wait()

    @pl.when(i + 1 < n)
    def _():
        gather(i + 1, 1 - slot)

    tw = tw_ref[...]
    wk = [jnp.broadcast_to(tw[:, k:k + 1], (tm, LANES)) for k in range(TOP_K)]
    gt = gt_ref[0]
    for s in range(o_ref.shape[1] // LANES):
        moe = wk[0] * buf[slot, 0, pl.ds(s, tm, stride=SUBLANES), :]
        for k in range(1, TOP_K):
            moe = moe + wk[k] * buf[slot, k, pl.ds(s, tm, stride=SUBLANES), :]
        cols = slice(s * LANES, (s + 1) * LANES)
        o_ref[:, cols] = y1_ref[:, cols] + gt[:, cols] * moe


def _combine(pos_flat, ys, y1, gate, tw, *, tm, rows_per_mod, tile_off):
    n, d = y1.shape
    mod_r = gate.shape[1]
    return pl.pallas_call(
        functools.partial(_combine_kernel, tile_off=tile_off),
        grid_spec=pltpu.PrefetchScalarGridSpec(
            num_scalar_prefetch=1,
            grid=(n // tm,),
            in_specs=[pl.BlockSpec(memory_space=pl.ANY),
                      pl.BlockSpec((tm, d), lambda i, pos: (i, 0)),
                      pl.BlockSpec((1, mod_r, d), lambda i, pos: (i // (rows_per_mod // tm), 0, 0)),
                      pl.BlockSpec((tm, LANES), lambda i, pos: (i, 0))],
            out_specs=pl.BlockSpec((tm, d), lambda i, pos: (i, 0)),
            scratch_shapes=[pltpu.VMEM((2, TOP_K, tm * SUBLANES, LANES), F32), pltpu.SemaphoreType.DMA((2,))]),
        out_shape=jax.ShapeDtypeStruct((n, d), F32),
        compiler_params=_params("arbitrary"),
        name="combine",
    )(pos_flat, ys, y1, gate, tw)


def _rope_tables(pos):
    half = HEAD_DIM // 2
    inv = ROPE_THETA ** (-jnp.arange(half, dtype=F32) / half)
    ang = pos.astype(F32)[:, None] * inv[None, :]
    cos = jnp.cos(ang)
    sin = jnp.sin(ang)
    reps = LANES // HEAD_DIM
    return (jnp.tile(jnp.concatenate([cos, cos], axis=1), (1, reps)),
            jnp.tile(jnp.concatenate([-sin, sin], axis=1), (1, reps)))


def _head_pad(a):
    return jnp.concatenate([a, jnp.zeros_like(a)], axis=-1)


def kernel(x_prompt, x_sample, c_prompt, c_sample, cache_moba_kv, cache_nsa_kv, state_nsa_win_kv, page_table,
           norm_g, w_ada, b_ada, w_in, qk_gain, cmp_pos, cmp_w1, cmp_w2, w_out, w_router, b_router, w_gu, b_gu,
           w_down, b_down):
    bsz, seq, d = x_prompt.shape
    n_req = x_sample.shape[0]
    depth = norm_g.shape[0]
    assert depth == 1 and x_sample.shape[1] == 1
    assert seq % ATT_TK == 0 and seq >= WINDOW + Q_TILE and seq // SEL_BLOCK <= LANES and seq // MOBA_BLOCK <= MOBA_BLOCK // SUBLANES
    n_pool, page = cache_moba_kv.shape[1], cache_moba_kv.shape[2]
    n_pages = page_table.shape[1]
    past = n_pages * page
    assert past % MOBA_BLOCK == 0 and 2 * page == MOBA_BLOCK and past // SEL_BLOCK < LANES
    assert state_nsa_win_kv.shape[2] == WINDOW
    layer = 0
    gw = G_NSA * HEAD_DIM
    n_prompt = bsz * seq

    bd = jnp.asarray(np.kron(np.eye(LANES // HEAD_DIM), np.full((HEAD_DIM, HEAD_DIM), 1.0 / HEAD_DIM)), BF16)
    w_in_bf = jnp.pad(w_in[layer], ((0, 0), (0, IN_COLS_PAD - IN_COLS))).astype(BF16)
    gains = jnp.tile(qk_gain[layer], (1, W_QM // HEAD_DIM))
    g1 = norm_g[layer, 0].reshape(1, d)
    g2 = norm_g[layer, 1].reshape(1, d)
    wo_bf = w_out[layer].astype(BF16)
    wr_pad = jnp.pad(w_router[layer], ((0, 0), (0, LANES - N_EXPERTS)))
    br_pad = jnp.pad(b_router[layer].reshape(1, N_EXPERTS), ((0, 0), (0, LANES - N_EXPERTS)),
                     constant_values=-jnp.inf)
    cmp_consts = _compress_consts(cmp_pos[layer], cmp_w1[layer], cmp_w2[layer], qk_gain[layer, 3])

    n_c = bsz + n_req
    n_c_pad = -(-n_c // SUBLANES) * SUBLANES
    c_all = jnp.pad(jnp.concatenate([c_prompt, c_sample], axis=0), ((0, n_c_pad - n_c), (0, 0)))
    mods = _ada(c_all, w_ada[layer], b_ada[layer])
    mods_p = [m.reshape(bsz, 1, d) for m in jnp.split(mods[:bsz], 6, axis=1)]
    mods_s = [m.reshape(1, n_req, d) for m in jnp.split(mods[bsz:n_c], 6, axis=1)]

    cos_p, sin_p = _rope_tables(jnp.arange(seq, dtype=jnp.int32))
    tm_p = 256
    (moba_t, nsa_t, win_t, q_m, qn, qr, _, kmean, cmp_raw, k_aug, v_aug, ks_aug, vs_aug, kw_pad, vw_aug,
     gates_g) = _proj(
        x_prompt.reshape(n_prompt, d), mods_p[0], mods_p[1], g1, w_in_bf, gains, bd, cos_p, sin_p,
        tm=tm_p, rows_per_mod=seq, pos_blocks=seq // tm_p, with_kmean=True)

    nbk = seq // MOBA_BLOCK
    kmean_h = kmean.reshape(bsz, nbk, H_MOBA, HEAD_DIM).transpose(0, 2, 1, 3)
    kmp = jnp.zeros((bsz, H_MOBA, LANES, LANES), F32).at[:, :, HEAD_DIM:HEAD_DIM + nbk, :HEAD_DIM].set(kmean_h)
    per_b = lambda a: a.reshape(bsz, seq, a.shape[-1])
    o_m = _moba_attention(per_b(q_m), kmp, per_b(k_aug), per_b(v_aug))

    kc, vc = _compress_prompt(per_b(cmp_raw), cmp_consts, bd)
    n_chunk = seq // CMP_STRIDE
    per_g = lambda a: a.reshape(bsz, -1, G_NSA, HEAD_DIM).transpose(0, 2, 1, 3)
    kcp = _head_pad(per_g(kc))
    vcd = _head_pad(per_g(vc)).astype(BF16)
    cover_p = _cover(n_chunk - 1, seq // SEL_BLOCK, n_chunk).T.astype(BF16)
    o_n = _nsa_attention(per_b(qn), per_b(qr), per_b(gates_g), kcp, vcd, cover_p, per_b(ks_aug), per_b(vs_aug),
                         per_b(kw_pad), per_b(vw_aug))
    o_p = jnp.concatenate([o_m, o_n], axis=-1).reshape(n_prompt, d)

    y1_p, h3_p, te_p, tw_p = _post(o_p, x_prompt.reshape(n_prompt, d), mods_p[2], mods_p[3], mods_p[4], g2, wo_bf,
                                   wr_pad, br_pad, tm=256, rows_per_mod=seq)

    cos_s, sin_s = _rope_tables(jnp.full((n_req,), past, jnp.int32))
    moba_new, nsa_new, win_new, q_m_s, qn_s, qr_s, gates_s = _proj(
        x_sample.reshape(n_req, d), mods_s[0], mods_s[1], g1, w_in_bf, gains, bd, cos_s, sin_s,
        tm=n_req, rows_per_mod=n_req, pos_blocks=1, with_kmean=False)
    pt_flat = page_table.reshape(-1).astype(jnp.int32)
    cache_m = cache_moba_kv[layer].transpose(0, 2, 3, 4, 1).reshape(n_pool, W_KVM, page)
    cache_n = cache_nsa_kv[layer].transpose(0, 2, 3, 4, 1).reshape(n_pool, 4 * gw, page)
    win_buf = state_nsa_win_kv[layer].transpose(0, 2, 3, 4, 1).reshape(n_req, 2 * gw, WINDOW)

    def rows8(q):
        qh = q.reshape(n_req, G_NSA, R_NSA, 1, HEAD_DIM)
        place = jnp.arange(G_NSA)[None, :, None, None, None] == jnp.arange(G_NSA)[None, None, None, :, None]
        return jnp.where(place, qh, 0.0).reshape(n_req, H_NSA, gw)

    def per_head(a):
        return jnp.repeat(a.reshape(n_req, G_NSA, HEAD_DIM), R_NSA, axis=1).reshape(n_req, W_QN)

    new_rows = jnp.stack([moba_new[:, :W_QM], moba_new[:, W_QM:], per_head(nsa_new[:, 2 * gw:3 * gw]),
                          per_head(nsa_new[:, 3 * gw:]), per_head(win_new[:, :gw]), per_head(win_new[:, gw:])], axis=1)
    gate_rows = jnp.repeat(gates_s[:, :N_GATE].reshape(n_req, H_NSA, 3).transpose(0, 2, 1), HEAD_DIM, axis=2)
    n_cmp_s = past // CMP_STRIDE
    cover_s = _cover(n_cmp_s - 1, past // SEL_BLOCK + 1, n_cmp_s).T
    wcat, pecat = _compress_cat_consts(cmp_pos[layer], cmp_w1[layer])
    o_s = _dec_attention(cache_m, cache_n, win_buf, pt_flat, jnp.stack([q_m_s, qr_s], axis=1), rows8(qn_s), new_rows,
                          gate_rows, wcat, pecat, cmp_consts[2], cmp_consts[3], bd, cover_s,
                          n_req=n_req, n_pages=n_pages).reshape(n_req, d).astype(BF16)
    y1_s, h3_s, te_s, tw_s = _post(o_s, x_sample.reshape(n_req, d), mods_s[2], mods_s[3], mods_s[4], g2, wo_bf,
                                   wr_pad, br_pad, tm=n_req, rows_per_mod=n_req)

    n_tok = n_prompt + n_req
    top_e = jnp.concatenate([te_p[:, :TOP_K], te_s[:, :TOP_K]], axis=0)
    blk_e, n_active, pos_flat = _routing_pos(top_e, n_tok)
    tm_c = 128
    xs = jnp.zeros((blk_e.shape[0] * MOE_ROWS * SUBLANES, LANES), F32)
    xs = _dispatch(pos_flat, h3_p, xs, tm=2 * tm_c, tile_off=0)
    xs = _dispatch(pos_flat, h3_s, xs, tm=n_req, tile_off=n_prompt // n_req)
    ys = _expert_blocks(blk_e, n_active, xs, w_gu[layer], b_gu[layer], w_down[layer], b_down[layer])
    y_p = _combine(pos_flat, ys, y1_p, mods_p[5], tw_p, tm=tm_c, rows_per_mod=seq, tile_off=0)
    y_s = _combine(pos_flat, ys, y1_s, mods_s[5], tw_s, tm=n_req, rows_per_mod=n_req, tile_off=n_prompt // n_req)

    def rows_view(t, n_slot, n_head):
        return t.reshape(1, bsz, n_slot, n_head, HEAD_DIM, t.shape[-1]).transpose(0, 1, 5, 2, 3, 4)

    keep = min(WINDOW, seq)
    win_s = jnp.concatenate([state_nsa_win_kv[layer][:, 1:], win_new.reshape(n_req, 1, 2, G_NSA, HEAD_DIM)], axis=1)
    return (y_p.reshape(bsz, seq, d), y_s.reshape(n_req, 1, d),
            rows_view(moba_t, 2, H_MOBA), rows_view(nsa_t, 4, G_NSA), rows_view(win_t[:, :, seq - keep:], 2, G_NSA),
            moba_new.reshape(1, n_req, 1, 2, H_MOBA, HEAD_DIM),
            nsa_new.reshape(1, n_req, 1, 4, G_NSA, HEAD_DIM), win_s[None])
```

```python
import functools

import numpy as np
import jax
import jax.numpy as jnp
from jax import lax
from jax.experimental import pallas as pl
from jax.experimental.pallas import tpu as pltpu

F32 = jnp.float32
BF16 = jnp.bfloat16
HIGHEST = lax.Precision.HIGHEST

LANES = 128
SUBLANES = 8
HEAD_DIM = 64
H_MOBA = 8
H_NSA = 8
G_NSA = 2
R_NSA = H_NSA // G_NSA
MOBA_BLOCK = 256
MOBA_TOPK = 3
CMP_LEN = 32
CMP_STRIDE = 16
SEL_BLOCK = 64
SEL_TOPN = 16
N_FORCED = 3
WINDOW = 512
N_EXPERTS = 32
TOP_K = 4
SWIGLU_LIMIT = 7.0
SWIGLU_ALPHA = 1.702
ROPE_THETA = 10000.0
NORM_EPS = 1e-6
NEG_BIG = -1e30
TINY = 1e-30
SCALE = HEAD_DIM ** -0.5
SCALE_LOG2E = SCALE * 1.4426950408889634
Q_TILE = 256
MOBA_Q = 1024
MOBA_HEADS = 2
ATT_TK = 1024
MOE_ROWS = 256
VMEM_LIMIT = 56 * 1024 * 1024

W_QM = H_MOBA * HEAD_DIM
W_KVM = 2 * H_MOBA * HEAD_DIM
W_QN = H_NSA * HEAD_DIM
W_KVN = 6 * G_NSA * HEAD_DIM
N_GATE = 3 * H_NSA
IN_COLS = W_QM + W_KVM + W_QN + W_KVN + N_GATE
IN_COLS_PAD = W_QM + W_KVM + W_QN + W_KVN + LANES


def _iota(shape, dim):
    return lax.broadcasted_iota(jnp.int32, shape, dim)


def _dot(a, b, precision=None):
    return jnp.dot(a, b, preferred_element_type=F32, precision=precision)


def _dot_nt(a, b, precision=None):
    return lax.dot_general(a, b, (((1,), (1,)), ((), ())), preferred_element_type=F32, precision=precision)


def _split_bf16(a):
    hi = a.astype(BF16)
    return hi, (a - hi.astype(F32)).astype(BF16)


def _dot_nt_x3(a, b):
    ah, al = _split_bf16(a)
    bh, bl = _split_bf16(b)
    return _dot_nt(ah, bh) + _dot_nt(al, bh) + _dot_nt(ah, bl)


def _dot_x3(a, b):
    ah, al = _split_bf16(a)
    bh, bl = _split_bf16(b)
    return _dot(ah, bh) + _dot(al, bh) + _dot(ah, bl)


def _params(*sem):
    return pltpu.CompilerParams(dimension_semantics=sem, vmem_limit_bytes=VMEM_LIMIT)


def _seg_meansq(z, bd):
    zz = z * z
    hi = zz.astype(BF16)
    lo = (zz - hi.astype(F32)).astype(BF16)
    outs = []
    for c in range(z.shape[1] // LANES):
        sl = slice(c * LANES, (c + 1) * LANES)
        outs.append(_dot(hi[:, sl], bd) + _dot(lo[:, sl], bd))
    return outs[0] if len(outs) == 1 else jnp.concatenate(outs, axis=1)


def _head_norm(z, gain, bd):
    return z * lax.rsqrt(_seg_meansq(z, bd) + NORM_EPS) * gain


def _rope(z, cos, sin):
    outs = []
    first = (_iota((z.shape[0], LANES), 1) % HEAD_DIM) < (HEAD_DIM // 2)
    for c in range(z.shape[1] // LANES):
        x = z[:, c * LANES:(c + 1) * LANES]
        swapped = jnp.where(first, pltpu.roll(x, LANES - HEAD_DIM // 2, 1), pltpu.roll(x, HEAD_DIM // 2, 1))
        outs.append(x * cos + swapped * sin)
    return outs[0] if len(outs) == 1 else jnp.concatenate(outs, axis=1)


def _top_k_lanes(cur, k):
    lane = _iota(cur.shape, 1).astype(F32)
    picked = jnp.zeros(cur.shape, F32)
    vals, ids = [], []
    for _ in range(k):
        mx = jnp.max(cur, axis=1, keepdims=True)
        first = jnp.min(jnp.where(cur == mx, lane, 1e9), axis=1, keepdims=True)
        hit = lane == first
        picked = jnp.where(hit, 1.0, picked)
        cur = jnp.where(hit, -jnp.inf, cur)
        vals.append(mx)
        ids.append(first)
    return picked, vals, ids


def _top_k_sublanes(cur, k):
    idx = _iota(cur.shape, 0).astype(F32)
    picked = jnp.zeros(cur.shape, F32)
    for _ in range(k):
        mx = jnp.max(cur, axis=0, keepdims=True)
        first = jnp.min(jnp.where(cur == mx, idx, 1e9), axis=0, keepdims=True)
        hit = idx == first
        picked = jnp.where(hit, 1.0, picked)
        cur = jnp.where(hit, -jnp.inf, cur)
    return picked


def _flash_step(q, k, v, mask, m, acc):
    s = _dot_nt(q, k)
    if mask is not None:
        s = jnp.where(mask, s, NEG_BIG)
    m_new = jnp.maximum(m, jnp.max(s, axis=1, keepdims=True))
    acc_new = jnp.exp2(m - m_new) * acc + _dot(jnp.exp2(s - m_new).astype(BF16), v)
    return m_new, acc_new


def _flash_finish(acc):
    return acc / jnp.maximum(acc[:, HEAD_DIM:HEAD_DIM + 1], TINY)


def _ada_kernel(c_ref, w_ref, b_ref, o_ref):
    c = c_ref[...]
    o_ref[...] = _dot(c * jax.nn.sigmoid(c), w_ref[...], HIGHEST) + b_ref[...]


def _ada(c_all, w_ada, b_ada):
    n, d = c_all.shape
    cols = w_ada.shape[1]
    tn = 1024
    return pl.pallas_call(
        _ada_kernel,
        grid=(cols // tn,),
        in_specs=[pl.BlockSpec((n, d), lambda j: (0, 0)),
                  pl.BlockSpec((d, tn), lambda j: (0, j)),
                  pl.BlockSpec((1, tn), lambda j: (0, j))],
        out_specs=pl.BlockSpec((n, tn), lambda j: (0, j)),
        out_shape=jax.ShapeDtypeStruct((n, cols), F32),
        compiler_params=_params("arbitrary"),
        name="ada",
    )(c_all, w_ada, b_ada.reshape(1, cols))


def _proj_kernel(x_ref, sh_ref, sc_ref, g_ref, w_ref, gains_ref, bd_ref, cos_ref, sin_ref,
                 moba_ref, nsa_ref, win_ref, qm_ref, qn_ref, qr_ref, gate_ref, *attn_refs, with_kmean, pos_blocks):
    x = x_ref[...]
    y = x * lax.rsqrt(jnp.mean(x * x, axis=1, keepdims=True) + NORM_EPS) * g_ref[...]
    h = (y * (1.0 + sc_ref[0]) + sh_ref[0]).astype(BF16)
    bd = bd_ref[...]
    cos = cos_ref[...]
    sin = sin_ref[...]
    o = 0

    def seg(width):
        nonlocal o
        z = _dot(h, w_ref[:, o:o + width])
        o += width
        return z

    def gain(i, width):
        return gains_ref[i:i + 1, 0:width]

    qm_ref[...] = _rope(_head_norm(seg(W_QM), gain(0, W_QM), bd), cos, sin)
    k_m = _rope(_head_norm(seg(W_QM), gain(1, W_QM), bd), cos, sin)
    v_m = seg(W_QM)
    qn = _head_norm(seg(W_QN), gain(2, W_QN), bd)
    qn_ref[...] = qn
    qr_ref[...] = _rope(qn, cos, sin)
    gw = G_NSA * HEAD_DIM
    cmp_raw = seg(2 * gw)
    k_sel = _rope(_head_norm(seg(gw), gain(4, gw), bd), cos, sin)
    v_sel = seg(gw)
    k_win = _rope(_head_norm(seg(gw), gain(5, gw), bd), cos, sin)
    v_win = seg(gw)
    gates = jax.nn.sigmoid(seg(LANES))
    gate_ref[...] = gates
    moba_rows = jnp.concatenate([k_m, v_m], axis=1)
    nsa_rows = jnp.concatenate([cmp_raw, k_sel, v_sel], axis=1)
    win_rows = jnp.concatenate([k_win, v_win], axis=1)
    if not with_kmean:
        moba_ref[...] = moba_rows
        nsa_ref[...] = nsa_rows
        win_ref[...] = win_rows
    else:
        kmean_ref, cmpraw_ref, kaug_ref, vaug_ref, ksaug_ref, vsaug_ref, kwp_ref, vwaug_ref, gg_ref = attn_refs
        moba_ref[0] = moba_rows.T
        nsa_ref[0] = nsa_rows.T
        win_ref[0] = win_rows.T
        cmpraw_ref[...] = cmp_raw
        tm = k_m.shape[0]
        kmean_ref[0] = jnp.mean(k_m.reshape(tm // MOBA_BLOCK, MOBA_BLOCK, W_QM), axis=1)
        lane = _iota((tm, LANES), 1)
        posv = (pl.program_id(0) % pos_blocks) * tm + _iota((tm, LANES), 0)
        lo = lane < HEAD_DIM
        pad_blk = jnp.where((lane >= HEAD_DIM) & (lane - HEAD_DIM == posv // MOBA_BLOCK), 1.0, 0.0)
        pad_one = jnp.where(lane == HEAD_DIM, 1.0, 0.0)
        oh_sel = jnp.where(lane == posv // SEL_BLOCK, 1.0, 0.0).astype(BF16)

        def halves(x):
            return x, pltpu.roll(x, HEAD_DIM, 1)

        for c in range(W_QM // LANES):
            cols = slice(c * LANES, (c + 1) * LANES)
            for hh, (kh, vh) in enumerate(zip(halves(k_m[:, cols]), halves(v_m[:, cols]))):
                hcols = slice((2 * c + hh) * LANES, (2 * c + hh + 1) * LANES)
                kaug_ref[:, hcols] = jnp.where(lo, kh, pad_blk).astype(BF16)
                vaug_ref[:, hcols] = jnp.where(lo, vh, pad_one).astype(BF16)
        for g, (ks, vs, kw, vw) in enumerate(zip(halves(k_sel), halves(v_sel), halves(k_win), halves(v_win))):
            ksaug_ref[:, 2 * g * LANES:(2 * g + 1) * LANES] = oh_sel
            ksaug_ref[:, (2 * g + 1) * LANES:(2 * g + 2) * LANES] = jnp.where(lo, ks, 0.0).astype(BF16)
            gcols = slice(g * LANES, (g + 1) * LANES)
            vsaug_ref[:, gcols] = jnp.where(lo, vs, pad_one).astype(BF16)
            kwp_ref[:, gcols] = jnp.where(lo, kw, 0.0).astype(BF16)
            vwaug_ref[:, gcols] = jnp.where(lo, vw, pad_one).astype(BF16)
            gg_ref[:, gcols] = gates if g == 0 else pltpu.roll(gates, LANES - g * 3 * R_NSA, 1)


def _proj(x, shift, scale, g, w_in_bf, gains, bd, cos, sin, *, tm, rows_per_mod, pos_blocks, with_kmean):
    n, d = x.shape
    nt = n // tm
    mod_r = shift.shape[1]
    mod_map = lambda i: (i // (rows_per_mod // tm), 0, 0)
    pos_map = lambda i: (i % pos_blocks, 0)
    row = lambda w: pl.BlockSpec((tm, w), lambda i: (i, 0))
    const = lambda a: pl.BlockSpec(a.shape, lambda i: (0,) * a.ndim)
    gw = G_NSA * HEAD_DIM
    cache_widths = (W_KVM, 4 * gw, 2 * gw)
    if with_kmean:
        nb = n // rows_per_mod
        tpb = rows_per_mod // tm
        out_shapes = [jax.ShapeDtypeStruct((nb, w, rows_per_mod), F32) for w in cache_widths]
        out_specs = [pl.BlockSpec((1, w, tm), lambda i: (i // tpb, 0, i % tpb)) for w in cache_widths]
    else:
        out_shapes = [jax.ShapeDtypeStruct((n, w), F32) for w in cache_widths]
        out_specs = [row(w) for w in cache_widths]
    out_shapes += [jax.ShapeDtypeStruct((n, W_QM), F32), jax.ShapeDtypeStruct((n, W_QN), F32),
                   jax.ShapeDtypeStruct((n, W_QN), F32), jax.ShapeDtypeStruct((n, LANES), F32)]
    out_specs += [row(W_QM), row(W_QN), row(W_QN), row(LANES)]
    if with_kmean:
        nbt = tm // MOBA_BLOCK
        out_shapes.append(jax.ShapeDtypeStruct((nt, nbt, W_QM), F32))
        out_specs.append(pl.BlockSpec((1, nbt, W_QM), lambda i: (i, 0, 0)))
        gl = G_NSA * LANES
        for width, dtype in ((2 * gw, F32), (H_MOBA * LANES, BF16), (H_MOBA * LANES, BF16), (2 * gl, BF16), (gl, BF16),
                             (gl, BF16), (gl, BF16), (gl, F32)):
            out_shapes.append(jax.ShapeDtypeStruct((n, width), dtype))
            out_specs.append(row(width))
    return pl.pallas_call(
        functools.partial(_proj_kernel, with_kmean=with_kmean, pos_blocks=pos_blocks),
        grid=(nt,),
        in_specs=[row(d), pl.BlockSpec((1, mod_r, d), mod_map), pl.BlockSpec((1, mod_r, d), mod_map),
                  const(g), const(w_in_bf), const(gains), const(bd),
                  pl.BlockSpec((tm, LANES), pos_map), pl.BlockSpec((tm, LANES), pos_map)],
        out_specs=out_specs,
        out_shape=out_shapes,
        compiler_params=_params("parallel"),
        name="proj",
    )(x, shift, scale, g, w_in_bf, gains, bd, cos, sin)


def _compress_compute(src_refs, pe_ref, w1_ref, w2_ref, gk_ref, bd_ref, kc_ref, vc_ref, n_rows):
    n_chunk = n_rows // CMP_STRIDE
    gw = G_NSA * HEAD_DIM
    for kv in range(2):
        acc_a = jnp.zeros((n_chunk, gw), F32)
        acc_b = jnp.zeros((n_chunk, gw), F32)
        for j in range(CMP_STRIDE):
            xj = src_refs[kv][pl.ds(j, n_chunk, stride=CMP_STRIDE), :]
            acc_a = acc_a + _dot(xj + pe_ref[kv, 0, j:j + 1, :], w1_ref[kv, 0, j], HIGHEST)
            acc_b = acc_b + _dot(xj + pe_ref[kv, 1, j:j + 1, :], w1_ref[kv, 1, j], HIGHEST)
        hid = jax.nn.gelu(acc_a + pltpu.roll(acc_b, n_chunk - 1, 0))
        out = _dot(hid, w2_ref[kv], HIGHEST)
        if kv == 0:
            kc_ref[0] = _head_norm(out, gk_ref[...], bd_ref[...])
        else:
            vc_ref[0] = out


def _compress_prompt_kernel(k_ref, v_ref, pe_ref, w1_ref, w2_ref, gk_ref, bd_ref, kc_ref, vc_ref, *, n_rows):
    _compress_compute((k_ref.at[0], v_ref.at[0]), pe_ref, w1_ref, w2_ref, gk_ref, bd_ref, kc_ref, vc_ref, n_rows)


def _compress_consts(cmp_pos, cmp_w1, cmp_w2, gain_k_cmp):
    pe = jnp.tile(cmp_pos.reshape(2, 2, CMP_STRIDE, HEAD_DIM), (1, 1, 1, G_NSA))
    eye = jnp.eye(G_NSA, dtype=F32)
    w1 = cmp_w1.reshape(2, 2, CMP_STRIDE, HEAD_DIM, HEAD_DIM)
    w1bd = jnp.einsum("gh,kajde->kajgdhe", eye, w1).reshape(2, 2, CMP_STRIDE, G_NSA * HEAD_DIM, G_NSA * HEAD_DIM)
    w2bd = jnp.einsum("gh,kde->kgdhe", eye, cmp_w2).reshape(2, G_NSA * HEAD_DIM, G_NSA * HEAD_DIM)
    gk = jnp.tile(gain_k_cmp.reshape(1, HEAD_DIM), (1, G_NSA))
    return pe, w1bd, w2bd, gk


def _compress_prompt(cmp_raw, consts, bd):
    b, s, _ = cmp_raw.shape
    pe, w1bd, w2bd, gk = consts
    n_chunk = s // CMP_STRIDE
    gw = G_NSA * HEAD_DIM
    const = lambda a: pl.BlockSpec(a.shape, lambda i: (0,) * a.ndim)
    out = jax.ShapeDtypeStruct((b, n_chunk, gw), F32)
    return pl.pallas_call(
        functools.partial(_compress_prompt_kernel, n_rows=s),
        grid=(b,),
        in_specs=[pl.BlockSpec((1, s, gw), lambda i: (i, 0, 0)), pl.BlockSpec((1, s, gw), lambda i: (i, 0, 1)),
                  const(pe), const(w1bd), const(w2bd), const(gk), const(bd)],
        out_specs=[pl.BlockSpec((1, n_chunk, gw), lambda i: (i, 0, 0))] * 2,
        out_shape=[out, out],
        compiler_params=_params("parallel"),
        name="compress_prompt",
    )(cmp_raw, cmp_raw, pe, w1bd, w2bd, gk, bd)


def _moba_kernel(q_ref, kmp_ref, k_ref, v_ref, o_ref):
    qi = pl.program_id(2)
    lane = _iota((MOBA_Q, LANES), 1)
    n_blk = MOBA_BLOCK // SUBLANES
    blk = _iota((n_blk, MOBA_Q), 0)
    own = (qi * MOBA_Q + _iota((n_blk, MOBA_Q), 1)) // MOBA_BLOCK
    valid = blk < own
    q_augs = []
    for h in range(MOBA_HEADS):
        q2 = q_ref[0, :, (h // 2) * LANES:(h // 2 + 1) * LANES]
        q0 = jnp.where(lane < HEAD_DIM, q2 if h % 2 == 0 else pltpu.roll(q2, HEAD_DIM, 1), 0.0)
        score = _dot_nt_x3(kmp_ref[0, h], q0)[HEAD_DIM:HEAD_DIM + n_blk, :]
        picked = _top_k_sublanes(jnp.where(valid, score, -jnp.inf), MOBA_TOPK)
        sel = ((picked > 0.5) & valid) | (blk == own)
        bias_t = jnp.concatenate([jnp.zeros((HEAD_DIM, MOBA_Q), F32), jnp.where(sel, 0.0, NEG_BIG),
                                  jnp.zeros((LANES - HEAD_DIM - n_blk, MOBA_Q), F32)], axis=0)
        q_augs.append(jnp.where(lane < HEAD_DIM, q0 * SCALE_LOG2E, bias_t.T).astype(BF16))
    pos = qi * MOBA_Q + _iota((MOBA_Q, ATT_TK), 0)

    def tile(j, carry, masked):
        start = pl.multiple_of(j * ATT_TK, ATT_TK)
        mask = None
        if masked:
            mask = (j * ATT_TK + _iota((MOBA_Q, ATT_TK), 1)) <= pos
        return tuple(_flash_step(q_augs[h], k_ref[0, pl.ds(start, ATT_TK), h * LANES:(h + 1) * LANES],
                                 v_ref[0, pl.ds(start, ATT_TK), h * LANES:(h + 1) * LANES], mask, *carry[h])
                     for h in range(MOBA_HEADS))

    init = (jnp.full((MOBA_Q, 1), NEG_BIG, F32), jnp.zeros((MOBA_Q, LANES), F32))
    jd = (qi * MOBA_Q) // ATT_TK
    carry = lax.fori_loop(0, jd, lambda j, c: tile(j, c, False), tile(jd, (init,) * MOBA_HEADS, True))
    for c in range(MOBA_HEADS // 2):
        o_ref[0, :, c * LANES:(c + 1) * LANES] = jnp.where(
            lane < HEAD_DIM, _flash_finish(carry[2 * c][1]),
            pltpu.roll(_flash_finish(carry[2 * c + 1][1]), HEAD_DIM, 1)).astype(o_ref.dtype)


def _moba_attention(q_m, kmp, k_aug, v_aug):
    b, s, _ = q_m.shape
    nh = MOBA_HEADS
    return pl.pallas_call(
        _moba_kernel,
        grid=(b, H_MOBA // nh, s // MOBA_Q),
        in_specs=[pl.BlockSpec((1, MOBA_Q, nh * HEAD_DIM), lambda b, h, i: (b, i, h)),
                  pl.BlockSpec((1, nh, LANES, LANES), lambda b, h, i: (b, h, 0, 0)),
                  pl.BlockSpec((1, s, nh * LANES), lambda b, h, i: (b, 0, h)),
                  pl.BlockSpec((1, s, nh * LANES), lambda b, h, i: (b, 0, h))],
        out_specs=pl.BlockSpec((1, MOBA_Q, nh * HEAD_DIM), lambda b, h, i: (b, i, h)),
        out_shape=jax.ShapeDtypeStruct((b, s, W_QM), BF16),
        compiler_params=_params("parallel", "parallel", "arbitrary"),
        name="moba_attention",
    )(q_m, kmp, k_aug, v_aug)


def _stack_heads(q4):
    lane = _iota((Q_TILE, LANES), 1)
    parts = []
    for r in range(R_NSA):
        c = q4[:, (r // 2) * LANES:(r // 2 + 1) * LANES]
        if r % 2:
            c = pltpu.roll(c, HEAD_DIM, 1)
        parts.append(jnp.where(lane < HEAD_DIM, c, 0.0))
    return jnp.concatenate(parts, axis=0)


def _nsa_kernel(qn_ref, qr_ref, gate_ref, kc_ref, vc_ref, cover_ref, ks_ref, vs_ref, kw_ref, vw_ref, o_ref,
                *, n_chunk):
    qi = pl.program_id(2)
    rows = R_NSA * Q_TILE
    qloc = _iota((rows, 1), 0) % Q_TILE
    pos = qi * Q_TILE + qloc

    qn = _stack_heads(qn_ref[0])
    s = _dot_nt_x3(qn, kc_ref[0, 0]) * SCALE
    cmask = (_iota((rows, n_chunk), 1) * CMP_STRIDE + (CMP_LEN - 1)) <= pos
    s = jnp.where(cmask, s, NEG_BIG)
    e = jnp.where(cmask, jnp.exp(s - jnp.max(s, axis=1, keepdims=True)), 0.0)
    p_cmp = e * (1.0 / jnp.maximum(jnp.sum(e, axis=1, keepdims=True), TINY))
    o_cmp = _dot(p_cmp.astype(BF16), vc_ref[0, 0])

    p_grp = p_cmp[0:Q_TILE]
    for r in range(1, R_NSA):
        p_grp = p_grp + p_cmp[r * Q_TILE:(r + 1) * Q_TILE]
    p_hi, p_lo = _split_bf16(p_grp)
    imp_t = _dot_nt(cover_ref[...], p_hi) + _dot_nt(cover_ref[...], p_lo)
    blk_t = _iota((LANES, Q_TILE), 0)
    own_t = (qi * Q_TILE + _iota((LANES, Q_TILE), 1)) // SEL_BLOCK
    forced = (blk_t == 0) | (blk_t == own_t) | (blk_t == own_t - 1)
    valid_t = blk_t <= own_t
    picked = _top_k_sublanes(jnp.where(valid_t & jnp.logical_not(forced), imp_t, -jnp.inf), SEL_TOPN - N_FORCED)
    bias = jnp.where(valid_t & (forced | (picked > 0.5)), 0.0, NEG_BIG).T
    blk = _iota((Q_TILE, LANES), 1)

    qr = (_stack_heads(qr_ref[0]) * SCALE_LOG2E).astype(BF16)
    q_aug = jnp.concatenate([jnp.concatenate([bias] * R_NSA, axis=0).astype(BF16), qr], axis=1)
    init = (jnp.full((rows, 1), NEG_BIG, F32), jnp.zeros((rows, LANES), F32))

    tk = ATT_TK
    jd = (qi * Q_TILE) // tk

    def sel_tile(j, carry, masked):
        start = pl.multiple_of(j * tk, tk)
        mask = None
        if masked:
            mask = (j * tk + _iota((rows, tk), 1)) <= pos
        return _flash_step(q_aug, ks_ref[0, pl.ds(start, tk), :], vs_ref[0, pl.ds(start, tk), :], mask, *carry)

    o_sel = _flash_finish(lax.fori_loop(0, jd, lambda j, c: sel_tile(j, c, False), sel_tile(jd, init, True))[1])

    span = WINDOW + Q_TILE
    w0 = pl.multiple_of(jnp.maximum(qi * Q_TILE - WINDOW, 0), Q_TILE)
    kpos = w0 + _iota((rows, span), 1)
    wmask = (kpos <= pos) & (kpos > pos - WINDOW)
    o_win = _flash_finish(_flash_step(qr, kw_ref[0, pl.ds(w0, span), :], vw_ref[0, pl.ds(w0, span), :], wmask,
                                      *init)[1])

    gates = gate_ref[0]
    heads = []
    for r in range(R_NSA):
        rs = slice(r * Q_TILE, (r + 1) * Q_TILE)
        heads.append(gates[:, 3 * r:3 * r + 1] * o_cmp[rs] + gates[:, 3 * r + 1:3 * r + 2] * o_sel[rs]
                     + gates[:, 3 * r + 2:3 * r + 3] * o_win[rs])
    lo = blk < HEAD_DIM
    o_ref[0] = jnp.concatenate([jnp.where(lo, heads[0], pltpu.roll(heads[1], HEAD_DIM, 1)),
                                jnp.where(lo, heads[2], pltpu.roll(heads[3], HEAD_DIM, 1))], axis=1).astype(o_ref.dtype)


def _nsa_attention(qn, qr, gates_g, kcp, vcd, cover, ks_aug, vs_dup, kw_pad, vw_dup):
    b, s, _ = qn.shape
    n_chunk = kcp.shape[2]
    gq = R_NSA * HEAD_DIM
    per_g = lambda w: pl.BlockSpec((1, s, w), lambda b, g, i: (b, 0, g))
    return pl.pallas_call(
        functools.partial(_nsa_kernel, n_chunk=n_chunk),
        grid=(b, G_NSA, s // Q_TILE),
        in_specs=[pl.BlockSpec((1, Q_TILE, gq), lambda b, g, i: (b, i, g)),
                  pl.BlockSpec((1, Q_TILE, gq), lambda b, g, i: (b, i, g)),
                  pl.BlockSpec((1, Q_TILE, LANES), lambda b, g, i: (b, i, g)),
                  pl.BlockSpec((1, 1, n_chunk, LANES), lambda b, g, i: (b, g, 0, 0)),
                  pl.BlockSpec((1, 1, n_chunk, LANES), lambda b, g, i: (b, g, 0, 0)),
                  pl.BlockSpec(cover.shape, lambda b, g, i: (0, 0)),
                  per_g(2 * LANES), per_g(LANES), per_g(LANES), per_g(LANES)],
        out_specs=pl.BlockSpec((1, Q_TILE, gq), lambda b, g, i: (b, i, g)),
        out_shape=jax.ShapeDtypeStruct((b, s, W_QN), BF16),
        compiler_params=_params("parallel", "parallel", "arbitrary"),
        name="nsa_attention",
    )(qn, qr, gates_g, kcp, vcd, cover, ks_aug, vs_dup, kw_pad, vw_dup)


def _cover(n_cmp, n_sel, rows):
    c0 = np.arange(rows)[:, None] * CMP_STRIDE
    b0 = np.arange(LANES)[None, :] * SEL_BLOCK
    ok = (c0 < b0 + SEL_BLOCK) & (c0 + CMP_LEN > b0) & (np.arange(rows)[:, None] < n_cmp) & (np.arange(LANES)[None, :] < n_sel)
    return jnp.asarray(ok.astype(np.float32))


def _lane_rep(col):
    return jnp.broadcast_to(col, (col.shape[0], LANES))


def _head_sums(prod_row):
    w = prod_row.shape[1]
    own = (_iota((SUBLANES, w), 1) // HEAD_DIM) == _iota((SUBLANES, w), 0)
    return _lane_rep(jnp.sum(jnp.where(own, jnp.broadcast_to(prod_row, (SUBLANES, w)), 0.0), axis=1, keepdims=True))


def _pair_row(x8, h):
    return jnp.where(_iota((1, LANES), 1) < HEAD_DIM, x8[h:h + 1, :], x8[h + 1:h + 2, :])


def _cols_to_row(acc_a, acc_b):
    return jnp.sum(jnp.concatenate([acc_a, acc_b], axis=0).T, axis=0, keepdims=True)


def _dec_kernel(pt_ref, cm_ref, cn_ref, wb_ref, q_ref, q8n_ref, new_ref, gate_ref, wcat_ref, pecat_ref, w2_ref,
                 gk_ref, bd_ref, cover_ref, o_ref,
                 mbuf, nbuf, sem, xk, xv, qmb, qrb, s_sc, p_sc, *, n_pages, page):
    b = pl.program_id(0)
    n_req = pl.num_programs(0)
    slot = b % 2
    hw = H_MOBA * HEAD_DIM
    gw = G_NSA * HEAD_DIM
    past = n_pages * page

    def copies(req, sl):
        out = []
        for p in range(n_pages):
            pg = pt_ref[req * n_pages + p]
            out.append(pltpu.make_async_copy(cm_ref.at[pg], mbuf.at[sl, p], sem.at[0, sl]))
            out.append(pltpu.make_async_copy(cn_ref.at[pg], nbuf.at[sl, p], sem.at[1, sl]))
        return out

    @pl.when(b == 0)
    def _():
        for c in copies(0, 0):
            c.start()

    for c in copies(b, slot):
        c.wait()

    @pl.when(b + 1 < n_req)
    def _():
        for c in copies(b + 1, 1 - slot):
            c.start()

    lane1 = _iota((1, LANES), 1)
    lane8 = _iota((SUBLANES, LANES), 1)
    qrow = q_ref[0]
    new = new_ref[0]
    for c in range(hw // LANES):
        cols = slice(c * LANES, (c + 1) * LANES)
        qmb[cols, :] = jnp.broadcast_to(qrow[0:1, cols] * SCALE, (LANES, LANES)).T
        qrb[cols, :] = jnp.broadcast_to(qrow[1:2, cols] * SCALE, (LANES, LANES)).T

    def softmax_pv(scores, s_new8, v_rows, vbuf_ref, v_row0, per_g):
        m8 = s_new8
        for s in scores:
            m8 = jnp.maximum(m8, _lane_rep(jnp.max(s, axis=1, keepdims=True)))
        w_new = jnp.exp(s_new8 - m8)
        l8 = w_new
        for p, s in enumerate(scores):
            pr = jnp.exp(s - m8)
            p_sc[p] = pr
            l8 = l8 + _lane_rep(jnp.sum(pr, axis=1, keepdims=True))
        inv8 = 1.0 / jnp.maximum(l8, TINY)
        rows = []
        for hp in range(SUBLANES // 2):
            accs = []
            for h in (2 * hp, 2 * hp + 1):
                r0 = v_row0 + (h // R_NSA if per_g else h) * HEAD_DIM

                def body(p, acc, h=h, r0=r0):
                    return acc + vbuf_ref[slot, p, r0:r0 + HEAD_DIM, :] * p_sc[p, h:h + 1, :]
                acc = jnp.zeros((HEAD_DIM, LANES), F32)
                for p in range(n_pages):
                    acc = body(p, acc)
                accs.append(acc)
            row = _cols_to_row(accs[0], accs[1])
            cols = slice(hp * LANES, (hp + 1) * LANES)
            rows.append((row + _pair_row(w_new, 2 * hp) * v_rows[:, cols]) * _pair_row(inv8, 2 * hp))
        return rows

    def moba_scores(p, carry):
        rows = [jnp.sum(mbuf[slot, p, h * HEAD_DIM:(h + 1) * HEAD_DIM, :] * qmb[h * HEAD_DIM:(h + 1) * HEAD_DIM, :],
                        axis=0, keepdims=True) for h in range(H_MOBA)]
        s_sc[p] = jnp.concatenate(rows, axis=0)
        return carry
    for p in range(n_pages):
        moba_scores(p, 0)
    s_all = [s_sc[p] for p in range(n_pages)]
    ppb = MOBA_BLOCK // page
    n_blk = n_pages // ppb
    bsc = []
    for j in range(n_blk):
        tot = s_all[j * ppb]
        for t in range(1, ppb):
            tot = tot + s_all[j * ppb + t]
        bsc.append(_lane_rep(jnp.sum(tot, axis=1, keepdims=True)))
    masked = []
    for j in range(n_blk):
        rank = jnp.zeros((SUBLANES, LANES), F32)
        for c in range(n_blk):
            if c != j:
                ahead = (bsc[c] > bsc[j]) | ((bsc[c] == bsc[j]) & (c < j))
                rank = rank + jnp.where(ahead, 1.0, 0.0)
        for t in range(ppb):
            masked.append(jnp.where(rank < MOBA_TOPK, s_all[j * ppb + t], NEG_BIG))
    s_own = _head_sums(qrow[0:1, :] * new[0:1, :]) * SCALE
    o_rows = softmax_pv(masked, s_own, new[1:2, :], mbuf, hw, False)

    for p in range(n_pages):
        xk[p * page:(p + 1) * page, :] = nbuf[slot, p, 0:gw, :].T
        xv[p * page:(p + 1) * page, :] = nbuf[slot, p, gw:2 * gw, :].T
    n_chunk = past // CMP_STRIDE
    cmp_out = []
    for kv, xref in enumerate((xk, xv)):
        xcat = jnp.concatenate([xref[pl.ds(j, n_chunk, stride=CMP_STRIDE), :] for j in range(CMP_STRIDE)], axis=1)
        pe2 = _dot(pecat_ref[kv].astype(BF16), wcat_ref[kv])
        ab = _dot(xcat.astype(BF16), wcat_ref[kv])
        hid = jax.nn.gelu(ab[:, 0:gw] + pe2[0:1, 0:gw] + pltpu.roll(ab[:, gw:2 * gw] + pe2[1:2, gw:2 * gw],
                                                                    n_chunk - 1, 0))
        cmp_out.append(_dot_x3(hid, w2_ref[kv]))
    kc = _head_norm(cmp_out[0], gk_ref[...], bd_ref[...])
    vc = cmp_out[1]

    s = _dot_nt_x3(q8n_ref[0], kc) * SCALE
    cmask = _iota((SUBLANES, n_chunk), 1) < (n_chunk - 1)
    s = jnp.where(cmask, s, NEG_BIG)
    e = jnp.where(cmask, jnp.exp(s - jnp.max(s, axis=1, keepdims=True)), 0.0)
    p_cmp = e / jnp.maximum(jnp.sum(e, axis=1, keepdims=True), TINY)
    o_cmp8 = _dot(p_cmp.astype(BF16), vc.astype(BF16))
    subc = _iota((SUBLANES, n_chunk), 0)
    g0 = jnp.sum(jnp.where(subc < R_NSA, p_cmp, 0.0), axis=0, keepdims=True)
    g1 = jnp.sum(jnp.where(subc >= R_NSA, p_cmp, 0.0), axis=0, keepdims=True)
    p_grp = jnp.concatenate([jnp.where(subc < R_NSA, g0, g1), jnp.zeros((LANES - SUBLANES, n_chunk), F32)], axis=0)
    own = past // SEL_BLOCK
    n_sel = -(-(own + 1) // SUBLANES) * SUBLANES
    p_hi, p_lo = _split_bf16(p_grp)
    imp_t = (_dot_nt(cover_ref[...], p_hi) + _dot_nt(cover_ref[...], p_lo))[0:n_sel]
    blk_t = _iota((n_sel, LANES), 0)
    forced = (blk_t == 0) | (blk_t == own) | (blk_t == own - 1)
    score = jnp.where(blk_t <= own, jnp.where(forced, jnp.inf, imp_t), -jnp.inf)
    picked = _top_k_sublanes(score, SEL_TOPN)
    bias_t = jnp.concatenate([jnp.where((picked > 0.5) & (blk_t <= own), 0.0, NEG_BIG),
                              jnp.full((LANES - n_sel, LANES), NEG_BIG, F32)], axis=0)
    bias8 = bias_t.T[0:SUBLANES, :]
    cmp_rows = []
    for hp in range(H_NSA // 2):
        g = (2 * hp) // R_NSA
        ra = o_cmp8[2 * hp:2 * hp + 1, :]
        rb = o_cmp8[2 * hp + 1:2 * hp + 2, :]
        cmp_rows.append(jnp.where(lane1 < HEAD_DIM, ra if g == 0 else pltpu.roll(ra, HEAD_DIM, 1),
                                  rb if g == 1 else pltpu.roll(rb, HEAD_DIM, 1)))

    def sel_scores(p, carry):
        rows = [jnp.sum(nbuf[slot, p, 2 * gw + (i // R_NSA) * HEAD_DIM:2 * gw + (i // R_NSA + 1) * HEAD_DIM, :]
                        * qrb[i * HEAD_DIM:(i + 1) * HEAD_DIM, :], axis=0, keepdims=True) for i in range(H_NSA)]
        s_sc[p] = jnp.concatenate(rows, axis=0)
        return carry
    for p in range(n_pages):
        sel_scores(p, 0)
    bpp = page // SEL_BLOCK
    sel_s = []
    for p in range(n_pages):
        bias_p = bias8[:, p * bpp:p * bpp + 1]
        for t in range(1, bpp):
            bias_p = jnp.where(lane8 < t * SEL_BLOCK, bias_p, bias8[:, p * bpp + t:p * bpp + t + 1])
        sel_s.append(s_sc[p] + bias_p)
    s_new = _head_sums(qrow[1:2, :] * new[2:3, :]) * SCALE
    sel_rows = softmax_pv(sel_s, s_new, new[3:4, :], nbuf, 3 * gw, True)

    nw = wb_ref.shape[2]
    wk = nw // LANES
    w_s = []
    for i in range(H_NSA):
        g = i // R_NSA
        qcol = jnp.concatenate([qrb[i * HEAD_DIM:(i + 1) * HEAD_DIM, :]] * wk, axis=1)
        w_s.append(jnp.sum(wb_ref[0, g * HEAD_DIM:(g + 1) * HEAD_DIM, :] * qcol, axis=0, keepdims=True))
    s = jnp.concatenate(w_s, axis=0)
    wmask = _iota((SUBLANES, nw), 1) >= 1
    s = jnp.where(wmask, s, NEG_BIG)
    s_new = _head_sums(qrow[1:2, :] * new[4:5, :]) * SCALE
    m8 = jnp.maximum(_lane_rep(jnp.max(s, axis=1, keepdims=True)), s_new)
    pw = jnp.where(wmask, jnp.exp(s - m8[:, 0:1]), 0.0)
    w_new = jnp.exp(s_new - m8)
    inv8 = 1.0 / jnp.maximum(_lane_rep(jnp.sum(pw, axis=1, keepdims=True)) + w_new, TINY)
    win_rows = []
    for hp in range(H_NSA // 2):
        accs = []
        for i in (2 * hp, 2 * hp + 1):
            g = i // R_NSA
            prod = wb_ref[0, gw + g * HEAD_DIM:gw + (g + 1) * HEAD_DIM, :] * pw[i:i + 1, :]
            acc = prod[:, 0:LANES]
            for c in range(1, wk):
                acc = acc + prod[:, c * LANES:(c + 1) * LANES]
            accs.append(acc)
        cols = slice(hp * LANES, (hp + 1) * LANES)
        win_rows.append((_cols_to_row(accs[0], accs[1]) + _pair_row(w_new, 2 * hp) * new[5:6, cols])
                        * _pair_row(inv8, 2 * hp))

    gates = gate_ref[0]
    for c in range(hw // LANES):
        o_ref[0, :, c * LANES:(c + 1) * LANES] = o_rows[c]
    for c in range(W_QN // LANES):
        cols = slice(c * LANES, (c + 1) * LANES)
        o_ref[0, :, hw + c * LANES:hw + (c + 1) * LANES] = (
            gates[0:1, cols] * cmp_rows[c] + gates[1:2, cols] * sel_rows[c] + gates[2:3, cols] * win_rows[c])


def _dec_attention(cm, cn, wb, pt_flat, qrows, q8n, new, gate_rows, wcat, pecat, w2bd, gk, bd, cover,
                    *, n_req, n_pages):
    page = cm.shape[2]
    n_chunk = n_pages * page // CMP_STRIDE
    req = lambda a: pl.BlockSpec((1,) + a.shape[1:], lambda b, pt: (b,) + (0,) * (a.ndim - 1))
    const = lambda a: pl.BlockSpec(a.shape, lambda b, pt: (0,) * a.ndim)
    return pl.pallas_call(
        functools.partial(_dec_kernel, n_pages=n_pages, page=page),
        grid_spec=pltpu.PrefetchScalarGridSpec(
            num_scalar_prefetch=1,
            grid=(n_req,),
            in_specs=[pl.BlockSpec(memory_space=pl.ANY), pl.BlockSpec(memory_space=pl.ANY), req(wb), req(qrows),
                      req(q8n), req(new), req(gate_rows), const(wcat), const(pecat), const(w2bd), const(gk),
                      const(bd), const(cover)],
            out_specs=pl.BlockSpec((1, 1, W_QM + W_QN), lambda b, pt: (b, 0, 0)),
            scratch_shapes=[pltpu.VMEM((2, n_pages) + cm.shape[1:], F32), pltpu.VMEM((2, n_pages) + cn.shape[1:], F32),
                            pltpu.SemaphoreType.DMA((2, 2)),
                            pltpu.VMEM((n_pages * page, LANES), F32), pltpu.VMEM((n_pages * page, LANES), F32),
                            pltpu.VMEM((W_QM, LANES), F32), pltpu.VMEM((W_QN, LANES), F32),
                            pltpu.VMEM((n_pages, SUBLANES, LANES), F32), pltpu.VMEM((n_pages, SUBLANES, LANES), F32)]),
        out_shape=jax.ShapeDtypeStruct((n_req, 1, W_QM + W_QN), F32),
        compiler_params=_params("arbitrary"),
        name="dec_attention",
    )(pt_flat, cm, cn, wb, qrows, q8n, new, gate_rows, wcat, pecat, w2bd, gk, bd, cover)


def _compress_cat_consts(cmp_pos, cmp_w1):
    eye = jnp.eye(G_NSA, dtype=F32)
    w1 = cmp_w1.reshape(2, 2, CMP_STRIDE, HEAD_DIM, HEAD_DIM)
    wcat = jnp.einsum("gh,kajde->kjgdahe", eye, w1).reshape(2, CMP_STRIDE * G_NSA * HEAD_DIM, 2 * G_NSA * HEAD_DIM)
    pe = jnp.tile(cmp_pos.reshape(2, 2, CMP_STRIDE, 1, HEAD_DIM), (1, 1, 1, G_NSA, 1))
    pecat = jnp.pad(pe.reshape(2, 2, CMP_STRIDE * G_NSA * HEAD_DIM), ((0, 0), (0, SUBLANES - 2), (0, 0)))
    return wcat.astype(BF16), pecat


def _post_kernel(o_ref, x_ref, gt_ref, sh_ref, sc_ref, g_ref, wo_ref, wr_ref, br_ref,
                 y_ref, h3_ref, te_ref, tw_ref):
    y = x_ref[...] + gt_ref[0] * _dot(o_ref[...], wo_ref[...])
    y_ref[...] = y
    h = y * lax.rsqrt(jnp.mean(y * y, axis=1, keepdims=True) + NORM_EPS) * g_ref[...]
    h = h * (1.0 + sc_ref[0]) + sh_ref[0]
    tm = h.shape[0]
    for s in range(h.shape[1] // LANES):
        h3_ref[pl.ds(s, tm, stride=SUBLANES), :] = h[:, s * LANES:(s + 1) * LANES]
    h_hi, h_lo = _split_bf16(h)
    w_hi, w_lo = _split_bf16(wr_ref[...])
    logits = _dot(h_hi, w_hi) + _dot(h_lo, w_hi) + _dot(h_hi, w_lo) + br_ref[...]
    _, vals, ids = _top_k_lanes(logits, TOP_K)
    lane = _iota((tm, LANES), 1)
    es = [jnp.exp(v - vals[0]) for v in vals]
    den = es[0]
    for e in es[1:]:
        den = den + e
    te = jnp.zeros((tm, LANES), F32)
    tw = jnp.zeros((tm, LANES), F32)
    for k in range(TOP_K):
        te = jnp.where(lane == k, ids[k], te)
        tw = jnp.where(lane == k, es[k] / den, tw)
    te_ref[...] = te.astype(jnp.int32)
    tw_ref[...] = tw


def _post(o, x, gate, shift, scale, g2, wo_bf, wr_pad, br_pad, *, tm, rows_per_mod):
    n, d = x.shape
    mod_r = gate.shape[1]
    mod_map = lambda i: (i // (rows_per_mod // tm), 0, 0)
    row = lambda w: pl.BlockSpec((tm, w), lambda i: (i, 0))
    const = lambda a: pl.BlockSpec(a.shape, lambda i: (0,) * a.ndim)
    mod = pl.BlockSpec((1, mod_r, d), mod_map)
    return pl.pallas_call(
        _post_kernel,
        grid=(n // tm,),
        in_specs=[row(d), row(d), mod, mod, mod, const(g2), const(wo_bf), const(wr_pad), const(br_pad)],
        out_specs=[row(d), pl.BlockSpec((tm * SUBLANES, LANES), lambda i: (i, 0)), row(LANES), row(LANES)],
        out_shape=[jax.ShapeDtypeStruct((n, d), F32), jax.ShapeDtypeStruct((n * SUBLANES, LANES), F32),
                   jax.ShapeDtypeStruct((n, LANES), jnp.int32), jax.ShapeDtypeStruct((n, LANES), F32)],
        compiler_params=_params("parallel"),
        name="post",
    )(o, x, gate, shift, scale, g2, wo_bf, wr_pad, br_pad)


def _dispatch_kernel(pos_ref, h3_ref, xs_in_ref, xs_ref, sem, *, tm, tile_off):
    del xs_in_ref
    i = pl.program_id(0)

    def body(r, carry):
        src = h3_ref.at[pl.ds(pl.multiple_of(r * SUBLANES, SUBLANES), SUBLANES), :]
        for k in range(TOP_K):
            p = pos_ref[((i + tile_off) * tm + r) * TOP_K + k]
            pltpu.make_async_copy(src, xs_ref.at[pl.ds(pl.multiple_of(p * SUBLANES, SUBLANES), SUBLANES), :],
                                  sem.at[0]).start()
        return carry
    lax.fori_loop(0, tm, body, 0)
    for k in range(TOP_K):
        pltpu.make_async_copy(h3_ref, xs_ref.at[pl.ds(0, tm * SUBLANES), :], sem.at[0]).wait()


def _dispatch(pos_flat, h3, xs, *, tm, tile_off):
    n8 = h3.shape[0]
    return pl.pallas_call(
        functools.partial(_dispatch_kernel, tm=tm, tile_off=tile_off),
        grid_spec=pltpu.PrefetchScalarGridSpec(
            num_scalar_prefetch=1,
            grid=(n8 // (tm * SUBLANES),),
            in_specs=[pl.BlockSpec((tm * SUBLANES, LANES), lambda i, pos: (i, 0)), pl.BlockSpec(memory_space=pl.ANY)],
            out_specs=pl.BlockSpec(memory_space=pl.ANY),
            scratch_shapes=[pltpu.SemaphoreType.DMA((1,))]),
        out_shape=jax.ShapeDtypeStruct(xs.shape, xs.dtype),
        input_output_aliases={2: 0},
        compiler_params=_params("arbitrary"),
        name="dispatch",
    )(pos_flat, h3, xs)


def _expert_block_kernel(be_ref, na_ref, x_ref, wgu_ref, bgu_ref, wd_ref, bd_ref, y_ref, xb, wgu_bf, wd_bf):
    i = pl.program_id(0)
    active = i < na_ref[0]

    @pl.when(active & ((i == 0) | (be_ref[i] != be_ref[jnp.maximum(i - 1, 0)])))
    def _():
        for r in range(0, wgu_bf.shape[0], LANES):
            wgu_bf[r:r + LANES, :] = wgu_ref[0, r:r + LANES, :].astype(BF16)
        for r in range(0, wd_bf.shape[0], LANES):
            wd_bf[r:r + LANES, :] = wd_ref[0, r:r + LANES, :].astype(BF16)

    @pl.when(active)
    def _():
        d = xb.shape[1]
        for s in range(d // LANES):
            xb[:, s * LANES:(s + 1) * LANES] = x_ref[pl.ds(s, MOE_ROWS, stride=SUBLANES), :].astype(BF16)
        gu = _dot(xb[...], wgu_bf[...]) + bgu_ref[0]
        f = gu.shape[1] // 2
        gt = jnp.minimum(gu[:, 0:f], SWIGLU_LIMIT)
        up = jnp.clip(gu[:, f:2 * f], -SWIGLU_LIMIT, SWIGLU_LIMIT)
        act = (up + 1.0) * (gt * jax.nn.sigmoid(SWIGLU_ALPHA * gt))
        y = _dot(act.astype(BF16), wd_bf[...]) + bd_ref[0]
        for s in range(d // LANES):
            y_ref[pl.ds(s, MOE_ROWS, stride=SUBLANES), :] = y[:, s * LANES:(s + 1) * LANES]

    @pl.when(i >= na_ref[0])
    def _():
        y_ref[...] = jnp.zeros(y_ref.shape, F32)


def _expert_blocks(blk_e, n_active, xs, w_gu, b_gu, w_down, b_down):
    n_blocks = blk_e.shape[0]
    e, d, f2 = w_gu.shape
    rows8 = MOE_ROWS * SUBLANES
    row_map = lambda i, be, na: (jnp.minimum(i, na[0] - 1), 0)
    return pl.pallas_call(
        _expert_block_kernel,
        grid_spec=pltpu.PrefetchScalarGridSpec(
            num_scalar_prefetch=2,
            grid=(n_blocks,),
            in_specs=[pl.BlockSpec((rows8, LANES), row_map),
                      pl.BlockSpec((1, d, f2), lambda i, be, na: (be[i], 0, 0)),
                      pl.BlockSpec((1, 1, f2), lambda i, be, na: (be[i], 0, 0)),
                      pl.BlockSpec((1, f2 // 2, d), lambda i, be, na: (be[i], 0, 0)),
                      pl.BlockSpec((1, 1, d), lambda i, be, na: (be[i], 0, 0))],
            out_specs=pl.BlockSpec((rows8, LANES), lambda i, be, na: (i, 0)),
            scratch_shapes=[pltpu.VMEM((MOE_ROWS, d), BF16), pltpu.VMEM((d, f2), BF16),
                            pltpu.VMEM((f2 // 2, d), BF16)]),
        out_shape=jax.ShapeDtypeStruct((n_blocks * rows8, LANES), F32),
        compiler_params=_params("arbitrary"),
        name="experts",
    )(blk_e, n_active, xs, w_gu, b_gu.reshape(e, 1, f2), w_down, b_down.reshape(e, 1, d))


def _routing_pos(top_e, n_tok):
    n_assign = n_tok * TOP_K
    n_pad = -(-n_assign // LANES) * LANES
    e_flat = jnp.pad(top_e.reshape(-1), (0, n_pad - n_assign), constant_values=N_EXPERTS)
    onehot = (e_flat[:, None] == jnp.arange(N_EXPERTS)[None, :]).astype(F32).reshape(n_pad // LANES, LANES, N_EXPERTS)
    tril = jnp.tril(jnp.ones((LANES, LANES), F32))
    within = jnp.einsum("ij,bjk->bik", tril, onehot)
    block_tot = within[:, -1, :]
    offs = jnp.cumsum(block_tot, axis=0) - block_tot
    counts = jnp.sum(block_tot, axis=0).astype(jnp.int32)
    padded = (counts + MOE_ROWS - 1) // MOE_ROWS * MOE_ROWS
    pad_end = jnp.cumsum(padded)
    pad_start = (pad_end - padded).astype(F32)
    slot = jnp.sum(onehot * (within + offs[:, None, :] - 1.0 + pad_start[None, None, :]), axis=-1)
    pos = slot.reshape(-1)[:n_assign].astype(jnp.int32)
    n_blocks = -(-n_assign // MOE_ROWS) + N_EXPERTS
    starts = jnp.arange(n_blocks, dtype=jnp.int32) * MOE_ROWS
    blk_e = jnp.minimum(jnp.sum((pad_end[None, :] <= starts[:, None]).astype(jnp.int32), axis=1), N_EXPERTS - 1)
    n_active = (pad_end[-1] // MOE_ROWS).astype(jnp.int32).reshape(1)
    return blk_e, n_active, pos


def _combine_kernel(pos_ref, ys_ref, y1_ref, gt_ref, tw_ref, o_ref, buf, sem, *, tile_off):
    i = pl.program_id(0)
    n = pl.num_programs(0)
    tm = y1_ref.shape[0]
    rows8 = tm * SUBLANES

    def gather(tile, slot):
        def body(r, carry):
            for k in range(TOP_K):
                p = pos_ref[((tile + tile_off) * tm + r) * TOP_K + k]
                pltpu.make_async_copy(ys_ref.at[pl.ds(pl.multiple_of(p * SUBLANES, SUBLANES), SUBLANES), :],
                                      buf.at[slot, k, pl.ds(pl.multiple_of(r * SUBLANES, SUBLANES), SUBLANES), :],
                                      sem.at[slot]).start()
            return carry
        lax.fori_loop(0, tm, body, 0)

    @pl.when(i == 0)
    def _():
        gather(0, 0)

    slot = i % 2
    for k in range(TOP_K):
        pltpu.make_async_copy(ys_ref.at[pl.ds(0, rows8), :], buf.at[slot, k], sem.at[slot]).wait()

    @pl.when(i + 1 < n)
    def _():
        gather(i + 1, 1 - slot)

    tw = tw_ref[...]
    wk = [jnp.broadcast_to(tw[:, k:k + 1], (tm, LANES)) for k in range(TOP_K)]
    gt = gt_ref[0]
    for s in range(o_ref.shape[1] // LANES):
        moe = wk[0] * buf[slot, 0, pl.ds(s, tm, stride=SUBLANES), :]
        for k in range(1, TOP_K):
            moe = moe + wk[k] * buf[slot, k, pl.ds(s, tm, stride=SUBLANES), :]
        cols = slice(s * LANES, (s + 1) * LANES)
        o_ref[:, cols] = y1_ref[:, cols] + gt[:, cols] * moe


def _combine(pos_flat, ys, y1, gate, tw, *, tm, rows_per_mod, tile_off):
    n, d = y1.shape
    mod_r = gate.shape[1]
    return pl.pallas_call(
        functools.partial(_combine_kernel, tile_off=tile_off),
        grid_spec=pltpu.PrefetchScalarGridSpec(
            num_scalar_prefetch=1,
            grid=(n // tm,),
            in_specs=[pl.BlockSpec(memory_space=pl.ANY),
                      pl.BlockSpec((tm, d), lambda i, pos: (i, 0)),
                      pl.BlockSpec((1, mod_r, d), lambda i, pos: (i // (rows_per_mod // tm), 0, 0)),
                      pl.BlockSpec((tm, LANES), lambda i, pos: (i, 0))],
            out_specs=pl.BlockSpec((tm, d), lambda i, pos: (i, 0)),
            scratch_shapes=[pltpu.VMEM((2, TOP_K, tm * SUBLANES, LANES), F32), pltpu.SemaphoreType.DMA((2,))]),
        out_shape=jax.ShapeDtypeStruct((n, d), F32),
        compiler_params=_params("arbitrary"),
        name="combine",
    )(pos_flat, ys, y1, gate, tw)


def _rope_tables(pos):
    half = HEAD_DIM // 2
    inv = ROPE_THETA ** (-jnp.arange(half, dtype=F32) / half)
    ang = pos.astype(F32)[:, None] * inv[None, :]
    cos = jnp.cos(ang)
    sin = jnp.sin(ang)
    reps = LANES // HEAD_DIM
    return (jnp.tile(jnp.concatenate([cos, cos], axis=1), (1, reps)),
            jnp.tile(jnp.concatenate([-sin, sin], axis=1), (1, reps)))


def _head_pad(a):
    return jnp.concatenate([a, jnp.zeros_like(a)], axis=-1)


def kernel(x_prompt, x_sample, c_prompt, c_sample, cache_moba_kv, cache_nsa_kv, state_nsa_win_kv, page_table,
           norm_g, w_ada, b_ada, w_in, qk_gain, cmp_pos, cmp_w1, cmp_w2, w_out, w_router, b_router, w_gu, b_gu,
           w_down, b_down):
    bsz, seq, d = x_prompt.shape
    n_req = x_sample.shape[0]
    depth = norm_g.shape[0]
    assert depth == 1 and x_sample.shape[1] == 1
    assert seq % ATT_TK == 0 and seq >= WINDOW + Q_TILE and seq // SEL_BLOCK <= LANES and seq // MOBA_BLOCK <= MOBA_BLOCK // SUBLANES
    n_pool, page = cache_moba_kv.shape[1], cache_moba_kv.shape[2]
    n_pages = page_table.shape[1]
    past = n_pages * page
    assert past % MOBA_BLOCK == 0 and 2 * page == MOBA_BLOCK and past // SEL_BLOCK < LANES
    assert state_nsa_win_kv.shape[2] == WINDOW
    layer = 0
    gw = G_NSA * HEAD_DIM
    n_prompt = bsz * seq

    bd = jnp.asarray(np.kron(np.eye(LANES // HEAD_DIM), np.full((HEAD_DIM, HEAD_DIM), 1.0 / HEAD_DIM)), BF16)
    w_in_bf = jnp.pad(w_in[layer], ((0, 0), (0, IN_COLS_PAD - IN_COLS))).astype(BF16)
    gains = jnp.tile(qk_gain[layer], (1, W_QM // HEAD_DIM))
    g1 = norm_g[layer, 0].reshape(1, d)
    g2 = norm_g[layer, 1].reshape(1, d)
    wo_bf = w_out[layer].astype(BF16)
    wr_pad = jnp.pad(w_router[layer], ((0, 0), (0, LANES - N_EXPERTS)))
    br_pad = jnp.pad(b_router[layer].reshape(1, N_EXPERTS), ((0, 0), (0, LANES - N_EXPERTS)),
                     constant_values=-jnp.inf)
    cmp_consts = _compress_consts(cmp_pos[layer], cmp_w1[layer], cmp_w2[layer], qk_gain[layer, 3])

    n_c = bsz + n_req
    n_c_pad = -(-n_c // SUBLANES) * SUBLANES
    c_all = jnp.pad(jnp.concatenate([c_prompt, c_sample], axis=0), ((0, n_c_pad - n_c), (0, 0)))
    mods = _ada(c_all, w_ada[layer], b_ada[layer])
    mods_p = [m.reshape(bsz, 1, d) for m in jnp.split(mods[:bsz], 6, axis=1)]
    mods_s = [m.reshape(1, n_req, d) for m in jnp.split(mods[bsz:n_c], 6, axis=1)]

    cos_p, sin_p = _rope_tables(jnp.arange(seq, dtype=jnp.int32))
    tm_p = 256
    (moba_t, nsa_t, win_t, q_m, qn, qr, _, kmean, cmp_raw, k_aug, v_aug, ks_aug, vs_aug, kw_pad, vw_aug,
     gates_g) = _proj(
        x_prompt.reshape(n_prompt, d), mods_p[0], mods_p[1], g1, w_in_bf, gains, bd, cos_p, sin_p,
        tm=tm_p, rows_per_mod=seq, pos_blocks=seq // tm_p, with_kmean=True)

    nbk = seq // MOBA_BLOCK
    kmean_h = kmean.reshape(bsz, nbk, H_MOBA, HEAD_DIM).transpose(0, 2, 1, 3)
    kmp = jnp.zeros((bsz, H_MOBA, LANES, LANES), F32).at[:, :, HEAD_DIM:HEAD_DIM + nbk, :HEAD_DIM].set(kmean_h)
    per_b = lambda a: a.reshape(bsz, seq, a.shape[-1])
    o_m = _moba_attention(per_b(q_m), kmp, per_b(k_aug), per_b(v_aug))

    kc, vc = _compress_prompt(per_b(cmp_raw), cmp_consts, bd)
    n_chunk = seq // CMP_STRIDE
    per_g = lambda a: a.reshape(bsz, -1, G_NSA, HEAD_DIM).transpose(0, 2, 1, 3)
    kcp = _head_pad(per_g(kc))
    vcd = _head_pad(per_g(vc)).astype(BF16)
    cover_p = _cover(n_chunk - 1, seq // SEL_BLOCK, n_chunk).T.astype(BF16)
    o_n = _nsa_attention(per_b(qn), per_b(qr), per_b(gates_g), kcp, vcd, cover_p, per_b(ks_aug), per_b(vs_aug),
                         per_b(kw_pad), per_b(vw_aug))
    o_p = jnp.concatenate([o_m, o_n], axis=-1).reshape(n_prompt, d)

    y1_p, h3_p, te_p, tw_p = _post(o_p, x_prompt.reshape(n_prompt, d), mods_p[2], mods_p[3], mods_p[4], g2, wo_bf,
                                   wr_pad, br_pad, tm=256, rows_per_mod=seq)

    cos_s, sin_s = _rope_tables(jnp.full((n_req,), past, jnp.int32))
    moba_new, nsa_new, win_new, q_m_s, qn_s, qr_s, gates_s = _proj(
        x_sample.reshape(n_req, d), mods_s[0], mods_s[1], g1, w_in_bf, gains, bd, cos_s, sin_s,
        tm=n_req, rows_per_mod=n_req, pos_blocks=1, with_kmean=False)
    pt_flat = page_table.reshape(-1).astype(jnp.int32)
    cache_m = cache_moba_kv[layer].transpose(0, 2, 3, 4, 1).reshape(n_pool, W_KVM, page)
    cache_n = cache_nsa_kv[layer].transpose(0, 2, 3, 4, 1).reshape(n_pool, 4 * gw, page)
    win_buf = state_nsa_win_kv[layer].transpose(0, 2, 3, 4, 1).reshape(n_req, 2 * gw, WINDOW)

    def rows8(q):
        qh = q.reshape(n_req, G_NSA, R_NSA, 1, HEAD_DIM)
        place = jnp.arange(G_NSA)[None, :, None, None, None] == jnp.arange(G_NSA)[None, None, None, :, None]
        return jnp.where(place, qh, 0.0).reshape(n_req, H_NSA, gw)

    def per_head(a):
        return jnp.repeat(a.reshape(n_req, G_NSA, HEAD_DIM), R_NSA, axis=1).reshape(n_req, W_QN)

    new_rows = jnp.stack([moba_new[:, :W_QM], moba_new[:, W_QM:], per_head(nsa_new[:, 2 * gw:3 * gw]),
                          per_head(nsa_new[:, 3 * gw:]), per_head(win_new[:, :gw]), per_head(win_new[:, gw:])], axis=1)
    gate_rows = jnp.repeat(gates_s[:, :N_GATE].reshape(n_req, H_NSA, 3).transpose(0, 2, 1), HEAD_DIM, axis=2)
    n_cmp_s = past // CMP_STRIDE
    cover_s = _cover(n_cmp_s - 1, past // SEL_BLOCK + 1, n_cmp_s).T.astype(BF16)
    wcat, pecat = _compress_cat_consts(cmp_pos[layer], cmp_w1[layer])
    o_s = _dec_attention(cache_m, cache_n, win_buf, pt_flat, jnp.stack([q_m_s, qr_s], axis=1), rows8(qn_s), new_rows,
                          gate_rows, wcat, pecat, cmp_consts[2], cmp_consts[3], bd, cover_s,
                          n_req=n_req, n_pages=n_pages).reshape(n_req, d).astype(BF16)
    y1_s, h3_s, te_s, tw_s = _post(o_s, x_sample.reshape(n_req, d), mods_s[2], mods_s[3], mods_s[4], g2, wo_bf,
                                   wr_pad, br_pad, tm=n_req, rows_per_mod=n_req)

    n_tok = n_prompt + n_req
    top_e = jnp.concatenate([te_p[:, :TOP_K], te_s[:, :TOP_K]], axis=0)
    blk_e, n_active, pos_flat = _routing_pos(top_e, n_tok)
    tm_c = 128
    xs = jnp.zeros((blk_e.shape[0] * MOE_ROWS * SUBLANES, LANES), F32)
    xs = _dispatch(pos_flat, h3_p, xs, tm=2 * tm_c, tile_off=0)
    xs = _dispatch(pos_flat, h3_s, xs, tm=n_req, tile_off=n_prompt // n_req)
    ys = _expert_blocks(blk_e, n_active, xs, w_gu[layer], b_gu[layer], w_down[layer], b_down[layer])
    y_p = _combine(pos_flat, ys, y1_p, mods_p[5], tw_p, tm=tm_c, rows_per_mod=seq, tile_off=0)
    y_s = _combine(pos_flat, ys, y1_s, mods_s[5], tw_s, tm=n_req, rows_per_mod=n_req, tile_off=n_prompt // n_req)

    def rows_view(t, n_slot, n_head):
        return t.reshape(1, bsz, n_slot, n_head, HEAD_DIM, t.shape[-1]).transpose(0, 1, 5, 2, 3, 4)

    keep = min(WINDOW, seq)
    win_s = jnp.concatenate([state_nsa_win_kv[layer][:, 1:], win_new.reshape(n_req, 1, 2, G_NSA, HEAD_DIM)], axis=1)
    return (y_p.reshape(bsz, seq, d), y_s.reshape(n_req, 1, d),
            rows_view(moba_t, 2, H_MOBA), rows_view(nsa_t, 4, G_NSA), rows_view(win_t[:, :, seq - keep:], 2, G_NSA),
            moba_new.reshape(1, n_req, 1, 2, H_MOBA, HEAD_DIM),
            nsa_new.reshape(1, n_req, 1, 4, G_NSA, HEAD_DIM), win_s[None])
```

```python
import functools

import numpy as np
import jax
import jax.numpy as jnp
from jax import lax
from jax.experimental import pallas as pl
from jax.experimental.pallas import tpu as pltpu

F32 = jnp.float32
BF16 = jnp.bfloat16
HIGHEST = lax.Precision.HIGHEST

LANES = 128
SUBLANES = 8
HEAD_DIM = 64
H_MOBA = 8
H_NSA = 8
G_NSA = 2
R_NSA = H_NSA // G_NSA
MOBA_BLOCK = 256
MOBA_TOPK = 3
CMP_LEN = 32
CMP_STRIDE = 16
SEL_BLOCK = 64
SEL_TOPN = 16
N_FORCED = 3
WINDOW = 512
N_EXPERTS = 32
TOP_K = 4
SWIGLU_LIMIT = 7.0
SWIGLU_ALPHA = 1.702
ROPE_THETA = 10000.0
NORM_EPS = 1e-6
NEG_BIG = -1e30
TINY = 1e-30
SCALE = HEAD_DIM ** -0.5
SCALE_LOG2E = SCALE * 1.4426950408889634
Q_TILE = 256
MOBA_Q = 1024
MOBA_HEADS = 2
ATT_TK = 1024
MOE_ROWS = 256
VMEM_LIMIT = 56 * 1024 * 1024

W_QM = H_MOBA * HEAD_DIM
W_KVM = 2 * H_MOBA * HEAD_DIM
W_QN = H_NSA * HEAD_DIM
W_KVN = 6 * G_NSA * HEAD_DIM
N_GATE = 3 * H_NSA
IN_COLS = W_QM + W_KVM + W_QN + W_KVN + N_GATE
IN_COLS_PAD = W_QM + W_KVM + W_QN + W_KVN + LANES


def _iota(shape, dim):
    return lax.broadcasted_iota(jnp.int32, shape, dim)


def _dot(a, b, precision=None):
    return jnp.dot(a, b, preferred_element_type=F32, precision=precision)


def _dot_nt(a, b, precision=None):
    return lax.dot_general(a, b, (((1,), (1,)), ((), ())), preferred_element_type=F32, precision=precision)


def _split_bf16(a):
    hi = a.astype(BF16)
    return hi, (a - hi.astype(F32)).astype(BF16)


def _dot_nt_x3(a, b):
    ah, al = _split_bf16(a)
    bh, bl = _split_bf16(b)
    return _dot_nt(ah, bh) + _dot_nt(al, bh) + _dot_nt(ah, bl)


def _dot_x3(a, b):
    ah, al = _split_bf16(a)
    bh, bl = _split_bf16(b)
    return _dot(ah, bh) + _dot(al, bh) + _dot(ah, bl)


def _params(*sem):
    return pltpu.CompilerParams(dimension_semantics=sem, vmem_limit_bytes=VMEM_LIMIT)


def _seg_meansq(z, bd):
    zz = z * z
    hi = zz.astype(BF16)
    lo = (zz - hi.astype(F32)).astype(BF16)
    outs = []
    for c in range(z.shape[1] // LANES):
        sl = slice(c * LANES, (c + 1) * LANES)
        outs.append(_dot(hi[:, sl], bd) + _dot(lo[:, sl], bd))
    return outs[0] if len(outs) == 1 else jnp.concatenate(outs, axis=1)


def _head_norm(z, gain, bd):
    return z * lax.rsqrt(_seg_meansq(z, bd) + NORM_EPS) * gain


def _rope(z, cos, sin):
    outs = []
    first = (_iota((z.shape[0], LANES), 1) % HEAD_DIM) < (HEAD_DIM // 2)
    for c in range(z.shape[1] // LANES):
        x = z[:, c * LANES:(c + 1) * LANES]
        swapped = jnp.where(first, pltpu.roll(x, LANES - HEAD_DIM // 2, 1), pltpu.roll(x, HEAD_DIM // 2, 1))
        outs.append(x * cos + swapped * sin)
    return outs[0] if len(outs) == 1 else jnp.concatenate(outs, axis=1)


def _top_k_lanes(cur, k):
    lane = _iota(cur.shape, 1).astype(F32)
    picked = jnp.zeros(cur.shape, F32)
    vals, ids = [], []
    for _ in range(k):
        mx = jnp.max(cur, axis=1, keepdims=True)
        first = jnp.min(jnp.where(cur == mx, lane, 1e9), axis=1, keepdims=True)
        hit = lane == first
        picked = jnp.where(hit, 1.0, picked)
        cur = jnp.where(hit, -jnp.inf, cur)
        vals.append(mx)
        ids.append(first)
    return picked, vals, ids


def _top_k_sublanes(cur, k):
    idx = _iota(cur.shape, 0).astype(F32)
    picked = jnp.zeros(cur.shape, F32)
    for _ in range(k):
        mx = jnp.max(cur, axis=0, keepdims=True)
        first = jnp.min(jnp.where(cur == mx, idx, 1e9), axis=0, keepdims=True)
        hit = idx == first
        picked = jnp.where(hit, 1.0, picked)
        cur = jnp.where(hit, -jnp.inf, cur)
    return picked


def _flash_step(q, k, v, mask, m, acc):
    s = _dot_nt(q, k)
    if mask is not None:
        s = jnp.where(mask, s, NEG_BIG)
    m_new = jnp.maximum(m, jnp.max(s, axis=1, keepdims=True))
    acc_new = jnp.exp2(m - m_new) * acc + _dot(jnp.exp2(s - m_new).astype(BF16), v)
    return m_new, acc_new


def _flash_finish(acc):
    return acc / jnp.maximum(acc[:, HEAD_DIM:HEAD_DIM + 1], TINY)


def _ada_kernel(c_ref, w_ref, b_ref, o_ref):
    c = c_ref[...]
    o_ref[...] = _dot(c * jax.nn.sigmoid(c), w_ref[...], HIGHEST) + b_ref[...]


def _ada(c_all, w_ada, b_ada):
    n, d = c_all.shape
    cols = w_ada.shape[1]
    tn = 1024
    return pl.pallas_call(
        _ada_kernel,
        grid=(cols // tn,),
        in_specs=[pl.BlockSpec((n, d), lambda j: (0, 0)),
                  pl.BlockSpec((d, tn), lambda j: (0, j)),
                  pl.BlockSpec((1, tn), lambda j: (0, j))],
        out_specs=pl.BlockSpec((n, tn), lambda j: (0, j)),
        out_shape=jax.ShapeDtypeStruct((n, cols), F32),
        compiler_params=_params("arbitrary"),
        name="ada",
    )(c_all, w_ada, b_ada.reshape(1, cols))


def _proj_kernel(x_ref, sh_ref, sc_ref, g_ref, w_ref, gains_ref, bd_ref, cos_ref, sin_ref,
                 moba_ref, nsa_ref, win_ref, qm_ref, qn_ref, qr_ref, gate_ref, *attn_refs, with_kmean, pos_blocks):
    x = x_ref[...]
    y = x * lax.rsqrt(jnp.mean(x * x, axis=1, keepdims=True) + NORM_EPS) * g_ref[...]
    h = (y * (1.0 + sc_ref[0]) + sh_ref[0]).astype(BF16)
    bd = bd_ref[...]
    cos = cos_ref[...]
    sin = sin_ref[...]
    o = 0

    def seg(width):
        nonlocal o
        z = _dot(h, w_ref[:, o:o + width])
        o += width
        return z

    def gain(i, width):
        return gains_ref[i:i + 1, 0:width]

    qm_ref[...] = _rope(_head_norm(seg(W_QM), gain(0, W_QM), bd), cos, sin)
    k_m = _rope(_head_norm(seg(W_QM), gain(1, W_QM), bd), cos, sin)
    v_m = seg(W_QM)
    qn = _head_norm(seg(W_QN), gain(2, W_QN), bd)
    qn_ref[...] = qn
    qr_ref[...] = _rope(qn, cos, sin)
    gw = G_NSA * HEAD_DIM
    cmp_raw = seg(2 * gw)
    k_sel = _rope(_head_norm(seg(gw), gain(4, gw), bd), cos, sin)
    v_sel = seg(gw)
    k_win = _rope(_head_norm(seg(gw), gain(5, gw), bd), cos, sin)
    v_win = seg(gw)
    gates = jax.nn.sigmoid(seg(LANES))
    gate_ref[...] = gates
    moba_rows = jnp.concatenate([k_m, v_m], axis=1)
    nsa_rows = jnp.concatenate([cmp_raw, k_sel, v_sel], axis=1)
    win_rows = jnp.concatenate([k_win, v_win], axis=1)
    if not with_kmean:
        moba_ref[...] = moba_rows
        nsa_ref[...] = nsa_rows
        win_ref[...] = win_rows
    else:
        kmean_ref, cmpraw_ref, kaug_ref, vaug_ref, ksaug_ref, vsaug_ref, kwp_ref, vwaug_ref, gg_ref = attn_refs
        moba_ref[0] = moba_rows.T
        nsa_ref[0] = nsa_rows.T
        win_ref[0] = win_rows.T
        cmpraw_ref[...] = cmp_raw
        tm = k_m.shape[0]
        kmean_ref[0] = jnp.mean(k_m.reshape(tm // MOBA_BLOCK, MOBA_BLOCK, W_QM), axis=1)
        lane = _iota((tm, LANES), 1)
        posv = (pl.program_id(0) % pos_blocks) * tm + _iota((tm, LANES), 0)
        lo = lane < HEAD_DIM
        pad_blk = jnp.where((lane >= HEAD_DIM) & (lane - HEAD_DIM == posv // MOBA_BLOCK), 1.0, 0.0)
        pad_one = jnp.where(lane == HEAD_DIM, 1.0, 0.0)
        oh_sel = jnp.where(lane == posv // SEL_BLOCK, 1.0, 0.0).astype(BF16)

        def halves(x):
            return x, pltpu.roll(x, HEAD_DIM, 1)

        for c in range(W_QM // LANES):
            cols = slice(c * LANES, (c + 1) * LANES)
            for hh, (kh, vh) in enumerate(zip(halves(k_m[:, cols]), halves(v_m[:, cols]))):
                hcols = slice((2 * c + hh) * LANES, (2 * c + hh + 1) * LANES)
                kaug_ref[:, hcols] = jnp.where(lo, kh, pad_blk).astype(BF16)
                vaug_ref[:, hcols] = jnp.where(lo, vh, pad_one).astype(BF16)
        for g, (ks, vs, kw, vw) in enumerate(zip(halves(k_sel), halves(v_sel), halves(k_win), halves(v_win))):
            ksaug_ref[:, 2 * g * LANES:(2 * g + 1) * LANES] = oh_sel
            ksaug_ref[:, (2 * g + 1) * LANES:(2 * g + 2) * LANES] = jnp.where(lo, ks, 0.0).astype(BF16)
            gcols = slice(g * LANES, (g + 1) * LANES)
            vsaug_ref[:, gcols] = jnp.where(lo, vs, pad_one).astype(BF16)
            kwp_ref[:, gcols] = jnp.where(lo, kw, 0.0).astype(BF16)
            vwaug_ref[:, gcols] = jnp.where(lo, vw, pad_one).astype(BF16)
            gg_ref[:, gcols] = gates if g == 0 else pltpu.roll(gates, LANES - g * 3 * R_NSA, 1)


def _proj(x, shift, scale, g, w_in_bf, gains, bd, cos, sin, *, tm, rows_per_mod, pos_blocks, with_kmean):
    n, d = x.shape
    nt = n // tm
    mod_r = shift.shape[1]
    mod_map = lambda i: (i // (rows_per_mod // tm), 0, 0)
    pos_map = lambda i: (i % pos_blocks, 0)
    row = lambda w: pl.BlockSpec((tm, w), lambda i: (i, 0))
    const = lambda a: pl.BlockSpec(a.shape, lambda i: (0,) * a.ndim)
    gw = G_NSA * HEAD_DIM
    cache_widths = (W_KVM, 4 * gw, 2 * gw)
    if with_kmean:
        nb = n // rows_per_mod
        tpb = rows_per_mod // tm
        out_shapes = [jax.ShapeDtypeStruct((nb, w, rows_per_mod), F32) for w in cache_widths]
        out_specs = [pl.BlockSpec((1, w, tm), lambda i: (i // tpb, 0, i % tpb)) for w in cache_widths]
    else:
        out_shapes = [jax.ShapeDtypeStruct((n, w), F32) for w in cache_widths]
        out_specs = [row(w) for w in cache_widths]
    out_shapes += [jax.ShapeDtypeStruct((n, W_QM), F32), jax.ShapeDtypeStruct((n, W_QN), F32),
                   jax.ShapeDtypeStruct((n, W_QN), F32), jax.ShapeDtypeStruct((n, LANES), F32)]
    out_specs += [row(W_QM), row(W_QN), row(W_QN), row(LANES)]
    if with_kmean:
        nbt = tm // MOBA_BLOCK
        out_shapes.append(jax.ShapeDtypeStruct((nt, nbt, W_QM), F32))
        out_specs.append(pl.BlockSpec((1, nbt, W_QM), lambda i: (i, 0, 0)))
        gl = G_NSA * LANES
        for width, dtype in ((2 * gw, F32), (H_MOBA * LANES, BF16), (H_MOBA * LANES, BF16), (2 * gl, BF16), (gl, BF16),
                             (gl, BF16), (gl, BF16), (gl, F32)):
            out_shapes.append(jax.ShapeDtypeStruct((n, width), dtype))
            out_specs.append(row(width))
    return pl.pallas_call(
        functools.partial(_proj_kernel, with_kmean=with_kmean, pos_blocks=pos_blocks),
        grid=(nt,),
        in_specs=[row(d), pl.BlockSpec((1, mod_r, d), mod_map), pl.BlockSpec((1, mod_r, d), mod_map),
                  const(g), const(w_in_bf), const(gains), const(bd),
                  pl.BlockSpec((tm, LANES), pos_map), pl.BlockSpec((tm, LANES), pos_map)],
        out_specs=out_specs,
        out_shape=out_shapes,
        compiler_params=_params("parallel"),
        name="proj",
    )(x, shift, scale, g, w_in_bf, gains, bd, cos, sin)


def _compress_compute(src_refs, pe_ref, w1_ref, w2_ref, gk_ref, bd_ref, kc_ref, vc_ref, n_rows):
    n_chunk = n_rows // CMP_STRIDE
    gw = G_NSA * HEAD_DIM
    for kv in range(2):
        acc_a = jnp.zeros((n_chunk, gw), F32)
        acc_b = jnp.zeros((n_chunk, gw), F32)
        for j in range(CMP_STRIDE):
            xj = src_refs[kv][pl.ds(j, n_chunk, stride=CMP_STRIDE), :]
            acc_a = acc_a + _dot(xj + pe_ref[kv, 0, j:j + 1, :], w1_ref[kv, 0, j], HIGHEST)
            acc_b = acc_b + _dot(xj + pe_ref[kv, 1, j:j + 1, :], w1_ref[kv, 1, j], HIGHEST)
        hid = jax.nn.gelu(acc_a + pltpu.roll(acc_b, n_chunk - 1, 0))
        out = _dot(hid, w2_ref[kv], HIGHEST)
        if kv == 0:
            kc_ref[0] = _head_norm(out, gk_ref[...], bd_ref[...])
        else:
            vc_ref[0] = out


def _compress_prompt_kernel(k_ref, v_ref, pe_ref, w1_ref, w2_ref, gk_ref, bd_ref, kc_ref, vc_ref, *, n_rows):
    _compress_compute((k_ref.at[0], v_ref.at[0]), pe_ref, w1_ref, w2_ref, gk_ref, bd_ref, kc_ref, vc_ref, n_rows)


def _compress_consts(cmp_pos, cmp_w1, cmp_w2, gain_k_cmp):
    pe = jnp.tile(cmp_pos.reshape(2, 2, CMP_STRIDE, HEAD_DIM), (1, 1, 1, G_NSA))
    eye = jnp.eye(G_NSA, dtype=F32)
    w1 = cmp_w1.reshape(2, 2, CMP_STRIDE, HEAD_DIM, HEAD_DIM)
    w1bd = jnp.einsum("gh,kajde->kajgdhe", eye, w1).reshape(2, 2, CMP_STRIDE, G_NSA * HEAD_DIM, G_NSA * HEAD_DIM)
    w2bd = jnp.einsum("gh,kde->kgdhe", eye, cmp_w2).reshape(2, G_NSA * HEAD_DIM, G_NSA * HEAD_DIM)
    gk = jnp.tile(gain_k_cmp.reshape(1, HEAD_DIM), (1, G_NSA))
    return pe, w1bd, w2bd, gk


def _compress_prompt(cmp_raw, consts, bd):
    b, s, _ = cmp_raw.shape
    pe, w1bd, w2bd, gk = consts
    n_chunk = s // CMP_STRIDE
    gw = G_NSA * HEAD_DIM
    const = lambda a: pl.BlockSpec(a.shape, lambda i: (0,) * a.ndim)
    out = jax.ShapeDtypeStruct((b, n_chunk, gw), F32)
    return pl.pallas_call(
        functools.partial(_compress_prompt_kernel, n_rows=s),
        grid=(b,),
        in_specs=[pl.BlockSpec((1, s, gw), lambda i: (i, 0, 0)), pl.BlockSpec((1, s, gw), lambda i: (i, 0, 1)),
                  const(pe), const(w1bd), const(w2bd), const(gk), const(bd)],
        out_specs=[pl.BlockSpec((1, n_chunk, gw), lambda i: (i, 0, 0))] * 2,
        out_shape=[out, out],
        compiler_params=_params("parallel"),
        name="compress_prompt",
    )(cmp_raw, cmp_raw, pe, w1bd, w2bd, gk, bd)


def _moba_kernel(q_ref, kmp_ref, k_ref, v_ref, o_ref):
    qi = pl.program_id(2)
    lane = _iota((MOBA_Q, LANES), 1)
    n_blk = MOBA_BLOCK // SUBLANES
    blk = _iota((n_blk, MOBA_Q), 0)
    own = (qi * MOBA_Q + _iota((n_blk, MOBA_Q), 1)) // MOBA_BLOCK
    valid = blk < own
    q_augs = []
    for h in range(MOBA_HEADS):
        q2 = q_ref[0, :, (h // 2) * LANES:(h // 2 + 1) * LANES]
        q0 = jnp.where(lane < HEAD_DIM, q2 if h % 2 == 0 else pltpu.roll(q2, HEAD_DIM, 1), 0.0)
        score = _dot_nt_x3(kmp_ref[0, h], q0)[HEAD_DIM:HEAD_DIM + n_blk, :]
        picked = _top_k_sublanes(jnp.where(valid, score, -jnp.inf), MOBA_TOPK)
        sel = ((picked > 0.5) & valid) | (blk == own)
        bias_t = jnp.concatenate([jnp.zeros((HEAD_DIM, MOBA_Q), F32), jnp.where(sel, 0.0, NEG_BIG),
                                  jnp.zeros((LANES - HEAD_DIM - n_blk, MOBA_Q), F32)], axis=0)
        q_augs.append(jnp.where(lane < HEAD_DIM, q0 * SCALE_LOG2E, bias_t.T).astype(BF16))
    pos = qi * MOBA_Q + _iota((MOBA_Q, ATT_TK), 0)

    def tile(j, carry, masked):
        start = pl.multiple_of(j * ATT_TK, ATT_TK)
        mask = None
        if masked:
            mask = (j * ATT_TK + _iota((MOBA_Q, ATT_TK), 1)) <= pos
        return tuple(_flash_step(q_augs[h], k_ref[0, pl.ds(start, ATT_TK), h * LANES:(h + 1) * LANES],
                                 v_ref[0, pl.ds(start, ATT_TK), h * LANES:(h + 1) * LANES], mask, *carry[h])
                     for h in range(MOBA_HEADS))

    init = (jnp.full((MOBA_Q, 1), NEG_BIG, F32), jnp.zeros((MOBA_Q, LANES), F32))
    jd = (qi * MOBA_Q) // ATT_TK
    carry = lax.fori_loop(0, jd, lambda j, c: tile(j, c, False), tile(jd, (init,) * MOBA_HEADS, True))
    for c in range(MOBA_HEADS // 2):
        o_ref[0, :, c * LANES:(c + 1) * LANES] = jnp.where(
            lane < HEAD_DIM, _flash_finish(carry[2 * c][1]),
            pltpu.roll(_flash_finish(carry[2 * c + 1][1]), HEAD_DIM, 1)).astype(o_ref.dtype)


def _moba_attention(q_m, kmp, k_aug, v_aug):
    b, s, _ = q_m.shape
    nh = MOBA_HEADS
    return pl.pallas_call(
        _moba_kernel,
        grid=(b, H_MOBA // nh, s // MOBA_Q),
        in_specs=[pl.BlockSpec((1, MOBA_Q, nh * HEAD_DIM), lambda b, h, i: (b, i, h)),
                  pl.BlockSpec((1, nh, LANES, LANES), lambda b, h, i: (b, h, 0, 0)),
                  pl.BlockSpec((1, s, nh * LANES), lambda b, h, i: (b, 0, h)),
                  pl.BlockSpec((1, s, nh * LANES), lambda b, h, i: (b, 0, h))],
        out_specs=pl.BlockSpec((1, MOBA_Q, nh * HEAD_DIM), lambda b, h, i: (b, i, h)),
        out_shape=jax.ShapeDtypeStruct((b, s, W_QM), BF16),
        compiler_params=_params("parallel", "parallel", "arbitrary"),
        name="moba_attention",
    )(q_m, kmp, k_aug, v_aug)


def _stack_heads(q4):
    lane = _iota((Q_TILE, LANES), 1)
    parts = []
    for r in range(R_NSA):
        c = q4[:, (r // 2) * LANES:(r // 2 + 1) * LANES]
        if r % 2:
            c = pltpu.roll(c, HEAD_DIM, 1)
        parts.append(jnp.where(lane < HEAD_DIM, c, 0.0))
    return jnp.concatenate(parts, axis=0)


def _nsa_kernel(qn_ref, qr_ref, gate_ref, kc_ref, vc_ref, cover_ref, ks_ref, vs_ref, kw_ref, vw_ref, o_ref,
                *, n_chunk):
    qi = pl.program_id(2)
    rows = R_NSA * Q_TILE
    qloc = _iota((rows, 1), 0) % Q_TILE
    pos = qi * Q_TILE + qloc

    qn = _stack_heads(qn_ref[0])
    s = _dot_nt_x3(qn, kc_ref[0, 0]) * SCALE
    cmask = (_iota((rows, n_chunk), 1) * CMP_STRIDE + (CMP_LEN - 1)) <= pos
    s = jnp.where(cmask, s, NEG_BIG)
    e = jnp.where(cmask, jnp.exp(s - jnp.max(s, axis=1, keepdims=True)), 0.0)
    p_cmp = e * (1.0 / jnp.maximum(jnp.sum(e, axis=1, keepdims=True), TINY))
    o_cmp = _dot(p_cmp.astype(BF16), vc_ref[0, 0])

    p_grp = p_cmp[0:Q_TILE]
    for r in range(1, R_NSA):
        p_grp = p_grp + p_cmp[r * Q_TILE:(r + 1) * Q_TILE]
    p_hi, p_lo = _split_bf16(p_grp)
    imp_t = _dot_nt(cover_ref[...], p_hi) + _dot_nt(cover_ref[...], p_lo)
    blk_t = _iota((LANES, Q_TILE), 0)
    own_t = (qi * Q_TILE + _iota((LANES, Q_TILE), 1)) // SEL_BLOCK
    forced = (blk_t == 0) | (blk_t == own_t) | (blk_t == own_t - 1)
    valid_t = blk_t <= own_t
    picked = _top_k_sublanes(jnp.where(valid_t & jnp.logical_not(forced), imp_t, -jnp.inf), SEL_TOPN - N_FORCED)
    bias = jnp.where(valid_t & (forced | (picked > 0.5)), 0.0, NEG_BIG).T
    blk = _iota((Q_TILE, LANES), 1)

    qr = (_stack_heads(qr_ref[0]) * SCALE_LOG2E).astype(BF16)
    q_aug = jnp.concatenate([jnp.concatenate([bias] * R_NSA, axis=0).astype(BF16), qr], axis=1)
    init = (jnp.full((rows, 1), NEG_BIG, F32), jnp.zeros((rows, LANES), F32))

    tk = ATT_TK
    jd = (qi * Q_TILE) // tk

    def sel_tile(j, carry, masked):
        start = pl.multiple_of(j * tk, tk)
        mask = None
        if masked:
            mask = (j * tk + _iota((rows, tk), 1)) <= pos
        return _flash_step(q_aug, ks_ref[0, pl.ds(start, tk), :], vs_ref[0, pl.ds(start, tk), :], mask, *carry)

    o_sel = _flash_finish(lax.fori_loop(0, jd, lambda j, c: sel_tile(j, c, False), sel_tile(jd, init, True))[1])

    span = WINDOW + Q_TILE
    w0 = pl.multiple_of(jnp.maximum(qi * Q_TILE - WINDOW, 0), Q_TILE)
    kpos = w0 + _iota((rows, span), 1)
    wmask = (kpos <= pos) & (kpos > pos - WINDOW)
    o_win = _flash_finish(_flash_step(qr, kw_ref[0, pl.ds(w0, span), :], vw_ref[0, pl.ds(w0, span), :], wmask,
                                      *init)[1])

    gates = gate_ref[0]
    heads = []
    for r in range(R_NSA):
        rs = slice(r * Q_TILE, (r + 1) * Q_TILE)
        heads.append(gates[:, 3 * r:3 * r + 1] * o_cmp[rs] + gates[:, 3 * r + 1:3 * r + 2] * o_sel[rs]
                     + gates[:, 3 * r + 2:3 * r + 3] * o_win[rs])
    lo = blk < HEAD_DIM
    o_ref[0] = jnp.concatenate([jnp.where(lo, heads[0], pltpu.roll(heads[1], HEAD_DIM, 1)),
                                jnp.where(lo, heads[2], pltpu.roll(heads[3], HEAD_DIM, 1))], axis=1).astype(o_ref.dtype)


def _nsa_attention(qn, qr, gates_g, kcp, vcd, cover, ks_aug, vs_dup, kw_pad, vw_dup):
    b, s, _ = qn.shape
    n_chunk = kcp.shape[2]
    gq = R_NSA * HEAD_DIM
    per_g = lambda w: pl.BlockSpec((1, s, w), lambda b, g, i: (b, 0, g))
    return pl.pallas_call(
        functools.partial(_nsa_kernel, n_chunk=n_chunk),
        grid=(b, G_NSA, s // Q_TILE),
        in_specs=[pl.BlockSpec((1, Q_TILE, gq), lambda b, g, i: (b, i, g)),
                  pl.BlockSpec((1, Q_TILE, gq), lambda b, g, i: (b, i, g)),
                  pl.BlockSpec((1, Q_TILE, LANES), lambda b, g, i: (b, i, g)),
                  pl.BlockSpec((1, 1, n_chunk, LANES), lambda b, g, i: (b, g, 0, 0)),
                  pl.BlockSpec((1, 1, n_chunk, LANES), lambda b, g, i: (b, g, 0, 0)),
                  pl.BlockSpec(cover.shape, lambda b, g, i: (0, 0)),
                  per_g(2 * LANES), per_g(LANES), per_g(LANES), per_g(LANES)],
        out_specs=pl.BlockSpec((1, Q_TILE, gq), lambda b, g, i: (b, i, g)),
        out_shape=jax.ShapeDtypeStruct((b, s, W_QN), BF16),
        compiler_params=_params("parallel", "parallel", "arbitrary"),
        name="nsa_attention",
    )(qn, qr, gates_g, kcp, vcd, cover, ks_aug, vs_dup, kw_pad, vw_dup)


def _cover(n_cmp, n_sel, rows):
    c0 = np.arange(rows)[:, None] * CMP_STRIDE
    b0 = np.arange(LANES)[None, :] * SEL_BLOCK
    ok = (c0 < b0 + SEL_BLOCK) & (c0 + CMP_LEN > b0) & (np.arange(rows)[:, None] < n_cmp) & (np.arange(LANES)[None, :] < n_sel)
    return jnp.asarray(ok.astype(np.float32))


def _lane_rep(col):
    return jnp.broadcast_to(col, (col.shape[0], LANES))


def _head_sums(prod_row):
    w = prod_row.shape[1]
    own = (_iota((SUBLANES, w), 1) // HEAD_DIM) == _iota((SUBLANES, w), 0)
    return _lane_rep(jnp.sum(jnp.where(own, jnp.broadcast_to(prod_row, (SUBLANES, w)), 0.0), axis=1, keepdims=True))


def _pair_row(x8, h):
    return jnp.where(_iota((1, LANES), 1) < HEAD_DIM, x8[h:h + 1, :], x8[h + 1:h + 2, :])


def _cols_to_row(acc_a, acc_b):
    return jnp.sum(jnp.concatenate([acc_a, acc_b], axis=0).T, axis=0, keepdims=True)


def _dec_kernel(pt_ref, cm_ref, cn_ref, wb_ref, q_ref, q8n_ref, new_ref, gate_ref, wcat_ref, pecat_ref, w2_ref,
                 gk_ref, bd_ref, cover_ref, o_ref,
                 mbuf, nbuf, sem, xk, xv, qmb, qrb, s_sc, p_sc, *, n_pages, page):
    b = pl.program_id(0)
    n_req = pl.num_programs(0)
    slot = b % 2
    hw = H_MOBA * HEAD_DIM
    gw = G_NSA * HEAD_DIM
    past = n_pages * page

    def copies(req, sl):
        out = []
        for p in range(n_pages):
            pg = pt_ref[req * n_pages + p]
            out.append(pltpu.make_async_copy(cm_ref.at[pg], mbuf.at[sl, p], sem.at[0, sl]))
            out.append(pltpu.make_async_copy(cn_ref.at[pg], nbuf.at[sl, p], sem.at[1, sl]))
        return out

    @pl.when(b == 0)
    def _():
        for c in copies(0, 0):
            c.start()

    for c in copies(b, slot):
        c.wait()

    @pl.when(b + 1 < n_req)
    def _():
        for c in copies(b + 1, 1 - slot):
            c.start()

    lane1 = _iota((1, LANES), 1)
    lane8 = _iota((SUBLANES, LANES), 1)
    qrow = q_ref[0]
    new = new_ref[0]
    for c in range(hw // LANES):
        cols = slice(c * LANES, (c + 1) * LANES)
        qmb[cols, :] = jnp.broadcast_to(qrow[0:1, cols] * SCALE, (LANES, LANES)).T
        qrb[cols, :] = jnp.broadcast_to(qrow[1:2, cols] * SCALE, (LANES, LANES)).T

    def softmax_pv(scores, s_new8, v_rows, vbuf_ref, v_row0, per_g):
        m8 = s_new8
        for s in scores:
            m8 = jnp.maximum(m8, _lane_rep(jnp.max(s, axis=1, keepdims=True)))
        w_new = jnp.exp(s_new8 - m8)
        l8 = w_new
        for p, s in enumerate(scores):
            pr = jnp.exp(s - m8)
            p_sc[p] = pr
            l8 = l8 + _lane_rep(jnp.sum(pr, axis=1, keepdims=True))
        inv8 = 1.0 / jnp.maximum(l8, TINY)
        rows = []
        for hp in range(SUBLANES // 2):
            accs = []
            for h in (2 * hp, 2 * hp + 1):
                r0 = v_row0 + (h // R_NSA if per_g else h) * HEAD_DIM

                def body(p, acc, h=h, r0=r0):
                    return acc + vbuf_ref[slot, p, r0:r0 + HEAD_DIM, :] * p_sc[p, h:h + 1, :]
                acc = jnp.zeros((HEAD_DIM, LANES), F32)
                for p in range(n_pages):
                    acc = body(p, acc)
                accs.append(acc)
            row = _cols_to_row(accs[0], accs[1])
            cols = slice(hp * LANES, (hp + 1) * LANES)
            rows.append((row + _pair_row(w_new, 2 * hp) * v_rows[:, cols]) * _pair_row(inv8, 2 * hp))
        return rows

    def moba_scores(p, carry):
        rows = [jnp.sum(mbuf[slot, p, h * HEAD_DIM:(h + 1) * HEAD_DIM, :] * qmb[h * HEAD_DIM:(h + 1) * HEAD_DIM, :],
                        axis=0, keepdims=True) for h in range(H_MOBA)]
        s_sc[p] = jnp.concatenate(rows, axis=0)
        return carry
    for p in range(n_pages):
        moba_scores(p, 0)
    s_all = [s_sc[p] for p in range(n_pages)]
    ppb = MOBA_BLOCK // page
    n_blk = n_pages // ppb
    bsc = []
    for j in range(n_blk):
        tot = s_all[j * ppb]
        for t in range(1, ppb):
            tot = tot + s_all[j * ppb + t]
        bsc.append(_lane_rep(jnp.sum(tot, axis=1, keepdims=True)))
    masked = []
    for j in range(n_blk):
        rank = jnp.zeros((SUBLANES, LANES), F32)
        for c in range(n_blk):
            if c != j:
                ahead = (bsc[c] > bsc[j]) | ((bsc[c] == bsc[j]) & (c < j))
                rank = rank + jnp.where(ahead, 1.0, 0.0)
        for t in range(ppb):
            masked.append(jnp.where(rank < MOBA_TOPK, s_all[j * ppb + t], NEG_BIG))
    s_own = _head_sums(qrow[0:1, :] * new[0:1, :]) * SCALE
    o_rows = softmax_pv(masked, s_own, new[1:2, :], mbuf, hw, False)

    for p in range(n_pages):
        xk[p * page:(p + 1) * page, :] = nbuf[slot, p, 0:gw, :].T
        xv[p * page:(p + 1) * page, :] = nbuf[slot, p, gw:2 * gw, :].T
    n_chunk = past // CMP_STRIDE
    cmp_out = []
    for kv, xref in enumerate((xk, xv)):
        xcat = jnp.concatenate([xref[pl.ds(j, n_chunk, stride=CMP_STRIDE), :] for j in range(CMP_STRIDE)], axis=1)
        pe2 = _dot(pecat_ref[kv].astype(BF16), wcat_ref[kv])
        ab = _dot(xcat.astype(BF16), wcat_ref[kv])
        hid = jax.nn.gelu(ab[:, 0:gw] + pe2[0:1, 0:gw] + pltpu.roll(ab[:, gw:2 * gw] + pe2[1:2, gw:2 * gw],
                                                                    n_chunk - 1, 0))
        cmp_out.append(_dot_x3(hid, w2_ref[kv]))
    kc = _head_norm(cmp_out[0], gk_ref[...], bd_ref[...])
    vc = cmp_out[1]

    s = _dot_nt_x3(q8n_ref[0], kc) * SCALE
    cmask = _iota((SUBLANES, n_chunk), 1) < (n_chunk - 1)
    s = jnp.where(cmask, s, NEG_BIG)
    e = jnp.where(cmask, jnp.exp(s - jnp.max(s, axis=1, keepdims=True)), 0.0)
    p_cmp = e / jnp.maximum(jnp.sum(e, axis=1, keepdims=True), TINY)
    o_cmp8 = _dot(p_cmp.astype(BF16), vc.astype(BF16))
    subc = _iota((SUBLANES, n_chunk), 0)
    g0 = jnp.sum(jnp.where(subc < R_NSA, p_cmp, 0.0), axis=0, keepdims=True)
    g1 = jnp.sum(jnp.where(subc >= R_NSA, p_cmp, 0.0), axis=0, keepdims=True)
    p_grp = jnp.concatenate([jnp.where(subc < R_NSA, g0, g1), jnp.zeros((LANES - SUBLANES, n_chunk), F32)], axis=0)
    own = past // SEL_BLOCK
    n_sel = -(-(own + 1) // SUBLANES) * SUBLANES
    p_hi, p_lo = _split_bf16(p_grp)
    imp_t = (_dot_nt(cover_ref[...], p_hi) + _dot_nt(cover_ref[...], p_lo))[0:n_sel]
    blk_t = _iota((n_sel, LANES), 0)
    forced = (blk_t == 0) | (blk_t == own) | (blk_t == own - 1)
    score = jnp.where(blk_t <= own, jnp.where(forced, jnp.inf, imp_t), -jnp.inf)
    picked = _top_k_sublanes(score, SEL_TOPN)
    bias_t = jnp.concatenate([jnp.where((picked > 0.5) & (blk_t <= own), 0.0, NEG_BIG),
                              jnp.full((LANES - n_sel, LANES), NEG_BIG, F32)], axis=0)
    bias8 = bias_t.T[0:SUBLANES, :]
    cmp_rows = []
    for hp in range(H_NSA // 2):
        g = (2 * hp) // R_NSA
        ra = o_cmp8[2 * hp:2 * hp + 1, :]
        rb = o_cmp8[2 * hp + 1:2 * hp + 2, :]
        cmp_rows.append(jnp.where(lane1 < HEAD_DIM, ra if g == 0 else pltpu.roll(ra, HEAD_DIM, 1),
                                  rb if g == 1 else pltpu.roll(rb, HEAD_DIM, 1)))

    def sel_scores(p, carry):
        rows = [jnp.sum(nbuf[slot, p, 2 * gw + (i // R_NSA) * HEAD_DIM:2 * gw + (i // R_NSA + 1) * HEAD_DIM, :]
                        * qrb[i * HEAD_DIM:(i + 1) * HEAD_DIM, :], axis=0, keepdims=True) for i in range(H_NSA)]
        s_sc[p] = jnp.concatenate(rows, axis=0)
        return carry
    for p in range(n_pages):
        sel_scores(p, 0)
    bpp = page // SEL_BLOCK
    sel_s = []
    for p in range(n_pages):
        bias_p = bias8[:, p * bpp:p * bpp + 1]
        for t in range(1, bpp):
            bias_p = jnp.where(lane8 < t * SEL_BLOCK, bias_p, bias8[:, p * bpp + t:p * bpp + t + 1])
        sel_s.append(s_sc[p] + bias_p)
    s_new = _head_sums(qrow[1:2, :] * new[2:3, :]) * SCALE
    sel_rows = softmax_pv(sel_s, s_new, new[3:4, :], nbuf, 3 * gw, True)

    nw = wb_ref.shape[2]
    wk = nw // LANES
    w_s = []
    for i in range(H_NSA):
        g = i // R_NSA
        qcol = jnp.concatenate([qrb[i * HEAD_DIM:(i + 1) * HEAD_DIM, :]] * wk, axis=1)
        w_s.append(jnp.sum(wb_ref[0, g * HEAD_DIM:(g + 1) * HEAD_DIM, :] * qcol, axis=0, keepdims=True))
    s = jnp.concatenate(w_s, axis=0)
    wmask = _iota((SUBLANES, nw), 1) >= 1
    s = jnp.where(wmask, s, NEG_BIG)
    s_new = _head_sums(qrow[1:2, :] * new[4:5, :]) * SCALE
    m8 = jnp.maximum(_lane_rep(jnp.max(s, axis=1, keepdims=True)), s_new)
    pw = jnp.where(wmask, jnp.exp(s - m8[:, 0:1]), 0.0)
    w_new = jnp.exp(s_new - m8)
    inv8 = 1.0 / jnp.maximum(_lane_rep(jnp.sum(pw, axis=1, keepdims=True)) + w_new, TINY)
    win_rows = []
    for hp in range(H_NSA // 2):
        accs = []
        for i in (2 * hp, 2 * hp + 1):
            g = i // R_NSA
            prod = wb_ref[0, gw + g * HEAD_DIM:gw + (g + 1) * HEAD_DIM, :] * pw[i:i + 1, :]
            acc = prod[:, 0:LANES]
            for c in range(1, wk):
                acc = acc + prod[:, c * LANES:(c + 1) * LANES]
            accs.append(acc)
        cols = slice(hp * LANES, (hp + 1) * LANES)
        win_rows.append((_cols_to_row(accs[0], accs[1]) + _pair_row(w_new, 2 * hp) * new[5:6, cols])
                        * _pair_row(inv8, 2 * hp))

    gates = gate_ref[0]
    for c in range(hw // LANES):
        o_ref[0, :, c * LANES:(c + 1) * LANES] = o_rows[c]
    for c in range(W_QN // LANES):
        cols = slice(c * LANES, (c + 1) * LANES)
        o_ref[0, :, hw + c * LANES:hw + (c + 1) * LANES] = (
            gates[0:1, cols] * cmp_rows[c] + gates[1:2, cols] * sel_rows[c] + gates[2:3, cols] * win_rows[c])


def _dec_attention(cm, cn, wb, pt_flat, qrows, q8n, new, gate_rows, wcat, pecat, w2bd, gk, bd, cover,
                    *, n_req, n_pages):
    page = cm.shape[2]
    n_chunk = n_pages * page // CMP_STRIDE
    req = lambda a: pl.BlockSpec((1,) + a.shape[1:], lambda b, pt: (b,) + (0,) * (a.ndim - 1))
    const = lambda a: pl.BlockSpec(a.shape, lambda b, pt: (0,) * a.ndim)
    return pl.pallas_call(
        functools.partial(_dec_kernel, n_pages=n_pages, page=page),
        grid_spec=pltpu.PrefetchScalarGridSpec(
            num_scalar_prefetch=1,
            grid=(n_req,),
            in_specs=[pl.BlockSpec(memory_space=pl.ANY), pl.BlockSpec(memory_space=pl.ANY), req(wb), req(qrows),
                      req(q8n), req(new), req(gate_rows), const(wcat), const(pecat), const(w2bd), const(gk),
                      const(bd), const(cover)],
            out_specs=pl.BlockSpec((1, 1, W_QM + W_QN), lambda b, pt: (b, 0, 0)),
            scratch_shapes=[pltpu.VMEM((2, n_pages) + cm.shape[1:], F32), pltpu.VMEM((2, n_pages) + cn.shape[1:], F32),
                            pltpu.SemaphoreType.DMA((2, 2)),
                            pltpu.VMEM((n_pages * page, LANES), F32), pltpu.VMEM((n_pages * page, LANES), F32),
                            pltpu.VMEM((W_QM, LANES), F32), pltpu.VMEM((W_QN, LANES), F32),
                            pltpu.VMEM((n_pages, SUBLANES, LANES), F32), pltpu.VMEM((n_pages, SUBLANES, LANES), F32)]),
        out_shape=jax.ShapeDtypeStruct((n_req, 1, W_QM + W_QN), F32),
        compiler_params=_params("arbitrary"),
        name="dec_attention",
    )(pt_flat, cm, cn, wb, qrows, q8n, new, gate_rows, wcat, pecat, w2bd, gk, bd, cover)


def _compress_cat_consts(cmp_pos, cmp_w1):
    eye = jnp.eye(G_NSA, dtype=F32)
    w1 = cmp_w1.reshape(2, 2, CMP_STRIDE, HEAD_DIM, HEAD_DIM)
    wcat = jnp.einsum("gh,kajde->kjgdahe", eye, w1).reshape(2, CMP_STRIDE * G_NSA * HEAD_DIM, 2 * G_NSA * HEAD_DIM)
    pe = jnp.tile(cmp_pos.reshape(2, 2, CMP_STRIDE, 1, HEAD_DIM), (1, 1, 1, G_NSA, 1))
    pecat = jnp.pad(pe.reshape(2, 2, CMP_STRIDE * G_NSA * HEAD_DIM), ((0, 0), (0, SUBLANES - 2), (0, 0)))
    return wcat.astype(BF16), pecat


def _post_kernel(o_ref, x_ref, gt_ref, sh_ref, sc_ref, g_ref, wo_ref, wr_ref, br_ref,
                 y_ref, h3_ref, te_ref, tw_ref):
    y = x_ref[...] + gt_ref[0] * _dot(o_ref[...], wo_ref[...])
    y_ref[...] = y
    h = y * lax.rsqrt(jnp.mean(y * y, axis=1, keepdims=True) + NORM_EPS) * g_ref[...]
    h = h * (1.0 + sc_ref[0]) + sh_ref[0]
    tm = h.shape[0]
    for s in range(h.shape[1] // LANES):
        h3_ref[pl.ds(s, tm, stride=SUBLANES), :] = h[:, s * LANES:(s + 1) * LANES]
    h_hi, h_lo = _split_bf16(h)
    w_hi, w_lo = _split_bf16(wr_ref[...])
    logits = _dot(h_hi, w_hi) + _dot(h_lo, w_hi) + _dot(h_hi, w_lo) + br_ref[...]
    _, vals, ids = _top_k_lanes(logits, TOP_K)
    lane = _iota((tm, LANES), 1)
    es = [jnp.exp(v - vals[0]) for v in vals]
    den = es[0]
    for e in es[1:]:
        den = den + e
    te = jnp.zeros((tm, LANES), F32)
    tw = jnp.zeros((tm, LANES), F32)
    for k in range(TOP_K):
        te = jnp.where(lane == k, ids[k], te)
        tw = jnp.where(lane == k, es[k] / den, tw)
    te_ref[...] = te.astype(jnp.int32)
    tw_ref[...] = tw


def _post(o, x, gate, shift, scale, g2, wo_bf, wr_pad, br_pad, *, tm, rows_per_mod):
    n, d = x.shape
    mod_r = gate.shape[1]
    mod_map = lambda i: (i // (rows_per_mod // tm), 0, 0)
    row = lambda w: pl.BlockSpec((tm, w), lambda i: (i, 0))
    const = lambda a: pl.BlockSpec(a.shape, lambda i: (0,) * a.ndim)
    mod = pl.BlockSpec((1, mod_r, d), mod_map)
    return pl.pallas_call(
        _post_kernel,
        grid=(n // tm,),
        in_specs=[row(d), row(d), mod, mod, mod, const(g2), const(wo_bf), const(wr_pad), const(br_pad)],
        out_specs=[row(d), pl.BlockSpec((tm * SUBLANES, LANES), lambda i: (i, 0)), row(LANES), row(LANES)],
        out_shape=[jax.ShapeDtypeStruct((n, d), F32), jax.ShapeDtypeStruct((n * SUBLANES, LANES), F32),
                   jax.ShapeDtypeStruct((n, LANES), jnp.int32), jax.ShapeDtypeStruct((n, LANES), F32)],
        compiler_params=_params("parallel"),
        name="post",
    )(o, x, gate, shift, scale, g2, wo_bf, wr_pad, br_pad)


def _dispatch_kernel(pos_ref, h3_ref, xs_in_ref, xs_ref, sem, *, tm, tile_off):
    del xs_in_ref
    i = pl.program_id(0)

    def body(r, carry):
        src = h3_ref.at[pl.ds(pl.multiple_of(r * SUBLANES, SUBLANES), SUBLANES), :]
        for k in range(TOP_K):
            p = pos_ref[((i + tile_off) * tm + r) * TOP_K + k]
            pltpu.make_async_copy(src, xs_ref.at[pl.ds(pl.multiple_of(p * SUBLANES, SUBLANES), SUBLANES), :],
                                  sem.at[0]).start(priority=k % 2)
        return carry
    lax.fori_loop(0, tm, body, 0)
    for k in range(TOP_K):
        pltpu.make_async_copy(h3_ref, xs_ref.at[pl.ds(0, tm * SUBLANES), :], sem.at[0]).wait()


def _dispatch(pos_flat, h3, xs, *, tm, tile_off):
    n8 = h3.shape[0]
    return pl.pallas_call(
        functools.partial(_dispatch_kernel, tm=tm, tile_off=tile_off),
        grid_spec=pltpu.PrefetchScalarGridSpec(
            num_scalar_prefetch=1,
            grid=(n8 // (tm * SUBLANES),),
            in_specs=[pl.BlockSpec((tm * SUBLANES, LANES), lambda i, pos: (i, 0)), pl.BlockSpec(memory_space=pl.ANY)],
            out_specs=pl.BlockSpec(memory_space=pl.ANY),
            scratch_shapes=[pltpu.SemaphoreType.DMA((1,))]),
        out_shape=jax.ShapeDtypeStruct(xs.shape, xs.dtype),
        input_output_aliases={2: 0},
        compiler_params=_params("arbitrary"),
        name="dispatch",
    )(pos_flat, h3, xs)


def _expert_block_kernel(be_ref, na_ref, x_ref, wgu_ref, bgu_ref, wd_ref, bd_ref, y_ref, xb, wgu_bf, wd_bf):
    i = pl.program_id(0)
    active = i < na_ref[0]

    @pl.when(active & ((i == 0) | (be_ref[i] != be_ref[jnp.maximum(i - 1, 0)])))
    def _():
        for r in range(0, wgu_bf.shape[0], LANES):
            wgu_bf[r:r + LANES, :] = wgu_ref[0, r:r + LANES, :].astype(BF16)
        for r in range(0, wd_bf.shape[0], LANES):
            wd_bf[r:r + LANES, :] = wd_ref[0, r:r + LANES, :].astype(BF16)

    @pl.when(active)
    def _():
        d = xb.shape[1]
        for s in range(d // LANES):
            xb[:, s * LANES:(s + 1) * LANES] = x_ref[pl.ds(s, MOE_ROWS, stride=SUBLANES), :].astype(BF16)
        gu = _dot(xb[...], wgu_bf[...]) + bgu_ref[0]
        f = gu.shape[1] // 2
        gt = jnp.minimum(gu[:, 0:f], SWIGLU_LIMIT)
        up = jnp.clip(gu[:, f:2 * f], -SWIGLU_LIMIT, SWIGLU_LIMIT)
        act = (up + 1.0) * (gt * jax.nn.sigmoid(SWIGLU_ALPHA * gt))
        y = _dot(act.astype(BF16), wd_bf[...]) + bd_ref[0]
        for s in range(d // LANES):
            y_ref[pl.ds(s, MOE_ROWS, stride=SUBLANES), :] = y[:, s * LANES:(s + 1) * LANES]

    @pl.when(i >= na_ref[0])
    def _():
        y_ref[...] = jnp.zeros(y_ref.shape, F32)


def _expert_blocks(blk_e, n_active, xs, w_gu, b_gu, w_down, b_down):
    n_blocks = blk_e.shape[0]
    e, d, f2 = w_gu.shape
    rows8 = MOE_ROWS * SUBLANES
    row_map = lambda i, be, na: (jnp.minimum(i, na[0] - 1), 0)
    return pl.pallas_call(
        _expert_block_kernel,
        grid_spec=pltpu.PrefetchScalarGridSpec(
            num_scalar_prefetch=2,
            grid=(n_blocks,),
            in_specs=[pl.BlockSpec((rows8, LANES), row_map),
                      pl.BlockSpec((1, d, f2), lambda i, be, na: (be[i], 0, 0)),
                      pl.BlockSpec((1, 1, f2), lambda i, be, na: (be[i], 0, 0)),
                      pl.BlockSpec((1, f2 // 2, d), lambda i, be, na: (be[i], 0, 0)),
                      pl.BlockSpec((1, 1, d), lambda i, be, na: (be[i], 0, 0))],
            out_specs=pl.BlockSpec((rows8, LANES), lambda i, be, na: (i, 0)),
            scratch_shapes=[pltpu.VMEM((MOE_ROWS, d), BF16), pltpu.VMEM((d, f2), BF16),
                            pltpu.VMEM((f2 // 2, d), BF16)]),
        out_shape=jax.ShapeDtypeStruct((n_blocks * rows8, LANES), F32),
        compiler_params=_params("arbitrary"),
        name="experts",
    )(blk_e, n_active, xs, w_gu, b_gu.reshape(e, 1, f2), w_down, b_down.reshape(e, 1, d))


def _routing_pos(top_e, n_tok):
    n_assign = n_tok * TOP_K
    n_pad = -(-n_assign // LANES) * LANES
    e_flat = jnp.pad(top_e.reshape(-1), (0, n_pad - n_assign), constant_values=N_EXPERTS)
    onehot = (e_flat[:, None] == jnp.arange(N_EXPERTS)[None, :]).astype(F32).reshape(n_pad // LANES, LANES, N_EXPERTS)
    tril = jnp.tril(jnp.ones((LANES, LANES), F32))
    within = jnp.einsum("ij,bjk->bik", tril, onehot)
    block_tot = within[:, -1, :]
    offs = jnp.cumsum(block_tot, axis=0) - block_tot
    counts = jnp.sum(block_tot, axis=0).astype(jnp.int32)
    padded = (counts + MOE_ROWS - 1) // MOE_ROWS * MOE_ROWS
    pad_end = jnp.cumsum(padded)
    pad_start = (pad_end - padded).astype(F32)
    slot = jnp.sum(onehot * (within + offs[:, None, :] - 1.0 + pad_start[None, None, :]), axis=-1)
    pos = slot.reshape(-1)[:n_assign].astype(jnp.int32)
    n_blocks = -(-n_assign // MOE_ROWS) + N_EXPERTS
    starts = jnp.arange(n_blocks, dtype=jnp.int32) * MOE_ROWS
    blk_e = jnp.minimum(jnp.sum((pad_end[None, :] <= starts[:, None]).astype(jnp.int32), axis=1), N_EXPERTS - 1)
    n_active = (pad_end[-1] // MOE_ROWS).astype(jnp.int32).reshape(1)
    return blk_e, n_active, pos


def _combine_kernel(pos_ref, ys_ref, y1_ref, gt_ref, tw_ref, o_ref, buf, sem, *, tile_off):
    i = pl.program_id(0)
    n = pl.num_programs(0)
    tm = y1_ref.shape[0]
    rows8 = tm * SUBLANES

    def gather(tile, slot):
        def body(r, carry):
            for k in range(TOP_K):
                p = pos_ref[((tile + tile_off) * tm + r) * TOP_K + k]
                pltpu.make_async_copy(ys_ref.at[pl.ds(pl.multiple_of(p * SUBLANES, SUBLANES), SUBLANES), :],
                                      buf.at[slot, k, pl.ds(pl.multiple_of(r * SUBLANES, SUBLANES), SUBLANES), :],
                                      sem.at[slot]).start(priority=k % 2)
            return carry
        lax.fori_loop(0, tm, body, 0)

    @pl.when(i == 0)
    def _():
        gather(0, 0)

    slot = i % 2
    for k in range(TOP_K):
        pltpu.make_async_copy(ys_ref.at[pl.ds(0, rows8), :], buf.at[slot, k], sem.at[slot]).wait()

    @pl.when(i + 1 < n)
    def _():
        gather(i + 1, 1 - slot)

    tw = tw_ref[...]
    wk = [jnp.broadcast_to(tw[:, k:k + 1], (tm, LANES)) for k in range(TOP_K)]
    gt = gt_ref[0]
    for s in range(o_ref.shape[1] // LANES):
        moe = wk[0] * buf[slot, 0, pl.ds(s, tm, stride=SUBLANES), :]
        for k in range(1, TOP_K):
            moe = moe + wk[k] * buf[slot, k, pl.ds(s, tm, stride=SUBLANES), :]
        cols = slice(s * LANES, (s + 1) * LANES)
        o_ref[:, cols] = y1_ref[:, cols] + gt[:, cols] * moe


def _combine(pos_flat, ys, y1, gate, tw, *, tm, rows_per_mod, tile_off):
    n, d = y1.shape
    mod_r = gate.shape[1]
    return pl.pallas_call(
        functools.partial(_combine_kernel, tile_off=tile_off),
        grid_spec=pltpu.PrefetchScalarGridSpec(
            num_scalar_prefetch=1,
            grid=(n // tm,),
            in_specs=[pl.BlockSpec(memory_space=pl.ANY),
                      pl.BlockSpec((tm, d), lambda i, pos: (i, 0)),
                      pl.BlockSpec((1, mod_r, d), lambda i, pos: (i // (rows_per_mod // tm), 0, 0)),
                      pl.BlockSpec((tm, LANES), lambda i, pos: (i, 0))],
            out_specs=pl.BlockSpec((tm, d), lambda i, pos: (i, 0)),
            scratch_shapes=[pltpu.VMEM((2, TOP_K, tm * SUBLANES, LANES), F32), pltpu.SemaphoreType.DMA((2,))]),
        out_shape=jax.ShapeDtypeStruct((n, d), F32),
        compiler_params=_params("arbitrary"),
        name="combine",
    )(pos_flat, ys, y1, gate, tw)


def _rope_tables(pos):
    half = HEAD_DIM // 2
    inv = ROPE_THETA ** (-jnp.arange(half, dtype=F32) / half)
    ang = pos.astype(F32)[:, None] * inv[None, :]
    cos = jnp.cos(ang)
    sin = jnp.sin(ang)
    reps = LANES // HEAD_DIM
    return (jnp.tile(jnp.concatenate([cos, cos], axis=1), (1, reps)),
            jnp.tile(jnp.concatenate([-sin, sin], axis=1), (1, reps)))


def _head_pad(a):
    return jnp.concatenate([a, jnp.zeros_like(a)], axis=-1)


def kernel(x_prompt, x_sample, c_prompt, c_sample, cache_moba_kv, cache_nsa_kv, state_nsa_win_kv, page_table,
           norm_g, w_ada, b_ada, w_in, qk_gain, cmp_pos, cmp_w1, cmp_w2, w_out, w_router, b_router, w_gu, b_gu,
           w_down, b_down):
    bsz, seq, d = x_prompt.shape
    n_req = x_sample.shape[0]
    depth = norm_g.shape[0]
    assert depth == 1 and x_sample.shape[1] == 1
    assert seq % ATT_TK == 0 and seq >= WINDOW + Q_TILE and seq // SEL_BLOCK <= LANES and seq // MOBA_BLOCK <= MOBA_BLOCK // SUBLANES
    n_pool, page = cache_moba_kv.shape[1], cache_moba_kv.shape[2]
    n_pages = page_table.shape[1]
    past = n_pages * page
    assert past % MOBA_BLOCK == 0 and 2 * page == MOBA_BLOCK and past // SEL_BLOCK < LANES
    assert state_nsa_win_kv.shape[2] == WINDOW
    layer = 0
    gw = G_NSA * HEAD_DIM
    n_prompt = bsz * seq

    bd = jnp.asarray(np.kron(np.eye(LANES // HEAD_DIM), np.full((HEAD_DIM, HEAD_DIM), 1.0 / HEAD_DIM)), BF16)
    w_in_bf = jnp.pad(w_in[layer], ((0, 0), (0, IN_COLS_PAD - IN_COLS))).astype(BF16)
    gains = jnp.tile(qk_gain[layer], (1, W_QM // HEAD_DIM))
    g1 = norm_g[layer, 0].reshape(1, d)
    g2 = norm_g[layer, 1].reshape(1, d)
    wo_bf = w_out[layer].astype(BF16)
    wr_pad = jnp.pad(w_router[layer], ((0, 0), (0, LANES - N_EXPERTS)))
    br_pad = jnp.pad(b_router[layer].reshape(1, N_EXPERTS), ((0, 0), (0, LANES - N_EXPERTS)),
                     constant_values=-jnp.inf)
    cmp_consts = _compress_consts(cmp_pos[layer], cmp_w1[layer], cmp_w2[layer], qk_gain[layer, 3])

    n_c = bsz + n_req
    n_c_pad = -(-n_c // SUBLANES) * SUBLANES
    c_all = jnp.pad(jnp.concatenate([c_prompt, c_sample], axis=0), ((0, n_c_pad - n_c), (0, 0)))
    mods = _ada(c_all, w_ada[layer], b_ada[layer])
    mods_p = [m.reshape(bsz, 1, d) for m in jnp.split(mods[:bsz], 6, axis=1)]
    mods_s = [m.reshape(1, n_req, d) for m in jnp.split(mods[bsz:n_c], 6, axis=1)]

    cos_p, sin_p = _rope_tables(jnp.arange(seq, dtype=jnp.int32))
    tm_p = 256
    (moba_t, nsa_t, win_t, q_m, qn, qr, _, kmean, cmp_raw, k_aug, v_aug, ks_aug, vs_aug, kw_pad, vw_aug,
     gates_g) = _proj(
        x_prompt.reshape(n_prompt, d), mods_p[0], mods_p[1], g1, w_in_bf, gains, bd, cos_p, sin_p,
        tm=tm_p, rows_per_mod=seq, pos_blocks=seq // tm_p, with_kmean=True)

    nbk = seq // MOBA_BLOCK
    kmean_h = kmean.reshape(bsz, nbk, H_MOBA, HEAD_DIM).transpose(0, 2, 1, 3)
    kmp = jnp.zeros((bsz, H_MOBA, LANES, LANES), F32).at[:, :, HEAD_DIM:HEAD_DIM + nbk, :HEAD_DIM].set(kmean_h)
    per_b = lambda a: a.reshape(bsz, seq, a.shape[-1])
    o_m = _moba_attention(per_b(q_m), kmp, per_b(k_aug), per_b(v_aug))

    kc, vc = _compress_prompt(per_b(cmp_raw), cmp_consts, bd)
    n_chunk = seq // CMP_STRIDE
    per_g = lambda a: a.reshape(bsz, -1, G_NSA, HEAD_DIM).transpose(0, 2, 1, 3)
    kcp = _head_pad(per_g(kc))
    vcd = _head_pad(per_g(vc)).astype(BF16)
    cover_p = _cover(n_chunk - 1, seq // SEL_BLOCK, n_chunk).T.astype(BF16)
    o_n = _nsa_attention(per_b(qn), per_b(qr), per_b(gates_g), kcp, vcd, cover_p, per_b(ks_aug), per_b(vs_aug),
                         per_b(kw_pad), per_b(vw_aug))
    o_p = jnp.concatenate([o_m, o_n], axis=-1).reshape(n_prompt, d)

    y1_p, h3_p, te_p, tw_p = _post(o_p, x_prompt.reshape(n_prompt, d), mods_p[2], mods_p[3], mods_p[4], g2, wo_bf,
                                   wr_pad, br_pad, tm=256, rows_per_mod=seq)

    cos_s, sin_s = _rope_tables(jnp.full((n_req,), past, jnp.int32))
    moba_new, nsa_new, win_new, q_m_s, qn_s, qr_s, gates_s = _proj(
        x_sample.reshape(n_req, d), mods_s[0], mods_s[1], g1, w_in_bf, gains, bd, cos_s, sin_s,
        tm=n_req, rows_per_mod=n_req, pos_blocks=1, with_kmean=False)
    pt_flat = page_table.reshape(-1).astype(jnp.int32)
    cache_m = cache_moba_kv[layer].transpose(0, 2, 3, 4, 1).reshape(n_pool, W_KVM, page)
    cache_n = cache_nsa_kv[layer].transpose(0, 2, 3, 4, 1).reshape(n_pool, 4 * gw, page)
    win_buf = state_nsa_win_kv[layer].transpose(0, 2, 3, 4, 1).reshape(n_req, 2 * gw, WINDOW)

    def rows8(q):
        qh = q.reshape(n_req, G_NSA, R_NSA, 1, HEAD_DIM)
        place = jnp.arange(G_NSA)[None, :, None, None, None] == jnp.arange(G_NSA)[None, None, None, :, None]
        return jnp.where(place, qh, 0.0).reshape(n_req, H_NSA, gw)

    def per_head(a):
        return jnp.repeat(a.reshape(n_req, G_NSA, HEAD_DIM), R_NSA, axis=1).reshape(n_req, W_QN)

    new_rows = jnp.stack([moba_new[:, :W_QM], moba_new[:, W_QM:], per_head(nsa_new[:, 2 * gw:3 * gw]),
                          per_head(nsa_new[:, 3 * gw:]), per_head(win_new[:, :gw]), per_head(win_new[:, gw:])], axis=1)
    gate_rows = jnp.repeat(gates_s[:, :N_GATE].reshape(n_req, H_NSA, 3).transpose(0, 2, 1), HEAD_DIM, axis=2)
    n_cmp_s = past // CMP_STRIDE
    cover_s = _cover(n_cmp_s - 1, past // SEL_BLOCK + 1, n_cmp_s).T.astype(BF16)
    wcat, pecat = _compress_cat_consts(cmp_pos[layer], cmp_w1[layer])
    o_s = _dec_attention(cache_m, cache_n, win_buf, pt_flat, jnp.stack([q_m_s, qr_s], axis=1), rows8(qn_s), new_rows,
                          gate_rows, wcat, pecat, cmp_consts[2], cmp_consts[3], bd, cover_s,
                          n_req=n_req, n_pages=n_pages).reshape(n_req, d).astype(BF16)
    y1_s, h3_s, te_s, tw_s = _post(o_s, x_sample.reshape(n_req, d), mods_s[2], mods_s[3], mods_s[4], g2, wo_bf,
                                   wr_pad, br_pad, tm=n_req, rows_per_mod=n_req)

    n_tok = n_prompt + n_req
    top_e = jnp.concatenate([te_p[:, :TOP_K], te_s[:, :TOP_K]], axis=0)
    blk_e, n_active, pos_flat = _routing_pos(top_e, n_tok)
    tm_c = 128
    xs = jnp.zeros((blk_e.shape[0] * MOE_ROWS * SUBLANES, LANES), F32)
    xs = _dispatch(pos_flat, h3_p, xs, tm=2 * tm_c, tile_off=0)
    xs = _dispatch(pos_flat, h3_s, xs, tm=n_req, tile_off=n_prompt // n_req)
    ys = _expert_blocks(blk_e, n_active, xs, w_gu[layer], b_gu[layer], w_down[layer], b_down[layer])
    y_p = _combine(pos_flat, ys, y1_p, mods_p[5], tw_p, tm=tm_c, rows_per_mod=seq, tile_off=0)
    y_s = _combine(pos_flat, ys, y1_s, mods_s[5], tw_s, tm=n_req, rows_per_mod=n_req, tile_off=n_prompt // n_req)

    def rows_view(t, n_slot, n_head):
        return t.reshape(1, bsz, n_slot, n_head, HEAD_DIM, t.shape[-1]).transpose(0, 1, 5, 2, 3, 4)

    keep = min(WINDOW, seq)
    win_s = jnp.concatenate([state_nsa_win_kv[layer][:, 1:], win_new.reshape(n_req, 1, 2, G_NSA, HEAD_DIM)], axis=1)
    return (y_p.reshape(bsz, seq, d), y_s.reshape(n_req, 1, d),
            rows_view(moba_t, 2, H_MOBA), rows_view(nsa_t, 4, G_NSA), rows_view(win_t[:, :, seq - keep:], 2, G_NSA),
            moba_new.reshape(1, n_req, 1, 2, H_MOBA, HEAD_DIM),
            nsa_new.reshape(1, n_req, 1, 4, G_NSA, HEAD_DIM), win_s[None])
```

```python
import functools

import numpy as np
import jax
import jax.numpy as jnp
from jax import lax
from jax.experimental import pallas as pl
from jax.experimental.pallas import tpu as pltpu

F32 = jnp.float32
BF16 = jnp.bfloat16
HIGHEST = lax.Precision.HIGHEST

LANES = 128
SUBLANES = 8
HEAD_DIM = 64
H_MOBA = 8
H_NSA = 8
G_NSA = 2
R_NSA = H_NSA // G_NSA
MOBA_BLOCK = 256
MOBA_TOPK = 3
CMP_LEN = 32
CMP_STRIDE = 16
SEL_BLOCK = 64
SEL_TOPN = 16
N_FORCED = 3
WINDOW = 512
N_EXPERTS = 32
TOP_K = 4
SWIGLU_LIMIT = 7.0
SWIGLU_ALPHA = 1.702
ROPE_THETA = 10000.0
NORM_EPS = 1e-6
NEG_BIG = -1e30
TINY = 1e-30
SCALE = HEAD_DIM ** -0.5
SCALE_LOG2E = SCALE * 1.4426950408889634
Q_TILE = 256
MOBA_Q = 1024
MOBA_HEADS = 2
ATT_TK = 1024
MOE_ROWS = 256
COPY_UNROLL = 8
VMEM_LIMIT = 56 * 1024 * 1024

W_QM = H_MOBA * HEAD_DIM
W_KVM = 2 * H_MOBA * HEAD_DIM
W_QN = H_NSA * HEAD_DIM
W_KVN = 6 * G_NSA * HEAD_DIM
N_GATE = 3 * H_NSA
IN_COLS = W_QM + W_KVM + W_QN + W_KVN + N_GATE
IN_COLS_PAD = W_QM + W_KVM + W_QN + W_KVN + LANES


def _iota(shape, dim):
    return lax.broadcasted_iota(jnp.int32, shape, dim)


def _dot(a, b, precision=None):
    return jnp.dot(a, b, preferred_element_type=F32, precision=precision)


def _dot_nt(a, b, precision=None):
    return lax.dot_general(a, b, (((1,), (1,)), ((), ())), preferred_element_type=F32, precision=precision)


def _split_bf16(a):
    hi = a.astype(BF16)
    return hi, (a - hi.astype(F32)).astype(BF16)


def _dot_nt_x3(a, b):
    ah, al = _split_bf16(a)
    bh, bl = _split_bf16(b)
    return _dot_nt(ah, bh) + _dot_nt(al, bh) + _dot_nt(ah, bl)


def _dot_x3(a, b):
    ah, al = _split_bf16(a)
    bh, bl = _split_bf16(b)
    return _dot(ah, bh) + _dot(al, bh) + _dot(ah, bl)


def _params(*sem):
    return pltpu.CompilerParams(dimension_semantics=sem, vmem_limit_bytes=VMEM_LIMIT)


def _seg_meansq(z, bd):
    zz = z * z
    hi = zz.astype(BF16)
    lo = (zz - hi.astype(F32)).astype(BF16)
    outs = []
    for c in range(z.shape[1] // LANES):
        sl = slice(c * LANES, (c + 1) * LANES)
        outs.append(_dot(hi[:, sl], bd) + _dot(lo[:, sl], bd))
    return outs[0] if len(outs) == 1 else jnp.concatenate(outs, axis=1)


def _head_norm(z, gain, bd):
    return z * lax.rsqrt(_seg_meansq(z, bd) + NORM_EPS) * gain


def _rope(z, cos, sin):
    outs = []
    first = (_iota((z.shape[0], LANES), 1) % HEAD_DIM) < (HEAD_DIM // 2)
    for c in range(z.shape[1] // LANES):
        x = z[:, c * LANES:(c + 1) * LANES]
        swapped = jnp.where(first, pltpu.roll(x, LANES - HEAD_DIM // 2, 1), pltpu.roll(x, HEAD_DIM // 2, 1))
        outs.append(x * cos + swapped * sin)
    return outs[0] if len(outs) == 1 else jnp.concatenate(outs, axis=1)


def _top_k_lanes(cur, k):
    lane = _iota(cur.shape, 1).astype(F32)
    picked = jnp.zeros(cur.shape, F32)
    vals, ids = [], []
    for _ in range(k):
        mx = jnp.max(cur, axis=1, keepdims=True)
        first = jnp.min(jnp.where(cur == mx, lane, 1e9), axis=1, keepdims=True)
        hit = lane == first
        picked = jnp.where(hit, 1.0, picked)
        cur = jnp.where(hit, -jnp.inf, cur)
        vals.append(mx)
        ids.append(first)
    return picked, vals, ids


def _top_k_sublanes(cur, k):
    idx = _iota(cur.shape, 0).astype(F32)
    picked = jnp.zeros(cur.shape, F32)
    for _ in range(k):
        mx = jnp.max(cur, axis=0, keepdims=True)
        first = jnp.min(jnp.where(cur == mx, idx, 1e9), axis=0, keepdims=True)
        hit = idx == first
        picked = jnp.where(hit, 1.0, picked)
        cur = jnp.where(hit, -jnp.inf, cur)
    return picked


def _flash_step(q, k, v, mask, m, acc):
    s = _dot_nt(q, k)
    if mask is not None:
        s = jnp.where(mask, s, NEG_BIG)
    m_new = jnp.maximum(m, jnp.max(s, axis=1, keepdims=True))
    acc_new = jnp.exp2(m - m_new) * acc + _dot(jnp.exp2(s - m_new).astype(BF16), v)
    return m_new, acc_new


def _flash_finish(acc):
    return acc / jnp.maximum(acc[:, HEAD_DIM:HEAD_DIM + 1], TINY)


def _ada_kernel(c_ref, w_ref, b_ref, o_ref):
    c = c_ref[...]
    o_ref[...] = _dot(c * jax.nn.sigmoid(c), w_ref[...], HIGHEST) + b_ref[...]


def _ada(c_all, w_ada, b_ada):
    n, d = c_all.shape
    cols = w_ada.shape[1]
    tn = 1024
    return pl.pallas_call(
        _ada_kernel,
        grid=(cols // tn,),
        in_specs=[pl.BlockSpec((n, d), lambda j: (0, 0)),
                  pl.BlockSpec((d, tn), lambda j: (0, j)),
                  pl.BlockSpec((1, tn), lambda j: (0, j))],
        out_specs=pl.BlockSpec((n, tn), lambda j: (0, j)),
        out_shape=jax.ShapeDtypeStruct((n, cols), F32),
        compiler_params=_params("arbitrary"),
        name="ada",
    )(c_all, w_ada, b_ada.reshape(1, cols))


def _proj_kernel(x_ref, sh_ref, sc_ref, g_ref, w_ref, gains_ref, bd_ref, cos_ref, sin_ref,
                 moba_ref, nsa_ref, win_ref, qm_ref, qn_ref, qr_ref, gate_ref, *attn_refs, with_kmean, pos_blocks):
    x = x_ref[...]
    y = x * lax.rsqrt(jnp.mean(x * x, axis=1, keepdims=True) + NORM_EPS) * g_ref[...]
    h = (y * (1.0 + sc_ref[0]) + sh_ref[0]).astype(BF16)
    bd = bd_ref[...]
    cos = cos_ref[...]
    sin = sin_ref[...]
    o = 0

    def seg(width):
        nonlocal o
        z = _dot(h, w_ref[:, o:o + width])
        o += width
        return z

    def gain(i, width):
        return gains_ref[i:i + 1, 0:width]

    qm_ref[...] = _rope(_head_norm(seg(W_QM), gain(0, W_QM), bd), cos, sin)
    k_m = _rope(_head_norm(seg(W_QM), gain(1, W_QM), bd), cos, sin)
    v_m = seg(W_QM)
    qn = _head_norm(seg(W_QN), gain(2, W_QN), bd)
    qn_ref[...] = qn
    qr_ref[...] = _rope(qn, cos, sin)
    gw = G_NSA * HEAD_DIM
    cmp_raw = seg(2 * gw)
    k_sel = _rope(_head_norm(seg(gw), gain(4, gw), bd), cos, sin)
    v_sel = seg(gw)
    k_win = _rope(_head_norm(seg(gw), gain(5, gw), bd), cos, sin)
    v_win = seg(gw)
    gates = jax.nn.sigmoid(seg(LANES))
    gate_ref[...] = gates
    moba_rows = jnp.concatenate([k_m, v_m], axis=1)
    nsa_rows = jnp.concatenate([cmp_raw, k_sel, v_sel], axis=1)
    win_rows = jnp.concatenate([k_win, v_win], axis=1)
    if not with_kmean:
        moba_ref[...] = moba_rows
        nsa_ref[...] = nsa_rows
        win_ref[...] = win_rows
    else:
        kmean_ref, cmpraw_ref, kaug_ref, vaug_ref, ksaug_ref, vsaug_ref, kwp_ref, vwaug_ref, gg_ref = attn_refs
        moba_ref[0] = moba_rows.T
        nsa_ref[0] = nsa_rows.T
        win_ref[0] = win_rows.T
        cmpraw_ref[...] = cmp_raw
        tm = k_m.shape[0]
        kmean_ref[0] = jnp.mean(k_m.reshape(tm // MOBA_BLOCK, MOBA_BLOCK, W_QM), axis=1)
        lane = _iota((tm, LANES), 1)
        posv = (pl.program_id(0) % pos_blocks) * tm + _iota((tm, LANES), 0)
        lo = lane < HEAD_DIM
        pad_blk = jnp.where((lane >= HEAD_DIM) & (lane - HEAD_DIM == posv // MOBA_BLOCK), 1.0, 0.0)
        pad_one = jnp.where(lane == HEAD_DIM, 1.0, 0.0)
        oh_sel = jnp.where(lane == posv // SEL_BLOCK, 1.0, 0.0).astype(BF16)

        def halves(x):
            return x, pltpu.roll(x, HEAD_DIM, 1)

        for c in range(W_QM // LANES):
            cols = slice(c * LANES, (c + 1) * LANES)
            for hh, (kh, vh) in enumerate(zip(halves(k_m[:, cols]), halves(v_m[:, cols]))):
                hcols = slice((2 * c + hh) * LANES, (2 * c + hh + 1) * LANES)
                kaug_ref[:, hcols] = jnp.where(lo, kh, pad_blk).astype(BF16)
                vaug_ref[:, hcols] = jnp.where(lo, vh, pad_one).astype(BF16)
        for g, (ks, vs, kw, vw) in enumerate(zip(halves(k_sel), halves(v_sel), halves(k_win), halves(v_win))):
            ksaug_ref[:, 2 * g * LANES:(2 * g + 1) * LANES] = oh_sel
            ksaug_ref[:, (2 * g + 1) * LANES:(2 * g + 2) * LANES] = jnp.where(lo, ks, 0.0).astype(BF16)
            gcols = slice(g * LANES, (g + 1) * LANES)
            vsaug_ref[:, gcols] = jnp.where(lo, vs, pad_one).astype(BF16)
            kwp_ref[:, gcols] = jnp.where(lo, kw, 0.0).astype(BF16)
            vwaug_ref[:, gcols] = jnp.where(lo, vw, pad_one).astype(BF16)
            gg_ref[:, gcols] = gates if g == 0 else pltpu.roll(gates, LANES - g * 3 * R_NSA, 1)


def _proj(x, shift, scale, g, w_in_bf, gains, bd, cos, sin, *, tm, rows_per_mod, pos_blocks, with_kmean):
    n, d = x.shape
    nt = n // tm
    mod_r = shift.shape[1]
    mod_map = lambda i: (i // (rows_per_mod // tm), 0, 0)
    pos_map = lambda i: (i % pos_blocks, 0)
    row = lambda w: pl.BlockSpec((tm, w), lambda i: (i, 0))
    const = lambda a: pl.BlockSpec(a.shape, lambda i: (0,) * a.ndim)
    gw = G_NSA * HEAD_DIM
    cache_widths = (W_KVM, 4 * gw, 2 * gw)
    if with_kmean:
        nb = n // rows_per_mod
        tpb = rows_per_mod // tm
        out_shapes = [jax.ShapeDtypeStruct((nb, w, rows_per_mod), F32) for w in cache_widths]
        out_specs = [pl.BlockSpec((1, w, tm), lambda i: (i // tpb, 0, i % tpb)) for w in cache_widths]
    else:
        out_shapes = [jax.ShapeDtypeStruct((n, w), F32) for w in cache_widths]
        out_specs = [row(w) for w in cache_widths]
    out_shapes += [jax.ShapeDtypeStruct((n, W_QM), F32), jax.ShapeDtypeStruct((n, W_QN), F32),
                   jax.ShapeDtypeStruct((n, W_QN), F32), jax.ShapeDtypeStruct((n, LANES), F32)]
    out_specs += [row(W_QM), row(W_QN), row(W_QN), row(LANES)]
    if with_kmean:
        nbt = tm // MOBA_BLOCK
        out_shapes.append(jax.ShapeDtypeStruct((nt, nbt, W_QM), F32))
        out_specs.append(pl.BlockSpec((1, nbt, W_QM), lambda i: (i, 0, 0)))
        gl = G_NSA * LANES
        for width, dtype in ((2 * gw, F32), (H_MOBA * LANES, BF16), (H_MOBA * LANES, BF16), (2 * gl, BF16), (gl, BF16),
                             (gl, BF16), (gl, BF16), (gl, F32)):
            out_shapes.append(jax.ShapeDtypeStruct((n, width), dtype))
            out_specs.append(row(width))
    return pl.pallas_call(
        functools.partial(_proj_kernel, with_kmean=with_kmean, pos_blocks=pos_blocks),
        grid=(nt,),
        in_specs=[row(d), pl.BlockSpec((1, mod_r, d), mod_map), pl.BlockSpec((1, mod_r, d), mod_map),
                  const(g), const(w_in_bf), const(gains), const(bd),
                  pl.BlockSpec((tm, LANES), pos_map), pl.BlockSpec((tm, LANES), pos_map)],
        out_specs=out_specs,
        out_shape=out_shapes,
        compiler_params=_params("parallel"),
        name="proj",
    )(x, shift, scale, g, w_in_bf, gains, bd, cos, sin)


def _compress_compute(src_refs, pe_ref, w1_ref, w2_ref, gk_ref, bd_ref, kc_ref, vc_ref, n_rows):
    n_chunk = n_rows // CMP_STRIDE
    gw = G_NSA * HEAD_DIM
    for kv in range(2):
        acc_a = jnp.zeros((n_chunk, gw), F32)
        acc_b = jnp.zeros((n_chunk, gw), F32)
        for j in range(CMP_STRIDE):
            xj = src_refs[kv][pl.ds(j, n_chunk, stride=CMP_STRIDE), :]
            acc_a = acc_a + _dot(xj + pe_ref[kv, 0, j:j + 1, :], w1_ref[kv, 0, j], HIGHEST)
            acc_b = acc_b + _dot(xj + pe_ref[kv, 1, j:j + 1, :], w1_ref[kv, 1, j], HIGHEST)
        hid = jax.nn.gelu(acc_a + pltpu.roll(acc_b, n_chunk - 1, 0))
        out = _dot(hid, w2_ref[kv], HIGHEST)
        if kv == 0:
            kc_ref[0] = _head_norm(out, gk_ref[...], bd_ref[...])
        else:
            vc_ref[0] = out


def _compress_prompt_kernel(k_ref, v_ref, pe_ref, w1_ref, w2_ref, gk_ref, bd_ref, kc_ref, vc_ref, *, n_rows):
    _compress_compute((k_ref.at[0], v_ref.at[0]), pe_ref, w1_ref, w2_ref, gk_ref, bd_ref, kc_ref, vc_ref, n_rows)


def _compress_consts(cmp_pos, cmp_w1, cmp_w2, gain_k_cmp):
    pe = jnp.tile(cmp_pos.reshape(2, 2, CMP_STRIDE, HEAD_DIM), (1, 1, 1, G_NSA))
    eye = jnp.eye(G_NSA, dtype=F32)
    w1 = cmp_w1.reshape(2, 2, CMP_STRIDE, HEAD_DIM, HEAD_DIM)
    w1bd = jnp.einsum("gh,kajde->kajgdhe", eye, w1).reshape(2, 2, CMP_STRIDE, G_NSA * HEAD_DIM, G_NSA * HEAD_DIM)
    w2bd = jnp.einsum("gh,kde->kgdhe", eye, cmp_w2).reshape(2, G_NSA * HEAD_DIM, G_NSA * HEAD_DIM)
    gk = jnp.tile(gain_k_cmp.reshape(1, HEAD_DIM), (1, G_NSA))
    return pe, w1bd, w2bd, gk


def _compress_prompt(cmp_raw, consts, bd):
    b, s, _ = cmp_raw.shape
    pe, w1bd, w2bd, gk = consts
    n_chunk = s // CMP_STRIDE
    gw = G_NSA * HEAD_DIM
    const = lambda a: pl.BlockSpec(a.shape, lambda i: (0,) * a.ndim)
    out = jax.ShapeDtypeStruct((b, n_chunk, gw), F32)
    return pl.pallas_call(
        functools.partial(_compress_prompt_kernel, n_rows=s),
        grid=(b,),
        in_specs=[pl.BlockSpec((1, s, gw), lambda i: (i, 0, 0)), pl.BlockSpec((1, s, gw), lambda i: (i, 0, 1)),
                  const(pe), const(w1bd), const(w2bd), const(gk), const(bd)],
        out_specs=[pl.BlockSpec((1, n_chunk, gw), lambda i: (i, 0, 0))] * 2,
        out_shape=[out, out],
        compiler_params=_params("parallel"),
        name="compress_prompt",
    )(cmp_raw, cmp_raw, pe, w1bd, w2bd, gk, bd)


def _moba_kernel(q_ref, kmp_ref, k_ref, v_ref, o_ref):
    qi = pl.program_id(2)
    lane = _iota((MOBA_Q, LANES), 1)
    n_blk = MOBA_BLOCK // SUBLANES
    blk = _iota((n_blk, MOBA_Q), 0)
    own = (qi * MOBA_Q + _iota((n_blk, MOBA_Q), 1)) // MOBA_BLOCK
    valid = blk < own
    q_augs = []
    for h in range(MOBA_HEADS):
        q2 = q_ref[0, :, (h // 2) * LANES:(h // 2 + 1) * LANES]
        q0 = jnp.where(lane < HEAD_DIM, q2 if h % 2 == 0 else pltpu.roll(q2, HEAD_DIM, 1), 0.0)
        score = _dot_nt_x3(kmp_ref[0, h], q0)[HEAD_DIM:HEAD_DIM + n_blk, :]
        picked = _top_k_sublanes(jnp.where(valid, score, -jnp.inf), MOBA_TOPK)
        sel = ((picked > 0.5) & valid) | (blk == own)
        bias_t = jnp.concatenate([jnp.zeros((HEAD_DIM, MOBA_Q), F32), jnp.where(sel, 0.0, NEG_BIG),
                                  jnp.zeros((LANES - HEAD_DIM - n_blk, MOBA_Q), F32)], axis=0)
        q_augs.append(jnp.where(lane < HEAD_DIM, q0 * SCALE_LOG2E, bias_t.T).astype(BF16))
    pos = qi * MOBA_Q + _iota((MOBA_Q, ATT_TK), 0)

    def tile(j, carry, masked):
        start = pl.multiple_of(j * ATT_TK, ATT_TK)
        mask = None
        if masked:
            mask = (j * ATT_TK + _iota((MOBA_Q, ATT_TK), 1)) <= pos
        return tuple(_flash_step(q_augs[h], k_ref[0, pl.ds(start, ATT_TK), h * LANES:(h + 1) * LANES],
                                 v_ref[0, pl.ds(start, ATT_TK), h * LANES:(h + 1) * LANES], mask, *carry[h])
                     for h in range(MOBA_HEADS))

    init = (jnp.full((MOBA_Q, 1), NEG_BIG, F32), jnp.zeros((MOBA_Q, LANES), F32))
    jd = (qi * MOBA_Q) // ATT_TK
    carry = lax.fori_loop(0, jd, lambda j, c: tile(j, c, False), tile(jd, (init,) * MOBA_HEADS, True))
    for c in range(MOBA_HEADS // 2):
        o_ref[0, :, c * LANES:(c + 1) * LANES] = jnp.where(
            lane < HEAD_DIM, _flash_finish(carry[2 * c][1]),
            pltpu.roll(_flash_finish(carry[2 * c + 1][1]), HEAD_DIM, 1)).astype(o_ref.dtype)


def _moba_attention(q_m, kmp, k_aug, v_aug):
    b, s, _ = q_m.shape
    nh = MOBA_HEADS
    return pl.pallas_call(
        _moba_kernel,
        grid=(b, H_MOBA // nh, s // MOBA_Q),
        in_specs=[pl.BlockSpec((1, MOBA_Q, nh * HEAD_DIM), lambda b, h, i: (b, i, h)),
                  pl.BlockSpec((1, nh, LANES, LANES), lambda b, h, i: (b, h, 0, 0)),
                  pl.BlockSpec((1, s, nh * LANES), lambda b, h, i: (b, 0, h)),
                  pl.BlockSpec((1, s, nh * LANES), lambda b, h, i: (b, 0, h))],
        out_specs=pl.BlockSpec((1, MOBA_Q, nh * HEAD_DIM), lambda b, h, i: (b, i, h)),
        out_shape=jax.ShapeDtypeStruct((b, s, W_QM), BF16),
        compiler_params=_params("parallel", "parallel", "arbitrary"),
        name="moba_attention",
    )(q_m, kmp, k_aug, v_aug)


def _stack_heads(q4):
    lane = _iota((Q_TILE, LANES), 1)
    parts = []
    for r in range(R_NSA):
        c = q4[:, (r // 2) * LANES:(r // 2 + 1) * LANES]
        if r % 2:
            c = pltpu.roll(c, HEAD_DIM, 1)
        parts.append(jnp.where(lane < HEAD_DIM, c, 0.0))
    return jnp.concatenate(parts, axis=0)


def _nsa_kernel(qn_ref, qr_ref, gate_ref, kc_ref, vc_ref, cover_ref, ks_ref, vs_ref, kw_ref, vw_ref, o_ref,
                *, n_chunk):
    qi = pl.program_id(2)
    rows = R_NSA * Q_TILE
    qloc = _iota((rows, 1), 0) % Q_TILE
    pos = qi * Q_TILE + qloc

    qn = _stack_heads(qn_ref[0])
    s = _dot_nt_x3(qn, kc_ref[0, 0]) * SCALE
    cmask = (_iota((rows, n_chunk), 1) * CMP_STRIDE + (CMP_LEN - 1)) <= pos
    s = jnp.where(cmask, s, NEG_BIG)
    e = jnp.where(cmask, jnp.exp(s - jnp.max(s, axis=1, keepdims=True)), 0.0)
    p_cmp = e * (1.0 / jnp.maximum(jnp.sum(e, axis=1, keepdims=True), TINY))
    o_cmp = _dot(p_cmp.astype(BF16), vc_ref[0, 0])

    p_grp = p_cmp[0:Q_TILE]
    for r in range(1, R_NSA):
        p_grp = p_grp + p_cmp[r * Q_TILE:(r + 1) * Q_TILE]
    p_hi, p_lo = _split_bf16(p_grp)
    imp_t = _dot_nt(cover_ref[...], p_hi) + _dot_nt(cover_ref[...], p_lo)
    blk_t = _iota((LANES, Q_TILE), 0)
    own_t = (qi * Q_TILE + _iota((LANES, Q_TILE), 1)) // SEL_BLOCK
    forced = (blk_t == 0) | (blk_t == own_t) | (blk_t == own_t - 1)
    valid_t = blk_t <= own_t
    picked = _top_k_sublanes(jnp.where(valid_t & jnp.logical_not(forced), imp_t, -jnp.inf), SEL_TOPN - N_FORCED)
    bias = jnp.where(valid_t & (forced | (picked > 0.5)), 0.0, NEG_BIG).T
    blk = _iota((Q_TILE, LANES), 1)

    qr = (_stack_heads(qr_ref[0]) * SCALE_LOG2E).astype(BF16)
    q_aug = jnp.concatenate([jnp.concatenate([bias] * R_NSA, axis=0).astype(BF16), qr], axis=1)
    init = (jnp.full((rows, 1), NEG_BIG, F32), jnp.zeros((rows, LANES), F32))

    tk = ATT_TK
    jd = (qi * Q_TILE) // tk

    def sel_tile(j, carry, masked):
        start = pl.multiple_of(j * tk, tk)
        mask = None
        if masked:
            mask = (j * tk + _iota((rows, tk), 1)) <= pos
        return _flash_step(q_aug, ks_ref[0, pl.ds(start, tk), :], vs_ref[0, pl.ds(start, tk), :], mask, *carry)

    o_sel = _flash_finish(lax.fori_loop(0, jd, lambda j, c: sel_tile(j, c, False), sel_tile(jd, init, True))[1])

    span = WINDOW + Q_TILE
    w0 = pl.multiple_of(jnp.maximum(qi * Q_TILE - WINDOW, 0), Q_TILE)
    kpos = w0 + _iota((rows, span), 1)
    wmask = (kpos <= pos) & (kpos > pos - WINDOW)
    o_win = _flash_finish(_flash_step(qr, kw_ref[0, pl.ds(w0, span), :], vw_ref[0, pl.ds(w0, span), :], wmask,
                                      *init)[1])

    gates = gate_ref[0]
    heads = []
    for r in range(R_NSA):
        rs = slice(r * Q_TILE, (r + 1) * Q_TILE)
        heads.append(gates[:, 3 * r:3 * r + 1] * o_cmp[rs] + gates[:, 3 * r + 1:3 * r + 2] * o_sel[rs]
                     + gates[:, 3 * r + 2:3 * r + 3] * o_win[rs])
    lo = blk < HEAD_DIM
    o_ref[0] = jnp.concatenate([jnp.where(lo, heads[0], pltpu.roll(heads[1], HEAD_DIM, 1)),
                                jnp.where(lo, heads[2], pltpu.roll(heads[3], HEAD_DIM, 1))], axis=1).astype(o_ref.dtype)


def _nsa_attention(qn, qr, gates_g, kcp, vcd, cover, ks_aug, vs_dup, kw_pad, vw_dup):
    b, s, _ = qn.shape
    n_chunk = kcp.shape[2]
    gq = R_NSA * HEAD_DIM
    per_g = lambda w: pl.BlockSpec((1, s, w), lambda b, g, i: (b, 0, g))
    return pl.pallas_call(
        functools.partial(_nsa_kernel, n_chunk=n_chunk),
        grid=(b, G_NSA, s // Q_TILE),
        in_specs=[pl.BlockSpec((1, Q_TILE, gq), lambda b, g, i: (b, i, g)),
                  pl.BlockSpec((1, Q_TILE, gq), lambda b, g, i: (b, i, g)),
                  pl.BlockSpec((1, Q_TILE, LANES), lambda b, g, i: (b, i, g)),
                  pl.BlockSpec((1, 1, n_chunk, LANES), lambda b, g, i: (b, g, 0, 0)),
                  pl.BlockSpec((1, 1, n_chunk, LANES), lambda b, g, i: (b, g, 0, 0)),
                  pl.BlockSpec(cover.shape, lambda b, g, i: (0, 0)),
                  per_g(2 * LANES), per_g(LANES), per_g(LANES), per_g(LANES)],
        out_specs=pl.BlockSpec((1, Q_TILE, gq), lambda b, g, i: (b, i, g)),
        out_shape=jax.ShapeDtypeStruct((b, s, W_QN), BF16),
        compiler_params=_params("parallel", "parallel", "arbitrary"),
        name="nsa_attention",
    )(qn, qr, gates_g, kcp, vcd, cover, ks_aug, vs_dup, kw_pad, vw_dup)


def _cover(n_cmp, n_sel, rows):
    c0 = np.arange(rows)[:, None] * CMP_STRIDE
    b0 = np.arange(LANES)[None, :] * SEL_BLOCK
    ok = (c0 < b0 + SEL_BLOCK) & (c0 + CMP_LEN > b0) & (np.arange(rows)[:, None] < n_cmp) & (np.arange(LANES)[None, :] < n_sel)
    return jnp.asarray(ok.astype(np.float32))


def _lane_rep(col):
    return jnp.broadcast_to(col, (col.shape[0], LANES))


def _head_sums(prod_row):
    w = prod_row.shape[1]
    own = (_iota((SUBLANES, w), 1) // HEAD_DIM) == _iota((SUBLANES, w), 0)
    return _lane_rep(jnp.sum(jnp.where(own, jnp.broadcast_to(prod_row, (SUBLANES, w)), 0.0), axis=1, keepdims=True))


def _pair_row(x8, h):
    return jnp.where(_iota((1, LANES), 1) < HEAD_DIM, x8[h:h + 1, :], x8[h + 1:h + 2, :])


def _cols_to_row(acc_a, acc_b):
    return jnp.sum(jnp.concatenate([acc_a, acc_b], axis=0).T, axis=0, keepdims=True)


def _dec_kernel(pt_ref, cm_ref, cn_ref, wb_ref, q_ref, q8n_ref, new_ref, gate_ref, wcat_ref, pecat_ref, w2_ref,
                 gk_ref, bd_ref, cover_ref, o_ref,
                 mbuf, nbuf, sem, xk, xv, qmb, qrb, s_sc, p_sc, *, n_pages, page):
    b = pl.program_id(0)
    n_req = pl.num_programs(0)
    slot = b % 2
    hw = H_MOBA * HEAD_DIM
    gw = G_NSA * HEAD_DIM
    past = n_pages * page

    def copies(req, sl):
        out = []
        for p in range(n_pages):
            pg = pt_ref[req * n_pages + p]
            out.append(pltpu.make_async_copy(cm_ref.at[pg], mbuf.at[sl, p], sem.at[0, sl]))
            out.append(pltpu.make_async_copy(cn_ref.at[pg], nbuf.at[sl, p], sem.at[1, sl]))
        return out

    @pl.when(b == 0)
    def _():
        for c in copies(0, 0):
            c.start()

    for c in copies(b, slot):
        c.wait()

    @pl.when(b + 1 < n_req)
    def _():
        for c in copies(b + 1, 1 - slot):
            c.start()

    lane1 = _iota((1, LANES), 1)
    lane8 = _iota((SUBLANES, LANES), 1)
    qrow = q_ref[0]
    new = new_ref[0]
    for c in range(hw // LANES):
        cols = slice(c * LANES, (c + 1) * LANES)
        qmb[cols, :] = jnp.broadcast_to(qrow[0:1, cols] * SCALE, (LANES, LANES)).T
        qrb[cols, :] = jnp.broadcast_to(qrow[1:2, cols] * SCALE, (LANES, LANES)).T

    def softmax_pv(scores, s_new8, v_rows, vbuf_ref, v_row0, per_g):
        m8 = s_new8
        for s in scores:
            m8 = jnp.maximum(m8, _lane_rep(jnp.max(s, axis=1, keepdims=True)))
        w_new = jnp.exp(s_new8 - m8)
        l8 = w_new
        for p, s in enumerate(scores):
            pr = jnp.exp(s - m8)
            p_sc[p] = pr
            l8 = l8 + _lane_rep(jnp.sum(pr, axis=1, keepdims=True))
        inv8 = 1.0 / jnp.maximum(l8, TINY)
        rows = []
        for hp in range(SUBLANES // 2):
            accs = []
            for h in (2 * hp, 2 * hp + 1):
                r0 = v_row0 + (h // R_NSA if per_g else h) * HEAD_DIM

                def body(p, acc, h=h, r0=r0):
                    return acc + vbuf_ref[slot, p, r0:r0 + HEAD_DIM, :] * p_sc[p, h:h + 1, :]
                acc = jnp.zeros((HEAD_DIM, LANES), F32)
                for p in range(n_pages):
                    acc = body(p, acc)
                accs.append(acc)
            row = _cols_to_row(accs[0], accs[1])
            cols = slice(hp * LANES, (hp + 1) * LANES)
            rows.append((row + _pair_row(w_new, 2 * hp) * v_rows[:, cols]) * _pair_row(inv8, 2 * hp))
        return rows

    def moba_scores(p, carry):
        rows = [jnp.sum(mbuf[slot, p, h * HEAD_DIM:(h + 1) * HEAD_DIM, :] * qmb[h * HEAD_DIM:(h + 1) * HEAD_DIM, :],
                        axis=0, keepdims=True) for h in range(H_MOBA)]
        s_sc[p] = jnp.concatenate(rows, axis=0)
        return carry
    for p in range(n_pages):
        moba_scores(p, 0)
    s_all = [s_sc[p] for p in range(n_pages)]
    ppb = MOBA_BLOCK // page
    n_blk = n_pages // ppb
    bsc = []
    for j in range(n_blk):
        tot = s_all[j * ppb]
        for t in range(1, ppb):
            tot = tot + s_all[j * ppb + t]
        bsc.append(_lane_rep(jnp.sum(tot, axis=1, keepdims=True)))
    masked = []
    for j in range(n_blk):
        rank = jnp.zeros((SUBLANES, LANES), F32)
        for c in range(n_blk):
            if c != j:
                ahead = (bsc[c] > bsc[j]) | ((bsc[c] == bsc[j]) & (c < j))
                rank = rank + jnp.where(ahead, 1.0, 0.0)
        for t in range(ppb):
            masked.append(jnp.where(rank < MOBA_TOPK, s_all[j * ppb + t], NEG_BIG))
    s_own = _head_sums(qrow[0:1, :] * new[0:1, :]) * SCALE
    o_rows = softmax_pv(masked, s_own, new[1:2, :], mbuf, hw, False)

    for p in range(n_pages):
        xk[p * page:(p + 1) * page, :] = nbuf[slot, p, 0:gw, :].T
        xv[p * page:(p + 1) * page, :] = nbuf[slot, p, gw:2 * gw, :].T
    n_chunk = past // CMP_STRIDE
    cmp_out = []
    for kv, xref in enumerate((xk, xv)):
        xcat = jnp.concatenate([xref[pl.ds(j, n_chunk, stride=CMP_STRIDE), :] for j in range(CMP_STRIDE)], axis=1)
        pe2 = _dot(pecat_ref[kv].astype(BF16), wcat_ref[kv])
        ab = _dot(xcat.astype(BF16), wcat_ref[kv])
        hid = jax.nn.gelu(ab[:, 0:gw] + pe2[0:1, 0:gw] + pltpu.roll(ab[:, gw:2 * gw] + pe2[1:2, gw:2 * gw],
                                                                    n_chunk - 1, 0))
        cmp_out.append(_dot_x3(hid, w2_ref[kv]))
    kc = _head_norm(cmp_out[0], gk_ref[...], bd_ref[...])
    vc = cmp_out[1]

    s = _dot_nt_x3(q8n_ref[0], kc) * SCALE
    cmask = _iota((SUBLANES, n_chunk), 1) < (n_chunk - 1)
    s = jnp.where(cmask, s, NEG_BIG)
    e = jnp.where(cmask, jnp.exp(s - jnp.max(s, axis=1, keepdims=True)), 0.0)
    p_cmp = e / jnp.maximum(jnp.sum(e, axis=1, keepdims=True), TINY)
    o_cmp8 = _dot(p_cmp.astype(BF16), vc.astype(BF16))
    subc = _iota((SUBLANES, n_chunk), 0)
    g0 = jnp.sum(jnp.where(subc < R_NSA, p_cmp, 0.0), axis=0, keepdims=True)
    g1 = jnp.sum(jnp.where(subc >= R_NSA, p_cmp, 0.0), axis=0, keepdims=True)
    p_grp = jnp.concatenate([jnp.where(subc < R_NSA, g0, g1), jnp.zeros((LANES - SUBLANES, n_chunk), F32)], axis=0)
    own = past // SEL_BLOCK
    n_sel = -(-(own + 1) // SUBLANES) * SUBLANES
    p_hi, p_lo = _split_bf16(p_grp)
    imp_t = (_dot_nt(cover_ref[...], p_hi) + _dot_nt(cover_ref[...], p_lo))[0:n_sel]
    blk_t = _iota((n_sel, LANES), 0)
    forced = (blk_t == 0) | (blk_t == own) | (blk_t == own - 1)
    score = jnp.where(blk_t <= own, jnp.where(forced, jnp.inf, imp_t), -jnp.inf)
    picked = _top_k_sublanes(score, SEL_TOPN)
    bias_t = jnp.concatenate([jnp.where((picked > 0.5) & (blk_t <= own), 0.0, NEG_BIG),
                              jnp.full((LANES - n_sel, LANES), NEG_BIG, F32)], axis=0)
    bias8 = bias_t.T[0:SUBLANES, :]
    cmp_rows = []
    for hp in range(H_NSA // 2):
        g = (2 * hp) // R_NSA
        ra = o_cmp8[2 * hp:2 * hp + 1, :]
        rb = o_cmp8[2 * hp + 1:2 * hp + 2, :]
        cmp_rows.append(jnp.where(lane1 < HEAD_DIM, ra if g == 0 else pltpu.roll(ra, HEAD_DIM, 1),
                                  rb if g == 1 else pltpu.roll(rb, HEAD_DIM, 1)))

    def sel_scores(p, carry):
        rows = [jnp.sum(nbuf[slot, p, 2 * gw + (i // R_NSA) * HEAD_DIM:2 * gw + (i // R_NSA + 1) * HEAD_DIM, :]
                        * qrb[i * HEAD_DIM:(i + 1) * HEAD_DIM, :], axis=0, keepdims=True) for i in range(H_NSA)]
        s_sc[p] = jnp.concatenate(rows, axis=0)
        return carry
    for p in range(n_pages):
        sel_scores(p, 0)
    bpp = page // SEL_BLOCK
    sel_s = []
    for p in range(n_pages):
        bias_p = bias8[:, p * bpp:p * bpp + 1]
        for t in range(1, bpp):
            bias_p = jnp.where(lane8 < t * SEL_BLOCK, bias_p, bias8[:, p * bpp + t:p * bpp + t + 1])
        sel_s.append(s_sc[p] + bias_p)
    s_new = _head_sums(qrow[1:2, :] * new[2:3, :]) * SCALE
    sel_rows = softmax_pv(sel_s, s_new, new[3:4, :], nbuf, 3 * gw, True)

    nw = wb_ref.shape[2]
    wk = nw // LANES
    w_s = []
    for i in range(H_NSA):
        g = i // R_NSA
        qcol = jnp.concatenate([qrb[i * HEAD_DIM:(i + 1) * HEAD_DIM, :]] * wk, axis=1)
        w_s.append(jnp.sum(wb_ref[0, g * HEAD_DIM:(g + 1) * HEAD_DIM, :] * qcol, axis=0, keepdims=True))
    s = jnp.concatenate(w_s, axis=0)
    wmask = _iota((SUBLANES, nw), 1) >= 1
    s = jnp.where(wmask, s, NEG_BIG)
    s_new = _head_sums(qrow[1:2, :] * new[4:5, :]) * SCALE
    m8 = jnp.maximum(_lane_rep(jnp.max(s, axis=1, keepdims=True)), s_new)
    pw = jnp.where(wmask, jnp.exp(s - m8[:, 0:1]), 0.0)
    w_new = jnp.exp(s_new - m8)
    inv8 = 1.0 / jnp.maximum(_lane_rep(jnp.sum(pw, axis=1, keepdims=True)) + w_new, TINY)
    win_rows = []
    for hp in range(H_NSA // 2):
        accs = []
        for i in (2 * hp, 2 * hp + 1):
            g = i // R_NSA
            prod = wb_ref[0, gw + g * HEAD_DIM:gw + (g + 1) * HEAD_DIM, :] * pw[i:i + 1, :]
            acc = prod[:, 0:LANES]
            for c in range(1, wk):
                acc = acc + prod[:, c * LANES:(c + 1) * LANES]
            accs.append(acc)
        cols = slice(hp * LANES, (hp + 1) * LANES)
        win_rows.append((_cols_to_row(accs[0], accs[1]) + _pair_row(w_new, 2 * hp) * new[5:6, cols])
                        * _pair_row(inv8, 2 * hp))

    gates = gate_ref[0]
    for c in range(hw // LANES):
        o_ref[0, :, c * LANES:(c + 1) * LANES] = o_rows[c]
    for c in range(W_QN // LANES):
        cols = slice(c * LANES, (c + 1) * LANES)
        o_ref[0, :, hw + c * LANES:hw + (c + 1) * LANES] = (
            gates[0:1, cols] * cmp_rows[c] + gates[1:2, cols] * sel_rows[c] + gates[2:3, cols] * win_rows[c])


def _dec_attention(cm, cn, wb, pt_flat, qrows, q8n, new, gate_rows, wcat, pecat, w2bd, gk, bd, cover,
                    *, n_req, n_pages):
    page = cm.shape[2]
    n_chunk = n_pages * page // CMP_STRIDE
    req = lambda a: pl.BlockSpec((1,) + a.shape[1:], lambda b, pt: (b,) + (0,) * (a.ndim - 1))
    const = lambda a: pl.BlockSpec(a.shape, lambda b, pt: (0,) * a.ndim)
    return pl.pallas_call(
        functools.partial(_dec_kernel, n_pages=n_pages, page=page),
        grid_spec=pltpu.PrefetchScalarGridSpec(
            num_scalar_prefetch=1,
            grid=(n_req,),
            in_specs=[pl.BlockSpec(memory_space=pl.ANY), pl.BlockSpec(memory_space=pl.ANY), req(wb), req(qrows),
                      req(q8n), req(new), req(gate_rows), const(wcat), const(pecat), const(w2bd), const(gk),
                      const(bd), const(cover)],
            out_specs=pl.BlockSpec((1, 1, W_QM + W_QN), lambda b, pt: (b, 0, 0)),
            scratch_shapes=[pltpu.VMEM((2, n_pages) + cm.shape[1:], F32), pltpu.VMEM((2, n_pages) + cn.shape[1:], F32),
                            pltpu.SemaphoreType.DMA((2, 2)),
                            pltpu.VMEM((n_pages * page, LANES), F32), pltpu.VMEM((n_pages * page, LANES), F32),
                            pltpu.VMEM((W_QM, LANES), F32), pltpu.VMEM((W_QN, LANES), F32),
                            pltpu.VMEM((n_pages, SUBLANES, LANES), F32), pltpu.VMEM((n_pages, SUBLANES, LANES), F32)]),
        out_shape=jax.ShapeDtypeStruct((n_req, 1, W_QM + W_QN), F32),
        compiler_params=_params("arbitrary"),
        name="dec_attention",
    )(pt_flat, cm, cn, wb, qrows, q8n, new, gate_rows, wcat, pecat, w2bd, gk, bd, cover)


def _compress_cat_consts(cmp_pos, cmp_w1):
    eye = jnp.eye(G_NSA, dtype=F32)
    w1 = cmp_w1.reshape(2, 2, CMP_STRIDE, HEAD_DIM, HEAD_DIM)
    wcat = jnp.einsum("gh,kajde->kjgdahe", eye, w1).reshape(2, CMP_STRIDE * G_NSA * HEAD_DIM, 2 * G_NSA * HEAD_DIM)
    pe = jnp.tile(cmp_pos.reshape(2, 2, CMP_STRIDE, 1, HEAD_DIM), (1, 1, 1, G_NSA, 1))
    pecat = jnp.pad(pe.reshape(2, 2, CMP_STRIDE * G_NSA * HEAD_DIM), ((0, 0), (0, SUBLANES - 2), (0, 0)))
    return wcat.astype(BF16), pecat


def _post_kernel(o_ref, x_ref, gt_ref, sh_ref, sc_ref, g_ref, wo_ref, wr_ref, br_ref,
                 y_ref, h3_ref, te_ref, tw_ref):
    y = x_ref[...] + gt_ref[0] * _dot(o_ref[...], wo_ref[...])
    y_ref[...] = y
    h = y * lax.rsqrt(jnp.mean(y * y, axis=1, keepdims=True) + NORM_EPS) * g_ref[...]
    h = h * (1.0 + sc_ref[0]) + sh_ref[0]
    tm = h.shape[0]
    for s in range(h.shape[1] // LANES):
        h3_ref[pl.ds(s, tm, stride=SUBLANES), :] = h[:, s * LANES:(s + 1) * LANES]
    h_hi, h_lo = _split_bf16(h)
    w_hi, w_lo = _split_bf16(wr_ref[...])
    logits = _dot(h_hi, w_hi) + _dot(h_lo, w_hi) + _dot(h_hi, w_lo) + br_ref[...]
    _, vals, ids = _top_k_lanes(logits, TOP_K)
    lane = _iota((tm, LANES), 1)
    es = [jnp.exp(v - vals[0]) for v in vals]
    den = es[0]
    for e in es[1:]:
        den = den + e
    te = jnp.zeros((tm, LANES), F32)
    tw = jnp.zeros((tm, LANES), F32)
    for k in range(TOP_K):
        te = jnp.where(lane == k, ids[k], te)
        tw = jnp.where(lane == k, es[k] / den, tw)
    te_ref[...] = te.astype(jnp.int32)
    tw_ref[...] = tw


def _post(o, x, gate, shift, scale, g2, wo_bf, wr_pad, br_pad, *, tm, rows_per_mod):
    n, d = x.shape
    mod_r = gate.shape[1]
    mod_map = lambda i: (i // (rows_per_mod // tm), 0, 0)
    row = lambda w: pl.BlockSpec((tm, w), lambda i: (i, 0))
    const = lambda a: pl.BlockSpec(a.shape, lambda i: (0,) * a.ndim)
    mod = pl.BlockSpec((1, mod_r, d), mod_map)
    return pl.pallas_call(
        _post_kernel,
        grid=(n // tm,),
        in_specs=[row(d), row(d), mod, mod, mod, const(g2), const(wo_bf), const(wr_pad), const(br_pad)],
        out_specs=[row(d), pl.BlockSpec((tm * SUBLANES, LANES), lambda i: (i, 0)), row(LANES), row(LANES)],
        out_shape=[jax.ShapeDtypeStruct((n, d), F32), jax.ShapeDtypeStruct((n * SUBLANES, LANES), F32),
                   jax.ShapeDtypeStruct((n, LANES), jnp.int32), jax.ShapeDtypeStruct((n, LANES), F32)],
        compiler_params=_params("parallel"),
        name="post",
    )(o, x, gate, shift, scale, g2, wo_bf, wr_pad, br_pad)


def _dispatch_kernel(pos_ref, h3_ref, xs_in_ref, xs_ref, sem, *, tm, tile_off):
    del xs_in_ref
    i = pl.program_id(0)

    def body(r, carry):
        src = h3_ref.at[pl.ds(pl.multiple_of(r * SUBLANES, SUBLANES), SUBLANES), :]
        for k in range(TOP_K):
            p = pos_ref[((i + tile_off) * tm + r) * TOP_K + k]
            pltpu.make_async_copy(src, xs_ref.at[pl.ds(pl.multiple_of(p * SUBLANES, SUBLANES), SUBLANES), :],
                                  sem.at[0]).start(priority=k % 2)
        return carry
    lax.fori_loop(0, tm, body, 0, unroll=COPY_UNROLL)
    for k in range(TOP_K):
        pltpu.make_async_copy(h3_ref, xs_ref.at[pl.ds(0, tm * SUBLANES), :], sem.at[0]).wait()


def _dispatch(pos_flat, h3, xs, *, tm, tile_off):
    n8 = h3.shape[0]
    return pl.pallas_call(
        functools.partial(_dispatch_kernel, tm=tm, tile_off=tile_off),
        grid_spec=pltpu.PrefetchScalarGridSpec(
            num_scalar_prefetch=1,
            grid=(n8 // (tm * SUBLANES),),
            in_specs=[pl.BlockSpec((tm * SUBLANES, LANES), lambda i, pos: (i, 0)), pl.BlockSpec(memory_space=pl.ANY)],
            out_specs=pl.BlockSpec(memory_space=pl.ANY),
            scratch_shapes=[pltpu.SemaphoreType.DMA((1,))]),
        out_shape=jax.ShapeDtypeStruct(xs.shape, xs.dtype),
        input_output_aliases={2: 0},
        compiler_params=_params("arbitrary"),
        name="dispatch",
    )(pos_flat, h3, xs)


def _expert_block_kernel(be_ref, na_ref, x_ref, wgu_ref, bgu_ref, wd_ref, bd_ref, y_ref, xb, wgu_bf, wd_bf):
    i = pl.program_id(0)
    active = i < na_ref[0]

    @pl.when(active & ((i == 0) | (be_ref[i] != be_ref[jnp.maximum(i - 1, 0)])))
    def _():
        for r in range(0, wgu_bf.shape[0], LANES):
            wgu_bf[r:r + LANES, :] = wgu_ref[0, r:r + LANES, :].astype(BF16)
        for r in range(0, wd_bf.shape[0], LANES):
            wd_bf[r:r + LANES, :] = wd_ref[0, r:r + LANES, :].astype(BF16)

    @pl.when(active)
    def _():
        d = xb.shape[1]
        for s in range(d // LANES):
            xb[:, s * LANES:(s + 1) * LANES] = x_ref[pl.ds(s, MOE_ROWS, stride=SUBLANES), :].astype(BF16)
        gu = _dot(xb[...], wgu_bf[...]) + bgu_ref[0]
        f = gu.shape[1] // 2
        gt = jnp.minimum(gu[:, 0:f], SWIGLU_LIMIT)
        up = jnp.clip(gu[:, f:2 * f], -SWIGLU_LIMIT, SWIGLU_LIMIT)
        act = (up + 1.0) * (gt * jax.nn.sigmoid(SWIGLU_ALPHA * gt))
        y = _dot(act.astype(BF16), wd_bf[...]) + bd_ref[0]
        for s in range(d // LANES):
            y_ref[pl.ds(s, MOE_ROWS, stride=SUBLANES), :] = y[:, s * LANES:(s + 1) * LANES]

    @pl.when(i >= na_ref[0])
    def _():
        y_ref[...] = jnp.zeros(y_ref.shape, F32)


def _expert_blocks(blk_e, n_active, xs, w_gu, b_gu, w_down, b_down):
    n_blocks = blk_e.shape[0]
    e, d, f2 = w_gu.shape
    rows8 = MOE_ROWS * SUBLANES
    row_map = lambda i, be, na: (jnp.minimum(i, na[0] - 1), 0)
    return pl.pallas_call(
        _expert_block_kernel,
        grid_spec=pltpu.PrefetchScalarGridSpec(
            num_scalar_prefetch=2,
            grid=(n_blocks,),
            in_specs=[pl.BlockSpec((rows8, LANES), row_map),
                      pl.BlockSpec((1, d, f2), lambda i, be, na: (be[i], 0, 0)),
                      pl.BlockSpec((1, 1, f2), lambda i, be, na: (be[i], 0, 0)),
                      pl.BlockSpec((1, f2 // 2, d), lambda i, be, na: (be[i], 0, 0)),
                      pl.BlockSpec((1, 1, d), lambda i, be, na: (be[i], 0, 0))],
            out_specs=pl.BlockSpec((rows8, LANES), lambda i, be, na: (i, 0)),
            scratch_shapes=[pltpu.VMEM((MOE_ROWS, d), BF16), pltpu.VMEM((d, f2), BF16),
                            pltpu.VMEM((f2 // 2, d), BF16)]),
        out_shape=jax.ShapeDtypeStruct((n_blocks * rows8, LANES), F32),
        compiler_params=_params("arbitrary"),
        name="experts",
    )(blk_e, n_active, xs, w_gu, b_gu.reshape(e, 1, f2), w_down, b_down.reshape(e, 1, d))


def _routing_pos(top_e, n_tok):
    n_assign = n_tok * TOP_K
    n_pad = -(-n_assign // LANES) * LANES
    e_flat = jnp.pad(top_e.reshape(-1), (0, n_pad - n_assign), constant_values=N_EXPERTS)
    onehot = (e_flat[:, None] == jnp.arange(N_EXPERTS)[None, :]).astype(F32).reshape(n_pad // LANES, LANES, N_EXPERTS)
    tril = jnp.tril(jnp.ones((LANES, LANES), F32))
    within = jnp.einsum("ij,bjk->bik", tril, onehot)
    block_tot = within[:, -1, :]
    offs = jnp.cumsum(block_tot, axis=0) - block_tot
    counts = jnp.sum(block_tot, axis=0).astype(jnp.int32)
    padded = (counts + MOE_ROWS - 1) // MOE_ROWS * MOE_ROWS
    pad_end = jnp.cumsum(padded)
    pad_start = (pad_end - padded).astype(F32)
    slot = jnp.sum(onehot * (within + offs[:, None, :] - 1.0 + pad_start[None, None, :]), axis=-1)
    pos = slot.reshape(-1)[:n_assign].astype(jnp.int32)
    n_blocks = -(-n_assign // MOE_ROWS) + N_EXPERTS
    starts = jnp.arange(n_blocks, dtype=jnp.int32) * MOE_ROWS
    blk_e = jnp.minimum(jnp.sum((pad_end[None, :] <= starts[:, None]).astype(jnp.int32), axis=1), N_EXPERTS - 1)
    n_active = (pad_end[-1] // MOE_ROWS).astype(jnp.int32).reshape(1)
    return blk_e, n_active, pos


def _combine_kernel(pos_ref, ys_ref, y1_ref, gt_ref, tw_ref, o_ref, buf, sem, *, tile_off):
    i = pl.program_id(0)
    n = pl.num_programs(0)
    tm = y1_ref.shape[0]
    rows8 = tm * SUBLANES

    def gather(tile, slot):
        def body(r, carry):
            for k in range(TOP_K):
                p = pos_ref[((tile + tile_off) * tm + r) * TOP_K + k]
                pltpu.make_async_copy(ys_ref.at[pl.ds(pl.multiple_of(p * SUBLANES, SUBLANES), SUBLANES), :],
                                      buf.at[slot, k, pl.ds(pl.multiple_of(r * SUBLANES, SUBLANES), SUBLANES), :],
                                      sem.at[slot]).start(priority=k % 2)
            return carry
        lax.fori_loop(0, tm, body, 0, unroll=COPY_UNROLL)

    @pl.when(i == 0)
    def _():
        gather(0, 0)

    slot = i % 2
    for k in range(TOP_K):
        pltpu.make_async_copy(ys_ref.at[pl.ds(0, rows8), :], buf.at[slot, k], sem.at[slot]).wait()

    @pl.when(i + 1 < n)
    def _():
        gather(i + 1, 1 - slot)

    tw = tw_ref[...]
    wk = [jnp.broadcast_to(tw[:, k:k + 1], (tm, LANES)) for k in range(TOP_K)]
    gt = gt_ref[0]
    for s in range(o_ref.shape[1] // LANES):
        moe = wk[0] * buf[slot, 0, pl.ds(s, tm, stride=SUBLANES), :]
        for k in range(1, TOP_K):
            moe = moe + wk[k] * buf[slot, k, pl.ds(s, tm, stride=SUBLANES), :]
        cols = slice(s * LANES, (s + 1) * LANES)
        o_ref[:, cols] = y1_ref[:, cols] + gt[:, cols] * moe


def _combine(pos_flat, ys, y1, gate, tw, *, tm, rows_per_mod, tile_off):
    n, d = y1.shape
    mod_r = gate.shape[1]
    return pl.pallas_call(
        functools.partial(_combine_kernel, tile_off=tile_off),
        grid_spec=pltpu.PrefetchScalarGridSpec(
            num_scalar_prefetch=1,
            grid=(n // tm,),
            in_specs=[pl.BlockSpec(memory_space=pl.ANY),
                      pl.BlockSpec((tm, d), lambda i, pos: (i, 0)),
                      pl.BlockSpec((1, mod_r, d), lambda i, pos: (i // (rows_per_mod // tm), 0, 0)),
                      pl.BlockSpec((tm, LANES), lambda i, pos: (i, 0))],
            out_specs=pl.BlockSpec((tm, d), lambda i, pos: (i, 0)),
            scratch_shapes=[pltpu.VMEM((2, TOP_K, tm * SUBLANES, LANES), F32), pltpu.SemaphoreType.DMA((2,))]),
        out_shape=jax.ShapeDtypeStruct((n, d), F32),
        compiler_params=_params("arbitrary"),
        name="combine",
    )(pos_flat, ys, y1, gate, tw)


def _rope_tables(pos):
    half = HEAD_DIM // 2
    inv = ROPE_THETA ** (-jnp.arange(half, dtype=F32) / half)
    ang = pos.astype(F32)[:, None] * inv[None, :]
    cos = jnp.cos(ang)
    sin = jnp.sin(ang)
    reps = LANES // HEAD_DIM
    return (jnp.tile(jnp.concatenate([cos, cos], axis=1), (1, reps)),
            jnp.tile(jnp.concatenate([-sin, sin], axis=1), (1, reps)))


def _head_pad(a):
    return jnp.concatenate([a, jnp.zeros_like(a)], axis=-1)


def kernel(x_prompt, x_sample, c_prompt, c_sample, cache_moba_kv, cache_nsa_kv, state_nsa_win_kv, page_table,
           norm_g, w_ada, b_ada, w_in, qk_gain, cmp_pos, cmp_w1, cmp_w2, w_out, w_router, b_router, w_gu, b_gu,
           w_down, b_down):
    bsz, seq, d = x_prompt.shape
    n_req = x_sample.shape[0]
    depth = norm_g.shape[0]
    assert depth == 1 and x_sample.shape[1] == 1
    assert seq % ATT_TK == 0 and seq >= WINDOW + Q_TILE and seq // SEL_BLOCK <= LANES and seq // MOBA_BLOCK <= MOBA_BLOCK // SUBLANES
    n_pool, page = cache_moba_kv.shape[1], cache_moba_kv.shape[2]
    n_pages = page_table.shape[1]
    past = n_pages * page
    assert past % MOBA_BLOCK == 0 and 2 * page == MOBA_BLOCK and past // SEL_BLOCK < LANES
    assert state_nsa_win_kv.shape[2] == WINDOW
    layer = 0
    gw = G_NSA * HEAD_DIM
    n_prompt = bsz * seq

    bd = jnp.asarray(np.kron(np.eye(LANES // HEAD_DIM), np.full((HEAD_DIM, HEAD_DIM), 1.0 / HEAD_DIM)), BF16)
    w_in_bf = jnp.pad(w_in[layer], ((0, 0), (0, IN_COLS_PAD - IN_COLS))).astype(BF16)
    gains = jnp.tile(qk_gain[layer], (1, W_QM // HEAD_DIM))
    g1 = norm_g[layer, 0].reshape(1, d)
    g2 = norm_g[layer, 1].reshape(1, d)
    wo_bf = w_out[layer].astype(BF16)
    wr_pad = jnp.pad(w_router[layer], ((0, 0), (0, LANES - N_EXPERTS)))
    br_pad = jnp.pad(b_router[layer].reshape(1, N_EXPERTS), ((0, 0), (0, LANES - N_EXPERTS)),
                     constant_values=-jnp.inf)
    cmp_consts = _compress_consts(cmp_pos[layer], cmp_w1[layer], cmp_w2[layer], qk_gain[layer, 3])

    n_c = bsz + n_req
    n_c_pad = -(-n_c // SUBLANES) * SUBLANES
    c_all = jnp.pad(jnp.concatenate([c_prompt, c_sample], axis=0), ((0, n_c_pad - n_c), (0, 0)))
    mods = _ada(c_all, w_ada[layer], b_ada[layer])
    mods_p = [m.reshape(bsz, 1, d) for m in jnp.split(mods[:bsz], 6, axis=1)]
    mods_s = [m.reshape(1, n_req, d) for m in jnp.split(mods[bsz:n_c], 6, axis=1)]

    cos_p, sin_p = _rope_tables(jnp.arange(seq, dtype=jnp.int32))
    tm_p = 256
    (moba_t, nsa_t, win_t, q_m, qn, qr, _, kmean, cmp_raw, k_aug, v_aug, ks_aug, vs_aug, kw_pad, vw_aug,
     gates_g) = _proj(
        x_prompt.reshape(n_prompt, d), mods_p[0], mods_p[1], g1, w_in_bf, gains, bd, cos_p, sin_p,
        tm=tm_p, rows_per_mod=seq, pos_blocks=seq // tm_p, with_kmean=True)

    nbk = seq // MOBA_BLOCK
    kmean_h = kmean.reshape(bsz, nbk, H_MOBA, HEAD_DIM).transpose(0, 2, 1, 3)
    kmp = jnp.zeros((bsz, H_MOBA, LANES, LANES), F32).at[:, :, HEAD_DIM:HEAD_DIM + nbk, :HEAD_DIM].set(kmean_h)
    per_b = lambda a: a.reshape(bsz, seq, a.shape[-1])
    o_m = _moba_attention(per_b(q_m), kmp, per_b(k_aug), per_b(v_aug))

    kc, vc = _compress_prompt(per_b(cmp_raw), cmp_consts, bd)
    n_chunk = seq // CMP_STRIDE
    per_g = lambda a: a.reshape(bsz, -1, G_NSA, HEAD_DIM).transpose(0, 2, 1, 3)
    kcp = _head_pad(per_g(kc))
    vcd = _head_pad(per_g(vc)).astype(BF16)
    cover_p = _cover(n_chunk - 1, seq // SEL_BLOCK, n_chunk).T.astype(BF16)
    o_n = _nsa_attention(per_b(qn), per_b(qr), per_b(gates_g), kcp, vcd, cover_p, per_b(ks_aug), per_b(vs_aug),
                         per_b(kw_pad), per_b(vw_aug))
    o_p = jnp.concatenate([o_m, o_n], axis=-1).reshape(n_prompt, d)

    y1_p, h3_p, te_p, tw_p = _post(o_p, x_prompt.reshape(n_prompt, d), mods_p[2], mods_p[3], mods_p[4], g2, wo_bf,
                                   wr_pad, br_pad, tm=256, rows_per_mod=seq)

    cos_s, sin_s = _rope_tables(jnp.full((n_req,), past, jnp.int32))
    moba_new, nsa_new, win_new, q_m_s, qn_s, qr_s, gates_s = _proj(
        x_sample.reshape(n_req, d), mods_s[0], mods_s[1], g1, w_in_bf, gains, bd, cos_s, sin_s,
        tm=n_req, rows_per_mod=n_req, pos_blocks=1, with_kmean=False)
    pt_flat = page_table.reshape(-1).astype(jnp.int32)
    cache_m = cache_moba_kv[layer].transpose(0, 2, 3, 4, 1).reshape(n_pool, W_KVM, page)
    cache_n = cache_nsa_kv[layer].transpose(0, 2, 3, 4, 1).reshape(n_pool, 4 * gw, page)
    win_buf = state_nsa_win_kv[layer].transpose(0, 2, 3, 4, 1).reshape(n_req, 2 * gw, WINDOW)

    def rows8(q):
        qh = q.reshape(n_req, G_NSA, R_NSA, 1, HEAD_DIM)
        place = jnp.arange(G_NSA)[None, :, None, None, None] == jnp.arange(G_NSA)[None, None, None, :, None]
        return jnp.where(place, qh, 0.0).reshape(n_req, H_NSA, gw)

    def per_head(a):
        return jnp.repeat(a.reshape(n_req, G_NSA, HEAD_DIM), R_NSA, axis=1).reshape(n_req, W_QN)

    new_rows = jnp.stack([moba_new[:, :W_QM], moba_new[:, W_QM:], per_head(nsa_new[:, 2 * gw:3 * gw]),
                          per_head(nsa_new[:, 3 * gw:]), per_head(win_new[:, :gw]), per_head(win_new[:, gw:])], axis=1)
    gate_rows = jnp.repeat(gates_s[:, :N_GATE].reshape(n_req, H_NSA, 3).transpose(0, 2, 1), HEAD_DIM, axis=2)
    n_cmp_s = past // CMP_STRIDE
    cover_s = _cover(n_cmp_s - 1, past // SEL_BLOCK + 1, n_cmp_s).T.astype(BF16)
    wcat, pecat = _compress_cat_consts(cmp_pos[layer], cmp_w1[layer])
    o_s = _dec_attention(cache_m, cache_n, win_buf, pt_flat, jnp.stack([q_m_s, qr_s], axis=1), rows8(qn_s), new_rows,
                          gate_rows, wcat, pecat, cmp_consts[2], cmp_consts[3], bd, cover_s,
                          n_req=n_req, n_pages=n_pages).reshape(n_req, d).astype(BF16)
    y1_s, h3_s, te_s, tw_s = _post(o_s, x_sample.reshape(n_req, d), mods_s[2], mods_s[3], mods_s[4], g2, wo_bf,
                                   wr_pad, br_pad, tm=n_req, rows_per_mod=n_req)

    n_tok = n_prompt + n_req
    top_e = jnp.concatenate([te_p[:, :TOP_K], te_s[:, :TOP_K]], axis=0)
    blk_e, n_active, pos_flat = _routing_pos(top_e, n_tok)
    tm_c = 128
    xs = jnp.zeros((blk_e.shape[0] * MOE_ROWS * SUBLANES, LANES), F32)
    xs = _dispatch(pos_flat, h3_p, xs, tm=2 * tm_c, tile_off=0)
    xs = _dispatch(pos_flat, h3_s, xs, tm=n_req, tile_off=n_prompt // n_req)
    ys = _expert_blocks(blk_e, n_active, xs, w_gu[layer], b_gu[layer], w_down[layer], b_down[layer])
    y_p = _combine(pos_flat, ys, y1_p, mods_p[5], tw_p, tm=tm_c, rows_per_mod=seq, tile_off=0)
    y_s = _combine(pos_flat, ys, y1_s, mods_s[5], tw_s, tm=n_req, rows_per_mod=n_req, tile_off=n_prompt // n_req)

    def rows_view(t, n_slot, n_head):
        return t.reshape(1, bsz, n_slot, n_head, HEAD_DIM, t.shape[-1]).transpose(0, 1, 5, 2, 3, 4)

    keep = min(WINDOW, seq)
    win_s = jnp.concatenate([state_nsa_win_kv[layer][:, 1:], win_new.reshape(n_req, 1, 2, G_NSA, HEAD_DIM)], axis=1)
    return (y_p.reshape(bsz, seq, d), y_s.reshape(n_req, 1, d),
            rows_view(moba_t, 2, H_MOBA), rows_view(nsa_t, 4, G_NSA), rows_view(win_t[:, :, seq - keep:], 2, G_NSA),
            moba_new.reshape(1, n_req, 1, 2, H_MOBA, HEAD_DIM),
            nsa_new.reshape(1, n_req, 1, 4, G_NSA, HEAD_DIM), win_s[None])
```

```python
import functools

import numpy as np
import jax
import jax.numpy as jnp
from jax import lax
from jax.experimental import pallas as pl
from jax.experimental.pallas import tpu as pltpu

F32 = jnp.float32
BF16 = jnp.bfloat16
HIGHEST = lax.Precision.HIGHEST

LANES = 128
SUBLANES = 8
HEAD_DIM = 64
H_MOBA = 8
H_NSA = 8
G_NSA = 2
R_NSA = H_NSA // G_NSA
MOBA_BLOCK = 256
MOBA_TOPK = 3
CMP_LEN = 32
CMP_STRIDE = 16
SEL_BLOCK = 64
SEL_TOPN = 16
N_FORCED = 3
WINDOW = 512
N_EXPERTS = 32
TOP_K = 4
SWIGLU_LIMIT = 7.0
SWIGLU_ALPHA = 1.702
ROPE_THETA = 10000.0
NORM_EPS = 1e-6
NEG_BIG = -1e30
TINY = 1e-30
SCALE = HEAD_DIM ** -0.5
SCALE_LOG2E = SCALE * 1.4426950408889634
Q_TILE = 256
MOBA_Q = 1024
MOBA_HEADS = 2
ATT_TK = 1024
MOE_ROWS = 256
COPY_UNROLL = 8
DISPATCH_TM = 1024
VMEM_LIMIT = 56 * 1024 * 1024

W_QM = H_MOBA * HEAD_DIM
W_KVM = 2 * H_MOBA * HEAD_DIM
W_QN = H_NSA * HEAD_DIM
W_KVN = 6 * G_NSA * HEAD_DIM
N_GATE = 3 * H_NSA
IN_COLS = W_QM + W_KVM + W_QN + W_KVN + N_GATE
IN_COLS_PAD = W_QM + W_KVM + W_QN + W_KVN + LANES


def _iota(shape, dim):
    return lax.broadcasted_iota(jnp.int32, shape, dim)


def _dot(a, b, precision=None):
    return jnp.dot(a, b, preferred_element_type=F32, precision=precision)


def _dot_nt(a, b, precision=None):
    return lax.dot_general(a, b, (((1,), (1,)), ((), ())), preferred_element_type=F32, precision=precision)


def _split_bf16(a):
    hi = a.astype(BF16)
    return hi, (a - hi.astype(F32)).astype(BF16)


def _dot_nt_x3(a, b):
    ah, al = _split_bf16(a)
    bh, bl = _split_bf16(b)
    return _dot_nt(ah, bh) + _dot_nt(al, bh) + _dot_nt(ah, bl)


def _dot_x3(a, b):
    ah, al = _split_bf16(a)
    bh, bl = _split_bf16(b)
    return _dot(ah, bh) + _dot(al, bh) + _dot(ah, bl)


def _params(*sem):
    return pltpu.CompilerParams(dimension_semantics=sem, vmem_limit_bytes=VMEM_LIMIT)


def _seg_meansq(z, bd):
    zz = z * z
    hi = zz.astype(BF16)
    lo = (zz - hi.astype(F32)).astype(BF16)
    outs = []
    for c in range(z.shape[1] // LANES):
        sl = slice(c * LANES, (c + 1) * LANES)
        outs.append(_dot(hi[:, sl], bd) + _dot(lo[:, sl], bd))
    return outs[0] if len(outs) == 1 else jnp.concatenate(outs, axis=1)


def _head_norm(z, gain, bd):
    return z * lax.rsqrt(_seg_meansq(z, bd) + NORM_EPS) * gain


def _rope(z, cos, sin):
    outs = []
    first = (_iota((z.shape[0], LANES), 1) % HEAD_DIM) < (HEAD_DIM // 2)
    for c in range(z.shape[1] // LANES):
        x = z[:, c * LANES:(c + 1) * LANES]
        swapped = jnp.where(first, pltpu.roll(x, LANES - HEAD_DIM // 2, 1), pltpu.roll(x, HEAD_DIM // 2, 1))
        outs.append(x * cos + swapped * sin)
    return outs[0] if len(outs) == 1 else jnp.concatenate(outs, axis=1)


def _top_k_lanes(cur, k):
    lane = _iota(cur.shape, 1).astype(F32)
    picked = jnp.zeros(cur.shape, F32)
    vals, ids = [], []
    for _ in range(k):
        mx = jnp.max(cur, axis=1, keepdims=True)
        first = jnp.min(jnp.where(cur == mx, lane, 1e9), axis=1, keepdims=True)
        hit = lane == first
        picked = jnp.where(hit, 1.0, picked)
        cur = jnp.where(hit, -jnp.inf, cur)
        vals.append(mx)
        ids.append(first)
    return picked, vals, ids


def _top_k_sublanes(cur, k):
    idx = _iota(cur.shape, 0).astype(F32)
    picked = jnp.zeros(cur.shape, F32)
    for _ in range(k):
        mx = jnp.max(cur, axis=0, keepdims=True)
        first = jnp.min(jnp.where(cur == mx, idx, 1e9), axis=0, keepdims=True)
        hit = idx == first
        picked = jnp.where(hit, 1.0, picked)
        cur = jnp.where(hit, -jnp.inf, cur)
    return picked


def _flash_step(q, k, v, mask, m, acc):
    s = _dot_nt(q, k)
    if mask is not None:
        s = jnp.where(mask, s, NEG_BIG)
    m_new = jnp.maximum(m, jnp.max(s, axis=1, keepdims=True))
    acc_new = jnp.exp2(m - m_new) * acc + _dot(jnp.exp2(s - m_new).astype(BF16), v)
    return m_new, acc_new


def _flash_finish(acc):
    return acc / jnp.maximum(acc[:, HEAD_DIM:HEAD_DIM + 1], TINY)


def _ada_kernel(c_ref, w_ref, b_ref, o_ref):
    c = c_ref[...]
    o_ref[...] = _dot(c * jax.nn.sigmoid(c), w_ref[...], HIGHEST) + b_ref[...]


def _ada(c_all, w_ada, b_ada):
    n, d = c_all.shape
    cols = w_ada.shape[1]
    tn = 1024
    return pl.pallas_call(
        _ada_kernel,
        grid=(cols // tn,),
        in_specs=[pl.BlockSpec((n, d), lambda j: (0, 0)),
                  pl.BlockSpec((d, tn), lambda j: (0, j)),
                  pl.BlockSpec((1, tn), lambda j: (0, j))],
        out_specs=pl.BlockSpec((n, tn), lambda j: (0, j)),
        out_shape=jax.ShapeDtypeStruct((n, cols), F32),
        compiler_params=_params("arbitrary"),
        name="ada",
    )(c_all, w_ada, b_ada.reshape(1, cols))


def _proj_kernel(x_ref, sh_ref, sc_ref, g_ref, w_ref, gains_ref, bd_ref, cos_ref, sin_ref,
                 moba_ref, nsa_ref, win_ref, qm_ref, qn_ref, qr_ref, gate_ref, *attn_refs, with_kmean, pos_blocks):
    x = x_ref[...]
    y = x * lax.rsqrt(jnp.mean(x * x, axis=1, keepdims=True) + NORM_EPS) * g_ref[...]
    h = (y * (1.0 + sc_ref[0]) + sh_ref[0]).astype(BF16)
    bd = bd_ref[...]
    cos = cos_ref[...]
    sin = sin_ref[...]
    o = 0

    def seg(width):
        nonlocal o
        z = _dot(h, w_ref[:, o:o + width])
        o += width
        return z

    def gain(i, width):
        return gains_ref[i:i + 1, 0:width]

    qm_ref[...] = _rope(_head_norm(seg(W_QM), gain(0, W_QM), bd), cos, sin)
    k_m = _rope(_head_norm(seg(W_QM), gain(1, W_QM), bd), cos, sin)
    v_m = seg(W_QM)
    qn = _head_norm(seg(W_QN), gain(2, W_QN), bd)
    qn_ref[...] = qn
    qr_ref[...] = _rope(qn, cos, sin)
    gw = G_NSA * HEAD_DIM
    cmp_raw = seg(2 * gw)
    k_sel = _rope(_head_norm(seg(gw), gain(4, gw), bd), cos, sin)
    v_sel = seg(gw)
    k_win = _rope(_head_norm(seg(gw), gain(5, gw), bd), cos, sin)
    v_win = seg(gw)
    gates = jax.nn.sigmoid(seg(LANES))
    gate_ref[...] = gates
    moba_rows = jnp.concatenate([k_m, v_m], axis=1)
    nsa_rows = jnp.concatenate([cmp_raw, k_sel, v_sel], axis=1)
    win_rows = jnp.concatenate([k_win, v_win], axis=1)
    if not with_kmean:
        moba_ref[...] = moba_rows
        nsa_ref[...] = nsa_rows
        win_ref[...] = win_rows
    else:
        kmean_ref, cmpraw_ref, kaug_ref, vaug_ref, ksaug_ref, vsaug_ref, kwp_ref, vwaug_ref, gg_ref = attn_refs
        moba_ref[0] = moba_rows.T
        nsa_ref[0] = nsa_rows.T
        win_ref[0] = win_rows.T
        cmpraw_ref[...] = cmp_raw
        tm = k_m.shape[0]
        kmean_ref[0] = jnp.mean(k_m.reshape(tm // MOBA_BLOCK, MOBA_BLOCK, W_QM), axis=1)
        lane = _iota((tm, LANES), 1)
        posv = (pl.program_id(0) % pos_blocks) * tm + _iota((tm, LANES), 0)
        lo = lane < HEAD_DIM
        pad_blk = jnp.where((lane >= HEAD_DIM) & (lane - HEAD_DIM == posv // MOBA_BLOCK), 1.0, 0.0)
        pad_one = jnp.where(lane == HEAD_DIM, 1.0, 0.0)
        oh_sel = jnp.where(lane == posv // SEL_BLOCK, 1.0, 0.0).astype(BF16)

        def halves(x):
            return x, pltpu.roll(x, HEAD_DIM, 1)

        for c in range(W_QM // LANES):
            cols = slice(c * LANES, (c + 1) * LANES)
            for hh, (kh, vh) in enumerate(zip(halves(k_m[:, cols]), halves(v_m[:, cols]))):
                hcols = slice((2 * c + hh) * LANES, (2 * c + hh + 1) * LANES)
                kaug_ref[:, hcols] = jnp.where(lo, kh, pad_blk).astype(BF16)
                vaug_ref[:, hcols] = jnp.where(lo, vh, pad_one).astype(BF16)
        for g, (ks, vs, kw, vw) in enumerate(zip(halves(k_sel), halves(v_sel), halves(k_win), halves(v_win))):
            ksaug_ref[:, 2 * g * LANES:(2 * g + 1) * LANES] = oh_sel
            ksaug_ref[:, (2 * g + 1) * LANES:(2 * g + 2) * LANES] = jnp.where(lo, ks, 0.0).astype(BF16)
            gcols = slice(g * LANES, (g + 1) * LANES)
            vsaug_ref[:, gcols] = jnp.where(lo, vs, pad_one).astype(BF16)
            kwp_ref[:, gcols] = jnp.where(lo, kw, 0.0).astype(BF16)
            vwaug_ref[:, gcols] = jnp.where(lo, vw, pad_one).astype(BF16)
            gg_ref[:, gcols] = gates if g == 0 else pltpu.roll(gates, LANES - g * 3 * R_NSA, 1)


def _proj(x, shift, scale, g, w_in_bf, gains, bd, cos, sin, *, tm, rows_per_mod, pos_blocks, with_kmean):
    n, d = x.shape
    nt = n // tm
    mod_r = shift.shape[1]
    mod_map = lambda i: (i // (rows_per_mod // tm), 0, 0)
    pos_map = lambda i: (i % pos_blocks, 0)
    row = lambda w: pl.BlockSpec((tm, w), lambda i: (i, 0))
    const = lambda a: pl.BlockSpec(a.shape, lambda i: (0,) * a.ndim)
    gw = G_NSA * HEAD_DIM
    cache_widths = (W_KVM, 4 * gw, 2 * gw)
    if with_kmean:
        nb = n // rows_per_mod
        tpb = rows_per_mod // tm
        out_shapes = [jax.ShapeDtypeStruct((nb, w, rows_per_mod), F32) for w in cache_widths]
        out_specs = [pl.BlockSpec((1, w, tm), lambda i: (i // tpb, 0, i % tpb)) for w in cache_widths]
    else:
        out_shapes = [jax.ShapeDtypeStruct((n, w), F32) for w in cache_widths]
        out_specs = [row(w) for w in cache_widths]
    out_shapes += [jax.ShapeDtypeStruct((n, W_QM), F32), jax.ShapeDtypeStruct((n, W_QN), F32),
                   jax.ShapeDtypeStruct((n, W_QN), F32), jax.ShapeDtypeStruct((n, LANES), F32)]
    out_specs += [row(W_QM), row(W_QN), row(W_QN), row(LANES)]
    if with_kmean:
        nbt = tm // MOBA_BLOCK
        out_shapes.append(jax.ShapeDtypeStruct((nt, nbt, W_QM), F32))
        out_specs.append(pl.BlockSpec((1, nbt, W_QM), lambda i: (i, 0, 0)))
        gl = G_NSA * LANES
        for width, dtype in ((2 * gw, F32), (H_MOBA * LANES, BF16), (H_MOBA * LANES, BF16), (2 * gl, BF16), (gl, BF16),
                             (gl, BF16), (gl, BF16), (gl, F32)):
            out_shapes.append(jax.ShapeDtypeStruct((n, width), dtype))
            out_specs.append(row(width))
    return pl.pallas_call(
        functools.partial(_proj_kernel, with_kmean=with_kmean, pos_blocks=pos_blocks),
        grid=(nt,),
        in_specs=[row(d), pl.BlockSpec((1, mod_r, d), mod_map), pl.BlockSpec((1, mod_r, d), mod_map),
                  const(g), const(w_in_bf), const(gains), const(bd),
                  pl.BlockSpec((tm, LANES), pos_map), pl.BlockSpec((tm, LANES), pos_map)],
        out_specs=out_specs,
        out_shape=out_shapes,
        compiler_params=_params("parallel"),
        name="proj",
    )(x, shift, scale, g, w_in_bf, gains, bd, cos, sin)


def _compress_compute(src_refs, pe_ref, w1_ref, w2_ref, gk_ref, bd_ref, kc_ref, vc_ref, n_rows):
    n_chunk = n_rows // CMP_STRIDE
    gw = G_NSA * HEAD_DIM
    for kv in range(2):
        acc_a = jnp.zeros((n_chunk, gw), F32)
        acc_b = jnp.zeros((n_chunk, gw), F32)
        for j in range(CMP_STRIDE):
            xj = src_refs[kv][pl.ds(j, n_chunk, stride=CMP_STRIDE), :]
            acc_a = acc_a + _dot(xj + pe_ref[kv, 0, j:j + 1, :], w1_ref[kv, 0, j], HIGHEST)
            acc_b = acc_b + _dot(xj + pe_ref[kv, 1, j:j + 1, :], w1_ref[kv, 1, j], HIGHEST)
        hid = jax.nn.gelu(acc_a + pltpu.roll(acc_b, n_chunk - 1, 0))
        out = _dot(hid, w2_ref[kv], HIGHEST)
        if kv == 0:
            kc_ref[0] = _head_norm(out, gk_ref[...], bd_ref[...])
        else:
            vc_ref[0] = out


def _compress_prompt_kernel(k_ref, v_ref, pe_ref, w1_ref, w2_ref, gk_ref, bd_ref, kc_ref, vc_ref, *, n_rows):
    _compress_compute((k_ref.at[0], v_ref.at[0]), pe_ref, w1_ref, w2_ref, gk_ref, bd_ref, kc_ref, vc_ref, n_rows)


def _compress_consts(cmp_pos, cmp_w1, cmp_w2, gain_k_cmp):
    pe = jnp.tile(cmp_pos.reshape(2, 2, CMP_STRIDE, HEAD_DIM), (1, 1, 1, G_NSA))
    eye = jnp.eye(G_NSA, dtype=F32)
    w1 = cmp_w1.reshape(2, 2, CMP_STRIDE, HEAD_DIM, HEAD_DIM)
    w1bd = jnp.einsum("gh,kajde->kajgdhe", eye, w1).reshape(2, 2, CMP_STRIDE, G_NSA * HEAD_DIM, G_NSA * HEAD_DIM)
    w2bd = jnp.einsum("gh,kde->kgdhe", eye, cmp_w2).reshape(2, G_NSA * HEAD_DIM, G_NSA * HEAD_DIM)
    gk = jnp.tile(gain_k_cmp.reshape(1, HEAD_DIM), (1, G_NSA))
    return pe, w1bd, w2bd, gk


def _compress_prompt(cmp_raw, consts, bd):
    b, s, _ = cmp_raw.shape
    pe, w1bd, w2bd, gk = consts
    n_chunk = s // CMP_STRIDE
    gw = G_NSA * HEAD_DIM
    const = lambda a: pl.BlockSpec(a.shape, lambda i: (0,) * a.ndim)
    out = jax.ShapeDtypeStruct((b, n_chunk, gw), F32)
    return pl.pallas_call(
        functools.partial(_compress_prompt_kernel, n_rows=s),
        grid=(b,),
        in_specs=[pl.BlockSpec((1, s, gw), lambda i: (i, 0, 0)), pl.BlockSpec((1, s, gw), lambda i: (i, 0, 1)),
                  const(pe), const(w1bd), const(w2bd), const(gk), const(bd)],
        out_specs=[pl.BlockSpec((1, n_chunk, gw), lambda i: (i, 0, 0))] * 2,
        out_shape=[out, out],
        compiler_params=_params("parallel"),
        name="compress_prompt",
    )(cmp_raw, cmp_raw, pe, w1bd, w2bd, gk, bd)


def _moba_kernel(q_ref, kmp_ref, k_ref, v_ref, o_ref):
    qi = pl.program_id(2)
    lane = _iota((MOBA_Q, LANES), 1)
    n_blk = MOBA_BLOCK // SUBLANES
    blk = _iota((n_blk, MOBA_Q), 0)
    own = (qi * MOBA_Q + _iota((n_blk, MOBA_Q), 1)) // MOBA_BLOCK
    valid = blk < own
    q_augs = []
    for h in range(MOBA_HEADS):
        q2 = q_ref[0, :, (h // 2) * LANES:(h // 2 + 1) * LANES]
        q0 = jnp.where(lane < HEAD_DIM, q2 if h % 2 == 0 else pltpu.roll(q2, HEAD_DIM, 1), 0.0)
        score = _dot_nt_x3(kmp_ref[0, h], q0)[HEAD_DIM:HEAD_DIM + n_blk, :]
        picked = _top_k_sublanes(jnp.where(valid, score, -jnp.inf), MOBA_TOPK)
        sel = ((picked > 0.5) & valid) | (blk == own)
        bias_t = jnp.concatenate([jnp.zeros((HEAD_DIM, MOBA_Q), F32), jnp.where(sel, 0.0, NEG_BIG),
                                  jnp.zeros((LANES - HEAD_DIM - n_blk, MOBA_Q), F32)], axis=0)
        q_augs.append(jnp.where(lane < HEAD_DIM, q0 * SCALE_LOG2E, bias_t.T).astype(BF16))
    pos = qi * MOBA_Q + _iota((MOBA_Q, ATT_TK), 0)

    def tile(j, carry, masked):
        start = pl.multiple_of(j * ATT_TK, ATT_TK)
        mask = None
        if masked:
            mask = (j * ATT_TK + _iota((MOBA_Q, ATT_TK), 1)) <= pos
        return tuple(_flash_step(q_augs[h], k_ref[0, pl.ds(start, ATT_TK), h * LANES:(h + 1) * LANES],
                                 v_ref[0, pl.ds(start, ATT_TK), h * LANES:(h + 1) * LANES], mask, *carry[h])
                     for h in range(MOBA_HEADS))

    init = (jnp.full((MOBA_Q, 1), NEG_BIG, F32), jnp.zeros((MOBA_Q, LANES), F32))
    jd = (qi * MOBA_Q) // ATT_TK
    carry = lax.fori_loop(0, jd, lambda j, c: tile(j, c, False), tile(jd, (init,) * MOBA_HEADS, True))
    for c in range(MOBA_HEADS // 2):
        o_ref[0, :, c * LANES:(c + 1) * LANES] = jnp.where(
            lane < HEAD_DIM, _flash_finish(carry[2 * c][1]),
            pltpu.roll(_flash_finish(carry[2 * c + 1][1]), HEAD_DIM, 1)).astype(o_ref.dtype)


def _moba_attention(q_m, kmp, k_aug, v_aug):
    b, s, _ = q_m.shape
    nh = MOBA_HEADS
    return pl.pallas_call(
        _moba_kernel,
        grid=(b, H_MOBA // nh, s // MOBA_Q),
        in_specs=[pl.BlockSpec((1, MOBA_Q, nh * HEAD_DIM), lambda b, h, i: (b, i, h)),
                  pl.BlockSpec((1, nh, LANES, LANES), lambda b, h, i: (b, h, 0, 0)),
                  pl.BlockSpec((1, s, nh * LANES), lambda b, h, i: (b, 0, h)),
                  pl.BlockSpec((1, s, nh * LANES), lambda b, h, i: (b, 0, h))],
        out_specs=pl.BlockSpec((1, MOBA_Q, nh * HEAD_DIM), lambda b, h, i: (b, i, h)),
        out_shape=jax.ShapeDtypeStruct((b, s, W_QM), BF16),
        compiler_params=_params("parallel", "parallel", "arbitrary"),
        name="moba_attention",
    )(q_m, kmp, k_aug, v_aug)


def _stack_heads(q4):
    lane = _iota((Q_TILE, LANES), 1)
    parts = []
    for r in range(R_NSA):
        c = q4[:, (r // 2) * LANES:(r // 2 + 1) * LANES]
        if r % 2:
            c = pltpu.roll(c, HEAD_DIM, 1)
        parts.append(jnp.where(lane < HEAD_DIM, c, 0.0))
    return jnp.concatenate(parts, axis=0)


def _nsa_kernel(qn_ref, qr_ref, gate_ref, kc_ref, vc_ref, cover_ref, ks_ref, vs_ref, kw_ref, vw_ref, o_ref,
                *, n_chunk):
    qi = pl.program_id(2)
    rows = R_NSA * Q_TILE
    qloc = _iota((rows, 1), 0) % Q_TILE
    pos = qi * Q_TILE + qloc

    qn = _stack_heads(qn_ref[0])
    s = _dot_nt_x3(qn, kc_ref[0, 0]) * SCALE
    cmask = (_iota((rows, n_chunk), 1) * CMP_STRIDE + (CMP_LEN - 1)) <= pos
    s = jnp.where(cmask, s, NEG_BIG)
    e = jnp.where(cmask, jnp.exp(s - jnp.max(s, axis=1, keepdims=True)), 0.0)
    p_cmp = e * (1.0 / jnp.maximum(jnp.sum(e, axis=1, keepdims=True), TINY))
    o_cmp = _dot(p_cmp.astype(BF16), vc_ref[0, 0])

    p_grp = p_cmp[0:Q_TILE]
    for r in range(1, R_NSA):
        p_grp = p_grp + p_cmp[r * Q_TILE:(r + 1) * Q_TILE]
    p_hi, p_lo = _split_bf16(p_grp)
    imp_t = _dot_nt(cover_ref[...], p_hi) + _dot_nt(cover_ref[...], p_lo)
    blk_t = _iota((LANES, Q_TILE), 0)
    own_t = (qi * Q_TILE + _iota((LANES, Q_TILE), 1)) // SEL_BLOCK
    forced = (blk_t == 0) | (blk_t == own_t) | (blk_t == own_t - 1)
    valid_t = blk_t <= own_t
    picked = _top_k_sublanes(jnp.where(valid_t & jnp.logical_not(forced), imp_t, -jnp.inf), SEL_TOPN - N_FORCED)
    bias = jnp.where(valid_t & (forced | (picked > 0.5)), 0.0, NEG_BIG).T
    blk = _iota((Q_TILE, LANES), 1)

    qr = (_stack_heads(qr_ref[0]) * SCALE_LOG2E).astype(BF16)
    q_aug = jnp.concatenate([jnp.concatenate([bias] * R_NSA, axis=0).astype(BF16), qr], axis=1)
    init = (jnp.full((rows, 1), NEG_BIG, F32), jnp.zeros((rows, LANES), F32))

    tk = ATT_TK
    jd = (qi * Q_TILE) // tk

    def sel_tile(j, carry, masked):
        start = pl.multiple_of(j * tk, tk)
        mask = None
        if masked:
            mask = (j * tk + _iota((rows, tk), 1)) <= pos
        return _flash_step(q_aug, ks_ref[0, pl.ds(start, tk), :], vs_ref[0, pl.ds(start, tk), :], mask, *carry)

    o_sel = _flash_finish(lax.fori_loop(0, jd, lambda j, c: sel_tile(j, c, False), sel_tile(jd, init, True))[1])

    span = WINDOW + Q_TILE
    w0 = pl.multiple_of(jnp.maximum(qi * Q_TILE - WINDOW, 0), Q_TILE)
    kpos = w0 + _iota((rows, span), 1)
    wmask = (kpos <= pos) & (kpos > pos - WINDOW)
    o_win = _flash_finish(_flash_step(qr, kw_ref[0, pl.ds(w0, span), :], vw_ref[0, pl.ds(w0, span), :], wmask,
                                      *init)[1])

    gates = gate_ref[0]
    heads = []
    for r in range(R_NSA):
        rs = slice(r * Q_TILE, (r + 1) * Q_TILE)
        heads.append(gates[:, 3 * r:3 * r + 1] * o_cmp[rs] + gates[:, 3 * r + 1:3 * r + 2] * o_sel[rs]
                     + gates[:, 3 * r + 2:3 * r + 3] * o_win[rs])
    lo = blk < HEAD_DIM
    o_ref[0] = jnp.concatenate([jnp.where(lo, heads[0], pltpu.roll(heads[1], HEAD_DIM, 1)),
                                jnp.where(lo, heads[2], pltpu.roll(heads[3], HEAD_DIM, 1))], axis=1).astype(o_ref.dtype)


def _nsa_attention(qn, qr, gates_g, kcp, vcd, cover, ks_aug, vs_dup, kw_pad, vw_dup):
    b, s, _ = qn.shape
    n_chunk = kcp.shape[2]
    gq = R_NSA * HEAD_DIM
    per_g = lambda w: pl.BlockSpec((1, s, w), lambda b, g, i: (b, 0, g))
    return pl.pallas_call(
        functools.partial(_nsa_kernel, n_chunk=n_chunk),
        grid=(b, G_NSA, s // Q_TILE),
        in_specs=[pl.BlockSpec((1, Q_TILE, gq), lambda b, g, i: (b, i, g)),
                  pl.BlockSpec((1, Q_TILE, gq), lambda b, g, i: (b, i, g)),
                  pl.BlockSpec((1, Q_TILE, LANES), lambda b, g, i: (b, i, g)),
                  pl.BlockSpec((1, 1, n_chunk, LANES), lambda b, g, i: (b, g, 0, 0)),
                  pl.BlockSpec((1, 1, n_chunk, LANES), lambda b, g, i: (b, g, 0, 0)),
                  pl.BlockSpec(cover.shape, lambda b, g, i: (0, 0)),
                  per_g(2 * LANES), per_g(LANES), per_g(LANES), per_g(LANES)],
        out_specs=pl.BlockSpec((1, Q_TILE, gq), lambda b, g, i: (b, i, g)),
        out_shape=jax.ShapeDtypeStruct((b, s, W_QN), BF16),
        compiler_params=_params("parallel", "parallel", "arbitrary"),
        name="nsa_attention",
    )(qn, qr, gates_g, kcp, vcd, cover, ks_aug, vs_dup, kw_pad, vw_dup)


def _cover(n_cmp, n_sel, rows):
    c0 = np.arange(rows)[:, None] * CMP_STRIDE
    b0 = np.arange(LANES)[None, :] * SEL_BLOCK
    ok = (c0 < b0 + SEL_BLOCK) & (c0 + CMP_LEN > b0) & (np.arange(rows)[:, None] < n_cmp) & (np.arange(LANES)[None, :] < n_sel)
    return jnp.asarray(ok.astype(np.float32))


def _lane_rep(col):
    return jnp.broadcast_to(col, (col.shape[0], LANES))


def _head_sums(prod_row):
    w = prod_row.shape[1]
    own = (_iota((SUBLANES, w), 1) // HEAD_DIM) == _iota((SUBLANES, w), 0)
    return _lane_rep(jnp.sum(jnp.where(own, jnp.broadcast_to(prod_row, (SUBLANES, w)), 0.0), axis=1, keepdims=True))


def _pair_row(x8, h):
    return jnp.where(_iota((1, LANES), 1) < HEAD_DIM, x8[h:h + 1, :], x8[h + 1:h + 2, :])


def _cols_to_row(acc_a, acc_b):
    return jnp.sum(jnp.concatenate([acc_a, acc_b], axis=0).T, axis=0, keepdims=True)


def _dec_kernel(pt_ref, cm_ref, cn_ref, wb_ref, q_ref, q8n_ref, new_ref, gate_ref, wcat_ref, pecat_ref, w2_ref,
                 gk_ref, bd_ref, cover_ref, o_ref,
                 mbuf, nbuf, sem, xk, xv, qmb, qrb, s_sc, p_sc, *, n_pages, page):
    b = pl.program_id(0)
    n_req = pl.num_programs(0)
    slot = b % 2
    hw = H_MOBA * HEAD_DIM
    gw = G_NSA * HEAD_DIM
    past = n_pages * page

    def copies(req, sl):
        out = []
        for p in range(n_pages):
            pg = pt_ref[req * n_pages + p]
            out.append(pltpu.make_async_copy(cm_ref.at[pg], mbuf.at[sl, p], sem.at[0, sl]))
            out.append(pltpu.make_async_copy(cn_ref.at[pg], nbuf.at[sl, p], sem.at[1, sl]))
        return out

    @pl.when(b == 0)
    def _():
        for c in copies(0, 0):
            c.start()

    for c in copies(b, slot):
        c.wait()

    @pl.when(b + 1 < n_req)
    def _():
        for c in copies(b + 1, 1 - slot):
            c.start()

    lane1 = _iota((1, LANES), 1)
    lane8 = _iota((SUBLANES, LANES), 1)
    qrow = q_ref[0]
    new = new_ref[0]
    for c in range(hw // LANES):
        cols = slice(c * LANES, (c + 1) * LANES)
        qmb[cols, :] = jnp.broadcast_to(qrow[0:1, cols] * SCALE, (LANES, LANES)).T
        qrb[cols, :] = jnp.broadcast_to(qrow[1:2, cols] * SCALE, (LANES, LANES)).T

    def softmax_pv(scores, s_new8, v_rows, vbuf_ref, v_row0, per_g):
        m8 = s_new8
        for s in scores:
            m8 = jnp.maximum(m8, _lane_rep(jnp.max(s, axis=1, keepdims=True)))
        w_new = jnp.exp(s_new8 - m8)
        l8 = w_new
        for p, s in enumerate(scores):
            pr = jnp.exp(s - m8)
            p_sc[p] = pr
            l8 = l8 + _lane_rep(jnp.sum(pr, axis=1, keepdims=True))
        inv8 = 1.0 / jnp.maximum(l8, TINY)
        rows = []
        for hp in range(SUBLANES // 2):
            accs = []
            for h in (2 * hp, 2 * hp + 1):
                r0 = v_row0 + (h // R_NSA if per_g else h) * HEAD_DIM

                def body(p, acc, h=h, r0=r0):
                    return acc + vbuf_ref[slot, p, r0:r0 + HEAD_DIM, :] * p_sc[p, h:h + 1, :]
                acc = jnp.zeros((HEAD_DIM, LANES), F32)
                for p in range(n_pages):
                    acc = body(p, acc)
                accs.append(acc)
            row = _cols_to_row(accs[0], accs[1])
            cols = slice(hp * LANES, (hp + 1) * LANES)
            rows.append((row + _pair_row(w_new, 2 * hp) * v_rows[:, cols]) * _pair_row(inv8, 2 * hp))
        return rows

    def moba_scores(p, carry):
        rows = [jnp.sum(mbuf[slot, p, h * HEAD_DIM:(h + 1) * HEAD_DIM, :] * qmb[h * HEAD_DIM:(h + 1) * HEAD_DIM, :],
                        axis=0, keepdims=True) for h in range(H_MOBA)]
        s_sc[p] = jnp.concatenate(rows, axis=0)
        return carry
    for p in range(n_pages):
        moba_scores(p, 0)
    s_all = [s_sc[p] for p in range(n_pages)]
    ppb = MOBA_BLOCK // page
    n_blk = n_pages // ppb
    bsc = []
    for j in range(n_blk):
        tot = s_all[j * ppb]
        for t in range(1, ppb):
            tot = tot + s_all[j * ppb + t]
        bsc.append(_lane_rep(jnp.sum(tot, axis=1, keepdims=True)))
    masked = []
    for j in range(n_blk):
        rank = jnp.zeros((SUBLANES, LANES), F32)
        for c in range(n_blk):
            if c != j:
                ahead = (bsc[c] > bsc[j]) | ((bsc[c] == bsc[j]) & (c < j))
                rank = rank + jnp.where(ahead, 1.0, 0.0)
        for t in range(ppb):
            masked.append(jnp.where(rank < MOBA_TOPK, s_all[j * ppb + t], NEG_BIG))
    s_own = _head_sums(qrow[0:1, :] * new[0:1, :]) * SCALE
    o_rows = softmax_pv(masked, s_own, new[1:2, :], mbuf, hw, False)

    for p in range(n_pages):
        xk[p * page:(p + 1) * page, :] = nbuf[slot, p, 0:gw, :].T
        xv[p * page:(p + 1) * page, :] = nbuf[slot, p, gw:2 * gw, :].T
    n_chunk = past // CMP_STRIDE
    cmp_out = []
    for kv, xref in enumerate((xk, xv)):
        xcat = jnp.concatenate([xref[pl.ds(j, n_chunk, stride=CMP_STRIDE), :] for j in range(CMP_STRIDE)], axis=1)
        pe2 = _dot(pecat_ref[kv].astype(BF16), wcat_ref[kv])
        ab = _dot(xcat.astype(BF16), wcat_ref[kv])
        hid = jax.nn.gelu(ab[:, 0:gw] + pe2[0:1, 0:gw] + pltpu.roll(ab[:, gw:2 * gw] + pe2[1:2, gw:2 * gw],
                                                                    n_chunk - 1, 0))
        cmp_out.append(_dot_x3(hid, w2_ref[kv]))
    kc = _head_norm(cmp_out[0], gk_ref[...], bd_ref[...])
    vc = cmp_out[1]

    s = _dot_nt_x3(q8n_ref[0], kc) * SCALE
    cmask = _iota((SUBLANES, n_chunk), 1) < (n_chunk - 1)
    s = jnp.where(cmask, s, NEG_BIG)
    e = jnp.where(cmask, jnp.exp(s - jnp.max(s, axis=1, keepdims=True)), 0.0)
    p_cmp = e / jnp.maximum(jnp.sum(e, axis=1, keepdims=True), TINY)
    o_cmp8 = _dot(p_cmp.astype(BF16), vc.astype(BF16))
    subc = _iota((SUBLANES, n_chunk), 0)
    g0 = jnp.sum(jnp.where(subc < R_NSA, p_cmp, 0.0), axis=0, keepdims=True)
    g1 = jnp.sum(jnp.where(subc >= R_NSA, p_cmp, 0.0), axis=0, keepdims=True)
    p_grp = jnp.concatenate([jnp.where(subc < R_NSA, g0, g1), jnp.zeros((LANES - SUBLANES, n_chunk), F32)], axis=0)
    own = past // SEL_BLOCK
    n_sel = -(-(own + 1) // SUBLANES) * SUBLANES
    p_hi, p_lo = _split_bf16(p_grp)
    imp_t = (_dot_nt(cover_ref[...], p_hi) + _dot_nt(cover_ref[...], p_lo))[0:n_sel]
    blk_t = _iota((n_sel, LANES), 0)
    forced = (blk_t == 0) | (blk_t == own) | (blk_t == own - 1)
    score = jnp.where(blk_t <= own, jnp.where(forced, jnp.inf, imp_t), -jnp.inf)
    picked = _top_k_sublanes(score, SEL_TOPN)
    bias_t = jnp.concatenate([jnp.where((picked > 0.5) & (blk_t <= own), 0.0, NEG_BIG),
                              jnp.full((LANES - n_sel, LANES), NEG_BIG, F32)], axis=0)
    bias8 = bias_t.T[0:SUBLANES, :]
    cmp_rows = []
    for hp in range(H_NSA // 2):
        g = (2 * hp) // R_NSA
        ra = o_cmp8[2 * hp:2 * hp + 1, :]
        rb = o_cmp8[2 * hp + 1:2 * hp + 2, :]
        cmp_rows.append(jnp.where(lane1 < HEAD_DIM, ra if g == 0 else pltpu.roll(ra, HEAD_DIM, 1),
                                  rb if g == 1 else pltpu.roll(rb, HEAD_DIM, 1)))

    def sel_scores(p, carry):
        rows = [jnp.sum(nbuf[slot, p, 2 * gw + (i // R_NSA) * HEAD_DIM:2 * gw + (i // R_NSA + 1) * HEAD_DIM, :]
                        * qrb[i * HEAD_DIM:(i + 1) * HEAD_DIM, :], axis=0, keepdims=True) for i in range(H_NSA)]
        s_sc[p] = jnp.concatenate(rows, axis=0)
        return carry
    for p in range(n_pages):
        sel_scores(p, 0)
    bpp = page // SEL_BLOCK
    sel_s = []
    for p in range(n_pages):
        bias_p = bias8[:, p * bpp:p * bpp + 1]
        for t in range(1, bpp):
            bias_p = jnp.where(lane8 < t * SEL_BLOCK, bias_p, bias8[:, p * bpp + t:p * bpp + t + 1])
        sel_s.append(s_sc[p] + bias_p)
    s_new = _head_sums(qrow[1:2, :] * new[2:3, :]) * SCALE
    sel_rows = softmax_pv(sel_s, s_new, new[3:4, :], nbuf, 3 * gw, True)

    nw = wb_ref.shape[2]
    wk = nw // LANES
    w_s = []
    for i in range(H_NSA):
        g = i // R_NSA
        qcol = jnp.concatenate([qrb[i * HEAD_DIM:(i + 1) * HEAD_DIM, :]] * wk, axis=1)
        w_s.append(jnp.sum(wb_ref[0, g * HEAD_DIM:(g + 1) * HEAD_DIM, :] * qcol, axis=0, keepdims=True))
    s = jnp.concatenate(w_s, axis=0)
    wmask = _iota((SUBLANES, nw), 1) >= 1
    s = jnp.where(wmask, s, NEG_BIG)
    s_new = _head_sums(qrow[1:2, :] * new[4:5, :]) * SCALE
    m8 = jnp.maximum(_lane_rep(jnp.max(s, axis=1, keepdims=True)), s_new)
    pw = jnp.where(wmask, jnp.exp(s - m8[:, 0:1]), 0.0)
    w_new = jnp.exp(s_new - m8)
    inv8 = 1.0 / jnp.maximum(_lane_rep(jnp.sum(pw, axis=1, keepdims=True)) + w_new, TINY)
    win_rows = []
    for hp in range(H_NSA // 2):
        accs = []
        for i in (2 * hp, 2 * hp + 1):
            g = i // R_NSA
            prod = wb_ref[0, gw + g * HEAD_DIM:gw + (g + 1) * HEAD_DIM, :] * pw[i:i + 1, :]
            acc = prod[:, 0:LANES]
            for c in range(1, wk):
                acc = acc + prod[:, c * LANES:(c + 1) * LANES]
            accs.append(acc)
        cols = slice(hp * LANES, (hp + 1) * LANES)
        win_rows.append((_cols_to_row(accs[0], accs[1]) + _pair_row(w_new, 2 * hp) * new[5:6, cols])
                        * _pair_row(inv8, 2 * hp))

    gates = gate_ref[0]
    for c in range(hw // LANES):
        o_ref[0, :, c * LANES:(c + 1) * LANES] = o_rows[c]
    for c in range(W_QN // LANES):
        cols = slice(c * LANES, (c + 1) * LANES)
        o_ref[0, :, hw + c * LANES:hw + (c + 1) * LANES] = (
            gates[0:1, cols] * cmp_rows[c] + gates[1:2, cols] * sel_rows[c] + gates[2:3, cols] * win_rows[c])


def _dec_attention(cm, cn, wb, pt_flat, qrows, q8n, new, gate_rows, wcat, pecat, w2bd, gk, bd, cover,
                    *, n_req, n_pages):
    page = cm.shape[2]
    n_chunk = n_pages * page // CMP_STRIDE
    req = lambda a: pl.BlockSpec((1,) + a.shape[1:], lambda b, pt: (b,) + (0,) * (a.ndim - 1))
    const = lambda a: pl.BlockSpec(a.shape, lambda b, pt: (0,) * a.ndim)
    return pl.pallas_call(
        functools.partial(_dec_kernel, n_pages=n_pages, page=page),
        grid_spec=pltpu.PrefetchScalarGridSpec(
            num_scalar_prefetch=1,
            grid=(n_req,),
            in_specs=[pl.BlockSpec(memory_space=pl.ANY), pl.BlockSpec(memory_space=pl.ANY), req(wb), req(qrows),
                      req(q8n), req(new), req(gate_rows), const(wcat), const(pecat), const(w2bd), const(gk),
                      const(bd), const(cover)],
            out_specs=pl.BlockSpec((1, 1, W_QM + W_QN), lambda b, pt: (b, 0, 0)),
            scratch_shapes=[pltpu.VMEM((2, n_pages) + cm.shape[1:], F32), pltpu.VMEM((2, n_pages) + cn.shape[1:], F32),
                            pltpu.SemaphoreType.DMA((2, 2)),
                            pltpu.VMEM((n_pages * page, LANES), F32), pltpu.VMEM((n_pages * page, LANES), F32),
                            pltpu.VMEM((W_QM, LANES), F32), pltpu.VMEM((W_QN, LANES), F32),
                            pltpu.VMEM((n_pages, SUBLANES, LANES), F32), pltpu.VMEM((n_pages, SUBLANES, LANES), F32)]),
        out_shape=jax.ShapeDtypeStruct((n_req, 1, W_QM + W_QN), F32),
        compiler_params=_params("arbitrary"),
        name="dec_attention",
    )(pt_flat, cm, cn, wb, qrows, q8n, new, gate_rows, wcat, pecat, w2bd, gk, bd, cover)


def _compress_cat_consts(cmp_pos, cmp_w1):
    eye = jnp.eye(G_NSA, dtype=F32)
    w1 = cmp_w1.reshape(2, 2, CMP_STRIDE, HEAD_DIM, HEAD_DIM)
    wcat = jnp.einsum("gh,kajde->kjgdahe", eye, w1).reshape(2, CMP_STRIDE * G_NSA * HEAD_DIM, 2 * G_NSA * HEAD_DIM)
    pe = jnp.tile(cmp_pos.reshape(2, 2, CMP_STRIDE, 1, HEAD_DIM), (1, 1, 1, G_NSA, 1))
    pecat = jnp.pad(pe.reshape(2, 2, CMP_STRIDE * G_NSA * HEAD_DIM), ((0, 0), (0, SUBLANES - 2), (0, 0)))
    return wcat.astype(BF16), pecat


def _post_kernel(o_ref, x_ref, gt_ref, sh_ref, sc_ref, g_ref, wo_ref, wr_ref, br_ref,
                 y_ref, h3_ref, te_ref, tw_ref):
    y = x_ref[...] + gt_ref[0] * _dot(o_ref[...], wo_ref[...])
    y_ref[...] = y
    h = y * lax.rsqrt(jnp.mean(y * y, axis=1, keepdims=True) + NORM_EPS) * g_ref[...]
    h = h * (1.0 + sc_ref[0]) + sh_ref[0]
    tm = h.shape[0]
    for s in range(h.shape[1] // LANES):
        h3_ref[pl.ds(s, tm, stride=SUBLANES), :] = h[:, s * LANES:(s + 1) * LANES]
    h_hi, h_lo = _split_bf16(h)
    w_hi, w_lo = _split_bf16(wr_ref[...])
    logits = _dot(h_hi, w_hi) + _dot(h_lo, w_hi) + _dot(h_hi, w_lo) + br_ref[...]
    _, vals, ids = _top_k_lanes(logits, TOP_K)
    lane = _iota((tm, LANES), 1)
    es = [jnp.exp(v - vals[0]) for v in vals]
    den = es[0]
    for e in es[1:]:
        den = den + e
    te = jnp.zeros((tm, LANES), F32)
    tw = jnp.zeros((tm, LANES), F32)
    for k in range(TOP_K):
        te = jnp.where(lane == k, ids[k], te)
        tw = jnp.where(lane == k, es[k] / den, tw)
    te_ref[...] = te.astype(jnp.int32)
    tw_ref[...] = tw


def _post(o, x, gate, shift, scale, g2, wo_bf, wr_pad, br_pad, *, tm, rows_per_mod):
    n, d = x.shape
    mod_r = gate.shape[1]
    mod_map = lambda i: (i // (rows_per_mod // tm), 0, 0)
    row = lambda w: pl.BlockSpec((tm, w), lambda i: (i, 0))
    const = lambda a: pl.BlockSpec(a.shape, lambda i: (0,) * a.ndim)
    mod = pl.BlockSpec((1, mod_r, d), mod_map)
    return pl.pallas_call(
        _post_kernel,
        grid=(n // tm,),
        in_specs=[row(d), row(d), mod, mod, mod, const(g2), const(wo_bf), const(wr_pad), const(br_pad)],
        out_specs=[row(d), pl.BlockSpec((tm * SUBLANES, LANES), lambda i: (i, 0)), row(LANES), row(LANES)],
        out_shape=[jax.ShapeDtypeStruct((n, d), F32), jax.ShapeDtypeStruct((n * SUBLANES, LANES), F32),
                   jax.ShapeDtypeStruct((n, LANES), jnp.int32), jax.ShapeDtypeStruct((n, LANES), F32)],
        compiler_params=_params("parallel"),
        name="post",
    )(o, x, gate, shift, scale, g2, wo_bf, wr_pad, br_pad)


def _dispatch_kernel(pos_ref, h3_ref, xs_in_ref, xs_ref, sem, *, tm, tile_off):
    del xs_in_ref
    i = pl.program_id(0)

    def body(r, carry):
        src = h3_ref.at[pl.ds(pl.multiple_of(r * SUBLANES, SUBLANES), SUBLANES), :]
        for k in range(TOP_K):
            p = pos_ref[((i + tile_off) * tm + r) * TOP_K + k]
            pltpu.make_async_copy(src, xs_ref.at[pl.ds(pl.multiple_of(p * SUBLANES, SUBLANES), SUBLANES), :],
                                  sem.at[0]).start(priority=k % 2)
        return carry
    lax.fori_loop(0, tm, body, 0, unroll=COPY_UNROLL)
    for k in range(TOP_K):
        pltpu.make_async_copy(h3_ref, xs_ref.at[pl.ds(0, tm * SUBLANES), :], sem.at[0]).wait()


def _dispatch(pos_flat, h3, xs, *, tm, tile_off):
    n8 = h3.shape[0]
    return pl.pallas_call(
        functools.partial(_dispatch_kernel, tm=tm, tile_off=tile_off),
        grid_spec=pltpu.PrefetchScalarGridSpec(
            num_scalar_prefetch=1,
            grid=(n8 // (tm * SUBLANES),),
            in_specs=[pl.BlockSpec((tm * SUBLANES, LANES), lambda i, pos: (i, 0)), pl.BlockSpec(memory_space=pl.ANY)],
            out_specs=pl.BlockSpec(memory_space=pl.ANY),
            scratch_shapes=[pltpu.SemaphoreType.DMA((1,))]),
        out_shape=jax.ShapeDtypeStruct(xs.shape, xs.dtype),
        input_output_aliases={2: 0},
        compiler_params=_params("arbitrary"),
        name="dispatch",
    )(pos_flat, h3, xs)


def _expert_block_kernel(be_ref, na_ref, x_ref, wgu_ref, bgu_ref, wd_ref, bd_ref, y_ref, xb, wgu_bf, wd_bf):
    i = pl.program_id(0)
    active = i < na_ref[0]

    @pl.when(active & ((i == 0) | (be_ref[i] != be_ref[jnp.maximum(i - 1, 0)])))
    def _():
        for r in range(0, wgu_bf.shape[0], LANES):
            wgu_bf[r:r + LANES, :] = wgu_ref[0, r:r + LANES, :].astype(BF16)
        for r in range(0, wd_bf.shape[0], LANES):
            wd_bf[r:r + LANES, :] = wd_ref[0, r:r + LANES, :].astype(BF16)

    @pl.when(active)
    def _():
        d = xb.shape[1]
        for s in range(d // LANES):
            xb[:, s * LANES:(s + 1) * LANES] = x_ref[pl.ds(s, MOE_ROWS, stride=SUBLANES), :].astype(BF16)
        gu = _dot(xb[...], wgu_bf[...]) + bgu_ref[0]
        f = gu.shape[1] // 2
        gt = jnp.minimum(gu[:, 0:f], SWIGLU_LIMIT)
        up = jnp.clip(gu[:, f:2 * f], -SWIGLU_LIMIT, SWIGLU_LIMIT)
        act = (up + 1.0) * (gt * jax.nn.sigmoid(SWIGLU_ALPHA * gt))
        y = _dot(act.astype(BF16), wd_bf[...]) + bd_ref[0]
        for s in range(d // LANES):
            y_ref[pl.ds(s, MOE_ROWS, stride=SUBLANES), :] = y[:, s * LANES:(s + 1) * LANES]

    @pl.when(i >= na_ref[0])
    def _():
        y_ref[...] = jnp.zeros(y_ref.shape, F32)


def _expert_blocks(blk_e, n_active, xs, w_gu, b_gu, w_down, b_down):
    n_blocks = blk_e.shape[0]
    e, d, f2 = w_gu.shape
    rows8 = MOE_ROWS * SUBLANES
    row_map = lambda i, be, na: (jnp.minimum(i, na[0] - 1), 0)
    return pl.pallas_call(
        _expert_block_kernel,
        grid_spec=pltpu.PrefetchScalarGridSpec(
            num_scalar_prefetch=2,
            grid=(n_blocks,),
            in_specs=[pl.BlockSpec((rows8, LANES), row_map),
                      pl.BlockSpec((1, d, f2), lambda i, be, na: (be[i], 0, 0)),
                      pl.BlockSpec((1, 1, f2), lambda i, be, na: (be[i], 0, 0)),
                      pl.BlockSpec((1, f2 // 2, d), lambda i, be, na: (be[i], 0, 0)),
                      pl.BlockSpec((1, 1, d), lambda i, be, na: (be[i], 0, 0))],
            out_specs=pl.BlockSpec((rows8, LANES), lambda i, be, na: (i, 0)),
            scratch_shapes=[pltpu.VMEM((MOE_ROWS, d), BF16), pltpu.VMEM((d, f2), BF16),
                            pltpu.VMEM((f2 // 2, d), BF16)]),
        out_shape=jax.ShapeDtypeStruct((n_blocks * rows8, LANES), F32),
        compiler_params=_params("arbitrary"),
        name="experts",
    )(blk_e, n_active, xs, w_gu, b_gu.reshape(e, 1, f2), w_down, b_down.reshape(e, 1, d))


def _routing_pos(top_e, n_tok):
    n_assign = n_tok * TOP_K
    n_pad = -(-n_assign // LANES) * LANES
    e_flat = jnp.pad(top_e.reshape(-1), (0, n_pad - n_assign), constant_values=N_EXPERTS)
    onehot = (e_flat[:, None] == jnp.arange(N_EXPERTS)[None, :]).astype(F32).reshape(n_pad // LANES, LANES, N_EXPERTS)
    tril = jnp.tril(jnp.ones((LANES, LANES), F32))
    within = jnp.einsum("ij,bjk->bik", tril, onehot)
    block_tot = within[:, -1, :]
    offs = jnp.cumsum(block_tot, axis=0) - block_tot
    counts = jnp.sum(block_tot, axis=0).astype(jnp.int32)
    padded = (counts + MOE_ROWS - 1) // MOE_ROWS * MOE_ROWS
    pad_end = jnp.cumsum(padded)
    pad_start = (pad_end - padded).astype(F32)
    slot = jnp.sum(onehot * (within + offs[:, None, :] - 1.0 + pad_start[None, None, :]), axis=-1)
    pos = slot.reshape(-1)[:n_assign].astype(jnp.int32)
    n_blocks = -(-n_assign // MOE_ROWS) + N_EXPERTS
    starts = jnp.arange(n_blocks, dtype=jnp.int32) * MOE_ROWS
    blk_e = jnp.minimum(jnp.sum((pad_end[None, :] <= starts[:, None]).astype(jnp.int32), axis=1), N_EXPERTS - 1)
    n_active = (pad_end[-1] // MOE_ROWS).astype(jnp.int32).reshape(1)
    return blk_e, n_active, pos


def _combine_kernel(pos_ref, ys_ref, y1_ref, gt_ref, tw_ref, o_ref, buf, sem, *, tile_off):
    i = pl.program_id(0)
    n = pl.num_programs(0)
    tm = y1_ref.shape[0]
    rows8 = tm * SUBLANES

    def gather(tile, slot):
        def body(r, carry):
            for k in range(TOP_K):
                p = pos_ref[((tile + tile_off) * tm + r) * TOP_K + k]
                pltpu.make_async_copy(ys_ref.at[pl.ds(pl.multiple_of(p * SUBLANES, SUBLANES), SUBLANES), :],
                                      buf.at[slot, k, pl.ds(pl.multiple_of(r * SUBLANES, SUBLANES), SUBLANES), :],
                                      sem.at[slot]).start(priority=k % 2)
            return carry
        lax.fori_loop(0, tm, body, 0, unroll=COPY_UNROLL)

    @pl.when(i == 0)
    def _():
        gather(0, 0)

    slot = i % 2
    for k in range(TOP_K):
        pltpu.make_async_copy(ys_ref.at[pl.ds(0, rows8), :], buf.at[slot, k], sem.at[slot]).wait()

    @pl.when(i + 1 < n)
    def _():
        gather(i + 1, 1 - slot)

    tw = tw_ref[...]
    wk = [jnp.broadcast_to(tw[:, k:k + 1], (tm, LANES)) for k in range(TOP_K)]
    gt = gt_ref[0]
    for s in range(o_ref.shape[1] // LANES):
        moe = wk[0] * buf[slot, 0, pl.ds(s, tm, stride=SUBLANES), :]
        for k in range(1, TOP_K):
            moe = moe + wk[k] * buf[slot, k, pl.ds(s, tm, stride=SUBLANES), :]
        cols = slice(s * LANES, (s + 1) * LANES)
        o_ref[:, cols] = y1_ref[:, cols] + gt[:, cols] * moe


def _combine(pos_flat, ys, y1, gate, tw, *, tm, rows_per_mod, tile_off):
    n, d = y1.shape
    mod_r = gate.shape[1]
    return pl.pallas_call(
        functools.partial(_combine_kernel, tile_off=tile_off),
        grid_spec=pltpu.PrefetchScalarGridSpec(
            num_scalar_prefetch=1,
            grid=(n // tm,),
            in_specs=[pl.BlockSpec(memory_space=pl.ANY),
                      pl.BlockSpec((tm, d), lambda i, pos: (i, 0)),
                      pl.BlockSpec((1, mod_r, d), lambda i, pos: (i // (rows_per_mod // tm), 0, 0)),
                      pl.BlockSpec((tm, LANES), lambda i, pos: (i, 0))],
            out_specs=pl.BlockSpec((tm, d), lambda i, pos: (i, 0)),
            scratch_shapes=[pltpu.VMEM((2, TOP_K, tm * SUBLANES, LANES), F32), pltpu.SemaphoreType.DMA((2,))]),
        out_shape=jax.ShapeDtypeStruct((n, d), F32),
        compiler_params=_params("arbitrary"),
        name="combine",
    )(pos_flat, ys, y1, gate, tw)


def _rope_tables(pos):
    half = HEAD_DIM // 2
    inv = ROPE_THETA ** (-jnp.arange(half, dtype=F32) / half)
    ang = pos.astype(F32)[:, None] * inv[None, :]
    cos = jnp.cos(ang)
    sin = jnp.sin(ang)
    reps = LANES // HEAD_DIM
    return (jnp.tile(jnp.concatenate([cos, cos], axis=1), (1, reps)),
            jnp.tile(jnp.concatenate([-sin, sin], axis=1), (1, reps)))


def _head_pad(a):
    return jnp.concatenate([a, jnp.zeros_like(a)], axis=-1)


def kernel(x_prompt, x_sample, c_prompt, c_sample, cache_moba_kv, cache_nsa_kv, state_nsa_win_kv, page_table,
           norm_g, w_ada, b_ada, w_in, qk_gain, cmp_pos, cmp_w1, cmp_w2, w_out, w_router, b_router, w_gu, b_gu,
           w_down, b_down):
    bsz, seq, d = x_prompt.shape
    n_req = x_sample.shape[0]
    depth = norm_g.shape[0]
    assert depth == 1 and x_sample.shape[1] == 1
    assert seq % ATT_TK == 0 and seq >= WINDOW + Q_TILE and seq // SEL_BLOCK <= LANES and seq // MOBA_BLOCK <= MOBA_BLOCK // SUBLANES
    n_pool, page = cache_moba_kv.shape[1], cache_moba_kv.shape[2]
    n_pages = page_table.shape[1]
    past = n_pages * page
    assert past % MOBA_BLOCK == 0 and 2 * page == MOBA_BLOCK and past // SEL_BLOCK < LANES
    assert state_nsa_win_kv.shape[2] == WINDOW
    layer = 0
    gw = G_NSA * HEAD_DIM
    n_prompt = bsz * seq

    bd = jnp.asarray(np.kron(np.eye(LANES // HEAD_DIM), np.full((HEAD_DIM, HEAD_DIM), 1.0 / HEAD_DIM)), BF16)
    w_in_bf = jnp.pad(w_in[layer], ((0, 0), (0, IN_COLS_PAD - IN_COLS))).astype(BF16)
    gains = jnp.tile(qk_gain[layer], (1, W_QM // HEAD_DIM))
    g1 = norm_g[layer, 0].reshape(1, d)
    g2 = norm_g[layer, 1].reshape(1, d)
    wo_bf = w_out[layer].astype(BF16)
    wr_pad = jnp.pad(w_router[layer], ((0, 0), (0, LANES - N_EXPERTS)))
    br_pad = jnp.pad(b_router[layer].reshape(1, N_EXPERTS), ((0, 0), (0, LANES - N_EXPERTS)),
                     constant_values=-jnp.inf)
    cmp_consts = _compress_consts(cmp_pos[layer], cmp_w1[layer], cmp_w2[layer], qk_gain[layer, 3])

    n_c = bsz + n_req
    n_c_pad = -(-n_c // SUBLANES) * SUBLANES
    c_all = jnp.pad(jnp.concatenate([c_prompt, c_sample], axis=0), ((0, n_c_pad - n_c), (0, 0)))
    mods = _ada(c_all, w_ada[layer], b_ada[layer])
    mods_p = [m.reshape(bsz, 1, d) for m in jnp.split(mods[:bsz], 6, axis=1)]
    mods_s = [m.reshape(1, n_req, d) for m in jnp.split(mods[bsz:n_c], 6, axis=1)]

    cos_p, sin_p = _rope_tables(jnp.arange(seq, dtype=jnp.int32))
    tm_p = 256
    (moba_t, nsa_t, win_t, q_m, qn, qr, _, kmean, cmp_raw, k_aug, v_aug, ks_aug, vs_aug, kw_pad, vw_aug,
     gates_g) = _proj(
        x_prompt.reshape(n_prompt, d), mods_p[0], mods_p[1], g1, w_in_bf, gains, bd, cos_p, sin_p,
        tm=tm_p, rows_per_mod=seq, pos_blocks=seq // tm_p, with_kmean=True)

    nbk = seq // MOBA_BLOCK
    kmean_h = kmean.reshape(bsz, nbk, H_MOBA, HEAD_DIM).transpose(0, 2, 1, 3)
    kmp = jnp.zeros((bsz, H_MOBA, LANES, LANES), F32).at[:, :, HEAD_DIM:HEAD_DIM + nbk, :HEAD_DIM].set(kmean_h)
    per_b = lambda a: a.reshape(bsz, seq, a.shape[-1])
    o_m = _moba_attention(per_b(q_m), kmp, per_b(k_aug), per_b(v_aug))

    kc, vc = _compress_prompt(per_b(cmp_raw), cmp_consts, bd)
    n_chunk = seq // CMP_STRIDE
    per_g = lambda a: a.reshape(bsz, -1, G_NSA, HEAD_DIM).transpose(0, 2, 1, 3)
    kcp = _head_pad(per_g(kc))
    vcd = _head_pad(per_g(vc)).astype(BF16)
    cover_p = _cover(n_chunk - 1, seq // SEL_BLOCK, n_chunk).T.astype(BF16)
    o_n = _nsa_attention(per_b(qn), per_b(qr), per_b(gates_g), kcp, vcd, cover_p, per_b(ks_aug), per_b(vs_aug),
                         per_b(kw_pad), per_b(vw_aug))
    o_p = jnp.concatenate([o_m, o_n], axis=-1).reshape(n_prompt, d)

    y1_p, h3_p, te_p, tw_p = _post(o_p, x_prompt.reshape(n_prompt, d), mods_p[2], mods_p[3], mods_p[4], g2, wo_bf,
                                   wr_pad, br_pad, tm=256, rows_per_mod=seq)

    cos_s, sin_s = _rope_tables(jnp.full((n_req,), past, jnp.int32))
    moba_new, nsa_new, win_new, q_m_s, qn_s, qr_s, gates_s = _proj(
        x_sample.reshape(n_req, d), mods_s[0], mods_s[1], g1, w_in_bf, gains, bd, cos_s, sin_s,
        tm=n_req, rows_per_mod=n_req, pos_blocks=1, with_kmean=False)
    pt_flat = page_table.reshape(-1).astype(jnp.int32)
    cache_m = cache_moba_kv[layer].transpose(0, 2, 3, 4, 1).reshape(n_pool, W_KVM, page)
    cache_n = cache_nsa_kv[layer].transpose(0, 2, 3, 4, 1).reshape(n_pool, 4 * gw, page)
    win_buf = state_nsa_win_kv[layer].transpose(0, 2, 3, 4, 1).reshape(n_req, 2 * gw, WINDOW)

    def rows8(q):
        qh = q.reshape(n_req, G_NSA, R_NSA, 1, HEAD_DIM)
        place = jnp.arange(G_NSA)[None, :, None, None, None] == jnp.arange(G_NSA)[None, None, None, :, None]
        return jnp.where(place, qh, 0.0).reshape(n_req, H_NSA, gw)

    def per_head(a):
        return jnp.repeat(a.reshape(n_req, G_NSA, HEAD_DIM), R_NSA, axis=1).reshape(n_req, W_QN)

    new_rows = jnp.stack([moba_new[:, :W_QM], moba_new[:, W_QM:], per_head(nsa_new[:, 2 * gw:3 * gw]),
                          per_head(nsa_new[:, 3 * gw:]), per_head(win_new[:, :gw]), per_head(win_new[:, gw:])], axis=1)
    gate_rows = jnp.repeat(gates_s[:, :N_GATE].reshape(n_req, H_NSA, 3).transpose(0, 2, 1), HEAD_DIM, axis=2)
    n_cmp_s = past // CMP_STRIDE
    cover_s = _cover(n_cmp_s - 1, past // SEL_BLOCK + 1, n_cmp_s).T.astype(BF16)
    wcat, pecat = _compress_cat_consts(cmp_pos[layer], cmp_w1[layer])
    o_s = _dec_attention(cache_m, cache_n, win_buf, pt_flat, jnp.stack([q_m_s, qr_s], axis=1), rows8(qn_s), new_rows,
                          gate_rows, wcat, pecat, cmp_consts[2], cmp_consts[3], bd, cover_s,
                          n_req=n_req, n_pages=n_pages).reshape(n_req, d).astype(BF16)
    y1_s, h3_s, te_s, tw_s = _post(o_s, x_sample.reshape(n_req, d), mods_s[2], mods_s[3], mods_s[4], g2, wo_bf,
                                   wr_pad, br_pad, tm=n_req, rows_per_mod=n_req)

    n_tok = n_prompt + n_req
    top_e = jnp.concatenate([te_p[:, :TOP_K], te_s[:, :TOP_K]], axis=0)
    blk_e, n_active, pos_flat = _routing_pos(top_e, n_tok)
    tm_c = 128
    xs = jnp.zeros((blk_e.shape[0] * MOE_ROWS * SUBLANES, LANES), F32)
    xs = _dispatch(pos_flat, h3_p, xs, tm=DISPATCH_TM, tile_off=0)
    xs = _dispatch(pos_flat, h3_s, xs, tm=n_req, tile_off=n_prompt // n_req)
    ys = _expert_blocks(blk_e, n_active, xs, w_gu[layer], b_gu[layer], w_down[layer], b_down[layer])
    y_p = _combine(pos_flat, ys, y1_p, mods_p[5], tw_p, tm=tm_c, rows_per_mod=seq, tile_off=0)
    y_s = _combine(pos_flat, ys, y1_s, mods_s[5], tw_s, tm=n_req, rows_per_mod=n_req, tile_off=n_prompt // n_req)

    def rows_view(t, n_slot, n_head):
        return t.reshape(1, bsz, n_slot, n_head, HEAD_DIM, t.shape[-1]).transpose(0, 1, 5, 2, 3, 4)

    keep = min(WINDOW, seq)
    win_s = jnp.concatenate([state_nsa_win_kv[layer][:, 1:], win_new.reshape(n_req, 1, 2, G_NSA, HEAD_DIM)], axis=1)
    return (y_p.reshape(bsz, seq, d), y_s.reshape(n_req, 1, d),
            rows_view(moba_t, 2, H_MOBA), rows_view(nsa_t, 4, G_NSA), rows_view(win_t[:, :, seq - keep:], 2, G_NSA),
            moba_new.reshape(1, n_req, 1, 2, H_MOBA, HEAD_DIM),
            nsa_new.reshape(1, n_req, 1, 4, G_NSA, HEAD_DIM), win_s[None])
```

```python
import functools

import numpy as np
import jax
import jax.numpy as jnp
from jax import lax
from jax.experimental import pallas as pl
from jax.experimental.pallas import tpu as pltpu

F32 = jnp.float32
BF16 = jnp.bfloat16
HIGHEST = lax.Precision.HIGHEST

LANES = 128
SUBLANES = 8
HEAD_DIM = 64
H_MOBA = 8
H_NSA = 8
G_NSA = 2
R_NSA = H_NSA // G_NSA
MOBA_BLOCK = 256
MOBA_TOPK = 3
CMP_LEN = 32
CMP_STRIDE = 16
SEL_BLOCK = 64
SEL_TOPN = 16
N_FORCED = 3
WINDOW = 512
N_EXPERTS = 32
TOP_K = 4
SWIGLU_LIMIT = 7.0
SWIGLU_ALPHA = 1.702
ROPE_THETA = 10000.0
NORM_EPS = 1e-6
NEG_BIG = -1e30
TINY = 1e-30
SCALE = HEAD_DIM ** -0.5
SCALE_LOG2E = SCALE * 1.4426950408889634
Q_TILE = 256
MOBA_Q = 1024
MOBA_HEADS = 2
ATT_TK = 1024
MOE_ROWS = 256
COPY_UNROLL = 8
ROW_TM = 512
DISPATCH_TM = 1024
VMEM_LIMIT = 56 * 1024 * 1024

W_QM = H_MOBA * HEAD_DIM
W_KVM = 2 * H_MOBA * HEAD_DIM
W_QN = H_NSA * HEAD_DIM
W_KVN = 6 * G_NSA * HEAD_DIM
N_GATE = 3 * H_NSA
IN_COLS = W_QM + W_KVM + W_QN + W_KVN + N_GATE
IN_COLS_PAD = W_QM + W_KVM + W_QN + W_KVN + LANES


def _iota(shape, dim):
    return lax.broadcasted_iota(jnp.int32, shape, dim)


def _dot(a, b, precision=None):
    return jnp.dot(a, b, preferred_element_type=F32, precision=precision)


def _dot_nt(a, b, precision=None):
    return lax.dot_general(a, b, (((1,), (1,)), ((), ())), preferred_element_type=F32, precision=precision)


def _split_bf16(a):
    hi = a.astype(BF16)
    return hi, (a - hi.astype(F32)).astype(BF16)


def _dot_nt_x3(a, b):
    ah, al = _split_bf16(a)
    bh, bl = _split_bf16(b)
    return _dot_nt(ah, bh) + _dot_nt(al, bh) + _dot_nt(ah, bl)


def _dot_x3(a, b):
    ah, al = _split_bf16(a)
    bh, bl = _split_bf16(b)
    return _dot(ah, bh) + _dot(al, bh) + _dot(ah, bl)


def _params(*sem):
    return pltpu.CompilerParams(dimension_semantics=sem, vmem_limit_bytes=VMEM_LIMIT)


def _seg_meansq(z, bd):
    zz = z * z
    hi = zz.astype(BF16)
    lo = (zz - hi.astype(F32)).astype(BF16)
    outs = []
    for c in range(z.shape[1] // LANES):
        sl = slice(c * LANES, (c + 1) * LANES)
        outs.append(_dot(hi[:, sl], bd) + _dot(lo[:, sl], bd))
    return outs[0] if len(outs) == 1 else jnp.concatenate(outs, axis=1)


def _head_norm(z, gain, bd):
    return z * lax.rsqrt(_seg_meansq(z, bd) + NORM_EPS) * gain


def _rope(z, cos, sin):
    outs = []
    first = (_iota((z.shape[0], LANES), 1) % HEAD_DIM) < (HEAD_DIM // 2)
    for c in range(z.shape[1] // LANES):
        x = z[:, c * LANES:(c + 1) * LANES]
        swapped = jnp.where(first, pltpu.roll(x, LANES - HEAD_DIM // 2, 1), pltpu.roll(x, HEAD_DIM // 2, 1))
        outs.append(x * cos + swapped * sin)
    return outs[0] if len(outs) == 1 else jnp.concatenate(outs, axis=1)


def _top_k_lanes(cur, k):
    lane = _iota(cur.shape, 1).astype(F32)
    picked = jnp.zeros(cur.shape, F32)
    vals, ids = [], []
    for _ in range(k):
        mx = jnp.max(cur, axis=1, keepdims=True)
        first = jnp.min(jnp.where(cur == mx, lane, 1e9), axis=1, keepdims=True)
        hit = lane == first
        picked = jnp.where(hit, 1.0, picked)
        cur = jnp.where(hit, -jnp.inf, cur)
        vals.append(mx)
        ids.append(first)
    return picked, vals, ids


def _top_k_sublanes(cur, k):
    idx = _iota(cur.shape, 0).astype(F32)
    picked = jnp.zeros(cur.shape, F32)
    for _ in range(k):
        mx = jnp.max(cur, axis=0, keepdims=True)
        first = jnp.min(jnp.where(cur == mx, idx, 1e9), axis=0, keepdims=True)
        hit = idx == first
        picked = jnp.where(hit, 1.0, picked)
        cur = jnp.where(hit, -jnp.inf, cur)
    return picked


def _flash_step(q, k, v, mask, m, acc):
    s = _dot_nt(q, k)
    if mask is not None:
        s = jnp.where(mask, s, NEG_BIG)
    m_new = jnp.maximum(m, jnp.max(s, axis=1, keepdims=True))
    acc_new = jnp.exp2(m - m_new) * acc + _dot(jnp.exp2(s - m_new).astype(BF16), v)
    return m_new, acc_new


def _flash_finish(acc):
    return acc / jnp.maximum(acc[:, HEAD_DIM:HEAD_DIM + 1], TINY)


def _ada_kernel(c_ref, w_ref, b_ref, o_ref):
    c = c_ref[...]
    o_ref[...] = _dot(c * jax.nn.sigmoid(c), w_ref[...], HIGHEST) + b_ref[...]


def _ada(c_all, w_ada, b_ada):
    n, d = c_all.shape
    cols = w_ada.shape[1]
    tn = 1024
    return pl.pallas_call(
        _ada_kernel,
        grid=(cols // tn,),
        in_specs=[pl.BlockSpec((n, d), lambda j: (0, 0)),
                  pl.BlockSpec((d, tn), lambda j: (0, j)),
                  pl.BlockSpec((1, tn), lambda j: (0, j))],
        out_specs=pl.BlockSpec((n, tn), lambda j: (0, j)),
        out_shape=jax.ShapeDtypeStruct((n, cols), F32),
        compiler_params=_params("arbitrary"),
        name="ada",
    )(c_all, w_ada, b_ada.reshape(1, cols))


def _proj_kernel(x_ref, sh_ref, sc_ref, g_ref, w_ref, gains_ref, bd_ref, cos_ref, sin_ref,
                 moba_ref, nsa_ref, win_ref, qm_ref, qn_ref, qr_ref, gate_ref, *attn_refs, with_kmean, pos_blocks):
    x = x_ref[...]
    y = x * lax.rsqrt(jnp.mean(x * x, axis=1, keepdims=True) + NORM_EPS) * g_ref[...]
    h = (y * (1.0 + sc_ref[0]) + sh_ref[0]).astype(BF16)
    bd = bd_ref[...]
    cos = cos_ref[...]
    sin = sin_ref[...]
    o = 0

    def seg(width):
        nonlocal o
        z = _dot(h, w_ref[:, o:o + width])
        o += width
        return z

    def gain(i, width):
        return gains_ref[i:i + 1, 0:width]

    qm_ref[...] = _rope(_head_norm(seg(W_QM), gain(0, W_QM), bd), cos, sin)
    k_m = _rope(_head_norm(seg(W_QM), gain(1, W_QM), bd), cos, sin)
    v_m = seg(W_QM)
    qn = _head_norm(seg(W_QN), gain(2, W_QN), bd)
    qn_ref[...] = qn
    qr_ref[...] = _rope(qn, cos, sin)
    gw = G_NSA * HEAD_DIM
    cmp_raw = seg(2 * gw)
    k_sel = _rope(_head_norm(seg(gw), gain(4, gw), bd), cos, sin)
    v_sel = seg(gw)
    k_win = _rope(_head_norm(seg(gw), gain(5, gw), bd), cos, sin)
    v_win = seg(gw)
    gates = jax.nn.sigmoid(seg(LANES))
    gate_ref[...] = gates
    moba_rows = jnp.concatenate([k_m, v_m], axis=1)
    nsa_rows = jnp.concatenate([cmp_raw, k_sel, v_sel], axis=1)
    win_rows = jnp.concatenate([k_win, v_win], axis=1)
    if not with_kmean:
        moba_ref[...] = moba_rows
        nsa_ref[...] = nsa_rows
        win_ref[...] = win_rows
    else:
        kmean_ref, cmpraw_ref, kaug_ref, vaug_ref, ksaug_ref, vsaug_ref, kwp_ref, vwaug_ref, gg_ref = attn_refs
        moba_ref[0] = moba_rows.T
        nsa_ref[0] = nsa_rows.T
        win_ref[0] = win_rows.T
        cmpraw_ref[...] = cmp_raw
        tm = k_m.shape[0]
        kmean_ref[0] = jnp.mean(k_m.reshape(tm // MOBA_BLOCK, MOBA_BLOCK, W_QM), axis=1)
        lane = _iota((tm, LANES), 1)
        posv = (pl.program_id(0) % pos_blocks) * tm + _iota((tm, LANES), 0)
        lo = lane < HEAD_DIM
        pad_blk = jnp.where((lane >= HEAD_DIM) & (lane - HEAD_DIM == posv // MOBA_BLOCK), 1.0, 0.0)
        pad_one = jnp.where(lane == HEAD_DIM, 1.0, 0.0)
        oh_sel = jnp.where(lane == posv // SEL_BLOCK, 1.0, 0.0).astype(BF16)

        def halves(x):
            return x, pltpu.roll(x, HEAD_DIM, 1)

        for c in range(W_QM // LANES):
            cols = slice(c * LANES, (c + 1) * LANES)
            for hh, (kh, vh) in enumerate(zip(halves(k_m[:, cols]), halves(v_m[:, cols]))):
                hcols = slice((2 * c + hh) * LANES, (2 * c + hh + 1) * LANES)
                kaug_ref[:, hcols] = jnp.where(lo, kh, pad_blk).astype(BF16)
                vaug_ref[:, hcols] = jnp.where(lo, vh, pad_one).astype(BF16)
        for g, (ks, vs, kw, vw) in enumerate(zip(halves(k_sel), halves(v_sel), halves(k_win), halves(v_win))):
            ksaug_ref[:, 2 * g * LANES:(2 * g + 1) * LANES] = oh_sel
            ksaug_ref[:, (2 * g + 1) * LANES:(2 * g + 2) * LANES] = jnp.where(lo, ks, 0.0).astype(BF16)
            gcols = slice(g * LANES, (g + 1) * LANES)
            vsaug_ref[:, gcols] = jnp.where(lo, vs, pad_one).astype(BF16)
            kwp_ref[:, gcols] = jnp.where(lo, kw, 0.0).astype(BF16)
            vwaug_ref[:, gcols] = jnp.where(lo, vw, pad_one).astype(BF16)
            gg_ref[:, gcols] = gates if g == 0 else pltpu.roll(gates, LANES - g * 3 * R_NSA, 1)


def _proj(x, shift, scale, g, w_in_bf, gains, bd, cos, sin, *, tm, rows_per_mod, pos_blocks, with_kmean):
    n, d = x.shape
    nt = n // tm
    mod_r = shift.shape[1]
    mod_map = lambda i: (i // (rows_per_mod // tm), 0, 0)
    pos_map = lambda i: (i % pos_blocks, 0)
    row = lambda w: pl.BlockSpec((tm, w), lambda i: (i, 0))
    const = lambda a: pl.BlockSpec(a.shape, lambda i: (0,) * a.ndim)
    gw = G_NSA * HEAD_DIM
    cache_widths = (W_KVM, 4 * gw, 2 * gw)
    if with_kmean:
        nb = n // rows_per_mod
        tpb = rows_per_mod // tm
        out_shapes = [jax.ShapeDtypeStruct((nb, w, rows_per_mod), F32) for w in cache_widths]
        out_specs = [pl.BlockSpec((1, w, tm), lambda i: (i // tpb, 0, i % tpb)) for w in cache_widths]
    else:
        out_shapes = [jax.ShapeDtypeStruct((n, w), F32) for w in cache_widths]
        out_specs = [row(w) for w in cache_widths]
    out_shapes += [jax.ShapeDtypeStruct((n, W_QM), F32), jax.ShapeDtypeStruct((n, W_QN), F32),
                   jax.ShapeDtypeStruct((n, W_QN), F32), jax.ShapeDtypeStruct((n, LANES), F32)]
    out_specs += [row(W_QM), row(W_QN), row(W_QN), row(LANES)]
    if with_kmean:
        nbt = tm // MOBA_BLOCK
        out_shapes.append(jax.ShapeDtypeStruct((nt, nbt, W_QM), F32))
        out_specs.append(pl.BlockSpec((1, nbt, W_QM), lambda i: (i, 0, 0)))
        gl = G_NSA * LANES
        for width, dtype in ((2 * gw, F32), (H_MOBA * LANES, BF16), (H_MOBA * LANES, BF16), (2 * gl, BF16), (gl, BF16),
                             (gl, BF16), (gl, BF16), (gl, F32)):
            out_shapes.append(jax.ShapeDtypeStruct((n, width), dtype))
            out_specs.append(row(width))
    return pl.pallas_call(
        functools.partial(_proj_kernel, with_kmean=with_kmean, pos_blocks=pos_blocks),
        grid=(nt,),
        in_specs=[row(d), pl.BlockSpec((1, mod_r, d), mod_map), pl.BlockSpec((1, mod_r, d), mod_map),
                  const(g), const(w_in_bf), const(gains), const(bd),
                  pl.BlockSpec((tm, LANES), pos_map), pl.BlockSpec((tm, LANES), pos_map)],
        out_specs=out_specs,
        out_shape=out_shapes,
        compiler_params=_params("parallel"),
        name="proj",
    )(x, shift, scale, g, w_in_bf, gains, bd, cos, sin)


def _compress_compute(src_refs, pe_ref, w1_ref, w2_ref, gk_ref, bd_ref, kc_ref, vc_ref, n_rows):
    n_chunk = n_rows // CMP_STRIDE
    gw = G_NSA * HEAD_DIM
    for kv in range(2):
        acc_a = jnp.zeros((n_chunk, gw), F32)
        acc_b = jnp.zeros((n_chunk, gw), F32)
        for j in range(CMP_STRIDE):
            xj = src_refs[kv][pl.ds(j, n_chunk, stride=CMP_STRIDE), :]
            acc_a = acc_a + _dot(xj + pe_ref[kv, 0, j:j + 1, :], w1_ref[kv, 0, j], HIGHEST)
            acc_b = acc_b + _dot(xj + pe_ref[kv, 1, j:j + 1, :], w1_ref[kv, 1, j], HIGHEST)
        hid = jax.nn.gelu(acc_a + pltpu.roll(acc_b, n_chunk - 1, 0))
        out = _dot(hid, w2_ref[kv], HIGHEST)
        if kv == 0:
            kc_ref[0] = _head_norm(out, gk_ref[...], bd_ref[...])
        else:
            vc_ref[0] = out


def _compress_prompt_kernel(k_ref, v_ref, pe_ref, w1_ref, w2_ref, gk_ref, bd_ref, kc_ref, vc_ref, *, n_rows):
    _compress_compute((k_ref.at[0], v_ref.at[0]), pe_ref, w1_ref, w2_ref, gk_ref, bd_ref, kc_ref, vc_ref, n_rows)


def _compress_consts(cmp_pos, cmp_w1, cmp_w2, gain_k_cmp):
    pe = jnp.tile(cmp_pos.reshape(2, 2, CMP_STRIDE, HEAD_DIM), (1, 1, 1, G_NSA))
    eye = jnp.eye(G_NSA, dtype=F32)
    w1 = cmp_w1.reshape(2, 2, CMP_STRIDE, HEAD_DIM, HEAD_DIM)
    w1bd = jnp.einsum("gh,kajde->kajgdhe", eye, w1).reshape(2, 2, CMP_STRIDE, G_NSA * HEAD_DIM, G_NSA * HEAD_DIM)
    w2bd = jnp.einsum("gh,kde->kgdhe", eye, cmp_w2).reshape(2, G_NSA * HEAD_DIM, G_NSA * HEAD_DIM)
    gk = jnp.tile(gain_k_cmp.reshape(1, HEAD_DIM), (1, G_NSA))
    return pe, w1bd, w2bd, gk


def _compress_prompt(cmp_raw, consts, bd):
    b, s, _ = cmp_raw.shape
    pe, w1bd, w2bd, gk = consts
    n_chunk = s // CMP_STRIDE
    gw = G_NSA * HEAD_DIM
    const = lambda a: pl.BlockSpec(a.shape, lambda i: (0,) * a.ndim)
    out = jax.ShapeDtypeStruct((b, n_chunk, gw), F32)
    return pl.pallas_call(
        functools.partial(_compress_prompt_kernel, n_rows=s),
        grid=(b,),
        in_specs=[pl.BlockSpec((1, s, gw), lambda i: (i, 0, 0)), pl.BlockSpec((1, s, gw), lambda i: (i, 0, 1)),
                  const(pe), const(w1bd), const(w2bd), const(gk), const(bd)],
        out_specs=[pl.BlockSpec((1, n_chunk, gw), lambda i: (i, 0, 0))] * 2,
        out_shape=[out, out],
        compiler_params=_params("parallel"),
        name="compress_prompt",
    )(cmp_raw, cmp_raw, pe, w1bd, w2bd, gk, bd)


def _moba_kernel(q_ref, kmp_ref, k_ref, v_ref, o_ref):
    qi = pl.program_id(2)
    lane = _iota((MOBA_Q, LANES), 1)
    n_blk = MOBA_BLOCK // SUBLANES
    blk = _iota((n_blk, MOBA_Q), 0)
    own = (qi * MOBA_Q + _iota((n_blk, MOBA_Q), 1)) // MOBA_BLOCK
    valid = blk < own
    q_augs = []
    for h in range(MOBA_HEADS):
        q2 = q_ref[0, :, (h // 2) * LANES:(h // 2 + 1) * LANES]
        q0 = jnp.where(lane < HEAD_DIM, q2 if h % 2 == 0 else pltpu.roll(q2, HEAD_DIM, 1), 0.0)
        score = _dot_nt_x3(kmp_ref[0, h], q0)[HEAD_DIM:HEAD_DIM + n_blk, :]
        picked = _top_k_sublanes(jnp.where(valid, score, -jnp.inf), MOBA_TOPK)
        sel = ((picked > 0.5) & valid) | (blk == own)
        bias_t = jnp.concatenate([jnp.zeros((HEAD_DIM, MOBA_Q), F32), jnp.where(sel, 0.0, NEG_BIG),
                                  jnp.zeros((LANES - HEAD_DIM - n_blk, MOBA_Q), F32)], axis=0)
        q_augs.append(jnp.where(lane < HEAD_DIM, q0 * SCALE_LOG2E, bias_t.T).astype(BF16))
    pos = qi * MOBA_Q + _iota((MOBA_Q, ATT_TK), 0)

    def tile(j, carry, masked):
        start = pl.multiple_of(j * ATT_TK, ATT_TK)
        mask = None
        if masked:
            mask = (j * ATT_TK + _iota((MOBA_Q, ATT_TK), 1)) <= pos
        return tuple(_flash_step(q_augs[h], k_ref[0, pl.ds(start, ATT_TK), h * LANES:(h + 1) * LANES],
                                 v_ref[0, pl.ds(start, ATT_TK), h * LANES:(h + 1) * LANES], mask, *carry[h])
                     for h in range(MOBA_HEADS))

    init = (jnp.full((MOBA_Q, 1), NEG_BIG, F32), jnp.zeros((MOBA_Q, LANES), F32))
    jd = (qi * MOBA_Q) // ATT_TK
    carry = lax.fori_loop(0, jd, lambda j, c: tile(j, c, False), tile(jd, (init,) * MOBA_HEADS, True))
    for c in range(MOBA_HEADS // 2):
        o_ref[0, :, c * LANES:(c + 1) * LANES] = jnp.where(
            lane < HEAD_DIM, _flash_finish(carry[2 * c][1]),
            pltpu.roll(_flash_finish(carry[2 * c + 1][1]), HEAD_DIM, 1)).astype(o_ref.dtype)


def _moba_attention(q_m, kmp, k_aug, v_aug):
    b, s, _ = q_m.shape
    nh = MOBA_HEADS
    return pl.pallas_call(
        _moba_kernel,
        grid=(b, H_MOBA // nh, s // MOBA_Q),
        in_specs=[pl.BlockSpec((1, MOBA_Q, nh * HEAD_DIM), lambda b, h, i: (b, i, h)),
                  pl.BlockSpec((1, nh, LANES, LANES), lambda b, h, i: (b, h, 0, 0)),
                  pl.BlockSpec((1, s, nh * LANES), lambda b, h, i: (b, 0, h)),
                  pl.BlockSpec((1, s, nh * LANES), lambda b, h, i: (b, 0, h))],
        out_specs=pl.BlockSpec((1, MOBA_Q, nh * HEAD_DIM), lambda b, h, i: (b, i, h)),
        out_shape=jax.ShapeDtypeStruct((b, s, W_QM), BF16),
        compiler_params=_params("parallel", "parallel", "arbitrary"),
        name="moba_attention",
    )(q_m, kmp, k_aug, v_aug)


def _stack_heads(q4):
    lane = _iota((Q_TILE, LANES), 1)
    parts = []
    for r in range(R_NSA):
        c = q4[:, (r // 2) * LANES:(r // 2 + 1) * LANES]
        if r % 2:
            c = pltpu.roll(c, HEAD_DIM, 1)
        parts.append(jnp.where(lane < HEAD_DIM, c, 0.0))
    return jnp.concatenate(parts, axis=0)


def _nsa_kernel(qn_ref, qr_ref, gate_ref, kc_ref, vc_ref, cover_ref, ks_ref, vs_ref, kw_ref, vw_ref, o_ref,
                *, n_chunk):
    qi = pl.program_id(2)
    rows = R_NSA * Q_TILE
    qloc = _iota((rows, 1), 0) % Q_TILE
    pos = qi * Q_TILE + qloc

    qn = _stack_heads(qn_ref[0])
    s = _dot_nt_x3(qn, kc_ref[0, 0]) * SCALE
    cmask = (_iota((rows, n_chunk), 1) * CMP_STRIDE + (CMP_LEN - 1)) <= pos
    s = jnp.where(cmask, s, NEG_BIG)
    e = jnp.where(cmask, jnp.exp(s - jnp.max(s, axis=1, keepdims=True)), 0.0)
    p_cmp = e * (1.0 / jnp.maximum(jnp.sum(e, axis=1, keepdims=True), TINY))
    o_cmp = _dot(p_cmp.astype(BF16), vc_ref[0, 0])

    p_grp = p_cmp[0:Q_TILE]
    for r in range(1, R_NSA):
        p_grp = p_grp + p_cmp[r * Q_TILE:(r + 1) * Q_TILE]
    p_hi, p_lo = _split_bf16(p_grp)
    imp_t = _dot_nt(cover_ref[...], p_hi) + _dot_nt(cover_ref[...], p_lo)
    blk_t = _iota((LANES, Q_TILE), 0)
    own_t = (qi * Q_TILE + _iota((LANES, Q_TILE), 1)) // SEL_BLOCK
    forced = (blk_t == 0) | (blk_t == own_t) | (blk_t == own_t - 1)
    valid_t = blk_t <= own_t
    picked = _top_k_sublanes(jnp.where(valid_t & jnp.logical_not(forced), imp_t, -jnp.inf), SEL_TOPN - N_FORCED)
    bias = jnp.where(valid_t & (forced | (picked > 0.5)), 0.0, NEG_BIG).T
    blk = _iota((Q_TILE, LANES), 1)

    qr = (_stack_heads(qr_ref[0]) * SCALE_LOG2E).astype(BF16)
    q_aug = jnp.concatenate([jnp.concatenate([bias] * R_NSA, axis=0).astype(BF16), qr], axis=1)
    init = (jnp.full((rows, 1), NEG_BIG, F32), jnp.zeros((rows, LANES), F32))

    tk = ATT_TK
    jd = (qi * Q_TILE) // tk

    def sel_tile(j, carry, masked):
        start = pl.multiple_of(j * tk, tk)
        mask = None
        if masked:
            mask = (j * tk + _iota((rows, tk), 1)) <= pos
        return _flash_step(q_aug, ks_ref[0, pl.ds(start, tk), :], vs_ref[0, pl.ds(start, tk), :], mask, *carry)

    o_sel = _flash_finish(lax.fori_loop(0, jd, lambda j, c: sel_tile(j, c, False), sel_tile(jd, init, True))[1])

    span = WINDOW + Q_TILE
    w0 = pl.multiple_of(jnp.maximum(qi * Q_TILE - WINDOW, 0), Q_TILE)
    kpos = w0 + _iota((rows, span), 1)
    wmask = (kpos <= pos) & (kpos > pos - WINDOW)
    o_win = _flash_finish(_flash_step(qr, kw_ref[0, pl.ds(w0, span), :], vw_ref[0, pl.ds(w0, span), :], wmask,
                                      *init)[1])

    gates = gate_ref[0]
    heads = []
    for r in range(R_NSA):
        rs = slice(r * Q_TILE, (r + 1) * Q_TILE)
        heads.append(gates[:, 3 * r:3 * r + 1] * o_cmp[rs] + gates[:, 3 * r + 1:3 * r + 2] * o_sel[rs]
                     + gates[:, 3 * r + 2:3 * r + 3] * o_win[rs])
    lo = blk < HEAD_DIM
    o_ref[0] = jnp.concatenate([jnp.where(lo, heads[0], pltpu.roll(heads[1], HEAD_DIM, 1)),
                                jnp.where(lo, heads[2], pltpu.roll(heads[3], HEAD_DIM, 1))], axis=1).astype(o_ref.dtype)


def _nsa_attention(qn, qr, gates_g, kcp, vcd, cover, ks_aug, vs_dup, kw_pad, vw_dup):
    b, s, _ = qn.shape
    n_chunk = kcp.shape[2]
    gq = R_NSA * HEAD_DIM
    per_g = lambda w: pl.BlockSpec((1, s, w), lambda b, g, i: (b, 0, g))
    return pl.pallas_call(
        functools.partial(_nsa_kernel, n_chunk=n_chunk),
        grid=(b, G_NSA, s // Q_TILE),
        in_specs=[pl.BlockSpec((1, Q_TILE, gq), lambda b, g, i: (b, i, g)),
                  pl.BlockSpec((1, Q_TILE, gq), lambda b, g, i: (b, i, g)),
                  pl.BlockSpec((1, Q_TILE, LANES), lambda b, g, i: (b, i, g)),
                  pl.BlockSpec((1, 1, n_chunk, LANES), lambda b, g, i: (b, g, 0, 0)),
                  pl.BlockSpec((1, 1, n_chunk, LANES), lambda b, g, i: (b, g, 0, 0)),
                  pl.BlockSpec(cover.shape, lambda b, g, i: (0, 0)),
                  per_g(2 * LANES), per_g(LANES), per_g(LANES), per_g(LANES)],
        out_specs=pl.BlockSpec((1, Q_TILE, gq), lambda b, g, i: (b, i, g)),
        out_shape=jax.ShapeDtypeStruct((b, s, W_QN), BF16),
        compiler_params=_params("parallel", "parallel", "arbitrary"),
        name="nsa_attention",
    )(qn, qr, gates_g, kcp, vcd, cover, ks_aug, vs_dup, kw_pad, vw_dup)


def _cover(n_cmp, n_sel, rows):
    c0 = np.arange(rows)[:, None] * CMP_STRIDE
    b0 = np.arange(LANES)[None, :] * SEL_BLOCK
    ok = (c0 < b0 + SEL_BLOCK) & (c0 + CMP_LEN > b0) & (np.arange(rows)[:, None] < n_cmp) & (np.arange(LANES)[None, :] < n_sel)
    return jnp.asarray(ok.astype(np.float32))


def _lane_rep(col):
    return jnp.broadcast_to(col, (col.shape[0], LANES))


def _head_sums(prod_row):
    w = prod_row.shape[1]
    own = (_iota((SUBLANES, w), 1) // HEAD_DIM) == _iota((SUBLANES, w), 0)
    return _lane_rep(jnp.sum(jnp.where(own, jnp.broadcast_to(prod_row, (SUBLANES, w)), 0.0), axis=1, keepdims=True))


def _pair_row(x8, h):
    return jnp.where(_iota((1, LANES), 1) < HEAD_DIM, x8[h:h + 1, :], x8[h + 1:h + 2, :])


def _cols_to_row(acc_a, acc_b):
    return jnp.sum(jnp.concatenate([acc_a, acc_b], axis=0).T, axis=0, keepdims=True)


def _dec_kernel(pt_ref, cm_ref, cn_ref, wb_ref, q_ref, q8n_ref, new_ref, gate_ref, wcat_ref, pecat_ref, w2_ref,
                 gk_ref, bd_ref, cover_ref, o_ref,
                 mbuf, nbuf, sem, xk, xv, qmb, qrb, s_sc, p_sc, *, n_pages, page):
    b = pl.program_id(0)
    n_req = pl.num_programs(0)
    slot = b % 2
    hw = H_MOBA * HEAD_DIM
    gw = G_NSA * HEAD_DIM
    past = n_pages * page

    def copies(req, sl):
        out = []
        for p in range(n_pages):
            pg = pt_ref[req * n_pages + p]
            out.append(pltpu.make_async_copy(cm_ref.at[pg], mbuf.at[sl, p], sem.at[0, sl]))
            out.append(pltpu.make_async_copy(cn_ref.at[pg], nbuf.at[sl, p], sem.at[1, sl]))
        return out

    @pl.when(b == 0)
    def _():
        for c in copies(0, 0):
            c.start()

    for c in copies(b, slot):
        c.wait()

    @pl.when(b + 1 < n_req)
    def _():
        for c in copies(b + 1, 1 - slot):
            c.start()

    lane1 = _iota((1, LANES), 1)
    lane8 = _iota((SUBLANES, LANES), 1)
    qrow = q_ref[0]
    new = new_ref[0]
    for c in range(hw // LANES):
        cols = slice(c * LANES, (c + 1) * LANES)
        qmb[cols, :] = jnp.broadcast_to(qrow[0:1, cols] * SCALE, (LANES, LANES)).T
        qrb[cols, :] = jnp.broadcast_to(qrow[1:2, cols] * SCALE, (LANES, LANES)).T

    def softmax_pv(scores, s_new8, v_rows, vbuf_ref, v_row0, per_g):
        m8 = s_new8
        for s in scores:
            m8 = jnp.maximum(m8, _lane_rep(jnp.max(s, axis=1, keepdims=True)))
        w_new = jnp.exp(s_new8 - m8)
        l8 = w_new
        for p, s in enumerate(scores):
            pr = jnp.exp(s - m8)
            p_sc[p] = pr
            l8 = l8 + _lane_rep(jnp.sum(pr, axis=1, keepdims=True))
        inv8 = 1.0 / jnp.maximum(l8, TINY)
        rows = []
        for hp in range(SUBLANES // 2):
            accs = []
            for h in (2 * hp, 2 * hp + 1):
                r0 = v_row0 + (h // R_NSA if per_g else h) * HEAD_DIM

                def body(p, acc, h=h, r0=r0):
                    return acc + vbuf_ref[slot, p, r0:r0 + HEAD_DIM, :] * p_sc[p, h:h + 1, :]
                acc = jnp.zeros((HEAD_DIM, LANES), F32)
                for p in range(n_pages):
                    acc = body(p, acc)
                accs.append(acc)
            row = _cols_to_row(accs[0], accs[1])
            cols = slice(hp * LANES, (hp + 1) * LANES)
            rows.append((row + _pair_row(w_new, 2 * hp) * v_rows[:, cols]) * _pair_row(inv8, 2 * hp))
        return rows

    def moba_scores(p, carry):
        rows = [jnp.sum(mbuf[slot, p, h * HEAD_DIM:(h + 1) * HEAD_DIM, :] * qmb[h * HEAD_DIM:(h + 1) * HEAD_DIM, :],
                        axis=0, keepdims=True) for h in range(H_MOBA)]
        s_sc[p] = jnp.concatenate(rows, axis=0)
        return carry
    for p in range(n_pages):
        moba_scores(p, 0)
    s_all = [s_sc[p] for p in range(n_pages)]
    ppb = MOBA_BLOCK // page
    n_blk = n_pages // ppb
    bsc = []
    for j in range(n_blk):
        tot = s_all[j * ppb]
        for t in range(1, ppb):
            tot = tot + s_all[j * ppb + t]
        bsc.append(_lane_rep(jnp.sum(tot, axis=1, keepdims=True)))
    masked = []
    for j in range(n_blk):
        rank = jnp.zeros((SUBLANES, LANES), F32)
        for c in range(n_blk):
            if c != j:
                ahead = (bsc[c] > bsc[j]) | ((bsc[c] == bsc[j]) & (c < j))
                rank = rank + jnp.where(ahead, 1.0, 0.0)
        for t in range(ppb):
            masked.append(jnp.where(rank < MOBA_TOPK, s_all[j * ppb + t], NEG_BIG))
    s_own = _head_sums(qrow[0:1, :] * new[0:1, :]) * SCALE
    o_rows = softmax_pv(masked, s_own, new[1:2, :], mbuf, hw, False)

    for p in range(n_pages):
        xk[p * page:(p + 1) * page, :] = nbuf[slot, p, 0:gw, :].T
        xv[p * page:(p + 1) * page, :] = nbuf[slot, p, gw:2 * gw, :].T
    n_chunk = past // CMP_STRIDE
    cmp_out = []
    for kv, xref in enumerate((xk, xv)):
        xcat = jnp.concatenate([xref[pl.ds(j, n_chunk, stride=CMP_STRIDE), :] for j in range(CMP_STRIDE)], axis=1)
        pe2 = _dot(pecat_ref[kv].astype(BF16), wcat_ref[kv])
        ab = _dot(xcat.astype(BF16), wcat_ref[kv])
        hid = jax.nn.gelu(ab[:, 0:gw] + pe2[0:1, 0:gw] + pltpu.roll(ab[:, gw:2 * gw] + pe2[1:2, gw:2 * gw],
                                                                    n_chunk - 1, 0))
        cmp_out.append(_dot_x3(hid, w2_ref[kv]))
    kc = _head_norm(cmp_out[0], gk_ref[...], bd_ref[...])
    vc = cmp_out[1]

    s = _dot_nt_x3(q8n_ref[0], kc) * SCALE
    cmask = _iota((SUBLANES, n_chunk), 1) < (n_chunk - 1)
    s = jnp.where(cmask, s, NEG_BIG)
    e = jnp.where(cmask, jnp.exp(s - jnp.max(s, axis=1, keepdims=True)), 0.0)
    p_cmp = e / jnp.maximum(jnp.sum(e, axis=1, keepdims=True), TINY)
    o_cmp8 = _dot(p_cmp.astype(BF16), vc.astype(BF16))
    subc = _iota((SUBLANES, n_chunk), 0)
    g0 = jnp.sum(jnp.where(subc < R_NSA, p_cmp, 0.0), axis=0, keepdims=True)
    g1 = jnp.sum(jnp.where(subc >= R_NSA, p_cmp, 0.0), axis=0, keepdims=True)
    p_grp = jnp.concatenate([jnp.where(subc < R_NSA, g0, g1), jnp.zeros((LANES - SUBLANES, n_chunk), F32)], axis=0)
    own = past // SEL_BLOCK
    n_sel = -(-(own + 1) // SUBLANES) * SUBLANES
    p_hi, p_lo = _split_bf16(p_grp)
    imp_t = (_dot_nt(cover_ref[...], p_hi) + _dot_nt(cover_ref[...], p_lo))[0:n_sel]
    blk_t = _iota((n_sel, LANES), 0)
    forced = (blk_t == 0) | (blk_t == own) | (blk_t == own - 1)
    score = jnp.where(blk_t <= own, jnp.where(forced, jnp.inf, imp_t), -jnp.inf)
    picked = _top_k_sublanes(score, SEL_TOPN)
    bias_t = jnp.concatenate([jnp.where((picked > 0.5) & (blk_t <= own), 0.0, NEG_BIG),
                              jnp.full((LANES - n_sel, LANES), NEG_BIG, F32)], axis=0)
    bias8 = bias_t.T[0:SUBLANES, :]
    cmp_rows = []
    for hp in range(H_NSA // 2):
        g = (2 * hp) // R_NSA
        ra = o_cmp8[2 * hp:2 * hp + 1, :]
        rb = o_cmp8[2 * hp + 1:2 * hp + 2, :]
        cmp_rows.append(jnp.where(lane1 < HEAD_DIM, ra if g == 0 else pltpu.roll(ra, HEAD_DIM, 1),
                                  rb if g == 1 else pltpu.roll(rb, HEAD_DIM, 1)))

    def sel_scores(p, carry):
        rows = [jnp.sum(nbuf[slot, p, 2 * gw + (i // R_NSA) * HEAD_DIM:2 * gw + (i // R_NSA + 1) * HEAD_DIM, :]
                        * qrb[i * HEAD_DIM:(i + 1) * HEAD_DIM, :], axis=0, keepdims=True) for i in range(H_NSA)]
        s_sc[p] = jnp.concatenate(rows, axis=0)
        return carry
    for p in range(n_pages):
        sel_scores(p, 0)
    bpp = page // SEL_BLOCK
    sel_s = []
    for p in range(n_pages):
        bias_p = bias8[:, p * bpp:p * bpp + 1]
        for t in range(1, bpp):
            bias_p = jnp.where(lane8 < t * SEL_BLOCK, bias_p, bias8[:, p * bpp + t:p * bpp + t + 1])
        sel_s.append(s_sc[p] + bias_p)
    s_new = _head_sums(qrow[1:2, :] * new[2:3, :]) * SCALE
    sel_rows = softmax_pv(sel_s, s_new, new[3:4, :], nbuf, 3 * gw, True)

    nw = wb_ref.shape[2]
    wk = nw // LANES
    w_s = []
    for i in range(H_NSA):
        g = i // R_NSA
        qcol = jnp.concatenate([qrb[i * HEAD_DIM:(i + 1) * HEAD_DIM, :]] * wk, axis=1)
        w_s.append(jnp.sum(wb_ref[0, g * HEAD_DIM:(g + 1) * HEAD_DIM, :] * qcol, axis=0, keepdims=True))
    s = jnp.concatenate(w_s, axis=0)
    wmask = _iota((SUBLANES, nw), 1) >= 1
    s = jnp.where(wmask, s, NEG_BIG)
    s_new = _head_sums(qrow[1:2, :] * new[4:5, :]) * SCALE
    m8 = jnp.maximum(_lane_rep(jnp.max(s, axis=1, keepdims=True)), s_new)
    pw = jnp.where(wmask, jnp.exp(s - m8[:, 0:1]), 0.0)
    w_new = jnp.exp(s_new - m8)
    inv8 = 1.0 / jnp.maximum(_lane_rep(jnp.sum(pw, axis=1, keepdims=True)) + w_new, TINY)
    win_rows = []
    for hp in range(H_NSA // 2):
        accs = []
        for i in (2 * hp, 2 * hp + 1):
            g = i // R_NSA
            prod = wb_ref[0, gw + g * HEAD_DIM:gw + (g + 1) * HEAD_DIM, :] * pw[i:i + 1, :]
            acc = prod[:, 0:LANES]
            for c in range(1, wk):
                acc = acc + prod[:, c * LANES:(c + 1) * LANES]
            accs.append(acc)
        cols = slice(hp * LANES, (hp + 1) * LANES)
        win_rows.append((_cols_to_row(accs[0], accs[1]) + _pair_row(w_new, 2 * hp) * new[5:6, cols])
                        * _pair_row(inv8, 2 * hp))

    gates = gate_ref[0]
    for c in range(hw // LANES):
        o_ref[0, :, c * LANES:(c + 1) * LANES] = o_rows[c]
    for c in range(W_QN // LANES):
        cols = slice(c * LANES, (c + 1) * LANES)
        o_ref[0, :, hw + c * LANES:hw + (c + 1) * LANES] = (
            gates[0:1, cols] * cmp_rows[c] + gates[1:2, cols] * sel_rows[c] + gates[2:3, cols] * win_rows[c])


def _dec_attention(cm, cn, wb, pt_flat, qrows, q8n, new, gate_rows, wcat, pecat, w2bd, gk, bd, cover,
                    *, n_req, n_pages):
    page = cm.shape[2]
    n_chunk = n_pages * page // CMP_STRIDE
    req = lambda a: pl.BlockSpec((1,) + a.shape[1:], lambda b, pt: (b,) + (0,) * (a.ndim - 1))
    const = lambda a: pl.BlockSpec(a.shape, lambda b, pt: (0,) * a.ndim)
    return pl.pallas_call(
        functools.partial(_dec_kernel, n_pages=n_pages, page=page),
        grid_spec=pltpu.PrefetchScalarGridSpec(
            num_scalar_prefetch=1,
            grid=(n_req,),
            in_specs=[pl.BlockSpec(memory_space=pl.ANY), pl.BlockSpec(memory_space=pl.ANY), req(wb), req(qrows),
                      req(q8n), req(new), req(gate_rows), const(wcat), const(pecat), const(w2bd), const(gk),
                      const(bd), const(cover)],
            out_specs=pl.BlockSpec((1, 1, W_QM + W_QN), lambda b, pt: (b, 0, 0)),
            scratch_shapes=[pltpu.VMEM((2, n_pages) + cm.shape[1:], F32), pltpu.VMEM((2, n_pages) + cn.shape[1:], F32),
                            pltpu.SemaphoreType.DMA((2, 2)),
                            pltpu.VMEM((n_pages * page, LANES), F32), pltpu.VMEM((n_pages * page, LANES), F32),
                            pltpu.VMEM((W_QM, LANES), F32), pltpu.VMEM((W_QN, LANES), F32),
                            pltpu.VMEM((n_pages, SUBLANES, LANES), F32), pltpu.VMEM((n_pages, SUBLANES, LANES), F32)]),
        out_shape=jax.ShapeDtypeStruct((n_req, 1, W_QM + W_QN), F32),
        compiler_params=_params("arbitrary"),
        name="dec_attention",
    )(pt_flat, cm, cn, wb, qrows, q8n, new, gate_rows, wcat, pecat, w2bd, gk, bd, cover)


def _compress_cat_consts(cmp_pos, cmp_w1):
    eye = jnp.eye(G_NSA, dtype=F32)
    w1 = cmp_w1.reshape(2, 2, CMP_STRIDE, HEAD_DIM, HEAD_DIM)
    wcat = jnp.einsum("gh,kajde->kjgdahe", eye, w1).reshape(2, CMP_STRIDE * G_NSA * HEAD_DIM, 2 * G_NSA * HEAD_DIM)
    pe = jnp.tile(cmp_pos.reshape(2, 2, CMP_STRIDE, 1, HEAD_DIM), (1, 1, 1, G_NSA, 1))
    pecat = jnp.pad(pe.reshape(2, 2, CMP_STRIDE * G_NSA * HEAD_DIM), ((0, 0), (0, SUBLANES - 2), (0, 0)))
    return wcat.astype(BF16), pecat


def _post_kernel(o_ref, x_ref, gt_ref, sh_ref, sc_ref, g_ref, wo_ref, wr_ref, br_ref,
                 y_ref, h3_ref, te_ref, tw_ref):
    y = x_ref[...] + gt_ref[0] * _dot(o_ref[...], wo_ref[...])
    y_ref[...] = y
    h = y * lax.rsqrt(jnp.mean(y * y, axis=1, keepdims=True) + NORM_EPS) * g_ref[...]
    h = h * (1.0 + sc_ref[0]) + sh_ref[0]
    tm = h.shape[0]
    for s in range(h.shape[1] // LANES):
        h3_ref[pl.ds(s, tm, stride=SUBLANES), :] = h[:, s * LANES:(s + 1) * LANES]
    h_hi, h_lo = _split_bf16(h)
    w_hi, w_lo = _split_bf16(wr_ref[...])
    logits = _dot(h_hi, w_hi) + _dot(h_lo, w_hi) + _dot(h_hi, w_lo) + br_ref[...]
    _, vals, ids = _top_k_lanes(logits, TOP_K)
    lane = _iota((tm, LANES), 1)
    es = [jnp.exp(v - vals[0]) for v in vals]
    den = es[0]
    for e in es[1:]:
        den = den + e
    te = jnp.zeros((tm, LANES), F32)
    tw = jnp.zeros((tm, LANES), F32)
    for k in range(TOP_K):
        te = jnp.where(lane == k, ids[k], te)
        tw = jnp.where(lane == k, es[k] / den, tw)
    te_ref[...] = te.astype(jnp.int32)
    tw_ref[...] = tw


def _post(o, x, gate, shift, scale, g2, wo_bf, wr_pad, br_pad, *, tm, rows_per_mod):
    n, d = x.shape
    mod_r = gate.shape[1]
    mod_map = lambda i: (i // (rows_per_mod // tm), 0, 0)
    row = lambda w: pl.BlockSpec((tm, w), lambda i: (i, 0))
    const = lambda a: pl.BlockSpec(a.shape, lambda i: (0,) * a.ndim)
    mod = pl.BlockSpec((1, mod_r, d), mod_map)
    return pl.pallas_call(
        _post_kernel,
        grid=(n // tm,),
        in_specs=[row(d), row(d), mod, mod, mod, const(g2), const(wo_bf), const(wr_pad), const(br_pad)],
        out_specs=[row(d), pl.BlockSpec((tm * SUBLANES, LANES), lambda i: (i, 0)), row(LANES), row(LANES)],
        out_shape=[jax.ShapeDtypeStruct((n, d), F32), jax.ShapeDtypeStruct((n * SUBLANES, LANES), F32),
                   jax.ShapeDtypeStruct((n, LANES), jnp.int32), jax.ShapeDtypeStruct((n, LANES), F32)],
        compiler_params=_params("parallel"),
        name="post",
    )(o, x, gate, shift, scale, g2, wo_bf, wr_pad, br_pad)


def _dispatch_kernel(pos_ref, h3_ref, xs_in_ref, xs_ref, sem, *, tm, tile_off):
    del xs_in_ref
    i = pl.program_id(0)

    def body(r, carry):
        src = h3_ref.at[pl.ds(pl.multiple_of(r * SUBLANES, SUBLANES), SUBLANES), :]
        for k in range(TOP_K):
            p = pos_ref[((i + tile_off) * tm + r) * TOP_K + k]
            pltpu.make_async_copy(src, xs_ref.at[pl.ds(pl.multiple_of(p * SUBLANES, SUBLANES), SUBLANES), :],
                                  sem.at[0]).start(priority=k % 2)
        return carry
    lax.fori_loop(0, tm, body, 0, unroll=COPY_UNROLL)
    for k in range(TOP_K):
        pltpu.make_async_copy(h3_ref, xs_ref.at[pl.ds(0, tm * SUBLANES), :], sem.at[0]).wait()


def _dispatch(pos_flat, h3, xs, *, tm, tile_off):
    n8 = h3.shape[0]
    return pl.pallas_call(
        functools.partial(_dispatch_kernel, tm=tm, tile_off=tile_off),
        grid_spec=pltpu.PrefetchScalarGridSpec(
            num_scalar_prefetch=1,
            grid=(n8 // (tm * SUBLANES),),
            in_specs=[pl.BlockSpec((tm * SUBLANES, LANES), lambda i, pos: (i, 0)), pl.BlockSpec(memory_space=pl.ANY)],
            out_specs=pl.BlockSpec(memory_space=pl.ANY),
            scratch_shapes=[pltpu.SemaphoreType.DMA((1,))]),
        out_shape=jax.ShapeDtypeStruct(xs.shape, xs.dtype),
        input_output_aliases={2: 0},
        compiler_params=_params("arbitrary"),
        name="dispatch",
    )(pos_flat, h3, xs)


def _expert_block_kernel(be_ref, na_ref, x_ref, wgu_ref, bgu_ref, wd_ref, bd_ref, y_ref, xb, wgu_bf, wd_bf):
    i = pl.program_id(0)
    active = i < na_ref[0]

    @pl.when(active & ((i == 0) | (be_ref[i] != be_ref[jnp.maximum(i - 1, 0)])))
    def _():
        for r in range(0, wgu_bf.shape[0], LANES):
            wgu_bf[r:r + LANES, :] = wgu_ref[0, r:r + LANES, :].astype(BF16)
        for r in range(0, wd_bf.shape[0], LANES):
            wd_bf[r:r + LANES, :] = wd_ref[0, r:r + LANES, :].astype(BF16)

    @pl.when(active)
    def _():
        d = xb.shape[1]
        for s in range(d // LANES):
            xb[:, s * LANES:(s + 1) * LANES] = x_ref[pl.ds(s, MOE_ROWS, stride=SUBLANES), :].astype(BF16)
        gu = _dot(xb[...], wgu_bf[...]) + bgu_ref[0]
        f = gu.shape[1] // 2
        gt = jnp.minimum(gu[:, 0:f], SWIGLU_LIMIT)
        up = jnp.clip(gu[:, f:2 * f], -SWIGLU_LIMIT, SWIGLU_LIMIT)
        act = (up + 1.0) * (gt * jax.nn.sigmoid(SWIGLU_ALPHA * gt))
        y = _dot(act.astype(BF16), wd_bf[...]) + bd_ref[0]
        for s in range(d // LANES):
            y_ref[pl.ds(s, MOE_ROWS, stride=SUBLANES), :] = y[:, s * LANES:(s + 1) * LANES]

    @pl.when(i >= na_ref[0])
    def _():
        y_ref[...] = jnp.zeros(y_ref.shape, F32)


def _expert_blocks(blk_e, n_active, xs, w_gu, b_gu, w_down, b_down):
    n_blocks = blk_e.shape[0]
    e, d, f2 = w_gu.shape
    rows8 = MOE_ROWS * SUBLANES
    row_map = lambda i, be, na: (jnp.minimum(i, na[0] - 1), 0)
    return pl.pallas_call(
        _expert_block_kernel,
        grid_spec=pltpu.PrefetchScalarGridSpec(
            num_scalar_prefetch=2,
            grid=(n_blocks,),
            in_specs=[pl.BlockSpec((rows8, LANES), row_map),
                      pl.BlockSpec((1, d, f2), lambda i, be, na: (be[i], 0, 0)),
                      pl.BlockSpec((1, 1, f2), lambda i, be, na: (be[i], 0, 0)),
                      pl.BlockSpec((1, f2 // 2, d), lambda i, be, na: (be[i], 0, 0)),
                      pl.BlockSpec((1, 1, d), lambda i, be, na: (be[i], 0, 0))],
            out_specs=pl.BlockSpec((rows8, LANES), lambda i, be, na: (i, 0)),
            scratch_shapes=[pltpu.VMEM((MOE_ROWS, d), BF16), pltpu.VMEM((d, f2), BF16),
                            pltpu.VMEM((f2 // 2, d), BF16)]),
        out_shape=jax.ShapeDtypeStruct((n_blocks * rows8, LANES), F32),
        compiler_params=_params("arbitrary"),
        name="experts",
    )(blk_e, n_active, xs, w_gu, b_gu.reshape(e, 1, f2), w_down, b_down.reshape(e, 1, d))


def _routing_pos(top_e, n_tok):
    n_assign = n_tok * TOP_K
    n_pad = -(-n_assign // LANES) * LANES
    e_flat = jnp.pad(top_e.reshape(-1), (0, n_pad - n_assign), constant_values=N_EXPERTS)
    onehot = (e_flat[:, None] == jnp.arange(N_EXPERTS)[None, :]).astype(F32).reshape(n_pad // LANES, LANES, N_EXPERTS)
    tril = jnp.tril(jnp.ones((LANES, LANES), F32))
    within = jnp.einsum("ij,bjk->bik", tril, onehot)
    block_tot = within[:, -1, :]
    offs = jnp.cumsum(block_tot, axis=0) - block_tot
    counts = jnp.sum(block_tot, axis=0).astype(jnp.int32)
    padded = (counts + MOE_ROWS - 1) // MOE_ROWS * MOE_ROWS
    pad_end = jnp.cumsum(padded)
    pad_start = (pad_end - padded).astype(F32)
    slot = jnp.sum(onehot * (within + offs[:, None, :] - 1.0 + pad_start[None, None, :]), axis=-1)
    pos = slot.reshape(-1)[:n_assign].astype(jnp.int32)
    n_blocks = -(-n_assign // MOE_ROWS) + N_EXPERTS
    starts = jnp.arange(n_blocks, dtype=jnp.int32) * MOE_ROWS
    blk_e = jnp.minimum(jnp.sum((pad_end[None, :] <= starts[:, None]).astype(jnp.int32), axis=1), N_EXPERTS - 1)
    n_active = (pad_end[-1] // MOE_ROWS).astype(jnp.int32).reshape(1)
    return blk_e, n_active, pos


def _combine_kernel(pos_ref, ys_ref, y1_ref, gt_ref, tw_ref, o_ref, buf, sem, *, tile_off):
    i = pl.program_id(0)
    n = pl.num_programs(0)
    tm = y1_ref.shape[0]
    rows8 = tm * SUBLANES

    def gather(tile, slot):
        def body(r, carry):
            for k in range(TOP_K):
                p = pos_ref[((tile + tile_off) * tm + r) * TOP_K + k]
                pltpu.make_async_copy(ys_ref.at[pl.ds(pl.multiple_of(p * SUBLANES, SUBLANES), SUBLANES), :],
                                      buf.at[slot, k, pl.ds(pl.multiple_of(r * SUBLANES, SUBLANES), SUBLANES), :],
                                      sem.at[slot]).start(priority=k % 2)
            return carry
        lax.fori_loop(0, tm, body, 0, unroll=COPY_UNROLL)

    @pl.when(i == 0)
    def _():
        gather(0, 0)

    slot = i % 2
    for k in range(TOP_K):
        pltpu.make_async_copy(ys_ref.at[pl.ds(0, rows8), :], buf.at[slot, k], sem.at[slot]).wait()

    @pl.when(i + 1 < n)
    def _():
        gather(i + 1, 1 - slot)

    tw = tw_ref[...]
    wk = [jnp.broadcast_to(tw[:, k:k + 1], (tm, LANES)) for k in range(TOP_K)]
    gt = gt_ref[0]
    for s in range(o_ref.shape[1] // LANES):
        moe = wk[0] * buf[slot, 0, pl.ds(s, tm, stride=SUBLANES), :]
        for k in range(1, TOP_K):
            moe = moe + wk[k] * buf[slot, k, pl.ds(s, tm, stride=SUBLANES), :]
        cols = slice(s * LANES, (s + 1) * LANES)
        o_ref[:, cols] = y1_ref[:, cols] + gt[:, cols] * moe


def _combine(pos_flat, ys, y1, gate, tw, *, tm, rows_per_mod, tile_off):
    n, d = y1.shape
    mod_r = gate.shape[1]
    return pl.pallas_call(
        functools.partial(_combine_kernel, tile_off=tile_off),
        grid_spec=pltpu.PrefetchScalarGridSpec(
            num_scalar_prefetch=1,
            grid=(n // tm,),
            in_specs=[pl.BlockSpec(memory_space=pl.ANY),
                      pl.BlockSpec((tm, d), lambda i, pos: (i, 0)),
                      pl.BlockSpec((1, mod_r, d), lambda i, pos: (i // (rows_per_mod // tm), 0, 0)),
                      pl.BlockSpec((tm, LANES), lambda i, pos: (i, 0))],
            out_specs=pl.BlockSpec((tm, d), lambda i, pos: (i, 0)),
            scratch_shapes=[pltpu.VMEM((2, TOP_K, tm * SUBLANES, LANES), F32), pltpu.SemaphoreType.DMA((2,))]),
        out_shape=jax.ShapeDtypeStruct((n, d), F32),
        compiler_params=_params("arbitrary"),
        name="combine",
    )(pos_flat, ys, y1, gate, tw)


def _rope_tables(pos):
    half = HEAD_DIM // 2
    inv = ROPE_THETA ** (-jnp.arange(half, dtype=F32) / half)
    ang = pos.astype(F32)[:, None] * inv[None, :]
    cos = jnp.cos(ang)
    sin = jnp.sin(ang)
    reps = LANES // HEAD_DIM
    return (jnp.tile(jnp.concatenate([cos, cos], axis=1), (1, reps)),
            jnp.tile(jnp.concatenate([-sin, sin], axis=1), (1, reps)))


def _head_pad(a):
    return jnp.concatenate([a, jnp.zeros_like(a)], axis=-1)


def kernel(x_prompt, x_sample, c_prompt, c_sample, cache_moba_kv, cache_nsa_kv, state_nsa_win_kv, page_table,
           norm_g, w_ada, b_ada, w_in, qk_gain, cmp_pos, cmp_w1, cmp_w2, w_out, w_router, b_router, w_gu, b_gu,
           w_down, b_down):
    bsz, seq, d = x_prompt.shape
    n_req = x_sample.shape[0]
    depth = norm_g.shape[0]
    assert depth == 1 and x_sample.shape[1] == 1
    assert seq % ATT_TK == 0 and seq >= WINDOW + Q_TILE and seq // SEL_BLOCK <= LANES and seq // MOBA_BLOCK <= MOBA_BLOCK // SUBLANES
    n_pool, page = cache_moba_kv.shape[1], cache_moba_kv.shape[2]
    n_pages = page_table.shape[1]
    past = n_pages * page
    assert past % MOBA_BLOCK == 0 and 2 * page == MOBA_BLOCK and past // SEL_BLOCK < LANES
    assert state_nsa_win_kv.shape[2] == WINDOW
    layer = 0
    gw = G_NSA * HEAD_DIM
    n_prompt = bsz * seq

    bd = jnp.asarray(np.kron(np.eye(LANES // HEAD_DIM), np.full((HEAD_DIM, HEAD_DIM), 1.0 / HEAD_DIM)), BF16)
    w_in_bf = jnp.pad(w_in[layer], ((0, 0), (0, IN_COLS_PAD - IN_COLS))).astype(BF16)
    gains = jnp.tile(qk_gain[layer], (1, W_QM // HEAD_DIM))
    g1 = norm_g[layer, 0].reshape(1, d)
    g2 = norm_g[layer, 1].reshape(1, d)
    wo_bf = w_out[layer].astype(BF16)
    wr_pad = jnp.pad(w_router[layer], ((0, 0), (0, LANES - N_EXPERTS)))
    br_pad = jnp.pad(b_router[layer].reshape(1, N_EXPERTS), ((0, 0), (0, LANES - N_EXPERTS)),
                     constant_values=-jnp.inf)
    cmp_consts = _compress_consts(cmp_pos[layer], cmp_w1[layer], cmp_w2[layer], qk_gain[layer, 3])

    n_c = bsz + n_req
    n_c_pad = -(-n_c // SUBLANES) * SUBLANES
    c_all = jnp.pad(jnp.concatenate([c_prompt, c_sample], axis=0), ((0, n_c_pad - n_c), (0, 0)))
    mods = _ada(c_all, w_ada[layer], b_ada[layer])
    mods_p = [m.reshape(bsz, 1, d) for m in jnp.split(mods[:bsz], 6, axis=1)]
    mods_s = [m.reshape(1, n_req, d) for m in jnp.split(mods[bsz:n_c], 6, axis=1)]

    cos_p, sin_p = _rope_tables(jnp.arange(seq, dtype=jnp.int32))
    tm_p = ROW_TM
    (moba_t, nsa_t, win_t, q_m, qn, qr, _, kmean, cmp_raw, k_aug, v_aug, ks_aug, vs_aug, kw_pad, vw_aug,
     gates_g) = _proj(
        x_prompt.reshape(n_prompt, d), mods_p[0], mods_p[1], g1, w_in_bf, gains, bd, cos_p, sin_p,
        tm=tm_p, rows_per_mod=seq, pos_blocks=seq // tm_p, with_kmean=True)

    nbk = seq // MOBA_BLOCK
    kmean_h = kmean.reshape(bsz, nbk, H_MOBA, HEAD_DIM).transpose(0, 2, 1, 3)
    kmp = jnp.zeros((bsz, H_MOBA, LANES, LANES), F32).at[:, :, HEAD_DIM:HEAD_DIM + nbk, :HEAD_DIM].set(kmean_h)
    per_b = lambda a: a.reshape(bsz, seq, a.shape[-1])
    o_m = _moba_attention(per_b(q_m), kmp, per_b(k_aug), per_b(v_aug))

    kc, vc = _compress_prompt(per_b(cmp_raw), cmp_consts, bd)
    n_chunk = seq // CMP_STRIDE
    per_g = lambda a: a.reshape(bsz, -1, G_NSA, HEAD_DIM).transpose(0, 2, 1, 3)
    kcp = _head_pad(per_g(kc))
    vcd = _head_pad(per_g(vc)).astype(BF16)
    cover_p = _cover(n_chunk - 1, seq // SEL_BLOCK, n_chunk).T.astype(BF16)
    o_n = _nsa_attention(per_b(qn), per_b(qr), per_b(gates_g), kcp, vcd, cover_p, per_b(ks_aug), per_b(vs_aug),
                         per_b(kw_pad), per_b(vw_aug))
    o_p = jnp.concatenate([o_m, o_n], axis=-1).reshape(n_prompt, d)

    y1_p, h3_p, te_p, tw_p = _post(o_p, x_prompt.reshape(n_prompt, d), mods_p[2], mods_p[3], mods_p[4], g2, wo_bf,
                                   wr_pad, br_pad, tm=ROW_TM, rows_per_mod=seq)

    cos_s, sin_s = _rope_tables(jnp.full((n_req,), past, jnp.int32))
    moba_new, nsa_new, win_new, q_m_s, qn_s, qr_s, gates_s = _proj(
        x_sample.reshape(n_req, d), mods_s[0], mods_s[1], g1, w_in_bf, gains, bd, cos_s, sin_s,
        tm=n_req, rows_per_mod=n_req, pos_blocks=1, with_kmean=False)
    pt_flat = page_table.reshape(-1).astype(jnp.int32)
    cache_m = cache_moba_kv[layer].transpose(0, 2, 3, 4, 1).reshape(n_pool, W_KVM, page)
    cache_n = cache_nsa_kv[layer].transpose(0, 2, 3, 4, 1).reshape(n_pool, 4 * gw, page)
    win_buf = state_nsa_win_kv[layer].transpose(0, 2, 3, 4, 1).reshape(n_req, 2 * gw, WINDOW)

    def rows8(q):
        qh = q.reshape(n_req, G_NSA, R_NSA, 1, HEAD_DIM)
        place = jnp.arange(G_NSA)[None, :, None, None, None] == jnp.arange(G_NSA)[None, None, None, :, None]
        return jnp.where(place, qh, 0.0).reshape(n_req, H_NSA, gw)

    def per_head(a):
        return jnp.repeat(a.reshape(n_req, G_NSA, HEAD_DIM), R_NSA, axis=1).reshape(n_req, W_QN)

    new_rows = jnp.stack([moba_new[:, :W_QM], moba_new[:, W_QM:], per_head(nsa_new[:, 2 * gw:3 * gw]),
                          per_head(nsa_new[:, 3 * gw:]), per_head(win_new[:, :gw]), per_head(win_new[:, gw:])], axis=1)
    gate_rows = jnp.repeat(gates_s[:, :N_GATE].reshape(n_req, H_NSA, 3).transpose(0, 2, 1), HEAD_DIM, axis=2)
    n_cmp_s = past // CMP_STRIDE
    cover_s = _cover(n_cmp_s - 1, past // SEL_BLOCK + 1, n_cmp_s).T.astype(BF16)
    wcat, pecat = _compress_cat_consts(cmp_pos[layer], cmp_w1[layer])
    o_s = _dec_attention(cache_m, cache_n, win_buf, pt_flat, jnp.stack([q_m_s, qr_s], axis=1), rows8(qn_s), new_rows,
                          gate_rows, wcat, pecat, cmp_consts[2], cmp_consts[3], bd, cover_s,
                          n_req=n_req, n_pages=n_pages).reshape(n_req, d).astype(BF16)
    y1_s, h3_s, te_s, tw_s = _post(o_s, x_sample.reshape(n_req, d), mods_s[2], mods_s[3], mods_s[4], g2, wo_bf,
                                   wr_pad, br_pad, tm=n_req, rows_per_mod=n_req)

    n_tok = n_prompt + n_req
    top_e = jnp.concatenate([te_p[:, :TOP_K], te_s[:, :TOP_K]], axis=0)
    blk_e, n_active, pos_flat = _routing_pos(top_e, n_tok)
    tm_c = 128
    xs = jnp.zeros((blk_e.shape[0] * MOE_ROWS * SUBLANES, LANES), F32)
    xs = _dispatch(pos_flat, h3_p, xs, tm=DISPATCH_TM, tile_off=0)
    xs = _dispatch(pos_flat, h3_s, xs, tm=n_req, tile_off=n_prompt // n_req)
    ys = _expert_blocks(blk_e, n_active, xs, w_gu[layer], b_gu[layer], w_down[layer], b_down[layer])
    y_p = _combine(pos_flat, ys, y1_p, mods_p[5], tw_p, tm=tm_c, rows_per_mod=seq, tile_off=0)
    y_s = _combine(pos_flat, ys, y1_s, mods_s[5], tw_s, tm=n_req, rows_per_mod=n_req, tile_off=n_prompt // n_req)

    def rows_view(t, n_slot, n_head):
        return t.reshape(1, bsz, n_slot, n_head, HEAD_DIM, t.shape[-1]).transpose(0, 1, 5, 2, 3, 4)

    keep = min(WINDOW, seq)
    win_s = jnp.concatenate([state_nsa_win_kv[layer][:, 1:], win_new.reshape(n_req, 1, 2, G_NSA, HEAD_DIM)], axis=1)
    return (y_p.reshape(bsz, seq, d), y_s.reshape(n_req, 1, d),
            rows_view(moba_t, 2, H_MOBA), rows_view(nsa_t, 4, G_NSA), rows_view(win_t[:, :, seq - keep:], 2, G_NSA),
            moba_new.reshape(1, n_req, 1, 2, H_MOBA, HEAD_DIM),
            nsa_new.reshape(1, n_req, 1, 4, G_NSA, HEAD_DIM), win_s[None])
```
